```python
import math
import jax, jax.numpy as jnp
from jax import lax
import numpy as np

D_MODEL = 1024
BATCH = 4
SEQ = 4096
DEPTH = 1

CHUNK = 64
Q_BLOCK = 128
EPS = 1e-6

DA_HEADS = 4
DA_HEAD_DIM = 64
DA_V_DIM = 2 * DA_HEAD_DIM
DA_WIDTH = DA_HEADS * 2 * DA_HEAD_DIM

HG_HEADS = 4
HG_EXPAND = 128
HG_V_DIM = 128
HG_WIDTH = HG_HEADS * HG_EXPAND

N_BRANCH = 2
IN_SPLITS = [DA_WIDTH] * 3 + [HG_WIDTH] * 4 + [D_MODEL] * N_BRANCH
IN_COLS = sum(IN_SPLITS)

N_GROUPS = 4
EXPERTS_PER_GROUP = 8
N_EXPERTS = N_GROUPS * EXPERTS_PER_GROUP
TOP_K = 2
D_EXPERT = D_MODEL // 2
MOE_BLOCK = 128

kernel_name = "hybrid_diffattn_hgrn2_hmoe_block"


def rmsnorm(x, g):
    xf = x.astype(jnp.float32)
    y = xf * lax.rsqrt(jnp.mean(xf * xf, axis=-1, keepdims=True) + EPS)
    return (y * g.astype(jnp.float32)).astype(x.dtype)


def alibi_slopes(n):
    return jnp.array([2.0 ** (-8.0 * (h + 1) / n) for h in range(n)], dtype=jnp.float32)


def diff_attention(q, k, v, lam, lam_init, subln_g):
    B, S, _ = q.shape
    H, d, dv = DA_HEADS, DA_HEAD_DIM, DA_V_DIM
    nq = S // Q_BLOCK
    qf = q.astype(jnp.float32).reshape(B, S, H, 2, d).transpose(0, 2, 3, 1, 4)
    kf = k.astype(jnp.float32).reshape(B, S, H, 2, d).transpose(0, 2, 3, 1, 4)
    vf = v.astype(jnp.float32).reshape(B, S, H, dv).transpose(0, 2, 1, 3)
    qb = jnp.moveaxis(qf.reshape(B, H, 2, nq, Q_BLOCK, d), 3, 0)
    slopes = alibi_slopes(H)
    key_pos = jnp.arange(S, dtype=jnp.int32)
    scale = 1.0 / math.sqrt(d)

    def one_block(args):
        qblk, bi = args
        q_pos = bi * Q_BLOCK + jnp.arange(Q_BLOCK, dtype=jnp.int32)
        s = jnp.einsum('bhmqd,bhmkd->bhmqk', qblk, kf) * scale
        dist = jnp.abs(q_pos[:, None] - key_pos[None, :]).astype(jnp.float32)
        bias = -slopes[:, None, None] * dist
        allowed = (key_pos // CHUNK)[None, :] <= (q_pos // CHUNK)[:, None]
        s = jnp.where(allowed, s + bias[None, :, None], -jnp.inf)
        p = jax.nn.softmax(s, axis=-1)
        attn = p[:, :, 0] - lam * p[:, :, 1]
        return jnp.einsum('bhqk,bhkv->bhqv', attn, vf)

    o = lax.map(one_block, (qb, jnp.arange(nq, dtype=jnp.int32)))
    o = o.transpose(1, 0, 3, 2, 4).reshape(B, S, H, dv)
    o = rmsnorm(o, subln_g) * (1.0 - lam_init)
    return o.reshape(B, S, H * dv).astype(q.dtype)


def hgrn2(q, f_logit, i, g, lb, norm_g):
    B, S, _ = q.shape
    H, dk, dv, C = HG_HEADS, HG_EXPAND, HG_V_DIM, CHUNK
    N = S // C
    f = lb + (1.0 - lb) * jax.nn.sigmoid(f_logit.astype(jnp.float32))
    logf = jnp.log(f)
    kk = 1.0 - f
    qq = jax.nn.silu(q.astype(jnp.float32))

    def to_chunks(t, width):
        return t.reshape(B, N, C, H, width).transpose(0, 3, 1, 2, 4)

    qc = to_chunks(qq, dk)
    kc = to_chunks(kk, dk)
    vc = to_chunks(i.astype(jnp.float32), dv)
    A = jnp.cumsum(to_chunks(logf, dk), axis=3)
    A_last = A[:, :, :, -1:, :]
    q_dec = qc * jnp.exp(A)
    k_dec = kc * jnp.exp(-A)
    k_end = kc * jnp.exp(A_last - A)

    tri = jnp.tril(jnp.ones((C, C), dtype=bool))
    scores = jnp.einsum('bhnik,bhnjk->bhnij', q_dec, k_dec)
    o_intra = jnp.einsum('bhnij,bhnjv->bhniv', jnp.where(tri, scores, 0.0), vc)

    dS = jnp.einsum('bhnck,bhncv->bhnkv', k_end, vc)
    decay = jnp.exp(A_last[:, :, :, 0, :])

    def step(state, inp):
        dS_n, d_n = inp
        return d_n[..., None] * state + dS_n, state

    S0 = jnp.zeros((B, H, dk, dv), jnp.float32)
    _, S_start = lax.scan(step, S0, (jnp.moveaxis(dS, 2, 0), jnp.moveaxis(decay, 2, 0)))
    S_start = jnp.moveaxis(S_start, 0, 2)
    o = o_intra + jnp.einsum('bhnck,bhnkv->bhncv', q_dec, S_start)
    o = o.transpose(0, 2, 3, 1, 4).reshape(B, S, H, dv)
    gg = g.astype(jnp.float32).reshape(B, S, H, dv)
    o = rmsnorm(o, norm_g) * jax.nn.silu(gg)
    return o.reshape(B, S, H * dv).astype(q.dtype)


def hier_moe(h, w_rg, b_rg, w_re, b_re, w1, w3, w2):
    B, S, D = h.shape
    T = B * S
    hf = h.reshape(T, D)
    g_logits = (hf @ w_rg + b_rg).astype(jnp.float32)
    pg = jax.nn.softmax(g_logits, axis=-1)
    g_idx = jnp.argmax(pg, axis=-1).astype(jnp.int32)
    g_p = jnp.take_along_axis(pg, g_idx[:, None], axis=-1)[:, 0]
    e_logits = (hf @ w_re + b_re).astype(jnp.float32).reshape(T, N_GROUPS, EXPERTS_PER_GROUP)
    sel = jnp.take_along_axis(e_logits, g_idx[:, None, None], axis=1)[:, 0]
    pe = jax.nn.softmax(sel, axis=-1)
    top_w, top_i = lax.top_k(pe, TOP_K)
    top_w = top_w / jnp.sum(top_w, axis=-1, keepdims=True)
    weights = g_p[:, None] * top_w
    experts = g_idx[:, None] * EXPERTS_PER_GROUP + top_i.astype(jnp.int32)

    M = T * TOP_K
    P = M + N_EXPERTS * MOE_BLOCK
    n_blk = P // MOE_BLOCK
    assign_e = experts.reshape(M)
    assign_tok = jnp.broadcast_to(jnp.arange(T, dtype=jnp.int32)[:, None], (T, TOP_K)).reshape(M)
    assign_w = weights.reshape(M)
    order = jnp.argsort(assign_e)
    sorted_e = assign_e[order]
    counts = jnp.bincount(assign_e, length=N_EXPERTS)
    starts = jnp.cumsum(counts) - counts
    pcounts = (counts + MOE_BLOCK - 1) // MOE_BLOCK * MOE_BLOCK
    pends = jnp.cumsum(pcounts)
    pstarts = pends - pcounts
    dest = pstarts[sorted_e] + (jnp.arange(M, dtype=jnp.int32) - starts[sorted_e])
    slot_tok = jnp.zeros((P,), jnp.int32).at[dest].set(assign_tok[order])
    slot_w = jnp.zeros((P,), jnp.float32).at[dest].set(assign_w[order])
    blk_e = jnp.minimum(
        jnp.searchsorted(pends, jnp.arange(n_blk, dtype=jnp.int32) * MOE_BLOCK, side='right'),
        N_EXPERTS - 1).astype(jnp.int32)
    xs = hf[slot_tok].reshape(n_blk, MOE_BLOCK, D)

    def expert_block(args):
        xb, e = args
        return (jax.nn.silu(xb @ w1[e]) * (xb @ w3[e])) @ w2[e]

    ys = lax.map(expert_block, (xs, blk_e)).reshape(P, D)
    out = jnp.zeros((T, D), ys.dtype).at[slot_tok].add(ys * slot_w[:, None].astype(ys.dtype))
    return out.reshape(B, S, D).astype(h.dtype)


def setup_inputs(seed: int = 0) -> dict:
    key = jax.random.key(seed)
    ks = jax.random.split(key, 32)
    f32 = jnp.float32

    def nrm(k, shape, scale):
        return jax.random.normal(k, shape, f32) * scale

    def gain(k, shape):
        return 1.0 + 0.02 * jax.random.normal(k, shape, f32)

    D, L = D_MODEL, DEPTH
    return {
        "x": nrm(ks[0], (BATCH, SEQ, D), 1.0),
        "c": nrm(ks[1], (BATCH, D), 1.0),
        "w_ada": nrm(ks[2], (L, D, 6 * D), 0.5 * D ** -0.5),
        "b_ada": nrm(ks[3], (L, 6 * D), 0.02),
        "norm1_g": gain(ks[4], (L, D)),
        "w_in": nrm(ks[5], (L, D, IN_COLS), D ** -0.5),
        "lambda_q1": nrm(ks[6], (L, DA_HEAD_DIM), 0.1),
        "lambda_k1": nrm(ks[7], (L, DA_HEAD_DIM), 0.1),
        "lambda_q2": nrm(ks[8], (L, DA_HEAD_DIM), 0.1),
        "lambda_k2": nrm(ks[9], (L, DA_HEAD_DIM), 0.1),
        "da_subln_g": gain(ks[10], (L, DA_V_DIM)),
        "hg_lb_logits": nrm(ks[11], (L + 1, HG_WIDTH), 0.1),
        "hg_norm_g": gain(ks[12], (L, HG_V_DIM)),
        "w_up_da": nrm(ks[13], (L, DA_WIDTH, D), DA_WIDTH ** -0.5),
        "w_up_hg": nrm(ks[14], (L, HG_WIDTH, D), HG_WIDTH ** -0.5),
        "w_out": nrm(ks[15], (L, D, D), D ** -0.5),
        "norm2_g": gain(ks[16], (L, D)),
        "w_rg": nrm(ks[17], (L, D, N_GROUPS), D ** -0.5),
        "b_rg": nrm(ks[18], (L, N_GROUPS), 0.01),
        "w_re": nrm(ks[19], (L, D, N_EXPERTS), D ** -0.5),
        "b_re": nrm(ks[20], (L, N_EXPERTS), 0.01),
        "w1": nrm(ks[21], (L, N_EXPERTS, D, D_EXPERT), D ** -0.5),
        "w3": nrm(ks[22], (L, N_EXPERTS, D, D_EXPERT), D ** -0.5),
        "w2": nrm(ks[23], (L, N_EXPERTS, D_EXPERT, D), D_EXPERT ** -0.5),
        "final_g": gain(ks[24], (D,)),
    }


def reference(x, c, w_ada, b_ada, norm1_g, w_in, lambda_q1, lambda_k1, lambda_q2, lambda_k2,
              da_subln_g, hg_lb_logits, hg_norm_g, w_up_da, w_up_hg, w_out, norm2_g,
              w_rg, b_rg, w_re, b_re, w1, w3, w2, final_g):
    split_idx = [int(v) for v in np.cumsum(IN_SPLITS)[:-1]]
    lb_all = jnp.cumsum(jax.nn.softmax(hg_lb_logits.astype(jnp.float32), axis=0), axis=0)
    c_act = jax.nn.silu(c)
    for l in range(DEPTH):
        mod = (c_act @ w_ada[l] + b_ada[l])[:, None, :]
        sh1, sc1, gt1, sh2, sc2, gt2 = jnp.split(mod, 6, axis=-1)

        h = rmsnorm(x, norm1_g[l]) * (1.0 + sc1) + sh1
        proj = h @ w_in[l]
        q_da, k_da, v_da, q_hg, f_hg, i_hg, g_hg, gate_da, gate_hg = jnp.split(proj, split_idx, axis=-1)
        lam_init = 0.8 - 0.6 * math.exp(-0.3 * l)
        lam = (jnp.exp(jnp.sum(lambda_q1[l].astype(jnp.float32) * lambda_k1[l].astype(jnp.float32)))
               - jnp.exp(jnp.sum(lambda_q2[l].astype(jnp.float32) * lambda_k2[l].astype(jnp.float32)))
               + lam_init)
        y_da = diff_attention(q_da, k_da, v_da, lam, lam_init, da_subln_g[l])
        y_hg = hgrn2(q_hg, f_hg, i_hg, g_hg, lb_all[l], hg_norm_g[l])
        merged = (jax.nn.sigmoid(gate_da) * (y_da @ w_up_da[l])
                  + jax.nn.sigmoid(gate_hg) * (y_hg @ w_up_hg[l]))
        x = x + gt1 * (merged @ w_out[l])

        h2 = rmsnorm(x, norm2_g[l]) * (1.0 + sc2) + sh2
        x = x + gt2 * hier_moe(h2, w_rg[l], b_rg[l], w_re[l], b_re[l], w1[l], w3[l], w2[l])
    return rmsnorm(x, final_g)
```

```python
import functools
import math

import jax
import jax.numpy as jnp
from jax import lax
from jax.experimental import pallas as pl
from jax.experimental.pallas import tpu as pltpu

F32 = jnp.float32
BF16 = jnp.bfloat16
I32 = jnp.int32

EPS = 1e-6
NEG = -1e30

CHUNK = 64
DA_HEADS = 4
DA_HEAD_DIM = 64
HG_HEADS = 4
HG_DK = 128
HG_DV = 128
N_GROUPS = 4
EXPERTS_PER_GROUP = 8
N_EXPERTS = N_GROUPS * EXPERTS_PER_GROUP

LANES = 128
ROW_TILE = 512
ATT_BLOCK = 512
EXPERT_BLOCK = 256
GATHER_TILE = 256
ROUTER_ROWS = 40
VMEM_LIMIT = 52 * 1024 * 1024


def _nt(a, b):
    return lax.dot_general(a, b, (((1,), (1,)), ((), ())), preferred_element_type=F32)


def _nn(a, b):
    return jnp.dot(a, b, preferred_element_type=F32)


def _split(a):
    hi = a.astype(BF16)
    lo = (a - hi.astype(F32)).astype(BF16)
    return hi, lo


def _sigmoid(x):
    return 1.0 / (1.0 + jnp.exp(-x))


def _silu(x):
    return x * _sigmoid(x)


def _adaln_kernel(c_ref, w_ref, b_ref, o_ref):
    c_hi, c_lo = _split(_silu(c_ref[...]))
    w_hi, w_lo = _split(w_ref[...])
    o_ref[...] = _nn(c_hi, w_hi) + _nn(c_lo, w_hi) + _nn(c_hi, w_lo) + b_ref[...]


def _adaln(c_pad, w, b):
    rows, d = c_pad.shape
    n = w.shape[1]
    tn = 1024
    return pl.pallas_call(
        _adaln_kernel,
        grid=(n // tn,),
        in_specs=[
            pl.BlockSpec((rows, d), lambda j: (0, 0)),
            pl.BlockSpec((d, tn), lambda j: (0, j)),
            pl.BlockSpec((1, tn), lambda j: (0, j)),
        ],
        out_specs=pl.BlockSpec((rows, tn), lambda j: (0, j)),
        out_shape=jax.ShapeDtypeStruct((rows, n), F32),
        name="adaln",
    )(c_pad, w, b)


def _inproj_kernel(x_ref, sh_ref, sc_ref, g_ref, w_ref, lb_ref,
                   qda_ref, kda_ref, vda_ref, qd_ref, kd_ref, ke_ref, hi_ref, sg_ref,
                   dec_ref, gda_ref, ghg_ref, *, layer):
    x = x_ref[0]
    tm = x.shape[0]
    h = x * lax.rsqrt(jnp.mean(x * x, axis=-1, keepdims=True) + EPS) * g_ref[...]
    h = h * (1.0 + sc_ref[0]) + sh_ref[0]
    hb = h.astype(BF16)

    def proj(c0, width):
        return _nn(hb, w_ref[:, c0:c0 + width])

    w_da = DA_HEADS * 2 * DA_HEAD_DIM
    w_hg = HG_HEADS * HG_DK
    qda_ref[0] = (proj(0, w_da) * (1.0 / math.sqrt(DA_HEAD_DIM))).astype(BF16)
    kda_ref[0] = proj(w_da, w_da).astype(BF16)
    vda_ref[0] = proj(2 * w_da, w_da).astype(BF16)
    c0 = 3 * w_da

    lbl = lb_ref[...]
    lbe = jnp.exp(lbl - jnp.max(lbl, axis=0, keepdims=True))
    lbs = lbe / jnp.sum(lbe, axis=0, keepdims=True)
    lb = jnp.sum(lbs[:layer + 1], axis=0, keepdims=True)

    f = lb + (1.0 - lb) * _sigmoid(proj(c0 + w_hg, w_hg))
    logf = jnp.log(f)
    kk = 1.0 - f
    r = lax.broadcasted_iota(I32, (tm, tm), 0)
    c = lax.broadcasted_iota(I32, (tm, tm), 1)
    tri = jnp.where((r >= c) & ((r // CHUNK) == (c // CHUNK)), 1.0, 0.0).astype(BF16)
    lf_hi, lf_lo = _split(logf)
    n_ch = tm // CHUNK
    a = (_nn(tri, lf_hi) + _nn(tri, lf_lo)).reshape(n_ch, CHUNK, w_hg)
    a_last = a[:, CHUNK - 1:CHUNK, :]
    kk3 = kk.reshape(n_ch, CHUNK, w_hg)
    qq3 = _silu(proj(c0, w_hg)).reshape(n_ch, CHUNK, w_hg)
    qd_ref[0] = (qq3 * jnp.exp(a)).reshape(tm, w_hg).astype(BF16)
    kd_ref[0] = (kk3 * jnp.exp(-a)).reshape(tm, w_hg).astype(BF16)
    ke_ref[0] = (kk3 * jnp.exp(a_last - a)).reshape(tm, w_hg).astype(BF16)
    dec_ref[0] = jnp.exp(a_last).reshape(n_ch, w_hg)
    hi_ref[0] = proj(c0 + 2 * w_hg, w_hg).astype(BF16)
    sg_ref[0] = _silu(proj(c0 + 3 * w_hg, w_hg)).astype(BF16)
    c1 = c0 + 4 * w_hg
    d = x.shape[1]
    gda_ref[0] = _sigmoid(proj(c1, d)).astype(BF16)
    ghg_ref[0] = _sigmoid(proj(c1 + d, d)).astype(BF16)


def _inproj(x, mod, g, w_bf, lb_logits, layer):
    b, s, d = x.shape
    tm = ROW_TILE
    n_cols = w_bf.shape[1]
    w_da = DA_HEADS * 2 * DA_HEAD_DIM
    w_hg = HG_HEADS * HG_DK

    def tok(width, dtype):
        return (pl.BlockSpec((1, tm, width), lambda bi, i: (bi, i, 0)),
                jax.ShapeDtypeStruct((b, s, width), dtype))

    outs = [tok(w_da, BF16)] * 3 + [tok(w_hg, BF16)] * 5
    outs.append((pl.BlockSpec((1, tm // CHUNK, w_hg), lambda bi, i: (bi, i, 0)),
                 jax.ShapeDtypeStruct((b, s // CHUNK, w_hg), F32)))
    outs += [tok(d, BF16)] * 2
    return pl.pallas_call(
        functools.partial(_inproj_kernel, layer=layer),
        grid=(b, s // tm),
        in_specs=[
            pl.BlockSpec((1, tm, d), lambda bi, i: (bi, i, 0)),
            pl.BlockSpec((1, 1, d), lambda bi, i: (bi, 0, 0)),
            pl.BlockSpec((1, 1, d), lambda bi, i: (bi, 0, 1)),
            pl.BlockSpec((1, d), lambda bi, i: (0, 0)),
            pl.BlockSpec((d, n_cols), lambda bi, i: (0, 0), pipeline_mode=pl.Buffered(1)),
            pl.BlockSpec(lb_logits.shape, lambda bi, i: (0, 0)),
        ],
        out_specs=[o[0] for o in outs],
        out_shape=[o[1] for o in outs],
        compiler_params=pltpu.CompilerParams(vmem_limit_bytes=VMEM_LIMIT),
        name="inproj",
    )(x, mod, mod, g, w_bf, lb_logits)


def _attn_kernel(q_ref, k_ref, v_ref, lq1_ref, lk1_ref, lq2_ref, lk2_ref, g_ref, o_ref,
                 vaug_ref, m_ref, acc_ref, *, lam_init, slopes):
    hd = pl.program_id(1)
    i = pl.program_id(2)
    tq = q_ref.shape[1]
    dv = v_ref.shape[2]

    @pl.when(i == 0)
    def _():
        vaug_ref[:, :dv] = v_ref[0]
        vaug_ref[:, dv:] = jnp.ones((vaug_ref.shape[0], vaug_ref.shape[1] - dv), BF16)

    slope = jnp.float32(slopes[-1])
    for idx in range(len(slopes) - 2, -1, -1):
        slope = jnp.where(hd == idx, jnp.float32(slopes[idx]), slope)

    q = q_ref[0]
    lane = lax.broadcasted_iota(I32, q.shape, 1)
    zero = jnp.zeros_like(q)
    q2 = jnp.concatenate([jnp.where(lane < DA_HEAD_DIM, q, zero),
                          jnp.where(lane >= DA_HEAD_DIM, q, zero)], axis=0)

    ii = lax.broadcasted_iota(I32, (2 * tq, tq), 0) & (tq - 1)
    jj = lax.broadcasted_iota(I32, (2 * tq, tq), 1)
    rel = (ii - jj).astype(F32)
    bias_off = -slope * rel
    bias_diag = jnp.where((jj // CHUNK) <= (ii // CHUNK), -slope * jnp.abs(rel), NEG)

    m_ref[...] = jnp.full(m_ref.shape, NEG, F32)
    acc_ref[...] = jnp.zeros(acc_ref.shape, F32)

    def block(j, bias, shift):
        r0 = pl.multiple_of(j * tq, tq)
        s = _nt(q2, k_ref[0, pl.ds(r0, tq), :]) + bias
        m_old = m_ref[...]
        m_new = jnp.maximum(m_old, jnp.max(s, axis=-1, keepdims=True) + shift)
        p = jnp.exp(s - (m_new - shift))
        acc_ref[...] = (jnp.exp(m_old - m_new) * acc_ref[...]
                        + _nn(p.astype(BF16), vaug_ref[pl.ds(r0, tq), :]))
        m_ref[...] = m_new

    def off_diag(j, carry):
        block(j, bias_off, -slope * ((i - j) * tq).astype(F32))
        return carry

    lax.fori_loop(0, i, off_diag, 0)
    block(i, bias_diag, jnp.float32(0.0))

    acc = acc_ref[...]
    o1 = acc[:tq, :dv] / acc[:tq, dv:]
    o2 = acc[tq:, :dv] / acc[tq:, dv:]
    lam = (jnp.exp(jnp.sum(lq1_ref[...] * lk1_ref[...], axis=-1, keepdims=True))
           - jnp.exp(jnp.sum(lq2_ref[...] * lk2_ref[...], axis=-1, keepdims=True)) + lam_init)
    o = o1 - lam * o2
    o = o * lax.rsqrt(jnp.mean(o * o, axis=-1, keepdims=True) + EPS) * g_ref[...]
    o_ref[0] = (o * (1.0 - lam_init)).astype(BF16)


def _attention(q, k, v, lq1, lk1, lq2, lk2, subln_g, lam_init):
    b, s, _ = q.shape
    tq = ATT_BLOCK
    hw = 2 * DA_HEAD_DIM
    slopes = tuple(2.0 ** (-8.0 * (h + 1) / DA_HEADS) for h in range(DA_HEADS))
    vec = lambda n: pl.BlockSpec((1, n), lambda bi, h, i: (0, 0))
    return pl.pallas_call(
        functools.partial(_attn_kernel, lam_init=lam_init, slopes=slopes),
        grid=(b, DA_HEADS, s // tq),
        in_specs=[
            pl.BlockSpec((1, tq, hw), lambda bi, h, i: (bi, i, h)),
            pl.BlockSpec((1, s, hw), lambda bi, h, i: (bi, 0, h)),
            pl.BlockSpec((1, s, hw), lambda bi, h, i: (bi, 0, h)),
            vec(DA_HEAD_DIM), vec(DA_HEAD_DIM), vec(DA_HEAD_DIM), vec(DA_HEAD_DIM), vec(hw),
        ],
        out_specs=pl.BlockSpec((1, tq, hw), lambda bi, h, i: (bi, i, h)),
        out_shape=jax.ShapeDtypeStruct((b, s, DA_HEADS * hw), BF16),
        scratch_shapes=[
            pltpu.VMEM((s, 2 * hw), BF16),
            pltpu.VMEM((2 * tq, 1), F32),
            pltpu.VMEM((2 * tq, 2 * hw), F32),
        ],
        compiler_params=pltpu.CompilerParams(vmem_limit_bytes=VMEM_LIMIT),
        name="attention",
    )(q, k, v, lq1, lk1, lq2, lk2, subln_g)


def _hgrn_kernel(qd_ref, kd_ref, ke_ref, v_ref, sg_ref, dec_ref, g_ref, o_ref, st_ref):
    @pl.when(pl.program_id(1) == 0)
    def _():
        st_ref[...] = jnp.zeros(st_ref.shape, F32)

    n_ch = qd_ref.shape[1] // CHUNK
    r = lax.broadcasted_iota(I32, (CHUNK, CHUNK), 0)
    c = lax.broadcasted_iota(I32, (CHUNK, CHUNK), 1)
    causal = r >= c

    def chunk(ci, carry):
        r0 = pl.multiple_of(ci * CHUNK, CHUNK)
        rows = pl.ds(r0, CHUNK)
        dec = dec_ref[0, pl.ds(ci, 1), :]
        for h in range(HG_HEADS):
            cols = slice(h * HG_DK, (h + 1) * HG_DK)
            qd = qd_ref[0, rows, cols]
            kd = kd_ref[0, rows, cols]
            ke = ke_ref[0, rows, cols]
            v = v_ref[0, rows, cols]
            st = st_ref[h]
            scores = jnp.where(causal, _nt(qd, kd), 0.0).astype(BF16)
            o = _nn(scores, v) + _nt(qd, st.astype(BF16))
            vt = v.astype(F32).T.astype(BF16)
            st_ref[h] = st * dec[:, cols] + _nn(vt, ke)
            o = o * lax.rsqrt(jnp.mean(o * o, axis=-1, keepdims=True) + EPS) * g_ref[...]
            o_ref[0, rows, cols] = (o * sg_ref[0, rows, cols].astype(F32)).astype(BF16)
        return carry

    lax.fori_loop(0, n_ch, chunk, 0)


def _hgrn(qd, kd, ke, v, sg, dec, norm_g):
    b, s, w = qd.shape
    ts = ROW_TILE
    tok = pl.BlockSpec((1, ts, w), lambda bi, i: (bi, i, 0))
    return pl.pallas_call(
        _hgrn_kernel,
        grid=(b, s // ts),
        in_specs=[tok, tok, tok, tok, tok,
                  pl.BlockSpec((1, ts // CHUNK, w), lambda bi, i: (bi, i, 0)),
                  pl.BlockSpec((1, HG_DV), lambda bi, i: (0, 0))],
        out_specs=tok,
        out_shape=jax.ShapeDtypeStruct((b, s, w), BF16),
        scratch_shapes=[pltpu.VMEM((HG_HEADS, HG_DV, HG_DK), F32)],
        name="hgrn",
    )(qd, kd, ke, v, sg, dec, norm_g)


def _merge_kernel(yda_ref, yhg_ref, gda_ref, ghg_ref, x_ref, gt1_ref, sh2_ref, sc2_ref, g2_ref,
                  wda_ref, whg_ref, wout_ref, wrh_ref, wrl_ref, br_ref,
                  x1_ref, h2_ref, eidx_ref, wts_ref):
    u = (gda_ref[0].astype(F32) * _nn(yda_ref[0], wda_ref[...])
         + ghg_ref[0].astype(F32) * _nn(yhg_ref[0], whg_ref[...]))
    x1 = x_ref[0] + gt1_ref[0] * _nn(u.astype(BF16), wout_ref[...])
    x1_ref[0] = x1
    h2 = x1 * lax.rsqrt(jnp.mean(x1 * x1, axis=-1, keepdims=True) + EPS) * g2_ref[...]
    h2 = h2 * (1.0 + sc2_ref[0]) + sh2_ref[0]
    h2_ref[0] = h2

    h_hi, h_lo = _split(h2)
    wrh = wrh_ref[...]
    lg = _nt(wrh, h_hi) + _nt(wrl_ref[...], h_hi) + _nt(wrh, h_lo) + br_ref[...]

    tm = h2.shape[0]
    eg = EXPERTS_PER_GROUP
    gl = lg[N_EXPERTS:N_EXPERTS + N_GROUPS]
    gmax = jnp.max(gl, axis=0, keepdims=True)
    g_p = 1.0 / jnp.sum(jnp.exp(gl - gmax), axis=0, keepdims=True)
    gi = lax.broadcasted_iota(I32, (N_GROUPS, tm), 0)
    g_idx = jnp.min(jnp.where(gl == gmax, gi, N_GROUPS), axis=0, keepdims=True)
    sel = lg[(N_GROUPS - 1) * eg:N_GROUPS * eg]
    for g in range(N_GROUPS - 2, -1, -1):
        sel = jnp.where(g_idx == g, lg[g * eg:(g + 1) * eg], sel)
    ei = lax.broadcasted_iota(I32, (eg, tm), 0)
    m1 = jnp.max(sel, axis=0, keepdims=True)
    i1 = jnp.min(jnp.where(sel == m1, ei, eg), axis=0, keepdims=True)
    sel2 = jnp.where(ei == i1, -jnp.inf, sel)
    m2 = jnp.max(sel2, axis=0, keepdims=True)
    i2 = jnp.min(jnp.where(sel2 == m2, ei, eg), axis=0, keepdims=True)
    t = jnp.exp(m2 - m1)
    wa = 1.0 / (1.0 + t)
    eidx_ref[0] = jnp.concatenate([g_idx * eg + i1, g_idx * eg + i2], axis=0)
    wts_ref[0] = jnp.concatenate([g_p * wa, g_p * (t * wa)], axis=0)


def _merge(yda, yhg, gda, ghg, x, mod, g2, wda, whg, wout, wrh, wrl, br):
    b, s, d = x.shape
    tm = ROW_TILE
    w = yda.shape[2]
    tok = lambda width: pl.BlockSpec((1, tm, width), lambda bi, i: (bi, i, 0))
    modspec = lambda k: pl.BlockSpec((1, 1, d), lambda bi, i: (bi, 0, k))
    full = lambda a: pl.BlockSpec(a.shape, lambda bi, i: (0,) * a.ndim)
    lane_rows = pl.BlockSpec((1, 2, tm), lambda bi, i: (bi, 0, i))
    return pl.pallas_call(
        _merge_kernel,
        grid=(b, s // tm),
        in_specs=[tok(w), tok(w), tok(d), tok(d), tok(d),
                  modspec(2), modspec(3), modspec(4), full(g2),
                  full(wda), full(whg), full(wout), full(wrh), full(wrl), full(br)],
        out_specs=[tok(d), tok(d), lane_rows, lane_rows],
        out_shape=[jax.ShapeDtypeStruct((b, s, d), F32), jax.ShapeDtypeStruct((b, s, d), F32),
                   jax.ShapeDtypeStruct((b, 2, s), I32), jax.ShapeDtypeStruct((b, 2, s), F32)],
        compiler_params=pltpu.CompilerParams(vmem_limit_bytes=VMEM_LIMIT),
        name="merge",
    )(yda, yhg, gda, ghg, x, mod, mod, mod, g2, wda, whg, wout, wrh, wrl, br)


def _positions_kernel(e_ref, dest_ref, blk_ref, nused_ref):
    e_mat = e_ref[...]
    nr = e_mat.shape[0]
    nb = blk_ref.shape[1]
    r = lax.broadcasted_iota(I32, (LANES, LANES), 0)
    c = lax.broadcasted_iota(I32, (LANES, LANES), 1)
    incl = jnp.where(r <= c, 1.0, 0.0).astype(BF16)
    ones = jnp.ones((LANES, LANES), BF16)
    rr = lax.broadcasted_iota(I32, (nr, nr), 0)
    cc = lax.broadcasted_iota(I32, (nr, nr), 1)
    before = jnp.where(cc < rr, 1.0, 0.0).astype(BF16)
    blk_start = (lax.broadcasted_iota(I32, (1, nb), 1) * EXPERT_BLOCK).astype(F32)

    def per_expert(e, carry):
        pstart, dest, blk = carry
        hit = e_mat == e
        oh = jnp.where(hit, 1.0, 0.0).astype(BF16)
        in_row = _nn(oh, incl)
        row_tot = _nn(oh, ones)
        row_off = _nn(before, row_tot.astype(BF16))
        total = row_off[nr - 1:nr, :] + row_tot[nr - 1:nr, :]
        dest = jnp.where(hit, pstart + row_off + in_row - 1.0, dest)
        pend = pstart + jnp.floor((total + (EXPERT_BLOCK - 1)) * (1.0 / EXPERT_BLOCK)) * EXPERT_BLOCK
        blk = blk + jnp.where(blk_start >= pend[:, :1], 1.0, 0.0)
        return pend, dest, blk

    pstart0 = jnp.zeros((1, LANES), F32)
    pend, dest, blk = lax.fori_loop(
        0, N_EXPERTS, per_expert,
        (pstart0, jnp.zeros(e_mat.shape, F32), jnp.zeros((1, nb), F32)))
    dest_ref[...] = dest.astype(I32)
    blk_ref[...] = jnp.minimum(blk, N_EXPERTS - 1).astype(I32)
    nused_ref[...] = (pend * (1.0 / EXPERT_BLOCK)).astype(I32)


def _positions(e_mat, n_blk_pad):
    nr = e_mat.shape[0]
    return pl.pallas_call(
        _positions_kernel,
        out_shape=[jax.ShapeDtypeStruct((nr, LANES), I32),
                   jax.ShapeDtypeStruct((1, n_blk_pad), I32),
                   jax.ShapeDtypeStruct((1, LANES), I32)],
        name="positions",
    )(e_mat)


def _dispatch_kernel(dest_ref, h2_ref, xs_in_ref, xs_ref, sem, *, n_tok):
    del xs_in_ref
    base = pl.program_id(0) * GATHER_TILE

    def copies(t, d0, d1):
        return (pltpu.make_async_copy(h2_ref.at[pl.ds(t, 1)], xs_ref.at[pl.ds(d0, 1)], sem.at[0]),
                pltpu.make_async_copy(h2_ref.at[pl.ds(t, 1)], xs_ref.at[pl.ds(d1, 1)], sem.at[1]))

    def issue(r, carry):
        t = base + r
        for cp in copies(t, dest_ref[t], dest_ref[n_tok + t]):
            cp.start()
        return carry

    def drain(r, carry):
        t = base + r
        for cp in copies(t, dest_ref[t], dest_ref[n_tok + t]):
            cp.wait()
        return carry

    lax.fori_loop(0, GATHER_TILE, issue, 0)
    lax.fori_loop(0, GATHER_TILE, drain, 0)


def _dispatch(dest, h2, xs_zero):
    n_tok = h2.shape[0]
    return pl.pallas_call(
        functools.partial(_dispatch_kernel, n_tok=n_tok),
        grid_spec=pltpu.PrefetchScalarGridSpec(
            num_scalar_prefetch=1,
            grid=(n_tok // GATHER_TILE,),
            in_specs=[pl.BlockSpec(memory_space=pl.ANY), pl.BlockSpec(memory_space=pl.ANY)],
            out_specs=pl.BlockSpec(memory_space=pl.ANY),
            scratch_shapes=[pltpu.SemaphoreType.DMA((2,))],
        ),
        out_shape=jax.ShapeDtypeStruct(xs_zero.shape, xs_zero.dtype),
        input_output_aliases={2: 0},
        name="dispatch",
    )(dest, h2, xs_zero)


def _expert_kernel(blk_ref, nused_ref, xs_ref, w1_ref, w3_ref, w2_ref, ys_ref, w1b, w3b, w2b):
    i = pl.program_id(0)
    e = blk_ref[i]
    fresh = (i == 0) | (blk_ref[jnp.maximum(i - 1, 0)] != e)

    @pl.when(i < nused_ref[0])
    def _():
        @pl.when(fresh)
        def _():
            w1b[...] = w1_ref[0].astype(BF16)
            w3b[...] = w3_ref[0].astype(BF16)
            w2b[...] = w2_ref[0].astype(BF16)

        x = xs_ref[...].astype(BF16)
        a = _nn(x, w1b[...])
        ys_ref[...] = _nn((_silu(a) * _nn(x, w3b[...])).astype(BF16), w2b[...])

    @pl.when(i >= nused_ref[0])
    def _():
        ys_ref[...] = jnp.zeros(ys_ref.shape, F32)


def _experts(blk_e, n_used, xs, w1, w3, w2):
    p, d = xs.shape
    de = w1.shape[2]
    rows = lambda i, blk, nu: (jnp.minimum(i, nu[0] - 1), 0)
    wsel = lambda i, blk, nu: (blk[i], 0, 0)
    return pl.pallas_call(
        _expert_kernel,
        grid_spec=pltpu.PrefetchScalarGridSpec(
            num_scalar_prefetch=2,
            grid=(p // EXPERT_BLOCK,),
            in_specs=[pl.BlockSpec((EXPERT_BLOCK, d), rows),
                      pl.BlockSpec((1, d, de), wsel),
                      pl.BlockSpec((1, d, de), wsel),
                      pl.BlockSpec((1, de, d), wsel)],
            out_specs=pl.BlockSpec((EXPERT_BLOCK, d), lambda i, blk, nu: (i, 0)),
            scratch_shapes=[pltpu.VMEM((d, de), BF16), pltpu.VMEM((d, de), BF16),
                            pltpu.VMEM((de, d), BF16)],
        ),
        out_shape=jax.ShapeDtypeStruct((p, d), F32),
        compiler_params=pltpu.CompilerParams(vmem_limit_bytes=VMEM_LIMIT),
        name="experts",
    )(blk_e, n_used, xs, w1, w3, w2)


def _combine_kernel(dest_ref, ys_ref, x1_ref, gt2_ref, wts_ref, g_ref, o_ref, y0, y1, sem, *, n_tok, last):
    base = pl.program_id(0) * GATHER_TILE

    def copies(r, d0, d1):
        return (pltpu.make_async_copy(ys_ref.at[pl.ds(d0, 1)], y0.at[pl.ds(r, 1)], sem.at[0]),
                pltpu.make_async_copy(ys_ref.at[pl.ds(d1, 1)], y1.at[pl.ds(r, 1)], sem.at[1]))

    def issue(r, carry):
        for cp in copies(r, dest_ref[base + r], dest_ref[n_tok + base + r]):
            cp.start()
        return carry

    def drain(r, carry):
        for cp in copies(r, dest_ref[base + r], dest_ref[n_tok + base + r]):
            cp.wait()
        return carry

    lax.fori_loop(0, GATHER_TILE, issue, 0)
    lax.fori_loop(0, GATHER_TILE, drain, 0)

    wts = wts_ref[...]
    x = x1_ref[...] + gt2_ref[0] * (wts[:, 0:1] * y0[...] + wts[:, 1:2] * y1[...])
    if last:
        x = x * lax.rsqrt(jnp.mean(x * x, axis=-1, keepdims=True) + EPS) * g_ref[...]
    o_ref[...] = x


def _combine(dest, ys, x1, mod, wts, final_g, seq, last=True):
    n_tok, d = x1.shape
    tt = GATHER_TILE
    per_batch = seq // tt
    return pl.pallas_call(
        functools.partial(_combine_kernel, n_tok=n_tok, last=last),
        grid_spec=pltpu.PrefetchScalarGridSpec(
            num_scalar_prefetch=1,
            grid=(n_tok // tt,),
            in_specs=[pl.BlockSpec(memory_space=pl.ANY),
                      pl.BlockSpec((tt, d), lambda i, dest: (i, 0)),
                      pl.BlockSpec((1, 1, d), lambda i, dest: (i // per_batch, 0, 5)),
                      pl.BlockSpec((tt, 2), lambda i, dest: (i, 0)),
                      pl.BlockSpec((1, d), lambda i, dest: (0, 0))],
            out_specs=pl.BlockSpec((tt, d), lambda i, dest: (i, 0)),
            scratch_shapes=[pltpu.VMEM((tt, d), F32), pltpu.VMEM((tt, d), F32),
                            pltpu.SemaphoreType.DMA((2,))],
        ),
        out_shape=jax.ShapeDtypeStruct((n_tok, d), F32),
        name="combine",
    )(dest, ys, x1, mod, wts, final_g)


def kernel(x, c, w_ada, b_ada, norm1_g, w_in, lambda_q1, lambda_k1, lambda_q2, lambda_k2, da_subln_g, hg_lb_logits, hg_norm_g, w_up_da, w_up_hg, w_out, norm2_g, w_rg, b_rg, w_re, b_re, w1, w3, w2, final_g):
    b, s, d = x.shape
    n_tok = b * s
    depth = w_ada.shape[0]
    assert s % ROW_TILE == 0 and s % ATT_BLOCK == 0 and s % GATHER_TILE == 0
    assert (2 * n_tok) % LANES == 0

    c_pad = jnp.pad(c, ((0, (-b) % 8), (0, 0)))
    n_slots = 2 * n_tok + N_EXPERTS * EXPERT_BLOCK
    n_blk = n_slots // EXPERT_BLOCK
    n_blk_pad = -(-n_blk // LANES) * LANES

    for l in range(depth):
        lam_init = 0.8 - 0.6 * math.exp(-0.3 * l)
        mod = _adaln(c_pad, w_ada[l], b_ada[l][None, :])[:b].reshape(b, 1, 6 * d)

        (qda, kda, vda, qd, kd, ke, hi, sg, dec, gda, ghg) = _inproj(
            x, mod, norm1_g[l][None, :], w_in[l].astype(BF16), hg_lb_logits, l)
        yda = _attention(qda, kda, vda, lambda_q1[l][None, :], lambda_k1[l][None, :],
                         lambda_q2[l][None, :], lambda_k2[l][None, :], da_subln_g[l][None, :], lam_init)
        yhg = _hgrn(qd, kd, ke, hi, sg, dec, hg_norm_g[l][None, :])

        wr = jnp.concatenate([w_re[l].T, w_rg[l].T,
                              jnp.zeros((ROUTER_ROWS - N_EXPERTS - N_GROUPS, d), F32)], axis=0)
        wr_hi = wr.astype(BF16)
        wr_lo = (wr - wr_hi.astype(F32)).astype(BF16)
        br = jnp.concatenate([b_re[l], b_rg[l],
                              jnp.zeros((ROUTER_ROWS - N_EXPERTS - N_GROUPS,), F32)])[:, None]
        x1, h2, eidx, wts = _merge(yda, yhg, gda, ghg, x, mod, norm2_g[l][None, :],
                                   w_up_da[l].astype(BF16), w_up_hg[l].astype(BF16),
                                   w_out[l].astype(BF16), wr_hi, wr_lo, br)

        e_mat = jnp.transpose(eidx, (1, 0, 2)).reshape(2 * n_tok // LANES, LANES)
        dest, blk_e, n_used = _positions(e_mat, n_blk_pad)
        dest = dest.reshape(2 * n_tok)
        xs = _dispatch(dest, h2.reshape(n_tok, d), jnp.zeros((n_slots, d), F32))
        ys = _experts(blk_e[0, :n_blk], n_used[0, :1], xs, w1[l], w3[l], w2[l])
        wts_tok = jnp.transpose(wts, (0, 2, 1)).reshape(n_tok, 2)
        x = _combine(dest, ys, x1.reshape(n_tok, d), mod, wts_tok, final_g[None, :], s,
                     last=(l == depth - 1)).reshape(b, s, d)
    return x
```

```python
import functools
import math

import jax
import jax.numpy as jnp
from jax import lax
from jax.experimental import pallas as pl
from jax.experimental.pallas import tpu as pltpu

F32 = jnp.float32
BF16 = jnp.bfloat16
I32 = jnp.int32

EPS = 1e-6
NEG = -1e30

CHUNK = 64
DA_HEADS = 4
DA_HEAD_DIM = 64
HG_HEADS = 4
HG_DK = 128
HG_DV = 128
N_GROUPS = 4
EXPERTS_PER_GROUP = 8
N_EXPERTS = N_GROUPS * EXPERTS_PER_GROUP

LANES = 128
ROW_TILE = 512
ATT_BLOCK = 512
EXPERT_BLOCK = 256
GATHER_TILE = 256
ROUTER_ROWS = 40
VMEM_LIMIT = 52 * 1024 * 1024


def _nt(a, b):
    return lax.dot_general(a, b, (((1,), (1,)), ((), ())), preferred_element_type=F32)


def _nn(a, b):
    return jnp.dot(a, b, preferred_element_type=F32)


def _split(a):
    hi = a.astype(BF16)
    lo = (a - hi.astype(F32)).astype(BF16)
    return hi, lo


def _sigmoid(x):
    return 1.0 / (1.0 + jnp.exp(-x))


def _silu(x):
    return x * _sigmoid(x)


def _adaln_kernel(c_ref, w_ref, b_ref, o_ref):
    c_hi, c_lo = _split(_silu(c_ref[...]))
    w_hi, w_lo = _split(w_ref[...])
    o_ref[...] = _nn(c_hi, w_hi) + _nn(c_lo, w_hi) + _nn(c_hi, w_lo) + b_ref[...]


def _adaln(c_pad, w, b):
    rows, d = c_pad.shape
    n = w.shape[1]
    tn = 1024
    return pl.pallas_call(
        _adaln_kernel,
        grid=(n // tn,),
        in_specs=[
            pl.BlockSpec((rows, d), lambda j: (0, 0)),
            pl.BlockSpec((d, tn), lambda j: (0, j)),
            pl.BlockSpec((1, tn), lambda j: (0, j)),
        ],
        out_specs=pl.BlockSpec((rows, tn), lambda j: (0, j)),
        out_shape=jax.ShapeDtypeStruct((rows, n), F32),
        name="adaln",
    )(c_pad, w, b)


def _inproj_kernel(x_ref, sh_ref, sc_ref, g_ref, w_ref, lb_ref,
                   qda_ref, kda_ref, vda_ref, qd_ref, kd_ref, ke_ref, hi_ref, sg_ref,
                   dec_ref, gda_ref, ghg_ref, *, layer):
    x = x_ref[0]
    tm = x.shape[0]
    h = x * lax.rsqrt(jnp.mean(x * x, axis=-1, keepdims=True) + EPS) * g_ref[...]
    h = h * (1.0 + sc_ref[0]) + sh_ref[0]
    hb = h.astype(BF16)

    def proj(c0, width):
        return _nn(hb, w_ref[:, c0:c0 + width])

    w_da = DA_HEADS * 2 * DA_HEAD_DIM
    w_hg = HG_HEADS * HG_DK
    qda_ref[0] = (proj(0, w_da) * (1.0 / math.sqrt(DA_HEAD_DIM))).astype(BF16)
    kda_ref[0] = proj(w_da, w_da).astype(BF16)
    vda_ref[0] = proj(2 * w_da, w_da).astype(BF16)
    c0 = 3 * w_da

    lbl = lb_ref[...]
    lbe = jnp.exp(lbl - jnp.max(lbl, axis=0, keepdims=True))
    lbs = lbe / jnp.sum(lbe, axis=0, keepdims=True)
    lb = jnp.sum(lbs[:layer + 1], axis=0, keepdims=True)

    f = lb + (1.0 - lb) * _sigmoid(proj(c0 + w_hg, w_hg))
    logf = jnp.log(f)
    kk = 1.0 - f
    r = lax.broadcasted_iota(I32, (tm, tm), 0)
    c = lax.broadcasted_iota(I32, (tm, tm), 1)
    tri = jnp.where((r >= c) & ((r // CHUNK) == (c // CHUNK)), 1.0, 0.0).astype(BF16)
    lf_hi, lf_lo = _split(logf)
    n_ch = tm // CHUNK
    a = (_nn(tri, lf_hi) + _nn(tri, lf_lo)).reshape(n_ch, CHUNK, w_hg)
    a_last = a[:, CHUNK - 1:CHUNK, :]
    kk3 = kk.reshape(n_ch, CHUNK, w_hg)
    qq3 = _silu(proj(c0, w_hg)).reshape(n_ch, CHUNK, w_hg)
    qd_ref[0] = (qq3 * jnp.exp(a)).reshape(tm, w_hg).astype(BF16)
    kd_ref[0] = (kk3 * jnp.exp(-a)).reshape(tm, w_hg).astype(BF16)
    ke_ref[0] = (kk3 * jnp.exp(a_last - a)).reshape(tm, w_hg).astype(BF16)
    dec_ref[0] = jnp.exp(a_last).reshape(n_ch, w_hg)
    hi_ref[0] = proj(c0 + 2 * w_hg, w_hg).astype(BF16)
    sg_ref[0] = _silu(proj(c0 + 3 * w_hg, w_hg)).astype(BF16)
    c1 = c0 + 4 * w_hg
    d = x.shape[1]
    gda_ref[0] = _sigmoid(proj(c1, d)).astype(BF16)
    ghg_ref[0] = _sigmoid(proj(c1 + d, d)).astype(BF16)


def _inproj(x, mod, g, w_bf, lb_logits, layer):
    b, s, d = x.shape
    tm = ROW_TILE
    n_cols = w_bf.shape[1]
    w_da = DA_HEADS * 2 * DA_HEAD_DIM
    w_hg = HG_HEADS * HG_DK

    def tok(width, dtype):
        return (pl.BlockSpec((1, tm, width), lambda bi, i: (bi, i, 0)),
                jax.ShapeDtypeStruct((b, s, width), dtype))

    outs = [tok(w_da, BF16)] * 3 + [tok(w_hg, BF16)] * 5
    outs.append((pl.BlockSpec((1, tm // CHUNK, w_hg), lambda bi, i: (bi, i, 0)),
                 jax.ShapeDtypeStruct((b, s // CHUNK, w_hg), F32)))
    outs += [tok(d, BF16)] * 2
    return pl.pallas_call(
        functools.partial(_inproj_kernel, layer=layer),
        grid=(b, s // tm),
        in_specs=[
            pl.BlockSpec((1, tm, d), lambda bi, i: (bi, i, 0)),
            pl.BlockSpec((1, 1, d), lambda bi, i: (bi, 0, 0)),
            pl.BlockSpec((1, 1, d), lambda bi, i: (bi, 0, 1)),
            pl.BlockSpec((1, d), lambda bi, i: (0, 0)),
            pl.BlockSpec((d, n_cols), lambda bi, i: (0, 0), pipeline_mode=pl.Buffered(1)),
            pl.BlockSpec(lb_logits.shape, lambda bi, i: (0, 0)),
        ],
        out_specs=[o[0] for o in outs],
        out_shape=[o[1] for o in outs],
        compiler_params=pltpu.CompilerParams(vmem_limit_bytes=VMEM_LIMIT),
        name="inproj",
    )(x, mod, mod, g, w_bf, lb_logits)


def _attn_kernel(q_ref, k_ref, v_ref, lq1_ref, lk1_ref, lq2_ref, lk2_ref, g_ref, o_ref,
                 vaug_ref, m_ref, acc_ref, *, lam_init, slopes):
    hd = pl.program_id(1)
    i = pl.program_id(2)
    tq = q_ref.shape[1]
    dv = v_ref.shape[2]

    @pl.when(i == 0)
    def _():
        vaug_ref[:, :dv] = v_ref[0]
        vaug_ref[:, dv:] = jnp.ones((vaug_ref.shape[0], vaug_ref.shape[1] - dv), BF16)

    slope = jnp.float32(slopes[-1])
    for idx in range(len(slopes) - 2, -1, -1):
        slope = jnp.where(hd == idx, jnp.float32(slopes[idx]), slope)

    q = q_ref[0]
    lane = lax.broadcasted_iota(I32, q.shape, 1)
    zero = jnp.zeros_like(q)
    q2 = jnp.concatenate([jnp.where(lane < DA_HEAD_DIM, q, zero),
                          jnp.where(lane >= DA_HEAD_DIM, q, zero)], axis=0)

    ii = lax.broadcasted_iota(I32, (2 * tq, tq), 0) & (tq - 1)
    jj = lax.broadcasted_iota(I32, (2 * tq, tq), 1)
    rel = (ii - jj).astype(F32)
    bias_off = -slope * rel
    bias_diag = jnp.where((jj // CHUNK) <= (ii // CHUNK), -slope * jnp.abs(rel), NEG)

    m_ref[...] = jnp.full(m_ref.shape, NEG, F32)
    acc_ref[...] = jnp.zeros(acc_ref.shape, F32)

    def block(j, bias, shift):
        r0 = pl.multiple_of(j * tq, tq)
        s = _nt(q2, k_ref[0, pl.ds(r0, tq), :]) + bias
        m_old = m_ref[...]
        m_new = jnp.maximum(m_old, jnp.max(s, axis=-1, keepdims=True) + shift)
        p = jnp.exp(s - (m_new - shift))
        acc_ref[...] = (jnp.exp(m_old - m_new) * acc_ref[...]
                        + _nn(p.astype(BF16), vaug_ref[pl.ds(r0, tq), :]))
        m_ref[...] = m_new

    def off_diag(j, carry):
        block(j, bias_off, -slope * ((i - j) * tq).astype(F32))
        return carry

    lax.fori_loop(0, i, off_diag, 0)
    block(i, bias_diag, jnp.float32(0.0))

    acc = acc_ref[...]
    o1 = acc[:tq, :dv] / acc[:tq, dv:]
    o2 = acc[tq:, :dv] / acc[tq:, dv:]
    lam = (jnp.exp(jnp.sum(lq1_ref[...] * lk1_ref[...], axis=-1, keepdims=True))
           - jnp.exp(jnp.sum(lq2_ref[...] * lk2_ref[...], axis=-1, keepdims=True)) + lam_init)
    o = o1 - lam * o2
    o = o * lax.rsqrt(jnp.mean(o * o, axis=-1, keepdims=True) + EPS) * g_ref[...]
    o_ref[0] = (o * (1.0 - lam_init)).astype(BF16)


def _attention(q, k, v, lq1, lk1, lq2, lk2, subln_g, lam_init):
    b, s, _ = q.shape
    tq = ATT_BLOCK
    hw = 2 * DA_HEAD_DIM
    slopes = tuple(2.0 ** (-8.0 * (h + 1) / DA_HEADS) for h in range(DA_HEADS))
    vec = lambda n: pl.BlockSpec((1, n), lambda bi, h, i: (0, 0))
    return pl.pallas_call(
        functools.partial(_attn_kernel, lam_init=lam_init, slopes=slopes),
        grid=(b, DA_HEADS, s // tq),
        in_specs=[
            pl.BlockSpec((1, tq, hw), lambda bi, h, i: (bi, i, h)),
            pl.BlockSpec((1, s, hw), lambda bi, h, i: (bi, 0, h)),
            pl.BlockSpec((1, s, hw), lambda bi, h, i: (bi, 0, h)),
            vec(DA_HEAD_DIM), vec(DA_HEAD_DIM), vec(DA_HEAD_DIM), vec(DA_HEAD_DIM), vec(hw),
        ],
        out_specs=pl.BlockSpec((1, tq, hw), lambda bi, h, i: (bi, i, h)),
        out_shape=jax.ShapeDtypeStruct((b, s, DA_HEADS * hw), BF16),
        scratch_shapes=[
            pltpu.VMEM((s, 2 * hw), BF16),
            pltpu.VMEM((2 * tq, 1), F32),
            pltpu.VMEM((2 * tq, 2 * hw), F32),
        ],
        compiler_params=pltpu.CompilerParams(vmem_limit_bytes=VMEM_LIMIT),
        name="attention",
    )(q, k, v, lq1, lk1, lq2, lk2, subln_g)


def _hgrn_kernel(qd_ref, kd_ref, ke_ref, v_ref, sg_ref, dec_ref, g_ref, o_ref, st_ref):
    @pl.when(pl.program_id(1) == 0)
    def _():
        st_ref[...] = jnp.zeros(st_ref.shape, F32)

    n_ch = qd_ref.shape[1] // CHUNK
    r = lax.broadcasted_iota(I32, (CHUNK, CHUNK), 0)
    c = lax.broadcasted_iota(I32, (CHUNK, CHUNK), 1)
    causal = r >= c

    def chunk(ci, carry):
        r0 = pl.multiple_of(ci * CHUNK, CHUNK)
        rows = pl.ds(r0, CHUNK)
        dec = dec_ref[0, pl.ds(ci, 1), :]
        for h in range(HG_HEADS):
            cols = slice(h * HG_DK, (h + 1) * HG_DK)
            qd = qd_ref[0, rows, cols]
            kd = kd_ref[0, rows, cols]
            ke = ke_ref[0, rows, cols]
            v = v_ref[0, rows, cols]
            st = st_ref[h]
            scores = jnp.where(causal, _nt(qd, kd), 0.0).astype(BF16)
            o = _nn(scores, v) + _nt(qd, st.astype(BF16))
            vt = v.astype(F32).T.astype(BF16)
            st_ref[h] = st * dec[:, cols] + _nn(vt, ke)
            o = o * lax.rsqrt(jnp.mean(o * o, axis=-1, keepdims=True) + EPS) * g_ref[...]
            o_ref[0, rows, cols] = (o * sg_ref[0, rows, cols].astype(F32)).astype(BF16)
        return carry

    lax.fori_loop(0, n_ch, chunk, 0)


def _hgrn(qd, kd, ke, v, sg, dec, norm_g):
    b, s, w = qd.shape
    ts = ROW_TILE
    tok = pl.BlockSpec((1, ts, w), lambda bi, i: (bi, i, 0))
    return pl.pallas_call(
        _hgrn_kernel,
        grid=(b, s // ts),
        in_specs=[tok, tok, tok, tok, tok,
                  pl.BlockSpec((1, ts // CHUNK, w), lambda bi, i: (bi, i, 0)),
                  pl.BlockSpec((1, HG_DV), lambda bi, i: (0, 0))],
        out_specs=tok,
        out_shape=jax.ShapeDtypeStruct((b, s, w), BF16),
        scratch_shapes=[pltpu.VMEM((HG_HEADS, HG_DV, HG_DK), F32)],
        name="hgrn",
    )(qd, kd, ke, v, sg, dec, norm_g)


def _merge_kernel(yda_ref, yhg_ref, gda_ref, ghg_ref, x_ref, gt1_ref, sh2_ref, sc2_ref, g2_ref,
                  wda_ref, whg_ref, wout_ref, wrh_ref, wrl_ref, br_ref,
                  x1_ref, h2_ref, eidx_ref, wts_ref):
    u = (gda_ref[0].astype(F32) * _nn(yda_ref[0], wda_ref[...])
         + ghg_ref[0].astype(F32) * _nn(yhg_ref[0], whg_ref[...]))
    x1 = x_ref[0] + gt1_ref[0] * _nn(u.astype(BF16), wout_ref[...])
    x1_ref[0] = x1
    h2 = x1 * lax.rsqrt(jnp.mean(x1 * x1, axis=-1, keepdims=True) + EPS) * g2_ref[...]
    h2 = h2 * (1.0 + sc2_ref[0]) + sh2_ref[0]
    h2_ref[0] = h2

    h_hi, h_lo = _split(h2)
    wrh = wrh_ref[...]
    lg = _nt(wrh, h_hi) + _nt(wrl_ref[...], h_hi) + _nt(wrh, h_lo) + br_ref[...]

    tm = h2.shape[0]
    eg = EXPERTS_PER_GROUP
    gl = lg[N_EXPERTS:N_EXPERTS + N_GROUPS]
    gmax = jnp.max(gl, axis=0, keepdims=True)
    g_p = 1.0 / jnp.sum(jnp.exp(gl - gmax), axis=0, keepdims=True)
    gi = lax.broadcasted_iota(I32, (N_GROUPS, tm), 0)
    g_idx = jnp.min(jnp.where(gl == gmax, gi, N_GROUPS), axis=0, keepdims=True)
    sel = lg[(N_GROUPS - 1) * eg:N_GROUPS * eg]
    for g in range(N_GROUPS - 2, -1, -1):
        sel = jnp.where(g_idx == g, lg[g * eg:(g + 1) * eg], sel)
    ei = lax.broadcasted_iota(I32, (eg, tm), 0)
    m1 = jnp.max(sel, axis=0, keepdims=True)
    i1 = jnp.min(jnp.where(sel == m1, ei, eg), axis=0, keepdims=True)
    sel2 = jnp.where(ei == i1, -jnp.inf, sel)
    m2 = jnp.max(sel2, axis=0, keepdims=True)
    i2 = jnp.min(jnp.where(sel2 == m2, ei, eg), axis=0, keepdims=True)
    t = jnp.exp(m2 - m1)
    wa = 1.0 / (1.0 + t)
    eidx_ref[0] = jnp.concatenate([g_idx * eg + i1, g_idx * eg + i2], axis=0)
    wts_ref[0] = jnp.concatenate([g_p * wa, g_p * (t * wa)], axis=0)


def _merge(yda, yhg, gda, ghg, x, mod, g2, wda, whg, wout, wrh, wrl, br):
    b, s, d = x.shape
    tm = ROW_TILE
    w = yda.shape[2]
    tok = lambda width: pl.BlockSpec((1, tm, width), lambda bi, i: (bi, i, 0))
    modspec = lambda k: pl.BlockSpec((1, 1, d), lambda bi, i: (bi, 0, k))
    full = lambda a: pl.BlockSpec(a.shape, lambda bi, i: (0,) * a.ndim)
    lane_rows = pl.BlockSpec((1, 2, tm), lambda bi, i: (bi, 0, i))
    return pl.pallas_call(
        _merge_kernel,
        grid=(b, s // tm),
        in_specs=[tok(w), tok(w), tok(d), tok(d), tok(d),
                  modspec(2), modspec(3), modspec(4), full(g2),
                  full(wda), full(whg), full(wout), full(wrh), full(wrl), full(br)],
        out_specs=[tok(d), tok(d), lane_rows, lane_rows],
        out_shape=[jax.ShapeDtypeStruct((b, s, d), F32), jax.ShapeDtypeStruct((b, s, d), F32),
                   jax.ShapeDtypeStruct((b, 2, s), I32), jax.ShapeDtypeStruct((b, 2, s), F32)],
        compiler_params=pltpu.CompilerParams(vmem_limit_bytes=VMEM_LIMIT),
        name="merge",
    )(yda, yhg, gda, ghg, x, mod, mod, mod, g2, wda, whg, wout, wrh, wrl, br)


def _positions_kernel(e_ref, dest_ref, blk_ref, nused_ref):
    e_mat = e_ref[...]
    nr = e_mat.shape[0]
    nb = blk_ref.shape[1]
    r = lax.broadcasted_iota(I32, (LANES, LANES), 0)
    c = lax.broadcasted_iota(I32, (LANES, LANES), 1)
    incl = jnp.where(r <= c, 1.0, 0.0).astype(BF16)
    ones = jnp.ones((LANES, LANES), BF16)
    rr = lax.broadcasted_iota(I32, (nr, nr), 0)
    cc = lax.broadcasted_iota(I32, (nr, nr), 1)
    before = jnp.where(cc < rr, 1.0, 0.0).astype(BF16)
    blk_start = (lax.broadcasted_iota(I32, (1, nb), 1) * EXPERT_BLOCK).astype(F32)

    def per_expert(e, carry):
        pstart, dest, blk = carry
        hit = e_mat == e
        oh = jnp.where(hit, 1.0, 0.0).astype(BF16)
        in_row = _nn(oh, incl)
        row_tot = _nn(oh, ones)
        row_off = _nn(before, row_tot.astype(BF16))
        total = row_off[nr - 1:nr, :] + row_tot[nr - 1:nr, :]
        dest = jnp.where(hit, pstart + row_off + in_row - 1.0, dest)
        pend = pstart + jnp.floor((total + (EXPERT_BLOCK - 1)) * (1.0 / EXPERT_BLOCK)) * EXPERT_BLOCK
        blk = blk + jnp.where(blk_start >= pend[:, :1], 1.0, 0.0)
        return pend, dest, blk

    pstart0 = jnp.zeros((1, LANES), F32)
    pend, dest, blk = lax.fori_loop(
        0, N_EXPERTS, per_expert,
        (pstart0, jnp.zeros(e_mat.shape, F32), jnp.zeros((1, nb), F32)))
    dest_ref[...] = dest.astype(I32)
    blk_ref[...] = jnp.minimum(blk, N_EXPERTS - 1).astype(I32)
    nused_ref[...] = (pend * (1.0 / EXPERT_BLOCK)).astype(I32)


def _positions(e_mat, n_blk_pad):
    nr = e_mat.shape[0]
    return pl.pallas_call(
        _positions_kernel,
        out_shape=[jax.ShapeDtypeStruct((nr, LANES), I32),
                   jax.ShapeDtypeStruct((1, n_blk_pad), I32),
                   jax.ShapeDtypeStruct((1, LANES), I32)],
        name="positions",
    )(e_mat)


def _dispatch_kernel(dest_ref, h2_ref, xs_in_ref, xs_ref, sem, *, n_tok):
    del xs_in_ref
    base = pl.program_id(0) * GATHER_TILE

    def copies(r, d0, d1):
        return (pltpu.make_async_copy(h2_ref.at[pl.ds(r, 1)], xs_ref.at[pl.ds(d0, 1)], sem.at[0]),
                pltpu.make_async_copy(h2_ref.at[pl.ds(r, 1)], xs_ref.at[pl.ds(d1, 1)], sem.at[1]))

    def issue(r, carry):
        for cp in copies(r, dest_ref[base + r], dest_ref[n_tok + base + r]):
            cp.start()
        return carry

    def drain(r, carry):
        for cp in copies(r, dest_ref[base + r], dest_ref[n_tok + base + r]):
            cp.wait()
        return carry

    lax.fori_loop(0, GATHER_TILE, issue, 0)
    lax.fori_loop(0, GATHER_TILE, drain, 0)


def _dispatch(dest, h2, xs_zero):
    n_tok = h2.shape[0]
    return pl.pallas_call(
        functools.partial(_dispatch_kernel, n_tok=n_tok),
        grid_spec=pltpu.PrefetchScalarGridSpec(
            num_scalar_prefetch=1,
            grid=(n_tok // GATHER_TILE,),
            in_specs=[pl.BlockSpec((GATHER_TILE, h2.shape[1]), lambda i, dest: (i, 0)),
                      pl.BlockSpec(memory_space=pl.ANY)],
            out_specs=pl.BlockSpec(memory_space=pl.ANY),
            scratch_shapes=[pltpu.SemaphoreType.DMA((2,))],
        ),
        out_shape=jax.ShapeDtypeStruct(xs_zero.shape, xs_zero.dtype),
        input_output_aliases={2: 0},
        name="dispatch",
    )(dest, h2, xs_zero)


def _expert_kernel(blk_ref, nused_ref, xs_ref, w1_ref, w3_ref, w2_ref, ys_ref, w1b, w3b, w2b):
    i = pl.program_id(0)
    e = blk_ref[i]
    fresh = (i == 0) | (blk_ref[jnp.maximum(i - 1, 0)] != e)

    @pl.when(i < nused_ref[0])
    def _():
        @pl.when(fresh)
        def _():
            w1b[...] = w1_ref[0].astype(BF16)
            w3b[...] = w3_ref[0].astype(BF16)
            w2b[...] = w2_ref[0].astype(BF16)

        x = xs_ref[...].astype(BF16)
        a = _nn(x, w1b[...])
        ys_ref[...] = _nn((_silu(a) * _nn(x, w3b[...])).astype(BF16), w2b[...])

    @pl.when(i >= nused_ref[0])
    def _():
        ys_ref[...] = jnp.zeros(ys_ref.shape, F32)


def _experts(blk_e, n_used, xs, w1, w3, w2):
    p, d = xs.shape
    de = w1.shape[2]
    rows = lambda i, blk, nu: (jnp.minimum(i, nu[0] - 1), 0)
    wsel = lambda i, blk, nu: (blk[i], 0, 0)
    return pl.pallas_call(
        _expert_kernel,
        grid_spec=pltpu.PrefetchScalarGridSpec(
            num_scalar_prefetch=2,
            grid=(p // EXPERT_BLOCK,),
            in_specs=[pl.BlockSpec((EXPERT_BLOCK, d), rows),
                      pl.BlockSpec((1, d, de), wsel),
                      pl.BlockSpec((1, d, de), wsel),
                      pl.BlockSpec((1, de, d), wsel)],
            out_specs=pl.BlockSpec((EXPERT_BLOCK, d), lambda i, blk, nu: (i, 0)),
            scratch_shapes=[pltpu.VMEM((d, de), BF16), pltpu.VMEM((d, de), BF16),
                            pltpu.VMEM((de, d), BF16)],
        ),
        out_shape=jax.ShapeDtypeStruct((p, d), F32),
        compiler_params=pltpu.CompilerParams(vmem_limit_bytes=VMEM_LIMIT),
        name="experts",
    )(blk_e, n_used, xs, w1, w3, w2)


def _combine_kernel(dest_ref, ys_ref, x1_ref, gt2_ref, wts_ref, g_ref, o_ref, y0, y1, sem, *, n_tok, last):
    base = pl.program_id(0) * GATHER_TILE

    def copies(r, d0, d1):
        return (pltpu.make_async_copy(ys_ref.at[pl.ds(d0, 1)], y0.at[pl.ds(r, 1)], sem.at[0]),
                pltpu.make_async_copy(ys_ref.at[pl.ds(d1, 1)], y1.at[pl.ds(r, 1)], sem.at[1]))

    def issue(r, carry):
        for cp in copies(r, dest_ref[base + r], dest_ref[n_tok + base + r]):
            cp.start()
        return carry

    def drain(r, carry):
        for cp in copies(r, dest_ref[base + r], dest_ref[n_tok + base + r]):
            cp.wait()
        return carry

    lax.fori_loop(0, GATHER_TILE, issue, 0)
    lax.fori_loop(0, GATHER_TILE, drain, 0)

    wts = wts_ref[...]
    x = x1_ref[...] + gt2_ref[0] * (wts[:, 0:1] * y0[...] + wts[:, 1:2] * y1[...])
    if last:
        x = x * lax.rsqrt(jnp.mean(x * x, axis=-1, keepdims=True) + EPS) * g_ref[...]
    o_ref[...] = x


def _combine(dest, ys, x1, mod, wts, final_g, seq, last=True):
    n_tok, d = x1.shape
    tt = GATHER_TILE
    per_batch = seq // tt
    return pl.pallas_call(
        functools.partial(_combine_kernel, n_tok=n_tok, last=last),
        grid_spec=pltpu.PrefetchScalarGridSpec(
            num_scalar_prefetch=1,
            grid=(n_tok // tt,),
            in_specs=[pl.BlockSpec(memory_space=pl.ANY),
                      pl.BlockSpec((tt, d), lambda i, dest: (i, 0)),
                      pl.BlockSpec((1, 1, d), lambda i, dest: (i // per_batch, 0, 5)),
                      pl.BlockSpec((tt, 2), lambda i, dest: (i, 0)),
                      pl.BlockSpec((1, d), lambda i, dest: (0, 0))],
            out_specs=pl.BlockSpec((tt, d), lambda i, dest: (i, 0)),
            scratch_shapes=[pltpu.VMEM((tt, d), F32), pltpu.VMEM((tt, d), F32),
                            pltpu.SemaphoreType.DMA((2,))],
        ),
        out_shape=jax.ShapeDtypeStruct((n_tok, d), F32),
        name="combine",
    )(dest, ys, x1, mod, wts, final_g)


def kernel(x, c, w_ada, b_ada, norm1_g, w_in, lambda_q1, lambda_k1, lambda_q2, lambda_k2, da_subln_g, hg_lb_logits, hg_norm_g, w_up_da, w_up_hg, w_out, norm2_g, w_rg, b_rg, w_re, b_re, w1, w3, w2, final_g):
    b, s, d = x.shape
    n_tok = b * s
    depth = w_ada.shape[0]
    assert s % ROW_TILE == 0 and s % ATT_BLOCK == 0 and s % GATHER_TILE == 0
    assert (2 * n_tok) % LANES == 0

    c_pad = jnp.pad(c, ((0, (-b) % 8), (0, 0)))
    n_slots = 2 * n_tok + N_EXPERTS * EXPERT_BLOCK
    n_blk = n_slots // EXPERT_BLOCK
    n_blk_pad = -(-n_blk // LANES) * LANES

    for l in range(depth):
        lam_init = 0.8 - 0.6 * math.exp(-0.3 * l)
        mod = _adaln(c_pad, w_ada[l], b_ada[l][None, :])[:b].reshape(b, 1, 6 * d)

        (qda, kda, vda, qd, kd, ke, hi, sg, dec, gda, ghg) = _inproj(
            x, mod, norm1_g[l][None, :], w_in[l].astype(BF16), hg_lb_logits, l)
        yda = _attention(qda, kda, vda, lambda_q1[l][None, :], lambda_k1[l][None, :],
                         lambda_q2[l][None, :], lambda_k2[l][None, :], da_subln_g[l][None, :], lam_init)
        yhg = _hgrn(qd, kd, ke, hi, sg, dec, hg_norm_g[l][None, :])

        wr = jnp.concatenate([w_re[l].T, w_rg[l].T,
                              jnp.zeros((ROUTER_ROWS - N_EXPERTS - N_GROUPS, d), F32)], axis=0)
        wr_hi = wr.astype(BF16)
        wr_lo = (wr - wr_hi.astype(F32)).astype(BF16)
        br = jnp.concatenate([b_re[l], b_rg[l],
                              jnp.zeros((ROUTER_ROWS - N_EXPERTS - N_GROUPS,), F32)])[:, None]
        x1, h2, eidx, wts = _merge(yda, yhg, gda, ghg, x, mod, norm2_g[l][None, :],
                                   w_up_da[l].astype(BF16), w_up_hg[l].astype(BF16),
                                   w_out[l].astype(BF16), wr_hi, wr_lo, br)

        e_mat = jnp.transpose(eidx, (1, 0, 2)).reshape(2 * n_tok // LANES, LANES)
        dest, blk_e, n_used = _positions(e_mat, n_blk_pad)
        dest = dest.reshape(2 * n_tok)
        xs = _dispatch(dest, h2.reshape(n_tok, d), jnp.zeros((n_slots, d), F32))
        ys = _experts(blk_e[0, :n_blk], n_used[0, :1], xs, w1[l], w3[l], w2[l])
        wts_tok = jnp.transpose(wts, (0, 2, 1)).reshape(n_tok, 2)
        x = _combine(dest, ys, x1.reshape(n_tok, d), mod, wts_tok, final_g[None, :], s,
                     last=(l == depth - 1)).reshape(b, s, d)
    return x
```

```python
import functools
import math

import jax
import jax.numpy as jnp
from jax import lax
from jax.experimental import pallas as pl
from jax.experimental.pallas import tpu as pltpu

F32 = jnp.float32
BF16 = jnp.bfloat16
I32 = jnp.int32

EPS = 1e-6
NEG = -1e30
LOG2E = math.log2(math.e)

CHUNK = 64
DA_HEADS = 4
DA_HEAD_DIM = 64
HG_HEADS = 4
HG_DK = 128
HG_DV = 128
N_GROUPS = 4
EXPERTS_PER_GROUP = 8
N_EXPERTS = N_GROUPS * EXPERTS_PER_GROUP

LANES = 128
ROW_TILE = 512
ATT_BLOCK = 512
EXPERT_BLOCK = 256
GATHER_TILE = 256
ROUTER_ROWS = 40
VMEM_LIMIT = 52 * 1024 * 1024


def _nt(a, b):
    return lax.dot_general(a, b, (((1,), (1,)), ((), ())), preferred_element_type=F32)


def _nn(a, b):
    return jnp.dot(a, b, preferred_element_type=F32)


def _split(a):
    hi = a.astype(BF16)
    lo = (a - hi.astype(F32)).astype(BF16)
    return hi, lo


def _sigmoid(x):
    return 1.0 / (1.0 + jnp.exp(-x))


def _silu(x):
    return x * _sigmoid(x)


def _adaln_kernel(c_ref, w_ref, b_ref, o_ref):
    c_hi, c_lo = _split(_silu(c_ref[...]))
    w_hi, w_lo = _split(w_ref[...])
    o_ref[...] = _nn(c_hi, w_hi) + _nn(c_lo, w_hi) + _nn(c_hi, w_lo) + b_ref[...]


def _adaln(c_pad, w, b):
    rows, d = c_pad.shape
    n = w.shape[1]
    tn = 1024
    return pl.pallas_call(
        _adaln_kernel,
        grid=(n // tn,),
        in_specs=[
            pl.BlockSpec((rows, d), lambda j: (0, 0)),
            pl.BlockSpec((d, tn), lambda j: (0, j)),
            pl.BlockSpec((1, tn), lambda j: (0, j)),
        ],
        out_specs=pl.BlockSpec((rows, tn), lambda j: (0, j)),
        out_shape=jax.ShapeDtypeStruct((rows, n), F32),
        name="adaln",
    )(c_pad, w, b)


def _inproj_kernel(x_ref, sh_ref, sc_ref, g_ref, w_ref, lb_ref,
                   qda_ref, kda_ref, vda_ref, qd_ref, kd_ref, ke_ref, hi_ref, sg_ref,
                   dec_ref, gda_ref, ghg_ref, *, layer):
    x = x_ref[0]
    tm = x.shape[0]
    h = x * lax.rsqrt(jnp.mean(x * x, axis=-1, keepdims=True) + EPS) * g_ref[...]
    h = h * (1.0 + sc_ref[0]) + sh_ref[0]
    hb = h.astype(BF16)

    def proj(c0, width):
        return _nn(hb, w_ref[:, c0:c0 + width])

    w_da = DA_HEADS * 2 * DA_HEAD_DIM
    w_hg = HG_HEADS * HG_DK
    qda_ref[0] = (proj(0, w_da) * (LOG2E / math.sqrt(DA_HEAD_DIM))).astype(BF16)
    kda_ref[0] = proj(w_da, w_da).astype(BF16)
    vda_ref[0] = proj(2 * w_da, w_da).astype(BF16)
    c0 = 3 * w_da

    lbl = lb_ref[...]
    lbe = jnp.exp(lbl - jnp.max(lbl, axis=0, keepdims=True))
    lbs = lbe / jnp.sum(lbe, axis=0, keepdims=True)
    lb = jnp.sum(lbs[:layer + 1], axis=0, keepdims=True)

    f = lb + (1.0 - lb) * _sigmoid(proj(c0 + w_hg, w_hg))
    logf = jnp.log(f)
    kk = 1.0 - f
    r = lax.broadcasted_iota(I32, (tm, tm), 0)
    c = lax.broadcasted_iota(I32, (tm, tm), 1)
    tri = jnp.where((r >= c) & ((r // CHUNK) == (c // CHUNK)), 1.0, 0.0).astype(BF16)
    lf_hi, lf_lo = _split(logf)
    n_ch = tm // CHUNK
    a = (_nn(tri, lf_hi) + _nn(tri, lf_lo)).reshape(n_ch, CHUNK, w_hg)
    a_last = a[:, CHUNK - 1:CHUNK, :]
    kk3 = kk.reshape(n_ch, CHUNK, w_hg)
    qq3 = _silu(proj(c0, w_hg)).reshape(n_ch, CHUNK, w_hg)
    qd_ref[0] = (qq3 * jnp.exp(a)).reshape(tm, w_hg).astype(BF16)
    kd_ref[0] = (kk3 * jnp.exp(-a)).reshape(tm, w_hg).astype(BF16)
    ke_ref[0] = (kk3 * jnp.exp(a_last - a)).reshape(tm, w_hg).astype(BF16)
    dec_ref[0] = jnp.exp(a_last).reshape(n_ch, w_hg)
    hi_ref[0] = proj(c0 + 2 * w_hg, w_hg).astype(BF16)
    sg_ref[0] = _silu(proj(c0 + 3 * w_hg, w_hg)).astype(BF16)
    c1 = c0 + 4 * w_hg
    d = x.shape[1]
    gda_ref[0] = _sigmoid(proj(c1, d)).astype(BF16)
    ghg_ref[0] = _sigmoid(proj(c1 + d, d)).astype(BF16)


def _inproj(x, mod, g, w_bf, lb_logits, layer):
    b, s, d = x.shape
    tm = ROW_TILE
    n_cols = w_bf.shape[1]
    w_da = DA_HEADS * 2 * DA_HEAD_DIM
    w_hg = HG_HEADS * HG_DK

    def tok(width, dtype):
        return (pl.BlockSpec((1, tm, width), lambda bi, i: (bi, i, 0)),
                jax.ShapeDtypeStruct((b, s, width), dtype))

    outs = [tok(w_da, BF16)] * 3 + [tok(w_hg, BF16)] * 5
    outs.append((pl.BlockSpec((1, tm // CHUNK, w_hg), lambda bi, i: (bi, i, 0)),
                 jax.ShapeDtypeStruct((b, s // CHUNK, w_hg), F32)))
    outs += [tok(d, BF16)] * 2
    return pl.pallas_call(
        functools.partial(_inproj_kernel, layer=layer),
        grid=(b, s // tm),
        in_specs=[
            pl.BlockSpec((1, tm, d), lambda bi, i: (bi, i, 0)),
            pl.BlockSpec((1, 1, d), lambda bi, i: (bi, 0, 0)),
            pl.BlockSpec((1, 1, d), lambda bi, i: (bi, 0, 1)),
            pl.BlockSpec((1, d), lambda bi, i: (0, 0)),
            pl.BlockSpec((d, n_cols), lambda bi, i: (0, 0), pipeline_mode=pl.Buffered(1)),
            pl.BlockSpec(lb_logits.shape, lambda bi, i: (0, 0)),
        ],
        out_specs=[o[0] for o in outs],
        out_shape=[o[1] for o in outs],
        compiler_params=pltpu.CompilerParams(vmem_limit_bytes=VMEM_LIMIT),
        name="inproj",
    )(x, mod, mod, g, w_bf, lb_logits)


def _attn_kernel(q_ref, k_ref, v_ref, lq1_ref, lk1_ref, lq2_ref, lk2_ref, g_ref, o_ref,
                 kaug_ref, vaug_ref, qaug_ref, m_ref, acc_ref, sa_ref, sb_ref, *, lam_init, slopes):
    hd = pl.program_id(1)
    i = pl.program_id(2)
    tq = q_ref.shape[1]
    hw = q_ref.shape[2]
    n_key = k_ref.shape[1]

    slope = jnp.float32(slopes[-1])
    for idx in range(len(slopes) - 2, -1, -1):
        slope = jnp.where(hd == idx, jnp.float32(slopes[idx]), slope)

    @pl.when(i == 0)
    def _():
        kaug_ref[:, :hw] = k_ref[0]
        jrel = (lax.broadcasted_iota(I32, (n_key, hw), 0) & (tq - 1)).astype(F32) * slope
        klane = lax.broadcasted_iota(I32, (n_key, hw), 1)
        j_hi = jrel.astype(BF16).astype(F32)
        j_mid = (jrel - j_hi).astype(BF16).astype(F32)
        kaug_ref[:, hw:] = jnp.where(
            klane == 0, j_hi,
            jnp.where(klane == 1, j_mid, jnp.where(klane == 2, jrel - j_hi - j_mid, 0.0))).astype(BF16)
        vaug_ref[:, :hw] = v_ref[0]
        vaug_ref[:, hw:] = jnp.ones((n_key, hw), BF16)

    q = q_ref[0]
    lane = lax.broadcasted_iota(I32, q.shape, 1)
    zero = jnp.zeros_like(q)
    ones2 = jnp.where(lane < 3, 1.0, 0.0).astype(BF16)
    qaug_ref[:tq, :hw] = jnp.where(lane < DA_HEAD_DIM, q, zero)
    qaug_ref[tq:, :hw] = jnp.where(lane >= DA_HEAD_DIM, q, zero)
    qaug_ref[:tq, hw:] = ones2
    qaug_ref[tq:, hw:] = ones2

    m_ref[...] = jnp.full(m_ref.shape, NEG, F32)
    acc_ref[...] = jnp.zeros(acc_ref.shape, F32)

    def scores(j, buf):
        buf[...] = _nt(qaug_ref[...], kaug_ref[pl.ds(pl.multiple_of(j * tq, tq), tq), :])

    def consume(j, buf, diagonal):
        s = buf[...]
        if diagonal:
            ii = lax.broadcasted_iota(I32, s.shape, 0) & (tq - 1)
            jj = lax.broadcasted_iota(I32, s.shape, 1)
            ahead = jnp.minimum(ii - jj, 0).astype(F32)
            s = s + jnp.where((jj // CHUNK) <= (ii // CHUNK), (2.0 * slope) * ahead, NEG)
            shift = jnp.float32(0.0)
        else:
            shift = -slope * ((i - j) * tq).astype(F32)
        m_old = m_ref[...]
        m_new = jnp.maximum(m_old, jnp.max(s, axis=-1, keepdims=True) + shift)
        p = jnp.exp2(s - jnp.concatenate([m_new - shift] * (tq // LANES), axis=1))
        alpha = jnp.concatenate([jnp.exp2(m_old - m_new)] * (acc_ref.shape[1] // LANES), axis=1)
        vb = vaug_ref[pl.ds(pl.multiple_of(j * tq, tq), tq), :]
        acc_ref[...] = alpha * acc_ref[...] + _nn(p.astype(BF16), vb)
        m_ref[...] = m_new

    scores(0, sa_ref)

    def pair(pi, carry):
        j = 2 * pi
        scores(j + 1, sb_ref)
        consume(j, sa_ref, False)
        scores(j + 2, sa_ref)
        consume(j + 1, sb_ref, False)
        return carry

    lax.fori_loop(0, i // 2, pair, 0)

    @pl.when(i % 2 == 0)
    def _():
        consume(i, sa_ref, True)

    @pl.when(i % 2 == 1)
    def _():
        scores(i, sb_ref)
        consume(i - 1, sa_ref, False)
        consume(i, sb_ref, True)

    acc = acc_ref[...]
    o1 = acc[:tq, :hw] / acc[:tq, hw:]
    o2 = acc[tq:, :hw] / acc[tq:, hw:]
    lam = (jnp.exp(jnp.sum(lq1_ref[...] * lk1_ref[...], axis=-1, keepdims=True))
           - jnp.exp(jnp.sum(lq2_ref[...] * lk2_ref[...], axis=-1, keepdims=True)) + lam_init)
    o = o1 - lam * o2
    o = o * lax.rsqrt(jnp.mean(o * o, axis=-1, keepdims=True) + EPS) * g_ref[...]
    o_ref[0] = (o * (1.0 - lam_init)).astype(BF16)


def _attention(q, k, v, lq1, lk1, lq2, lk2, subln_g, lam_init):
    b, s, _ = q.shape
    tq = ATT_BLOCK
    hw = 2 * DA_HEAD_DIM
    assert hw == LANES
    slopes = tuple(LOG2E * 2.0 ** (-8.0 * (h + 1) / DA_HEADS) for h in range(DA_HEADS))
    vec = lambda n: pl.BlockSpec((1, n), lambda bi, h, i: (0, 0))
    return pl.pallas_call(
        functools.partial(_attn_kernel, lam_init=lam_init, slopes=slopes),
        grid=(b, DA_HEADS, s // tq),
        in_specs=[
            pl.BlockSpec((1, tq, hw), lambda bi, h, i: (bi, i, h)),
            pl.BlockSpec((1, s, hw), lambda bi, h, i: (bi, 0, h)),
            pl.BlockSpec((1, s, hw), lambda bi, h, i: (bi, 0, h)),
            vec(DA_HEAD_DIM), vec(DA_HEAD_DIM), vec(DA_HEAD_DIM), vec(DA_HEAD_DIM), vec(hw),
        ],
        out_specs=pl.BlockSpec((1, tq, hw), lambda bi, h, i: (bi, i, h)),
        out_shape=jax.ShapeDtypeStruct((b, s, DA_HEADS * hw), BF16),
        scratch_shapes=[
            pltpu.VMEM((s, 2 * hw), BF16),
            pltpu.VMEM((s, 2 * hw), BF16),
            pltpu.VMEM((2 * tq, 2 * hw), BF16),
            pltpu.VMEM((2 * tq, LANES), F32),
            pltpu.VMEM((2 * tq, 2 * hw), F32),
            pltpu.VMEM((2 * tq, tq), F32),
            pltpu.VMEM((2 * tq, tq), F32),
        ],
        compiler_params=pltpu.CompilerParams(vmem_limit_bytes=VMEM_LIMIT),
        name="attention",
    )(q, k, v, lq1, lk1, lq2, lk2, subln_g)


def _hgrn_kernel(qd_ref, kd_ref, ke_ref, v_ref, sg_ref, dec_ref, g_ref, o_ref, st_ref):
    @pl.when(pl.program_id(1) == 0)
    def _():
        st_ref[...] = jnp.zeros(st_ref.shape, F32)

    n_ch = qd_ref.shape[1] // CHUNK
    r = lax.broadcasted_iota(I32, (CHUNK, CHUNK), 0)
    c = lax.broadcasted_iota(I32, (CHUNK, CHUNK), 1)
    causal = r >= c

    def chunk(ci, carry):
        r0 = pl.multiple_of(ci * CHUNK, CHUNK)
        rows = pl.ds(r0, CHUNK)
        dec = dec_ref[0, pl.ds(ci, 1), :]
        for h in range(HG_HEADS):
            cols = slice(h * HG_DK, (h + 1) * HG_DK)
            qd = qd_ref[0, rows, cols]
            kd = kd_ref[0, rows, cols]
            ke = ke_ref[0, rows, cols]
            v = v_ref[0, rows, cols]
            st = st_ref[h]
            scores = jnp.where(causal, _nt(qd, kd), 0.0).astype(BF16)
            o = _nn(scores, v) + _nt(qd, st.astype(BF16))
            vt = v.astype(F32).T.astype(BF16)
            st_ref[h] = st * dec[:, cols] + _nn(vt, ke)
            o = o * lax.rsqrt(jnp.mean(o * o, axis=-1, keepdims=True) + EPS) * g_ref[...]
            o_ref[0, rows, cols] = (o * sg_ref[0, rows, cols].astype(F32)).astype(BF16)
        return carry

    lax.fori_loop(0, n_ch, chunk, 0)


def _hgrn(qd, kd, ke, v, sg, dec, norm_g):
    b, s, w = qd.shape
    ts = ROW_TILE
    tok = pl.BlockSpec((1, ts, w), lambda bi, i: (bi, i, 0))
    return pl.pallas_call(
        _hgrn_kernel,
        grid=(b, s // ts),
        in_specs=[tok, tok, tok, tok, tok,
                  pl.BlockSpec((1, ts // CHUNK, w), lambda bi, i: (bi, i, 0)),
                  pl.BlockSpec((1, HG_DV), lambda bi, i: (0, 0))],
        out_specs=tok,
        out_shape=jax.ShapeDtypeStruct((b, s, w), BF16),
        scratch_shapes=[pltpu.VMEM((HG_HEADS, HG_DV, HG_DK), F32)],
        name="hgrn",
    )(qd, kd, ke, v, sg, dec, norm_g)


def _merge_kernel(yda_ref, yhg_ref, gda_ref, ghg_ref, x_ref, gt1_ref, sh2_ref, sc2_ref, g2_ref,
                  wda_ref, whg_ref, wout_ref, wrh_ref, wrl_ref, br_ref,
                  x1_ref, h2_ref, eidx_ref, wts_ref):
    u = (gda_ref[0].astype(F32) * _nn(yda_ref[0], wda_ref[...])
         + ghg_ref[0].astype(F32) * _nn(yhg_ref[0], whg_ref[...]))
    x1 = x_ref[0] + gt1_ref[0] * _nn(u.astype(BF16), wout_ref[...])
    x1_ref[0] = x1
    h2 = x1 * lax.rsqrt(jnp.mean(x1 * x1, axis=-1, keepdims=True) + EPS) * g2_ref[...]
    h2 = h2 * (1.0 + sc2_ref[0]) + sh2_ref[0]
    h2_ref[0] = h2

    h_hi, h_lo = _split(h2)
    wrh = wrh_ref[...]
    lg = _nt(wrh, h_hi) + _nt(wrl_ref[...], h_hi) + _nt(wrh, h_lo) + br_ref[...]

    tm = h2.shape[0]
    eg = EXPERTS_PER_GROUP
    gl = lg[N_EXPERTS:N_EXPERTS + N_GROUPS]
    gmax = jnp.max(gl, axis=0, keepdims=True)
    g_p = 1.0 / jnp.sum(jnp.exp(gl - gmax), axis=0, keepdims=True)
    gi = lax.broadcasted_iota(I32, (N_GROUPS, tm), 0)
    g_idx = jnp.min(jnp.where(gl == gmax, gi, N_GROUPS), axis=0, keepdims=True)
    sel = lg[(N_GROUPS - 1) * eg:N_GROUPS * eg]
    for g in range(N_GROUPS - 2, -1, -1):
        sel = jnp.where(g_idx == g, lg[g * eg:(g + 1) * eg], sel)
    ei = lax.broadcasted_iota(I32, (eg, tm), 0)
    m1 = jnp.max(sel, axis=0, keepdims=True)
    i1 = jnp.min(jnp.where(sel == m1, ei, eg), axis=0, keepdims=True)
    sel2 = jnp.where(ei == i1, -jnp.inf, sel)
    m2 = jnp.max(sel2, axis=0, keepdims=True)
    i2 = jnp.min(jnp.where(sel2 == m2, ei, eg), axis=0, keepdims=True)
    t = jnp.exp(m2 - m1)
    wa = 1.0 / (1.0 + t)
    eidx_ref[0] = jnp.concatenate([g_idx * eg + i1, g_idx * eg + i2], axis=0)
    wts_ref[0] = jnp.concatenate([g_p * wa, g_p * (t * wa)], axis=0)


def _merge(yda, yhg, gda, ghg, x, mod, g2, wda, whg, wout, wrh, wrl, br):
    b, s, d = x.shape
    tm = ROW_TILE
    w = yda.shape[2]
    tok = lambda width: pl.BlockSpec((1, tm, width), lambda bi, i: (bi, i, 0))
    modspec = lambda k: pl.BlockSpec((1, 1, d), lambda bi, i: (bi, 0, k))
    full = lambda a: pl.BlockSpec(a.shape, lambda bi, i: (0,) * a.ndim)
    lane_rows = pl.BlockSpec((1, 2, tm), lambda bi, i: (bi, 0, i))
    return pl.pallas_call(
        _merge_kernel,
        grid=(b, s // tm),
        in_specs=[tok(w), tok(w), tok(d), tok(d), tok(d),
                  modspec(2), modspec(3), modspec(4), full(g2),
                  full(wda), full(whg), full(wout), full(wrh), full(wrl), full(br)],
        out_specs=[tok(d), tok(d), lane_rows, lane_rows],
        out_shape=[jax.ShapeDtypeStruct((b, s, d), F32), jax.ShapeDtypeStruct((b, s, d), F32),
                   jax.ShapeDtypeStruct((b, 2, s), I32), jax.ShapeDtypeStruct((b, 2, s), F32)],
        compiler_params=pltpu.CompilerParams(vmem_limit_bytes=VMEM_LIMIT),
        name="merge",
    )(yda, yhg, gda, ghg, x, mod, mod, mod, g2, wda, whg, wout, wrh, wrl, br)


def _positions_kernel(e_ref, dest_ref, blk_ref, nused_ref):
    e_mat = e_ref[...]
    nr = e_mat.shape[0]
    nb = blk_ref.shape[1]
    r = lax.broadcasted_iota(I32, (LANES, LANES), 0)
    c = lax.broadcasted_iota(I32, (LANES, LANES), 1)
    incl = jnp.where(r <= c, 1.0, 0.0).astype(BF16)
    ones = jnp.ones((LANES, LANES), BF16)
    rr = lax.broadcasted_iota(I32, (nr, nr), 0)
    cc = lax.broadcasted_iota(I32, (nr, nr), 1)
    before = jnp.where(cc < rr, 1.0, 0.0).astype(BF16)
    blk_start = (lax.broadcasted_iota(I32, (1, nb), 1) * EXPERT_BLOCK).astype(F32)

    def per_expert(e, carry):
        pstart, dest, blk = carry
        hit = e_mat == e
        oh = jnp.where(hit, 1.0, 0.0).astype(BF16)
        in_row = _nn(oh, incl)
        row_tot = _nn(oh, ones)
        row_off = _nn(before, row_tot.astype(BF16))
        total = row_off[nr - 1:nr, :] + row_tot[nr - 1:nr, :]
        dest = jnp.where(hit, pstart + row_off + in_row - 1.0, dest)
        pend = pstart + jnp.floor((total + (EXPERT_BLOCK - 1)) * (1.0 / EXPERT_BLOCK)) * EXPERT_BLOCK
        blk = blk + jnp.where(blk_start >= pend[:, :1], 1.0, 0.0)
        return pend, dest, blk

    pstart0 = jnp.zeros((1, LANES), F32)
    pend, dest, blk = lax.fori_loop(
        0, N_EXPERTS, per_expert,
        (pstart0, jnp.zeros(e_mat.shape, F32), jnp.zeros((1, nb), F32)))
    dest_ref[...] = dest.astype(I32)
    blk_ref[...] = jnp.minimum(blk, N_EXPERTS - 1).astype(I32)
    nused_ref[...] = (pend * (1.0 / EXPERT_BLOCK)).astype(I32)


def _positions(e_mat, n_blk_pad):
    nr = e_mat.shape[0]
    return pl.pallas_call(
        _positions_kernel,
        out_shape=[jax.ShapeDtypeStruct((nr, LANES), I32),
                   jax.ShapeDtypeStruct((1, n_blk_pad), I32),
                   jax.ShapeDtypeStruct((1, LANES), I32)],
        name="positions",
    )(e_mat)


def _dispatch_kernel(dest_ref, h2_ref, xs_in_ref, xs_ref, sem, *, n_tok):
    del xs_in_ref
    base = pl.program_id(0) * GATHER_TILE

    def copies(r, d0, d1):
        return (pltpu.make_async_copy(h2_ref.at[pl.ds(r, 1)], xs_ref.at[pl.ds(d0, 1)], sem.at[0]),
                pltpu.make_async_copy(h2_ref.at[pl.ds(r, 1)], xs_ref.at[pl.ds(d1, 1)], sem.at[1]))

    def issue(r, carry):
        for cp in copies(r, dest_ref[base + r], dest_ref[n_tok + base + r]):
            cp.start()
        return carry

    def drain(r, carry):
        for cp in copies(r, dest_ref[base + r], dest_ref[n_tok + base + r]):
            cp.wait()
        return carry

    lax.fori_loop(0, GATHER_TILE, issue, 0)
    lax.fori_loop(0, GATHER_TILE, drain, 0)


def _dispatch(dest, h2, xs_zero):
    n_tok = h2.shape[0]
    return pl.pallas_call(
        functools.partial(_dispatch_kernel, n_tok=n_tok),
        grid_spec=pltpu.PrefetchScalarGridSpec(
            num_scalar_prefetch=1,
            grid=(n_tok // GATHER_TILE,),
            in_specs=[pl.BlockSpec((GATHER_TILE, h2.shape[1]), lambda i, dest: (i, 0)),
                      pl.BlockSpec(memory_space=pl.ANY)],
            out_specs=pl.BlockSpec(memory_space=pl.ANY),
            scratch_shapes=[pltpu.SemaphoreType.DMA((2,))],
        ),
        out_shape=jax.ShapeDtypeStruct(xs_zero.shape, xs_zero.dtype),
        input_output_aliases={2: 0},
        name="dispatch",
    )(dest, h2, xs_zero)


def _expert_kernel(blk_ref, nused_ref, xs_ref, w1_ref, w3_ref, w2_ref, ys_ref, w1b, w3b, w2b):
    i = pl.program_id(0)
    e = blk_ref[i]
    fresh = (i == 0) | (blk_ref[jnp.maximum(i - 1, 0)] != e)

    @pl.when(i < nused_ref[0])
    def _():
        @pl.when(fresh)
        def _():
            w1b[...] = w1_ref[0].astype(BF16)
            w3b[...] = w3_ref[0].astype(BF16)
            w2b[...] = w2_ref[0].astype(BF16)

        x = xs_ref[...].astype(BF16)
        a = _nn(x, w1b[...])
        ys_ref[...] = _nn((_silu(a) * _nn(x, w3b[...])).astype(BF16), w2b[...])

    @pl.when(i >= nused_ref[0])
    def _():
        ys_ref[...] = jnp.zeros(ys_ref.shape, F32)


def _experts(blk_e, n_used, xs, w1, w3, w2):
    p, d = xs.shape
    de = w1.shape[2]
    rows = lambda i, blk, nu: (jnp.minimum(i, nu[0] - 1), 0)
    wsel = lambda i, blk, nu: (blk[i], 0, 0)
    return pl.pallas_call(
        _expert_kernel,
        grid_spec=pltpu.PrefetchScalarGridSpec(
            num_scalar_prefetch=2,
            grid=(p // EXPERT_BLOCK,),
            in_specs=[pl.BlockSpec((EXPERT_BLOCK, d), rows),
                      pl.BlockSpec((1, d, de), wsel),
                      pl.BlockSpec((1, d, de), wsel),
                      pl.BlockSpec((1, de, d), wsel)],
            out_specs=pl.BlockSpec((EXPERT_BLOCK, d), lambda i, blk, nu: (i, 0)),
            scratch_shapes=[pltpu.VMEM((d, de), BF16), pltpu.VMEM((d, de), BF16),
                            pltpu.VMEM((de, d), BF16)],
        ),
        out_shape=jax.ShapeDtypeStruct((p, d), F32),
        compiler_params=pltpu.CompilerParams(vmem_limit_bytes=VMEM_LIMIT),
        name="experts",
    )(blk_e, n_used, xs, w1, w3, w2)


def _combine_kernel(dest_ref, ys_ref, x1_ref, gt2_ref, wts_ref, g_ref, o_ref, y0, y1, sem, *, n_tok, last):
    base = pl.program_id(0) * GATHER_TILE

    def copies(r, d0, d1):
        return (pltpu.make_async_copy(ys_ref.at[pl.ds(d0, 1)], y0.at[pl.ds(r, 1)], sem.at[0]),
                pltpu.make_async_copy(ys_ref.at[pl.ds(d1, 1)], y1.at[pl.ds(r, 1)], sem.at[1]))

    def issue(r, carry):
        for cp in copies(r, dest_ref[base + r], dest_ref[n_tok + base + r]):
            cp.start()
        return carry

    def drain(r, carry):
        for cp in copies(r, dest_ref[base + r], dest_ref[n_tok + base + r]):
            cp.wait()
        return carry

    lax.fori_loop(0, GATHER_TILE, issue, 0)
    lax.fori_loop(0, GATHER_TILE, drain, 0)

    wts = wts_ref[...]
    x = x1_ref[...] + gt2_ref[0] * (wts[:, 0:1] * y0[...] + wts[:, 1:2] * y1[...])
    if last:
        x = x * lax.rsqrt(jnp.mean(x * x, axis=-1, keepdims=True) + EPS) * g_ref[...]
    o_ref[...] = x


def _combine(dest, ys, x1, mod, wts, final_g, seq, last=True):
    n_tok, d = x1.shape
    tt = GATHER_TILE
    per_batch = seq // tt
    return pl.pallas_call(
        functools.partial(_combine_kernel, n_tok=n_tok, last=last),
        grid_spec=pltpu.PrefetchScalarGridSpec(
            num_scalar_prefetch=1,
            grid=(n_tok // tt,),
            in_specs=[pl.BlockSpec(memory_space=pl.ANY),
                      pl.BlockSpec((tt, d), lambda i, dest: (i, 0)),
                      pl.BlockSpec((1, 1, d), lambda i, dest: (i // per_batch, 0, 5)),
                      pl.BlockSpec((tt, 2), lambda i, dest: (i, 0)),
                      pl.BlockSpec((1, d), lambda i, dest: (0, 0))],
            out_specs=pl.BlockSpec((tt, d), lambda i, dest: (i, 0)),
            scratch_shapes=[pltpu.VMEM((tt, d), F32), pltpu.VMEM((tt, d), F32),
                            pltpu.SemaphoreType.DMA((2,))],
        ),
        out_shape=jax.ShapeDtypeStruct((n_tok, d), F32),
        name="combine",
    )(dest, ys, x1, mod, wts, final_g)


def kernel(x, c, w_ada, b_ada, norm1_g, w_in, lambda_q1, lambda_k1, lambda_q2, lambda_k2, da_subln_g, hg_lb_logits, hg_norm_g, w_up_da, w_up_hg, w_out, norm2_g, w_rg, b_rg, w_re, b_re, w1, w3, w2, final_g):
    b, s, d = x.shape
    n_tok = b * s
    depth = w_ada.shape[0]
    assert s % ROW_TILE == 0 and s % ATT_BLOCK == 0 and s % GATHER_TILE == 0
    assert (2 * n_tok) % LANES == 0

    c_pad = jnp.pad(c, ((0, (-b) % 8), (0, 0)))
    n_slots = 2 * n_tok + N_EXPERTS * EXPERT_BLOCK
    n_blk = n_slots // EXPERT_BLOCK
    n_blk_pad = -(-n_blk // LANES) * LANES

    for l in range(depth):
        lam_init = 0.8 - 0.6 * math.exp(-0.3 * l)
        mod = _adaln(c_pad, w_ada[l], b_ada[l][None, :])[:b].reshape(b, 1, 6 * d)

        (qda, kda, vda, qd, kd, ke, hi, sg, dec, gda, ghg) = _inproj(
            x, mod, norm1_g[l][None, :], w_in[l].astype(BF16), hg_lb_logits, l)
        yda = _attention(qda, kda, vda, lambda_q1[l][None, :], lambda_k1[l][None, :],
                         lambda_q2[l][None, :], lambda_k2[l][None, :], da_subln_g[l][None, :], lam_init)
        yhg = _hgrn(qd, kd, ke, hi, sg, dec, hg_norm_g[l][None, :])

        wr = jnp.concatenate([w_re[l].T, w_rg[l].T,
                              jnp.zeros((ROUTER_ROWS - N_EXPERTS - N_GROUPS, d), F32)], axis=0)
        wr_hi = wr.astype(BF16)
        wr_lo = (wr - wr_hi.astype(F32)).astype(BF16)
        br = jnp.concatenate([b_re[l], b_rg[l],
                              jnp.zeros((ROUTER_ROWS - N_EXPERTS - N_GROUPS,), F32)])[:, None]
        x1, h2, eidx, wts = _merge(yda, yhg, gda, ghg, x, mod, norm2_g[l][None, :],
                                   w_up_da[l].astype(BF16), w_up_hg[l].astype(BF16),
                                   w_out[l].astype(BF16), wr_hi, wr_lo, br)

        e_mat = jnp.transpose(eidx, (1, 0, 2)).reshape(2 * n_tok // LANES, LANES)
        dest, blk_e, n_used = _positions(e_mat, n_blk_pad)
        dest = dest.reshape(2 * n_tok)
        xs = _dispatch(dest, h2.reshape(n_tok, d), jnp.zeros((n_slots, d), F32))
        ys = _experts(blk_e[0, :n_blk], n_used[0, :1], xs, w1[l], w3[l], w2[l])
        wts_tok = jnp.transpose(wts, (0, 2, 1)).reshape(n_tok, 2)
        x = _combine(dest, ys, x1.reshape(n_tok, d), mod, wts_tok, final_g[None, :], s,
                     last=(l == depth - 1)).reshape(b, s, d)
    return x
```

```python
import functools
import math

import jax
import jax.numpy as jnp
from jax import lax
from jax.experimental import pallas as pl
from jax.experimental.pallas import tpu as pltpu

F32 = jnp.float32
BF16 = jnp.bfloat16
I32 = jnp.int32

EPS = 1e-6
NEG = -1e30
LOG2E = math.log2(math.e)

CHUNK = 64
DA_HEADS = 4
DA_HEAD_DIM = 64
HG_HEADS = 4
HG_DK = 128
HG_DV = 128
N_GROUPS = 4
EXPERTS_PER_GROUP = 8
N_EXPERTS = N_GROUPS * EXPERTS_PER_GROUP

LANES = 128
SUBLANES = 8
ROW_TILE = 512
ATT_BLOCK = 512
EXPERT_BLOCK = 256
GATHER_TILE = 256
ISSUE_UNROLL = 8
ROUTER_ROWS = 40
VMEM_LIMIT = 52 * 1024 * 1024


def _nt(a, b):
    return lax.dot_general(a, b, (((1,), (1,)), ((), ())), preferred_element_type=F32)


def _nn(a, b):
    return jnp.dot(a, b, preferred_element_type=F32)


def _split(a):
    hi = a.astype(BF16)
    lo = (a - hi.astype(F32)).astype(BF16)
    return hi, lo


def _sigmoid(x):
    return 1.0 / (1.0 + jnp.exp(-x))


def _silu(x):
    return x * _sigmoid(x)


def _adaln_kernel(c_ref, w_ref, b_ref, o_ref):
    c_hi, c_lo = _split(_silu(c_ref[...]))
    w_hi, w_lo = _split(w_ref[...])
    o_ref[...] = _nn(c_hi, w_hi) + _nn(c_lo, w_hi) + _nn(c_hi, w_lo) + b_ref[...]


def _adaln(c_pad, w, b):
    rows, d = c_pad.shape
    n = w.shape[1]
    tn = 1024
    return pl.pallas_call(
        _adaln_kernel,
        grid=(n // tn,),
        in_specs=[
            pl.BlockSpec((rows, d), lambda j: (0, 0)),
            pl.BlockSpec((d, tn), lambda j: (0, j)),
            pl.BlockSpec((1, tn), lambda j: (0, j)),
        ],
        out_specs=pl.BlockSpec((rows, tn), lambda j: (0, j)),
        out_shape=jax.ShapeDtypeStruct((rows, n), F32),
        name="adaln",
    )(c_pad, w, b)


def _inproj_kernel(x_ref, sh_ref, sc_ref, g_ref, w_ref, lb_ref,
                   qda_ref, kda_ref, vda_ref, qd_ref, kd_ref, ke_ref, hi_ref, sg_ref,
                   dec_ref, gda_ref, ghg_ref, *, layer):
    x = x_ref[0]
    tm = x.shape[0]
    h = x * lax.rsqrt(jnp.mean(x * x, axis=-1, keepdims=True) + EPS) * g_ref[...]
    h = h * (1.0 + sc_ref[0]) + sh_ref[0]
    hb = h.astype(BF16)

    def proj(c0, width):
        return _nn(hb, w_ref[:, c0:c0 + width])

    w_da = DA_HEADS * 2 * DA_HEAD_DIM
    w_hg = HG_HEADS * HG_DK
    qda_ref[0] = (proj(0, w_da) * (LOG2E / math.sqrt(DA_HEAD_DIM))).astype(BF16)
    kda_ref[0] = proj(w_da, w_da).astype(BF16)
    vda_ref[0] = proj(2 * w_da, w_da).astype(BF16)
    c0 = 3 * w_da

    lbl = lb_ref[...]
    lbe = jnp.exp(lbl - jnp.max(lbl, axis=0, keepdims=True))
    lbs = lbe / jnp.sum(lbe, axis=0, keepdims=True)
    lb = jnp.sum(lbs[:layer + 1], axis=0, keepdims=True)

    f = lb + (1.0 - lb) * _sigmoid(proj(c0 + w_hg, w_hg))
    logf = jnp.log(f)
    kk = 1.0 - f
    r = lax.broadcasted_iota(I32, (tm, tm), 0)
    c = lax.broadcasted_iota(I32, (tm, tm), 1)
    tri = jnp.where((r >= c) & ((r // CHUNK) == (c // CHUNK)), 1.0, 0.0).astype(BF16)
    lf_hi, lf_lo = _split(logf)
    n_ch = tm // CHUNK
    a = (_nn(tri, lf_hi) + _nn(tri, lf_lo)).reshape(n_ch, CHUNK, w_hg)
    a_last = a[:, CHUNK - 1:CHUNK, :]
    kk3 = kk.reshape(n_ch, CHUNK, w_hg)
    qq3 = _silu(proj(c0, w_hg)).reshape(n_ch, CHUNK, w_hg)
    qd_ref[0] = (qq3 * jnp.exp(a)).reshape(tm, w_hg).astype(BF16)
    kd_ref[0] = (kk3 * jnp.exp(-a)).reshape(tm, w_hg).astype(BF16)
    ke_ref[0] = (kk3 * jnp.exp(a_last - a)).reshape(tm, w_hg).astype(BF16)
    dec_ref[0] = jnp.exp(a_last).reshape(n_ch, w_hg)
    hi_ref[0] = proj(c0 + 2 * w_hg, w_hg).astype(BF16)
    sg_ref[0] = _silu(proj(c0 + 3 * w_hg, w_hg)).astype(BF16)
    c1 = c0 + 4 * w_hg
    d = x.shape[1]
    gda_ref[0] = _sigmoid(proj(c1, d)).astype(BF16)
    ghg_ref[0] = _sigmoid(proj(c1 + d, d)).astype(BF16)


def _inproj(x, mod, g, w_bf, lb_logits, layer):
    b, s, d = x.shape
    tm = ROW_TILE
    n_cols = w_bf.shape[1]
    w_da = DA_HEADS * 2 * DA_HEAD_DIM
    w_hg = HG_HEADS * HG_DK

    def tok(width, dtype):
        return (pl.BlockSpec((1, tm, width), lambda bi, i: (bi, i, 0)),
                jax.ShapeDtypeStruct((b, s, width), dtype))

    outs = [tok(w_da, BF16)] * 3 + [tok(w_hg, BF16)] * 5
    outs.append((pl.BlockSpec((1, tm // CHUNK, w_hg), lambda bi, i: (bi, i, 0)),
                 jax.ShapeDtypeStruct((b, s // CHUNK, w_hg), F32)))
    outs += [tok(d, BF16)] * 2
    return pl.pallas_call(
        functools.partial(_inproj_kernel, layer=layer),
        grid=(b, s // tm),
        in_specs=[
            pl.BlockSpec((1, tm, d), lambda bi, i: (bi, i, 0)),
            pl.BlockSpec((1, 1, d), lambda bi, i: (bi, 0, 0)),
            pl.BlockSpec((1, 1, d), lambda bi, i: (bi, 0, 1)),
            pl.BlockSpec((1, d), lambda bi, i: (0, 0)),
            pl.BlockSpec((d, n_cols), lambda bi, i: (0, 0), pipeline_mode=pl.Buffered(1)),
            pl.BlockSpec(lb_logits.shape, lambda bi, i: (0, 0)),
        ],
        out_specs=[o[0] for o in outs],
        out_shape=[o[1] for o in outs],
        compiler_params=pltpu.CompilerParams(vmem_limit_bytes=VMEM_LIMIT),
        name="inproj",
    )(x, mod, mod, g, w_bf, lb_logits)


def _attn_kernel(q_ref, k_ref, v_ref, lq1_ref, lk1_ref, lq2_ref, lk2_ref, g_ref, o_ref,
                 kaug_ref, vaug_ref, qaug_ref, m_ref, acc_ref, sa_ref, sb_ref, *, lam_init, slopes):
    hd = pl.program_id(1)
    i = pl.program_id(2)
    tq = q_ref.shape[1]
    hw = q_ref.shape[2]
    n_key = k_ref.shape[1]

    slope = jnp.float32(slopes[-1])
    for idx in range(len(slopes) - 2, -1, -1):
        slope = jnp.where(hd == idx, jnp.float32(slopes[idx]), slope)

    @pl.when(i == 0)
    def _():
        kaug_ref[:, :hw] = k_ref[0]
        jrel = (lax.broadcasted_iota(I32, (n_key, hw), 0) & (tq - 1)).astype(F32) * slope
        klane = lax.broadcasted_iota(I32, (n_key, hw), 1)
        j_hi = jrel.astype(BF16).astype(F32)
        j_mid = (jrel - j_hi).astype(BF16).astype(F32)
        kaug_ref[:, hw:] = jnp.where(
            klane == 0, j_hi,
            jnp.where(klane == 1, j_mid, jnp.where(klane == 2, jrel - j_hi - j_mid, 0.0))).astype(BF16)
        vaug_ref[:, :hw] = v_ref[0]
        vaug_ref[:, hw:] = jnp.ones((n_key, hw), BF16)

    q = q_ref[0]
    lane = lax.broadcasted_iota(I32, q.shape, 1)
    zero = jnp.zeros_like(q)
    ones2 = jnp.where(lane < 3, 1.0, 0.0).astype(BF16)
    qaug_ref[:tq, :hw] = jnp.where(lane < DA_HEAD_DIM, q, zero)
    qaug_ref[tq:, :hw] = jnp.where(lane >= DA_HEAD_DIM, q, zero)
    qaug_ref[:tq, hw:] = ones2
    qaug_ref[tq:, hw:] = ones2

    m_ref[...] = jnp.full(m_ref.shape, NEG, F32)
    acc_ref[...] = jnp.zeros(acc_ref.shape, F32)

    def scores(j, buf):
        buf[...] = _nt(qaug_ref[...], kaug_ref[pl.ds(pl.multiple_of(j * tq, tq), tq), :])

    def consume(j, buf, diagonal):
        s = buf[...]
        if diagonal:
            ii = lax.broadcasted_iota(I32, s.shape, 0) & (tq - 1)
            jj = lax.broadcasted_iota(I32, s.shape, 1)
            ahead = jnp.minimum(ii - jj, 0).astype(F32)
            s = s + jnp.where((jj // CHUNK) <= (ii // CHUNK), (2.0 * slope) * ahead, NEG)
            shift = jnp.float32(0.0)
        else:
            shift = -slope * ((i - j) * tq).astype(F32)
        m_old = m_ref[...]
        m_new = jnp.maximum(m_old, jnp.max(s, axis=-1, keepdims=True) + shift)
        p = jnp.exp2(s - jnp.concatenate([m_new - shift] * (tq // LANES), axis=1))
        alpha = jnp.concatenate([jnp.exp2(m_old - m_new)] * (acc_ref.shape[1] // LANES), axis=1)
        vb = vaug_ref[pl.ds(pl.multiple_of(j * tq, tq), tq), :]
        acc_ref[...] = alpha * acc_ref[...] + _nn(p.astype(BF16), vb)
        m_ref[...] = m_new

    scores(0, sa_ref)

    def pair(pi, carry):
        j = 2 * pi
        scores(j + 1, sb_ref)
        consume(j, sa_ref, False)
        scores(j + 2, sa_ref)
        consume(j + 1, sb_ref, False)
        return carry

    lax.fori_loop(0, i // 2, pair, 0)

    @pl.when(i % 2 == 0)
    def _():
        consume(i, sa_ref, True)

    @pl.when(i % 2 == 1)
    def _():
        scores(i, sb_ref)
        consume(i - 1, sa_ref, False)
        consume(i, sb_ref, True)

    acc = acc_ref[...]
    o1 = acc[:tq, :hw] / acc[:tq, hw:]
    o2 = acc[tq:, :hw] / acc[tq:, hw:]
    lam = (jnp.exp(jnp.sum(lq1_ref[...] * lk1_ref[...], axis=-1, keepdims=True))
           - jnp.exp(jnp.sum(lq2_ref[...] * lk2_ref[...], axis=-1, keepdims=True)) + lam_init)
    o = o1 - lam * o2
    o = o * lax.rsqrt(jnp.mean(o * o, axis=-1, keepdims=True) + EPS) * g_ref[...]
    o_ref[0] = (o * (1.0 - lam_init)).astype(BF16)


def _attention(q, k, v, lq1, lk1, lq2, lk2, subln_g, lam_init):
    b, s, _ = q.shape
    tq = ATT_BLOCK
    hw = 2 * DA_HEAD_DIM
    assert hw == LANES
    slopes = tuple(LOG2E * 2.0 ** (-8.0 * (h + 1) / DA_HEADS) for h in range(DA_HEADS))
    vec = lambda n: pl.BlockSpec((1, n), lambda bi, h, i: (0, 0))
    return pl.pallas_call(
        functools.partial(_attn_kernel, lam_init=lam_init, slopes=slopes),
        grid=(b, DA_HEADS, s // tq),
        in_specs=[
            pl.BlockSpec((1, tq, hw), lambda bi, h, i: (bi, i, h)),
            pl.BlockSpec((1, s, hw), lambda bi, h, i: (bi, 0, h)),
            pl.BlockSpec((1, s, hw), lambda bi, h, i: (bi, 0, h)),
            vec(DA_HEAD_DIM), vec(DA_HEAD_DIM), vec(DA_HEAD_DIM), vec(DA_HEAD_DIM), vec(hw),
        ],
        out_specs=pl.BlockSpec((1, tq, hw), lambda bi, h, i: (bi, i, h)),
        out_shape=jax.ShapeDtypeStruct((b, s, DA_HEADS * hw), BF16),
        scratch_shapes=[
            pltpu.VMEM((s, 2 * hw), BF16),
            pltpu.VMEM((s, 2 * hw), BF16),
            pltpu.VMEM((2 * tq, 2 * hw), BF16),
            pltpu.VMEM((2 * tq, LANES), F32),
            pltpu.VMEM((2 * tq, 2 * hw), F32),
            pltpu.VMEM((2 * tq, tq), F32),
            pltpu.VMEM((2 * tq, tq), F32),
        ],
        compiler_params=pltpu.CompilerParams(vmem_limit_bytes=VMEM_LIMIT),
        name="attention",
    )(q, k, v, lq1, lk1, lq2, lk2, subln_g)


def _hgrn_kernel(qd_ref, kd_ref, ke_ref, v_ref, sg_ref, dec_ref, g_ref, o_ref, st_ref):
    @pl.when(pl.program_id(1) == 0)
    def _():
        st_ref[...] = jnp.zeros(st_ref.shape, F32)

    n_ch = qd_ref.shape[1] // CHUNK
    r = lax.broadcasted_iota(I32, (CHUNK, CHUNK), 0)
    c = lax.broadcasted_iota(I32, (CHUNK, CHUNK), 1)
    causal = r >= c

    def chunk(ci, carry):
        r0 = pl.multiple_of(ci * CHUNK, CHUNK)
        rows = pl.ds(r0, CHUNK)
        dec = dec_ref[0, pl.ds(ci, 1), :]
        for h in range(HG_HEADS):
            cols = slice(h * HG_DK, (h + 1) * HG_DK)
            qd = qd_ref[0, rows, cols]
            kd = kd_ref[0, rows, cols]
            ke = ke_ref[0, rows, cols]
            v = v_ref[0, rows, cols]
            st = st_ref[h]
            scores = jnp.where(causal, _nt(qd, kd), 0.0).astype(BF16)
            o = _nn(scores, v) + _nt(qd, st.astype(BF16))
            vt = v.astype(F32).T.astype(BF16)
            st_ref[h] = st * dec[:, cols] + _nn(vt, ke)
            o = o * lax.rsqrt(jnp.mean(o * o, axis=-1, keepdims=True) + EPS) * g_ref[...]
            o_ref[0, rows, cols] = (o * sg_ref[0, rows, cols].astype(F32)).astype(BF16)
        return carry

    lax.fori_loop(0, n_ch, chunk, 0)


def _hgrn(qd, kd, ke, v, sg, dec, norm_g):
    b, s, w = qd.shape
    ts = ROW_TILE
    tok = pl.BlockSpec((1, ts, w), lambda bi, i: (bi, i, 0))
    return pl.pallas_call(
        _hgrn_kernel,
        grid=(b, s // ts),
        in_specs=[tok, tok, tok, tok, tok,
                  pl.BlockSpec((1, ts // CHUNK, w), lambda bi, i: (bi, i, 0)),
                  pl.BlockSpec((1, HG_DV), lambda bi, i: (0, 0))],
        out_specs=tok,
        out_shape=jax.ShapeDtypeStruct((b, s, w), BF16),
        scratch_shapes=[pltpu.VMEM((HG_HEADS, HG_DV, HG_DK), F32)],
        name="hgrn",
    )(qd, kd, ke, v, sg, dec, norm_g)


def _merge_kernel(yda_ref, yhg_ref, gda_ref, ghg_ref, x_ref, gt1_ref, sh2_ref, sc2_ref, g2_ref,
                  wda_ref, whg_ref, wout_ref, wrh_ref, wrl_ref, br_ref,
                  x1_ref, h2_ref, eidx_ref, wts_ref):
    u = (gda_ref[0].astype(F32) * _nn(yda_ref[0], wda_ref[...])
         + ghg_ref[0].astype(F32) * _nn(yhg_ref[0], whg_ref[...]))
    x1 = x_ref[0] + gt1_ref[0] * _nn(u.astype(BF16), wout_ref[...])
    x1_ref[0] = x1
    h2 = x1 * lax.rsqrt(jnp.mean(x1 * x1, axis=-1, keepdims=True) + EPS) * g2_ref[...]
    h2 = h2 * (1.0 + sc2_ref[0]) + sh2_ref[0]
    h2_ref[0] = h2.reshape(h2.shape[0], SUBLANES, h2.shape[1] // SUBLANES)

    h_hi, h_lo = _split(h2)
    wrh = wrh_ref[...]
    lg = _nt(wrh, h_hi) + _nt(wrl_ref[...], h_hi) + _nt(wrh, h_lo) + br_ref[...]

    tm = h2.shape[0]
    eg = EXPERTS_PER_GROUP
    gl = lg[N_EXPERTS:N_EXPERTS + N_GROUPS]
    gmax = jnp.max(gl, axis=0, keepdims=True)
    g_p = 1.0 / jnp.sum(jnp.exp(gl - gmax), axis=0, keepdims=True)
    gi = lax.broadcasted_iota(I32, (N_GROUPS, tm), 0)
    g_idx = jnp.min(jnp.where(gl == gmax, gi, N_GROUPS), axis=0, keepdims=True)
    sel = lg[(N_GROUPS - 1) * eg:N_GROUPS * eg]
    for g in range(N_GROUPS - 2, -1, -1):
        sel = jnp.where(g_idx == g, lg[g * eg:(g + 1) * eg], sel)
    ei = lax.broadcasted_iota(I32, (eg, tm), 0)
    m1 = jnp.max(sel, axis=0, keepdims=True)
    i1 = jnp.min(jnp.where(sel == m1, ei, eg), axis=0, keepdims=True)
    sel2 = jnp.where(ei == i1, -jnp.inf, sel)
    m2 = jnp.max(sel2, axis=0, keepdims=True)
    i2 = jnp.min(jnp.where(sel2 == m2, ei, eg), axis=0, keepdims=True)
    t = jnp.exp(m2 - m1)
    wa = 1.0 / (1.0 + t)
    eidx_ref[0] = jnp.concatenate([g_idx * eg + i1, g_idx * eg + i2], axis=0)
    wts_ref[0] = jnp.concatenate([g_p * wa, g_p * (t * wa)], axis=0)


def _merge(yda, yhg, gda, ghg, x, mod, g2, wda, whg, wout, wrh, wrl, br):
    b, s, d = x.shape
    tm = ROW_TILE
    w = yda.shape[2]
    tok = lambda width: pl.BlockSpec((1, tm, width), lambda bi, i: (bi, i, 0))
    modspec = lambda k: pl.BlockSpec((1, 1, d), lambda bi, i: (bi, 0, k))
    full = lambda a: pl.BlockSpec(a.shape, lambda bi, i: (0,) * a.ndim)
    lane_rows = pl.BlockSpec((1, 2, tm), lambda bi, i: (bi, 0, i))
    return pl.pallas_call(
        _merge_kernel,
        grid=(b, s // tm),
        in_specs=[tok(w), tok(w), tok(d), tok(d), tok(d),
                  modspec(2), modspec(3), modspec(4), full(g2),
                  full(wda), full(whg), full(wout), full(wrh), full(wrl), full(br)],
        out_specs=[tok(d), pl.BlockSpec((1, tm, SUBLANES, d // SUBLANES), lambda bi, i: (bi, i, 0, 0)),
                   lane_rows, lane_rows],
        out_shape=[jax.ShapeDtypeStruct((b, s, d), F32),
                   jax.ShapeDtypeStruct((b, s, SUBLANES, d // SUBLANES), F32),
                   jax.ShapeDtypeStruct((b, 2, s), I32), jax.ShapeDtypeStruct((b, 2, s), F32)],
        compiler_params=pltpu.CompilerParams(vmem_limit_bytes=VMEM_LIMIT),
        name="merge",
    )(yda, yhg, gda, ghg, x, mod, mod, mod, g2, wda, whg, wout, wrh, wrl, br)


def _positions_kernel(e_ref, dest_ref, blk_ref, nused_ref):
    e_mat = e_ref[...]
    nr = e_mat.shape[0]
    nb = blk_ref.shape[1]
    r = lax.broadcasted_iota(I32, (LANES, LANES), 0)
    c = lax.broadcasted_iota(I32, (LANES, LANES), 1)
    incl = jnp.where(r <= c, 1.0, 0.0).astype(BF16)
    ones = jnp.ones((LANES, LANES), BF16)
    rr = lax.broadcasted_iota(I32, (nr, nr), 0)
    cc = lax.broadcasted_iota(I32, (nr, nr), 1)
    before = jnp.where(cc < rr, 1.0, 0.0).astype(BF16)
    blk_start = (lax.broadcasted_iota(I32, (1, nb), 1) * EXPERT_BLOCK).astype(F32)

    def per_expert(e, carry):
        pstart, dest, blk = carry
        hit = e_mat == e
        oh = jnp.where(hit, 1.0, 0.0).astype(BF16)
        in_row = _nn(oh, incl)
        row_tot = _nn(oh, ones)
        row_off = _nn(before, row_tot.astype(BF16))
        total = row_off[nr - 1:nr, :] + row_tot[nr - 1:nr, :]
        dest = jnp.where(hit, pstart + row_off + in_row - 1.0, dest)
        pend = pstart + jnp.floor((total + (EXPERT_BLOCK - 1)) * (1.0 / EXPERT_BLOCK)) * EXPERT_BLOCK
        blk = blk + jnp.where(blk_start >= pend[:, :1], 1.0, 0.0)
        return pend, dest, blk

    pstart0 = jnp.zeros((1, LANES), F32)
    pend, dest, blk = lax.fori_loop(
        0, N_EXPERTS, per_expert,
        (pstart0, jnp.zeros(e_mat.shape, F32), jnp.zeros((1, nb), F32)))
    dest_ref[...] = dest.astype(I32)
    blk_ref[...] = jnp.minimum(blk, N_EXPERTS - 1).astype(I32)
    nused_ref[...] = (pend * (1.0 / EXPERT_BLOCK)).astype(I32)


def _positions(e_mat, n_blk_pad):
    nr = e_mat.shape[0]
    return pl.pallas_call(
        _positions_kernel,
        out_shape=[jax.ShapeDtypeStruct((nr, LANES), I32),
                   jax.ShapeDtypeStruct((1, n_blk_pad), I32),
                   jax.ShapeDtypeStruct((1, LANES), I32)],
        name="positions",
    )(e_mat)


def _dispatch_kernel(dest_ref, h2_ref, xs_in_ref, xs_ref, sem, *, n_tok):
    del xs_in_ref
    base = pl.program_id(0) * GATHER_TILE

    def copies(r, d0, d1):
        return (pltpu.make_async_copy(h2_ref.at[r], xs_ref.at[d0], sem.at[0]),
                pltpu.make_async_copy(h2_ref.at[r], xs_ref.at[d1], sem.at[1]))

    def issue(r, carry):
        for k, cp in enumerate(copies(r, dest_ref[base + r], dest_ref[n_tok + base + r])):
            cp.start(priority=k)
        return carry

    lax.fori_loop(0, GATHER_TILE, issue, 0, unroll=ISSUE_UNROLL)
    for k in range(2):
        pltpu.make_async_copy(h2_ref, xs_ref.at[pl.ds(0, GATHER_TILE)], sem.at[k]).wait()


def _dispatch(dest, h2, xs_zero):
    n_tok = h2.shape[0]
    return pl.pallas_call(
        functools.partial(_dispatch_kernel, n_tok=n_tok),
        grid_spec=pltpu.PrefetchScalarGridSpec(
            num_scalar_prefetch=1,
            grid=(n_tok // GATHER_TILE,),
            in_specs=[pl.BlockSpec((GATHER_TILE,) + h2.shape[1:], lambda i, dest: (i, 0, 0)),
                      pl.BlockSpec(memory_space=pl.ANY)],
            out_specs=pl.BlockSpec(memory_space=pl.ANY),
            scratch_shapes=[pltpu.SemaphoreType.DMA((2,))],
        ),
        out_shape=jax.ShapeDtypeStruct(xs_zero.shape, xs_zero.dtype),
        input_output_aliases={2: 0},
        name="dispatch",
    )(dest, h2, xs_zero)


def _expert_kernel(blk_ref, nused_ref, xs_ref, w1_ref, w3_ref, w2_ref, ys_ref, w1b, w3b, w2b):
    i = pl.program_id(0)
    e = blk_ref[i]
    fresh = (i == 0) | (blk_ref[jnp.maximum(i - 1, 0)] != e)

    @pl.when(i < nused_ref[0])
    def _():
        @pl.when(fresh)
        def _():
            w1b[...] = w1_ref[0].astype(BF16)
            w3b[...] = w3_ref[0].astype(BF16)
            w2b[...] = w2_ref[0].astype(BF16)

        rows = xs_ref.shape[0]
        x = xs_ref[...].reshape(rows, w1b.shape[0]).astype(BF16)
        a = _nn(x, w1b[...])
        y = _nn((_silu(a) * _nn(x, w3b[...])).astype(BF16), w2b[...])
        ys_ref[...] = y.reshape(ys_ref.shape)

    @pl.when(i >= nused_ref[0])
    def _():
        ys_ref[...] = jnp.zeros(ys_ref.shape, F32)


def _experts(blk_e, n_used, xs, w1, w3, w2):
    p = xs.shape[0]
    tile = xs.shape[1:]
    d, de = w1.shape[1], w1.shape[2]
    rows = lambda i, blk, nu: (jnp.minimum(i, nu[0] - 1), 0, 0)
    wsel = lambda i, blk, nu: (blk[i], 0, 0)
    return pl.pallas_call(
        _expert_kernel,
        grid_spec=pltpu.PrefetchScalarGridSpec(
            num_scalar_prefetch=2,
            grid=(p // EXPERT_BLOCK,),
            in_specs=[pl.BlockSpec((EXPERT_BLOCK,) + tile, rows),
                      pl.BlockSpec((1, d, de), wsel),
                      pl.BlockSpec((1, d, de), wsel),
                      pl.BlockSpec((1, de, d), wsel)],
            out_specs=pl.BlockSpec((EXPERT_BLOCK,) + tile, lambda i, blk, nu: (i, 0, 0)),
            scratch_shapes=[pltpu.VMEM((d, de), BF16), pltpu.VMEM((d, de), BF16),
                            pltpu.VMEM((de, d), BF16)],
        ),
        out_shape=jax.ShapeDtypeStruct(xs.shape, F32),
        compiler_params=pltpu.CompilerParams(vmem_limit_bytes=VMEM_LIMIT),
        name="experts",
    )(blk_e, n_used, xs, w1, w3, w2)


def _combine_kernel(dest_ref, ys_ref, x1_ref, gt2_ref, wts_ref, g_ref, o_ref, y0, y1, sem, *, n_tok, last):
    base = pl.program_id(0) * GATHER_TILE

    def copies(r, d0, d1):
        return (pltpu.make_async_copy(ys_ref.at[d0], y0.at[r], sem.at[0]),
                pltpu.make_async_copy(ys_ref.at[d1], y1.at[r], sem.at[1]))

    def issue(r, carry):
        for k, cp in enumerate(copies(r, dest_ref[base + r], dest_ref[n_tok + base + r])):
            cp.start(priority=k)
        return carry

    lax.fori_loop(0, GATHER_TILE, issue, 0, unroll=ISSUE_UNROLL)
    pltpu.make_async_copy(ys_ref.at[pl.ds(0, GATHER_TILE)], y0, sem.at[0]).wait()
    pltpu.make_async_copy(ys_ref.at[pl.ds(0, GATHER_TILE)], y1, sem.at[1]).wait()

    wts = wts_ref[...]
    ya = y0[...].reshape(x1_ref.shape)
    yb = y1[...].reshape(x1_ref.shape)
    x = x1_ref[...] + gt2_ref[0] * (wts[:, 0:1] * ya + wts[:, 1:2] * yb)
    if last:
        x = x * lax.rsqrt(jnp.mean(x * x, axis=-1, keepdims=True) + EPS) * g_ref[...]
    o_ref[...] = x


def _combine(dest, ys, x1, mod, wts, final_g, seq, last=True):
    n_tok, d = x1.shape
    tt = GATHER_TILE
    per_batch = seq // tt
    return pl.pallas_call(
        functools.partial(_combine_kernel, n_tok=n_tok, last=last),
        grid_spec=pltpu.PrefetchScalarGridSpec(
            num_scalar_prefetch=1,
            grid=(n_tok // tt,),
            in_specs=[pl.BlockSpec(memory_space=pl.ANY),
                      pl.BlockSpec((tt, d), lambda i, dest: (i, 0)),
                      pl.BlockSpec((1, 1, d), lambda i, dest: (i // per_batch, 0, 5)),
                      pl.BlockSpec((tt, 2), lambda i, dest: (i, 0)),
                      pl.BlockSpec((1, d), lambda i, dest: (0, 0))],
            out_specs=pl.BlockSpec((tt, d), lambda i, dest: (i, 0)),
            scratch_shapes=[pltpu.VMEM((tt,) + ys.shape[1:], F32), pltpu.VMEM((tt,) + ys.shape[1:], F32),
                            pltpu.SemaphoreType.DMA((2,))],
        ),
        out_shape=jax.ShapeDtypeStruct((n_tok, d), F32),
        name="combine",
    )(dest, ys, x1, mod, wts, final_g)


def kernel(x, c, w_ada, b_ada, norm1_g, w_in, lambda_q1, lambda_k1, lambda_q2, lambda_k2, da_subln_g, hg_lb_logits, hg_norm_g, w_up_da, w_up_hg, w_out, norm2_g, w_rg, b_rg, w_re, b_re, w1, w3, w2, final_g):
    b, s, d = x.shape
    n_tok = b * s
    depth = w_ada.shape[0]
    assert s % ROW_TILE == 0 and s % ATT_BLOCK == 0 and s % GATHER_TILE == 0
    assert (2 * n_tok) % LANES == 0 and d == SUBLANES * LANES

    c_pad = jnp.pad(c, ((0, (-b) % 8), (0, 0)))
    n_slots = 2 * n_tok + N_EXPERTS * EXPERT_BLOCK
    n_blk = n_slots // EXPERT_BLOCK
    n_blk_pad = -(-n_blk // LANES) * LANES

    for l in range(depth):
        lam_init = 0.8 - 0.6 * math.exp(-0.3 * l)
        mod = _adaln(c_pad, w_ada[l], b_ada[l][None, :])[:b].reshape(b, 1, 6 * d)

        (qda, kda, vda, qd, kd, ke, hi, sg, dec, gda, ghg) = _inproj(
            x, mod, norm1_g[l][None, :], w_in[l].astype(BF16), hg_lb_logits, l)
        yda = _attention(qda, kda, vda, lambda_q1[l][None, :], lambda_k1[l][None, :],
                         lambda_q2[l][None, :], lambda_k2[l][None, :], da_subln_g[l][None, :], lam_init)
        yhg = _hgrn(qd, kd, ke, hi, sg, dec, hg_norm_g[l][None, :])

        wr = jnp.concatenate([w_re[l].T, w_rg[l].T,
                              jnp.zeros((ROUTER_ROWS - N_EXPERTS - N_GROUPS, d), F32)], axis=0)
        wr_hi = wr.astype(BF16)
        wr_lo = (wr - wr_hi.astype(F32)).astype(BF16)
        br = jnp.concatenate([b_re[l], b_rg[l],
                              jnp.zeros((ROUTER_ROWS - N_EXPERTS - N_GROUPS,), F32)])[:, None]
        x1, h2, eidx, wts = _merge(yda, yhg, gda, ghg, x, mod, norm2_g[l][None, :],
                                   w_up_da[l].astype(BF16), w_up_hg[l].astype(BF16),
                                   w_out[l].astype(BF16), wr_hi, wr_lo, br)

        e_mat = jnp.transpose(eidx, (1, 0, 2)).reshape(2 * n_tok // LANES, LANES)
        dest, blk_e, n_used = _positions(e_mat, n_blk_pad)
        dest = dest.reshape(2 * n_tok)
        h2 = h2.reshape((n_tok,) + h2.shape[2:])
        xs = _dispatch(dest, h2, jnp.zeros((n_slots,) + h2.shape[1:], F32))
        ys = _experts(blk_e[0, :n_blk], n_used[0, :1], xs, w1[l], w3[l], w2[l])
        wts_tok = jnp.transpose(wts, (0, 2, 1)).reshape(n_tok, 2)
        x = _combine(dest, ys, x1.reshape(n_tok, d), mod, wts_tok, final_g[None, :], s,
                     last=(l == depth - 1)).reshape(b, s, d)
    return x
```

```python
import functools
import math

import jax
import jax.numpy as jnp
from jax import lax
from jax.experimental import pallas as pl
from jax.experimental.pallas import tpu as pltpu

F32 = jnp.float32
BF16 = jnp.bfloat16
I32 = jnp.int32

EPS = 1e-6
NEG = -1e30
LOG2E = math.log2(math.e)

CHUNK = 64
DA_HEADS = 4
DA_HEAD_DIM = 64
HG_HEADS = 4
HG_DK = 128
HG_DV = 128
N_GROUPS = 4
EXPERTS_PER_GROUP = 8
N_EXPERTS = N_GROUPS * EXPERTS_PER_GROUP

LANES = 128
SUBLANES = 8
ROW_TILE = 512
ATT_BLOCK = 512
EXPERT_BLOCK = 256
GATHER_TILE = 512
ISSUE_UNROLL = 8
ROUTER_ROWS = 40
VMEM_LIMIT = 52 * 1024 * 1024


def _nt(a, b):
    return lax.dot_general(a, b, (((1,), (1,)), ((), ())), preferred_element_type=F32)


def _nn(a, b):
    return jnp.dot(a, b, preferred_element_type=F32)


def _split(a):
    hi = a.astype(BF16)
    lo = (a - hi.astype(F32)).astype(BF16)
    return hi, lo


def _sigmoid(x):
    return 1.0 / (1.0 + jnp.exp(-x))


def _silu(x):
    return x * _sigmoid(x)


def _adaln_kernel(c_ref, w_ref, b_ref, o_ref):
    c_hi, c_lo = _split(_silu(c_ref[...]))
    w_hi, w_lo = _split(w_ref[...])
    o_ref[...] = _nn(c_hi, w_hi) + _nn(c_lo, w_hi) + _nn(c_hi, w_lo) + b_ref[...]


def _adaln(c_pad, w, b):
    rows, d = c_pad.shape
    n = w.shape[1]
    tn = 1024
    return pl.pallas_call(
        _adaln_kernel,
        grid=(n // tn,),
        in_specs=[
            pl.BlockSpec((rows, d), lambda j: (0, 0)),
            pl.BlockSpec((d, tn), lambda j: (0, j)),
            pl.BlockSpec((1, tn), lambda j: (0, j)),
        ],
        out_specs=pl.BlockSpec((rows, tn), lambda j: (0, j)),
        out_shape=jax.ShapeDtypeStruct((rows, n), F32),
        name="adaln",
    )(c_pad, w, b)


def _inproj_kernel(x_ref, sh_ref, sc_ref, g_ref, w_ref, lb_ref,
                   qda_ref, kda_ref, vda_ref, qd_ref, kd_ref, ke_ref, hi_ref, sg_ref,
                   dec_ref, gda_ref, ghg_ref, *, layer):
    x = x_ref[0]
    tm = x.shape[0]
    h = x * lax.rsqrt(jnp.mean(x * x, axis=-1, keepdims=True) + EPS) * g_ref[...]
    h = h * (1.0 + sc_ref[0]) + sh_ref[0]
    hb = h.astype(BF16)

    def proj(c0, width):
        return _nn(hb, w_ref[:, c0:c0 + width])

    w_da = DA_HEADS * 2 * DA_HEAD_DIM
    w_hg = HG_HEADS * HG_DK
    qda_ref[0] = (proj(0, w_da) * (LOG2E / math.sqrt(DA_HEAD_DIM))).astype(BF16)
    kda_ref[0] = proj(w_da, w_da).astype(BF16)
    vda_ref[0] = proj(2 * w_da, w_da).astype(BF16)
    c0 = 3 * w_da

    lbl = lb_ref[...]
    lbe = jnp.exp(lbl - jnp.max(lbl, axis=0, keepdims=True))
    lbs = lbe / jnp.sum(lbe, axis=0, keepdims=True)
    lb = jnp.sum(lbs[:layer + 1], axis=0, keepdims=True)

    f = lb + (1.0 - lb) * _sigmoid(proj(c0 + w_hg, w_hg))
    logf = jnp.log(f)
    kk = 1.0 - f
    r = lax.broadcasted_iota(I32, (tm, tm), 0)
    c = lax.broadcasted_iota(I32, (tm, tm), 1)
    tri = jnp.where((r >= c) & ((r // CHUNK) == (c // CHUNK)), 1.0, 0.0).astype(BF16)
    lf_hi, lf_lo = _split(logf)
    n_ch = tm // CHUNK
    a = (_nn(tri, lf_hi) + _nn(tri, lf_lo)).reshape(n_ch, CHUNK, w_hg)
    a_last = a[:, CHUNK - 1:CHUNK, :]
    kk3 = kk.reshape(n_ch, CHUNK, w_hg)
    qq3 = _silu(proj(c0, w_hg)).reshape(n_ch, CHUNK, w_hg)
    qd_ref[0] = (qq3 * jnp.exp(a)).reshape(tm, w_hg).astype(BF16)
    kd_ref[0] = (kk3 * jnp.exp(-a)).reshape(tm, w_hg).astype(BF16)
    ke_ref[0] = (kk3 * jnp.exp(a_last - a)).reshape(tm, w_hg).astype(BF16)
    dec_ref[0] = jnp.exp(a_last).reshape(n_ch, w_hg)
    hi_ref[0] = proj(c0 + 2 * w_hg, w_hg).astype(BF16)
    sg_ref[0] = _silu(proj(c0 + 3 * w_hg, w_hg)).astype(BF16)
    c1 = c0 + 4 * w_hg
    d = x.shape[1]
    gda_ref[0] = _sigmoid(proj(c1, d)).astype(BF16)
    ghg_ref[0] = _sigmoid(proj(c1 + d, d)).astype(BF16)


def _inproj(x, mod, g, w_bf, lb_logits, layer):
    b, s, d = x.shape
    tm = ROW_TILE
    n_cols = w_bf.shape[1]
    w_da = DA_HEADS * 2 * DA_HEAD_DIM
    w_hg = HG_HEADS * HG_DK

    def tok(width, dtype):
        return (pl.BlockSpec((1, tm, width), lambda bi, i: (bi, i, 0)),
                jax.ShapeDtypeStruct((b, s, width), dtype))

    outs = [tok(w_da, BF16)] * 3 + [tok(w_hg, BF16)] * 5
    outs.append((pl.BlockSpec((1, tm // CHUNK, w_hg), lambda bi, i: (bi, i, 0)),
                 jax.ShapeDtypeStruct((b, s // CHUNK, w_hg), F32)))
    outs += [tok(d, BF16)] * 2
    return pl.pallas_call(
        functools.partial(_inproj_kernel, layer=layer),
        grid=(b, s // tm),
        in_specs=[
            pl.BlockSpec((1, tm, d), lambda bi, i: (bi, i, 0)),
            pl.BlockSpec((1, 1, d), lambda bi, i: (bi, 0, 0)),
            pl.BlockSpec((1, 1, d), lambda bi, i: (bi, 0, 1)),
            pl.BlockSpec((1, d), lambda bi, i: (0, 0)),
            pl.BlockSpec((d, n_cols), lambda bi, i: (0, 0), pipeline_mode=pl.Buffered(1)),
            pl.BlockSpec(lb_logits.shape, lambda bi, i: (0, 0)),
        ],
        out_specs=[o[0] for o in outs],
        out_shape=[o[1] for o in outs],
        compiler_params=pltpu.CompilerParams(vmem_limit_bytes=VMEM_LIMIT),
        name="inproj",
    )(x, mod, mod, g, w_bf, lb_logits)


def _attn_kernel(q_ref, k_ref, v_ref, lq1_ref, lk1_ref, lq2_ref, lk2_ref, g_ref, o_ref,
                 kaug_ref, vaug_ref, qaug_ref, m_ref, acc_ref, sa_ref, sb_ref, *, lam_init, slopes):
    hd = pl.program_id(1)
    i = pl.program_id(2)
    tq = q_ref.shape[1]
    hw = q_ref.shape[2]
    n_key = k_ref.shape[1]

    slope = jnp.float32(slopes[-1])
    for idx in range(len(slopes) - 2, -1, -1):
        slope = jnp.where(hd == idx, jnp.float32(slopes[idx]), slope)

    @pl.when(i == 0)
    def _():
        kaug_ref[:hw, :] = k_ref[0].astype(F32).T.astype(BF16)
        jrel = (lax.broadcasted_iota(I32, (hw, n_key), 1) & (tq - 1)).astype(F32) * slope
        krow = lax.broadcasted_iota(I32, (hw, n_key), 0)
        j_hi = jrel.astype(BF16).astype(F32)
        j_mid = (jrel - j_hi).astype(BF16).astype(F32)
        kaug_ref[hw:, :] = jnp.where(
            krow == 0, j_hi,
            jnp.where(krow == 1, j_mid, jnp.where(krow == 2, jrel - j_hi - j_mid, 0.0))).astype(BF16)
        vaug_ref[:, :hw] = v_ref[0]
        vaug_ref[:, hw:] = jnp.ones((n_key, hw), BF16)

    q = q_ref[0]
    lane = lax.broadcasted_iota(I32, q.shape, 1)
    zero = jnp.zeros_like(q)
    ones2 = jnp.where(lane < 3, 1.0, 0.0).astype(BF16)
    qaug_ref[:tq, :hw] = jnp.where(lane < DA_HEAD_DIM, q, zero)
    qaug_ref[tq:, :hw] = jnp.where(lane >= DA_HEAD_DIM, q, zero)
    qaug_ref[:tq, hw:] = ones2
    qaug_ref[tq:, hw:] = ones2

    m_ref[...] = jnp.full(m_ref.shape, NEG, F32)
    acc_ref[...] = jnp.zeros(acc_ref.shape, F32)

    def scores(j, buf):
        buf[...] = _nn(qaug_ref[...], kaug_ref[:, pl.ds(pl.multiple_of(j * tq, tq), tq)])

    def consume(j, buf, diagonal):
        s = buf[...]
        if diagonal:
            ii = lax.broadcasted_iota(I32, s.shape, 0) & (tq - 1)
            jj = lax.broadcasted_iota(I32, s.shape, 1)
            ahead = jnp.minimum(ii - jj, 0).astype(F32)
            s = s + jnp.where((jj // CHUNK) <= (ii // CHUNK), (2.0 * slope) * ahead, NEG)
            shift = jnp.float32(0.0)
        else:
            shift = -slope * ((i - j) * tq).astype(F32)
        m_old = m_ref[...]
        m_new = jnp.maximum(m_old, jnp.max(s, axis=-1, keepdims=True) + shift)
        p = jnp.exp2(s - jnp.concatenate([m_new - shift] * (tq // LANES), axis=1))
        alpha = jnp.concatenate([jnp.exp2(m_old - m_new)] * (acc_ref.shape[1] // LANES), axis=1)
        vb = vaug_ref[pl.ds(pl.multiple_of(j * tq, tq), tq), :]
        acc_ref[...] = alpha * acc_ref[...] + _nn(p.astype(BF16), vb)
        m_ref[...] = m_new

    scores(0, sa_ref)

    def pair(pi, carry):
        j = 2 * pi
        scores(j + 1, sb_ref)
        consume(j, sa_ref, False)
        scores(j + 2, sa_ref)
        consume(j + 1, sb_ref, False)
        return carry

    lax.fori_loop(0, i // 2, pair, 0)

    @pl.when(i % 2 == 0)
    def _():
        consume(i, sa_ref, True)

    @pl.when(i % 2 == 1)
    def _():
        scores(i, sb_ref)
        consume(i - 1, sa_ref, False)
        consume(i, sb_ref, True)

    acc = acc_ref[...]
    o1 = acc[:tq, :hw] / acc[:tq, hw:]
    o2 = acc[tq:, :hw] / acc[tq:, hw:]
    lam = (jnp.exp(jnp.sum(lq1_ref[...] * lk1_ref[...], axis=-1, keepdims=True))
           - jnp.exp(jnp.sum(lq2_ref[...] * lk2_ref[...], axis=-1, keepdims=True)) + lam_init)
    o = o1 - lam * o2
    o = o * lax.rsqrt(jnp.mean(o * o, axis=-1, keepdims=True) + EPS) * g_ref[...]
    o_ref[0] = (o * (1.0 - lam_init)).astype(BF16)


def _attention(q, k, v, lq1, lk1, lq2, lk2, subln_g, lam_init):
    b, s, _ = q.shape
    tq = ATT_BLOCK
    hw = 2 * DA_HEAD_DIM
    assert hw == LANES
    slopes = tuple(LOG2E * 2.0 ** (-8.0 * (h + 1) / DA_HEADS) for h in range(DA_HEADS))
    vec = lambda n: pl.BlockSpec((1, n), lambda bi, h, i: (0, 0))
    return pl.pallas_call(
        functools.partial(_attn_kernel, lam_init=lam_init, slopes=slopes),
        grid=(b, DA_HEADS, s // tq),
        in_specs=[
            pl.BlockSpec((1, tq, hw), lambda bi, h, i: (bi, i, h)),
            pl.BlockSpec((1, s, hw), lambda bi, h, i: (bi, 0, h)),
            pl.BlockSpec((1, s, hw), lambda bi, h, i: (bi, 0, h)),
            vec(DA_HEAD_DIM), vec(DA_HEAD_DIM), vec(DA_HEAD_DIM), vec(DA_HEAD_DIM), vec(hw),
        ],
        out_specs=pl.BlockSpec((1, tq, hw), lambda bi, h, i: (bi, i, h)),
        out_shape=jax.ShapeDtypeStruct((b, s, DA_HEADS * hw), BF16),
        scratch_shapes=[
            pltpu.VMEM((2 * hw, s), BF16),
            pltpu.VMEM((s, 2 * hw), BF16),
            pltpu.VMEM((2 * tq, 2 * hw), BF16),
            pltpu.VMEM((2 * tq, LANES), F32),
            pltpu.VMEM((2 * tq, 2 * hw), F32),
            pltpu.VMEM((2 * tq, tq), F32),
            pltpu.VMEM((2 * tq, tq), F32),
        ],
        compiler_params=pltpu.CompilerParams(vmem_limit_bytes=VMEM_LIMIT),
        name="attention",
    )(q, k, v, lq1, lk1, lq2, lk2, subln_g)


def _hgrn_kernel(qd_ref, kd_ref, ke_ref, v_ref, sg_ref, dec_ref, g_ref, o_ref, st_ref):
    @pl.when(pl.program_id(1) == 0)
    def _():
        st_ref[...] = jnp.zeros(st_ref.shape, F32)

    n_ch = qd_ref.shape[1] // CHUNK
    r = lax.broadcasted_iota(I32, (CHUNK, CHUNK), 0)
    c = lax.broadcasted_iota(I32, (CHUNK, CHUNK), 1)
    causal = r >= c

    heads = [slice(h * HG_DK, (h + 1) * HG_DK) for h in range(HG_HEADS)]
    chunks = [slice(ci * CHUNK, (ci + 1) * CHUNK) for ci in range(n_ch)]
    incr = [[_nn(v_ref[0, rows, cols].astype(F32).T.astype(BF16), ke_ref[0, rows, cols])
             for rows in chunks] for cols in heads]
    start = []
    for h, cols in enumerate(heads):
        st = st_ref[h]
        before = []
        for ci in range(n_ch):
            before.append(st.astype(BF16))
            st = st * dec_ref[0, ci:ci + 1, cols] + incr[h][ci]
        st_ref[h] = st
        start.append(before)
    for h, cols in enumerate(heads):
        for ci, rows in enumerate(chunks):
            qd = qd_ref[0, rows, cols]
            scores = jnp.where(causal, _nt(qd, kd_ref[0, rows, cols]), 0.0).astype(BF16)
            o = _nn(scores, v_ref[0, rows, cols]) + _nt(qd, start[h][ci])
            o = o * lax.rsqrt(jnp.mean(o * o, axis=-1, keepdims=True) + EPS) * g_ref[...]
            o_ref[0, rows, cols] = (o * sg_ref[0, rows, cols].astype(F32)).astype(BF16)


def _hgrn(qd, kd, ke, v, sg, dec, norm_g):
    b, s, w = qd.shape
    ts = ROW_TILE
    tok = pl.BlockSpec((1, ts, w), lambda bi, i: (bi, i, 0))
    return pl.pallas_call(
        _hgrn_kernel,
        grid=(b, s // ts),
        in_specs=[tok, tok, tok, tok, tok,
                  pl.BlockSpec((1, ts // CHUNK, w), lambda bi, i: (bi, i, 0)),
                  pl.BlockSpec((1, HG_DV), lambda bi, i: (0, 0))],
        out_specs=tok,
        out_shape=jax.ShapeDtypeStruct((b, s, w), BF16),
        scratch_shapes=[pltpu.VMEM((HG_HEADS, HG_DV, HG_DK), F32)],
        name="hgrn",
    )(qd, kd, ke, v, sg, dec, norm_g)


def _merge_kernel(yda_ref, yhg_ref, gda_ref, ghg_ref, x_ref, gt1_ref, sh2_ref, sc2_ref, g2_ref,
                  wda_ref, whg_ref, wout_ref, wrh_ref, wrl_ref, br_ref,
                  x1_ref, h2_ref, eidx_ref, wts_ref):
    u = (gda_ref[0].astype(F32) * _nn(yda_ref[0], wda_ref[...])
         + ghg_ref[0].astype(F32) * _nn(yhg_ref[0], whg_ref[...]))
    x1 = x_ref[0] + gt1_ref[0] * _nn(u.astype(BF16), wout_ref[...])
    x1_ref[0] = x1
    h2 = x1 * lax.rsqrt(jnp.mean(x1 * x1, axis=-1, keepdims=True) + EPS) * g2_ref[...]
    h2 = h2 * (1.0 + sc2_ref[0]) + sh2_ref[0]
    h2_ref[0] = h2.reshape(h2.shape[0], SUBLANES, h2.shape[1] // SUBLANES)

    h_hi, h_lo = _split(h2)
    wrh = wrh_ref[...]
    lg = _nt(wrh, h_hi) + _nt(wrl_ref[...], h_hi) + _nt(wrh, h_lo) + br_ref[...]

    tm = h2.shape[0]
    eg = EXPERTS_PER_GROUP
    gl = lg[N_EXPERTS:N_EXPERTS + N_GROUPS]
    gmax = jnp.max(gl, axis=0, keepdims=True)
    g_p = 1.0 / jnp.sum(jnp.exp(gl - gmax), axis=0, keepdims=True)
    gi = lax.broadcasted_iota(I32, (N_GROUPS, tm), 0)
    g_idx = jnp.min(jnp.where(gl == gmax, gi, N_GROUPS), axis=0, keepdims=True)
    sel = lg[(N_GROUPS - 1) * eg:N_GROUPS * eg]
    for g in range(N_GROUPS - 2, -1, -1):
        sel = jnp.where(g_idx == g, lg[g * eg:(g + 1) * eg], sel)
    ei = lax.broadcasted_iota(I32, (eg, tm), 0)
    m1 = jnp.max(sel, axis=0, keepdims=True)
    i1 = jnp.min(jnp.where(sel == m1, ei, eg), axis=0, keepdims=True)
    sel2 = jnp.where(ei == i1, -jnp.inf, sel)
    m2 = jnp.max(sel2, axis=0, keepdims=True)
    i2 = jnp.min(jnp.where(sel2 == m2, ei, eg), axis=0, keepdims=True)
    t = jnp.exp(m2 - m1)
    wa = 1.0 / (1.0 + t)
    eidx_ref[0] = jnp.concatenate([g_idx * eg + i1, g_idx * eg + i2], axis=0)
    wts_ref[0] = jnp.concatenate([g_p * wa, g_p * (t * wa)], axis=0)


def _merge(yda, yhg, gda, ghg, x, mod, g2, wda, whg, wout, wrh, wrl, br):
    b, s, d = x.shape
    tm = ROW_TILE
    w = yda.shape[2]
    tok = lambda width: pl.BlockSpec((1, tm, width), lambda bi, i: (bi, i, 0))
    modspec = lambda k: pl.BlockSpec((1, 1, d), lambda bi, i: (bi, 0, k))
    full = lambda a: pl.BlockSpec(a.shape, lambda bi, i: (0,) * a.ndim)
    lane_rows = pl.BlockSpec((1, 2, tm), lambda bi, i: (bi, 0, i))
    return pl.pallas_call(
        _merge_kernel,
        grid=(b, s // tm),
        in_specs=[tok(w), tok(w), tok(d), tok(d), tok(d),
                  modspec(2), modspec(3), modspec(4), full(g2),
                  full(wda), full(whg), full(wout), full(wrh), full(wrl), full(br)],
        out_specs=[tok(d), pl.BlockSpec((1, tm, SUBLANES, d // SUBLANES), lambda bi, i: (bi, i, 0, 0)),
                   lane_rows, lane_rows],
        out_shape=[jax.ShapeDtypeStruct((b, s, d), F32),
                   jax.ShapeDtypeStruct((b, s, SUBLANES, d // SUBLANES), F32),
                   jax.ShapeDtypeStruct((b, 2, s), I32), jax.ShapeDtypeStruct((b, 2, s), F32)],
        compiler_params=pltpu.CompilerParams(vmem_limit_bytes=VMEM_LIMIT),
        name="merge",
    )(yda, yhg, gda, ghg, x, mod, mod, mod, g2, wda, whg, wout, wrh, wrl, br)


def _positions_kernel(e_ref, dest_ref, blk_ref, nused_ref):
    e_mat = e_ref[...]
    nr = e_mat.shape[0]
    nb = blk_ref.shape[1]
    r = lax.broadcasted_iota(I32, (LANES, LANES), 0)
    c = lax.broadcasted_iota(I32, (LANES, LANES), 1)
    incl = jnp.where(r <= c, 1.0, 0.0).astype(BF16)
    ones = jnp.ones((LANES, LANES), BF16)
    rr = lax.broadcasted_iota(I32, (nr, nr), 0)
    cc = lax.broadcasted_iota(I32, (nr, nr), 1)
    before = jnp.where(cc < rr, 1.0, 0.0).astype(BF16)
    blk_start = (lax.broadcasted_iota(I32, (1, nb), 1) * EXPERT_BLOCK).astype(F32)

    def per_expert(e, carry):
        pstart, dest, blk = carry
        hit = e_mat == e
        oh = jnp.where(hit, 1.0, 0.0).astype(BF16)
        in_row = _nn(oh, incl)
        row_tot = _nn(oh, ones)
        row_off = _nn(before, row_tot.astype(BF16))
        total = row_off[nr - 1:nr, :] + row_tot[nr - 1:nr, :]
        dest = jnp.where(hit, pstart + row_off + in_row - 1.0, dest)
        pend = pstart + jnp.floor((total + (EXPERT_BLOCK - 1)) * (1.0 / EXPERT_BLOCK)) * EXPERT_BLOCK
        blk = blk + jnp.where(blk_start >= pend[:, :1], 1.0, 0.0)
        return pend, dest, blk

    pstart0 = jnp.zeros((1, LANES), F32)
    pend, dest, blk = lax.fori_loop(
        0, N_EXPERTS, per_expert,
        (pstart0, jnp.zeros(e_mat.shape, F32), jnp.zeros((1, nb), F32)))
    dest_ref[...] = dest.astype(I32)
    blk_ref[...] = jnp.minimum(blk, N_EXPERTS - 1).astype(I32)
    nused_ref[...] = (pend * (1.0 / EXPERT_BLOCK)).astype(I32)


def _positions(e_mat, n_blk_pad):
    nr = e_mat.shape[0]
    return pl.pallas_call(
        _positions_kernel,
        out_shape=[jax.ShapeDtypeStruct((nr, LANES), I32),
                   jax.ShapeDtypeStruct((1, n_blk_pad), I32),
                   jax.ShapeDtypeStruct((1, LANES), I32)],
        name="positions",
    )(e_mat)


def _dispatch_kernel(dest_ref, h2_ref, xs_in_ref, xs_ref, sem, *, n_tok):
    del xs_in_ref
    base = pl.program_id(0) * GATHER_TILE

    def copies(r, d0, d1):
        return (pltpu.make_async_copy(h2_ref.at[r], xs_ref.at[d0], sem.at[0]),
                pltpu.make_async_copy(h2_ref.at[r], xs_ref.at[d1], sem.at[1]))

    def issue(r, carry):
        for k, cp in enumerate(copies(r, dest_ref[base + r], dest_ref[n_tok + base + r])):
            cp.start(priority=k)
        return carry

    lax.fori_loop(0, GATHER_TILE, issue, 0, unroll=ISSUE_UNROLL)
    for k in range(2):
        pltpu.make_async_copy(h2_ref, xs_ref.at[pl.ds(0, GATHER_TILE)], sem.at[k]).wait()


def _dispatch(dest, h2, xs_zero):
    n_tok = h2.shape[0]
    return pl.pallas_call(
        functools.partial(_dispatch_kernel, n_tok=n_tok),
        grid_spec=pltpu.PrefetchScalarGridSpec(
            num_scalar_prefetch=1,
            grid=(n_tok // GATHER_TILE,),
            in_specs=[pl.BlockSpec((GATHER_TILE,) + h2.shape[1:], lambda i, dest: (i, 0, 0)),
                      pl.BlockSpec(memory_space=pl.ANY)],
            out_specs=pl.BlockSpec(memory_space=pl.ANY),
            scratch_shapes=[pltpu.SemaphoreType.DMA((2,))],
        ),
        out_shape=jax.ShapeDtypeStruct(xs_zero.shape, xs_zero.dtype),
        input_output_aliases={2: 0},
        name="dispatch",
    )(dest, h2, xs_zero)


def _expert_kernel(blk_ref, nused_ref, xs_ref, w1_ref, w3_ref, w2_ref, ys_ref, w1b, w3b, w2b):
    i = pl.program_id(0)
    e = blk_ref[i]
    fresh = (i == 0) | (blk_ref[jnp.maximum(i - 1, 0)] != e)

    @pl.when(i < nused_ref[0])
    def _():
        @pl.when(fresh)
        def _():
            w1b[...] = w1_ref[0].astype(BF16)
            w3b[...] = w3_ref[0].astype(BF16)
            w2b[...] = w2_ref[0].astype(BF16)

        rows = xs_ref.shape[0]
        x = xs_ref[...].reshape(rows, w1b.shape[0]).astype(BF16)
        a = _nn(x, w1b[...])
        y = _nn((_silu(a) * _nn(x, w3b[...])).astype(BF16), w2b[...])
        ys_ref[...] = y.reshape(ys_ref.shape)

    @pl.when(i >= nused_ref[0])
    def _():
        ys_ref[...] = jnp.zeros(ys_ref.shape, F32)


def _experts(blk_e, n_used, xs, w1, w3, w2):
    p = xs.shape[0]
    tile = xs.shape[1:]
    d, de = w1.shape[1], w1.shape[2]
    rows = lambda i, blk, nu: (jnp.minimum(i, nu[0] - 1), 0, 0)
    wsel = lambda i, blk, nu: (blk[i], 0, 0)
    return pl.pallas_call(
        _expert_kernel,
        grid_spec=pltpu.PrefetchScalarGridSpec(
            num_scalar_prefetch=2,
            grid=(p // EXPERT_BLOCK,),
            in_specs=[pl.BlockSpec((EXPERT_BLOCK,) + tile, rows),
                      pl.BlockSpec((1, d, de), wsel),
                      pl.BlockSpec((1, d, de), wsel),
                      pl.BlockSpec((1, de, d), wsel)],
            out_specs=pl.BlockSpec((EXPERT_BLOCK,) + tile, lambda i, blk, nu: (i, 0, 0)),
            scratch_shapes=[pltpu.VMEM((d, de), BF16), pltpu.VMEM((d, de), BF16),
                            pltpu.VMEM((de, d), BF16)],
        ),
        out_shape=jax.ShapeDtypeStruct(xs.shape, F32),
        compiler_params=pltpu.CompilerParams(vmem_limit_bytes=VMEM_LIMIT),
        name="experts",
    )(blk_e, n_used, xs, w1, w3, w2)


def _combine_kernel(dest_ref, ys_ref, x1_ref, gt2_ref, wts_ref, g_ref, o_ref, y0, y1, sem, *, n_tok, last):
    base = pl.program_id(0) * GATHER_TILE

    def copies(r, d0, d1):
        return (pltpu.make_async_copy(ys_ref.at[d0], y0.at[r], sem.at[0]),
                pltpu.make_async_copy(ys_ref.at[d1], y1.at[r], sem.at[1]))

    def issue(r, carry):
        for k, cp in enumerate(copies(r, dest_ref[base + r], dest_ref[n_tok + base + r])):
            cp.start(priority=k)
        return carry

    lax.fori_loop(0, GATHER_TILE, issue, 0, unroll=ISSUE_UNROLL)
    pltpu.make_async_copy(ys_ref.at[pl.ds(0, GATHER_TILE)], y0, sem.at[0]).wait()
    pltpu.make_async_copy(ys_ref.at[pl.ds(0, GATHER_TILE)], y1, sem.at[1]).wait()

    wts = wts_ref[...]
    ya = y0[...].reshape(x1_ref.shape)
    yb = y1[...].reshape(x1_ref.shape)
    x = x1_ref[...] + gt2_ref[0] * (wts[:, 0:1] * ya + wts[:, 1:2] * yb)
    if last:
        x = x * lax.rsqrt(jnp.mean(x * x, axis=-1, keepdims=True) + EPS) * g_ref[...]
    o_ref[...] = x


def _combine(dest, ys, x1, mod, wts, final_g, seq, last=True):
    n_tok, d = x1.shape
    tt = GATHER_TILE
    per_batch = seq // tt
    return pl.pallas_call(
        functools.partial(_combine_kernel, n_tok=n_tok, last=last),
        grid_spec=pltpu.PrefetchScalarGridSpec(
            num_scalar_prefetch=1,
            grid=(n_tok // tt,),
            in_specs=[pl.BlockSpec(memory_space=pl.ANY),
                      pl.BlockSpec((tt, d), lambda i, dest: (i, 0)),
                      pl.BlockSpec((1, 1, d), lambda i, dest: (i // per_batch, 0, 5)),
                      pl.BlockSpec((tt, 2), lambda i, dest: (i, 0)),
                      pl.BlockSpec((1, d), lambda i, dest: (0, 0))],
            out_specs=pl.BlockSpec((tt, d), lambda i, dest: (i, 0)),
            scratch_shapes=[pltpu.VMEM((tt,) + ys.shape[1:], F32), pltpu.VMEM((tt,) + ys.shape[1:], F32),
                            pltpu.SemaphoreType.DMA((2,))],
        ),
        out_shape=jax.ShapeDtypeStruct((n_tok, d), F32),
        name="combine",
    )(dest, ys, x1, mod, wts, final_g)


def kernel(x, c, w_ada, b_ada, norm1_g, w_in, lambda_q1, lambda_k1, lambda_q2, lambda_k2, da_subln_g, hg_lb_logits, hg_norm_g, w_up_da, w_up_hg, w_out, norm2_g, w_rg, b_rg, w_re, b_re, w1, w3, w2, final_g):
    b, s, d = x.shape
    n_tok = b * s
    depth = w_ada.shape[0]
    assert s % ROW_TILE == 0 and s % ATT_BLOCK == 0 and s % GATHER_TILE == 0
    assert (2 * n_tok) % LANES == 0 and d == SUBLANES * LANES

    c_pad = jnp.pad(c, ((0, (-b) % 8), (0, 0)))
    n_slots = 2 * n_tok + N_EXPERTS * EXPERT_BLOCK
    n_blk = n_slots // EXPERT_BLOCK
    n_blk_pad = -(-n_blk // LANES) * LANES

    for l in range(depth):
        lam_init = 0.8 - 0.6 * math.exp(-0.3 * l)
        mod = _adaln(c_pad, w_ada[l], b_ada[l][None, :])[:b].reshape(b, 1, 6 * d)

        (qda, kda, vda, qd, kd, ke, hi, sg, dec, gda, ghg) = _inproj(
            x, mod, norm1_g[l][None, :], w_in[l].astype(BF16), hg_lb_logits, l)
        yda = _attention(qda, kda, vda, lambda_q1[l][None, :], lambda_k1[l][None, :],
                         lambda_q2[l][None, :], lambda_k2[l][None, :], da_subln_g[l][None, :], lam_init)
        yhg = _hgrn(qd, kd, ke, hi, sg, dec, hg_norm_g[l][None, :])

        wr = jnp.concatenate([w_re[l].T, w_rg[l].T,
                              jnp.zeros((ROUTER_ROWS - N_EXPERTS - N_GROUPS, d), F32)], axis=0)
        wr_hi = wr.astype(BF16)
        wr_lo = (wr - wr_hi.astype(F32)).astype(BF16)
        br = jnp.concatenate([b_re[l], b_rg[l],
                              jnp.zeros((ROUTER_ROWS - N_EXPERTS - N_GROUPS,), F32)])[:, None]
        x1, h2, eidx, wts = _merge(yda, yhg, gda, ghg, x, mod, norm2_g[l][None, :],
                                   w_up_da[l].astype(BF16), w_up_hg[l].astype(BF16),
                                   w_out[l].astype(BF16), wr_hi, wr_lo, br)

        e_mat = jnp.transpose(eidx, (1, 0, 2)).reshape(2 * n_tok // LANES, LANES)
        dest, blk_e, n_used = _positions(e_mat, n_blk_pad)
        dest = dest.reshape(2 * n_tok)
        h2 = h2.reshape((n_tok,) + h2.shape[2:])
        xs = _dispatch(dest, h2, jnp.zeros((n_slots,) + h2.shape[1:], F32))
        ys = _experts(blk_e[0, :n_blk], n_used[0, :1], xs, w1[l], w3[l], w2[l])
        wts_tok = jnp.transpose(wts, (0, 2, 1)).reshape(n_tok, 2)
        x = _combine(dest, ys, x1.reshape(n_tok, d), mod, wts_tok, final_g[None, :], s,
                     last=(l == depth - 1)).reshape(b, s, d)
    return x
```

```python
import functools
import math

import jax
import jax.numpy as jnp
from jax import lax
from jax.experimental import pallas as pl
from jax.experimental.pallas import tpu as pltpu

F32 = jnp.float32
BF16 = jnp.bfloat16
I32 = jnp.int32

EPS = 1e-6
NEG = -1e30
LOG2E = math.log2(math.e)

CHUNK = 64
DA_HEADS = 4
DA_HEAD_DIM = 64
HG_HEADS = 4
HG_DK = 128
HG_DV = 128
N_GROUPS = 4
EXPERTS_PER_GROUP = 8
N_EXPERTS = N_GROUPS * EXPERTS_PER_GROUP

LANES = 128
SUBLANES = 8
ROW_TILE = 512
ATT_BLOCK = 512
EXPERT_BLOCK = 256
GATHER_TILE = 512
ISSUE_UNROLL = 8
ROUTER_ROWS = 40
VMEM_LIMIT = 52 * 1024 * 1024


def _nt(a, b):
    return lax.dot_general(a, b, (((1,), (1,)), ((), ())), preferred_element_type=F32)


def _nn(a, b):
    return jnp.dot(a, b, preferred_element_type=F32)


def _split(a):
    hi = a.astype(BF16)
    lo = (a - hi.astype(F32)).astype(BF16)
    return hi, lo


def _sigmoid(x):
    return 1.0 / (1.0 + jnp.exp(-x))


def _silu(x):
    return x * _sigmoid(x)


def _adaln_kernel(c_ref, w_ref, b_ref, o_ref):
    c_hi, c_lo = _split(_silu(c_ref[...]))
    w_hi, w_lo = _split(w_ref[...])
    o_ref[...] = _nn(c_hi, w_hi) + _nn(c_lo, w_hi) + _nn(c_hi, w_lo) + b_ref[...]


def _adaln(c_pad, w, b):
    rows, d = c_pad.shape
    n = w.shape[1]
    tn = 1024
    return pl.pallas_call(
        _adaln_kernel,
        grid=(n // tn,),
        in_specs=[
            pl.BlockSpec((rows, d), lambda j: (0, 0)),
            pl.BlockSpec((d, tn), lambda j: (0, j)),
            pl.BlockSpec((1, tn), lambda j: (0, j)),
        ],
        out_specs=pl.BlockSpec((rows, tn), lambda j: (0, j)),
        out_shape=jax.ShapeDtypeStruct((rows, n), F32),
        name="adaln",
    )(c_pad, w, b)


def _inproj_kernel(x_ref, sh_ref, sc_ref, g_ref, w_ref, lb_ref,
                   qda_ref, kda_ref, vda_ref, qd_ref, kd_ref, ke_ref, hi_ref, sg_ref,
                   dec_ref, gda_ref, ghg_ref, *, layer):
    x = x_ref[0]
    tm = x.shape[0]
    h = x * lax.rsqrt(jnp.mean(x * x, axis=-1, keepdims=True) + EPS) * g_ref[...]
    h = h * (1.0 + sc_ref[0]) + sh_ref[0]
    hb = h.astype(BF16)

    def proj(c0, width):
        return _nn(hb, w_ref[:, c0:c0 + width])

    w_da = DA_HEADS * 2 * DA_HEAD_DIM
    w_hg = HG_HEADS * HG_DK
    qda_ref[0] = (proj(0, w_da) * (LOG2E / math.sqrt(DA_HEAD_DIM))).astype(BF16)
    kda_ref[0] = proj(w_da, w_da).astype(BF16)
    vda_ref[0] = proj(2 * w_da, w_da).astype(BF16)
    c0 = 3 * w_da

    lbl = lb_ref[...]
    lbe = jnp.exp(lbl - jnp.max(lbl, axis=0, keepdims=True))
    lbs = lbe / jnp.sum(lbe, axis=0, keepdims=True)
    lb = jnp.sum(lbs[:layer + 1], axis=0, keepdims=True)

    f = lb + (1.0 - lb) * _sigmoid(proj(c0 + w_hg, w_hg))
    logf = jnp.log(f)
    kk = 1.0 - f
    r = lax.broadcasted_iota(I32, (tm, tm), 0)
    c = lax.broadcasted_iota(I32, (tm, tm), 1)
    tri = jnp.where((r >= c) & ((r // CHUNK) == (c // CHUNK)), 1.0, 0.0).astype(BF16)
    lf_hi, lf_lo = _split(logf)
    n_ch = tm // CHUNK
    a = (_nn(tri, lf_hi) + _nn(tri, lf_lo)).reshape(n_ch, CHUNK, w_hg)
    a_last = a[:, CHUNK - 1:CHUNK, :]
    kk3 = kk.reshape(n_ch, CHUNK, w_hg)
    qq3 = _silu(proj(c0, w_hg)).reshape(n_ch, CHUNK, w_hg)
    qd_ref[0] = (qq3 * jnp.exp(a)).reshape(tm, w_hg).astype(BF16)
    kd_ref[0] = (kk3 * jnp.exp(-a)).reshape(tm, w_hg).astype(BF16)
    ke_ref[0] = (kk3 * jnp.exp(a_last - a)).reshape(tm, w_hg).astype(BF16)
    dec_ref[0] = jnp.exp(a_last).reshape(n_ch, w_hg)
    hi_ref[0] = proj(c0 + 2 * w_hg, w_hg).astype(BF16)
    sg_ref[0] = _silu(proj(c0 + 3 * w_hg, w_hg)).astype(BF16)
    c1 = c0 + 4 * w_hg
    d = x.shape[1]
    gda_ref[0] = _sigmoid(proj(c1, d)).astype(BF16)
    ghg_ref[0] = _sigmoid(proj(c1 + d, d)).astype(BF16)


def _inproj(x, mod, g, w_bf, lb_logits, layer):
    b, s, d = x.shape
    tm = ROW_TILE
    n_cols = w_bf.shape[1]
    w_da = DA_HEADS * 2 * DA_HEAD_DIM
    w_hg = HG_HEADS * HG_DK

    def tok(width, dtype):
        return (pl.BlockSpec((1, tm, width), lambda bi, i: (bi, i, 0)),
                jax.ShapeDtypeStruct((b, s, width), dtype))

    outs = [tok(w_da, BF16)] * 3 + [tok(w_hg, BF16)] * 5
    outs.append((pl.BlockSpec((1, tm // CHUNK, w_hg), lambda bi, i: (bi, i, 0)),
                 jax.ShapeDtypeStruct((b, s // CHUNK, w_hg), F32)))
    outs += [tok(d, BF16)] * 2
    return pl.pallas_call(
        functools.partial(_inproj_kernel, layer=layer),
        grid=(b, s // tm),
        in_specs=[
            pl.BlockSpec((1, tm, d), lambda bi, i: (bi, i, 0)),
            pl.BlockSpec((1, 1, d), lambda bi, i: (bi, 0, 0)),
            pl.BlockSpec((1, 1, d), lambda bi, i: (bi, 0, 1)),
            pl.BlockSpec((1, d), lambda bi, i: (0, 0)),
            pl.BlockSpec((d, n_cols), lambda bi, i: (0, 0), pipeline_mode=pl.Buffered(1)),
            pl.BlockSpec(lb_logits.shape, lambda bi, i: (0, 0)),
        ],
        out_specs=[o[0] for o in outs],
        out_shape=[o[1] for o in outs],
        compiler_params=pltpu.CompilerParams(vmem_limit_bytes=VMEM_LIMIT),
        name="inproj",
    )(x, mod, mod, g, w_bf, lb_logits)


def _attn_kernel(q_ref, k_ref, v_ref, lq1_ref, lk1_ref, lq2_ref, lk2_ref, g_ref, o_ref,
                 kaug_ref, vaug_ref, qaug_ref, m_ref, acc_ref, sa_ref, sb_ref, *, lam_init, slopes):
    hd = pl.program_id(1)
    i = pl.program_id(2)
    tq = q_ref.shape[1]
    hw = q_ref.shape[2]
    n_key = k_ref.shape[1]

    slope = jnp.float32(slopes[-1])
    for idx in range(len(slopes) - 2, -1, -1):
        slope = jnp.where(hd == idx, jnp.float32(slopes[idx]), slope)

    @pl.when(i == 0)
    def _():
        kaug_ref[:hw, :] = k_ref[0].astype(F32).T.astype(BF16)
        jrel = (lax.broadcasted_iota(I32, (hw, n_key), 1) & (tq - 1)).astype(F32) * slope
        krow = lax.broadcasted_iota(I32, (hw, n_key), 0)
        j_hi = jrel.astype(BF16).astype(F32)
        j_mid = (jrel - j_hi).astype(BF16).astype(F32)
        kaug_ref[hw:, :] = jnp.where(
            krow == 0, j_hi,
            jnp.where(krow == 1, j_mid, jnp.where(krow == 2, jrel - j_hi - j_mid, 0.0))).astype(BF16)
        vaug_ref[:, :hw] = v_ref[0]
        vaug_ref[:, hw:] = jnp.ones((n_key, hw), BF16)

    q = q_ref[0]
    lane = lax.broadcasted_iota(I32, q.shape, 1)
    zero = jnp.zeros_like(q)
    ones2 = jnp.where(lane < 3, 1.0, 0.0).astype(BF16)
    qaug_ref[:tq, :hw] = jnp.where(lane < DA_HEAD_DIM, q, zero)
    qaug_ref[tq:, :hw] = jnp.where(lane >= DA_HEAD_DIM, q, zero)
    qaug_ref[:tq, hw:] = ones2
    qaug_ref[tq:, hw:] = ones2

    m_ref[...] = jnp.full(m_ref.shape, NEG, F32)
    acc_ref[...] = jnp.zeros(acc_ref.shape, F32)

    def scores(j, buf):
        buf[...] = _nn(qaug_ref[...], kaug_ref[:, pl.ds(pl.multiple_of(j * tq, tq), tq)])

    def consume(j, buf, diagonal):
        s = buf[...]
        if diagonal:
            ii = lax.broadcasted_iota(I32, s.shape, 0) & (tq - 1)
            jj = lax.broadcasted_iota(I32, s.shape, 1)
            ahead = jnp.minimum(ii - jj, 0).astype(F32)
            s = s + jnp.where((jj // CHUNK) <= (ii // CHUNK), (2.0 * slope) * ahead, NEG)
            shift = jnp.float32(0.0)
        else:
            shift = -slope * ((i - j) * tq).astype(F32)
        m_old = m_ref[...]
        m_new = jnp.maximum(m_old, jnp.max(s, axis=-1, keepdims=True) + shift)
        p = jnp.exp2(s - jnp.concatenate([m_new - shift] * (tq // LANES), axis=1))
        alpha = jnp.concatenate([jnp.exp2(m_old - m_new)] * (acc_ref.shape[1] // LANES), axis=1)
        vb = vaug_ref[pl.ds(pl.multiple_of(j * tq, tq), tq), :]
        acc_ref[...] = alpha * acc_ref[...] + _nn(p.astype(BF16), vb)
        m_ref[...] = m_new

    scores(0, sa_ref)

    def pair(pi, carry):
        j = 2 * pi
        scores(j + 1, sb_ref)
        consume(j, sa_ref, False)
        scores(j + 2, sa_ref)
        consume(j + 1, sb_ref, False)
        return carry

    lax.fori_loop(0, i // 2, pair, 0)

    @pl.when(i % 2 == 0)
    def _():
        consume(i, sa_ref, True)

    @pl.when(i % 2 == 1)
    def _():
        scores(i, sb_ref)
        consume(i - 1, sa_ref, False)
        consume(i, sb_ref, True)

    acc = acc_ref[...]
    o1 = acc[:tq, :hw] / acc[:tq, hw:]
    o2 = acc[tq:, :hw] / acc[tq:, hw:]
    lam = (jnp.exp(jnp.sum(lq1_ref[...] * lk1_ref[...], axis=-1, keepdims=True))
           - jnp.exp(jnp.sum(lq2_ref[...] * lk2_ref[...], axis=-1, keepdims=True)) + lam_init)
    o = o1 - lam * o2
    o = o * lax.rsqrt(jnp.mean(o * o, axis=-1, keepdims=True) + EPS) * g_ref[...]
    o_ref[0] = (o * (1.0 - lam_init)).astype(BF16)


def _attention(q, k, v, lq1, lk1, lq2, lk2, subln_g, lam_init):
    b, s, _ = q.shape
    tq = ATT_BLOCK
    hw = 2 * DA_HEAD_DIM
    assert hw == LANES
    slopes = tuple(LOG2E * 2.0 ** (-8.0 * (h + 1) / DA_HEADS) for h in range(DA_HEADS))
    vec = lambda n: pl.BlockSpec((1, n), lambda bi, h, i: (0, 0))
    return pl.pallas_call(
        functools.partial(_attn_kernel, lam_init=lam_init, slopes=slopes),
        grid=(b, DA_HEADS, s // tq),
        in_specs=[
            pl.BlockSpec((1, tq, hw), lambda bi, h, i: (bi, i, h)),
            pl.BlockSpec((1, s, hw), lambda bi, h, i: (bi, 0, h)),
            pl.BlockSpec((1, s, hw), lambda bi, h, i: (bi, 0, h)),
            vec(DA_HEAD_DIM), vec(DA_HEAD_DIM), vec(DA_HEAD_DIM), vec(DA_HEAD_DIM), vec(hw),
        ],
        out_specs=pl.BlockSpec((1, tq, hw), lambda bi, h, i: (bi, i, h)),
        out_shape=jax.ShapeDtypeStruct((b, s, DA_HEADS * hw), BF16),
        scratch_shapes=[
            pltpu.VMEM((2 * hw, s), BF16),
            pltpu.VMEM((s, 2 * hw), BF16),
            pltpu.VMEM((2 * tq, 2 * hw), BF16),
            pltpu.VMEM((2 * tq, LANES), F32),
            pltpu.VMEM((2 * tq, 2 * hw), F32),
            pltpu.VMEM((2 * tq, tq), F32),
            pltpu.VMEM((2 * tq, tq), F32),
        ],
        compiler_params=pltpu.CompilerParams(vmem_limit_bytes=VMEM_LIMIT),
        name="attention",
    )(q, k, v, lq1, lk1, lq2, lk2, subln_g)


def _hgrn_kernel(qd_ref, kd_ref, ke_ref, v_ref, sg_ref, dec_ref, g_ref, o_ref, st_ref):
    @pl.when(pl.program_id(1) == 0)
    def _():
        st_ref[...] = jnp.zeros(st_ref.shape, F32)

    n_ch = qd_ref.shape[1] // CHUNK
    r = lax.broadcasted_iota(I32, (CHUNK, CHUNK), 0)
    c = lax.broadcasted_iota(I32, (CHUNK, CHUNK), 1)
    causal = r >= c

    heads = [slice(h * HG_DK, (h + 1) * HG_DK) for h in range(HG_HEADS)]
    chunks = [slice(ci * CHUNK, (ci + 1) * CHUNK) for ci in range(n_ch)]
    incr = [[_nn(v_ref[0, rows, cols].astype(F32).T.astype(BF16), ke_ref[0, rows, cols])
             for rows in chunks] for cols in heads]
    start = []
    for h, cols in enumerate(heads):
        st = st_ref[h]
        before = []
        for ci in range(n_ch):
            before.append(st.astype(BF16))
            st = st * dec_ref[0, ci:ci + 1, cols] + incr[h][ci]
        st_ref[h] = st
        start.append(before)
    for h, cols in enumerate(heads):
        for ci, rows in enumerate(chunks):
            qd = qd_ref[0, rows, cols]
            scores = jnp.where(causal, _nt(qd, kd_ref[0, rows, cols]), 0.0).astype(BF16)
            o = _nn(scores, v_ref[0, rows, cols]) + _nt(qd, start[h][ci])
            o = o * lax.rsqrt(jnp.mean(o * o, axis=-1, keepdims=True) + EPS) * g_ref[...]
            o_ref[0, rows, cols] = (o * sg_ref[0, rows, cols].astype(F32)).astype(BF16)


def _hgrn(qd, kd, ke, v, sg, dec, norm_g):
    b, s, w = qd.shape
    ts = ROW_TILE
    tok = pl.BlockSpec((1, ts, w), lambda bi, i: (bi, i, 0))
    return pl.pallas_call(
        _hgrn_kernel,
        grid=(b, s // ts),
        in_specs=[tok, tok, tok, tok, tok,
                  pl.BlockSpec((1, ts // CHUNK, w), lambda bi, i: (bi, i, 0)),
                  pl.BlockSpec((1, HG_DV), lambda bi, i: (0, 0))],
        out_specs=tok,
        out_shape=jax.ShapeDtypeStruct((b, s, w), BF16),
        scratch_shapes=[pltpu.VMEM((HG_HEADS, HG_DV, HG_DK), F32)],
        name="hgrn",
    )(qd, kd, ke, v, sg, dec, norm_g)


def _merge_kernel(yda_ref, yhg_ref, gda_ref, ghg_ref, x_ref, gt1_ref, sh2_ref, sc2_ref, g2_ref,
                  wda_ref, whg_ref, wout_ref, wrh_ref, wrl_ref, br_ref,
                  x1_ref, h2_ref, eidx_ref, wts_ref):
    u = (gda_ref[0].astype(F32) * _nn(yda_ref[0], wda_ref[...])
         + ghg_ref[0].astype(F32) * _nn(yhg_ref[0], whg_ref[...]))
    x1 = x_ref[0] + gt1_ref[0] * _nn(u.astype(BF16), wout_ref[...])
    x1_ref[0] = x1
    h2 = x1 * lax.rsqrt(jnp.mean(x1 * x1, axis=-1, keepdims=True) + EPS) * g2_ref[...]
    h2 = h2 * (1.0 + sc2_ref[0]) + sh2_ref[0]
    h2_ref[0] = h2.reshape(h2.shape[0], SUBLANES, h2.shape[1] // SUBLANES)

    h_hi, h_lo = _split(h2)
    wrh = wrh_ref[...]
    lg = _nt(wrh, h_hi) + _nt(wrl_ref[...], h_hi) + _nt(wrh, h_lo) + br_ref[...]

    tm = h2.shape[0]
    eg = EXPERTS_PER_GROUP
    gl = lg[N_EXPERTS:N_EXPERTS + N_GROUPS]
    gmax = jnp.max(gl, axis=0, keepdims=True)
    g_p = 1.0 / jnp.sum(jnp.exp(gl - gmax), axis=0, keepdims=True)
    gi = lax.broadcasted_iota(I32, (N_GROUPS, tm), 0)
    g_idx = jnp.min(jnp.where(gl == gmax, gi, N_GROUPS), axis=0, keepdims=True)
    sel = lg[(N_GROUPS - 1) * eg:N_GROUPS * eg]
    for g in range(N_GROUPS - 2, -1, -1):
        sel = jnp.where(g_idx == g, lg[g * eg:(g + 1) * eg], sel)
    ei = lax.broadcasted_iota(I32, (eg, tm), 0)
    m1 = jnp.max(sel, axis=0, keepdims=True)
    i1 = jnp.min(jnp.where(sel == m1, ei, eg), axis=0, keepdims=True)
    sel2 = jnp.where(ei == i1, -jnp.inf, sel)
    m2 = jnp.max(sel2, axis=0, keepdims=True)
    i2 = jnp.min(jnp.where(sel2 == m2, ei, eg), axis=0, keepdims=True)
    t = jnp.exp(m2 - m1)
    wa = 1.0 / (1.0 + t)
    eidx_ref[0] = jnp.concatenate([g_idx * eg + i1, g_idx * eg + i2], axis=0)
    wts_ref[0] = jnp.concatenate([g_p * wa, g_p * (t * wa)], axis=0)


def _merge(yda, yhg, gda, ghg, x, mod, g2, wda, whg, wout, wrh, wrl, br):
    b, s, d = x.shape
    tm = ROW_TILE
    w = yda.shape[2]
    tok = lambda width: pl.BlockSpec((1, tm, width), lambda bi, i: (bi, i, 0))
    modspec = lambda k: pl.BlockSpec((1, 1, d), lambda bi, i: (bi, 0, k))
    full = lambda a: pl.BlockSpec(a.shape, lambda bi, i: (0,) * a.ndim)
    lane_rows = pl.BlockSpec((1, 2, tm), lambda bi, i: (bi, 0, i))
    return pl.pallas_call(
        _merge_kernel,
        grid=(b, s // tm),
        in_specs=[tok(w), tok(w), tok(d), tok(d), tok(d),
                  modspec(2), modspec(3), modspec(4), full(g2),
                  full(wda), full(whg), full(wout), full(wrh), full(wrl), full(br)],
        out_specs=[tok(d), pl.BlockSpec((1, tm, SUBLANES, d // SUBLANES), lambda bi, i: (bi, i, 0, 0)),
                   lane_rows, lane_rows],
        out_shape=[jax.ShapeDtypeStruct((b, s, d), F32),
                   jax.ShapeDtypeStruct((b, s, SUBLANES, d // SUBLANES), F32),
                   jax.ShapeDtypeStruct((b, 2, s), I32), jax.ShapeDtypeStruct((b, 2, s), F32)],
        compiler_params=pltpu.CompilerParams(vmem_limit_bytes=VMEM_LIMIT),
        name="merge",
    )(yda, yhg, gda, ghg, x, mod, mod, mod, g2, wda, whg, wout, wrh, wrl, br)


def _positions_kernel(e_ref, dest_ref, blk_ref, nused_ref):
    e_mat = e_ref[...]
    nr = e_mat.shape[0]
    nb = blk_ref.shape[1]
    r = lax.broadcasted_iota(I32, (LANES, LANES), 0)
    c = lax.broadcasted_iota(I32, (LANES, LANES), 1)
    incl = jnp.where(r <= c, 1.0, 0.0).astype(BF16)
    ones = jnp.ones((LANES, LANES), BF16)
    rr = lax.broadcasted_iota(I32, (nr, nr), 0)
    cc = lax.broadcasted_iota(I32, (nr, nr), 1)
    before = jnp.where(cc < rr, 1.0, 0.0).astype(BF16)
    blk_start = (lax.broadcasted_iota(I32, (1, nb), 1) * EXPERT_BLOCK).astype(F32)

    def per_expert(e, carry):
        pstart, dest, blk = carry
        hit = e_mat == e
        oh = jnp.where(hit, 1.0, 0.0).astype(BF16)
        in_row = _nn(oh, incl)
        row_tot = _nn(oh, ones)
        row_off = _nn(before, row_tot.astype(BF16))
        total = row_off[nr - 1:nr, :] + row_tot[nr - 1:nr, :]
        dest = jnp.where(hit, pstart + row_off + in_row - 1.0, dest)
        pend = pstart + jnp.floor((total + (EXPERT_BLOCK - 1)) * (1.0 / EXPERT_BLOCK)) * EXPERT_BLOCK
        blk = blk + jnp.where(blk_start >= pend[:, :1], 1.0, 0.0)
        return pend, dest, blk

    pstart0 = jnp.zeros((1, LANES), F32)
    pend, dest, blk = lax.fori_loop(
        0, N_EXPERTS, per_expert,
        (pstart0, jnp.zeros(e_mat.shape, F32), jnp.zeros((1, nb), F32)))
    dest_ref[...] = dest.astype(I32)
    blk_ref[...] = jnp.minimum(blk, N_EXPERTS - 1).astype(I32)
    nused_ref[...] = (pend * (1.0 / EXPERT_BLOCK)).astype(I32)


def _positions(e_mat, n_blk_pad):
    nr = e_mat.shape[0]
    return pl.pallas_call(
        _positions_kernel,
        out_shape=[jax.ShapeDtypeStruct((nr, LANES), I32),
                   jax.ShapeDtypeStruct((1, n_blk_pad), I32),
                   jax.ShapeDtypeStruct((1, LANES), I32)],
        name="positions",
    )(e_mat)


def _dispatch_kernel(dest_ref, h2_ref, xs_in_ref, xs_ref, hbuf, in_sem, out_sem, *, n_tok):
    del xs_in_ref
    i = pl.program_id(0)
    last = pl.num_programs(0) - 1
    slot = i % 2

    def load(step, buf):
        return pltpu.make_async_copy(h2_ref.at[pl.ds(step * GATHER_TILE, GATHER_TILE)], hbuf.at[buf],
                                     in_sem.at[buf])

    def drain(buf):
        for k in range(2):
            pltpu.make_async_copy(hbuf.at[buf], xs_ref.at[pl.ds(0, GATHER_TILE)], out_sem.at[buf, k]).wait()

    @pl.when(i == 0)
    def _():
        load(0, 0).start()

    @pl.when(i > 0)
    def _():
        drain(1 - slot)

    @pl.when(i < last)
    def _():
        load(i + 1, 1 - slot).start()

    load(i, slot).wait()
    base = i * GATHER_TILE

    def issue(r, carry):
        for k in range(2):
            pltpu.make_async_copy(hbuf.at[slot, r], xs_ref.at[dest_ref[k * n_tok + base + r]],
                                  out_sem.at[slot, k]).start(priority=k)
        return carry

    lax.fori_loop(0, GATHER_TILE, issue, 0, unroll=ISSUE_UNROLL)

    @pl.when(i == last)
    def _():
        drain(slot)


def _dispatch(dest, h2, xs_zero):
    n_tok = h2.shape[0]
    return pl.pallas_call(
        functools.partial(_dispatch_kernel, n_tok=n_tok),
        grid_spec=pltpu.PrefetchScalarGridSpec(
            num_scalar_prefetch=1,
            grid=(n_tok // GATHER_TILE,),
            in_specs=[pl.BlockSpec(memory_space=pl.ANY), pl.BlockSpec(memory_space=pl.ANY)],
            out_specs=pl.BlockSpec(memory_space=pl.ANY),
            scratch_shapes=[pltpu.VMEM((2, GATHER_TILE) + h2.shape[1:], F32),
                            pltpu.SemaphoreType.DMA((2,)), pltpu.SemaphoreType.DMA((2, 2))],
        ),
        out_shape=jax.ShapeDtypeStruct(xs_zero.shape, xs_zero.dtype),
        input_output_aliases={2: 0},
        name="dispatch",
    )(dest, h2, xs_zero)


def _expert_kernel(blk_ref, nused_ref, xs_ref, w1_ref, w3_ref, w2_ref, ys_ref, w1b, w3b, w2b):
    i = pl.program_id(0)
    e = blk_ref[i]
    fresh = (i == 0) | (blk_ref[jnp.maximum(i - 1, 0)] != e)

    @pl.when(i < nused_ref[0])
    def _():
        @pl.when(fresh)
        def _():
            w1b[...] = w1_ref[0].astype(BF16)
            w3b[...] = w3_ref[0].astype(BF16)
            w2b[...] = w2_ref[0].astype(BF16)

        rows = xs_ref.shape[0]
        x = xs_ref[...].reshape(rows, w1b.shape[0]).astype(BF16)
        a = _nn(x, w1b[...])
        y = _nn((_silu(a) * _nn(x, w3b[...])).astype(BF16), w2b[...])
        ys_ref[...] = y.reshape(ys_ref.shape)

    @pl.when(i >= nused_ref[0])
    def _():
        ys_ref[...] = jnp.zeros(ys_ref.shape, F32)


def _experts(blk_e, n_used, xs, w1, w3, w2):
    p = xs.shape[0]
    tile = xs.shape[1:]
    d, de = w1.shape[1], w1.shape[2]
    rows = lambda i, blk, nu: (jnp.minimum(i, nu[0] - 1), 0, 0)
    wsel = lambda i, blk, nu: (blk[i], 0, 0)
    return pl.pallas_call(
        _expert_kernel,
        grid_spec=pltpu.PrefetchScalarGridSpec(
            num_scalar_prefetch=2,
            grid=(p // EXPERT_BLOCK,),
            in_specs=[pl.BlockSpec((EXPERT_BLOCK,) + tile, rows),
                      pl.BlockSpec((1, d, de), wsel),
                      pl.BlockSpec((1, d, de), wsel),
                      pl.BlockSpec((1, de, d), wsel)],
            out_specs=pl.BlockSpec((EXPERT_BLOCK,) + tile, lambda i, blk, nu: (i, 0, 0)),
            scratch_shapes=[pltpu.VMEM((d, de), BF16), pltpu.VMEM((d, de), BF16),
                            pltpu.VMEM((de, d), BF16)],
        ),
        out_shape=jax.ShapeDtypeStruct(xs.shape, F32),
        compiler_params=pltpu.CompilerParams(vmem_limit_bytes=VMEM_LIMIT),
        name="experts",
    )(blk_e, n_used, xs, w1, w3, w2)


def _combine_kernel(dest_ref, ys_ref, x1_ref, gt2_ref, wts_ref, g_ref, o_ref, ybuf, sem, *, n_tok, last):
    i = pl.program_id(0)
    slot = i % 2

    def gather(step, buf):
        base = step * GATHER_TILE

        def issue(r, carry):
            for k in range(2):
                pltpu.make_async_copy(ys_ref.at[dest_ref[k * n_tok + base + r]], ybuf.at[buf, k, r],
                                      sem.at[buf, k]).start(priority=k)
            return carry

        lax.fori_loop(0, GATHER_TILE, issue, 0, unroll=ISSUE_UNROLL)

    @pl.when(i == 0)
    def _():
        gather(0, 0)

    @pl.when(i + 1 < pl.num_programs(0))
    def _():
        gather(i + 1, 1 - slot)

    for k in range(2):
        pltpu.make_async_copy(ys_ref.at[pl.ds(0, GATHER_TILE)], ybuf.at[slot, k], sem.at[slot, k]).wait()

    wts = wts_ref[...]
    ya = ybuf[slot, 0].reshape(x1_ref.shape)
    yb = ybuf[slot, 1].reshape(x1_ref.shape)
    x = x1_ref[...] + gt2_ref[0] * (wts[:, 0:1] * ya + wts[:, 1:2] * yb)
    if last:
        x = x * lax.rsqrt(jnp.mean(x * x, axis=-1, keepdims=True) + EPS) * g_ref[...]
    o_ref[...] = x


def _combine(dest, ys, x1, mod, wts, final_g, seq, last=True):
    n_tok, d = x1.shape
    tt = GATHER_TILE
    per_batch = seq // tt
    return pl.pallas_call(
        functools.partial(_combine_kernel, n_tok=n_tok, last=last),
        grid_spec=pltpu.PrefetchScalarGridSpec(
            num_scalar_prefetch=1,
            grid=(n_tok // tt,),
            in_specs=[pl.BlockSpec(memory_space=pl.ANY),
                      pl.BlockSpec((tt, d), lambda i, dest: (i, 0)),
                      pl.BlockSpec((1, 1, d), lambda i, dest: (i // per_batch, 0, 5)),
                      pl.BlockSpec((tt, 2), lambda i, dest: (i, 0)),
                      pl.BlockSpec((1, d), lambda i, dest: (0, 0))],
            out_specs=pl.BlockSpec((tt, d), lambda i, dest: (i, 0)),
            scratch_shapes=[pltpu.VMEM((2, 2, tt) + ys.shape[1:], F32),
                            pltpu.SemaphoreType.DMA((2, 2))],
        ),
        out_shape=jax.ShapeDtypeStruct((n_tok, d), F32),
        name="combine",
    )(dest, ys, x1, mod, wts, final_g)


def kernel(x, c, w_ada, b_ada, norm1_g, w_in, lambda_q1, lambda_k1, lambda_q2, lambda_k2, da_subln_g, hg_lb_logits, hg_norm_g, w_up_da, w_up_hg, w_out, norm2_g, w_rg, b_rg, w_re, b_re, w1, w3, w2, final_g):
    b, s, d = x.shape
    n_tok = b * s
    depth = w_ada.shape[0]
    assert s % ROW_TILE == 0 and s % ATT_BLOCK == 0 and s % GATHER_TILE == 0
    assert (2 * n_tok) % LANES == 0 and d == SUBLANES * LANES

    c_pad = jnp.pad(c, ((0, (-b) % 8), (0, 0)))
    n_slots = 2 * n_tok + N_EXPERTS * EXPERT_BLOCK
    n_blk = n_slots // EXPERT_BLOCK
    n_blk_pad = -(-n_blk // LANES) * LANES

    for l in range(depth):
        lam_init = 0.8 - 0.6 * math.exp(-0.3 * l)
        mod = _adaln(c_pad, w_ada[l], b_ada[l][None, :])[:b].reshape(b, 1, 6 * d)

        (qda, kda, vda, qd, kd, ke, hi, sg, dec, gda, ghg) = _inproj(
            x, mod, norm1_g[l][None, :], w_in[l].astype(BF16), hg_lb_logits, l)
        yda = _attention(qda, kda, vda, lambda_q1[l][None, :], lambda_k1[l][None, :],
                         lambda_q2[l][None, :], lambda_k2[l][None, :], da_subln_g[l][None, :], lam_init)
        yhg = _hgrn(qd, kd, ke, hi, sg, dec, hg_norm_g[l][None, :])

        wr = jnp.concatenate([w_re[l].T, w_rg[l].T,
                              jnp.zeros((ROUTER_ROWS - N_EXPERTS - N_GROUPS, d), F32)], axis=0)
        wr_hi = wr.astype(BF16)
        wr_lo = (wr - wr_hi.astype(F32)).astype(BF16)
        br = jnp.concatenate([b_re[l], b_rg[l],
                              jnp.zeros((ROUTER_ROWS - N_EXPERTS - N_GROUPS,), F32)])[:, None]
        x1, h2, eidx, wts = _merge(yda, yhg, gda, ghg, x, mod, norm2_g[l][None, :],
                                   w_up_da[l].astype(BF16), w_up_hg[l].astype(BF16),
                                   w_out[l].astype(BF16), wr_hi, wr_lo, br)

        e_mat = jnp.transpose(eidx, (1, 0, 2)).reshape(2 * n_tok // LANES, LANES)
        dest, blk_e, n_used = _positions(e_mat, n_blk_pad)
        dest = dest.reshape(2 * n_tok)
        h2 = h2.reshape((n_tok,) + h2.shape[2:])
        xs = _dispatch(dest, h2, jnp.zeros((n_slots,) + h2.shape[1:], F32))
        ys = _experts(blk_e[0, :n_blk], n_used[0, :1], xs, w1[l], w3[l], w2[l])
        wts_tok = jnp.transpose(wts, (0, 2, 1)).reshape(n_tok, 2)
        x = _combine(dest, ys, x1.reshape(n_tok, d), mod, wts_tok, final_g[None, :], s,
                     last=(l == depth - 1)).reshape(b, s, d)
    return x
```

```python
import functools
import math

import jax
import jax.numpy as jnp
from jax import lax
from jax.experimental import pallas as pl
from jax.experimental.pallas import tpu as pltpu

F32 = jnp.float32
BF16 = jnp.bfloat16
I32 = jnp.int32

EPS = 1e-6
NEG = -1e30
LOG2E = math.log2(math.e)

CHUNK = 64
DA_HEADS = 4
DA_HEAD_DIM = 64
HG_HEADS = 4
HG_DK = 128
HG_DV = 128
N_GROUPS = 4
EXPERTS_PER_GROUP = 8
N_EXPERTS = N_GROUPS * EXPERTS_PER_GROUP

LANES = 128
SUBLANES = 8
ROW_TILE = 512
ATT_BLOCK = 512
EXPERT_BLOCK = 256
GATHER_TILE = 512
ISSUE_UNROLL = 8
ROUTER_ROWS = 40
VMEM_LIMIT = 52 * 1024 * 1024


def _nt(a, b):
    return lax.dot_general(a, b, (((1,), (1,)), ((), ())), preferred_element_type=F32)


def _nn(a, b):
    return jnp.dot(a, b, preferred_element_type=F32)


def _split(a):
    hi = a.astype(BF16)
    lo = (a - hi.astype(F32)).astype(BF16)
    return hi, lo


def _sigmoid(x):
    return 1.0 / (1.0 + jnp.exp(-x))


def _silu(x):
    return x * _sigmoid(x)


def _adaln_kernel(c_ref, w_ref, b_ref, o_ref):
    c_hi, c_lo = _split(_silu(c_ref[...]))
    w_hi, w_lo = _split(w_ref[...])
    o_ref[...] = _nn(c_hi, w_hi) + _nn(c_lo, w_hi) + _nn(c_hi, w_lo) + b_ref[...]


def _adaln(c_pad, w, b):
    rows, d = c_pad.shape
    n = w.shape[1]
    tn = 1024
    return pl.pallas_call(
        _adaln_kernel,
        grid=(n // tn,),
        in_specs=[
            pl.BlockSpec((rows, d), lambda j: (0, 0)),
            pl.BlockSpec((d, tn), lambda j: (0, j)),
            pl.BlockSpec((1, tn), lambda j: (0, j)),
        ],
        out_specs=pl.BlockSpec((rows, tn), lambda j: (0, j)),
        out_shape=jax.ShapeDtypeStruct((rows, n), F32),
        name="adaln",
    )(c_pad, w, b)


def _inproj_kernel(x_ref, sh_ref, sc_ref, g_ref, w_ref, lb_ref,
                   qda_ref, kda_ref, vda_ref, qd_ref, kd_ref, ke_ref, hi_ref, sg_ref,
                   dec_ref, gda_ref, ghg_ref, *, layer):
    x = x_ref[0]
    tm = x.shape[0]
    h = x * lax.rsqrt(jnp.mean(x * x, axis=-1, keepdims=True) + EPS) * g_ref[...]
    h = h * (1.0 + sc_ref[0]) + sh_ref[0]
    hb = h.astype(BF16)

    def proj(c0, width):
        return _nn(hb, w_ref[:, c0:c0 + width])

    w_da = DA_HEADS * 2 * DA_HEAD_DIM
    w_hg = HG_HEADS * HG_DK
    qda_ref[0] = (proj(0, w_da) * (LOG2E / math.sqrt(DA_HEAD_DIM))).astype(BF16)
    kda_ref[0] = proj(w_da, w_da).astype(BF16)
    vda_ref[0] = proj(2 * w_da, w_da).astype(BF16)
    c0 = 3 * w_da

    lbl = lb_ref[...]
    lbe = jnp.exp(lbl - jnp.max(lbl, axis=0, keepdims=True))
    lbs = lbe / jnp.sum(lbe, axis=0, keepdims=True)
    lb = jnp.sum(lbs[:layer + 1], axis=0, keepdims=True)

    f = lb + (1.0 - lb) * _sigmoid(proj(c0 + w_hg, w_hg))
    logf = jnp.log(f)
    kk = 1.0 - f
    pos = lax.broadcasted_iota(I32, logf.shape, 0) & (CHUNK - 1)
    a = logf
    step = 1
    while step < CHUNK:
        a = a + jnp.where(pos >= step, pltpu.roll(a, step, axis=0), 0.0)
        step *= 2
    n_ch = tm // CHUNK
    a = a.reshape(n_ch, CHUNK, w_hg)
    a_last = a[:, CHUNK - 1:CHUNK, :]
    kk3 = kk.reshape(n_ch, CHUNK, w_hg)
    qq3 = _silu(proj(c0, w_hg)).reshape(n_ch, CHUNK, w_hg)
    qd_ref[0] = (qq3 * jnp.exp(a)).reshape(tm, w_hg).astype(BF16)
    kd_ref[0] = (kk3 * jnp.exp(-a)).reshape(tm, w_hg).astype(BF16)
    ke_ref[0] = (kk3 * jnp.exp(a_last - a)).reshape(tm, w_hg).astype(BF16)
    dec_ref[0] = jnp.exp(a_last).reshape(n_ch, w_hg)
    hi_ref[0] = proj(c0 + 2 * w_hg, w_hg).astype(BF16)
    sg_ref[0] = _silu(proj(c0 + 3 * w_hg, w_hg)).astype(BF16)
    c1 = c0 + 4 * w_hg
    d = x.shape[1]
    gda_ref[0] = _sigmoid(proj(c1, d)).astype(BF16)
    ghg_ref[0] = _sigmoid(proj(c1 + d, d)).astype(BF16)


def _inproj(x, mod, g, w_bf, lb_logits, layer):
    b, s, d = x.shape
    tm = ROW_TILE
    n_cols = w_bf.shape[1]
    w_da = DA_HEADS * 2 * DA_HEAD_DIM
    w_hg = HG_HEADS * HG_DK

    def tok(width, dtype):
        return (pl.BlockSpec((1, tm, width), lambda bi, i: (bi, i, 0)),
                jax.ShapeDtypeStruct((b, s, width), dtype))

    outs = [tok(w_da, BF16)] * 3 + [tok(w_hg, BF16)] * 5
    outs.append((pl.BlockSpec((1, tm // CHUNK, w_hg), lambda bi, i: (bi, i, 0)),
                 jax.ShapeDtypeStruct((b, s // CHUNK, w_hg), F32)))
    outs += [tok(d, BF16)] * 2
    return pl.pallas_call(
        functools.partial(_inproj_kernel, layer=layer),
        grid=(b, s // tm),
        in_specs=[
            pl.BlockSpec((1, tm, d), lambda bi, i: (bi, i, 0)),
            pl.BlockSpec((1, 1, d), lambda bi, i: (bi, 0, 0)),
            pl.BlockSpec((1, 1, d), lambda bi, i: (bi, 0, 1)),
            pl.BlockSpec((1, d), lambda bi, i: (0, 0)),
            pl.BlockSpec((d, n_cols), lambda bi, i: (0, 0), pipeline_mode=pl.Buffered(1)),
            pl.BlockSpec(lb_logits.shape, lambda bi, i: (0, 0)),
        ],
        out_specs=[o[0] for o in outs],
        out_shape=[o[1] for o in outs],
        compiler_params=pltpu.CompilerParams(vmem_limit_bytes=VMEM_LIMIT),
        name="inproj",
    )(x, mod, mod, g, w_bf, lb_logits)


def _attn_kernel(q_ref, k_ref, v_ref, lq1_ref, lk1_ref, lq2_ref, lk2_ref, g_ref, o_ref,
                 kaug_ref, vaug_ref, qaug_ref, m_ref, acc_ref, sa_ref, sb_ref, *, lam_init, slopes):
    hd = pl.program_id(1)
    i = pl.program_id(2)
    tq = q_ref.shape[1]
    hw = q_ref.shape[2]
    n_key = k_ref.shape[1]

    slope = jnp.float32(slopes[-1])
    for idx in range(len(slopes) - 2, -1, -1):
        slope = jnp.where(hd == idx, jnp.float32(slopes[idx]), slope)

    @pl.when(i == 0)
    def _():
        kaug_ref[:hw, :] = k_ref[0].astype(F32).T.astype(BF16)
        jrel = (lax.broadcasted_iota(I32, (hw, n_key), 1) & (tq - 1)).astype(F32) * slope
        krow = lax.broadcasted_iota(I32, (hw, n_key), 0)
        j_hi = jrel.astype(BF16).astype(F32)
        j_mid = (jrel - j_hi).astype(BF16).astype(F32)
        kaug_ref[hw:, :] = jnp.where(
            krow == 0, j_hi,
            jnp.where(krow == 1, j_mid, jnp.where(krow == 2, jrel - j_hi - j_mid, 0.0))).astype(BF16)
        vaug_ref[:, :hw] = v_ref[0]
        vaug_ref[:, hw:] = jnp.ones((n_key, hw), BF16)

    q = q_ref[0]
    lane = lax.broadcasted_iota(I32, q.shape, 1)
    zero = jnp.zeros_like(q)
    ones2 = jnp.where(lane < 3, 1.0, 0.0).astype(BF16)
    qaug_ref[:tq, :hw] = jnp.where(lane < DA_HEAD_DIM, q, zero)
    qaug_ref[tq:, :hw] = jnp.where(lane >= DA_HEAD_DIM, q, zero)
    qaug_ref[:tq, hw:] = ones2
    qaug_ref[tq:, hw:] = ones2

    m_ref[...] = jnp.full(m_ref.shape, NEG, F32)
    acc_ref[...] = jnp.zeros(acc_ref.shape, F32)

    def scores(j, buf):
        buf[...] = _nn(qaug_ref[...], kaug_ref[:, pl.ds(pl.multiple_of(j * tq, tq), tq)])

    def consume(j, buf, diagonal):
        s = buf[...]
        if diagonal:
            ii = lax.broadcasted_iota(I32, s.shape, 0) & (tq - 1)
            jj = lax.broadcasted_iota(I32, s.shape, 1)
            ahead = jnp.minimum(ii - jj, 0).astype(F32)
            s = s + jnp.where((jj // CHUNK) <= (ii // CHUNK), (2.0 * slope) * ahead, NEG)
            shift = jnp.float32(0.0)
        else:
            shift = -slope * ((i - j) * tq).astype(F32)
        m_old = m_ref[...]
        m_new = jnp.maximum(m_old, jnp.max(s, axis=-1, keepdims=True) + shift)
        p = jnp.exp2(s - jnp.concatenate([m_new - shift] * (tq // LANES), axis=1))
        alpha = jnp.concatenate([jnp.exp2(m_old - m_new)] * (acc_ref.shape[1] // LANES), axis=1)
        vb = vaug_ref[pl.ds(pl.multiple_of(j * tq, tq), tq), :]
        acc_ref[...] = alpha * acc_ref[...] + _nn(p.astype(BF16), vb)
        m_ref[...] = m_new

    scores(0, sa_ref)

    def pair(pi, carry):
        j = 2 * pi
        scores(j + 1, sb_ref)
        consume(j, sa_ref, False)
        scores(j + 2, sa_ref)
        consume(j + 1, sb_ref, False)
        return carry

    lax.fori_loop(0, i // 2, pair, 0)

    @pl.when(i % 2 == 0)
    def _():
        consume(i, sa_ref, True)

    @pl.when(i % 2 == 1)
    def _():
        scores(i, sb_ref)
        consume(i - 1, sa_ref, False)
        consume(i, sb_ref, True)

    acc = acc_ref[...]
    o1 = acc[:tq, :hw] / acc[:tq, hw:]
    o2 = acc[tq:, :hw] / acc[tq:, hw:]
    lam = (jnp.exp(jnp.sum(lq1_ref[...] * lk1_ref[...], axis=-1, keepdims=True))
           - jnp.exp(jnp.sum(lq2_ref[...] * lk2_ref[...], axis=-1, keepdims=True)) + lam_init)
    o = o1 - lam * o2
    o = o * lax.rsqrt(jnp.mean(o * o, axis=-1, keepdims=True) + EPS) * g_ref[...]
    o_ref[0] = (o * (1.0 - lam_init)).astype(BF16)


def _attention(q, k, v, lq1, lk1, lq2, lk2, subln_g, lam_init):
    b, s, _ = q.shape
    tq = ATT_BLOCK
    hw = 2 * DA_HEAD_DIM
    assert hw == LANES
    slopes = tuple(LOG2E * 2.0 ** (-8.0 * (h + 1) / DA_HEADS) for h in range(DA_HEADS))
    vec = lambda n: pl.BlockSpec((1, n), lambda bi, h, i: (0, 0))
    return pl.pallas_call(
        functools.partial(_attn_kernel, lam_init=lam_init, slopes=slopes),
        grid=(b, DA_HEADS, s // tq),
        in_specs=[
            pl.BlockSpec((1, tq, hw), lambda bi, h, i: (bi, i, h)),
            pl.BlockSpec((1, s, hw), lambda bi, h, i: (bi, 0, h)),
            pl.BlockSpec((1, s, hw), lambda bi, h, i: (bi, 0, h)),
            vec(DA_HEAD_DIM), vec(DA_HEAD_DIM), vec(DA_HEAD_DIM), vec(DA_HEAD_DIM), vec(hw),
        ],
        out_specs=pl.BlockSpec((1, tq, hw), lambda bi, h, i: (bi, i, h)),
        out_shape=jax.ShapeDtypeStruct((b, s, DA_HEADS * hw), BF16),
        scratch_shapes=[
            pltpu.VMEM((2 * hw, s), BF16),
            pltpu.VMEM((s, 2 * hw), BF16),
            pltpu.VMEM((2 * tq, 2 * hw), BF16),
            pltpu.VMEM((2 * tq, LANES), F32),
            pltpu.VMEM((2 * tq, 2 * hw), F32),
            pltpu.VMEM((2 * tq, tq), F32),
            pltpu.VMEM((2 * tq, tq), F32),
        ],
        compiler_params=pltpu.CompilerParams(vmem_limit_bytes=VMEM_LIMIT),
        name="attention",
    )(q, k, v, lq1, lk1, lq2, lk2, subln_g)


def _hgrn_kernel(qd_ref, kd_ref, ke_ref, v_ref, sg_ref, dec_ref, g_ref, o_ref, st_ref):
    @pl.when(pl.program_id(1) == 0)
    def _():
        st_ref[...] = jnp.zeros(st_ref.shape, F32)

    n_ch = qd_ref.shape[1] // CHUNK
    r = lax.broadcasted_iota(I32, (CHUNK, CHUNK), 0)
    c = lax.broadcasted_iota(I32, (CHUNK, CHUNK), 1)
    causal = r >= c

    heads = [slice(h * HG_DK, (h + 1) * HG_DK) for h in range(HG_HEADS)]
    chunks = [slice(ci * CHUNK, (ci + 1) * CHUNK) for ci in range(n_ch)]
    incr = [[_nn(v_ref[0, rows, cols].astype(F32).T.astype(BF16), ke_ref[0, rows, cols])
             for rows in chunks] for cols in heads]
    start = []
    for h, cols in enumerate(heads):
        st = st_ref[h]
        before = []
        for ci in range(n_ch):
            before.append(st.astype(BF16))
            st = st * dec_ref[0, ci:ci + 1, cols] + incr[h][ci]
        st_ref[h] = st
        start.append(before)
    for h, cols in enumerate(heads):
        for ci, rows in enumerate(chunks):
            qd = qd_ref[0, rows, cols]
            scores = jnp.where(causal, _nt(qd, kd_ref[0, rows, cols]), 0.0).astype(BF16)
            o = _nn(scores, v_ref[0, rows, cols]) + _nt(qd, start[h][ci])
            o = o * lax.rsqrt(jnp.mean(o * o, axis=-1, keepdims=True) + EPS) * g_ref[...]
            o_ref[0, rows, cols] = (o * sg_ref[0, rows, cols].astype(F32)).astype(BF16)


def _hgrn(qd, kd, ke, v, sg, dec, norm_g):
    b, s, w = qd.shape
    ts = ROW_TILE
    tok = pl.BlockSpec((1, ts, w), lambda bi, i: (bi, i, 0))
    return pl.pallas_call(
        _hgrn_kernel,
        grid=(b, s // ts),
        in_specs=[tok, tok, tok, tok, tok,
                  pl.BlockSpec((1, ts // CHUNK, w), lambda bi, i: (bi, i, 0)),
                  pl.BlockSpec((1, HG_DV), lambda bi, i: (0, 0))],
        out_specs=tok,
        out_shape=jax.ShapeDtypeStruct((b, s, w), BF16),
        scratch_shapes=[pltpu.VMEM((HG_HEADS, HG_DV, HG_DK), F32)],
        name="hgrn",
    )(qd, kd, ke, v, sg, dec, norm_g)


def _merge_kernel(yda_ref, yhg_ref, gda_ref, ghg_ref, x_ref, gt1_ref, sh2_ref, sc2_ref, g2_ref,
                  wda_ref, whg_ref, wout_ref, wrh_ref, wrl_ref, br_ref,
                  x1_ref, h2_ref, eidx_ref, wts_ref):
    u = (gda_ref[0].astype(F32) * _nn(yda_ref[0], wda_ref[...])
         + ghg_ref[0].astype(F32) * _nn(yhg_ref[0], whg_ref[...]))
    x1 = x_ref[0] + gt1_ref[0] * _nn(u.astype(BF16), wout_ref[...])
    x1_ref[0] = x1
    h2 = x1 * lax.rsqrt(jnp.mean(x1 * x1, axis=-1, keepdims=True) + EPS) * g2_ref[...]
    h2 = h2 * (1.0 + sc2_ref[0]) + sh2_ref[0]
    h2_ref[0] = h2.reshape(h2.shape[0], SUBLANES, h2.shape[1] // SUBLANES)

    h_hi, h_lo = _split(h2)
    wrh = wrh_ref[...]
    lg = _nt(wrh, h_hi) + _nt(wrl_ref[...], h_hi) + _nt(wrh, h_lo) + br_ref[...]

    tm = h2.shape[0]
    eg = EXPERTS_PER_GROUP
    gl = lg[N_EXPERTS:N_EXPERTS + N_GROUPS]
    gmax = jnp.max(gl, axis=0, keepdims=True)
    g_p = 1.0 / jnp.sum(jnp.exp(gl - gmax), axis=0, keepdims=True)
    gi = lax.broadcasted_iota(I32, (N_GROUPS, tm), 0)
    g_idx = jnp.min(jnp.where(gl == gmax, gi, N_GROUPS), axis=0, keepdims=True)
    sel = lg[(N_GROUPS - 1) * eg:N_GROUPS * eg]
    for g in range(N_GROUPS - 2, -1, -1):
        sel = jnp.where(g_idx == g, lg[g * eg:(g + 1) * eg], sel)
    ei = lax.broadcasted_iota(I32, (eg, tm), 0)
    m1 = jnp.max(sel, axis=0, keepdims=True)
    i1 = jnp.min(jnp.where(sel == m1, ei, eg), axis=0, keepdims=True)
    sel2 = jnp.where(ei == i1, -jnp.inf, sel)
    m2 = jnp.max(sel2, axis=0, keepdims=True)
    i2 = jnp.min(jnp.where(sel2 == m2, ei, eg), axis=0, keepdims=True)
    t = jnp.exp(m2 - m1)
    wa = 1.0 / (1.0 + t)
    eidx_ref[0] = jnp.concatenate([g_idx * eg + i1, g_idx * eg + i2], axis=0)
    wts_ref[0] = jnp.concatenate([g_p * wa, g_p * (t * wa)], axis=0)


def _merge(yda, yhg, gda, ghg, x, mod, g2, wda, whg, wout, wrh, wrl, br):
    b, s, d = x.shape
    tm = ROW_TILE
    w = yda.shape[2]
    tok = lambda width: pl.BlockSpec((1, tm, width), lambda bi, i: (bi, i, 0))
    modspec = lambda k: pl.BlockSpec((1, 1, d), lambda bi, i: (bi, 0, k))
    full = lambda a: pl.BlockSpec(a.shape, lambda bi, i: (0,) * a.ndim)
    lane_rows = pl.BlockSpec((1, 2, tm), lambda bi, i: (bi, 0, i))
    return pl.pallas_call(
        _merge_kernel,
        grid=(b, s // tm),
        in_specs=[tok(w), tok(w), tok(d), tok(d), tok(d),
                  modspec(2), modspec(3), modspec(4), full(g2),
                  full(wda), full(whg), full(wout), full(wrh), full(wrl), full(br)],
        out_specs=[tok(d), pl.BlockSpec((1, tm, SUBLANES, d // SUBLANES), lambda bi, i: (bi, i, 0, 0)),
                   lane_rows, lane_rows],
        out_shape=[jax.ShapeDtypeStruct((b, s, d), F32),
                   jax.ShapeDtypeStruct((b, s, SUBLANES, d // SUBLANES), F32),
                   jax.ShapeDtypeStruct((b, 2, s), I32), jax.ShapeDtypeStruct((b, 2, s), F32)],
        compiler_params=pltpu.CompilerParams(vmem_limit_bytes=VMEM_LIMIT),
        name="merge",
    )(yda, yhg, gda, ghg, x, mod, mod, mod, g2, wda, whg, wout, wrh, wrl, br)


def _positions_kernel(e_ref, dest_ref, blk_ref, nused_ref):
    e_mat = e_ref[...]
    nr = e_mat.shape[0]
    nb = blk_ref.shape[1]
    r = lax.broadcasted_iota(I32, (LANES, LANES), 0)
    c = lax.broadcasted_iota(I32, (LANES, LANES), 1)
    incl = jnp.where(r <= c, 1.0, 0.0).astype(BF16)
    ones = jnp.ones((LANES, LANES), BF16)
    rr = lax.broadcasted_iota(I32, (nr, nr), 0)
    cc = lax.broadcasted_iota(I32, (nr, nr), 1)
    before = jnp.where(cc < rr, 1.0, 0.0).astype(BF16)
    blk_start = (lax.broadcasted_iota(I32, (1, nb), 1) * EXPERT_BLOCK).astype(F32)

    def per_expert(e, carry):
        pstart, dest, blk = carry
        hit = e_mat == e
        oh = jnp.where(hit, 1.0, 0.0).astype(BF16)
        in_row = _nn(oh, incl)
        row_tot = _nn(oh, ones)
        row_off = _nn(before, row_tot.astype(BF16))
        total = row_off[nr - 1:nr, :] + row_tot[nr - 1:nr, :]
        dest = jnp.where(hit, pstart + row_off + in_row - 1.0, dest)
        pend = pstart + jnp.floor((total + (EXPERT_BLOCK - 1)) * (1.0 / EXPERT_BLOCK)) * EXPERT_BLOCK
        blk = blk + jnp.where(blk_start >= pend[:, :1], 1.0, 0.0)
        return pend, dest, blk

    pstart0 = jnp.zeros((1, LANES), F32)
    pend, dest, blk = lax.fori_loop(
        0, N_EXPERTS, per_expert,
        (pstart0, jnp.zeros(e_mat.shape, F32), jnp.zeros((1, nb), F32)))
    dest_ref[...] = dest.astype(I32)
    blk_ref[...] = jnp.minimum(blk, N_EXPERTS - 1).astype(I32)
    nused_ref[...] = (pend * (1.0 / EXPERT_BLOCK)).astype(I32)


def _positions(e_mat, n_blk_pad):
    nr = e_mat.shape[0]
    return pl.pallas_call(
        _positions_kernel,
        out_shape=[jax.ShapeDtypeStruct((nr, LANES), I32),
                   jax.ShapeDtypeStruct((1, n_blk_pad), I32),
                   jax.ShapeDtypeStruct((1, LANES), I32)],
        name="positions",
    )(e_mat)


def _dispatch_kernel(dest_ref, h2_ref, xs_in_ref, xs_ref, hbuf, in_sem, out_sem, *, n_tok):
    del xs_in_ref
    i = pl.program_id(0)
    last = pl.num_programs(0) - 1
    slot = i % 2

    def load(step, buf):
        return pltpu.make_async_copy(h2_ref.at[pl.ds(step * GATHER_TILE, GATHER_TILE)], hbuf.at[buf],
                                     in_sem.at[buf])

    def drain(buf):
        for k in range(2):
            pltpu.make_async_copy(hbuf.at[buf], xs_ref.at[pl.ds(0, GATHER_TILE)], out_sem.at[buf, k]).wait()

    @pl.when(i == 0)
    def _():
        load(0, 0).start()

    @pl.when(i > 0)
    def _():
        drain(1 - slot)

    @pl.when(i < last)
    def _():
        load(i + 1, 1 - slot).start()

    load(i, slot).wait()
    base = i * GATHER_TILE

    def issue(r, carry):
        for k in range(2):
            pltpu.make_async_copy(hbuf.at[slot, r], xs_ref.at[dest_ref[k * n_tok + base + r]],
                                  out_sem.at[slot, k]).start(priority=k)
        return carry

    lax.fori_loop(0, GATHER_TILE, issue, 0, unroll=ISSUE_UNROLL)

    @pl.when(i == last)
    def _():
        drain(slot)


def _dispatch(dest, h2, xs_zero):
    n_tok = h2.shape[0]
    return pl.pallas_call(
        functools.partial(_dispatch_kernel, n_tok=n_tok),
        grid_spec=pltpu.PrefetchScalarGridSpec(
            num_scalar_prefetch=1,
            grid=(n_tok // GATHER_TILE,),
            in_specs=[pl.BlockSpec(memory_space=pl.ANY), pl.BlockSpec(memory_space=pl.ANY)],
            out_specs=pl.BlockSpec(memory_space=pl.ANY),
            scratch_shapes=[pltpu.VMEM((2, GATHER_TILE) + h2.shape[1:], F32),
                            pltpu.SemaphoreType.DMA((2,)), pltpu.SemaphoreType.DMA((2, 2))],
        ),
        out_shape=jax.ShapeDtypeStruct(xs_zero.shape, xs_zero.dtype),
        input_output_aliases={2: 0},
        compiler_params=pltpu.CompilerParams(dimension_semantics=("arbitrary",)),
        name="dispatch",
    )(dest, h2, xs_zero)


def _expert_kernel(blk_ref, nused_ref, xs_ref, w1_ref, w3_ref, w2_ref, ys_ref,
                   w1b, w3b, w2b, w1s, w3s, w2s, sem):
    i = pl.program_id(0)
    n_used = nused_ref[0]
    e = blk_ref[i]
    fresh = (i == 0) | (blk_ref[jnp.maximum(i - 1, 0)] != e)

    def fetch(expert):
        return (pltpu.make_async_copy(w1_ref.at[expert], w1s, sem.at[0]),
                pltpu.make_async_copy(w3_ref.at[expert], w3s, sem.at[1]),
                pltpu.make_async_copy(w2_ref.at[expert], w2s, sem.at[2]))

    @pl.when(i < n_used)
    def _():
        @pl.when(i == 0)
        def _():
            for cp in fetch(e):
                cp.start()

        @pl.when(fresh)
        def _():
            for cp in fetch(e):
                cp.wait()
            w1b[...] = w1s[...].astype(BF16)
            w3b[...] = w3s[...].astype(BF16)
            w2b[...] = w2s[...].astype(BF16)
            nxt = lax.while_loop(lambda j: (j < n_used) & (blk_ref[jnp.minimum(j, n_used - 1)] == e),
                                 lambda j: j + 1, i + 1)

            @pl.when(nxt < n_used)
            def _():
                for cp in fetch(blk_ref[jnp.minimum(nxt, n_used - 1)]):
                    cp.start()

        rows = xs_ref.shape[0]
        x = xs_ref[...].reshape(rows, w1b.shape[0]).astype(BF16)
        a = _nn(x, w1b[...])
        y = _nn((_silu(a) * _nn(x, w3b[...])).astype(BF16), w2b[...])
        ys_ref[...] = y.reshape(ys_ref.shape)

    @pl.when(i >= nused_ref[0])
    def _():
        ys_ref[...] = jnp.zeros(ys_ref.shape, F32)


def _experts(blk_e, n_used, xs, w1, w3, w2):
    p = xs.shape[0]
    tile = xs.shape[1:]
    d, de = w1.shape[1], w1.shape[2]
    rows = lambda i, blk, nu: (jnp.minimum(i, nu[0] - 1), 0, 0)
    hbm = pl.BlockSpec(memory_space=pl.ANY)
    return pl.pallas_call(
        _expert_kernel,
        grid_spec=pltpu.PrefetchScalarGridSpec(
            num_scalar_prefetch=2,
            grid=(p // EXPERT_BLOCK,),
            in_specs=[pl.BlockSpec((EXPERT_BLOCK,) + tile, rows), hbm, hbm, hbm],
            out_specs=pl.BlockSpec((EXPERT_BLOCK,) + tile, lambda i, blk, nu: (i, 0, 0)),
            scratch_shapes=[pltpu.VMEM((d, de), BF16), pltpu.VMEM((d, de), BF16),
                            pltpu.VMEM((de, d), BF16),
                            pltpu.VMEM((d, de), F32), pltpu.VMEM((d, de), F32),
                            pltpu.VMEM((de, d), F32), pltpu.SemaphoreType.DMA((3,))],
        ),
        out_shape=jax.ShapeDtypeStruct(xs.shape, F32),
        compiler_params=pltpu.CompilerParams(vmem_limit_bytes=VMEM_LIMIT,
                                             dimension_semantics=("arbitrary",)),
        name="experts",
    )(blk_e, n_used, xs, w1, w3, w2)


def _combine_kernel(dest_ref, ys_ref, x1_ref, gt2_ref, wts_ref, g_ref, o_ref, ybuf, sem, *, n_tok, last):
    i = pl.program_id(0)
    slot = i % 2

    def gather(step, buf):
        base = step * GATHER_TILE

        def issue(r, carry):
            for k in range(2):
                pltpu.make_async_copy(ys_ref.at[dest_ref[k * n_tok + base + r]], ybuf.at[buf, k, r],
                                      sem.at[buf, k]).start(priority=k)
            return carry

        lax.fori_loop(0, GATHER_TILE, issue, 0, unroll=ISSUE_UNROLL)

    @pl.when(i == 0)
    def _():
        gather(0, 0)

    @pl.when(i + 1 < pl.num_programs(0))
    def _():
        gather(i + 1, 1 - slot)

    for k in range(2):
        pltpu.make_async_copy(ys_ref.at[pl.ds(0, GATHER_TILE)], ybuf.at[slot, k], sem.at[slot, k]).wait()

    wts = wts_ref[...]
    ya = ybuf[slot, 0].reshape(x1_ref.shape)
    yb = ybuf[slot, 1].reshape(x1_ref.shape)
    x = x1_ref[...] + gt2_ref[0] * (wts[:, 0:1] * ya + wts[:, 1:2] * yb)
    if last:
        x = x * lax.rsqrt(jnp.mean(x * x, axis=-1, keepdims=True) + EPS) * g_ref[...]
    o_ref[...] = x


def _combine(dest, ys, x1, mod, wts, final_g, seq, last=True):
    n_tok, d = x1.shape
    tt = GATHER_TILE
    per_batch = seq // tt
    return pl.pallas_call(
        functools.partial(_combine_kernel, n_tok=n_tok, last=last),
        grid_spec=pltpu.PrefetchScalarGridSpec(
            num_scalar_prefetch=1,
            grid=(n_tok // tt,),
            in_specs=[pl.BlockSpec(memory_space=pl.ANY),
                      pl.BlockSpec((tt, d), lambda i, dest: (i, 0)),
                      pl.BlockSpec((1, 1, d), lambda i, dest: (i // per_batch, 0, 5)),
                      pl.BlockSpec((tt, 2), lambda i, dest: (i, 0)),
                      pl.BlockSpec((1, d), lambda i, dest: (0, 0))],
            out_specs=pl.BlockSpec((tt, d), lambda i, dest: (i, 0)),
            scratch_shapes=[pltpu.VMEM((2, 2, tt) + ys.shape[1:], F32),
                            pltpu.SemaphoreType.DMA((2, 2))],
        ),
        out_shape=jax.ShapeDtypeStruct((n_tok, d), F32),
        compiler_params=pltpu.CompilerParams(dimension_semantics=("arbitrary",)),
        name="combine",
    )(dest, ys, x1, mod, wts, final_g)


def kernel(x, c, w_ada, b_ada, norm1_g, w_in, lambda_q1, lambda_k1, lambda_q2, lambda_k2, da_subln_g, hg_lb_logits, hg_norm_g, w_up_da, w_up_hg, w_out, norm2_g, w_rg, b_rg, w_re, b_re, w1, w3, w2, final_g):
    b, s, d = x.shape
    n_tok = b * s
    depth = w_ada.shape[0]
    assert s % ROW_TILE == 0 and s % ATT_BLOCK == 0 and s % GATHER_TILE == 0
    assert (2 * n_tok) % LANES == 0 and d == SUBLANES * LANES

    c_pad = jnp.pad(c, ((0, (-b) % 8), (0, 0)))
    n_slots = 2 * n_tok + N_EXPERTS * EXPERT_BLOCK
    n_blk = n_slots // EXPERT_BLOCK
    n_blk_pad = -(-n_blk // LANES) * LANES

    for l in range(depth):
        lam_init = 0.8 - 0.6 * math.exp(-0.3 * l)
        mod = _adaln(c_pad, w_ada[l], b_ada[l][None, :])[:b].reshape(b, 1, 6 * d)

        (qda, kda, vda, qd, kd, ke, hi, sg, dec, gda, ghg) = _inproj(
            x, mod, norm1_g[l][None, :], w_in[l].astype(BF16), hg_lb_logits, l)
        yda = _attention(qda, kda, vda, lambda_q1[l][None, :], lambda_k1[l][None, :],
                         lambda_q2[l][None, :], lambda_k2[l][None, :], da_subln_g[l][None, :], lam_init)
        yhg = _hgrn(qd, kd, ke, hi, sg, dec, hg_norm_g[l][None, :])

        wr = jnp.concatenate([w_re[l].T, w_rg[l].T,
                              jnp.zeros((ROUTER_ROWS - N_EXPERTS - N_GROUPS, d), F32)], axis=0)
        wr_hi = wr.astype(BF16)
        wr_lo = (wr - wr_hi.astype(F32)).astype(BF16)
        br = jnp.concatenate([b_re[l], b_rg[l],
                              jnp.zeros((ROUTER_ROWS - N_EXPERTS - N_GROUPS,), F32)])[:, None]
        x1, h2, eidx, wts = _merge(yda, yhg, gda, ghg, x, mod, norm2_g[l][None, :],
                                   w_up_da[l].astype(BF16), w_up_hg[l].astype(BF16),
                                   w_out[l].astype(BF16), wr_hi, wr_lo, br)

        e_mat = jnp.transpose(eidx, (1, 0, 2)).reshape(2 * n_tok // LANES, LANES)
        dest, blk_e, n_used = _positions(e_mat, n_blk_pad)
        dest = dest.reshape(2 * n_tok)
        h2 = h2.reshape((n_tok,) + h2.shape[2:])
        xs = _dispatch(dest, h2, jnp.zeros((n_slots,) + h2.shape[1:], F32))
        ys = _experts(blk_e[0, :n_blk], n_used[0, :1], xs, w1[l], w3[l], w2[l])
        wts_tok = jnp.transpose(wts, (0, 2, 1)).reshape(n_tok, 2)
        x = _combine(dest, ys, x1.reshape(n_tok, d), mod, wts_tok, final_g[None, :], s,
                     last=(l == depth - 1)).reshape(b, s, d)
    return x
```

```python
import functools
import math

import jax
import jax.numpy as jnp
from jax import lax
from jax.experimental import pallas as pl
from jax.experimental.pallas import tpu as pltpu

F32 = jnp.float32
BF16 = jnp.bfloat16
I32 = jnp.int32

EPS = 1e-6
NEG = -1e30
LOG2E = math.log2(math.e)

CHUNK = 64
DA_HEADS = 4
DA_HEAD_DIM = 64
HG_HEADS = 4
HG_DK = 128
HG_DV = 128
N_GROUPS = 4
EXPERTS_PER_GROUP = 8
N_EXPERTS = N_GROUPS * EXPERTS_PER_GROUP

LANES = 128
SUBLANES = 8
ROW_TILE = 512
ATT_BLOCK = 512
EXPERT_BLOCK = 256
GATHER_TILE = 512
ISSUE_UNROLL = 8
ROUTER_ROWS = 40
VMEM_LIMIT = 52 * 1024 * 1024


def _nt(a, b):
    return lax.dot_general(a, b, (((1,), (1,)), ((), ())), preferred_element_type=F32)


def _nn(a, b):
    return jnp.dot(a, b, preferred_element_type=F32)


def _split(a):
    hi = a.astype(BF16)
    lo = (a - hi.astype(F32)).astype(BF16)
    return hi, lo


def _sigmoid(x):
    return 1.0 / (1.0 + jnp.exp(-x))


def _silu(x):
    return x * _sigmoid(x)


def _adaln_kernel(c_ref, w_ref, b_ref, o_ref):
    c_hi, c_lo = _split(_silu(c_ref[...]))
    w_hi, w_lo = _split(w_ref[...])
    o_ref[...] = _nn(c_hi, w_hi) + _nn(c_lo, w_hi) + _nn(c_hi, w_lo) + b_ref[...]


def _adaln(c_pad, w, b):
    rows, d = c_pad.shape
    n = w.shape[1]
    tn = 1024
    return pl.pallas_call(
        _adaln_kernel,
        grid=(n // tn,),
        in_specs=[
            pl.BlockSpec((rows, d), lambda j: (0, 0)),
            pl.BlockSpec((d, tn), lambda j: (0, j)),
            pl.BlockSpec((1, tn), lambda j: (0, j)),
        ],
        out_specs=pl.BlockSpec((rows, tn), lambda j: (0, j)),
        out_shape=jax.ShapeDtypeStruct((rows, n), F32),
        name="adaln",
    )(c_pad, w, b)


def _inproj_kernel(x_ref, sh_ref, sc_ref, g_ref, w_ref, lb_ref,
                   qda_ref, kda_ref, vda_ref, qd_ref, kd_ref, ke_ref, hi_ref, sg_ref,
                   dec_ref, gda_ref, ghg_ref, *, layer):
    x = x_ref[0]
    tm = x.shape[0]
    h = x * lax.rsqrt(jnp.mean(x * x, axis=-1, keepdims=True) + EPS) * g_ref[...]
    h = h * (1.0 + sc_ref[0]) + sh_ref[0]
    hb = h.astype(BF16)

    def proj(c0, width):
        return _nn(hb, w_ref[:, c0:c0 + width])

    w_da = DA_HEADS * 2 * DA_HEAD_DIM
    w_hg = HG_HEADS * HG_DK
    qda_ref[0] = (proj(0, w_da) * (LOG2E / math.sqrt(DA_HEAD_DIM))).astype(BF16)
    kda_ref[0] = proj(w_da, w_da).astype(BF16)
    vda_ref[0] = proj(2 * w_da, w_da).astype(BF16)
    c0 = 3 * w_da

    lbl = lb_ref[...]
    lbe = jnp.exp(lbl - jnp.max(lbl, axis=0, keepdims=True))
    lbs = lbe / jnp.sum(lbe, axis=0, keepdims=True)
    lb = jnp.sum(lbs[:layer + 1], axis=0, keepdims=True)

    f = lb + (1.0 - lb) * _sigmoid(proj(c0 + w_hg, w_hg))
    logf = jnp.log(f)
    kk = 1.0 - f
    pos = lax.broadcasted_iota(I32, logf.shape, 0) & (CHUNK - 1)
    a = logf
    step = 1
    while step < CHUNK:
        a = a + jnp.where(pos >= step, pltpu.roll(a, step, axis=0), 0.0)
        step *= 2
    n_ch = tm // CHUNK
    a = a.reshape(n_ch, CHUNK, w_hg)
    a_last = a[:, CHUNK - 1:CHUNK, :]
    kk3 = kk.reshape(n_ch, CHUNK, w_hg)
    qq3 = _silu(proj(c0, w_hg)).reshape(n_ch, CHUNK, w_hg)
    qd_ref[0] = (qq3 * jnp.exp(a)).reshape(tm, w_hg).astype(BF16)
    kd_ref[0] = (kk3 * jnp.exp(-a)).reshape(tm, w_hg).astype(BF16)
    ke_ref[0] = (kk3 * jnp.exp(a_last - a)).reshape(tm, w_hg).astype(BF16)
    dec_ref[0] = jnp.exp(a_last).reshape(n_ch, w_hg)
    hi_ref[0] = proj(c0 + 2 * w_hg, w_hg).astype(BF16)
    sg_ref[0] = _silu(proj(c0 + 3 * w_hg, w_hg)).astype(BF16)
    c1 = c0 + 4 * w_hg
    d = x.shape[1]
    gda_ref[0] = _sigmoid(proj(c1, d)).astype(BF16)
    ghg_ref[0] = _sigmoid(proj(c1 + d, d)).astype(BF16)


def _inproj(x, mod, g, w_bf, lb_logits, layer):
    b, s, d = x.shape
    tm = ROW_TILE
    n_cols = w_bf.shape[1]
    w_da = DA_HEADS * 2 * DA_HEAD_DIM
    w_hg = HG_HEADS * HG_DK

    def tok(width, dtype):
        return (pl.BlockSpec((1, tm, width), lambda bi, i: (bi, i, 0)),
                jax.ShapeDtypeStruct((b, s, width), dtype))

    outs = [tok(w_da, BF16)] * 3 + [tok(w_hg, BF16)] * 5
    outs.append((pl.BlockSpec((1, tm // CHUNK, w_hg), lambda bi, i: (bi, i, 0)),
                 jax.ShapeDtypeStruct((b, s // CHUNK, w_hg), F32)))
    outs += [tok(d, BF16)] * 2
    return pl.pallas_call(
        functools.partial(_inproj_kernel, layer=layer),
        grid=(b, s // tm),
        in_specs=[
            pl.BlockSpec((1, tm, d), lambda bi, i: (bi, i, 0)),
            pl.BlockSpec((1, 1, d), lambda bi, i: (bi, 0, 0)),
            pl.BlockSpec((1, 1, d), lambda bi, i: (bi, 0, 1)),
            pl.BlockSpec((1, d), lambda bi, i: (0, 0)),
            pl.BlockSpec((d, n_cols), lambda bi, i: (0, 0), pipeline_mode=pl.Buffered(1)),
            pl.BlockSpec(lb_logits.shape, lambda bi, i: (0, 0)),
        ],
        out_specs=[o[0] for o in outs],
        out_shape=[o[1] for o in outs],
        compiler_params=pltpu.CompilerParams(vmem_limit_bytes=VMEM_LIMIT),
        name="inproj",
    )(x, mod, mod, g, w_bf, lb_logits)


def _attn_kernel(q_ref, k_ref, v_ref, lq1_ref, lk1_ref, lq2_ref, lk2_ref, g_ref, o_ref, zs_ref,
                 kaug_ref, vaug_ref, qaug_ref, m_ref, acc_ref, sa_ref, sb_ref, zero_ref, zsem,
                 *, lam_init, slopes):
    hd = pl.program_id(1)
    i = pl.program_id(2)

    step = (pl.program_id(0) * pl.num_programs(1) + hd) * pl.num_programs(2) + i
    n_steps = pl.num_programs(0) * pl.num_programs(1) * pl.num_programs(2)
    zrows = zero_ref.shape[0]

    def zero_copy(t):
        return pltpu.make_async_copy(zero_ref, zs_ref.at[pl.ds(t * zrows, zrows)], zsem)

    @pl.when(step == 0)
    def _():
        zero_ref[...] = jnp.zeros(zero_ref.shape, F32)

    @pl.when(step > 0)
    def _():
        zero_copy(step - 1).wait()

    zero_copy(step).start()

    @pl.when(step == n_steps - 1)
    def _():
        zero_copy(step).wait()
    tq = q_ref.shape[1]
    hw = q_ref.shape[2]
    n_key = k_ref.shape[1]

    slope = jnp.float32(slopes[-1])
    for idx in range(len(slopes) - 2, -1, -1):
        slope = jnp.where(hd == idx, jnp.float32(slopes[idx]), slope)

    @pl.when(i == 0)
    def _():
        kaug_ref[:hw, :] = k_ref[0].astype(F32).T.astype(BF16)
        jrel = (lax.broadcasted_iota(I32, (hw, n_key), 1) & (tq - 1)).astype(F32) * slope
        krow = lax.broadcasted_iota(I32, (hw, n_key), 0)
        j_hi = jrel.astype(BF16).astype(F32)
        j_mid = (jrel - j_hi).astype(BF16).astype(F32)
        kaug_ref[hw:, :] = jnp.where(
            krow == 0, j_hi,
            jnp.where(krow == 1, j_mid, jnp.where(krow == 2, jrel - j_hi - j_mid, 0.0))).astype(BF16)
        vaug_ref[:, :hw] = v_ref[0]
        vaug_ref[:, hw:] = jnp.ones((n_key, hw), BF16)

    q = q_ref[0]
    lane = lax.broadcasted_iota(I32, q.shape, 1)
    zero = jnp.zeros_like(q)
    ones2 = jnp.where(lane < 3, 1.0, 0.0).astype(BF16)
    qaug_ref[:tq, :hw] = jnp.where(lane < DA_HEAD_DIM, q, zero)
    qaug_ref[tq:, :hw] = jnp.where(lane >= DA_HEAD_DIM, q, zero)
    qaug_ref[:tq, hw:] = ones2
    qaug_ref[tq:, hw:] = ones2

    m_ref[...] = jnp.full(m_ref.shape, NEG, F32)
    acc_ref[...] = jnp.zeros(acc_ref.shape, F32)

    def scores(j, buf):
        buf[...] = _nn(qaug_ref[...], kaug_ref[:, pl.ds(pl.multiple_of(j * tq, tq), tq)])

    def consume(j, buf, diagonal):
        s = buf[...]
        if diagonal:
            ii = lax.broadcasted_iota(I32, s.shape, 0) & (tq - 1)
            jj = lax.broadcasted_iota(I32, s.shape, 1)
            ahead = jnp.minimum(ii - jj, 0).astype(F32)
            s = s + jnp.where((jj // CHUNK) <= (ii // CHUNK), (2.0 * slope) * ahead, NEG)
            shift = jnp.float32(0.0)
        else:
            shift = -slope * ((i - j) * tq).astype(F32)
        m_old = m_ref[...]
        m_new = jnp.maximum(m_old, jnp.max(s, axis=-1, keepdims=True) + shift)
        p = jnp.exp2(s - jnp.concatenate([m_new - shift] * (tq // LANES), axis=1))
        alpha = jnp.concatenate([jnp.exp2(m_old - m_new)] * (acc_ref.shape[1] // LANES), axis=1)
        vb = vaug_ref[pl.ds(pl.multiple_of(j * tq, tq), tq), :]
        acc_ref[...] = alpha * acc_ref[...] + _nn(p.astype(BF16), vb)
        m_ref[...] = m_new

    scores(0, sa_ref)

    def pair(pi, carry):
        j = 2 * pi
        scores(j + 1, sb_ref)
        consume(j, sa_ref, False)
        scores(j + 2, sa_ref)
        consume(j + 1, sb_ref, False)
        return carry

    lax.fori_loop(0, i // 2, pair, 0)

    @pl.when(i % 2 == 0)
    def _():
        consume(i, sa_ref, True)

    @pl.when(i % 2 == 1)
    def _():
        scores(i, sb_ref)
        consume(i - 1, sa_ref, False)
        consume(i, sb_ref, True)

    acc = acc_ref[...]
    o1 = acc[:tq, :hw] / acc[:tq, hw:]
    o2 = acc[tq:, :hw] / acc[tq:, hw:]
    lam = (jnp.exp(jnp.sum(lq1_ref[...] * lk1_ref[...], axis=-1, keepdims=True))
           - jnp.exp(jnp.sum(lq2_ref[...] * lk2_ref[...], axis=-1, keepdims=True)) + lam_init)
    o = o1 - lam * o2
    o = o * lax.rsqrt(jnp.mean(o * o, axis=-1, keepdims=True) + EPS) * g_ref[...]
    o_ref[0] = (o * (1.0 - lam_init)).astype(BF16)


def _attention(q, k, v, lq1, lk1, lq2, lk2, subln_g, lam_init, zero_shape):
    b, s, _ = q.shape
    tq = ATT_BLOCK
    hw = 2 * DA_HEAD_DIM
    assert hw == LANES
    n_steps = b * DA_HEADS * (s // tq)
    assert zero_shape[0] % n_steps == 0
    zrows = zero_shape[0] // n_steps
    slopes = tuple(LOG2E * 2.0 ** (-8.0 * (h + 1) / DA_HEADS) for h in range(DA_HEADS))
    vec = lambda n: pl.BlockSpec((1, n), lambda bi, h, i: (0, 0))
    return pl.pallas_call(
        functools.partial(_attn_kernel, lam_init=lam_init, slopes=slopes),
        grid=(b, DA_HEADS, s // tq),
        in_specs=[
            pl.BlockSpec((1, tq, hw), lambda bi, h, i: (bi, i, h)),
            pl.BlockSpec((1, s, hw), lambda bi, h, i: (bi, 0, h)),
            pl.BlockSpec((1, s, hw), lambda bi, h, i: (bi, 0, h)),
            vec(DA_HEAD_DIM), vec(DA_HEAD_DIM), vec(DA_HEAD_DIM), vec(DA_HEAD_DIM), vec(hw),
        ],
        out_specs=[pl.BlockSpec((1, tq, hw), lambda bi, h, i: (bi, i, h)),
                   pl.BlockSpec(memory_space=pl.ANY)],
        out_shape=[jax.ShapeDtypeStruct((b, s, DA_HEADS * hw), BF16),
                   jax.ShapeDtypeStruct(zero_shape, F32)],
        scratch_shapes=[
            pltpu.VMEM((2 * hw, s), BF16),
            pltpu.VMEM((s, 2 * hw), BF16),
            pltpu.VMEM((2 * tq, 2 * hw), BF16),
            pltpu.VMEM((2 * tq, LANES), F32),
            pltpu.VMEM((2 * tq, 2 * hw), F32),
            pltpu.VMEM((2 * tq, tq), F32),
            pltpu.VMEM((2 * tq, tq), F32),
            pltpu.VMEM((zrows,) + tuple(zero_shape[1:]), F32),
            pltpu.SemaphoreType.DMA(()),
        ],
        compiler_params=pltpu.CompilerParams(vmem_limit_bytes=VMEM_LIMIT,
                                             dimension_semantics=("arbitrary",) * 3),
        name="attention",
    )(q, k, v, lq1, lk1, lq2, lk2, subln_g)


def _hgrn_kernel(qd_ref, kd_ref, ke_ref, v_ref, sg_ref, dec_ref, g_ref, o_ref, st_ref):
    @pl.when(pl.program_id(1) == 0)
    def _():
        st_ref[...] = jnp.zeros(st_ref.shape, F32)

    n_ch = qd_ref.shape[1] // CHUNK
    r = lax.broadcasted_iota(I32, (CHUNK, CHUNK), 0)
    c = lax.broadcasted_iota(I32, (CHUNK, CHUNK), 1)
    causal = r >= c

    heads = [slice(h * HG_DK, (h + 1) * HG_DK) for h in range(HG_HEADS)]
    chunks = [slice(ci * CHUNK, (ci + 1) * CHUNK) for ci in range(n_ch)]
    incr = [[_nn(v_ref[0, rows, cols].astype(F32).T.astype(BF16), ke_ref[0, rows, cols])
             for rows in chunks] for cols in heads]
    start = []
    for h, cols in enumerate(heads):
        st = st_ref[h]
        before = []
        for ci in range(n_ch):
            before.append(st.astype(BF16))
            st = st * dec_ref[0, ci:ci + 1, cols] + incr[h][ci]
        st_ref[h] = st
        start.append(before)
    for h, cols in enumerate(heads):
        for ci, rows in enumerate(chunks):
            qd = qd_ref[0, rows, cols]
            scores = jnp.where(causal, _nt(qd, kd_ref[0, rows, cols]), 0.0).astype(BF16)
            o = _nn(scores, v_ref[0, rows, cols]) + _nt(qd, start[h][ci])
            o = o * lax.rsqrt(jnp.mean(o * o, axis=-1, keepdims=True) + EPS) * g_ref[...]
            o_ref[0, rows, cols] = (o * sg_ref[0, rows, cols].astype(F32)).astype(BF16)


def _hgrn(qd, kd, ke, v, sg, dec, norm_g):
    b, s, w = qd.shape
    ts = ROW_TILE
    tok = pl.BlockSpec((1, ts, w), lambda bi, i: (bi, i, 0))
    return pl.pallas_call(
        _hgrn_kernel,
        grid=(b, s // ts),
        in_specs=[tok, tok, tok, tok, tok,
                  pl.BlockSpec((1, ts // CHUNK, w), lambda bi, i: (bi, i, 0)),
                  pl.BlockSpec((1, HG_DV), lambda bi, i: (0, 0))],
        out_specs=tok,
        out_shape=jax.ShapeDtypeStruct((b, s, w), BF16),
        scratch_shapes=[pltpu.VMEM((HG_HEADS, HG_DV, HG_DK), F32)],
        name="hgrn",
    )(qd, kd, ke, v, sg, dec, norm_g)


def _merge_kernel(yda_ref, yhg_ref, gda_ref, ghg_ref, x_ref, gt1_ref, sh2_ref, sc2_ref, g2_ref,
                  wda_ref, whg_ref, wout_ref, wrh_ref, wrl_ref, br_ref,
                  x1_ref, h2_ref, eidx_ref, wts_ref):
    u = (gda_ref[0].astype(F32) * _nn(yda_ref[0], wda_ref[...])
         + ghg_ref[0].astype(F32) * _nn(yhg_ref[0], whg_ref[...]))
    x1 = x_ref[0] + gt1_ref[0] * _nn(u.astype(BF16), wout_ref[...])
    x1_ref[0] = x1
    h2 = x1 * lax.rsqrt(jnp.mean(x1 * x1, axis=-1, keepdims=True) + EPS) * g2_ref[...]
    h2 = h2 * (1.0 + sc2_ref[0]) + sh2_ref[0]
    h2_ref[0] = h2.reshape(h2.shape[0], SUBLANES, h2.shape[1] // SUBLANES)

    h_hi, h_lo = _split(h2)
    wrh = wrh_ref[...]
    lg = _nt(wrh, h_hi) + _nt(wrl_ref[...], h_hi) + _nt(wrh, h_lo) + br_ref[...]

    tm = h2.shape[0]
    eg = EXPERTS_PER_GROUP
    gl = lg[N_EXPERTS:N_EXPERTS + N_GROUPS]
    gmax = jnp.max(gl, axis=0, keepdims=True)
    g_p = 1.0 / jnp.sum(jnp.exp(gl - gmax), axis=0, keepdims=True)
    gi = lax.broadcasted_iota(I32, (N_GROUPS, tm), 0)
    g_idx = jnp.min(jnp.where(gl == gmax, gi, N_GROUPS), axis=0, keepdims=True)
    sel = lg[(N_GROUPS - 1) * eg:N_GROUPS * eg]
    for g in range(N_GROUPS - 2, -1, -1):
        sel = jnp.where(g_idx == g, lg[g * eg:(g + 1) * eg], sel)
    ei = lax.broadcasted_iota(I32, (eg, tm), 0)
    m1 = jnp.max(sel, axis=0, keepdims=True)
    i1 = jnp.min(jnp.where(sel == m1, ei, eg), axis=0, keepdims=True)
    sel2 = jnp.where(ei == i1, -jnp.inf, sel)
    m2 = jnp.max(sel2, axis=0, keepdims=True)
    i2 = jnp.min(jnp.where(sel2 == m2, ei, eg), axis=0, keepdims=True)
    t = jnp.exp(m2 - m1)
    wa = 1.0 / (1.0 + t)
    eidx_ref[0] = jnp.concatenate([g_idx * eg + i1, g_idx * eg + i2], axis=0)
    wts_ref[0] = jnp.concatenate([g_p * wa, g_p * (t * wa)], axis=0)


def _merge(yda, yhg, gda, ghg, x, mod, g2, wda, whg, wout, wrh, wrl, br):
    b, s, d = x.shape
    tm = ROW_TILE
    w = yda.shape[2]
    tok = lambda width: pl.BlockSpec((1, tm, width), lambda bi, i: (bi, i, 0))
    modspec = lambda k: pl.BlockSpec((1, 1, d), lambda bi, i: (bi, 0, k))
    full = lambda a: pl.BlockSpec(a.shape, lambda bi, i: (0,) * a.ndim)
    lane_rows = pl.BlockSpec((1, 2, tm), lambda bi, i: (bi, 0, i))
    return pl.pallas_call(
        _merge_kernel,
        grid=(b, s // tm),
        in_specs=[tok(w), tok(w), tok(d), tok(d), tok(d),
                  modspec(2), modspec(3), modspec(4), full(g2),
                  full(wda), full(whg), full(wout), full(wrh), full(wrl), full(br)],
        out_specs=[tok(d), pl.BlockSpec((1, tm, SUBLANES, d // SUBLANES), lambda bi, i: (bi, i, 0, 0)),
                   lane_rows, lane_rows],
        out_shape=[jax.ShapeDtypeStruct((b, s, d), F32),
                   jax.ShapeDtypeStruct((b, s, SUBLANES, d // SUBLANES), F32),
                   jax.ShapeDtypeStruct((b, 2, s), I32), jax.ShapeDtypeStruct((b, 2, s), F32)],
        compiler_params=pltpu.CompilerParams(vmem_limit_bytes=VMEM_LIMIT),
        name="merge",
    )(yda, yhg, gda, ghg, x, mod, mod, mod, g2, wda, whg, wout, wrh, wrl, br)


def _positions_kernel(e_ref, dest_ref, blk_ref, nused_ref):
    e_mat = e_ref[...]
    nr = e_mat.shape[0]
    nb = blk_ref.shape[1]
    r = lax.broadcasted_iota(I32, (LANES, LANES), 0)
    c = lax.broadcasted_iota(I32, (LANES, LANES), 1)
    incl = jnp.where(r <= c, 1.0, 0.0).astype(BF16)
    ones = jnp.ones((LANES, LANES), BF16)
    rr = lax.broadcasted_iota(I32, (nr, nr), 0)
    cc = lax.broadcasted_iota(I32, (nr, nr), 1)
    before = jnp.where(cc < rr, 1.0, 0.0).astype(BF16)
    blk_start = (lax.broadcasted_iota(I32, (1, nb), 1) * EXPERT_BLOCK).astype(F32)

    def per_expert(e, carry):
        pstart, dest, blk = carry
        hit = e_mat == e
        oh = jnp.where(hit, 1.0, 0.0).astype(BF16)
        in_row = _nn(oh, incl)
        row_tot = _nn(oh, ones)
        row_off = _nn(before, row_tot.astype(BF16))
        total = row_off[nr - 1:nr, :] + row_tot[nr - 1:nr, :]
        dest = jnp.where(hit, pstart + row_off + in_row - 1.0, dest)
        pend = pstart + jnp.floor((total + (EXPERT_BLOCK - 1)) * (1.0 / EXPERT_BLOCK)) * EXPERT_BLOCK
        blk = blk + jnp.where(blk_start >= pend[:, :1], 1.0, 0.0)
        return pend, dest, blk

    pstart0 = jnp.zeros((1, LANES), F32)
    pend, dest, blk = lax.fori_loop(
        0, N_EXPERTS, per_expert,
        (pstart0, jnp.zeros(e_mat.shape, F32), jnp.zeros((1, nb), F32)))
    dest_ref[...] = dest.astype(I32)
    blk_ref[...] = jnp.minimum(blk, N_EXPERTS - 1).astype(I32)
    nused_ref[...] = (pend * (1.0 / EXPERT_BLOCK)).astype(I32)


def _positions(e_mat, n_blk_pad):
    nr = e_mat.shape[0]
    return pl.pallas_call(
        _positions_kernel,
        out_shape=[jax.ShapeDtypeStruct((nr, LANES), I32),
                   jax.ShapeDtypeStruct((1, n_blk_pad), I32),
                   jax.ShapeDtypeStruct((1, LANES), I32)],
        name="positions",
    )(e_mat)


def _dispatch_kernel(dest_ref, h2_ref, xs_in_ref, xs_ref, hbuf, in_sem, out_sem, *, n_tok):
    del xs_in_ref
    i = pl.program_id(0)
    last = pl.num_programs(0) - 1
    slot = i % 2

    def load(step, buf):
        return pltpu.make_async_copy(h2_ref.at[pl.ds(step * GATHER_TILE, GATHER_TILE)], hbuf.at[buf],
                                     in_sem.at[buf])

    def drain(buf):
        for k in range(2):
            pltpu.make_async_copy(hbuf.at[buf], xs_ref.at[pl.ds(0, GATHER_TILE)], out_sem.at[buf, k]).wait()

    @pl.when(i == 0)
    def _():
        load(0, 0).start()

    @pl.when(i > 0)
    def _():
        drain(1 - slot)

    @pl.when(i < last)
    def _():
        load(i + 1, 1 - slot).start()

    load(i, slot).wait()
    base = i * GATHER_TILE

    def issue(r, carry):
        for k in range(2):
            pltpu.make_async_copy(hbuf.at[slot, r], xs_ref.at[dest_ref[k * n_tok + base + r]],
                                  out_sem.at[slot, k]).start(priority=k)
        return carry

    lax.fori_loop(0, GATHER_TILE, issue, 0, unroll=ISSUE_UNROLL)

    @pl.when(i == last)
    def _():
        drain(slot)


def _dispatch(dest, h2, xs_zero):
    n_tok = h2.shape[0]
    return pl.pallas_call(
        functools.partial(_dispatch_kernel, n_tok=n_tok),
        grid_spec=pltpu.PrefetchScalarGridSpec(
            num_scalar_prefetch=1,
            grid=(n_tok // GATHER_TILE,),
            in_specs=[pl.BlockSpec(memory_space=pl.ANY), pl.BlockSpec(memory_space=pl.ANY)],
            out_specs=pl.BlockSpec(memory_space=pl.ANY),
            scratch_shapes=[pltpu.VMEM((2, GATHER_TILE) + h2.shape[1:], F32),
                            pltpu.SemaphoreType.DMA((2,)), pltpu.SemaphoreType.DMA((2, 2))],
        ),
        out_shape=jax.ShapeDtypeStruct(xs_zero.shape, xs_zero.dtype),
        input_output_aliases={2: 0},
        compiler_params=pltpu.CompilerParams(dimension_semantics=("arbitrary",)),
        name="dispatch",
    )(dest, h2, xs_zero)


def _expert_kernel(blk_ref, nused_ref, xs_ref, w1_ref, w3_ref, w2_ref, ys_ref,
                   w1b, w3b, w2b, w1s, w3s, w2s, sem):
    i = pl.program_id(0)
    n_used = nused_ref[0]
    e = blk_ref[i]
    fresh = (i == 0) | (blk_ref[jnp.maximum(i - 1, 0)] != e)

    def fetch(expert):
        return (pltpu.make_async_copy(w1_ref.at[expert], w1s, sem.at[0]),
                pltpu.make_async_copy(w3_ref.at[expert], w3s, sem.at[1]),
                pltpu.make_async_copy(w2_ref.at[expert], w2s, sem.at[2]))

    @pl.when(i < n_used)
    def _():
        @pl.when(i == 0)
        def _():
            for cp in fetch(e):
                cp.start()

        @pl.when(fresh)
        def _():
            for cp in fetch(e):
                cp.wait()
            w1b[...] = w1s[...].astype(BF16)
            w3b[...] = w3s[...].astype(BF16)
            w2b[...] = w2s[...].astype(BF16)
            nxt = lax.while_loop(lambda j: (j < n_used) & (blk_ref[jnp.minimum(j, n_used - 1)] == e),
                                 lambda j: j + 1, i + 1)

            @pl.when(nxt < n_used)
            def _():
                for cp in fetch(blk_ref[jnp.minimum(nxt, n_used - 1)]):
                    cp.start()

        rows = xs_ref.shape[0]
        x = xs_ref[...].reshape(rows, w1b.shape[0]).astype(BF16)
        a = _nn(x, w1b[...])
        y = _nn((_silu(a) * _nn(x, w3b[...])).astype(BF16), w2b[...])
        ys_ref[...] = y.reshape(ys_ref.shape)

    @pl.when(i >= nused_ref[0])
    def _():
        ys_ref[...] = jnp.zeros(ys_ref.shape, F32)


def _experts(blk_e, n_used, xs, w1, w3, w2):
    p = xs.shape[0]
    tile = xs.shape[1:]
    d, de = w1.shape[1], w1.shape[2]
    rows = lambda i, blk, nu: (jnp.minimum(i, nu[0] - 1), 0, 0)
    hbm = pl.BlockSpec(memory_space=pl.ANY)
    return pl.pallas_call(
        _expert_kernel,
        grid_spec=pltpu.PrefetchScalarGridSpec(
            num_scalar_prefetch=2,
            grid=(p // EXPERT_BLOCK,),
            in_specs=[pl.BlockSpec((EXPERT_BLOCK,) + tile, rows), hbm, hbm, hbm],
            out_specs=pl.BlockSpec((EXPERT_BLOCK,) + tile, lambda i, blk, nu: (i, 0, 0)),
            scratch_shapes=[pltpu.VMEM((d, de), BF16), pltpu.VMEM((d, de), BF16),
                            pltpu.VMEM((de, d), BF16),
                            pltpu.VMEM((d, de), F32), pltpu.VMEM((d, de), F32),
                            pltpu.VMEM((de, d), F32), pltpu.SemaphoreType.DMA((3,))],
        ),
        out_shape=jax.ShapeDtypeStruct(xs.shape, F32),
        compiler_params=pltpu.CompilerParams(vmem_limit_bytes=VMEM_LIMIT,
                                             dimension_semantics=("arbitrary",)),
        name="experts",
    )(blk_e, n_used, xs, w1, w3, w2)


def _combine_kernel(dest_ref, ys_ref, x1_ref, gt2_ref, wts_ref, g_ref, o_ref, ybuf, sem, *, n_tok, last):
    i = pl.program_id(0)
    slot = i % 2

    def gather(step, buf):
        base = step * GATHER_TILE

        def issue(r, carry):
            for k in range(2):
                pltpu.make_async_copy(ys_ref.at[dest_ref[k * n_tok + base + r]], ybuf.at[buf, k, r],
                                      sem.at[buf, k]).start(priority=k)
            return carry

        lax.fori_loop(0, GATHER_TILE, issue, 0, unroll=ISSUE_UNROLL)

    @pl.when(i == 0)
    def _():
        gather(0, 0)

    @pl.when(i + 1 < pl.num_programs(0))
    def _():
        gather(i + 1, 1 - slot)

    for k in range(2):
        pltpu.make_async_copy(ys_ref.at[pl.ds(0, GATHER_TILE)], ybuf.at[slot, k], sem.at[slot, k]).wait()

    wts = wts_ref[...]
    ya = ybuf[slot, 0].reshape(x1_ref.shape)
    yb = ybuf[slot, 1].reshape(x1_ref.shape)
    x = x1_ref[...] + gt2_ref[0] * (wts[:, 0:1] * ya + wts[:, 1:2] * yb)
    if last:
        x = x * lax.rsqrt(jnp.mean(x * x, axis=-1, keepdims=True) + EPS) * g_ref[...]
    o_ref[...] = x


def _combine(dest, ys, x1, mod, wts, final_g, seq, last=True):
    n_tok, d = x1.shape
    tt = GATHER_TILE
    per_batch = seq // tt
    return pl.pallas_call(
        functools.partial(_combine_kernel, n_tok=n_tok, last=last),
        grid_spec=pltpu.PrefetchScalarGridSpec(
            num_scalar_prefetch=1,
            grid=(n_tok // tt,),
            in_specs=[pl.BlockSpec(memory_space=pl.ANY),
                      pl.BlockSpec((tt, d), lambda i, dest: (i, 0)),
                      pl.BlockSpec((1, 1, d), lambda i, dest: (i // per_batch, 0, 5)),
                      pl.BlockSpec((tt, 2), lambda i, dest: (i, 0)),
                      pl.BlockSpec((1, d), lambda i, dest: (0, 0))],
            out_specs=pl.BlockSpec((tt, d), lambda i, dest: (i, 0)),
            scratch_shapes=[pltpu.VMEM((2, 2, tt) + ys.shape[1:], F32),
                            pltpu.SemaphoreType.DMA((2, 2))],
        ),
        out_shape=jax.ShapeDtypeStruct((n_tok, d), F32),
        compiler_params=pltpu.CompilerParams(dimension_semantics=("arbitrary",)),
        name="combine",
    )(dest, ys, x1, mod, wts, final_g)


def kernel(x, c, w_ada, b_ada, norm1_g, w_in, lambda_q1, lambda_k1, lambda_q2, lambda_k2, da_subln_g, hg_lb_logits, hg_norm_g, w_up_da, w_up_hg, w_out, norm2_g, w_rg, b_rg, w_re, b_re, w1, w3, w2, final_g):
    b, s, d = x.shape
    n_tok = b * s
    depth = w_ada.shape[0]
    assert s % ROW_TILE == 0 and s % ATT_BLOCK == 0 and s % GATHER_TILE == 0
    assert (2 * n_tok) % LANES == 0 and d == SUBLANES * LANES

    c_pad = jnp.pad(c, ((0, (-b) % 8), (0, 0)))
    n_slots = 2 * n_tok + N_EXPERTS * EXPERT_BLOCK
    n_blk = n_slots // EXPERT_BLOCK
    n_blk_pad = -(-n_blk // LANES) * LANES

    for l in range(depth):
        lam_init = 0.8 - 0.6 * math.exp(-0.3 * l)
        mod = _adaln(c_pad, w_ada[l], b_ada[l][None, :])[:b].reshape(b, 1, 6 * d)

        (qda, kda, vda, qd, kd, ke, hi, sg, dec, gda, ghg) = _inproj(
            x, mod, norm1_g[l][None, :], w_in[l].astype(BF16), hg_lb_logits, l)
        yda, xs_zero = _attention(qda, kda, vda, lambda_q1[l][None, :], lambda_k1[l][None, :],
                                  lambda_q2[l][None, :], lambda_k2[l][None, :], da_subln_g[l][None, :],
                                  lam_init, (n_slots, SUBLANES, d // SUBLANES))
        yhg = _hgrn(qd, kd, ke, hi, sg, dec, hg_norm_g[l][None, :])

        wr = jnp.concatenate([w_re[l].T, w_rg[l].T,
                              jnp.zeros((ROUTER_ROWS - N_EXPERTS - N_GROUPS, d), F32)], axis=0)
        wr_hi = wr.astype(BF16)
        wr_lo = (wr - wr_hi.astype(F32)).astype(BF16)
        br = jnp.concatenate([b_re[l], b_rg[l],
                              jnp.zeros((ROUTER_ROWS - N_EXPERTS - N_GROUPS,), F32)])[:, None]
        x1, h2, eidx, wts = _merge(yda, yhg, gda, ghg, x, mod, norm2_g[l][None, :],
                                   w_up_da[l].astype(BF16), w_up_hg[l].astype(BF16),
                                   w_out[l].astype(BF16), wr_hi, wr_lo, br)

        e_mat = jnp.transpose(eidx, (1, 0, 2)).reshape(2 * n_tok // LANES, LANES)
        dest, blk_e, n_used = _positions(e_mat, n_blk_pad)
        dest = dest.reshape(2 * n_tok)
        h2 = h2.reshape((n_tok,) + h2.shape[2:])
        xs = _dispatch(dest, h2, xs_zero)
        ys = _experts(blk_e[0, :n_blk], n_used[0, :1], xs, w1[l], w3[l], w2[l])
        wts_tok = jnp.transpose(wts, (0, 2, 1)).reshape(n_tok, 2)
        x = _combine(dest, ys, x1.reshape(n_tok, d), mod, wts_tok, final_g[None, :], s,
                     last=(l == depth - 1)).reshape(b, s, d)
    return x
```

```python
import functools
import math

import jax
import jax.numpy as jnp
from jax import lax
from jax.experimental import pallas as pl
from jax.experimental.pallas import tpu as pltpu

F32 = jnp.float32
BF16 = jnp.bfloat16
I32 = jnp.int32

EPS = 1e-6
NEG = -1e30
LOG2E = math.log2(math.e)

CHUNK = 64
DA_HEADS = 4
DA_HEAD_DIM = 64
HG_HEADS = 4
HG_DK = 128
HG_DV = 128
N_GROUPS = 4
EXPERTS_PER_GROUP = 8
N_EXPERTS = N_GROUPS * EXPERTS_PER_GROUP

LANES = 128
SUBLANES = 8
ROW_TILE = 512
ATT_BLOCK = 512
EXPERT_BLOCK = 256
GATHER_TILE = 512
ISSUE_UNROLL = 8
COMBINE_UNROLL = 8
ROUTER_ROWS = 40
VMEM_LIMIT = 52 * 1024 * 1024


def _nt(a, b):
    return lax.dot_general(a, b, (((1,), (1,)), ((), ())), preferred_element_type=F32)


def _nn(a, b):
    return jnp.dot(a, b, preferred_element_type=F32)


def _split(a):
    hi = a.astype(BF16)
    lo = (a - hi.astype(F32)).astype(BF16)
    return hi, lo


def _sigmoid(x):
    return 1.0 / (1.0 + jnp.exp(-x))


def _silu(x):
    return x * _sigmoid(x)


def _adaln_kernel(c_ref, w_ref, b_ref, o_ref):
    c_hi, c_lo = _split(_silu(c_ref[...]))
    w_hi, w_lo = _split(w_ref[...])
    o_ref[...] = _nn(c_hi, w_hi) + _nn(c_lo, w_hi) + _nn(c_hi, w_lo) + b_ref[...]


def _adaln(c_pad, w, b):
    rows, d = c_pad.shape
    n = w.shape[1]
    tn = 1024
    return pl.pallas_call(
        _adaln_kernel,
        grid=(n // tn,),
        in_specs=[
            pl.BlockSpec((rows, d), lambda j: (0, 0)),
            pl.BlockSpec((d, tn), lambda j: (0, j)),
            pl.BlockSpec((1, tn), lambda j: (0, j)),
        ],
        out_specs=pl.BlockSpec((rows, tn), lambda j: (0, j)),
        out_shape=jax.ShapeDtypeStruct((rows, n), F32),
        name="adaln",
    )(c_pad, w, b)


def _inproj_kernel(x_ref, sh_ref, sc_ref, g_ref, w_ref, lb_ref,
                   qda_ref, kda_ref, vda_ref, qd_ref, kd_ref, ke_ref, hi_ref, sg_ref,
                   dec_ref, gda_ref, ghg_ref, *, layer):
    x = x_ref[0]
    tm = x.shape[0]
    h = x * lax.rsqrt(jnp.mean(x * x, axis=-1, keepdims=True) + EPS) * g_ref[...]
    h = h * (1.0 + sc_ref[0]) + sh_ref[0]
    hb = h.astype(BF16)

    def proj(c0, width):
        return _nn(hb, w_ref[:, c0:c0 + width])

    w_da = DA_HEADS * 2 * DA_HEAD_DIM
    w_hg = HG_HEADS * HG_DK
    qda_ref[0] = (proj(0, w_da) * (LOG2E / math.sqrt(DA_HEAD_DIM))).astype(BF16)
    kda_ref[0] = proj(w_da, w_da).astype(BF16)
    vda_ref[0] = proj(2 * w_da, w_da).astype(BF16)
    c0 = 3 * w_da

    lbl = lb_ref[...]
    lbe = jnp.exp(lbl - jnp.max(lbl, axis=0, keepdims=True))
    lbs = lbe / jnp.sum(lbe, axis=0, keepdims=True)
    lb = jnp.sum(lbs[:layer + 1], axis=0, keepdims=True)

    f = lb + (1.0 - lb) * _sigmoid(proj(c0 + w_hg, w_hg))
    logf = jnp.log(f)
    kk = 1.0 - f
    pos = lax.broadcasted_iota(I32, logf.shape, 0) & (CHUNK - 1)
    a = logf
    step = 1
    while step < CHUNK:
        a = a + jnp.where(pos >= step, pltpu.roll(a, step, axis=0), 0.0)
        step *= 2
    n_ch = tm // CHUNK
    a = a.reshape(n_ch, CHUNK, w_hg)
    a_last = a[:, CHUNK - 1:CHUNK, :]
    kk3 = kk.reshape(n_ch, CHUNK, w_hg)
    qq3 = _silu(proj(c0, w_hg)).reshape(n_ch, CHUNK, w_hg)
    qd_ref[0] = (qq3 * jnp.exp(a)).reshape(tm, w_hg).astype(BF16)
    kd_ref[0] = (kk3 * jnp.exp(-a)).reshape(tm, w_hg).astype(BF16)
    ke_ref[0] = (kk3 * jnp.exp(a_last - a)).reshape(tm, w_hg).astype(BF16)
    dec_ref[0] = jnp.exp(a_last).reshape(n_ch, w_hg)
    hi_ref[0] = proj(c0 + 2 * w_hg, w_hg).astype(BF16)
    sg_ref[0] = _silu(proj(c0 + 3 * w_hg, w_hg)).astype(BF16)
    c1 = c0 + 4 * w_hg
    d = x.shape[1]
    gda_ref[0] = _sigmoid(proj(c1, d)).astype(BF16)
    ghg_ref[0] = _sigmoid(proj(c1 + d, d)).astype(BF16)


def _inproj(x, mod, g, w_bf, lb_logits, layer):
    b, s, d = x.shape
    tm = ROW_TILE
    n_cols = w_bf.shape[1]
    w_da = DA_HEADS * 2 * DA_HEAD_DIM
    w_hg = HG_HEADS * HG_DK

    def tok(width, dtype):
        return (pl.BlockSpec((1, tm, width), lambda bi, i: (bi, i, 0)),
                jax.ShapeDtypeStruct((b, s, width), dtype))

    outs = [tok(w_da, BF16)] * 3 + [tok(w_hg, BF16)] * 5
    outs.append((pl.BlockSpec((1, tm // CHUNK, w_hg), lambda bi, i: (bi, i, 0)),
                 jax.ShapeDtypeStruct((b, s // CHUNK, w_hg), F32)))
    outs += [tok(d, BF16)] * 2
    return pl.pallas_call(
        functools.partial(_inproj_kernel, layer=layer),
        grid=(b, s // tm),
        in_specs=[
            pl.BlockSpec((1, tm, d), lambda bi, i: (bi, i, 0)),
            pl.BlockSpec((1, 1, d), lambda bi, i: (bi, 0, 0)),
            pl.BlockSpec((1, 1, d), lambda bi, i: (bi, 0, 1)),
            pl.BlockSpec((1, d), lambda bi, i: (0, 0)),
            pl.BlockSpec((d, n_cols), lambda bi, i: (0, 0), pipeline_mode=pl.Buffered(1)),
            pl.BlockSpec(lb_logits.shape, lambda bi, i: (0, 0)),
        ],
        out_specs=[o[0] for o in outs],
        out_shape=[o[1] for o in outs],
        compiler_params=pltpu.CompilerParams(vmem_limit_bytes=VMEM_LIMIT),
        name="inproj",
    )(x, mod, mod, g, w_bf, lb_logits)


def _attn_kernel(q_ref, k_ref, v_ref, lq1_ref, lk1_ref, lq2_ref, lk2_ref, g_ref, o_ref, zs_ref,
                 kaug_ref, vaug_ref, qaug_ref, m_ref, acc_ref, sa_ref, sb_ref, zero_ref, zsem,
                 *, lam_init, slopes):
    hd = pl.program_id(1)
    i = pl.program_id(2)

    step = (pl.program_id(0) * pl.num_programs(1) + hd) * pl.num_programs(2) + i
    n_steps = pl.num_programs(0) * pl.num_programs(1) * pl.num_programs(2)
    zrows = zero_ref.shape[0]

    def zero_copy(t):
        return pltpu.make_async_copy(zero_ref, zs_ref.at[pl.ds(t * zrows, zrows)], zsem)

    @pl.when(step == 0)
    def _():
        zero_ref[...] = jnp.zeros(zero_ref.shape, F32)

    @pl.when(step > 0)
    def _():
        zero_copy(step - 1).wait()

    zero_copy(step).start()

    @pl.when(step == n_steps - 1)
    def _():
        zero_copy(step).wait()
    tq = q_ref.shape[1]
    hw = q_ref.shape[2]
    n_key = k_ref.shape[1]

    slope = jnp.float32(slopes[-1])
    for idx in range(len(slopes) - 2, -1, -1):
        slope = jnp.where(hd == idx, jnp.float32(slopes[idx]), slope)

    @pl.when(i == 0)
    def _():
        kaug_ref[:hw, :] = k_ref[0].astype(F32).T.astype(BF16)
        jrel = (lax.broadcasted_iota(I32, (hw, n_key), 1) & (tq - 1)).astype(F32) * slope
        krow = lax.broadcasted_iota(I32, (hw, n_key), 0)
        j_hi = jrel.astype(BF16).astype(F32)
        j_mid = (jrel - j_hi).astype(BF16).astype(F32)
        kaug_ref[hw:, :] = jnp.where(
            krow == 0, j_hi,
            jnp.where(krow == 1, j_mid, jnp.where(krow == 2, jrel - j_hi - j_mid, 0.0))).astype(BF16)
        vaug_ref[:, :hw] = v_ref[0]
        vaug_ref[:, hw:] = jnp.ones((n_key, hw), BF16)

    q = q_ref[0]
    lane = lax.broadcasted_iota(I32, q.shape, 1)
    zero = jnp.zeros_like(q)
    ones2 = jnp.where(lane < 3, 1.0, 0.0).astype(BF16)
    qaug_ref[:tq, :hw] = jnp.where(lane < DA_HEAD_DIM, q, zero)
    qaug_ref[tq:, :hw] = jnp.where(lane >= DA_HEAD_DIM, q, zero)
    qaug_ref[:tq, hw:] = ones2
    qaug_ref[tq:, hw:] = ones2

    m_ref[...] = jnp.full(m_ref.shape, NEG, F32)
    acc_ref[...] = jnp.zeros(acc_ref.shape, F32)

    def scores(j, buf):
        buf[...] = _nn(qaug_ref[...], kaug_ref[:, pl.ds(pl.multiple_of(j * tq, tq), tq)])

    def consume(j, buf, diagonal):
        s = buf[...]
        if diagonal:
            ii = lax.broadcasted_iota(I32, s.shape, 0) & (tq - 1)
            jj = lax.broadcasted_iota(I32, s.shape, 1)
            ahead = jnp.minimum(ii - jj, 0).astype(F32)
            s = s + jnp.where((jj // CHUNK) <= (ii // CHUNK), (2.0 * slope) * ahead, NEG)
            shift = jnp.float32(0.0)
        else:
            shift = -slope * ((i - j) * tq).astype(F32)
        m_old = m_ref[...]
        m_new = jnp.maximum(m_old, jnp.max(s, axis=-1, keepdims=True) + shift)
        p = jnp.exp2(s - jnp.concatenate([m_new - shift] * (tq // LANES), axis=1))
        alpha = jnp.concatenate([jnp.exp2(m_old - m_new)] * (acc_ref.shape[1] // LANES), axis=1)
        vb = vaug_ref[pl.ds(pl.multiple_of(j * tq, tq), tq), :]
        acc_ref[...] = alpha * acc_ref[...] + _nn(p.astype(BF16), vb)
        m_ref[...] = m_new

    scores(0, sa_ref)

    def pair(pi, carry):
        j = 2 * pi
        scores(j + 1, sb_ref)
        consume(j, sa_ref, False)
        scores(j + 2, sa_ref)
        consume(j + 1, sb_ref, False)
        return carry

    lax.fori_loop(0, i // 2, pair, 0)

    @pl.when(i % 2 == 0)
    def _():
        consume(i, sa_ref, True)

    @pl.when(i % 2 == 1)
    def _():
        scores(i, sb_ref)
        consume(i - 1, sa_ref, False)
        consume(i, sb_ref, True)

    acc = acc_ref[...]
    o1 = acc[:tq, :hw] / acc[:tq, hw:]
    o2 = acc[tq:, :hw] / acc[tq:, hw:]
    lam = (jnp.exp(jnp.sum(lq1_ref[...] * lk1_ref[...], axis=-1, keepdims=True))
           - jnp.exp(jnp.sum(lq2_ref[...] * lk2_ref[...], axis=-1, keepdims=True)) + lam_init)
    o = o1 - lam * o2
    o = o * lax.rsqrt(jnp.mean(o * o, axis=-1, keepdims=True) + EPS) * g_ref[...]
    o_ref[0] = (o * (1.0 - lam_init)).astype(BF16)


def _attention(q, k, v, lq1, lk1, lq2, lk2, subln_g, lam_init, zero_shape):
    b, s, _ = q.shape
    tq = ATT_BLOCK
    hw = 2 * DA_HEAD_DIM
    assert hw == LANES
    n_steps = b * DA_HEADS * (s // tq)
    assert zero_shape[0] % n_steps == 0
    zrows = zero_shape[0] // n_steps
    slopes = tuple(LOG2E * 2.0 ** (-8.0 * (h + 1) / DA_HEADS) for h in range(DA_HEADS))
    vec = lambda n: pl.BlockSpec((1, n), lambda bi, h, i: (0, 0))
    return pl.pallas_call(
        functools.partial(_attn_kernel, lam_init=lam_init, slopes=slopes),
        grid=(b, DA_HEADS, s // tq),
        in_specs=[
            pl.BlockSpec((1, tq, hw), lambda bi, h, i: (bi, i, h)),
            pl.BlockSpec((1, s, hw), lambda bi, h, i: (bi, 0, h)),
            pl.BlockSpec((1, s, hw), lambda bi, h, i: (bi, 0, h)),
            vec(DA_HEAD_DIM), vec(DA_HEAD_DIM), vec(DA_HEAD_DIM), vec(DA_HEAD_DIM), vec(hw),
        ],
        out_specs=[pl.BlockSpec((1, tq, hw), lambda bi, h, i: (bi, i, h)),
                   pl.BlockSpec(memory_space=pl.ANY)],
        out_shape=[jax.ShapeDtypeStruct((b, s, DA_HEADS * hw), BF16),
                   jax.ShapeDtypeStruct(zero_shape, F32)],
        scratch_shapes=[
            pltpu.VMEM((2 * hw, s), BF16),
            pltpu.VMEM((s, 2 * hw), BF16),
            pltpu.VMEM((2 * tq, 2 * hw), BF16),
            pltpu.VMEM((2 * tq, LANES), F32),
            pltpu.VMEM((2 * tq, 2 * hw), F32),
            pltpu.VMEM((2 * tq, tq), F32),
            pltpu.VMEM((2 * tq, tq), F32),
            pltpu.VMEM((zrows,) + tuple(zero_shape[1:]), F32),
            pltpu.SemaphoreType.DMA(()),
        ],
        compiler_params=pltpu.CompilerParams(vmem_limit_bytes=VMEM_LIMIT,
                                             dimension_semantics=("arbitrary",) * 3),
        name="attention",
    )(q, k, v, lq1, lk1, lq2, lk2, subln_g)


def _hgrn_kernel(qd_ref, kd_ref, ke_ref, v_ref, sg_ref, dec_ref, g_ref, o_ref, st_ref):
    @pl.when(pl.program_id(1) == 0)
    def _():
        st_ref[...] = jnp.zeros(st_ref.shape, F32)

    n_ch = qd_ref.shape[1] // CHUNK
    r = lax.broadcasted_iota(I32, (CHUNK, CHUNK), 0)
    c = lax.broadcasted_iota(I32, (CHUNK, CHUNK), 1)
    causal = r >= c

    heads = [slice(h * HG_DK, (h + 1) * HG_DK) for h in range(HG_HEADS)]
    chunks = [slice(ci * CHUNK, (ci + 1) * CHUNK) for ci in range(n_ch)]
    incr = [[_nn(v_ref[0, rows, cols].astype(F32).T.astype(BF16), ke_ref[0, rows, cols])
             for rows in chunks] for cols in heads]
    start = []
    for h, cols in enumerate(heads):
        st = st_ref[h]
        before = []
        for ci in range(n_ch):
            before.append(st.astype(BF16))
            st = st * dec_ref[0, ci:ci + 1, cols] + incr[h][ci]
        st_ref[h] = st
        start.append(before)
    for h, cols in enumerate(heads):
        for ci, rows in enumerate(chunks):
            qd = qd_ref[0, rows, cols]
            scores = jnp.where(causal, _nt(qd, kd_ref[0, rows, cols]), 0.0).astype(BF16)
            o = _nn(scores, v_ref[0, rows, cols]) + _nt(qd, start[h][ci])
            o = o * lax.rsqrt(jnp.mean(o * o, axis=-1, keepdims=True) + EPS) * g_ref[...]
            o_ref[0, rows, cols] = (o * sg_ref[0, rows, cols].astype(F32)).astype(BF16)


def _hgrn(qd, kd, ke, v, sg, dec, norm_g):
    b, s, w = qd.shape
    ts = ROW_TILE
    tok = pl.BlockSpec((1, ts, w), lambda bi, i: (bi, i, 0))
    return pl.pallas_call(
        _hgrn_kernel,
        grid=(b, s // ts),
        in_specs=[tok, tok, tok, tok, tok,
                  pl.BlockSpec((1, ts // CHUNK, w), lambda bi, i: (bi, i, 0)),
                  pl.BlockSpec((1, HG_DV), lambda bi, i: (0, 0))],
        out_specs=tok,
        out_shape=jax.ShapeDtypeStruct((b, s, w), BF16),
        scratch_shapes=[pltpu.VMEM((HG_HEADS, HG_DV, HG_DK), F32)],
        name="hgrn",
    )(qd, kd, ke, v, sg, dec, norm_g)


def _merge_kernel(yda_ref, yhg_ref, gda_ref, ghg_ref, x_ref, gt1_ref, sh2_ref, sc2_ref, g2_ref,
                  wda_ref, whg_ref, wout_ref, wrh_ref, wrl_ref, br_ref,
                  x1_ref, h2_ref, eidx_ref, wts_ref):
    u = (gda_ref[0].astype(F32) * _nn(yda_ref[0], wda_ref[...])
         + ghg_ref[0].astype(F32) * _nn(yhg_ref[0], whg_ref[...]))
    x1 = x_ref[0] + gt1_ref[0] * _nn(u.astype(BF16), wout_ref[...])
    x1_ref[0] = x1
    h2 = x1 * lax.rsqrt(jnp.mean(x1 * x1, axis=-1, keepdims=True) + EPS) * g2_ref[...]
    h2 = h2 * (1.0 + sc2_ref[0]) + sh2_ref[0]
    h2_ref[0] = h2.reshape(h2.shape[0], SUBLANES, h2.shape[1] // SUBLANES)

    h_hi, h_lo = _split(h2)
    wrh = wrh_ref[...]
    lg = _nt(wrh, h_hi) + _nt(wrl_ref[...], h_hi) + _nt(wrh, h_lo) + br_ref[...]

    tm = h2.shape[0]
    eg = EXPERTS_PER_GROUP
    gl = lg[N_EXPERTS:N_EXPERTS + N_GROUPS]
    gmax = jnp.max(gl, axis=0, keepdims=True)
    g_p = 1.0 / jnp.sum(jnp.exp(gl - gmax), axis=0, keepdims=True)
    gi = lax.broadcasted_iota(I32, (N_GROUPS, tm), 0)
    g_idx = jnp.min(jnp.where(gl == gmax, gi, N_GROUPS), axis=0, keepdims=True)
    sel = lg[(N_GROUPS - 1) * eg:N_GROUPS * eg]
    for g in range(N_GROUPS - 2, -1, -1):
        sel = jnp.where(g_idx == g, lg[g * eg:(g + 1) * eg], sel)
    ei = lax.broadcasted_iota(I32, (eg, tm), 0)
    m1 = jnp.max(sel, axis=0, keepdims=True)
    i1 = jnp.min(jnp.where(sel == m1, ei, eg), axis=0, keepdims=True)
    sel2 = jnp.where(ei == i1, -jnp.inf, sel)
    m2 = jnp.max(sel2, axis=0, keepdims=True)
    i2 = jnp.min(jnp.where(sel2 == m2, ei, eg), axis=0, keepdims=True)
    t = jnp.exp(m2 - m1)
    wa = 1.0 / (1.0 + t)
    eidx_ref[0] = jnp.concatenate([g_idx * eg + i1, g_idx * eg + i2], axis=0)
    wts_ref[0] = jnp.concatenate([g_p * wa, g_p * (t * wa)], axis=0)


def _merge(yda, yhg, gda, ghg, x, mod, g2, wda, whg, wout, wrh, wrl, br):
    b, s, d = x.shape
    tm = ROW_TILE
    w = yda.shape[2]
    tok = lambda width: pl.BlockSpec((1, tm, width), lambda bi, i: (bi, i, 0))
    modspec = lambda k: pl.BlockSpec((1, 1, d), lambda bi, i: (bi, 0, k))
    full = lambda a: pl.BlockSpec(a.shape, lambda bi, i: (0,) * a.ndim)
    lane_rows = pl.BlockSpec((1, 2, tm), lambda bi, i: (bi, 0, i))
    return pl.pallas_call(
        _merge_kernel,
        grid=(b, s // tm),
        in_specs=[tok(w), tok(w), tok(d), tok(d), tok(d),
                  modspec(2), modspec(3), modspec(4), full(g2),
                  full(wda), full(whg), full(wout), full(wrh), full(wrl), full(br)],
        out_specs=[tok(d), pl.BlockSpec((1, tm, SUBLANES, d // SUBLANES), lambda bi, i: (bi, i, 0, 0)),
                   lane_rows, lane_rows],
        out_shape=[jax.ShapeDtypeStruct((b, s, d), F32),
                   jax.ShapeDtypeStruct((b, s, SUBLANES, d // SUBLANES), F32),
                   jax.ShapeDtypeStruct((b, 2, s), I32), jax.ShapeDtypeStruct((b, 2, s), F32)],
        compiler_params=pltpu.CompilerParams(vmem_limit_bytes=VMEM_LIMIT),
        name="merge",
    )(yda, yhg, gda, ghg, x, mod, mod, mod, g2, wda, whg, wout, wrh, wrl, br)


def _positions_kernel(e_ref, dest_ref, blk_ref, nused_ref):
    e_mat = e_ref[...]
    nr = e_mat.shape[0]
    nb = blk_ref.shape[1]
    r = lax.broadcasted_iota(I32, (LANES, LANES), 0)
    c = lax.broadcasted_iota(I32, (LANES, LANES), 1)
    incl = jnp.where(r <= c, 1.0, 0.0).astype(BF16)
    ones = jnp.ones((LANES, LANES), BF16)
    rr = lax.broadcasted_iota(I32, (nr, nr), 0)
    cc = lax.broadcasted_iota(I32, (nr, nr), 1)
    before = jnp.where(cc < rr, 1.0, 0.0).astype(BF16)
    blk_start = (lax.broadcasted_iota(I32, (1, nb), 1) * EXPERT_BLOCK).astype(F32)

    def per_expert(e, carry):
        pstart, dest, blk = carry
        hit = e_mat == e
        oh = jnp.where(hit, 1.0, 0.0).astype(BF16)
        in_row = _nn(oh, incl)
        row_tot = _nn(oh, ones)
        row_off = _nn(before, row_tot.astype(BF16))
        total = row_off[nr - 1:nr, :] + row_tot[nr - 1:nr, :]
        dest = jnp.where(hit, pstart + row_off + in_row - 1.0, dest)
        pend = pstart + jnp.floor((total + (EXPERT_BLOCK - 1)) * (1.0 / EXPERT_BLOCK)) * EXPERT_BLOCK
        blk = blk + jnp.where(blk_start >= pend[:, :1], 1.0, 0.0)
        return pend, dest, blk

    pstart0 = jnp.zeros((1, LANES), F32)
    pend, dest, blk = lax.fori_loop(
        0, N_EXPERTS, per_expert,
        (pstart0, jnp.zeros(e_mat.shape, F32), jnp.zeros((1, nb), F32)))
    dest_ref[...] = dest.astype(I32)
    blk_ref[...] = jnp.minimum(blk, N_EXPERTS - 1).astype(I32)
    nused_ref[...] = (pend * (1.0 / EXPERT_BLOCK)).astype(I32)


def _positions(e_mat, n_blk_pad):
    nr = e_mat.shape[0]
    return pl.pallas_call(
        _positions_kernel,
        out_shape=[jax.ShapeDtypeStruct((nr, LANES), I32),
                   jax.ShapeDtypeStruct((1, n_blk_pad), I32),
                   jax.ShapeDtypeStruct((1, LANES), I32)],
        name="positions",
    )(e_mat)


def _dispatch_kernel(dest_ref, h2_ref, xs_in_ref, xs_ref, hbuf, in_sem, out_sem, *, n_tok):
    del xs_in_ref
    i = pl.program_id(0)
    last = pl.num_programs(0) - 1
    slot = i % 2

    def load(step, buf):
        return pltpu.make_async_copy(h2_ref.at[pl.ds(step * GATHER_TILE, GATHER_TILE)], hbuf.at[buf],
                                     in_sem.at[buf])

    def drain(buf):
        for k in range(2):
            pltpu.make_async_copy(hbuf.at[buf], xs_ref.at[pl.ds(0, GATHER_TILE)], out_sem.at[buf, k]).wait()

    @pl.when(i == 0)
    def _():
        load(0, 0).start()

    @pl.when(i > 0)
    def _():
        drain(1 - slot)

    @pl.when(i < last)
    def _():
        load(i + 1, 1 - slot).start()

    load(i, slot).wait()
    base = i * GATHER_TILE

    def issue(r, carry):
        for k in range(2):
            pltpu.make_async_copy(hbuf.at[slot, r], xs_ref.at[dest_ref[k * n_tok + base + r]],
                                  out_sem.at[slot, k]).start(priority=k)
        return carry

    lax.fori_loop(0, GATHER_TILE, issue, 0, unroll=ISSUE_UNROLL)

    @pl.when(i == last)
    def _():
        drain(slot)


def _dispatch(dest, h2, xs_zero):
    n_tok = h2.shape[0]
    return pl.pallas_call(
        functools.partial(_dispatch_kernel, n_tok=n_tok),
        grid_spec=pltpu.PrefetchScalarGridSpec(
            num_scalar_prefetch=1,
            grid=(n_tok // GATHER_TILE,),
            in_specs=[pl.BlockSpec(memory_space=pl.ANY), pl.BlockSpec(memory_space=pl.ANY)],
            out_specs=pl.BlockSpec(memory_space=pl.ANY),
            scratch_shapes=[pltpu.VMEM((2, GATHER_TILE) + h2.shape[1:], F32),
                            pltpu.SemaphoreType.DMA((2,)), pltpu.SemaphoreType.DMA((2, 2))],
        ),
        out_shape=jax.ShapeDtypeStruct(xs_zero.shape, xs_zero.dtype),
        input_output_aliases={2: 0},
        compiler_params=pltpu.CompilerParams(dimension_semantics=("arbitrary",)),
        name="dispatch",
    )(dest, h2, xs_zero)


def _expert_kernel(blk_ref, nused_ref, xs_ref, w1_ref, w3_ref, w2_ref, ys_ref,
                   w1b, w3b, w2b, w1s, w3s, w2s, sem):
    i = pl.program_id(0)
    n_used = nused_ref[0]
    e = blk_ref[i]
    fresh = (i == 0) | (blk_ref[jnp.maximum(i - 1, 0)] != e)

    def fetch(expert):
        return (pltpu.make_async_copy(w1_ref.at[expert], w1s, sem.at[0]),
                pltpu.make_async_copy(w3_ref.at[expert], w3s, sem.at[1]),
                pltpu.make_async_copy(w2_ref.at[expert], w2s, sem.at[2]))

    @pl.when(i < n_used)
    def _():
        @pl.when(i == 0)
        def _():
            for cp in fetch(e):
                cp.start()

        @pl.when(fresh)
        def _():
            for cp in fetch(e):
                cp.wait()
            w1b[...] = w1s[...].astype(BF16)
            w3b[...] = w3s[...].astype(BF16)
            w2b[...] = w2s[...].astype(BF16)
            nxt = lax.while_loop(lambda j: (j < n_used) & (blk_ref[jnp.minimum(j, n_used - 1)] == e),
                                 lambda j: j + 1, i + 1)

            @pl.when(nxt < n_used)
            def _():
                for cp in fetch(blk_ref[jnp.minimum(nxt, n_used - 1)]):
                    cp.start()

        rows = xs_ref.shape[0]
        x = xs_ref[...].reshape(rows, w1b.shape[0]).astype(BF16)
        a = _nn(x, w1b[...])
        y = _nn((_silu(a) * _nn(x, w3b[...])).astype(BF16), w2b[...])
        ys_ref[...] = y.reshape(ys_ref.shape)

    @pl.when(i >= nused_ref[0])
    def _():
        ys_ref[...] = jnp.zeros(ys_ref.shape, F32)


def _experts(blk_e, n_used, xs, w1, w3, w2):
    p = xs.shape[0]
    tile = xs.shape[1:]
    d, de = w1.shape[1], w1.shape[2]
    rows = lambda i, blk, nu: (jnp.minimum(i, nu[0] - 1), 0, 0)
    hbm = pl.BlockSpec(memory_space=pl.ANY)
    return pl.pallas_call(
        _expert_kernel,
        grid_spec=pltpu.PrefetchScalarGridSpec(
            num_scalar_prefetch=2,
            grid=(p // EXPERT_BLOCK,),
            in_specs=[pl.BlockSpec((EXPERT_BLOCK,) + tile, rows), hbm, hbm, hbm],
            out_specs=pl.BlockSpec((EXPERT_BLOCK,) + tile, lambda i, blk, nu: (i, 0, 0)),
            scratch_shapes=[pltpu.VMEM((d, de), BF16), pltpu.VMEM((d, de), BF16),
                            pltpu.VMEM((de, d), BF16),
                            pltpu.VMEM((d, de), F32), pltpu.VMEM((d, de), F32),
                            pltpu.VMEM((de, d), F32), pltpu.SemaphoreType.DMA((3,))],
        ),
        out_shape=jax.ShapeDtypeStruct(xs.shape, F32),
        compiler_params=pltpu.CompilerParams(vmem_limit_bytes=VMEM_LIMIT,
                                             dimension_semantics=("arbitrary",)),
        name="experts",
    )(blk_e, n_used, xs, w1, w3, w2)


def _combine_kernel(dest_ref, ys_ref, x1_ref, gt2_ref, wts_ref, g_ref, o_ref, ybuf, sem, *, n_tok, last):
    i = pl.program_id(0)
    last_step = pl.num_programs(0) - 1
    slot = i % 2
    d = x1_ref.shape[1]
    n_groups = GATHER_TILE // ISSUE_UNROLL

    def gather(step, buf, r0):
        base = step * GATHER_TILE + r0
        for u in range(ISSUE_UNROLL):
            for k in range(2):
                pltpu.make_async_copy(ys_ref.at[dest_ref[k * n_tok + base + u]], ybuf.at[buf, k, r0 + u],
                                      sem.at[buf, k]).start(priority=k)

    def drain(buf):
        for k in range(2):
            pltpu.make_async_copy(ys_ref.at[pl.ds(0, GATHER_TILE)], ybuf.at[buf, k], sem.at[buf, k]).wait()

    @pl.when(i == 0)
    def _():
        def first(g, carry):
            gather(0, 0, g * ISSUE_UNROLL)
            return carry

        lax.fori_loop(0, n_groups, first, 0)

    drain(slot)

    nxt = jnp.minimum(i + 1, last_step)
    gt2 = gt2_ref[0]
    gain = g_ref[...]

    def group(g, carry):
        r0 = pl.multiple_of(g * ISSUE_UNROLL, ISSUE_UNROLL)
        rows = pl.ds(r0, ISSUE_UNROLL)
        w = wts_ref[rows, :]
        ya = ybuf[slot, 0, rows].reshape(ISSUE_UNROLL, d)
        yb = ybuf[slot, 1, rows].reshape(ISSUE_UNROLL, d)
        x = x1_ref[rows, :] + gt2 * (w[:, 0:1] * ya + w[:, 1:2] * yb)
        if last:
            x = x * lax.rsqrt(jnp.mean(x * x, axis=-1, keepdims=True) + EPS) * gain
        o_ref[rows, :] = x
        gather(nxt, 1 - slot, r0)
        return carry

    lax.fori_loop(0, n_groups, group, 0, unroll=COMBINE_UNROLL)

    @pl.when(i == last_step)
    def _():
        drain(1 - slot)


def _combine(dest, ys, x1, mod, wts, final_g, seq, last=True):
    n_tok, d = x1.shape
    tt = GATHER_TILE
    per_batch = seq // tt
    return pl.pallas_call(
        functools.partial(_combine_kernel, n_tok=n_tok, last=last),
        grid_spec=pltpu.PrefetchScalarGridSpec(
            num_scalar_prefetch=1,
            grid=(n_tok // tt,),
            in_specs=[pl.BlockSpec(memory_space=pl.ANY),
                      pl.BlockSpec((tt, d), lambda i, dest: (i, 0)),
                      pl.BlockSpec((1, 1, d), lambda i, dest: (i // per_batch, 0, 5)),
                      pl.BlockSpec((tt, 2), lambda i, dest: (i, 0)),
                      pl.BlockSpec((1, d), lambda i, dest: (0, 0))],
            out_specs=pl.BlockSpec((tt, d), lambda i, dest: (i, 0)),
            scratch_shapes=[pltpu.VMEM((2, 2, tt) + ys.shape[1:], F32),
                            pltpu.SemaphoreType.DMA((2, 2))],
        ),
        out_shape=jax.ShapeDtypeStruct((n_tok, d), F32),
        compiler_params=pltpu.CompilerParams(dimension_semantics=("arbitrary",)),
        name="combine",
    )(dest, ys, x1, mod, wts, final_g)


def kernel(x, c, w_ada, b_ada, norm1_g, w_in, lambda_q1, lambda_k1, lambda_q2, lambda_k2, da_subln_g, hg_lb_logits, hg_norm_g, w_up_da, w_up_hg, w_out, norm2_g, w_rg, b_rg, w_re, b_re, w1, w3, w2, final_g):
    b, s, d = x.shape
    n_tok = b * s
    depth = w_ada.shape[0]
    assert s % ROW_TILE == 0 and s % ATT_BLOCK == 0 and s % GATHER_TILE == 0
    assert (2 * n_tok) % LANES == 0 and d == SUBLANES * LANES

    c_pad = jnp.pad(c, ((0, (-b) % 8), (0, 0)))
    n_slots = 2 * n_tok + N_EXPERTS * EXPERT_BLOCK
    n_blk = n_slots // EXPERT_BLOCK
    n_blk_pad = -(-n_blk // LANES) * LANES

    for l in range(depth):
        lam_init = 0.8 - 0.6 * math.exp(-0.3 * l)
        mod = _adaln(c_pad, w_ada[l], b_ada[l][None, :])[:b].reshape(b, 1, 6 * d)

        (qda, kda, vda, qd, kd, ke, hi, sg, dec, gda, ghg) = _inproj(
            x, mod, norm1_g[l][None, :], w_in[l].astype(BF16), hg_lb_logits, l)
        yda, xs_zero = _attention(qda, kda, vda, lambda_q1[l][None, :], lambda_k1[l][None, :],
                                  lambda_q2[l][None, :], lambda_k2[l][None, :], da_subln_g[l][None, :],
                                  lam_init, (n_slots, SUBLANES, d // SUBLANES))
        yhg = _hgrn(qd, kd, ke, hi, sg, dec, hg_norm_g[l][None, :])

        wr = jnp.concatenate([w_re[l].T, w_rg[l].T,
                              jnp.zeros((ROUTER_ROWS - N_EXPERTS - N_GROUPS, d), F32)], axis=0)
        wr_hi = wr.astype(BF16)
        wr_lo = (wr - wr_hi.astype(F32)).astype(BF16)
        br = jnp.concatenate([b_re[l], b_rg[l],
                              jnp.zeros((ROUTER_ROWS - N_EXPERTS - N_GROUPS,), F32)])[:, None]
        x1, h2, eidx, wts = _merge(yda, yhg, gda, ghg, x, mod, norm2_g[l][None, :],
                                   w_up_da[l].astype(BF16), w_up_hg[l].astype(BF16),
                                   w_out[l].astype(BF16), wr_hi, wr_lo, br)

        e_mat = jnp.transpose(eidx, (1, 0, 2)).reshape(2 * n_tok // LANES, LANES)
        dest, blk_e, n_used = _positions(e_mat, n_blk_pad)
        dest = dest.reshape(2 * n_tok)
        h2 = h2.reshape((n_tok,) + h2.shape[2:])
        xs = _dispatch(dest, h2, xs_zero)
        ys = _experts(blk_e[0, :n_blk], n_used[0, :1], xs, w1[l], w3[l], w2[l])
        wts_tok = jnp.transpose(wts, (0, 2, 1)).reshape(n_tok, 2)
        x = _combine(dest, ys, x1.reshape(n_tok, d), mod, wts_tok, final_g[None, :], s,
                     last=(l == depth - 1)).reshape(b, s, d)
    return x
```

```python
import functools
import math

import jax
import jax.numpy as jnp
from jax import lax
from jax.experimental import pallas as pl
from jax.experimental.pallas import tpu as pltpu

F32 = jnp.float32
BF16 = jnp.bfloat16
I32 = jnp.int32

EPS = 1e-6
NEG = -1e30
LOG2E = math.log2(math.e)

CHUNK = 64
DA_HEADS = 4
DA_HEAD_DIM = 64
HG_HEADS = 4
HG_DK = 128
HG_DV = 128
N_GROUPS = 4
EXPERTS_PER_GROUP = 8
N_EXPERTS = N_GROUPS * EXPERTS_PER_GROUP

LANES = 128
SUBLANES = 8
ADALN_COLS = 2048
ROW_TILE = 512
ATT_BLOCK = 512
BIAS_TERMS = 3
EXPERT_BLOCK = 256
GATHER_TILE = 512
ISSUE_UNROLL = 8
ROUTER_ROWS = 40
VMEM_LIMIT = 52 * 1024 * 1024


def _nt(a, b):
    return lax.dot_general(a, b, (((1,), (1,)), ((), ())), preferred_element_type=F32)


def _nn(a, b):
    return jnp.dot(a, b, preferred_element_type=F32)


def _split(a):
    hi = a.astype(BF16)
    lo = (a - hi.astype(F32)).astype(BF16)
    return hi, lo


def _sigmoid(x):
    return 1.0 / (1.0 + jnp.exp(-x))


def _silu(x):
    return x * _sigmoid(x)


def _adaln_kernel(c_ref, w_ref, b_ref, o_ref):
    c_hi, c_lo = _split(_silu(c_ref[...]))
    w_hi, w_lo = _split(w_ref[...])
    o_ref[...] = _nn(c_hi, w_hi) + _nn(c_lo, w_hi) + _nn(c_hi, w_lo) + b_ref[...]


def _adaln(c_pad, w, b):
    rows, d = c_pad.shape
    n = w.shape[1]
    tn = ADALN_COLS
    return pl.pallas_call(
        _adaln_kernel,
        grid=(n // tn,),
        in_specs=[
            pl.BlockSpec((rows, d), lambda j: (0, 0)),
            pl.BlockSpec((d, tn), lambda j: (0, j)),
            pl.BlockSpec((1, tn), lambda j: (0, j)),
        ],
        out_specs=pl.BlockSpec((rows, tn), lambda j: (0, j)),
        out_shape=jax.ShapeDtypeStruct((rows, n), F32),
        name="adaln",
    )(c_pad, w, b)


def _inproj_kernel(x_ref, sh_ref, sc_ref, g_ref, w_ref, lb_ref,
                   qda_ref, kda_ref, vda_ref, qd_ref, kd_ref, ke_ref, hi_ref, sg_ref,
                   dec_ref, gda_ref, ghg_ref, *, layer):
    x = x_ref[0]
    tm = x.shape[0]
    h = x * lax.rsqrt(jnp.mean(x * x, axis=-1, keepdims=True) + EPS) * g_ref[...]
    h = h * (1.0 + sc_ref[0]) + sh_ref[0]
    hb = h.astype(BF16)

    def proj(c0, width):
        return _nn(hb, w_ref[:, c0:c0 + width])

    w_da = DA_HEADS * 2 * DA_HEAD_DIM
    w_hg = HG_HEADS * HG_DK
    qda_ref[0] = (proj(0, w_da) * (LOG2E / math.sqrt(DA_HEAD_DIM))).astype(BF16)
    kda_ref[0] = proj(w_da, w_da).astype(BF16)
    vda_ref[0] = proj(2 * w_da, w_da).astype(BF16)
    c0 = 3 * w_da

    lbl = lb_ref[...]
    lbe = jnp.exp(lbl - jnp.max(lbl, axis=0, keepdims=True))
    lbs = lbe / jnp.sum(lbe, axis=0, keepdims=True)
    lb = jnp.sum(lbs[:layer + 1], axis=0, keepdims=True)

    f = lb + (1.0 - lb) * _sigmoid(proj(c0 + w_hg, w_hg))
    logf = jnp.log(f)
    kk = 1.0 - f
    pos = lax.broadcasted_iota(I32, logf.shape, 0) & (CHUNK - 1)
    a = logf
    step = 1
    while step < CHUNK:
        a = a + jnp.where(pos >= step, pltpu.roll(a, step, axis=0), 0.0)
        step *= 2
    n_ch = tm // CHUNK
    a = a.reshape(n_ch, CHUNK, w_hg)
    a_last = a[:, CHUNK - 1:CHUNK, :]
    kk3 = kk.reshape(n_ch, CHUNK, w_hg)
    qq3 = _silu(proj(c0, w_hg)).reshape(n_ch, CHUNK, w_hg)
    qd_ref[0] = (qq3 * jnp.exp(a)).reshape(tm, w_hg).astype(BF16)
    kd_ref[0] = (kk3 * jnp.exp(-a)).reshape(tm, w_hg).astype(BF16)
    ke_ref[0] = (kk3 * jnp.exp(a_last - a)).reshape(tm, w_hg).astype(BF16)
    dec_ref[0] = jnp.exp(a_last).reshape(n_ch, w_hg)
    hi_ref[0] = proj(c0 + 2 * w_hg, w_hg).astype(BF16)
    sg_ref[0] = _silu(proj(c0 + 3 * w_hg, w_hg)).astype(BF16)
    c1 = c0 + 4 * w_hg
    d = x.shape[1]
    gda_ref[0] = _sigmoid(proj(c1, d)).astype(BF16)
    ghg_ref[0] = _sigmoid(proj(c1 + d, d)).astype(BF16)


def _inproj(x, mod, g, w_bf, lb_logits, layer):
    b, s, d = x.shape
    tm = ROW_TILE
    n_cols = w_bf.shape[1]
    w_da = DA_HEADS * 2 * DA_HEAD_DIM
    w_hg = HG_HEADS * HG_DK

    def tok(width, dtype):
        return (pl.BlockSpec((1, tm, width), lambda bi, i: (bi, i, 0)),
                jax.ShapeDtypeStruct((b, s, width), dtype))

    outs = [tok(w_da, BF16)] * 3 + [tok(w_hg, BF16)] * 5
    outs.append((pl.BlockSpec((1, tm // CHUNK, w_hg), lambda bi, i: (bi, i, 0)),
                 jax.ShapeDtypeStruct((b, s // CHUNK, w_hg), F32)))
    outs += [tok(d, BF16)] * 2
    return pl.pallas_call(
        functools.partial(_inproj_kernel, layer=layer),
        grid=(b, s // tm),
        in_specs=[
            pl.BlockSpec((1, tm, d), lambda bi, i: (bi, i, 0)),
            pl.BlockSpec((1, 1, d), lambda bi, i: (bi, 0, 0)),
            pl.BlockSpec((1, 1, d), lambda bi, i: (bi, 0, 1)),
            pl.BlockSpec((1, d), lambda bi, i: (0, 0)),
            pl.BlockSpec((d, n_cols), lambda bi, i: (0, 0), pipeline_mode=pl.Buffered(1)),
            pl.BlockSpec(lb_logits.shape, lambda bi, i: (0, 0)),
        ],
        out_specs=[o[0] for o in outs],
        out_shape=[o[1] for o in outs],
        compiler_params=pltpu.CompilerParams(vmem_limit_bytes=VMEM_LIMIT),
        name="inproj",
    )(x, mod, mod, g, w_bf, lb_logits)


def _attn_kernel(q_ref, k_ref, v_ref, lq1_ref, lk1_ref, lq2_ref, lk2_ref, g_ref, o_ref, zs_ref,
                 kaug_ref, vaug_ref, qaug_ref, m_ref, acc_ref, sa_ref, sb_ref, zero_ref, zsem,
                 *, lam_init, slopes):
    hd = pl.program_id(1)
    i = pl.program_id(2)

    step = (pl.program_id(0) * pl.num_programs(1) + hd) * pl.num_programs(2) + i
    n_steps = pl.num_programs(0) * pl.num_programs(1) * pl.num_programs(2)
    zrows = zero_ref.shape[0]

    def zero_copy(t):
        return pltpu.make_async_copy(zero_ref, zs_ref.at[pl.ds(t * zrows, zrows)], zsem)

    @pl.when(step == 0)
    def _():
        zero_ref[...] = jnp.zeros(zero_ref.shape, F32)

    @pl.when(step > 0)
    def _():
        zero_copy(step - 1).wait()

    zero_copy(step).start()

    @pl.when(step == n_steps - 1)
    def _():
        zero_copy(step).wait()
    tq = q_ref.shape[1]
    hw = q_ref.shape[2]
    n_key = k_ref.shape[1]

    slope = jnp.float32(slopes[-1])
    for idx in range(len(slopes) - 2, -1, -1):
        slope = jnp.where(hd == idx, jnp.float32(slopes[idx]), slope)

    @pl.when(i == 0)
    def _():
        kaug_ref[:hw, :] = k_ref[0].astype(F32).T.astype(BF16)
        jrel = (lax.broadcasted_iota(I32, (hw, n_key), 1) & (tq - 1)).astype(F32) * slope
        krow = lax.broadcasted_iota(I32, (hw, n_key), 0)
        bias_rows = jnp.zeros((hw, n_key), F32)
        rest = jrel
        for term in range(BIAS_TERMS):
            part = rest.astype(BF16).astype(F32)
            bias_rows = jnp.where(krow == term, part, bias_rows)
            rest = rest - part
        kaug_ref[hw:, :] = bias_rows.astype(BF16)
        vaug_ref[:, :hw] = v_ref[0]
        vaug_ref[:, hw:] = jnp.ones((n_key, hw), BF16)

    q = q_ref[0]
    lane = lax.broadcasted_iota(I32, q.shape, 1)
    zero = jnp.zeros_like(q)
    ones = jnp.where(lane < BIAS_TERMS, 1.0, 0.0).astype(BF16)
    qaug_ref[:tq, :hw] = jnp.where(lane < DA_HEAD_DIM, q, zero)
    qaug_ref[tq:, :hw] = jnp.where(lane >= DA_HEAD_DIM, q, zero)
    qaug_ref[:tq, hw:] = ones
    qaug_ref[tq:, hw:] = ones

    m_ref[...] = jnp.full(m_ref.shape, NEG, F32)
    acc_ref[...] = jnp.zeros(acc_ref.shape, F32)

    def scores(j, buf):
        buf[...] = _nn(qaug_ref[...], kaug_ref[:, pl.ds(pl.multiple_of(j * tq, tq), tq)])

    def consume(j, buf, diagonal):
        s = buf[...]
        if diagonal:
            ii = lax.broadcasted_iota(I32, s.shape, 0) & (tq - 1)
            jj = lax.broadcasted_iota(I32, s.shape, 1)
            ahead = jnp.minimum(ii - jj, 0).astype(F32)
            s = s + jnp.where((jj // CHUNK) <= (ii // CHUNK), (2.0 * slope) * ahead, NEG)
            shift = jnp.float32(0.0)
        else:
            shift = -slope * ((i - j) * tq).astype(F32)
        m_old = m_ref[...]
        m_new = jnp.maximum(m_old, jnp.max(s, axis=-1, keepdims=True) + shift)
        p = jnp.exp2(s - jnp.concatenate([m_new - shift] * (tq // LANES), axis=1))
        alpha = jnp.concatenate([jnp.exp2(m_old - m_new)] * (acc_ref.shape[1] // LANES), axis=1)
        vb = vaug_ref[pl.ds(pl.multiple_of(j * tq, tq), tq), :]
        acc_ref[...] = alpha * acc_ref[...] + _nn(p.astype(BF16), vb)
        m_ref[...] = m_new

    scores(0, sa_ref)

    def pair(pi, carry):
        j = 2 * pi
        scores(j + 1, sb_ref)
        consume(j, sa_ref, False)
        scores(j + 2, sa_ref)
        consume(j + 1, sb_ref, False)
        return carry

    lax.fori_loop(0, i // 2, pair, 0)

    @pl.when(i % 2 == 0)
    def _():
        consume(i, sa_ref, True)

    @pl.when(i % 2 == 1)
    def _():
        scores(i, sb_ref)
        consume(i - 1, sa_ref, False)
        consume(i, sb_ref, True)

    acc = acc_ref[...]
    o1 = acc[:tq, :hw] / acc[:tq, hw:]
    o2 = acc[tq:, :hw] / acc[tq:, hw:]
    lam = (jnp.exp(jnp.sum(lq1_ref[...] * lk1_ref[...], axis=-1, keepdims=True))
           - jnp.exp(jnp.sum(lq2_ref[...] * lk2_ref[...], axis=-1, keepdims=True)) + lam_init)
    o = o1 - lam * o2
    o = o * lax.rsqrt(jnp.mean(o * o, axis=-1, keepdims=True) + EPS) * g_ref[...]
    o_ref[0] = (o * (1.0 - lam_init)).astype(BF16)


def _attention(q, k, v, lq1, lk1, lq2, lk2, subln_g, lam_init, zero_shape):
    b, s, _ = q.shape
    tq = ATT_BLOCK
    hw = 2 * DA_HEAD_DIM
    assert hw == LANES
    n_steps = b * DA_HEADS * (s // tq)
    assert zero_shape[0] % n_steps == 0
    zrows = zero_shape[0] // n_steps
    slopes = tuple(LOG2E * 2.0 ** (-8.0 * (h + 1) / DA_HEADS) for h in range(DA_HEADS))
    vec = lambda n: pl.BlockSpec((1, n), lambda bi, h, i: (0, 0))
    return pl.pallas_call(
        functools.partial(_attn_kernel, lam_init=lam_init, slopes=slopes),
        grid=(b, DA_HEADS, s // tq),
        in_specs=[
            pl.BlockSpec((1, tq, hw), lambda bi, h, i: (bi, i, h)),
            pl.BlockSpec((1, s, hw), lambda bi, h, i: (bi, 0, h)),
            pl.BlockSpec((1, s, hw), lambda bi, h, i: (bi, 0, h)),
            vec(DA_HEAD_DIM), vec(DA_HEAD_DIM), vec(DA_HEAD_DIM), vec(DA_HEAD_DIM), vec(hw),
        ],
        out_specs=[pl.BlockSpec((1, tq, hw), lambda bi, h, i: (bi, i, h)),
                   pl.BlockSpec(memory_space=pl.ANY)],
        out_shape=[jax.ShapeDtypeStruct((b, s, DA_HEADS * hw), BF16),
                   jax.ShapeDtypeStruct(zero_shape, F32)],
        scratch_shapes=[
            pltpu.VMEM((2 * hw, s), BF16),
            pltpu.VMEM((s, 2 * hw), BF16),
            pltpu.VMEM((2 * tq, 2 * hw), BF16),
            pltpu.VMEM((2 * tq, LANES), F32),
            pltpu.VMEM((2 * tq, 2 * hw), F32),
            pltpu.VMEM((2 * tq, tq), F32),
            pltpu.VMEM((2 * tq, tq), F32),
            pltpu.VMEM((zrows,) + tuple(zero_shape[1:]), F32),
            pltpu.SemaphoreType.DMA(()),
        ],
        compiler_params=pltpu.CompilerParams(vmem_limit_bytes=VMEM_LIMIT,
                                             dimension_semantics=("arbitrary",) * 3),
        name="attention",
    )(q, k, v, lq1, lk1, lq2, lk2, subln_g)


def _hgrn_kernel(qd_ref, kd_ref, ke_ref, v_ref, sg_ref, dec_ref, g_ref, o_ref, st_ref):
    @pl.when(pl.program_id(1) == 0)
    def _():
        st_ref[...] = jnp.zeros(st_ref.shape, F32)

    n_ch = qd_ref.shape[1] // CHUNK
    r = lax.broadcasted_iota(I32, (CHUNK, CHUNK), 0)
    c = lax.broadcasted_iota(I32, (CHUNK, CHUNK), 1)
    causal = r >= c

    heads = [slice(h * HG_DK, (h + 1) * HG_DK) for h in range(HG_HEADS)]
    chunks = [slice(ci * CHUNK, (ci + 1) * CHUNK) for ci in range(n_ch)]
    incr = [[_nn(v_ref[0, rows, cols].astype(F32).T.astype(BF16), ke_ref[0, rows, cols])
             for rows in chunks] for cols in heads]
    start = []
    for h, cols in enumerate(heads):
        st = st_ref[h]
        before = []
        for ci in range(n_ch):
            before.append(st.astype(BF16))
            st = st * dec_ref[0, ci:ci + 1, cols] + incr[h][ci]
        st_ref[h] = st
        start.append(before)
    for h, cols in enumerate(heads):
        for ci, rows in enumerate(chunks):
            qd = qd_ref[0, rows, cols]
            scores = jnp.where(causal, _nt(qd, kd_ref[0, rows, cols]), 0.0).astype(BF16)
            o = _nn(scores, v_ref[0, rows, cols]) + _nt(qd, start[h][ci])
            o = o * lax.rsqrt(jnp.mean(o * o, axis=-1, keepdims=True) + EPS) * g_ref[...]
            o_ref[0, rows, cols] = (o * sg_ref[0, rows, cols].astype(F32)).astype(BF16)


def _hgrn(qd, kd, ke, v, sg, dec, norm_g):
    b, s, w = qd.shape
    ts = ROW_TILE
    tok = pl.BlockSpec((1, ts, w), lambda bi, i: (bi, i, 0))
    return pl.pallas_call(
        _hgrn_kernel,
        grid=(b, s // ts),
        in_specs=[tok, tok, tok, tok, tok,
                  pl.BlockSpec((1, ts // CHUNK, w), lambda bi, i: (bi, i, 0)),
                  pl.BlockSpec((1, HG_DV), lambda bi, i: (0, 0))],
        out_specs=tok,
        out_shape=jax.ShapeDtypeStruct((b, s, w), BF16),
        scratch_shapes=[pltpu.VMEM((HG_HEADS, HG_DV, HG_DK), F32)],
        name="hgrn",
    )(qd, kd, ke, v, sg, dec, norm_g)


def _merge_kernel(yda_ref, yhg_ref, gda_ref, ghg_ref, x_ref, gt1_ref, sh2_ref, sc2_ref, g2_ref,
                  wda_ref, whg_ref, wout_ref, wrh_ref, wrl_ref, br_ref,
                  x1_ref, h2_ref, eidx_ref, wts_ref):
    u = (gda_ref[0].astype(F32) * _nn(yda_ref[0], wda_ref[...])
         + ghg_ref[0].astype(F32) * _nn(yhg_ref[0], whg_ref[...]))
    x1 = x_ref[0] + gt1_ref[0] * _nn(u.astype(BF16), wout_ref[...])
    x1_ref[0] = x1
    h2 = x1 * lax.rsqrt(jnp.mean(x1 * x1, axis=-1, keepdims=True) + EPS) * g2_ref[...]
    h2 = h2 * (1.0 + sc2_ref[0]) + sh2_ref[0]
    h2_ref[0] = h2.reshape(h2.shape[0], SUBLANES, h2.shape[1] // SUBLANES)

    h_hi, h_lo = _split(h2)
    wrh = wrh_ref[...]
    lg = _nt(wrh, h_hi) + _nt(wrl_ref[...], h_hi) + _nt(wrh, h_lo) + br_ref[...]

    tm = h2.shape[0]
    eg = EXPERTS_PER_GROUP
    gl = lg[N_EXPERTS:N_EXPERTS + N_GROUPS]
    gmax = jnp.max(gl, axis=0, keepdims=True)
    g_p = 1.0 / jnp.sum(jnp.exp(gl - gmax), axis=0, keepdims=True)
    gi = lax.broadcasted_iota(I32, (N_GROUPS, tm), 0)
    g_idx = jnp.min(jnp.where(gl == gmax, gi, N_GROUPS), axis=0, keepdims=True)
    sel = lg[(N_GROUPS - 1) * eg:N_GROUPS * eg]
    for g in range(N_GROUPS - 2, -1, -1):
        sel = jnp.where(g_idx == g, lg[g * eg:(g + 1) * eg], sel)
    ei = lax.broadcasted_iota(I32, (eg, tm), 0)
    m1 = jnp.max(sel, axis=0, keepdims=True)
    i1 = jnp.min(jnp.where(sel == m1, ei, eg), axis=0, keepdims=True)
    sel2 = jnp.where(ei == i1, -jnp.inf, sel)
    m2 = jnp.max(sel2, axis=0, keepdims=True)
    i2 = jnp.min(jnp.where(sel2 == m2, ei, eg), axis=0, keepdims=True)
    t = jnp.exp(m2 - m1)
    wa = 1.0 / (1.0 + t)
    eidx_ref[0] = jnp.concatenate([g_idx * eg + i1, g_idx * eg + i2], axis=0)
    wts_ref[0] = jnp.concatenate([g_p * wa, g_p * (t * wa)], axis=0)


def _merge(yda, yhg, gda, ghg, x, mod, g2, wda, whg, wout, wrh, wrl, br):
    b, s, d = x.shape
    tm = ROW_TILE
    w = yda.shape[2]
    tok = lambda width: pl.BlockSpec((1, tm, width), lambda bi, i: (bi, i, 0))
    modspec = lambda k: pl.BlockSpec((1, 1, d), lambda bi, i: (bi, 0, k))
    full = lambda a: pl.BlockSpec(a.shape, lambda bi, i: (0,) * a.ndim)
    lane_rows = pl.BlockSpec((1, 2, tm), lambda bi, i: (bi, 0, i))
    return pl.pallas_call(
        _merge_kernel,
        grid=(b, s // tm),
        in_specs=[tok(w), tok(w), tok(d), tok(d), tok(d),
                  modspec(2), modspec(3), modspec(4), full(g2),
                  full(wda), full(whg), full(wout), full(wrh), full(wrl), full(br)],
        out_specs=[tok(d), pl.BlockSpec((1, tm, SUBLANES, d // SUBLANES), lambda bi, i: (bi, i, 0, 0)),
                   lane_rows, lane_rows],
        out_shape=[jax.ShapeDtypeStruct((b, s, d), F32),
                   jax.ShapeDtypeStruct((b, s, SUBLANES, d // SUBLANES), F32),
                   jax.ShapeDtypeStruct((b, 2, s), I32), jax.ShapeDtypeStruct((b, 2, s), F32)],
        compiler_params=pltpu.CompilerParams(vmem_limit_bytes=VMEM_LIMIT),
        name="merge",
    )(yda, yhg, gda, ghg, x, mod, mod, mod, g2, wda, whg, wout, wrh, wrl, br)


def _positions_kernel(e_ref, dest_ref, blk_ref, nused_ref):
    e_mat = e_ref[...]
    nr = e_mat.shape[0]
    nb = blk_ref.shape[1]
    r = lax.broadcasted_iota(I32, (LANES, LANES), 0)
    c = lax.broadcasted_iota(I32, (LANES, LANES), 1)
    incl = jnp.where(r <= c, 1.0, 0.0).astype(BF16)
    ones = jnp.ones((LANES, LANES), BF16)
    rr = lax.broadcasted_iota(I32, (nr, nr), 0)
    cc = lax.broadcasted_iota(I32, (nr, nr), 1)
    before = jnp.where(cc < rr, 1.0, 0.0).astype(BF16)
    blk_start = (lax.broadcasted_iota(I32, (1, nb), 1) * EXPERT_BLOCK).astype(F32)

    def per_expert(e, carry):
        pstart, dest, blk = carry
        hit = e_mat == e
        oh = jnp.where(hit, 1.0, 0.0).astype(BF16)
        in_row = _nn(oh, incl)
        row_tot = _nn(oh, ones)
        row_off = _nn(before, row_tot.astype(BF16))
        total = row_off[nr - 1:nr, :] + row_tot[nr - 1:nr, :]
        dest = jnp.where(hit, pstart + row_off + in_row - 1.0, dest)
        pend = pstart + jnp.floor((total + (EXPERT_BLOCK - 1)) * (1.0 / EXPERT_BLOCK)) * EXPERT_BLOCK
        blk = blk + jnp.where(blk_start >= pend[:, :1], 1.0, 0.0)
        return pend, dest, blk

    pstart0 = jnp.zeros((1, LANES), F32)
    pend, dest, blk = lax.fori_loop(
        0, N_EXPERTS, per_expert,
        (pstart0, jnp.zeros(e_mat.shape, F32), jnp.zeros((1, nb), F32)), unroll=4)
    dest_ref[...] = dest.astype(I32)
    blk_ref[...] = jnp.minimum(blk, N_EXPERTS - 1).astype(I32)
    nused_ref[...] = (pend * (1.0 / EXPERT_BLOCK)).astype(I32)


def _positions(e_mat, n_blk_pad):
    nr = e_mat.shape[0]
    return pl.pallas_call(
        _positions_kernel,
        out_shape=[jax.ShapeDtypeStruct((nr, LANES), I32),
                   jax.ShapeDtypeStruct((1, n_blk_pad), I32),
                   jax.ShapeDtypeStruct((1, LANES), I32)],
        name="positions",
    )(e_mat)


def _dispatch_kernel(dest_ref, h2_ref, xs_in_ref, xs_ref, hbuf, in_sem, out_sem, *, n_tok):
    del xs_in_ref
    i = pl.program_id(0)
    last = pl.num_programs(0) - 1
    slot = i % 2

    def load(step, buf):
        return pltpu.make_async_copy(h2_ref.at[pl.ds(step * GATHER_TILE, GATHER_TILE)], hbuf.at[buf],
                                     in_sem.at[buf])

    def drain(buf):
        for k in range(2):
            pltpu.make_async_copy(hbuf.at[buf], xs_ref.at[pl.ds(0, GATHER_TILE)], out_sem.at[buf, k]).wait()

    @pl.when(i == 0)
    def _():
        load(0, 0).start()

    @pl.when(i > 0)
    def _():
        drain(1 - slot)

    @pl.when(i < last)
    def _():
        load(i + 1, 1 - slot).start()

    load(i, slot).wait()
    base = i * GATHER_TILE

    def issue(r, carry):
        for k in range(2):
            pltpu.make_async_copy(hbuf.at[slot, r], xs_ref.at[dest_ref[k * n_tok + base + r]],
                                  out_sem.at[slot, k]).start(priority=k)
        return carry

    lax.fori_loop(0, GATHER_TILE, issue, 0, unroll=ISSUE_UNROLL)

    @pl.when(i == last)
    def _():
        drain(slot)


def _dispatch(dest, h2, xs_zero):
    n_tok = h2.shape[0]
    return pl.pallas_call(
        functools.partial(_dispatch_kernel, n_tok=n_tok),
        grid_spec=pltpu.PrefetchScalarGridSpec(
            num_scalar_prefetch=1,
            grid=(n_tok // GATHER_TILE,),
            in_specs=[pl.BlockSpec(memory_space=pl.ANY), pl.BlockSpec(memory_space=pl.ANY)],
            out_specs=pl.BlockSpec(memory_space=pl.ANY),
            scratch_shapes=[pltpu.VMEM((2, GATHER_TILE) + h2.shape[1:], F32),
                            pltpu.SemaphoreType.DMA((2,)), pltpu.SemaphoreType.DMA((2, 2))],
        ),
        out_shape=jax.ShapeDtypeStruct(xs_zero.shape, xs_zero.dtype),
        input_output_aliases={2: 0},
        compiler_params=pltpu.CompilerParams(dimension_semantics=("arbitrary",)),
        name="dispatch",
    )(dest, h2, xs_zero)


def _expert_kernel(blk_ref, nused_ref, xs_ref, w1_ref, w3_ref, w2_ref, ys_ref,
                   w1b, w3b, w2b, w1s, w3s, w2s, sem):
    i = pl.program_id(0)
    n_used = nused_ref[0]
    e = blk_ref[i]
    fresh = (i == 0) | (blk_ref[jnp.maximum(i - 1, 0)] != e)

    def fetch(expert):
        return (pltpu.make_async_copy(w1_ref.at[expert], w1s, sem.at[0]),
                pltpu.make_async_copy(w3_ref.at[expert], w3s, sem.at[1]),
                pltpu.make_async_copy(w2_ref.at[expert], w2s, sem.at[2]))

    @pl.when(i < n_used)
    def _():
        @pl.when(i == 0)
        def _():
            for cp in fetch(e):
                cp.start()

        @pl.when(fresh)
        def _():
            for cp in fetch(e):
                cp.wait()
            w1b[...] = w1s[...].astype(BF16)
            w3b[...] = w3s[...].astype(BF16)
            w2b[...] = w2s[...].astype(BF16)
            nxt = lax.while_loop(lambda j: (j < n_used) & (blk_ref[jnp.minimum(j, n_used - 1)] == e),
                                 lambda j: j + 1, i + 1)

            @pl.when(nxt < n_used)
            def _():
                for cp in fetch(blk_ref[jnp.minimum(nxt, n_used - 1)]):
                    cp.start()

        rows = xs_ref.shape[0]
        x = xs_ref[...].reshape(rows, w1b.shape[0]).astype(BF16)
        a = _nn(x, w1b[...])
        y = _nn((_silu(a) * _nn(x, w3b[...])).astype(BF16), w2b[...])
        ys_ref[...] = y.reshape(ys_ref.shape)

    @pl.when(i >= nused_ref[0])
    def _():
        ys_ref[...] = jnp.zeros(ys_ref.shape, F32)


def _experts(blk_e, n_used, xs, w1, w3, w2):
    p = xs.shape[0]
    tile = xs.shape[1:]
    d, de = w1.shape[1], w1.shape[2]
    rows = lambda i, blk, nu: (jnp.minimum(i, nu[0] - 1), 0, 0)
    hbm = pl.BlockSpec(memory_space=pl.ANY)
    return pl.pallas_call(
        _expert_kernel,
        grid_spec=pltpu.PrefetchScalarGridSpec(
            num_scalar_prefetch=2,
            grid=(p // EXPERT_BLOCK,),
            in_specs=[pl.BlockSpec((EXPERT_BLOCK,) + tile, rows), hbm, hbm, hbm],
            out_specs=pl.BlockSpec((EXPERT_BLOCK,) + tile, lambda i, blk, nu: (i, 0, 0)),
            scratch_shapes=[pltpu.VMEM((d, de), BF16), pltpu.VMEM((d, de), BF16),
                            pltpu.VMEM((de, d), BF16),
                            pltpu.VMEM((d, de), F32), pltpu.VMEM((d, de), F32),
                            pltpu.VMEM((de, d), F32), pltpu.SemaphoreType.DMA((3,))],
        ),
        out_shape=jax.ShapeDtypeStruct(xs.shape, F32),
        compiler_params=pltpu.CompilerParams(vmem_limit_bytes=VMEM_LIMIT,
                                             dimension_semantics=("arbitrary",)),
        name="experts",
    )(blk_e, n_used, xs, w1, w3, w2)


def _combine_kernel(dest_ref, ys_ref, x1_ref, gt2_ref, wts_ref, g_ref, o_ref, ybuf, sem, *, n_tok, last):
    i = pl.program_id(0)
    slot = i % 2

    def gather(step, buf):
        base = step * GATHER_TILE

        def issue(r, carry):
            for k in range(2):
                pltpu.make_async_copy(ys_ref.at[dest_ref[k * n_tok + base + r]], ybuf.at[buf, k, r],
                                      sem.at[buf, k]).start(priority=k)
            return carry

        lax.fori_loop(0, GATHER_TILE, issue, 0, unroll=ISSUE_UNROLL)

    @pl.when(i == 0)
    def _():
        gather(0, 0)

    @pl.when(i + 1 < pl.num_programs(0))
    def _():
        gather(i + 1, 1 - slot)

    for k in range(2):
        pltpu.make_async_copy(ys_ref.at[pl.ds(0, GATHER_TILE)], ybuf.at[slot, k], sem.at[slot, k]).wait()

    wts = wts_ref[...]
    ya = ybuf[slot, 0].reshape(x1_ref.shape)
    yb = ybuf[slot, 1].reshape(x1_ref.shape)
    x = x1_ref[...] + gt2_ref[0] * (wts[:, 0:1] * ya + wts[:, 1:2] * yb)
    if last:
        x = x * lax.rsqrt(jnp.mean(x * x, axis=-1, keepdims=True) + EPS) * g_ref[...]
    o_ref[...] = x


def _combine(dest, ys, x1, mod, wts, final_g, seq, last=True):
    n_tok, d = x1.shape
    tt = GATHER_TILE
    per_batch = seq // tt
    return pl.pallas_call(
        functools.partial(_combine_kernel, n_tok=n_tok, last=last),
        grid_spec=pltpu.PrefetchScalarGridSpec(
            num_scalar_prefetch=1,
            grid=(n_tok // tt,),
            in_specs=[pl.BlockSpec(memory_space=pl.ANY),
                      pl.BlockSpec((tt, d), lambda i, dest: (i, 0)),
                      pl.BlockSpec((1, 1, d), lambda i, dest: (i // per_batch, 0, 5)),
                      pl.BlockSpec((tt, 2), lambda i, dest: (i, 0)),
                      pl.BlockSpec((1, d), lambda i, dest: (0, 0))],
            out_specs=pl.BlockSpec((tt, d), lambda i, dest: (i, 0)),
            scratch_shapes=[pltpu.VMEM((2, 2, tt) + ys.shape[1:], F32),
                            pltpu.SemaphoreType.DMA((2, 2))],
        ),
        out_shape=jax.ShapeDtypeStruct((n_tok, d), F32),
        compiler_params=pltpu.CompilerParams(dimension_semantics=("arbitrary",)),
        name="combine",
    )(dest, ys, x1, mod, wts, final_g)


def kernel(x, c, w_ada, b_ada, norm1_g, w_in, lambda_q1, lambda_k1, lambda_q2, lambda_k2, da_subln_g, hg_lb_logits, hg_norm_g, w_up_da, w_up_hg, w_out, norm2_g, w_rg, b_rg, w_re, b_re, w1, w3, w2, final_g):
    b, s, d = x.shape
    n_tok = b * s
    depth = w_ada.shape[0]
    assert s % ROW_TILE == 0 and s % ATT_BLOCK == 0 and s % GATHER_TILE == 0
    assert (2 * n_tok) % LANES == 0 and d == SUBLANES * LANES

    c_pad = jnp.pad(c, ((0, (-b) % 8), (0, 0)))
    n_slots = 2 * n_tok + N_EXPERTS * EXPERT_BLOCK
    n_blk = n_slots // EXPERT_BLOCK
    n_blk_pad = -(-n_blk // LANES) * LANES

    for l in range(depth):
        lam_init = 0.8 - 0.6 * math.exp(-0.3 * l)
        mod = _adaln(c_pad, w_ada[l], b_ada[l][None, :])[:b].reshape(b, 1, 6 * d)

        (qda, kda, vda, qd, kd, ke, hi, sg, dec, gda, ghg) = _inproj(
            x, mod, norm1_g[l][None, :], w_in[l].astype(BF16), hg_lb_logits, l)
        yda, xs_zero = _attention(qda, kda, vda, lambda_q1[l][None, :], lambda_k1[l][None, :],
                                  lambda_q2[l][None, :], lambda_k2[l][None, :], da_subln_g[l][None, :],
                                  lam_init, (n_slots, SUBLANES, d // SUBLANES))
        yhg = _hgrn(qd, kd, ke, hi, sg, dec, hg_norm_g[l][None, :])

        wr = jnp.concatenate([w_re[l].T, w_rg[l].T,
                              jnp.zeros((ROUTER_ROWS - N_EXPERTS - N_GROUPS, d), F32)], axis=0)
        wr_hi = wr.astype(BF16)
        wr_lo = (wr - wr_hi.astype(F32)).astype(BF16)
        br = jnp.concatenate([b_re[l], b_rg[l],
                              jnp.zeros((ROUTER_ROWS - N_EXPERTS - N_GROUPS,), F32)])[:, None]
        x1, h2, eidx, wts = _merge(yda, yhg, gda, ghg, x, mod, norm2_g[l][None, :],
                                   w_up_da[l].astype(BF16), w_up_hg[l].astype(BF16),
                                   w_out[l].astype(BF16), wr_hi, wr_lo, br)

        e_mat = jnp.transpose(eidx, (1, 0, 2)).reshape(2 * n_tok // LANES, LANES)
        dest, blk_e, n_used = _positions(e_mat, n_blk_pad)
        dest = dest.reshape(2 * n_tok)
        h2 = h2.reshape((n_tok,) + h2.shape[2:])
        xs = _dispatch(dest, h2, xs_zero)
        ys = _experts(blk_e[0, :n_blk], n_used[0, :1], xs, w1[l], w3[l], w2[l])
        wts_tok = jnp.transpose(wts, (0, 2, 1)).reshape(n_tok, 2)
        x = _combine(dest, ys, x1.reshape(n_tok, d), mod, wts_tok, final_g[None, :], s,
                     last=(l == depth - 1)).reshape(b, s, d)
    return x
```

```python
import functools
import math

import jax
import jax.numpy as jnp
from jax import lax
from jax.experimental import pallas as pl
from jax.experimental.pallas import tpu as pltpu

F32 = jnp.float32
BF16 = jnp.bfloat16
I32 = jnp.int32

EPS = 1e-6
NEG = -1e30
LOG2E = math.log2(math.e)

CHUNK = 64
DA_HEADS = 4
DA_HEAD_DIM = 64
HG_HEADS = 4
HG_DK = 128
HG_DV = 128
N_GROUPS = 4
EXPERTS_PER_GROUP = 8
N_EXPERTS = N_GROUPS * EXPERTS_PER_GROUP

LANES = 128
SUBLANES = 8
ADALN_COLS = 1024
ROW_TILE = 512
ATT_BLOCK = 512
BIAS_TERMS = 3
EXPERT_BLOCK = 256
GATHER_TILE = 512
ISSUE_UNROLL = 8
ROUTER_ROWS = 48
VMEM_LIMIT = 52 * 1024 * 1024


def _nt(a, b):
    return lax.dot_general(a, b, (((1,), (1,)), ((), ())), preferred_element_type=F32)


def _nn(a, b):
    return jnp.dot(a, b, preferred_element_type=F32)


def _split(a):
    hi = a.astype(BF16)
    lo = (a - hi.astype(F32)).astype(BF16)
    return hi, lo


def _sigmoid(x):
    return 1.0 / (1.0 + jnp.exp(-x))


def _silu(x):
    return x * _sigmoid(x)


def _adaln_kernel(c_ref, w_ref, b_ref, o_ref):
    c_hi, c_lo = _split(_silu(c_ref[...]))
    w_hi, w_lo = _split(w_ref[...])
    o_ref[...] = _nn(c_hi, w_hi) + _nn(c_lo, w_hi) + _nn(c_hi, w_lo) + b_ref[...]


def _adaln(c_pad, w, b):
    rows, d = c_pad.shape
    n = w.shape[1]
    tn = ADALN_COLS
    return pl.pallas_call(
        _adaln_kernel,
        grid=(n // tn,),
        in_specs=[
            pl.BlockSpec((rows, d), lambda j: (0, 0)),
            pl.BlockSpec((d, tn), lambda j: (0, j)),
            pl.BlockSpec((1, tn), lambda j: (0, j)),
        ],
        out_specs=pl.BlockSpec((rows, tn), lambda j: (0, j)),
        out_shape=jax.ShapeDtypeStruct((rows, n), F32),
        name="adaln",
    )(c_pad, w, b)


def _inproj_kernel(x_ref, sh_ref, sc_ref, g_ref, w_ref, lb_ref,
                   qda_ref, kda_ref, vda_ref, qd_ref, kd_ref, ke_ref, hi_ref, sg_ref,
                   dec_ref, gda_ref, ghg_ref, *, layer):
    x = x_ref[0]
    tm = x.shape[0]
    h = x * lax.rsqrt(jnp.mean(x * x, axis=-1, keepdims=True) + EPS) * g_ref[...]
    h = h * (1.0 + sc_ref[0]) + sh_ref[0]
    hb = h.astype(BF16)

    def proj(c0, width):
        return _nn(hb, w_ref[:, c0:c0 + width])

    w_da = DA_HEADS * 2 * DA_HEAD_DIM
    w_hg = HG_HEADS * HG_DK
    qda_ref[0] = (proj(0, w_da) * (LOG2E / math.sqrt(DA_HEAD_DIM))).astype(BF16)
    kda_ref[0] = proj(w_da, w_da).astype(BF16)
    vda_ref[0] = proj(2 * w_da, w_da).astype(BF16)
    c0 = 3 * w_da

    lbl = lb_ref[...]
    lbe = jnp.exp(lbl - jnp.max(lbl, axis=0, keepdims=True))
    lbs = lbe / jnp.sum(lbe, axis=0, keepdims=True)
    lb = jnp.sum(lbs[:layer + 1], axis=0, keepdims=True)

    f = lb + (1.0 - lb) * _sigmoid(proj(c0 + w_hg, w_hg))
    logf = jnp.log(f)
    kk = 1.0 - f
    pos = lax.broadcasted_iota(I32, logf.shape, 0) & (CHUNK - 1)
    a = logf
    step = 1
    while step < CHUNK:
        a = a + jnp.where(pos >= step, pltpu.roll(a, step, axis=0), 0.0)
        step *= 2
    n_ch = tm // CHUNK
    a = a.reshape(n_ch, CHUNK, w_hg)
    a_last = a[:, CHUNK - 1:CHUNK, :]
    kk3 = kk.reshape(n_ch, CHUNK, w_hg)
    qq3 = _silu(proj(c0, w_hg)).reshape(n_ch, CHUNK, w_hg)
    qd_ref[0] = (qq3 * jnp.exp(a)).reshape(tm, w_hg).astype(BF16)
    kd_ref[0] = (kk3 * jnp.exp(-a)).reshape(tm, w_hg).astype(BF16)
    ke_ref[0] = (kk3 * jnp.exp(a_last - a)).reshape(tm, w_hg).astype(BF16)
    dec_ref[0] = jnp.exp(a_last).reshape(n_ch, w_hg)
    hi_ref[0] = proj(c0 + 2 * w_hg, w_hg).astype(BF16)
    sg_ref[0] = _silu(proj(c0 + 3 * w_hg, w_hg)).astype(BF16)
    c1 = c0 + 4 * w_hg
    d = x.shape[1]
    gda_ref[0] = _sigmoid(proj(c1, d)).astype(BF16)
    ghg_ref[0] = _sigmoid(proj(c1 + d, d)).astype(BF16)


def _inproj(x, mod, g, w_bf, lb_logits, layer):
    b, s, d = x.shape
    tm = ROW_TILE
    n_cols = w_bf.shape[1]
    w_da = DA_HEADS * 2 * DA_HEAD_DIM
    w_hg = HG_HEADS * HG_DK

    def tok(width, dtype):
        return (pl.BlockSpec((1, tm, width), lambda bi, i: (bi, i, 0)),
                jax.ShapeDtypeStruct((b, s, width), dtype))

    outs = [tok(w_da, BF16)] * 3 + [tok(w_hg, BF16)] * 5
    outs.append((pl.BlockSpec((1, tm // CHUNK, w_hg), lambda bi, i: (bi, i, 0)),
                 jax.ShapeDtypeStruct((b, s // CHUNK, w_hg), F32)))
    outs += [tok(d, BF16)] * 2
    return pl.pallas_call(
        functools.partial(_inproj_kernel, layer=layer),
        grid=(b, s // tm),
        in_specs=[
            pl.BlockSpec((1, tm, d), lambda bi, i: (bi, i, 0)),
            pl.BlockSpec((1, 1, d), lambda bi, i: (bi, 0, 0)),
            pl.BlockSpec((1, 1, d), lambda bi, i: (bi, 0, 1)),
            pl.BlockSpec((1, d), lambda bi, i: (0, 0)),
            pl.BlockSpec((d, n_cols), lambda bi, i: (0, 0), pipeline_mode=pl.Buffered(1)),
            pl.BlockSpec(lb_logits.shape, lambda bi, i: (0, 0)),
        ],
        out_specs=[o[0] for o in outs],
        out_shape=[o[1] for o in outs],
        compiler_params=pltpu.CompilerParams(vmem_limit_bytes=VMEM_LIMIT),
        name="inproj",
    )(x, mod, mod, g, w_bf, lb_logits)


def _attn_kernel(q_ref, k_ref, v_ref, lq1_ref, lk1_ref, lq2_ref, lk2_ref, g_ref, o_ref, zs_ref,
                 kaug_ref, vaug_ref, qaug_ref, m_ref, acc_ref, sa_ref, sb_ref, zero_ref, zsem,
                 *, lam_init, slopes):
    hd = pl.program_id(1)
    i = pl.program_id(2)

    step = (pl.program_id(0) * pl.num_programs(1) + hd) * pl.num_programs(2) + i
    n_steps = pl.num_programs(0) * pl.num_programs(1) * pl.num_programs(2)
    zrows = zero_ref.shape[0]

    def zero_copy(t):
        return pltpu.make_async_copy(zero_ref, zs_ref.at[pl.ds(t * zrows, zrows)], zsem)

    @pl.when(step == 0)
    def _():
        zero_ref[...] = jnp.zeros(zero_ref.shape, F32)

    @pl.when(step > 0)
    def _():
        zero_copy(step - 1).wait()

    zero_copy(step).start()

    @pl.when(step == n_steps - 1)
    def _():
        zero_copy(step).wait()
    tq = q_ref.shape[1]
    hw = q_ref.shape[2]
    n_key = k_ref.shape[1]

    slope = jnp.float32(slopes[-1])
    for idx in range(len(slopes) - 2, -1, -1):
        slope = jnp.where(hd == idx, jnp.float32(slopes[idx]), slope)

    @pl.when(i == 0)
    def _():
        kaug_ref[:hw, :] = k_ref[0].astype(F32).T.astype(BF16)
        jrel = (lax.broadcasted_iota(I32, (hw, n_key), 1) & (tq - 1)).astype(F32) * slope
        krow = lax.broadcasted_iota(I32, (hw, n_key), 0)
        bias_rows = jnp.zeros((hw, n_key), F32)
        rest = jrel
        for term in range(BIAS_TERMS):
            part = rest.astype(BF16).astype(F32)
            bias_rows = jnp.where(krow == term, part, bias_rows)
            rest = rest - part
        kaug_ref[hw:, :] = bias_rows.astype(BF16)
        vaug_ref[:, :hw] = v_ref[0]
        vaug_ref[:, hw:] = jnp.ones((n_key, hw), BF16)

    q = q_ref[0]
    lane = lax.broadcasted_iota(I32, q.shape, 1)
    zero = jnp.zeros_like(q)
    ones = jnp.where(lane < BIAS_TERMS, 1.0, 0.0).astype(BF16)
    qaug_ref[:tq, :hw] = jnp.where(lane < DA_HEAD_DIM, q, zero)
    qaug_ref[tq:, :hw] = jnp.where(lane >= DA_HEAD_DIM, q, zero)
    qaug_ref[:tq, hw:] = ones
    qaug_ref[tq:, hw:] = ones

    m_ref[...] = jnp.full(m_ref.shape, NEG, F32)
    acc_ref[...] = jnp.zeros(acc_ref.shape, F32)

    def scores(j, buf):
        buf[...] = _nn(qaug_ref[...], kaug_ref[:, pl.ds(pl.multiple_of(j * tq, tq), tq)])

    def consume(j, buf, diagonal):
        s = buf[...]
        if diagonal:
            ii = lax.broadcasted_iota(I32, s.shape, 0) & (tq - 1)
            jj = lax.broadcasted_iota(I32, s.shape, 1)
            ahead = jnp.minimum(ii - jj, 0).astype(F32)
            s = s + jnp.where((jj // CHUNK) <= (ii // CHUNK), (2.0 * slope) * ahead, NEG)
            shift = jnp.float32(0.0)
        else:
            shift = -slope * ((i - j) * tq).astype(F32)
        m_old = m_ref[...]
        m_new = jnp.maximum(m_old, jnp.max(s, axis=-1, keepdims=True) + shift)
        p = jnp.exp2(s - jnp.concatenate([m_new - shift] * (tq // LANES), axis=1))
        alpha = jnp.concatenate([jnp.exp2(m_old - m_new)] * (acc_ref.shape[1] // LANES), axis=1)
        vb = vaug_ref[pl.ds(pl.multiple_of(j * tq, tq), tq), :]
        acc_ref[...] = alpha * acc_ref[...] + _nn(p.astype(BF16), vb)
        m_ref[...] = m_new

    scores(0, sa_ref)

    def pair(pi, carry):
        j = 2 * pi
        scores(j + 1, sb_ref)
        consume(j, sa_ref, False)
        scores(j + 2, sa_ref)
        consume(j + 1, sb_ref, False)
        return carry

    lax.fori_loop(0, i // 2, pair, 0)

    @pl.when(i % 2 == 0)
    def _():
        consume(i, sa_ref, True)

    @pl.when(i % 2 == 1)
    def _():
        scores(i, sb_ref)
        consume(i - 1, sa_ref, False)
        consume(i, sb_ref, True)

    acc = acc_ref[...]
    o1 = acc[:tq, :hw] / acc[:tq, hw:]
    o2 = acc[tq:, :hw] / acc[tq:, hw:]
    lam = (jnp.exp(jnp.sum(lq1_ref[...] * lk1_ref[...], axis=-1, keepdims=True))
           - jnp.exp(jnp.sum(lq2_ref[...] * lk2_ref[...], axis=-1, keepdims=True)) + lam_init)
    o = o1 - lam * o2
    o = o * lax.rsqrt(jnp.mean(o * o, axis=-1, keepdims=True) + EPS) * g_ref[...]
    o_ref[0] = (o * (1.0 - lam_init)).astype(BF16)


def _attention(q, k, v, lq1, lk1, lq2, lk2, subln_g, lam_init, zero_shape):
    b, s, _ = q.shape
    tq = ATT_BLOCK
    hw = 2 * DA_HEAD_DIM
    assert hw == LANES
    n_steps = b * DA_HEADS * (s // tq)
    assert zero_shape[0] % n_steps == 0
    zrows = zero_shape[0] // n_steps
    slopes = tuple(LOG2E * 2.0 ** (-8.0 * (h + 1) / DA_HEADS) for h in range(DA_HEADS))
    vec = lambda n: pl.BlockSpec((1, n), lambda bi, h, i: (0, 0))
    return pl.pallas_call(
        functools.partial(_attn_kernel, lam_init=lam_init, slopes=slopes),
        grid=(b, DA_HEADS, s // tq),
        in_specs=[
            pl.BlockSpec((1, tq, hw), lambda bi, h, i: (bi, i, h)),
            pl.BlockSpec((1, s, hw), lambda bi, h, i: (bi, 0, h)),
            pl.BlockSpec((1, s, hw), lambda bi, h, i: (bi, 0, h)),
            vec(DA_HEAD_DIM), vec(DA_HEAD_DIM), vec(DA_HEAD_DIM), vec(DA_HEAD_DIM), vec(hw),
        ],
        out_specs=[pl.BlockSpec((1, tq, hw), lambda bi, h, i: (bi, i, h)),
                   pl.BlockSpec(memory_space=pl.ANY)],
        out_shape=[jax.ShapeDtypeStruct((b, s, DA_HEADS * hw), BF16),
                   jax.ShapeDtypeStruct(zero_shape, F32)],
        scratch_shapes=[
            pltpu.VMEM((2 * hw, s), BF16),
            pltpu.VMEM((s, 2 * hw), BF16),
            pltpu.VMEM((2 * tq, 2 * hw), BF16),
            pltpu.VMEM((2 * tq, LANES), F32),
            pltpu.VMEM((2 * tq, 2 * hw), F32),
            pltpu.VMEM((2 * tq, tq), F32),
            pltpu.VMEM((2 * tq, tq), F32),
            pltpu.VMEM((zrows,) + tuple(zero_shape[1:]), F32),
            pltpu.SemaphoreType.DMA(()),
        ],
        compiler_params=pltpu.CompilerParams(vmem_limit_bytes=VMEM_LIMIT,
                                             dimension_semantics=("arbitrary",) * 3),
        name="attention",
    )(q, k, v, lq1, lk1, lq2, lk2, subln_g)


def _hgrn_kernel(qd_ref, kd_ref, ke_ref, v_ref, sg_ref, dec_ref, g_ref, o_ref, st_ref):
    @pl.when(pl.program_id(1) == 0)
    def _():
        st_ref[...] = jnp.zeros(st_ref.shape, F32)

    n_ch = qd_ref.shape[1] // CHUNK
    r = lax.broadcasted_iota(I32, (CHUNK, CHUNK), 0)
    c = lax.broadcasted_iota(I32, (CHUNK, CHUNK), 1)
    causal = r >= c

    heads = [slice(h * HG_DK, (h + 1) * HG_DK) for h in range(HG_HEADS)]
    chunks = [slice(ci * CHUNK, (ci + 1) * CHUNK) for ci in range(n_ch)]
    incr = [[_nn(v_ref[0, rows, cols].astype(F32).T.astype(BF16), ke_ref[0, rows, cols])
             for rows in chunks] for cols in heads]
    start = []
    for h, cols in enumerate(heads):
        st = st_ref[h]
        before = []
        for ci in range(n_ch):
            before.append(st.astype(BF16))
            st = st * dec_ref[0, ci:ci + 1, cols] + incr[h][ci]
        st_ref[h] = st
        start.append(before)
    for h, cols in enumerate(heads):
        for ci, rows in enumerate(chunks):
            qd = qd_ref[0, rows, cols]
            scores = jnp.where(causal, _nt(qd, kd_ref[0, rows, cols]), 0.0).astype(BF16)
            o = _nn(scores, v_ref[0, rows, cols]) + _nt(qd, start[h][ci])
            o = o * lax.rsqrt(jnp.mean(o * o, axis=-1, keepdims=True) + EPS) * g_ref[...]
            o_ref[0, rows, cols] = (o * sg_ref[0, rows, cols].astype(F32)).astype(BF16)


def _hgrn(qd, kd, ke, v, sg, dec, norm_g):
    b, s, w = qd.shape
    ts = ROW_TILE
    tok = pl.BlockSpec((1, ts, w), lambda bi, i: (bi, i, 0))
    return pl.pallas_call(
        _hgrn_kernel,
        grid=(b, s // ts),
        in_specs=[tok, tok, tok, tok, tok,
                  pl.BlockSpec((1, ts // CHUNK, w), lambda bi, i: (bi, i, 0)),
                  pl.BlockSpec((1, HG_DV), lambda bi, i: (0, 0))],
        out_specs=tok,
        out_shape=jax.ShapeDtypeStruct((b, s, w), BF16),
        scratch_shapes=[pltpu.VMEM((HG_HEADS, HG_DV, HG_DK), F32)],
        name="hgrn",
    )(qd, kd, ke, v, sg, dec, norm_g)


def _merge_kernel(yda_ref, yhg_ref, gda_ref, ghg_ref, x_ref, gt1_ref, sh2_ref, sc2_ref, g2_ref,
                  wda_ref, whg_ref, wout_ref, wrh_ref, wrl_ref, br_ref,
                  x1_ref, h2_ref, eidx_ref, wts_ref):
    u = (gda_ref[0].astype(F32) * _nn(yda_ref[0], wda_ref[...])
         + ghg_ref[0].astype(F32) * _nn(yhg_ref[0], whg_ref[...]))
    x1 = x_ref[0] + gt1_ref[0] * _nn(u.astype(BF16), wout_ref[...])
    x1_ref[0] = x1
    h2 = x1 * lax.rsqrt(jnp.mean(x1 * x1, axis=-1, keepdims=True) + EPS) * g2_ref[...]
    h2 = h2 * (1.0 + sc2_ref[0]) + sh2_ref[0]
    h2_ref[0] = h2.reshape(h2.shape[0], SUBLANES, h2.shape[1] // SUBLANES)

    h_hi, h_lo = _split(h2)
    wrh = wrh_ref[...]
    both = _nt(jnp.concatenate([wrh, wrl_ref[...]], axis=0), h_hi)
    lg = both[:ROUTER_ROWS] + both[ROUTER_ROWS:] + _nt(wrh, h_lo) + br_ref[...]

    tm = h2.shape[0]
    eg = EXPERTS_PER_GROUP
    gl = lg[N_EXPERTS:N_EXPERTS + N_GROUPS]
    gmax = jnp.max(gl, axis=0, keepdims=True)
    g_p = 1.0 / jnp.sum(jnp.exp(gl - gmax), axis=0, keepdims=True)
    gi = lax.broadcasted_iota(I32, (N_GROUPS, tm), 0)
    g_idx = jnp.min(jnp.where(gl == gmax, gi, N_GROUPS), axis=0, keepdims=True)
    sel = lg[(N_GROUPS - 1) * eg:N_GROUPS * eg]
    for g in range(N_GROUPS - 2, -1, -1):
        sel = jnp.where(g_idx == g, lg[g * eg:(g + 1) * eg], sel)
    ei = lax.broadcasted_iota(I32, (eg, tm), 0)
    m1 = jnp.max(sel, axis=0, keepdims=True)
    i1 = jnp.min(jnp.where(sel == m1, ei, eg), axis=0, keepdims=True)
    sel2 = jnp.where(ei == i1, -jnp.inf, sel)
    m2 = jnp.max(sel2, axis=0, keepdims=True)
    i2 = jnp.min(jnp.where(sel2 == m2, ei, eg), axis=0, keepdims=True)
    t = jnp.exp(m2 - m1)
    wa = 1.0 / (1.0 + t)
    eidx_ref[0] = jnp.concatenate([g_idx * eg + i1, g_idx * eg + i2], axis=0)
    wts_ref[0] = jnp.concatenate([g_p * wa, g_p * (t * wa)], axis=0)


def _merge(yda, yhg, gda, ghg, x, mod, g2, wda, whg, wout, wrh, wrl, br):
    b, s, d = x.shape
    tm = ROW_TILE
    w = yda.shape[2]
    tok = lambda width: pl.BlockSpec((1, tm, width), lambda bi, i: (bi, i, 0))
    modspec = lambda k: pl.BlockSpec((1, 1, d), lambda bi, i: (bi, 0, k))
    full = lambda a: pl.BlockSpec(a.shape, lambda bi, i: (0,) * a.ndim)
    lane_rows = pl.BlockSpec((1, 2, tm), lambda bi, i: (bi, 0, i))
    return pl.pallas_call(
        _merge_kernel,
        grid=(b, s // tm),
        in_specs=[tok(w), tok(w), tok(d), tok(d), tok(d),
                  modspec(2), modspec(3), modspec(4), full(g2),
                  full(wda), full(whg), full(wout), full(wrh), full(wrl), full(br)],
        out_specs=[tok(d), pl.BlockSpec((1, tm, SUBLANES, d // SUBLANES), lambda bi, i: (bi, i, 0, 0)),
                   lane_rows, lane_rows],
        out_shape=[jax.ShapeDtypeStruct((b, s, d), F32),
                   jax.ShapeDtypeStruct((b, s, SUBLANES, d // SUBLANES), F32),
                   jax.ShapeDtypeStruct((b, 2, s), I32), jax.ShapeDtypeStruct((b, 2, s), F32)],
        compiler_params=pltpu.CompilerParams(vmem_limit_bytes=VMEM_LIMIT),
        name="merge",
    )(yda, yhg, gda, ghg, x, mod, mod, mod, g2, wda, whg, wout, wrh, wrl, br)


def _positions_kernel(e_ref, dest_ref, blk_ref, nused_ref):
    e_mat = e_ref[...]
    nr = e_mat.shape[0]
    nb = blk_ref.shape[1]
    r = lax.broadcasted_iota(I32, (LANES, LANES), 0)
    c = lax.broadcasted_iota(I32, (LANES, LANES), 1)
    incl = jnp.where(r <= c, 1.0, 0.0).astype(BF16)
    ones = jnp.ones((LANES, LANES), BF16)
    rr = lax.broadcasted_iota(I32, (nr, nr), 0)
    cc = lax.broadcasted_iota(I32, (nr, nr), 1)
    before = jnp.where(cc < rr, 1.0, 0.0).astype(BF16)
    blk_start = (lax.broadcasted_iota(I32, (1, nb), 1) * EXPERT_BLOCK).astype(F32)

    def per_expert(e, carry):
        pstart, dest, blk = carry
        hit = e_mat == e
        oh = jnp.where(hit, 1.0, 0.0).astype(BF16)
        in_row = _nn(oh, incl)
        row_tot = _nn(oh, ones)
        row_off = _nn(before, row_tot.astype(BF16))
        total = row_off[nr - 1:nr, :] + row_tot[nr - 1:nr, :]
        dest = jnp.where(hit, pstart + row_off + in_row - 1.0, dest)
        pend = pstart + jnp.floor((total + (EXPERT_BLOCK - 1)) * (1.0 / EXPERT_BLOCK)) * EXPERT_BLOCK
        blk = blk + jnp.where(blk_start >= pend[:, :1], 1.0, 0.0)
        return pend, dest, blk

    pstart0 = jnp.zeros((1, LANES), F32)
    pend, dest, blk = lax.fori_loop(
        0, N_EXPERTS, per_expert,
        (pstart0, jnp.zeros(e_mat.shape, F32), jnp.zeros((1, nb), F32)), unroll=4)
    dest_ref[...] = dest.astype(I32)
    blk_ref[...] = jnp.minimum(blk, N_EXPERTS - 1).astype(I32)
    nused_ref[...] = (pend * (1.0 / EXPERT_BLOCK)).astype(I32)


def _positions(e_mat, n_blk_pad):
    nr = e_mat.shape[0]
    return pl.pallas_call(
        _positions_kernel,
        out_shape=[jax.ShapeDtypeStruct((nr, LANES), I32),
                   jax.ShapeDtypeStruct((1, n_blk_pad), I32),
                   jax.ShapeDtypeStruct((1, LANES), I32)],
        name="positions",
    )(e_mat)


def _dispatch_kernel(dest_ref, h2_ref, xs_in_ref, xs_ref, hbuf, in_sem, out_sem, *, n_tok):
    del xs_in_ref
    i = pl.program_id(0)
    last = pl.num_programs(0) - 1
    slot = i % 2

    def load(step, buf):
        return pltpu.make_async_copy(h2_ref.at[pl.ds(step * GATHER_TILE, GATHER_TILE)], hbuf.at[buf],
                                     in_sem.at[buf])

    def drain(buf):
        for k in range(2):
            pltpu.make_async_copy(hbuf.at[buf], xs_ref.at[pl.ds(0, GATHER_TILE)], out_sem.at[buf, k]).wait()

    @pl.when(i == 0)
    def _():
        load(0, 0).start()

    @pl.when(i > 0)
    def _():
        drain(1 - slot)

    @pl.when(i < last)
    def _():
        load(i + 1, 1 - slot).start()

    load(i, slot).wait()
    base = i * GATHER_TILE

    def issue(r, carry):
        for k in range(2):
            pltpu.make_async_copy(hbuf.at[slot, r], xs_ref.at[dest_ref[k * n_tok + base + r]],
                                  out_sem.at[slot, k]).start(priority=k)
        return carry

    lax.fori_loop(0, GATHER_TILE, issue, 0, unroll=ISSUE_UNROLL)

    @pl.when(i == last)
    def _():
        drain(slot)


def _dispatch(dest, h2, xs_zero):
    n_tok = h2.shape[0]
    return pl.pallas_call(
        functools.partial(_dispatch_kernel, n_tok=n_tok),
        grid_spec=pltpu.PrefetchScalarGridSpec(
            num_scalar_prefetch=1,
            grid=(n_tok // GATHER_TILE,),
            in_specs=[pl.BlockSpec(memory_space=pl.ANY), pl.BlockSpec(memory_space=pl.ANY)],
            out_specs=pl.BlockSpec(memory_space=pl.ANY),
            scratch_shapes=[pltpu.VMEM((2, GATHER_TILE) + h2.shape[1:], F32),
                            pltpu.SemaphoreType.DMA((2,)), pltpu.SemaphoreType.DMA((2, 2))],
        ),
        out_shape=jax.ShapeDtypeStruct(xs_zero.shape, xs_zero.dtype),
        input_output_aliases={2: 0},
        compiler_params=pltpu.CompilerParams(dimension_semantics=("arbitrary",)),
        name="dispatch",
    )(dest, h2, xs_zero)


def _expert_kernel(blk_ref, nused_ref, xs_ref, w1_ref, w3_ref, w2_ref, ys_ref,
                   w1b, w3b, w2b, w1s, w3s, w2s, sem):
    i = pl.program_id(0)
    n_used = nused_ref[0]
    e = blk_ref[i]
    fresh = (i == 0) | (blk_ref[jnp.maximum(i - 1, 0)] != e)

    def fetch(expert):
        return (pltpu.make_async_copy(w1_ref.at[expert], w1s, sem.at[0]),
                pltpu.make_async_copy(w3_ref.at[expert], w3s, sem.at[1]),
                pltpu.make_async_copy(w2_ref.at[expert], w2s, sem.at[2]))

    @pl.when(i < n_used)
    def _():
        @pl.when(i == 0)
        def _():
            for cp in fetch(e):
                cp.start()

        @pl.when(fresh)
        def _():
            for cp in fetch(e):
                cp.wait()
            w1b[...] = w1s[...].astype(BF16)
            w3b[...] = w3s[...].astype(BF16)
            w2b[...] = w2s[...].astype(BF16)
            nxt = lax.while_loop(lambda j: (j < n_used) & (blk_ref[jnp.minimum(j, n_used - 1)] == e),
                                 lambda j: j + 1, i + 1)

            @pl.when(nxt < n_used)
            def _():
                for cp in fetch(blk_ref[jnp.minimum(nxt, n_used - 1)]):
                    cp.start(priority=1)

        rows = xs_ref.shape[0]
        x = xs_ref[...].reshape(rows, w1b.shape[0]).astype(BF16)
        a = _nn(x, w1b[...])
        y = _nn((_silu(a) * _nn(x, w3b[...])).astype(BF16), w2b[...])
        ys_ref[...] = y.reshape(ys_ref.shape)

    @pl.when(i >= nused_ref[0])
    def _():
        ys_ref[...] = jnp.zeros(ys_ref.shape, F32)


def _experts(blk_e, n_used, xs, w1, w3, w2):
    p = xs.shape[0]
    tile = xs.shape[1:]
    d, de = w1.shape[1], w1.shape[2]
    rows = lambda i, blk, nu: (jnp.minimum(i, nu[0] - 1), 0, 0)
    hbm = pl.BlockSpec(memory_space=pl.ANY)
    return pl.pallas_call(
        _expert_kernel,
        grid_spec=pltpu.PrefetchScalarGridSpec(
            num_scalar_prefetch=2,
            grid=(p // EXPERT_BLOCK,),
            in_specs=[pl.BlockSpec((EXPERT_BLOCK,) + tile, rows), hbm, hbm, hbm],
            out_specs=pl.BlockSpec((EXPERT_BLOCK,) + tile, lambda i, blk, nu: (i, 0, 0)),
            scratch_shapes=[pltpu.VMEM((d, de), BF16), pltpu.VMEM((d, de), BF16),
                            pltpu.VMEM((de, d), BF16),
                            pltpu.VMEM((d, de), F32), pltpu.VMEM((d, de), F32),
                            pltpu.VMEM((de, d), F32), pltpu.SemaphoreType.DMA((3,))],
        ),
        out_shape=jax.ShapeDtypeStruct(xs.shape, F32),
        compiler_params=pltpu.CompilerParams(vmem_limit_bytes=VMEM_LIMIT,
                                             dimension_semantics=("arbitrary",)),
        name="experts",
    )(blk_e, n_used, xs, w1, w3, w2)


def _combine_kernel(dest_ref, ys_ref, x1_ref, gt2_ref, wts_ref, g_ref, o_ref, ybuf, sem, *, n_tok, last):
    i = pl.program_id(0)
    slot = i % 2

    def gather(step, buf):
        base = step * GATHER_TILE

        def issue(r, carry):
            for k in range(2):
                pltpu.make_async_copy(ys_ref.at[dest_ref[k * n_tok + base + r]], ybuf.at[buf, k, r],
                                      sem.at[buf, k]).start(priority=k)
            return carry

        lax.fori_loop(0, GATHER_TILE, issue, 0, unroll=ISSUE_UNROLL)

    @pl.when(i == 0)
    def _():
        gather(0, 0)

    @pl.when(i + 1 < pl.num_programs(0))
    def _():
        gather(i + 1, 1 - slot)

    for k in range(2):
        pltpu.make_async_copy(ys_ref.at[pl.ds(0, GATHER_TILE)], ybuf.at[slot, k], sem.at[slot, k]).wait()

    wts = wts_ref[...]
    ya = ybuf[slot, 0].reshape(x1_ref.shape)
    yb = ybuf[slot, 1].reshape(x1_ref.shape)
    x = x1_ref[...] + gt2_ref[0] * (wts[:, 0:1] * ya + wts[:, 1:2] * yb)
    if last:
        x = x * lax.rsqrt(jnp.mean(x * x, axis=-1, keepdims=True) + EPS) * g_ref[...]
    o_ref[...] = x


def _combine(dest, ys, x1, mod, wts, final_g, seq, last=True):
    n_tok, d = x1.shape
    tt = GATHER_TILE
    per_batch = seq // tt
    return pl.pallas_call(
        functools.partial(_combine_kernel, n_tok=n_tok, last=last),
        grid_spec=pltpu.PrefetchScalarGridSpec(
            num_scalar_prefetch=1,
            grid=(n_tok // tt,),
            in_specs=[pl.BlockSpec(memory_space=pl.ANY),
                      pl.BlockSpec((tt, d), lambda i, dest: (i, 0)),
                      pl.BlockSpec((1, 1, d), lambda i, dest: (i // per_batch, 0, 5)),
                      pl.BlockSpec((tt, 2), lambda i, dest: (i, 0)),
                      pl.BlockSpec((1, d), lambda i, dest: (0, 0))],
            out_specs=pl.BlockSpec((tt, d), lambda i, dest: (i, 0)),
            scratch_shapes=[pltpu.VMEM((2, 2, tt) + ys.shape[1:], F32),
                            pltpu.SemaphoreType.DMA((2, 2))],
        ),
        out_shape=jax.ShapeDtypeStruct((n_tok, d), F32),
        compiler_params=pltpu.CompilerParams(dimension_semantics=("arbitrary",)),
        name="combine",
    )(dest, ys, x1, mod, wts, final_g)


def kernel(x, c, w_ada, b_ada, norm1_g, w_in, lambda_q1, lambda_k1, lambda_q2, lambda_k2, da_subln_g, hg_lb_logits, hg_norm_g, w_up_da, w_up_hg, w_out, norm2_g, w_rg, b_rg, w_re, b_re, w1, w3, w2, final_g):
    b, s, d = x.shape
    n_tok = b * s
    depth = w_ada.shape[0]
    assert s % ROW_TILE == 0 and s % ATT_BLOCK == 0 and s % GATHER_TILE == 0
    assert (2 * n_tok) % LANES == 0 and d == SUBLANES * LANES

    c_pad = jnp.pad(c, ((0, (-b) % 8), (0, 0)))
    n_slots = 2 * n_tok + N_EXPERTS * EXPERT_BLOCK
    n_blk = n_slots // EXPERT_BLOCK
    n_blk_pad = -(-n_blk // LANES) * LANES

    for l in range(depth):
        lam_init = 0.8 - 0.6 * math.exp(-0.3 * l)
        mod = _adaln(c_pad, w_ada[l], b_ada[l][None, :])[:b].reshape(b, 1, 6 * d)

        (qda, kda, vda, qd, kd, ke, hi, sg, dec, gda, ghg) = _inproj(
            x, mod, norm1_g[l][None, :], w_in[l].astype(BF16), hg_lb_logits, l)
        yda, xs_zero = _attention(qda, kda, vda, lambda_q1[l][None, :], lambda_k1[l][None, :],
                                  lambda_q2[l][None, :], lambda_k2[l][None, :], da_subln_g[l][None, :],
                                  lam_init, (n_slots, SUBLANES, d // SUBLANES))
        yhg = _hgrn(qd, kd, ke, hi, sg, dec, hg_norm_g[l][None, :])

        wr = jnp.concatenate([w_re[l].T, w_rg[l].T,
                              jnp.zeros((ROUTER_ROWS - N_EXPERTS - N_GROUPS, d), F32)], axis=0)
        wr_hi = wr.astype(BF16)
        wr_lo = (wr - wr_hi.astype(F32)).astype(BF16)
        br = jnp.concatenate([b_re[l], b_rg[l],
                              jnp.zeros((ROUTER_ROWS - N_EXPERTS - N_GROUPS,), F32)])[:, None]
        x1, h2, eidx, wts = _merge(yda, yhg, gda, ghg, x, mod, norm2_g[l][None, :],
                                   w_up_da[l].astype(BF16), w_up_hg[l].astype(BF16),
                                   w_out[l].astype(BF16), wr_hi, wr_lo, br)

        e_mat = jnp.transpose(eidx, (1, 0, 2)).reshape(2 * n_tok // LANES, LANES)
        dest, blk_e, n_used = _positions(e_mat, n_blk_pad)
        dest = dest.reshape(2 * n_tok)
        h2 = h2.reshape((n_tok,) + h2.shape[2:])
        xs = _dispatch(dest, h2, xs_zero)
        ys = _experts(blk_e[0, :n_blk], n_used[0, :1], xs, w1[l], w3[l], w2[l])
        wts_tok = jnp.transpose(wts, (0, 2, 1)).reshape(n_tok, 2)
        x = _combine(dest, ys, x1.reshape(n_tok, d), mod, wts_tok, final_g[None, :], s,
                     last=(l == depth - 1)).reshape(b, s, d)
    return x
```

```python
import functools
import math

import jax
import jax.numpy as jnp
from jax import lax
from jax.experimental import pallas as pl
from jax.experimental.pallas import tpu as pltpu

F32 = jnp.float32
BF16 = jnp.bfloat16
I32 = jnp.int32

EPS = 1e-6
NEG = -1e30
LOG2E = math.log2(math.e)

CHUNK = 64
DA_HEADS = 4
DA_HEAD_DIM = 64
HG_HEADS = 4
HG_DK = 128
HG_DV = 128
N_GROUPS = 4
EXPERTS_PER_GROUP = 8
N_EXPERTS = N_GROUPS * EXPERTS_PER_GROUP

LANES = 128
SUBLANES = 8
ADALN_COLS = 1024
ROW_TILE = 512
ATT_BLOCK = 512
BIAS_TERMS = 3
EXPERT_BLOCK = 256
GATHER_TILE = 512
ISSUE_UNROLL = 8
ROUTER_ROWS = 48
VMEM_LIMIT = 52 * 1024 * 1024


def _nt(a, b):
    return lax.dot_general(a, b, (((1,), (1,)), ((), ())), preferred_element_type=F32)


def _nn(a, b):
    return jnp.dot(a, b, preferred_element_type=F32)


def _split(a):
    hi = a.astype(BF16)
    lo = (a - hi.astype(F32)).astype(BF16)
    return hi, lo


def _sigmoid(x):
    return 1.0 / (1.0 + jnp.exp(-x))


def _silu(x):
    return x * _sigmoid(x)


def _adaln_kernel(c_ref, w_ref, b_ref, o_ref):
    c_hi, c_lo = _split(_silu(c_ref[...]))
    w_hi, w_lo = _split(w_ref[...])
    o_ref[...] = _nn(c_hi, w_hi) + _nn(c_lo, w_hi) + _nn(c_hi, w_lo) + b_ref[...]


def _adaln(c_pad, w, b):
    rows, d = c_pad.shape
    n = w.shape[1]
    tn = ADALN_COLS
    return pl.pallas_call(
        _adaln_kernel,
        grid=(n // tn,),
        in_specs=[
            pl.BlockSpec((rows, d), lambda j: (0, 0)),
            pl.BlockSpec((d, tn), lambda j: (0, j)),
            pl.BlockSpec((1, tn), lambda j: (0, j)),
        ],
        out_specs=pl.BlockSpec((rows, tn), lambda j: (0, j)),
        out_shape=jax.ShapeDtypeStruct((rows, n), F32),
        name="adaln",
    )(c_pad, w, b)


def _inproj_kernel(x_ref, sh_ref, sc_ref, g_ref, w_ref, lb_ref,
                   qda_ref, kda_ref, vda_ref, qd_ref, kd_ref, ke_ref, hi_ref, sg_ref,
                   dec_ref, gda_ref, ghg_ref, *, layer):
    x = x_ref[0]
    tm = x.shape[0]
    h = x * lax.rsqrt(jnp.mean(x * x, axis=-1, keepdims=True) + EPS) * g_ref[...]
    h = h * (1.0 + sc_ref[0]) + sh_ref[0]
    hb = h.astype(BF16)

    def proj(c0, width):
        return _nn(hb, w_ref[:, c0:c0 + width])

    w_da = DA_HEADS * 2 * DA_HEAD_DIM
    w_hg = HG_HEADS * HG_DK
    qda_ref[0] = (proj(0, w_da) * (LOG2E / math.sqrt(DA_HEAD_DIM))).astype(BF16)
    kda_ref[0] = proj(w_da, w_da).astype(BF16)
    vda_ref[0] = proj(2 * w_da, w_da).astype(BF16)
    c0 = 3 * w_da

    lbl = lb_ref[...]
    lbe = jnp.exp(lbl - jnp.max(lbl, axis=0, keepdims=True))
    lbs = lbe / jnp.sum(lbe, axis=0, keepdims=True)
    lb = jnp.sum(lbs[:layer + 1], axis=0, keepdims=True)

    f = lb + (1.0 - lb) * _sigmoid(proj(c0 + w_hg, w_hg))
    logf = jnp.log(f)
    kk = 1.0 - f
    pos = lax.broadcasted_iota(I32, logf.shape, 0) & (CHUNK - 1)
    a = logf
    step = 1
    while step < CHUNK:
        a = a + jnp.where(pos >= step, pltpu.roll(a, step, axis=0), 0.0)
        step *= 2
    n_ch = tm // CHUNK
    a = a.reshape(n_ch, CHUNK, w_hg)
    a_last = a[:, CHUNK - 1:CHUNK, :]
    kk3 = kk.reshape(n_ch, CHUNK, w_hg)
    qq3 = _silu(proj(c0, w_hg)).reshape(n_ch, CHUNK, w_hg)
    qd_ref[0] = (qq3 * jnp.exp(a)).reshape(tm, w_hg).astype(BF16)
    kd_ref[0] = (kk3 * jnp.exp(-a)).reshape(tm, w_hg).astype(BF16)
    ke_ref[0] = (kk3 * jnp.exp(a_last - a)).reshape(tm, w_hg).astype(BF16)
    dec_ref[0] = jnp.exp(a_last).reshape(n_ch, w_hg)
    hi_ref[0] = proj(c0 + 2 * w_hg, w_hg).astype(BF16)
    sg_ref[0] = _silu(proj(c0 + 3 * w_hg, w_hg)).astype(BF16)
    c1 = c0 + 4 * w_hg
    d = x.shape[1]
    gda_ref[0] = _sigmoid(proj(c1, d)).astype(BF16)
    ghg_ref[0] = _sigmoid(proj(c1 + d, d)).astype(BF16)


def _inproj(x, mod, g, w_bf, lb_logits, layer):
    b, s, d = x.shape
    tm = ROW_TILE
    n_cols = w_bf.shape[1]
    w_da = DA_HEADS * 2 * DA_HEAD_DIM
    w_hg = HG_HEADS * HG_DK

    def tok(width, dtype):
        return (pl.BlockSpec((1, tm, width), lambda bi, i: (bi, i, 0)),
                jax.ShapeDtypeStruct((b, s, width), dtype))

    outs = [tok(w_da, BF16)] * 3 + [tok(w_hg, BF16)] * 5
    outs.append((pl.BlockSpec((1, tm // CHUNK, w_hg), lambda bi, i: (bi, i, 0)),
                 jax.ShapeDtypeStruct((b, s // CHUNK, w_hg), F32)))
    outs += [tok(d, BF16)] * 2
    return pl.pallas_call(
        functools.partial(_inproj_kernel, layer=layer),
        grid=(b, s // tm),
        in_specs=[
            pl.BlockSpec((1, tm, d), lambda bi, i: (bi, i, 0)),
            pl.BlockSpec((1, 1, d), lambda bi, i: (bi, 0, 0)),
            pl.BlockSpec((1, 1, d), lambda bi, i: (bi, 0, 1)),
            pl.BlockSpec((1, d), lambda bi, i: (0, 0)),
            pl.BlockSpec((d, n_cols), lambda bi, i: (0, 0), pipeline_mode=pl.Buffered(1)),
            pl.BlockSpec(lb_logits.shape, lambda bi, i: (0, 0)),
        ],
        out_specs=[o[0] for o in outs],
        out_shape=[o[1] for o in outs],
        compiler_params=pltpu.CompilerParams(vmem_limit_bytes=VMEM_LIMIT),
        name="inproj",
    )(x, mod, mod, g, w_bf, lb_logits)


def _attn_kernel(q_ref, k_ref, v_ref, lq1_ref, lk1_ref, lq2_ref, lk2_ref, g_ref, o_ref,
                 kaug_ref, vaug_ref, qaug_ref, m_ref, acc_ref, sa_ref, sb_ref, *, lam_init, slopes):
    hd = pl.program_id(1)
    i = pl.program_id(2)

    tq = q_ref.shape[1]
    hw = q_ref.shape[2]
    n_key = k_ref.shape[1]

    slope = jnp.float32(slopes[-1])
    for idx in range(len(slopes) - 2, -1, -1):
        slope = jnp.where(hd == idx, jnp.float32(slopes[idx]), slope)

    @pl.when(i == 0)
    def _():
        kaug_ref[:hw, :] = k_ref[0].astype(F32).T.astype(BF16)
        jrel = (lax.broadcasted_iota(I32, (hw, n_key), 1) & (tq - 1)).astype(F32) * slope
        krow = lax.broadcasted_iota(I32, (hw, n_key), 0)
        bias_rows = jnp.zeros((hw, n_key), F32)
        rest = jrel
        for term in range(BIAS_TERMS):
            part = rest.astype(BF16).astype(F32)
            bias_rows = jnp.where(krow == term, part, bias_rows)
            rest = rest - part
        kaug_ref[hw:, :] = bias_rows.astype(BF16)
        vaug_ref[:, :hw] = v_ref[0]
        vaug_ref[:, hw:] = jnp.ones((n_key, hw), BF16)

    q = q_ref[0]
    lane = lax.broadcasted_iota(I32, q.shape, 1)
    zero = jnp.zeros_like(q)
    ones = jnp.where(lane < BIAS_TERMS, 1.0, 0.0).astype(BF16)
    qaug_ref[:tq, :hw] = jnp.where(lane < DA_HEAD_DIM, q, zero)
    qaug_ref[tq:, :hw] = jnp.where(lane >= DA_HEAD_DIM, q, zero)
    qaug_ref[:tq, hw:] = ones
    qaug_ref[tq:, hw:] = ones

    m_ref[...] = jnp.full(m_ref.shape, NEG, F32)
    acc_ref[...] = jnp.zeros(acc_ref.shape, F32)

    def scores(j, buf):
        buf[...] = _nn(qaug_ref[...], kaug_ref[:, pl.ds(pl.multiple_of(j * tq, tq), tq)])

    def consume(j, buf, diagonal):
        s = buf[...]
        if diagonal:
            ii = lax.broadcasted_iota(I32, s.shape, 0) & (tq - 1)
            jj = lax.broadcasted_iota(I32, s.shape, 1)
            ahead = jnp.minimum(ii - jj, 0).astype(F32)
            s = s + jnp.where((jj // CHUNK) <= (ii // CHUNK), (2.0 * slope) * ahead, NEG)
            shift = jnp.float32(0.0)
        else:
            shift = -slope * ((i - j) * tq).astype(F32)
        m_old = m_ref[...]
        m_new = jnp.maximum(m_old, jnp.max(s, axis=-1, keepdims=True) + shift)
        p = jnp.exp2(s - jnp.concatenate([m_new - shift] * (tq // LANES), axis=1))
        alpha = jnp.concatenate([jnp.exp2(m_old - m_new)] * (acc_ref.shape[1] // LANES), axis=1)
        vb = vaug_ref[pl.ds(pl.multiple_of(j * tq, tq), tq), :]
        acc_ref[...] = alpha * acc_ref[...] + _nn(p.astype(BF16), vb)
        m_ref[...] = m_new

    scores(0, sa_ref)

    def pair(pi, carry):
        j = 2 * pi
        scores(j + 1, sb_ref)
        consume(j, sa_ref, False)
        scores(j + 2, sa_ref)
        consume(j + 1, sb_ref, False)
        return carry

    lax.fori_loop(0, i // 2, pair, 0)

    @pl.when(i % 2 == 0)
    def _():
        consume(i, sa_ref, True)

    @pl.when(i % 2 == 1)
    def _():
        scores(i, sb_ref)
        consume(i - 1, sa_ref, False)
        consume(i, sb_ref, True)

    acc = acc_ref[...]
    o1 = acc[:tq, :hw] / acc[:tq, hw:]
    o2 = acc[tq:, :hw] / acc[tq:, hw:]
    lam = (jnp.exp(jnp.sum(lq1_ref[...] * lk1_ref[...], axis=-1, keepdims=True))
           - jnp.exp(jnp.sum(lq2_ref[...] * lk2_ref[...], axis=-1, keepdims=True)) + lam_init)
    o = o1 - lam * o2
    o = o * lax.rsqrt(jnp.mean(o * o, axis=-1, keepdims=True) + EPS) * g_ref[...]
    o_ref[0] = (o * (1.0 - lam_init)).astype(BF16)


def _attention(q, k, v, lq1, lk1, lq2, lk2, subln_g, lam_init):
    b, s, _ = q.shape
    tq = ATT_BLOCK
    hw = 2 * DA_HEAD_DIM
    assert hw == LANES
    slopes = tuple(LOG2E * 2.0 ** (-8.0 * (h + 1) / DA_HEADS) for h in range(DA_HEADS))
    vec = lambda n: pl.BlockSpec((1, n), lambda bi, h, i: (0, 0))
    return pl.pallas_call(
        functools.partial(_attn_kernel, lam_init=lam_init, slopes=slopes),
        grid=(b, DA_HEADS, s // tq),
        in_specs=[
            pl.BlockSpec((1, tq, hw), lambda bi, h, i: (bi, i, h)),
            pl.BlockSpec((1, s, hw), lambda bi, h, i: (bi, 0, h)),
            pl.BlockSpec((1, s, hw), lambda bi, h, i: (bi, 0, h)),
            vec(DA_HEAD_DIM), vec(DA_HEAD_DIM), vec(DA_HEAD_DIM), vec(DA_HEAD_DIM), vec(hw),
        ],
        out_specs=pl.BlockSpec((1, tq, hw), lambda bi, h, i: (bi, i, h)),
        out_shape=jax.ShapeDtypeStruct((b, s, DA_HEADS * hw), BF16),
        scratch_shapes=[
            pltpu.VMEM((2 * hw, s), BF16),
            pltpu.VMEM((s, 2 * hw), BF16),
            pltpu.VMEM((2 * tq, 2 * hw), BF16),
            pltpu.VMEM((2 * tq, LANES), F32),
            pltpu.VMEM((2 * tq, 2 * hw), F32),
            pltpu.VMEM((2 * tq, tq), F32),
            pltpu.VMEM((2 * tq, tq), F32),
        ],
        compiler_params=pltpu.CompilerParams(vmem_limit_bytes=VMEM_LIMIT),
        name="attention",
    )(q, k, v, lq1, lk1, lq2, lk2, subln_g)


def _hgrn_kernel(qd_ref, kd_ref, ke_ref, v_ref, sg_ref, dec_ref, g_ref, o_ref, st_ref):
    @pl.when(pl.program_id(1) == 0)
    def _():
        st_ref[...] = jnp.zeros(st_ref.shape, F32)

    n_ch = qd_ref.shape[1] // CHUNK
    r = lax.broadcasted_iota(I32, (CHUNK, CHUNK), 0)
    c = lax.broadcasted_iota(I32, (CHUNK, CHUNK), 1)
    causal = r >= c

    heads = [slice(h * HG_DK, (h + 1) * HG_DK) for h in range(HG_HEADS)]
    chunks = [slice(ci * CHUNK, (ci + 1) * CHUNK) for ci in range(n_ch)]
    incr = [[_nn(v_ref[0, rows, cols].astype(F32).T.astype(BF16), ke_ref[0, rows, cols])
             for rows in chunks] for cols in heads]
    start = []
    for h, cols in enumerate(heads):
        st = st_ref[h]
        before = []
        for ci in range(n_ch):
            before.append(st.astype(BF16))
            st = st * dec_ref[0, ci:ci + 1, cols] + incr[h][ci]
        st_ref[h] = st
        start.append(before)
    for h, cols in enumerate(heads):
        for ci, rows in enumerate(chunks):
            qd = qd_ref[0, rows, cols]
            scores = jnp.where(causal, _nt(qd, kd_ref[0, rows, cols]), 0.0).astype(BF16)
            o = _nn(scores, v_ref[0, rows, cols]) + _nt(qd, start[h][ci])
            o = o * lax.rsqrt(jnp.mean(o * o, axis=-1, keepdims=True) + EPS) * g_ref[...]
            o_ref[0, rows, cols] = (o * sg_ref[0, rows, cols].astype(F32)).astype(BF16)


def _hgrn(qd, kd, ke, v, sg, dec, norm_g):
    b, s, w = qd.shape
    ts = ROW_TILE
    tok = pl.BlockSpec((1, ts, w), lambda bi, i: (bi, i, 0))
    return pl.pallas_call(
        _hgrn_kernel,
        grid=(b, s // ts),
        in_specs=[tok, tok, tok, tok, tok,
                  pl.BlockSpec((1, ts // CHUNK, w), lambda bi, i: (bi, i, 0)),
                  pl.BlockSpec((1, HG_DV), lambda bi, i: (0, 0))],
        out_specs=tok,
        out_shape=jax.ShapeDtypeStruct((b, s, w), BF16),
        scratch_shapes=[pltpu.VMEM((HG_HEADS, HG_DV, HG_DK), F32)],
        name="hgrn",
    )(qd, kd, ke, v, sg, dec, norm_g)


def _merge_kernel(yda_ref, yhg_ref, gda_ref, ghg_ref, x_ref, gt1_ref, sh2_ref, sc2_ref, g2_ref,
                  wda_ref, whg_ref, wout_ref, wrh_ref, wrl_ref, br_ref,
                  x1_ref, h2_ref, eidx_ref, wts_ref):
    u = (gda_ref[0].astype(F32) * _nn(yda_ref[0], wda_ref[...])
         + ghg_ref[0].astype(F32) * _nn(yhg_ref[0], whg_ref[...]))
    x1 = x_ref[0] + gt1_ref[0] * _nn(u.astype(BF16), wout_ref[...])
    x1_ref[0] = x1
    h2 = x1 * lax.rsqrt(jnp.mean(x1 * x1, axis=-1, keepdims=True) + EPS) * g2_ref[...]
    h2 = h2 * (1.0 + sc2_ref[0]) + sh2_ref[0]
    h2_ref[0] = h2.reshape(h2.shape[0], SUBLANES, h2.shape[1] // SUBLANES)

    h_hi, h_lo = _split(h2)
    wrh = wrh_ref[...]
    both = _nt(jnp.concatenate([wrh, wrl_ref[...]], axis=0), h_hi)
    lg = both[:ROUTER_ROWS] + both[ROUTER_ROWS:] + _nt(wrh, h_lo) + br_ref[...]

    tm = h2.shape[0]
    eg = EXPERTS_PER_GROUP
    gl = lg[N_EXPERTS:N_EXPERTS + N_GROUPS]
    gmax = jnp.max(gl, axis=0, keepdims=True)
    g_p = 1.0 / jnp.sum(jnp.exp(gl - gmax), axis=0, keepdims=True)
    gi = lax.broadcasted_iota(I32, (N_GROUPS, tm), 0)
    g_idx = jnp.min(jnp.where(gl == gmax, gi, N_GROUPS), axis=0, keepdims=True)
    sel = lg[(N_GROUPS - 1) * eg:N_GROUPS * eg]
    for g in range(N_GROUPS - 2, -1, -1):
        sel = jnp.where(g_idx == g, lg[g * eg:(g + 1) * eg], sel)
    ei = lax.broadcasted_iota(I32, (eg, tm), 0)
    m1 = jnp.max(sel, axis=0, keepdims=True)
    i1 = jnp.min(jnp.where(sel == m1, ei, eg), axis=0, keepdims=True)
    sel2 = jnp.where(ei == i1, -jnp.inf, sel)
    m2 = jnp.max(sel2, axis=0, keepdims=True)
    i2 = jnp.min(jnp.where(sel2 == m2, ei, eg), axis=0, keepdims=True)
    t = jnp.exp(m2 - m1)
    wa = 1.0 / (1.0 + t)
    eidx_ref[0] = jnp.concatenate([g_idx * eg + i1, g_idx * eg + i2], axis=0)
    wts_ref[0] = jnp.concatenate([g_p * wa, g_p * (t * wa)], axis=0)


def _merge(yda, yhg, gda, ghg, x, mod, g2, wda, whg, wout, wrh, wrl, br):
    b, s, d = x.shape
    tm = ROW_TILE
    w = yda.shape[2]
    tok = lambda width: pl.BlockSpec((1, tm, width), lambda bi, i: (bi, i, 0))
    modspec = lambda k: pl.BlockSpec((1, 1, d), lambda bi, i: (bi, 0, k))
    full = lambda a: pl.BlockSpec(a.shape, lambda bi, i: (0,) * a.ndim)
    lane_rows = pl.BlockSpec((1, 2, tm), lambda bi, i: (bi, 0, i))
    return pl.pallas_call(
        _merge_kernel,
        grid=(b, s // tm),
        in_specs=[tok(w), tok(w), tok(d), tok(d), tok(d),
                  modspec(2), modspec(3), modspec(4), full(g2),
                  full(wda), full(whg), full(wout), full(wrh), full(wrl), full(br)],
        out_specs=[tok(d), pl.BlockSpec((1, tm, SUBLANES, d // SUBLANES), lambda bi, i: (bi, i, 0, 0)),
                   lane_rows, lane_rows],
        out_shape=[jax.ShapeDtypeStruct((b, s, d), F32),
                   jax.ShapeDtypeStruct((b, s, SUBLANES, d // SUBLANES), F32),
                   jax.ShapeDtypeStruct((b, 2, s), I32), jax.ShapeDtypeStruct((b, 2, s), F32)],
        compiler_params=pltpu.CompilerParams(vmem_limit_bytes=VMEM_LIMIT),
        name="merge",
    )(yda, yhg, gda, ghg, x, mod, mod, mod, g2, wda, whg, wout, wrh, wrl, br)


def _positions_kernel(e_ref, slot_ref, blk_ref, nused_ref, pt_ref, off_ref, end_ref, cnt_ref, cnt_smem, sem,
                      *, n_tok):
    e_mat = e_ref[...]
    nr = e_mat.shape[0]
    nb = blk_ref.shape[1]
    r = lax.broadcasted_iota(I32, (LANES, LANES), 0)
    c = lax.broadcasted_iota(I32, (LANES, LANES), 1)
    incl = jnp.where(r <= c, 1.0, 0.0).astype(BF16)
    ones = jnp.ones((LANES, LANES), BF16)
    rr = lax.broadcasted_iota(I32, (nr, nr), 0)
    cc = lax.broadcasted_iota(I32, (nr, nr), 1)
    before = jnp.where(cc < rr, 1.0, 0.0).astype(BF16)
    blk_start = (lax.broadcasted_iota(I32, (1, nb), 1) * EXPERT_BLOCK).astype(F32)

    lane = lax.broadcasted_iota(I32, (1, LANES), 1)

    def per_expert(e, carry):
        pstart, blk, totals, first_blk = carry
        oh = jnp.where(e_mat == e, 1.0, 0.0).astype(BF16)
        in_row = _nn(oh, incl)
        row_tot = _nn(oh, ones)
        row_off = _nn(before, row_tot.astype(BF16))
        total = row_off[nr - 1:nr, :] + row_tot[nr - 1:nr, :]
        pt_ref[e] = in_row.T
        off_ref[e] = row_off
        end_ref[e] = row_off + row_tot
        totals = jnp.where(lane == e, total, totals)
        first_blk = jnp.where(lane == e, pstart * (1.0 / EXPERT_BLOCK), first_blk)
        pend = pstart + jnp.floor((total + (EXPERT_BLOCK - 1)) * (1.0 / EXPERT_BLOCK)) * EXPERT_BLOCK
        blk = blk + jnp.where(blk_start >= pend[:, :1], 1.0, 0.0)
        return pend, blk, totals, first_blk

    zeros = jnp.zeros((1, LANES), F32)
    pend, blk, totals, first_blk = lax.fori_loop(
        0, N_EXPERTS, per_expert, (zeros, jnp.zeros((1, nb), F32), zeros, zeros), unroll=4)
    blk_ref[...] = jnp.minimum(blk, N_EXPERTS - 1).astype(I32)
    nused_ref[...] = (pend * (1.0 / EXPERT_BLOCK)).astype(I32)

    cnt_ref[...] = jnp.zeros(cnt_ref.shape, I32)
    cnt_ref[0:1, :] = totals.astype(I32)
    cnt_ref[1:2, :] = first_blk.astype(I32)
    to_smem = pltpu.make_async_copy(cnt_ref, cnt_smem, sem)
    to_smem.start()
    to_smem.wait()

    slot_ref[...] = jnp.full(slot_ref.shape, 2 * n_tok * n_tok, I32)
    width = slot_ref.shape[1]
    slot_lane = lax.broadcasted_iota(I32, (1, width), 1)
    row_id = lax.broadcasted_iota(I32, (nr, width), 0).astype(F32)

    def per_expert_slots(e, carry):
        total = cnt_smem[0, e]
        blk0 = cnt_smem[1, e]
        ends = jnp.concatenate([end_ref[e]] * (width // LANES), axis=1)
        offs = jnp.concatenate([off_ref[e]] * (width // LANES), axis=1)
        pt = pt_ref[e]

        def per_block(ti, inner):
            s_int = ti * width + slot_lane
            s_loc = s_int.astype(F32)
            row = jnp.sum(jnp.where(ends <= s_loc, 1.0, 0.0), axis=0, keepdims=True)
            onehot = jnp.where(row_id == row, 1.0, 0.0)
            rank = s_loc - jnp.sum(offs * onehot, axis=0, keepdims=True)
            prefix = _nn(pt, onehot)
            lane_in_row = jnp.sum(jnp.where(prefix <= rank, 1.0, 0.0), axis=0, keepdims=True)
            flat = (row * LANES + lane_in_row).astype(I32)
            valid = s_int < total
            tok = jnp.where(valid, jnp.where(flat >= n_tok, flat - n_tok, flat), 0)
            spare = 2 * n_tok + ((blk0 + ti) & 1) * width + slot_lane
            slot_ref[pl.ds(blk0 + ti, 1), :] = jnp.where(valid, flat, spare) * n_tok + tok
            return inner

        lax.fori_loop(0, (total + (width - 1)) // width, per_block, 0)
        return carry

    lax.fori_loop(0, N_EXPERTS, per_expert_slots, 0)


def _positions(e_mat, n_tok, n_blk, n_blk_pad):
    nr = e_mat.shape[0]
    assert EXPERT_BLOCK % LANES == 0 and n_tok & (n_tok - 1) == 0 and (2 * n_tok + 2 * EXPERT_BLOCK) * n_tok < 2 ** 31
    return pl.pallas_call(
        functools.partial(_positions_kernel, n_tok=n_tok),
        out_shape=[jax.ShapeDtypeStruct((-(-n_blk // SUBLANES) * SUBLANES, EXPERT_BLOCK), I32),
                   jax.ShapeDtypeStruct((1, n_blk_pad), I32),
                   jax.ShapeDtypeStruct((1, LANES), I32)],
        scratch_shapes=[pltpu.VMEM((N_EXPERTS, LANES, nr), F32),
                        pltpu.VMEM((N_EXPERTS, nr, LANES), F32),
                        pltpu.VMEM((N_EXPERTS, nr, LANES), F32),
                        pltpu.VMEM((SUBLANES, LANES), I32),
                        pltpu.SMEM((SUBLANES, LANES), I32),
                        pltpu.SemaphoreType.DMA(())],
        name="positions",
    )(e_mat)


def _expert_kernel(blk_ref, nused_ref, slot_ref, h2_ref, w1_ref, w3_ref, w2_ref, y2_ref,
                   xbuf, ybuf, w1b, w3b, w2b, w1s, w3s, w2s, gsem, ssem, wsem, *, n_tok):
    i = pl.program_id(0)
    n_used = nused_ref[0]
    rows = EXPERT_BLOCK
    slot = i % 2
    e = blk_ref[jnp.minimum(i, n_used - 1)]
    fresh = (i == 0) | (blk_ref[jnp.maximum(jnp.minimum(i, n_used - 1) - 1, 0)] != e)

    shift = n_tok.bit_length() - 1

    def gather(block, r, buf):
        tok = slot_ref[block * rows + r] & (n_tok - 1)
        return pltpu.make_async_copy(h2_ref.at[tok], xbuf.at[buf, r], gsem.at[buf])

    def scatter(block, r, buf):
        dst = lax.shift_right_logical(slot_ref[block * rows + r], shift)
        return pltpu.make_async_copy(ybuf.at[buf, r], y2_ref.at[dst], ssem.at[buf])

    def wait_gather(buf):
        pltpu.make_async_copy(h2_ref.at[pl.ds(0, rows)], xbuf.at[buf], gsem.at[buf]).wait()

    def wait_scatter(buf):
        pltpu.make_async_copy(ybuf.at[buf], y2_ref.at[pl.ds(0, rows)], ssem.at[buf]).wait()

    def fetch(expert):
        return (pltpu.make_async_copy(w1_ref.at[expert], w1s, wsem.at[0]),
                pltpu.make_async_copy(w3_ref.at[expert], w3s, wsem.at[1]),
                pltpu.make_async_copy(w2_ref.at[expert], w2s, wsem.at[2]))

    @pl.when(i < n_used)
    def _():
        @pl.when(i == 0)
        def _():
            for cp in fetch(e):
                cp.start()
            ybuf[...] = jnp.zeros(ybuf.shape, F32)

            def prime(r, carry):
                gather(0, r, 0).start()
                return carry

            lax.fori_loop(0, rows, prime, 0, unroll=ISSUE_UNROLL)

        @pl.when(fresh)
        def _():
            for cp in fetch(e):
                cp.wait()
            w1b[...] = w1s[...].astype(BF16)
            w3b[...] = w3s[...].astype(BF16)
            w2b[...] = w2s[...].astype(BF16)
            nxt_e = lax.while_loop(lambda j: (j < n_used) & (blk_ref[jnp.minimum(j, n_used - 1)] == e),
                                   lambda j: j + 1, i + 1)

            @pl.when(nxt_e < n_used)
            def _():
                for cp in fetch(blk_ref[jnp.minimum(nxt_e, n_used - 1)]):
                    cp.start(priority=1)

        wait_gather(slot)

        @pl.when(i > 0)
        def _():
            wait_scatter(slot)

        nxt = jnp.minimum(i + 1, n_used - 1)
        prev = jnp.maximum(i - 1, 0)
        for r in range(rows):
            gather(nxt, r, 1 - slot).start()
        for r in range(rows):
            scatter(prev, r, 1 - slot).start(priority=1)
        x = xbuf[slot].reshape(rows, w1b.shape[0]).astype(BF16)
        a = _nn(x, w1b[...])
        y = _nn((_silu(a) * _nn(x, w3b[...])).astype(BF16), w2b[...])
        ybuf[slot] = y.reshape(ybuf.shape[1:])

    @pl.when(i == n_used)
    def _():
        last = 1 - slot

        def tail(r, carry):
            scatter(n_used - 1, r, last).start(priority=1)
            return carry

        lax.fori_loop(0, rows, tail, 0, unroll=ISSUE_UNROLL)
        wait_scatter(slot)
        wait_scatter(last)
        wait_gather(slot)
        ybuf[...] = jnp.zeros(ybuf.shape, F32)
        for buf in range(2):
            spare = pltpu.make_async_copy(ybuf.at[buf], y2_ref.at[pl.ds(2 * n_tok + buf * rows, rows)], ssem.at[buf])
            spare.start()
            spare.wait()


def _experts(blk_e, n_used, slots, h2, w1, w3, w2):
    n_tok = h2.shape[0]
    tile = h2.shape[1:]
    d, de = w1.shape[1], w1.shape[2]
    n_blk = blk_e.shape[0]
    hbm = pl.BlockSpec(memory_space=pl.ANY)
    return pl.pallas_call(
        functools.partial(_expert_kernel, n_tok=n_tok),
        grid_spec=pltpu.PrefetchScalarGridSpec(
            num_scalar_prefetch=3,
            grid=(n_blk + 1,),
            in_specs=[hbm, hbm, hbm, hbm],
            out_specs=hbm,
            scratch_shapes=[pltpu.VMEM((2, EXPERT_BLOCK) + tile, F32), pltpu.VMEM((2, EXPERT_BLOCK) + tile, F32),
                            pltpu.VMEM((d, de), BF16), pltpu.VMEM((d, de), BF16), pltpu.VMEM((de, d), BF16),
                            pltpu.VMEM((d, de), F32), pltpu.VMEM((d, de), F32), pltpu.VMEM((de, d), F32),
                            pltpu.SemaphoreType.DMA((2,)), pltpu.SemaphoreType.DMA((2,)),
                            pltpu.SemaphoreType.DMA((3,))],
        ),
        out_shape=jax.ShapeDtypeStruct((2 * n_tok + 2 * EXPERT_BLOCK,) + tile, F32),
        compiler_params=pltpu.CompilerParams(vmem_limit_bytes=VMEM_LIMIT,
                                             dimension_semantics=("arbitrary",)),
        name="experts",
    )(blk_e, n_used, slots, h2, w1, w3, w2)


def _combine_kernel(y0_ref, y1_ref, x1_ref, gt2_ref, wts_ref, g_ref, o_ref, *, last):
    wts = wts_ref[...]
    ya = y0_ref[...].reshape(x1_ref.shape)
    yb = y1_ref[...].reshape(x1_ref.shape)
    x = x1_ref[...] + gt2_ref[0] * (wts[:, 0:1] * ya + wts[:, 1:2] * yb)
    if last:
        x = x * lax.rsqrt(jnp.mean(x * x, axis=-1, keepdims=True) + EPS) * g_ref[...]
    o_ref[...] = x


def _combine(y2, x1, mod, wts, final_g, seq, last=True):
    n_tok, d = x1.shape
    tt = GATHER_TILE
    per_batch = seq // tt
    n_tiles = n_tok // tt
    tile = y2.shape[1:]
    return pl.pallas_call(
        functools.partial(_combine_kernel, last=last),
        grid=(n_tiles,),
        in_specs=[pl.BlockSpec((tt,) + tile, lambda i: (i, 0, 0)),
                  pl.BlockSpec((tt,) + tile, lambda i: (i + n_tiles, 0, 0)),
                  pl.BlockSpec((tt, d), lambda i: (i, 0)),
                  pl.BlockSpec((1, 1, d), lambda i: (i // per_batch, 0, 5)),
                  pl.BlockSpec((tt, 2), lambda i: (i, 0)),
                  pl.BlockSpec((1, d), lambda i: (0, 0))],
        out_specs=pl.BlockSpec((tt, d), lambda i: (i, 0)),
        out_shape=jax.ShapeDtypeStruct((n_tok, d), F32),
        name="combine",
    )(y2, y2, x1, mod, wts, final_g)


def kernel(x, c, w_ada, b_ada, norm1_g, w_in, lambda_q1, lambda_k1, lambda_q2, lambda_k2, da_subln_g, hg_lb_logits, hg_norm_g, w_up_da, w_up_hg, w_out, norm2_g, w_rg, b_rg, w_re, b_re, w1, w3, w2, final_g):
    b, s, d = x.shape
    n_tok = b * s
    depth = w_ada.shape[0]
    assert s % ROW_TILE == 0 and s % ATT_BLOCK == 0 and s % GATHER_TILE == 0
    assert (2 * n_tok) % LANES == 0 and d == SUBLANES * LANES

    c_pad = jnp.pad(c, ((0, (-b) % 8), (0, 0)))
    n_slots = 2 * n_tok + N_EXPERTS * EXPERT_BLOCK
    n_blk = n_slots // EXPERT_BLOCK
    n_blk_pad = -(-n_blk // LANES) * LANES

    for l in range(depth):
        lam_init = 0.8 - 0.6 * math.exp(-0.3 * l)
        mod = _adaln(c_pad, w_ada[l], b_ada[l][None, :])[:b].reshape(b, 1, 6 * d)

        (qda, kda, vda, qd, kd, ke, hi, sg, dec, gda, ghg) = _inproj(
            x, mod, norm1_g[l][None, :], w_in[l].astype(BF16), hg_lb_logits, l)
        yda = _attention(qda, kda, vda, lambda_q1[l][None, :], lambda_k1[l][None, :],
                         lambda_q2[l][None, :], lambda_k2[l][None, :], da_subln_g[l][None, :], lam_init)
        yhg = _hgrn(qd, kd, ke, hi, sg, dec, hg_norm_g[l][None, :])

        wr = jnp.concatenate([w_re[l].T, w_rg[l].T,
                              jnp.zeros((ROUTER_ROWS - N_EXPERTS - N_GROUPS, d), F32)], axis=0)
        wr_hi = wr.astype(BF16)
        wr_lo = (wr - wr_hi.astype(F32)).astype(BF16)
        br = jnp.concatenate([b_re[l], b_rg[l],
                              jnp.zeros((ROUTER_ROWS - N_EXPERTS - N_GROUPS,), F32)])[:, None]
        x1, h2, eidx, wts = _merge(yda, yhg, gda, ghg, x, mod, norm2_g[l][None, :],
                                   w_up_da[l].astype(BF16), w_up_hg[l].astype(BF16),
                                   w_out[l].astype(BF16), wr_hi, wr_lo, br)

        e_mat = jnp.transpose(eidx, (1, 0, 2)).reshape(2 * n_tok // LANES, LANES)
        slots, blk_e, n_used = _positions(e_mat, n_tok, n_blk, n_blk_pad)
        h2 = h2.reshape((n_tok,) + h2.shape[2:])
        y2 = _experts(blk_e[0, :n_blk], n_used[0, :1], slots[:n_blk].reshape(n_slots), h2, w1[l], w3[l], w2[l])
        wts_tok = jnp.transpose(wts, (0, 2, 1)).reshape(n_tok, 2)
        x = _combine(y2, x1.reshape(n_tok, d), mod, wts_tok, final_g[None, :], s,
                     last=(l == depth - 1)).reshape(b, s, d)
    return x
```

```python
import functools
import math

import jax
import jax.numpy as jnp
from jax import lax
from jax.experimental import pallas as pl
from jax.experimental.pallas import tpu as pltpu

F32 = jnp.float32
BF16 = jnp.bfloat16
I32 = jnp.int32

EPS = 1e-6
NEG = -1e30
LOG2E = math.log2(math.e)

CHUNK = 64
DA_HEADS = 4
DA_HEAD_DIM = 64
HG_HEADS = 4
HG_DK = 128
HG_DV = 128
N_GROUPS = 4
EXPERTS_PER_GROUP = 8
N_EXPERTS = N_GROUPS * EXPERTS_PER_GROUP

LANES = 128
SUBLANES = 8
MXU_COLS = 256
ADALN_COLS = 1024
ROW_TILE = 512
ATT_BLOCK = 512
BIAS_TERMS = 3
EXPERT_BLOCK = 256
GATHER_TILE = 512
ISSUE_UNROLL = 8
ROUTER_ROWS = 48
VMEM_LIMIT = 52 * 1024 * 1024


def _nt(a, b):
    return lax.dot_general(a, b, (((1,), (1,)), ((), ())), preferred_element_type=F32)


def _nn(a, b):
    return jnp.dot(a, b, preferred_element_type=F32)


def _split(a):
    hi = a.astype(BF16)
    lo = (a - hi.astype(F32)).astype(BF16)
    return hi, lo


def _sigmoid(x):
    return 1.0 / (1.0 + jnp.exp(-x))


def _silu(x):
    return x * _sigmoid(x)


def _adaln_kernel(c_ref, w_ref, b_ref, o_ref):
    c_hi, c_lo = _split(_silu(c_ref[...]))
    w_hi, w_lo = _split(w_ref[...])
    o_ref[...] = _nn(c_hi, w_hi) + _nn(c_lo, w_hi) + _nn(c_hi, w_lo) + b_ref[...]


def _adaln(c_pad, w, b):
    rows, d = c_pad.shape
    n = w.shape[1]
    tn = ADALN_COLS
    return pl.pallas_call(
        _adaln_kernel,
        grid=(n // tn,),
        in_specs=[
            pl.BlockSpec((rows, d), lambda j: (0, 0)),
            pl.BlockSpec((d, tn), lambda j: (0, j)),
            pl.BlockSpec((1, tn), lambda j: (0, j)),
        ],
        out_specs=pl.BlockSpec((rows, tn), lambda j: (0, j)),
        out_shape=jax.ShapeDtypeStruct((rows, n), F32),
        name="adaln",
    )(c_pad, w, b)


def _inproj_kernel(x_ref, sh_ref, sc_ref, g_ref, w_ref, lb_ref,
                   qda_ref, kda_ref, vda_ref, qd_ref, kd_ref, ke_ref, hi_ref, sg_ref,
                   dec_ref, gda_ref, ghg_ref, *, layer):
    x = x_ref[0]
    tm = x.shape[0]
    h = x * lax.rsqrt(jnp.mean(x * x, axis=-1, keepdims=True) + EPS) * g_ref[...]
    h = h * (1.0 + sc_ref[0]) + sh_ref[0]
    hb = h.astype(BF16)

    def proj(c0, width):
        return _nn(hb, w_ref[:, c0:c0 + width])

    w_da = DA_HEADS * 2 * DA_HEAD_DIM
    w_hg = HG_HEADS * HG_DK
    qda_ref[0] = (proj(0, w_da) * (LOG2E / math.sqrt(DA_HEAD_DIM))).astype(BF16)
    kda_ref[0] = proj(w_da, w_da).astype(BF16)
    vda_ref[0] = proj(2 * w_da, w_da).astype(BF16)
    c0 = 3 * w_da

    lbl = lb_ref[...]
    lbe = jnp.exp(lbl - jnp.max(lbl, axis=0, keepdims=True))
    lbs = lbe / jnp.sum(lbe, axis=0, keepdims=True)
    lb = jnp.sum(lbs[:layer + 1], axis=0, keepdims=True)

    f = lb + (1.0 - lb) * _sigmoid(proj(c0 + w_hg, w_hg))
    logf = jnp.log(f)
    kk = 1.0 - f
    pos = lax.broadcasted_iota(I32, logf.shape, 0) & (CHUNK - 1)
    a = logf
    step = 1
    while step < CHUNK:
        a = a + jnp.where(pos >= step, pltpu.roll(a, step, axis=0), 0.0)
        step *= 2
    n_ch = tm // CHUNK
    a = a.reshape(n_ch, CHUNK, w_hg)
    a_last = a[:, CHUNK - 1:CHUNK, :]
    kk3 = kk.reshape(n_ch, CHUNK, w_hg)
    qq3 = _silu(proj(c0, w_hg)).reshape(n_ch, CHUNK, w_hg)
    qd_ref[0] = (qq3 * jnp.exp(a)).reshape(tm, w_hg).astype(BF16)
    kd_ref[0] = (kk3 * jnp.exp(-a)).reshape(tm, w_hg).astype(BF16)
    ke_ref[0] = (kk3 * jnp.exp(a_last - a)).reshape(tm, w_hg).astype(BF16)
    dec_ref[0] = jnp.exp(a_last).reshape(n_ch, w_hg)
    hi_ref[0] = proj(c0 + 2 * w_hg, w_hg).astype(BF16)
    sg_ref[0] = _silu(proj(c0 + 3 * w_hg, w_hg)).astype(BF16)
    c1 = c0 + 4 * w_hg
    d = x.shape[1]
    gda_ref[0] = _sigmoid(proj(c1, d)).astype(BF16)
    ghg_ref[0] = _sigmoid(proj(c1 + d, d)).astype(BF16)


def _inproj(x, mod, g, w_bf, lb_logits, layer):
    b, s, d = x.shape
    tm = ROW_TILE
    n_cols = w_bf.shape[1]
    w_da = DA_HEADS * 2 * DA_HEAD_DIM
    w_hg = HG_HEADS * HG_DK

    def tok(width, dtype):
        return (pl.BlockSpec((1, tm, width), lambda bi, i: (bi, i, 0)),
                jax.ShapeDtypeStruct((b, s, width), dtype))

    outs = [tok(w_da, BF16)] * 3 + [tok(w_hg, BF16)] * 5
    outs.append((pl.BlockSpec((1, tm // CHUNK, w_hg), lambda bi, i: (bi, i, 0)),
                 jax.ShapeDtypeStruct((b, s // CHUNK, w_hg), F32)))
    outs += [tok(d, BF16)] * 2
    return pl.pallas_call(
        functools.partial(_inproj_kernel, layer=layer),
        grid=(b, s // tm),
        in_specs=[
            pl.BlockSpec((1, tm, d), lambda bi, i: (bi, i, 0)),
            pl.BlockSpec((1, 1, d), lambda bi, i: (bi, 0, 0)),
            pl.BlockSpec((1, 1, d), lambda bi, i: (bi, 0, 1)),
            pl.BlockSpec((1, d), lambda bi, i: (0, 0)),
            pl.BlockSpec((d, n_cols), lambda bi, i: (0, 0), pipeline_mode=pl.Buffered(1)),
            pl.BlockSpec(lb_logits.shape, lambda bi, i: (0, 0)),
        ],
        out_specs=[o[0] for o in outs],
        out_shape=[o[1] for o in outs],
        compiler_params=pltpu.CompilerParams(vmem_limit_bytes=VMEM_LIMIT),
        name="inproj",
    )(x, mod, mod, g, w_bf, lb_logits)


def _attn_kernel(q_ref, k_ref, v_ref, lq1_ref, lk1_ref, lq2_ref, lk2_ref, g_ref, o_ref,
                 kaug_ref, vaug_ref, qaug_ref, m_ref, acc_ref, sa_ref, sb_ref, *, lam_init, slopes):
    hd = pl.program_id(1)
    i = pl.program_id(2)

    tq = q_ref.shape[1]
    hw = q_ref.shape[2]
    n_key = k_ref.shape[1]

    slope = jnp.float32(slopes[-1])
    for idx in range(len(slopes) - 2, -1, -1):
        slope = jnp.where(hd == idx, jnp.float32(slopes[idx]), slope)

    @pl.when(i == 0)
    def _():
        kaug_ref[:hw, :] = k_ref[0].astype(F32).T.astype(BF16)
        jrel = (lax.broadcasted_iota(I32, (hw, n_key), 1) & (tq - 1)).astype(F32) * slope
        krow = lax.broadcasted_iota(I32, (hw, n_key), 0)
        bias_rows = jnp.zeros((hw, n_key), F32)
        rest = jrel
        for term in range(BIAS_TERMS):
            part = rest.astype(BF16).astype(F32)
            bias_rows = jnp.where(krow == term, part, bias_rows)
            rest = rest - part
        kaug_ref[hw:, :] = bias_rows.astype(BF16)
        vaug_ref[:, :hw] = v_ref[0]
        vaug_ref[:, hw:] = jnp.ones((n_key, hw), BF16)

    q = q_ref[0]
    lane = lax.broadcasted_iota(I32, q.shape, 1)
    zero = jnp.zeros_like(q)
    ones = jnp.where(lane < BIAS_TERMS, 1.0, 0.0).astype(BF16)
    qaug_ref[:tq, :hw] = jnp.where(lane < DA_HEAD_DIM, q, zero)
    qaug_ref[tq:, :hw] = jnp.where(lane >= DA_HEAD_DIM, q, zero)
    qaug_ref[:tq, hw:] = ones
    qaug_ref[tq:, hw:] = ones

    m_ref[...] = jnp.full(m_ref.shape, NEG, F32)
    acc_ref[...] = jnp.zeros(acc_ref.shape, F32)

    def scores(j, buf):
        buf[...] = _nn(qaug_ref[...], kaug_ref[:, pl.ds(pl.multiple_of(j * tq, tq), tq)])

    def consume(j, buf, diagonal):
        s = buf[...]
        if diagonal:
            ii = lax.broadcasted_iota(I32, s.shape, 0) & (tq - 1)
            jj = lax.broadcasted_iota(I32, s.shape, 1)
            ahead = jnp.minimum(ii - jj, 0).astype(F32)
            s = s + jnp.where((jj // CHUNK) <= (ii // CHUNK), (2.0 * slope) * ahead, NEG)
            shift = jnp.float32(0.0)
        else:
            shift = -slope * ((i - j) * tq).astype(F32)
        m_old = m_ref[...]
        m_new = jnp.maximum(m_old, jnp.max(s, axis=-1, keepdims=True) + shift)
        p = jnp.exp2(s - jnp.concatenate([m_new - shift] * (tq // LANES), axis=1))
        alpha = jnp.concatenate([jnp.exp2(m_old - m_new)] * (acc_ref.shape[1] // LANES), axis=1)
        vb = vaug_ref[pl.ds(pl.multiple_of(j * tq, tq), tq), :]
        acc_ref[...] = alpha * acc_ref[...] + _nn(p.astype(BF16), vb)
        m_ref[...] = m_new

    scores(0, sa_ref)

    def pair(pi, carry):
        j = 2 * pi
        scores(j + 1, sb_ref)
        consume(j, sa_ref, False)
        scores(j + 2, sa_ref)
        consume(j + 1, sb_ref, False)
        return carry

    lax.fori_loop(0, i // 2, pair, 0)

    @pl.when(i % 2 == 0)
    def _():
        consume(i, sa_ref, True)

    @pl.when(i % 2 == 1)
    def _():
        scores(i, sb_ref)
        consume(i - 1, sa_ref, False)
        consume(i, sb_ref, True)

    acc = acc_ref[...]
    o1 = acc[:tq, :hw] / acc[:tq, hw:]
    o2 = acc[tq:, :hw] / acc[tq:, hw:]
    lam = (jnp.exp(jnp.sum(lq1_ref[...] * lk1_ref[...], axis=-1, keepdims=True))
           - jnp.exp(jnp.sum(lq2_ref[...] * lk2_ref[...], axis=-1, keepdims=True)) + lam_init)
    o = o1 - lam * o2
    o = o * lax.rsqrt(jnp.mean(o * o, axis=-1, keepdims=True) + EPS) * g_ref[...]
    o_ref[0] = (o * (1.0 - lam_init)).astype(BF16)


def _attention(q, k, v, lq1, lk1, lq2, lk2, subln_g, lam_init):
    b, s, _ = q.shape
    tq = ATT_BLOCK
    hw = 2 * DA_HEAD_DIM
    assert hw == LANES
    slopes = tuple(LOG2E * 2.0 ** (-8.0 * (h + 1) / DA_HEADS) for h in range(DA_HEADS))
    vec = lambda n: pl.BlockSpec((1, n), lambda bi, h, i: (0, 0))
    return pl.pallas_call(
        functools.partial(_attn_kernel, lam_init=lam_init, slopes=slopes),
        grid=(b, DA_HEADS, s // tq),
        in_specs=[
            pl.BlockSpec((1, tq, hw), lambda bi, h, i: (bi, i, h)),
            pl.BlockSpec((1, s, hw), lambda bi, h, i: (bi, 0, h)),
            pl.BlockSpec((1, s, hw), lambda bi, h, i: (bi, 0, h)),
            vec(DA_HEAD_DIM), vec(DA_HEAD_DIM), vec(DA_HEAD_DIM), vec(DA_HEAD_DIM), vec(hw),
        ],
        out_specs=pl.BlockSpec((1, tq, hw), lambda bi, h, i: (bi, i, h)),
        out_shape=jax.ShapeDtypeStruct((b, s, DA_HEADS * hw), BF16),
        scratch_shapes=[
            pltpu.VMEM((2 * hw, s), BF16),
            pltpu.VMEM((s, 2 * hw), BF16),
            pltpu.VMEM((2 * tq, 2 * hw), BF16),
            pltpu.VMEM((2 * tq, LANES), F32),
            pltpu.VMEM((2 * tq, 2 * hw), F32),
            pltpu.VMEM((2 * tq, tq), F32),
            pltpu.VMEM((2 * tq, tq), F32),
        ],
        compiler_params=pltpu.CompilerParams(vmem_limit_bytes=VMEM_LIMIT),
        name="attention",
    )(q, k, v, lq1, lk1, lq2, lk2, subln_g)


def _hgrn_kernel(qd_ref, kd_ref, ke_ref, v_ref, sg_ref, dec_ref, g_ref, o_ref, st_ref):
    @pl.when(pl.program_id(1) == 0)
    def _():
        st_ref[...] = jnp.zeros(st_ref.shape, F32)

    n_ch = qd_ref.shape[1] // CHUNK
    r = lax.broadcasted_iota(I32, (CHUNK, CHUNK), 0)
    c = lax.broadcasted_iota(I32, (CHUNK, CHUNK), 1)
    causal = r >= c

    heads = [slice(h * HG_DK, (h + 1) * HG_DK) for h in range(HG_HEADS)]
    chunks = [slice(ci * CHUNK, (ci + 1) * CHUNK) for ci in range(n_ch)]
    incr = [[_nn(v_ref[0, rows, cols].astype(F32).T.astype(BF16), ke_ref[0, rows, cols])
             for rows in chunks] for cols in heads]
    start = []
    for h, cols in enumerate(heads):
        st = st_ref[h]
        before = []
        for ci in range(n_ch):
            before.append(st.astype(BF16))
            st = st * dec_ref[0, ci:ci + 1, cols] + incr[h][ci]
        st_ref[h] = st
        start.append(before)
    for h, cols in enumerate(heads):
        for ci, rows in enumerate(chunks):
            qd = qd_ref[0, rows, cols]
            scores = jnp.where(causal, _nt(qd, kd_ref[0, rows, cols]), 0.0).astype(BF16)
            o = _nn(scores, v_ref[0, rows, cols]) + _nt(qd, start[h][ci])
            o = o * lax.rsqrt(jnp.mean(o * o, axis=-1, keepdims=True) + EPS) * g_ref[...]
            o_ref[0, rows, cols] = (o * sg_ref[0, rows, cols].astype(F32)).astype(BF16)


def _hgrn(qd, kd, ke, v, sg, dec, norm_g):
    b, s, w = qd.shape
    ts = ROW_TILE
    tok = pl.BlockSpec((1, ts, w), lambda bi, i: (bi, i, 0))
    return pl.pallas_call(
        _hgrn_kernel,
        grid=(b, s // ts),
        in_specs=[tok, tok, tok, tok, tok,
                  pl.BlockSpec((1, ts // CHUNK, w), lambda bi, i: (bi, i, 0)),
                  pl.BlockSpec((1, HG_DV), lambda bi, i: (0, 0))],
        out_specs=tok,
        out_shape=jax.ShapeDtypeStruct((b, s, w), BF16),
        scratch_shapes=[pltpu.VMEM((HG_HEADS, HG_DV, HG_DK), F32)],
        name="hgrn",
    )(qd, kd, ke, v, sg, dec, norm_g)


def _merge_kernel(yda_ref, yhg_ref, gda_ref, ghg_ref, x_ref, gt1_ref, sh2_ref, sc2_ref, g2_ref,
                  wda_ref, whg_ref, wout_ref, wrh_ref, wrl_ref, br_ref,
                  x1_ref, h2_ref, eidx_ref, wts_ref):
    u = (gda_ref[0].astype(F32) * _nn(yda_ref[0], wda_ref[...])
         + ghg_ref[0].astype(F32) * _nn(yhg_ref[0], whg_ref[...]))
    x1 = x_ref[0] + gt1_ref[0] * _nn(u.astype(BF16), wout_ref[...])
    x1_ref[0] = x1
    h2 = x1 * lax.rsqrt(jnp.mean(x1 * x1, axis=-1, keepdims=True) + EPS) * g2_ref[...]
    h2 = h2 * (1.0 + sc2_ref[0]) + sh2_ref[0]
    h2_ref[0] = h2.reshape(h2.shape[0], SUBLANES, h2.shape[1] // SUBLANES)

    h_hi, h_lo = _split(h2)
    wrh = wrh_ref[...]
    both = _nt(jnp.concatenate([wrh, wrl_ref[...]], axis=0), h_hi)
    lg = both[:ROUTER_ROWS] + both[ROUTER_ROWS:] + _nt(wrh, h_lo) + br_ref[...]

    tm = h2.shape[0]
    eg = EXPERTS_PER_GROUP
    gl = lg[N_EXPERTS:N_EXPERTS + N_GROUPS]
    gmax = jnp.max(gl, axis=0, keepdims=True)
    g_p = 1.0 / jnp.sum(jnp.exp(gl - gmax), axis=0, keepdims=True)
    gi = lax.broadcasted_iota(I32, (N_GROUPS, tm), 0)
    g_idx = jnp.min(jnp.where(gl == gmax, gi, N_GROUPS), axis=0, keepdims=True)
    sel = lg[(N_GROUPS - 1) * eg:N_GROUPS * eg]
    for g in range(N_GROUPS - 2, -1, -1):
        sel = jnp.where(g_idx == g, lg[g * eg:(g + 1) * eg], sel)
    ei = lax.broadcasted_iota(I32, (eg, tm), 0)
    m1 = jnp.max(sel, axis=0, keepdims=True)
    i1 = jnp.min(jnp.where(sel == m1, ei, eg), axis=0, keepdims=True)
    sel2 = jnp.where(ei == i1, -jnp.inf, sel)
    m2 = jnp.max(sel2, axis=0, keepdims=True)
    i2 = jnp.min(jnp.where(sel2 == m2, ei, eg), axis=0, keepdims=True)
    t = jnp.exp(m2 - m1)
    wa = 1.0 / (1.0 + t)
    eidx_ref[0] = jnp.concatenate([g_idx * eg + i1, g_idx * eg + i2], axis=0)
    wts_ref[0] = jnp.concatenate([g_p * wa, g_p * (t * wa)], axis=0)


def _merge(yda, yhg, gda, ghg, x, mod, g2, wda, whg, wout, wrh, wrl, br):
    b, s, d = x.shape
    tm = ROW_TILE
    w = yda.shape[2]
    tok = lambda width: pl.BlockSpec((1, tm, width), lambda bi, i: (bi, i, 0))
    modspec = lambda k: pl.BlockSpec((1, 1, d), lambda bi, i: (bi, 0, k))
    full = lambda a: pl.BlockSpec(a.shape, lambda bi, i: (0,) * a.ndim)
    lane_rows = pl.BlockSpec((1, 2, tm), lambda bi, i: (bi, 0, i))
    return pl.pallas_call(
        _merge_kernel,
        grid=(b, s // tm),
        in_specs=[tok(w), tok(w), tok(d), tok(d), tok(d),
                  modspec(2), modspec(3), modspec(4), full(g2),
                  full(wda), full(whg), full(wout), full(wrh), full(wrl), full(br)],
        out_specs=[tok(d), pl.BlockSpec((1, tm, SUBLANES, d // SUBLANES), lambda bi, i: (bi, i, 0, 0)),
                   lane_rows, lane_rows],
        out_shape=[jax.ShapeDtypeStruct((b, s, d), F32),
                   jax.ShapeDtypeStruct((b, s, SUBLANES, d // SUBLANES), F32),
                   jax.ShapeDtypeStruct((b, 2, s), I32), jax.ShapeDtypeStruct((b, 2, s), F32)],
        compiler_params=pltpu.CompilerParams(vmem_limit_bytes=VMEM_LIMIT),
        name="merge",
    )(yda, yhg, gda, ghg, x, mod, mod, mod, g2, wda, whg, wout, wrh, wrl, br)


def _positions_kernel(e_ref, slot_ref, blk_ref, nused_ref, pt_ref, off_ref, end_ref, cnt_ref, cnt_smem, sem,
                      *, n_tok):
    e_mat = e_ref[...]
    nr = e_mat.shape[0]
    nb = blk_ref.shape[1]
    r = lax.broadcasted_iota(I32, (LANES, LANES), 0)
    c = lax.broadcasted_iota(I32, (LANES, LANES), 1)
    incl = jnp.where(r <= c, 1.0, 0.0).astype(BF16)
    ones = jnp.ones((LANES, LANES), BF16)
    rr = lax.broadcasted_iota(I32, (nr, nr), 0)
    cc = lax.broadcasted_iota(I32, (nr, nr), 1)
    before = jnp.where(cc < rr, 1.0, 0.0).astype(BF16)
    blk_start = (lax.broadcasted_iota(I32, (1, nb), 1) * EXPERT_BLOCK).astype(F32)

    lane = lax.broadcasted_iota(I32, (1, LANES), 1)

    def per_expert(e, carry):
        pstart, blk, totals, first_blk = carry
        oh = jnp.where(e_mat == e, 1.0, 0.0).astype(BF16)
        in_row = _nn(oh, incl)
        row_tot = _nn(oh, ones)
        row_off = _nn(before, row_tot.astype(BF16))
        total = row_off[nr - 1:nr, :] + row_tot[nr - 1:nr, :]
        pt_ref[e] = in_row.T
        off_ref[e] = row_off
        end_ref[e] = row_off + row_tot
        totals = jnp.where(lane == e, total, totals)
        first_blk = jnp.where(lane == e, pstart * (1.0 / EXPERT_BLOCK), first_blk)
        pend = pstart + jnp.floor((total + (EXPERT_BLOCK - 1)) * (1.0 / EXPERT_BLOCK)) * EXPERT_BLOCK
        blk = blk + jnp.where(blk_start >= pend[:, :1], 1.0, 0.0)
        return pend, blk, totals, first_blk

    zeros = jnp.zeros((1, LANES), F32)
    pend, blk, totals, first_blk = lax.fori_loop(
        0, N_EXPERTS, per_expert, (zeros, jnp.zeros((1, nb), F32), zeros, zeros), unroll=4)
    blk_ref[...] = jnp.minimum(blk, N_EXPERTS - 1).astype(I32)
    nused_ref[...] = (pend * (1.0 / EXPERT_BLOCK)).astype(I32)

    cnt_ref[...] = jnp.zeros(cnt_ref.shape, I32)
    cnt_ref[0:1, :] = totals.astype(I32)
    cnt_ref[1:2, :] = first_blk.astype(I32)
    to_smem = pltpu.make_async_copy(cnt_ref, cnt_smem, sem)
    to_smem.start()
    to_smem.wait()

    slot_ref[...] = jnp.full(slot_ref.shape, 2 * n_tok * n_tok, I32)
    width = slot_ref.shape[1]
    slot_lane = lax.broadcasted_iota(I32, (1, width), 1)
    row_id = lax.broadcasted_iota(I32, (nr, width), 0).astype(F32)

    def per_expert_slots(e, carry):
        total = cnt_smem[0, e]
        blk0 = cnt_smem[1, e]
        ends = jnp.concatenate([end_ref[e]] * (width // LANES), axis=1)
        offs = jnp.concatenate([off_ref[e]] * (width // LANES), axis=1)
        pt = pt_ref[e]

        def per_block(ti, inner):
            s_int = ti * width + slot_lane
            s_loc = s_int.astype(F32)
            row = jnp.sum(jnp.where(ends <= s_loc, 1.0, 0.0), axis=0, keepdims=True)
            onehot = jnp.where(row_id == row, 1.0, 0.0)
            rank = s_loc - jnp.sum(offs * onehot, axis=0, keepdims=True)
            prefix = _nn(pt, onehot)
            lane_in_row = jnp.sum(jnp.where(prefix <= rank, 1.0, 0.0), axis=0, keepdims=True)
            flat = (row * LANES + lane_in_row).astype(I32)
            valid = s_int < total
            tok = jnp.where(valid, jnp.where(flat >= n_tok, flat - n_tok, flat), 0)
            spare = 2 * n_tok + ((blk0 + ti) & 1) * width + slot_lane
            slot_ref[pl.ds(blk0 + ti, 1), :] = jnp.where(valid, flat, spare) * n_tok + tok
            return inner

        lax.fori_loop(0, (total + (width - 1)) // width, per_block, 0)
        return carry

    lax.fori_loop(0, N_EXPERTS, per_expert_slots, 0)


def _positions(e_mat, n_tok, n_blk, n_blk_pad):
    nr = e_mat.shape[0]
    assert EXPERT_BLOCK % LANES == 0 and n_tok & (n_tok - 1) == 0 and (2 * n_tok + 2 * EXPERT_BLOCK) * n_tok < 2 ** 31
    return pl.pallas_call(
        functools.partial(_positions_kernel, n_tok=n_tok),
        out_shape=[jax.ShapeDtypeStruct((-(-n_blk // SUBLANES) * SUBLANES, EXPERT_BLOCK), I32),
                   jax.ShapeDtypeStruct((1, n_blk_pad), I32),
                   jax.ShapeDtypeStruct((1, LANES), I32)],
        scratch_shapes=[pltpu.VMEM((N_EXPERTS, LANES, nr), F32),
                        pltpu.VMEM((N_EXPERTS, nr, LANES), F32),
                        pltpu.VMEM((N_EXPERTS, nr, LANES), F32),
                        pltpu.VMEM((SUBLANES, LANES), I32),
                        pltpu.SMEM((SUBLANES, LANES), I32),
                        pltpu.SemaphoreType.DMA(())],
        name="positions",
    )(e_mat)


def _expert_kernel(blk_ref, nused_ref, slot_ref, h2_ref, w1_ref, w3_ref, w2_ref, y2_ref,
                   xbuf, ybuf, w1b, w3b, w2b, w1s, w3s, w2s, gsem, ssem, wsem, *, n_tok):
    i = pl.program_id(0)
    n_used = nused_ref[0]
    rows = EXPERT_BLOCK
    slot = i % 2
    e = blk_ref[jnp.minimum(i, n_used - 1)]
    fresh = (i == 0) | (blk_ref[jnp.maximum(jnp.minimum(i, n_used - 1) - 1, 0)] != e)

    shift = n_tok.bit_length() - 1

    def gather(block, r, buf):
        tok = slot_ref[block * rows + r] & (n_tok - 1)
        return pltpu.make_async_copy(h2_ref.at[tok], xbuf.at[buf, r], gsem.at[buf])

    def scatter(block, r, buf):
        dst = lax.shift_right_logical(slot_ref[block * rows + r], shift)
        return pltpu.make_async_copy(ybuf.at[buf, r], y2_ref.at[dst], ssem.at[buf])

    def wait_gather(buf):
        pltpu.make_async_copy(h2_ref.at[pl.ds(0, rows)], xbuf.at[buf], gsem.at[buf]).wait()

    def wait_scatter(buf):
        pltpu.make_async_copy(ybuf.at[buf], y2_ref.at[pl.ds(0, rows)], ssem.at[buf]).wait()

    def fetch(expert):
        return (pltpu.make_async_copy(w1_ref.at[expert], w1s, wsem.at[0]),
                pltpu.make_async_copy(w3_ref.at[expert], w3s, wsem.at[1]),
                pltpu.make_async_copy(w2_ref.at[expert], w2s, wsem.at[2]))

    @pl.when(i < n_used)
    def _():
        @pl.when(i == 0)
        def _():
            for cp in fetch(e):
                cp.start()
            ybuf[...] = jnp.zeros(ybuf.shape, F32)

            def prime(r, carry):
                gather(0, r, 0).start()
                return carry

            lax.fori_loop(0, rows, prime, 0, unroll=ISSUE_UNROLL)

        @pl.when(fresh)
        def _():
            for cp in fetch(e):
                cp.wait()
            w1b[...] = w1s[...].astype(BF16)
            w3b[...] = w3s[...].astype(BF16)
            w2b[...] = w2s[...].astype(BF16)
            nxt_e = lax.while_loop(lambda j: (j < n_used) & (blk_ref[jnp.minimum(j, n_used - 1)] == e),
                                   lambda j: j + 1, i + 1)

            @pl.when(nxt_e < n_used)
            def _():
                for cp in fetch(blk_ref[jnp.minimum(nxt_e, n_used - 1)]):
                    cp.start(priority=1)

        wait_gather(slot)

        @pl.when(i > 0)
        def _():
            wait_scatter(slot)

        nxt = jnp.minimum(i + 1, n_used - 1)
        prev = jnp.maximum(i - 1, 0)
        d, de = w1b.shape
        n_up, n_down = de // MXU_COLS, d // MXU_COLS
        portion = rows // (2 * n_up + n_down)

        def issue(part):
            for r in range(part * portion, (part + 1) * portion):
                gather(nxt, r, 1 - slot).start()
                scatter(prev, r, 1 - slot).start(priority=1)

        x = xbuf[slot].reshape(rows, d).astype(BF16)
        part = 0
        up, gate = [], []
        for c in range(n_up):
            cols = slice(c * MXU_COLS, (c + 1) * MXU_COLS)
            up.append(_nn(x, w1b[:, cols]))
            issue(part)
            gate.append(_nn(x, w3b[:, cols]))
            issue(part + 1)
            part += 2
        a = jnp.concatenate(up, axis=1)
        hmid = (_silu(a) * jnp.concatenate(gate, axis=1)).astype(BF16)
        down = []
        for c in range(n_down):
            down.append(_nn(hmid, w2b[:, c * MXU_COLS:(c + 1) * MXU_COLS]))
            issue(part)
            part += 1
        ybuf[slot] = jnp.concatenate(down, axis=1).reshape(ybuf.shape[1:])

    @pl.when(i == n_used)
    def _():
        last = 1 - slot

        def tail(r, carry):
            scatter(n_used - 1, r, last).start(priority=1)
            return carry

        lax.fori_loop(0, rows, tail, 0, unroll=ISSUE_UNROLL)
        wait_scatter(slot)
        wait_scatter(last)
        wait_gather(slot)
        ybuf[...] = jnp.zeros(ybuf.shape, F32)
        for buf in range(2):
            spare = pltpu.make_async_copy(ybuf.at[buf], y2_ref.at[pl.ds(2 * n_tok + buf * rows, rows)], ssem.at[buf])
            spare.start()
            spare.wait()


def _experts(blk_e, n_used, slots, h2, w1, w3, w2):
    n_tok = h2.shape[0]
    tile = h2.shape[1:]
    d, de = w1.shape[1], w1.shape[2]
    n_blk = blk_e.shape[0]
    hbm = pl.BlockSpec(memory_space=pl.ANY)
    return pl.pallas_call(
        functools.partial(_expert_kernel, n_tok=n_tok),
        grid_spec=pltpu.PrefetchScalarGridSpec(
            num_scalar_prefetch=3,
            grid=(n_blk + 1,),
            in_specs=[hbm, hbm, hbm, hbm],
            out_specs=hbm,
            scratch_shapes=[pltpu.VMEM((2, EXPERT_BLOCK) + tile, F32), pltpu.VMEM((2, EXPERT_BLOCK) + tile, F32),
                            pltpu.VMEM((d, de), BF16), pltpu.VMEM((d, de), BF16), pltpu.VMEM((de, d), BF16),
                            pltpu.VMEM((d, de), F32), pltpu.VMEM((d, de), F32), pltpu.VMEM((de, d), F32),
                            pltpu.SemaphoreType.DMA((2,)), pltpu.SemaphoreType.DMA((2,)),
                            pltpu.SemaphoreType.DMA((3,))],
        ),
        out_shape=jax.ShapeDtypeStruct((2 * n_tok + 2 * EXPERT_BLOCK,) + tile, F32),
        compiler_params=pltpu.CompilerParams(vmem_limit_bytes=VMEM_LIMIT,
                                             dimension_semantics=("arbitrary",)),
        name="experts",
    )(blk_e, n_used, slots, h2, w1, w3, w2)


def _combine_kernel(y0_ref, y1_ref, x1_ref, gt2_ref, wts_ref, g_ref, o_ref, *, last):
    wts = wts_ref[...]
    ya = y0_ref[...].reshape(x1_ref.shape)
    yb = y1_ref[...].reshape(x1_ref.shape)
    x = x1_ref[...] + gt2_ref[0] * (wts[:, 0:1] * ya + wts[:, 1:2] * yb)
    if last:
        x = x * lax.rsqrt(jnp.mean(x * x, axis=-1, keepdims=True) + EPS) * g_ref[...]
    o_ref[...] = x


def _combine(y2, x1, mod, wts, final_g, seq, last=True):
    n_tok, d = x1.shape
    tt = GATHER_TILE
    per_batch = seq // tt
    n_tiles = n_tok // tt
    tile = y2.shape[1:]
    return pl.pallas_call(
        functools.partial(_combine_kernel, last=last),
        grid=(n_tiles,),
        in_specs=[pl.BlockSpec((tt,) + tile, lambda i: (i, 0, 0)),
                  pl.BlockSpec((tt,) + tile, lambda i: (i + n_tiles, 0, 0)),
                  pl.BlockSpec((tt, d), lambda i: (i, 0)),
                  pl.BlockSpec((1, 1, d), lambda i: (i // per_batch, 0, 5)),
                  pl.BlockSpec((tt, 2), lambda i: (i, 0)),
                  pl.BlockSpec((1, d), lambda i: (0, 0))],
        out_specs=pl.BlockSpec((tt, d), lambda i: (i, 0)),
        out_shape=jax.ShapeDtypeStruct((n_tok, d), F32),
        name="combine",
    )(y2, y2, x1, mod, wts, final_g)


def kernel(x, c, w_ada, b_ada, norm1_g, w_in, lambda_q1, lambda_k1, lambda_q2, lambda_k2, da_subln_g, hg_lb_logits, hg_norm_g, w_up_da, w_up_hg, w_out, norm2_g, w_rg, b_rg, w_re, b_re, w1, w3, w2, final_g):
    b, s, d = x.shape
    n_tok = b * s
    depth = w_ada.shape[0]
    assert s % ROW_TILE == 0 and s % ATT_BLOCK == 0 and s % GATHER_TILE == 0
    assert (2 * n_tok) % LANES == 0 and d == SUBLANES * LANES

    c_pad = jnp.pad(c, ((0, (-b) % 8), (0, 0)))
    n_slots = 2 * n_tok + N_EXPERTS * EXPERT_BLOCK
    n_blk = n_slots // EXPERT_BLOCK
    n_blk_pad = -(-n_blk // LANES) * LANES

    for l in range(depth):
        lam_init = 0.8 - 0.6 * math.exp(-0.3 * l)
        mod = _adaln(c_pad, w_ada[l], b_ada[l][None, :])[:b].reshape(b, 1, 6 * d)

        (qda, kda, vda, qd, kd, ke, hi, sg, dec, gda, ghg) = _inproj(
            x, mod, norm1_g[l][None, :], w_in[l].astype(BF16), hg_lb_logits, l)
        yda = _attention(qda, kda, vda, lambda_q1[l][None, :], lambda_k1[l][None, :],
                         lambda_q2[l][None, :], lambda_k2[l][None, :], da_subln_g[l][None, :], lam_init)
        yhg = _hgrn(qd, kd, ke, hi, sg, dec, hg_norm_g[l][None, :])

        wr = jnp.concatenate([w_re[l].T, w_rg[l].T,
                              jnp.zeros((ROUTER_ROWS - N_EXPERTS - N_GROUPS, d), F32)], axis=0)
        wr_hi = wr.astype(BF16)
        wr_lo = (wr - wr_hi.astype(F32)).astype(BF16)
        br = jnp.concatenate([b_re[l], b_rg[l],
                              jnp.zeros((ROUTER_ROWS - N_EXPERTS - N_GROUPS,), F32)])[:, None]
        x1, h2, eidx, wts = _merge(yda, yhg, gda, ghg, x, mod, norm2_g[l][None, :],
                                   w_up_da[l].astype(BF16), w_up_hg[l].astype(BF16),
                                   w_out[l].astype(BF16), wr_hi, wr_lo, br)

        e_mat = jnp.transpose(eidx, (1, 0, 2)).reshape(2 * n_tok // LANES, LANES)
        slots, blk_e, n_used = _positions(e_mat, n_tok, n_blk, n_blk_pad)
        h2 = h2.reshape((n_tok,) + h2.shape[2:])
        y2 = _experts(blk_e[0, :n_blk], n_used[0, :1], slots[:n_blk].reshape(n_slots), h2, w1[l], w3[l], w2[l])
        wts_tok = jnp.transpose(wts, (0, 2, 1)).reshape(n_tok, 2)
        x = _combine(y2, x1.reshape(n_tok, d), mod, wts_tok, final_g[None, :], s,
                     last=(l == depth - 1)).reshape(b, s, d)
    return x
```

```python
import functools
import math

import jax
import jax.numpy as jnp
from jax import lax
from jax.experimental import pallas as pl
from jax.experimental.pallas import tpu as pltpu

F32 = jnp.float32
BF16 = jnp.bfloat16
I32 = jnp.int32

EPS = 1e-6
NEG = -1e30
LOG2E = math.log2(math.e)

CHUNK = 64
DA_HEADS = 4
DA_HEAD_DIM = 64
HG_HEADS = 4
HG_DK = 128
HG_DV = 128
N_GROUPS = 4
EXPERTS_PER_GROUP = 8
N_EXPERTS = N_GROUPS * EXPERTS_PER_GROUP

LANES = 128
SUBLANES = 8
MXU_COLS = 256
ADALN_COLS = 1024
ROW_TILE = 512
ATT_BLOCK = 512
BIAS_TERMS = 3
EXPERT_BLOCK = 256
GATHER_TILE = 512
ISSUE_UNROLL = 8
CODE_RING = 4
ROUTER_ROWS = 48
VMEM_LIMIT = 52 * 1024 * 1024


def _nt(a, b):
    return lax.dot_general(a, b, (((1,), (1,)), ((), ())), preferred_element_type=F32)


def _nn(a, b):
    return jnp.dot(a, b, preferred_element_type=F32)


def _split(a):
    hi = a.astype(BF16)
    lo = (a - hi.astype(F32)).astype(BF16)
    return hi, lo


def _sigmoid(x):
    return 1.0 / (1.0 + jnp.exp(-x))


def _silu(x):
    return x * _sigmoid(x)


def _adaln_kernel(c_ref, w_ref, b_ref, o_ref):
    c_hi, c_lo = _split(_silu(c_ref[...]))
    w_hi, w_lo = _split(w_ref[...])
    o_ref[...] = _nn(c_hi, w_hi) + _nn(c_lo, w_hi) + _nn(c_hi, w_lo) + b_ref[...]


def _adaln(c_pad, w, b):
    rows, d = c_pad.shape
    n = w.shape[1]
    tn = ADALN_COLS
    return pl.pallas_call(
        _adaln_kernel,
        grid=(n // tn,),
        in_specs=[
            pl.BlockSpec((rows, d), lambda j: (0, 0)),
            pl.BlockSpec((d, tn), lambda j: (0, j)),
            pl.BlockSpec((1, tn), lambda j: (0, j)),
        ],
        out_specs=pl.BlockSpec((rows, tn), lambda j: (0, j)),
        out_shape=jax.ShapeDtypeStruct((rows, n), F32),
        name="adaln",
    )(c_pad, w, b)


def _inproj_kernel(x_ref, sh_ref, sc_ref, g_ref, w_ref, lb_ref,
                   qda_ref, kda_ref, vda_ref, qd_ref, kd_ref, ke_ref, hi_ref, sg_ref,
                   dec_ref, gda_ref, ghg_ref, *, layer):
    x = x_ref[0]
    tm = x.shape[0]
    h = x * lax.rsqrt(jnp.mean(x * x, axis=-1, keepdims=True) + EPS) * g_ref[...]
    h = h * (1.0 + sc_ref[0]) + sh_ref[0]
    hb = h.astype(BF16)

    def proj(c0, width):
        return _nn(hb, w_ref[:, c0:c0 + width])

    w_da = DA_HEADS * 2 * DA_HEAD_DIM
    w_hg = HG_HEADS * HG_DK
    qda_ref[0] = (proj(0, w_da) * (LOG2E / math.sqrt(DA_HEAD_DIM))).astype(BF16)
    kda_ref[0] = proj(w_da, w_da).astype(BF16)
    vda_ref[0] = proj(2 * w_da, w_da).astype(BF16)
    c0 = 3 * w_da

    lbl = lb_ref[...]
    lbe = jnp.exp(lbl - jnp.max(lbl, axis=0, keepdims=True))
    lbs = lbe / jnp.sum(lbe, axis=0, keepdims=True)
    lb = jnp.sum(lbs[:layer + 1], axis=0, keepdims=True)

    f = lb + (1.0 - lb) * _sigmoid(proj(c0 + w_hg, w_hg))
    logf = jnp.log(f)
    kk = 1.0 - f
    pos = lax.broadcasted_iota(I32, logf.shape, 0) & (CHUNK - 1)
    a = logf
    step = 1
    while step < CHUNK:
        a = a + jnp.where(pos >= step, pltpu.roll(a, step, axis=0), 0.0)
        step *= 2
    n_ch = tm // CHUNK
    a = a.reshape(n_ch, CHUNK, w_hg)
    a_last = a[:, CHUNK - 1:CHUNK, :]
    kk3 = kk.reshape(n_ch, CHUNK, w_hg)
    qq3 = _silu(proj(c0, w_hg)).reshape(n_ch, CHUNK, w_hg)
    qd_ref[0] = (qq3 * jnp.exp(a)).reshape(tm, w_hg).astype(BF16)
    kd_ref[0] = (kk3 * jnp.exp(-a)).reshape(tm, w_hg).astype(BF16)
    ke_ref[0] = (kk3 * jnp.exp(a_last - a)).reshape(tm, w_hg).astype(BF16)
    dec_ref[0] = jnp.exp(a_last).reshape(n_ch, w_hg)
    hi_ref[0] = proj(c0 + 2 * w_hg, w_hg).astype(BF16)
    sg_ref[0] = _silu(proj(c0 + 3 * w_hg, w_hg)).astype(BF16)
    c1 = c0 + 4 * w_hg
    d = x.shape[1]
    gda_ref[0] = _sigmoid(proj(c1, d)).astype(BF16)
    ghg_ref[0] = _sigmoid(proj(c1 + d, d)).astype(BF16)


def _inproj(x, mod, g, w_bf, lb_logits, layer):
    b, s, d = x.shape
    tm = ROW_TILE
    n_cols = w_bf.shape[1]
    w_da = DA_HEADS * 2 * DA_HEAD_DIM
    w_hg = HG_HEADS * HG_DK

    def tok(width, dtype):
        return (pl.BlockSpec((1, tm, width), lambda bi, i: (bi, i, 0)),
                jax.ShapeDtypeStruct((b, s, width), dtype))

    outs = [tok(w_da, BF16)] * 3 + [tok(w_hg, BF16)] * 5
    outs.append((pl.BlockSpec((1, tm // CHUNK, w_hg), lambda bi, i: (bi, i, 0)),
                 jax.ShapeDtypeStruct((b, s // CHUNK, w_hg), F32)))
    outs += [tok(d, BF16)] * 2
    return pl.pallas_call(
        functools.partial(_inproj_kernel, layer=layer),
        grid=(b, s // tm),
        in_specs=[
            pl.BlockSpec((1, tm, d), lambda bi, i: (bi, i, 0)),
            pl.BlockSpec((1, 1, d), lambda bi, i: (bi, 0, 0)),
            pl.BlockSpec((1, 1, d), lambda bi, i: (bi, 0, 1)),
            pl.BlockSpec((1, d), lambda bi, i: (0, 0)),
            pl.BlockSpec((d, n_cols), lambda bi, i: (0, 0), pipeline_mode=pl.Buffered(1)),
            pl.BlockSpec(lb_logits.shape, lambda bi, i: (0, 0)),
        ],
        out_specs=[o[0] for o in outs],
        out_shape=[o[1] for o in outs],
        compiler_params=pltpu.CompilerParams(vmem_limit_bytes=VMEM_LIMIT),
        name="inproj",
    )(x, mod, mod, g, w_bf, lb_logits)


def _attn_kernel(q_ref, k_ref, v_ref, lq1_ref, lk1_ref, lq2_ref, lk2_ref, g_ref, o_ref,
                 kaug_ref, vaug_ref, qaug_ref, m_ref, acc_ref, sa_ref, sb_ref, *, lam_init, slopes):
    hd = pl.program_id(1)
    i = pl.program_id(2)

    tq = q_ref.shape[1]
    hw = q_ref.shape[2]
    n_key = k_ref.shape[1]

    slope = jnp.float32(slopes[-1])
    for idx in range(len(slopes) - 2, -1, -1):
        slope = jnp.where(hd == idx, jnp.float32(slopes[idx]), slope)

    @pl.when(i == 0)
    def _():
        kaug_ref[:hw, :] = k_ref[0].astype(F32).T.astype(BF16)
        jrel = (lax.broadcasted_iota(I32, (hw, n_key), 1) & (tq - 1)).astype(F32) * slope
        krow = lax.broadcasted_iota(I32, (hw, n_key), 0)
        bias_rows = jnp.zeros((hw, n_key), F32)
        rest = jrel
        for term in range(BIAS_TERMS):
            part = rest.astype(BF16).astype(F32)
            bias_rows = jnp.where(krow == term, part, bias_rows)
            rest = rest - part
        kaug_ref[hw:, :] = bias_rows.astype(BF16)
        vaug_ref[:, :hw] = v_ref[0]
        vaug_ref[:, hw:] = jnp.ones((n_key, hw), BF16)

    q = q_ref[0]
    lane = lax.broadcasted_iota(I32, q.shape, 1)
    zero = jnp.zeros_like(q)
    ones = jnp.where(lane < BIAS_TERMS, 1.0, 0.0).astype(BF16)
    qaug_ref[:tq, :hw] = jnp.where(lane < DA_HEAD_DIM, q, zero)
    qaug_ref[tq:, :hw] = jnp.where(lane >= DA_HEAD_DIM, q, zero)
    qaug_ref[:tq, hw:] = ones
    qaug_ref[tq:, hw:] = ones

    m_ref[...] = jnp.full(m_ref.shape, NEG, F32)
    acc_ref[...] = jnp.zeros(acc_ref.shape, F32)

    def scores(j, buf):
        buf[...] = _nn(qaug_ref[...], kaug_ref[:, pl.ds(pl.multiple_of(j * tq, tq), tq)])

    def consume(j, buf, diagonal):
        s = buf[...]
        if diagonal:
            ii = lax.broadcasted_iota(I32, s.shape, 0) & (tq - 1)
            jj = lax.broadcasted_iota(I32, s.shape, 1)
            ahead = jnp.minimum(ii - jj, 0).astype(F32)
            s = s + jnp.where((jj // CHUNK) <= (ii // CHUNK), (2.0 * slope) * ahead, NEG)
            shift = jnp.float32(0.0)
        else:
            shift = -slope * ((i - j) * tq).astype(F32)
        m_old = m_ref[...]
        m_new = jnp.maximum(m_old, jnp.max(s, axis=-1, keepdims=True) + shift)
        p = jnp.exp2(s - jnp.concatenate([m_new - shift] * (tq // LANES), axis=1))
        alpha = jnp.concatenate([jnp.exp2(m_old - m_new)] * (acc_ref.shape[1] // LANES), axis=1)
        vb = vaug_ref[pl.ds(pl.multiple_of(j * tq, tq), tq), :]
        acc_ref[...] = alpha * acc_ref[...] + _nn(p.astype(BF16), vb)
        m_ref[...] = m_new

    scores(0, sa_ref)

    def pair(pi, carry):
        j = 2 * pi
        scores(j + 1, sb_ref)
        consume(j, sa_ref, False)
        scores(j + 2, sa_ref)
        consume(j + 1, sb_ref, False)
        return carry

    lax.fori_loop(0, i // 2, pair, 0)

    @pl.when(i % 2 == 0)
    def _():
        consume(i, sa_ref, True)

    @pl.when(i % 2 == 1)
    def _():
        scores(i, sb_ref)
        consume(i - 1, sa_ref, False)
        consume(i, sb_ref, True)

    acc = acc_ref[...]
    o1 = acc[:tq, :hw] / acc[:tq, hw:]
    o2 = acc[tq:, :hw] / acc[tq:, hw:]
    lam = (jnp.exp(jnp.sum(lq1_ref[...] * lk1_ref[...], axis=-1, keepdims=True))
           - jnp.exp(jnp.sum(lq2_ref[...] * lk2_ref[...], axis=-1, keepdims=True)) + lam_init)
    o = o1 - lam * o2
    o = o * lax.rsqrt(jnp.mean(o * o, axis=-1, keepdims=True) + EPS) * g_ref[...]
    o_ref[0] = (o * (1.0 - lam_init)).astype(BF16)


def _attention(q, k, v, lq1, lk1, lq2, lk2, subln_g, lam_init):
    b, s, _ = q.shape
    tq = ATT_BLOCK
    hw = 2 * DA_HEAD_DIM
    assert hw == LANES
    slopes = tuple(LOG2E * 2.0 ** (-8.0 * (h + 1) / DA_HEADS) for h in range(DA_HEADS))
    vec = lambda n: pl.BlockSpec((1, n), lambda bi, h, i: (0, 0))
    return pl.pallas_call(
        functools.partial(_attn_kernel, lam_init=lam_init, slopes=slopes),
        grid=(b, DA_HEADS, s // tq),
        in_specs=[
            pl.BlockSpec((1, tq, hw), lambda bi, h, i: (bi, i, h)),
            pl.BlockSpec((1, s, hw), lambda bi, h, i: (bi, 0, h)),
            pl.BlockSpec((1, s, hw), lambda bi, h, i: (bi, 0, h)),
            vec(DA_HEAD_DIM), vec(DA_HEAD_DIM), vec(DA_HEAD_DIM), vec(DA_HEAD_DIM), vec(hw),
        ],
        out_specs=pl.BlockSpec((1, tq, hw), lambda bi, h, i: (bi, i, h)),
        out_shape=jax.ShapeDtypeStruct((b, s, DA_HEADS * hw), BF16),
        scratch_shapes=[
            pltpu.VMEM((2 * hw, s), BF16),
            pltpu.VMEM((s, 2 * hw), BF16),
            pltpu.VMEM((2 * tq, 2 * hw), BF16),
            pltpu.VMEM((2 * tq, LANES), F32),
            pltpu.VMEM((2 * tq, 2 * hw), F32),
            pltpu.VMEM((2 * tq, tq), F32),
            pltpu.VMEM((2 * tq, tq), F32),
        ],
        compiler_params=pltpu.CompilerParams(vmem_limit_bytes=VMEM_LIMIT),
        name="attention",
    )(q, k, v, lq1, lk1, lq2, lk2, subln_g)


def _hgrn_kernel(qd_ref, kd_ref, ke_ref, v_ref, sg_ref, dec_ref, g_ref, o_ref, st_ref):
    @pl.when(pl.program_id(1) == 0)
    def _():
        st_ref[...] = jnp.zeros(st_ref.shape, F32)

    n_ch = qd_ref.shape[1] // CHUNK
    r = lax.broadcasted_iota(I32, (CHUNK, CHUNK), 0)
    c = lax.broadcasted_iota(I32, (CHUNK, CHUNK), 1)
    causal = r >= c

    heads = [slice(h * HG_DK, (h + 1) * HG_DK) for h in range(HG_HEADS)]
    chunks = [slice(ci * CHUNK, (ci + 1) * CHUNK) for ci in range(n_ch)]
    incr = [[_nn(v_ref[0, rows, cols].astype(F32).T.astype(BF16), ke_ref[0, rows, cols])
             for rows in chunks] for cols in heads]
    start = []
    for h, cols in enumerate(heads):
        st = st_ref[h]
        before = []
        for ci in range(n_ch):
            before.append(st.astype(BF16))
            st = st * dec_ref[0, ci:ci + 1, cols] + incr[h][ci]
        st_ref[h] = st
        start.append(before)
    for h, cols in enumerate(heads):
        for ci, rows in enumerate(chunks):
            qd = qd_ref[0, rows, cols]
            scores = jnp.where(causal, _nt(qd, kd_ref[0, rows, cols]), 0.0).astype(BF16)
            o = _nn(scores, v_ref[0, rows, cols]) + _nt(qd, start[h][ci])
            o = o * lax.rsqrt(jnp.mean(o * o, axis=-1, keepdims=True) + EPS) * g_ref[...]
            o_ref[0, rows, cols] = (o * sg_ref[0, rows, cols].astype(F32)).astype(BF16)


def _hgrn(qd, kd, ke, v, sg, dec, norm_g):
    b, s, w = qd.shape
    ts = ROW_TILE
    tok = pl.BlockSpec((1, ts, w), lambda bi, i: (bi, i, 0))
    return pl.pallas_call(
        _hgrn_kernel,
        grid=(b, s // ts),
        in_specs=[tok, tok, tok, tok, tok,
                  pl.BlockSpec((1, ts // CHUNK, w), lambda bi, i: (bi, i, 0)),
                  pl.BlockSpec((1, HG_DV), lambda bi, i: (0, 0))],
        out_specs=tok,
        out_shape=jax.ShapeDtypeStruct((b, s, w), BF16),
        scratch_shapes=[pltpu.VMEM((HG_HEADS, HG_DV, HG_DK), F32)],
        name="hgrn",
    )(qd, kd, ke, v, sg, dec, norm_g)


def _merge_kernel(yda_ref, yhg_ref, gda_ref, ghg_ref, x_ref, gt1_ref, sh2_ref, sc2_ref, g2_ref,
                  wda_ref, whg_ref, wout_ref, wrh_ref, wrl_ref, br_ref,
                  x1_ref, h2_ref, eidx_ref, wts_ref):
    u = (gda_ref[0].astype(F32) * _nn(yda_ref[0], wda_ref[...])
         + ghg_ref[0].astype(F32) * _nn(yhg_ref[0], whg_ref[...]))
    x1 = x_ref[0] + gt1_ref[0] * _nn(u.astype(BF16), wout_ref[...])
    x1_ref[0] = x1
    h2 = x1 * lax.rsqrt(jnp.mean(x1 * x1, axis=-1, keepdims=True) + EPS) * g2_ref[...]
    h2 = h2 * (1.0 + sc2_ref[0]) + sh2_ref[0]
    h2_ref[0] = h2.reshape(h2.shape[0], SUBLANES, h2.shape[1] // SUBLANES)

    h_hi, h_lo = _split(h2)
    wrh = wrh_ref[...]
    both = _nt(jnp.concatenate([wrh, wrl_ref[...]], axis=0), h_hi)
    lg = both[:ROUTER_ROWS] + both[ROUTER_ROWS:] + _nt(wrh, h_lo) + br_ref[...]

    tm = h2.shape[0]
    eg = EXPERTS_PER_GROUP
    gl = lg[N_EXPERTS:N_EXPERTS + N_GROUPS]
    gmax = jnp.max(gl, axis=0, keepdims=True)
    g_p = 1.0 / jnp.sum(jnp.exp(gl - gmax), axis=0, keepdims=True)
    gi = lax.broadcasted_iota(I32, (N_GROUPS, tm), 0)
    g_idx = jnp.min(jnp.where(gl == gmax, gi, N_GROUPS), axis=0, keepdims=True)
    sel = lg[(N_GROUPS - 1) * eg:N_GROUPS * eg]
    for g in range(N_GROUPS - 2, -1, -1):
        sel = jnp.where(g_idx == g, lg[g * eg:(g + 1) * eg], sel)
    ei = lax.broadcasted_iota(I32, (eg, tm), 0)
    m1 = jnp.max(sel, axis=0, keepdims=True)
    i1 = jnp.min(jnp.where(sel == m1, ei, eg), axis=0, keepdims=True)
    sel2 = jnp.where(ei == i1, -jnp.inf, sel)
    m2 = jnp.max(sel2, axis=0, keepdims=True)
    i2 = jnp.min(jnp.where(sel2 == m2, ei, eg), axis=0, keepdims=True)
    t = jnp.exp(m2 - m1)
    wa = 1.0 / (1.0 + t)
    eidx_ref[0] = jnp.concatenate([g_idx * eg + i1, g_idx * eg + i2], axis=0)
    wts_ref[0] = jnp.concatenate([g_p * wa, g_p * (t * wa)], axis=0)


def _merge(yda, yhg, gda, ghg, x, mod, g2, wda, whg, wout, wrh, wrl, br):
    b, s, d = x.shape
    tm = ROW_TILE
    w = yda.shape[2]
    tok = lambda width: pl.BlockSpec((1, tm, width), lambda bi, i: (bi, i, 0))
    modspec = lambda k: pl.BlockSpec((1, 1, d), lambda bi, i: (bi, 0, k))
    full = lambda a: pl.BlockSpec(a.shape, lambda bi, i: (0,) * a.ndim)
    lane_rows = pl.BlockSpec((1, 2, tm), lambda bi, i: (bi, 0, i))
    return pl.pallas_call(
        _merge_kernel,
        grid=(b, s // tm),
        in_specs=[tok(w), tok(w), tok(d), tok(d), tok(d),
                  modspec(2), modspec(3), modspec(4), full(g2),
                  full(wda), full(whg), full(wout), full(wrh), full(wrl), full(br)],
        out_specs=[tok(d), pl.BlockSpec((1, tm, SUBLANES, d // SUBLANES), lambda bi, i: (bi, i, 0, 0)),
                   lane_rows, lane_rows],
        out_shape=[jax.ShapeDtypeStruct((b, s, d), F32),
                   jax.ShapeDtypeStruct((b, s, SUBLANES, d // SUBLANES), F32),
                   jax.ShapeDtypeStruct((b, 2, s), I32), jax.ShapeDtypeStruct((b, 2, s), F32)],
        compiler_params=pltpu.CompilerParams(vmem_limit_bytes=VMEM_LIMIT),
        name="merge",
    )(yda, yhg, gda, ghg, x, mod, mod, mod, g2, wda, whg, wout, wrh, wrl, br)


def _positions_kernel(e_ref, slot_ref, blk_ref, nused_ref, pt_ref, off_ref, end_ref, cnt_ref, cnt_smem, sem,
                      *, n_tok):
    e_mat = e_ref[...]
    nr = e_mat.shape[0]
    nb = blk_ref.shape[1]
    r = lax.broadcasted_iota(I32, (LANES, LANES), 0)
    c = lax.broadcasted_iota(I32, (LANES, LANES), 1)
    incl = jnp.where(r <= c, 1.0, 0.0).astype(BF16)
    ones = jnp.ones((LANES, LANES), BF16)
    rr = lax.broadcasted_iota(I32, (nr, nr), 0)
    cc = lax.broadcasted_iota(I32, (nr, nr), 1)
    before = jnp.where(cc < rr, 1.0, 0.0).astype(BF16)
    blk_start = (lax.broadcasted_iota(I32, (1, nb), 1) * EXPERT_BLOCK).astype(F32)

    lane = lax.broadcasted_iota(I32, (1, LANES), 1)

    def per_expert(e, carry):
        pstart, blk, totals, first_blk = carry
        oh = jnp.where(e_mat == e, 1.0, 0.0).astype(BF16)
        in_row = _nn(oh, incl)
        row_tot = _nn(oh, ones)
        row_off = _nn(before, row_tot.astype(BF16))
        total = row_off[nr - 1:nr, :] + row_tot[nr - 1:nr, :]
        pt_ref[e] = in_row.T
        off_ref[e] = row_off
        end_ref[e] = row_off + row_tot
        totals = jnp.where(lane == e, total, totals)
        first_blk = jnp.where(lane == e, pstart * (1.0 / EXPERT_BLOCK), first_blk)
        pend = pstart + jnp.floor((total + (EXPERT_BLOCK - 1)) * (1.0 / EXPERT_BLOCK)) * EXPERT_BLOCK
        blk = blk + jnp.where(blk_start >= pend[:, :1], 1.0, 0.0)
        return pend, blk, totals, first_blk

    zeros = jnp.zeros((1, LANES), F32)
    pend, blk, totals, first_blk = lax.fori_loop(
        0, N_EXPERTS, per_expert, (zeros, jnp.zeros((1, nb), F32), zeros, zeros), unroll=4)
    blk_ref[...] = jnp.minimum(blk, N_EXPERTS - 1).astype(I32)
    nused_ref[...] = (pend * (1.0 / EXPERT_BLOCK)).astype(I32)

    cnt_ref[...] = jnp.zeros(cnt_ref.shape, I32)
    cnt_ref[0:1, :] = totals.astype(I32)
    cnt_ref[1:2, :] = first_blk.astype(I32)
    to_smem = pltpu.make_async_copy(cnt_ref, cnt_smem, sem)
    to_smem.start()
    to_smem.wait()

    slot_ref[...] = jnp.full(slot_ref.shape, 2 * n_tok * n_tok, I32)
    width = slot_ref.shape[1]
    slot_lane = lax.broadcasted_iota(I32, (1, width), 1)
    row_id = lax.broadcasted_iota(I32, (nr, width), 0).astype(F32)

    def per_expert_slots(e, carry):
        total = cnt_smem[0, e]
        blk0 = cnt_smem[1, e]
        ends = jnp.concatenate([end_ref[e]] * (width // LANES), axis=1)
        offs = jnp.concatenate([off_ref[e]] * (width // LANES), axis=1)
        pt = pt_ref[e]

        def per_block(ti, inner):
            s_int = ti * width + slot_lane
            s_loc = s_int.astype(F32)
            row = jnp.sum(jnp.where(ends <= s_loc, 1.0, 0.0), axis=0, keepdims=True)
            onehot = jnp.where(row_id == row, 1.0, 0.0)
            rank = s_loc - jnp.sum(offs * onehot, axis=0, keepdims=True)
            prefix = _nn(pt, onehot)
            lane_in_row = jnp.sum(jnp.where(prefix <= rank, 1.0, 0.0), axis=0, keepdims=True)
            flat = (row * LANES + lane_in_row).astype(I32)
            valid = s_int < total
            tok = jnp.where(valid, jnp.where(flat >= n_tok, flat - n_tok, flat), 0)
            spare = 2 * n_tok + ((blk0 + ti) & 1) * width + slot_lane
            slot_ref[pl.ds(blk0 + ti, 1), :] = jnp.where(valid, flat, spare) * n_tok + tok
            return inner

        lax.fori_loop(0, (total + (width - 1)) // width, per_block, 0)
        return carry

    lax.fori_loop(0, N_EXPERTS, per_expert_slots, 0)


def _positions(e_mat, n_tok, n_blk, n_blk_pad):
    nr = e_mat.shape[0]
    assert EXPERT_BLOCK % LANES == 0 and n_tok & (n_tok - 1) == 0 and (2 * n_tok + 2 * EXPERT_BLOCK) * n_tok < 2 ** 31
    return pl.pallas_call(
        functools.partial(_positions_kernel, n_tok=n_tok),
        out_shape=[jax.ShapeDtypeStruct((-(-n_blk // SUBLANES) * SUBLANES, EXPERT_BLOCK), I32),
                   jax.ShapeDtypeStruct((1, n_blk_pad), I32),
                   jax.ShapeDtypeStruct((1, LANES), I32)],
        scratch_shapes=[pltpu.VMEM((N_EXPERTS, LANES, nr), F32),
                        pltpu.VMEM((N_EXPERTS, nr, LANES), F32),
                        pltpu.VMEM((N_EXPERTS, nr, LANES), F32),
                        pltpu.VMEM((SUBLANES, LANES), I32),
                        pltpu.SMEM((SUBLANES, LANES), I32),
                        pltpu.SemaphoreType.DMA(())],
        name="positions",
    )(e_mat)


def _expert_kernel(blk_ref, nused_ref, slots_ref, h2_ref, w1_ref, w3_ref, w2_ref, y2_ref,
                   xbuf, ybuf, w1b, w3b, w2b, w1s, w3s, w2s, codes, gsem, ssem, wsem, csem, *, n_tok):
    i = pl.program_id(0)
    n_used = nused_ref[0]
    rows = EXPERT_BLOCK
    slot = i % 2
    e = blk_ref[jnp.minimum(i, n_used - 1)]
    fresh = (i == 0) | (blk_ref[jnp.maximum(jnp.minimum(i, n_used - 1) - 1, 0)] != e)

    shift = n_tok.bit_length() - 1

    ring = codes.shape[0]

    def load_codes(block):
        return pltpu.make_async_copy(slots_ref.at[block], codes.at[block % ring], csem.at[block % ring])

    def gather(block, r, buf):
        tok = codes[block % ring, r] & (n_tok - 1)
        return pltpu.make_async_copy(h2_ref.at[tok], xbuf.at[buf, r], gsem.at[buf])

    def scatter(block, r, buf):
        dst = lax.shift_right_logical(codes[block % ring, r], shift)
        return pltpu.make_async_copy(ybuf.at[buf, r], y2_ref.at[dst], ssem.at[buf])

    def wait_gather(buf):
        pltpu.make_async_copy(h2_ref.at[pl.ds(0, rows)], xbuf.at[buf], gsem.at[buf]).wait()

    def wait_scatter(buf):
        pltpu.make_async_copy(ybuf.at[buf], y2_ref.at[pl.ds(0, rows)], ssem.at[buf]).wait()

    def fetch(expert):
        return (pltpu.make_async_copy(w1_ref.at[expert], w1s, wsem.at[0]),
                pltpu.make_async_copy(w3_ref.at[expert], w3s, wsem.at[1]),
                pltpu.make_async_copy(w2_ref.at[expert], w2s, wsem.at[2]))

    @pl.when(i < n_used)
    def _():
        @pl.when(i == 0)
        def _():
            for cp in fetch(e):
                cp.start()
            ybuf[...] = jnp.zeros(ybuf.shape, F32)
            load_codes(0).start()

            @pl.when(n_used > 1)
            def _():
                load_codes(1).start()

            load_codes(0).wait()

            def prime(r, carry):
                gather(0, r, 0).start()
                return carry

            lax.fori_loop(0, rows, prime, 0, unroll=ISSUE_UNROLL)

        @pl.when(fresh)
        def _():
            for cp in fetch(e):
                cp.wait()
            w1b[...] = w1s[...].astype(BF16)
            w3b[...] = w3s[...].astype(BF16)
            w2b[...] = w2s[...].astype(BF16)
            nxt_e = lax.while_loop(lambda j: (j < n_used) & (blk_ref[jnp.minimum(j, n_used - 1)] == e),
                                   lambda j: j + 1, i + 1)

            @pl.when(nxt_e < n_used)
            def _():
                for cp in fetch(blk_ref[jnp.minimum(nxt_e, n_used - 1)]):
                    cp.start(priority=1)

        wait_gather(slot)

        @pl.when(i > 0)
        def _():
            wait_scatter(slot)

        @pl.when(i + 1 < n_used)
        def _():
            load_codes(i + 1).wait()

        @pl.when(i + 2 < n_used)
        def _():
            load_codes(i + 2).start()

        nxt = jnp.minimum(i + 1, n_used - 1)
        prev = jnp.maximum(i - 1, 0)
        d, de = w1b.shape
        n_up, n_down = de // MXU_COLS, d // MXU_COLS
        portion = rows // (2 * n_up + n_down)

        def issue(part):
            for r in range(part * portion, (part + 1) * portion):
                gather(nxt, r, 1 - slot).start()
                scatter(prev, r, 1 - slot).start(priority=1)

        x = xbuf[slot].reshape(rows, d).astype(BF16)
        part = 0
        up, gate = [], []
        for c in range(n_up):
            cols = slice(c * MXU_COLS, (c + 1) * MXU_COLS)
            up.append(_nn(x, w1b[:, cols]))
            issue(part)
            gate.append(_nn(x, w3b[:, cols]))
            issue(part + 1)
            part += 2
        a = jnp.concatenate(up, axis=1)
        hmid = (_silu(a) * jnp.concatenate(gate, axis=1)).astype(BF16)
        down = []
        for c in range(n_down):
            down.append(_nn(hmid, w2b[:, c * MXU_COLS:(c + 1) * MXU_COLS]))
            issue(part)
            part += 1
        ybuf[slot] = jnp.concatenate(down, axis=1).reshape(ybuf.shape[1:])

    @pl.when(i == n_used)
    def _():
        last = 1 - slot

        def tail(r, carry):
            scatter(n_used - 1, r, last).start(priority=1)
            return carry

        lax.fori_loop(0, rows, tail, 0, unroll=ISSUE_UNROLL)
        wait_scatter(slot)
        wait_scatter(last)
        wait_gather(slot)
        ybuf[...] = jnp.zeros(ybuf.shape, F32)
        for buf in range(2):
            spare = pltpu.make_async_copy(ybuf.at[buf], y2_ref.at[pl.ds(2 * n_tok + buf * rows, rows)], ssem.at[buf])
            spare.start()
            spare.wait()


def _experts(blk_e, n_used, slots, h2, w1, w3, w2):
    n_tok = h2.shape[0]
    tile = h2.shape[1:]
    d, de = w1.shape[1], w1.shape[2]
    n_blk = blk_e.shape[0]
    hbm = pl.BlockSpec(memory_space=pl.ANY)
    return pl.pallas_call(
        functools.partial(_expert_kernel, n_tok=n_tok),
        grid_spec=pltpu.PrefetchScalarGridSpec(
            num_scalar_prefetch=2,
            grid=(n_blk + 1,),
            in_specs=[hbm, hbm, hbm, hbm, hbm],
            out_specs=hbm,
            scratch_shapes=[pltpu.VMEM((2, EXPERT_BLOCK) + tile, F32), pltpu.VMEM((2, EXPERT_BLOCK) + tile, F32),
                            pltpu.VMEM((d, de), BF16), pltpu.VMEM((d, de), BF16), pltpu.VMEM((de, d), BF16),
                            pltpu.VMEM((d, de), F32), pltpu.VMEM((d, de), F32), pltpu.VMEM((de, d), F32),
                            pltpu.SMEM((CODE_RING, EXPERT_BLOCK), I32),
                            pltpu.SemaphoreType.DMA((2,)), pltpu.SemaphoreType.DMA((2,)),
                            pltpu.SemaphoreType.DMA((3,)), pltpu.SemaphoreType.DMA((CODE_RING,))],
        ),
        out_shape=jax.ShapeDtypeStruct((2 * n_tok + 2 * EXPERT_BLOCK,) + tile, F32),
        compiler_params=pltpu.CompilerParams(vmem_limit_bytes=VMEM_LIMIT,
                                             dimension_semantics=("arbitrary",)),
        name="experts",
    )(blk_e, n_used, slots, h2, w1, w3, w2)


def _combine_kernel(y0_ref, y1_ref, x1_ref, gt2_ref, wts_ref, g_ref, o_ref, *, last):
    wts = wts_ref[...]
    ya = y0_ref[...].reshape(x1_ref.shape)
    yb = y1_ref[...].reshape(x1_ref.shape)
    x = x1_ref[...] + gt2_ref[0] * (wts[:, 0:1] * ya + wts[:, 1:2] * yb)
    if last:
        x = x * lax.rsqrt(jnp.mean(x * x, axis=-1, keepdims=True) + EPS) * g_ref[...]
    o_ref[...] = x


def _combine(y2, x1, mod, wts, final_g, seq, last=True):
    n_tok, d = x1.shape
    tt = GATHER_TILE
    per_batch = seq // tt
    n_tiles = n_tok // tt
    tile = y2.shape[1:]
    return pl.pallas_call(
        functools.partial(_combine_kernel, last=last),
        grid=(n_tiles,),
        in_specs=[pl.BlockSpec((tt,) + tile, lambda i: (i, 0, 0)),
                  pl.BlockSpec((tt,) + tile, lambda i: (i + n_tiles, 0, 0)),
                  pl.BlockSpec((tt, d), lambda i: (i, 0)),
                  pl.BlockSpec((1, 1, d), lambda i: (i // per_batch, 0, 5)),
                  pl.BlockSpec((tt, 2), lambda i: (i, 0)),
                  pl.BlockSpec((1, d), lambda i: (0, 0))],
        out_specs=pl.BlockSpec((tt, d), lambda i: (i, 0)),
        out_shape=jax.ShapeDtypeStruct((n_tok, d), F32),
        name="combine",
    )(y2, y2, x1, mod, wts, final_g)


def kernel(x, c, w_ada, b_ada, norm1_g, w_in, lambda_q1, lambda_k1, lambda_q2, lambda_k2, da_subln_g, hg_lb_logits, hg_norm_g, w_up_da, w_up_hg, w_out, norm2_g, w_rg, b_rg, w_re, b_re, w1, w3, w2, final_g):
    b, s, d = x.shape
    n_tok = b * s
    depth = w_ada.shape[0]
    assert s % ROW_TILE == 0 and s % ATT_BLOCK == 0 and s % GATHER_TILE == 0
    assert (2 * n_tok) % LANES == 0 and d == SUBLANES * LANES

    c_pad = jnp.pad(c, ((0, (-b) % 8), (0, 0)))
    n_slots = 2 * n_tok + N_EXPERTS * EXPERT_BLOCK
    n_blk = n_slots // EXPERT_BLOCK
    n_blk_pad = -(-n_blk // LANES) * LANES

    for l in range(depth):
        lam_init = 0.8 - 0.6 * math.exp(-0.3 * l)
        mod = _adaln(c_pad, w_ada[l], b_ada[l][None, :])[:b].reshape(b, 1, 6 * d)

        (qda, kda, vda, qd, kd, ke, hi, sg, dec, gda, ghg) = _inproj(
            x, mod, norm1_g[l][None, :], w_in[l].astype(BF16), hg_lb_logits, l)
        yda = _attention(qda, kda, vda, lambda_q1[l][None, :], lambda_k1[l][None, :],
                         lambda_q2[l][None, :], lambda_k2[l][None, :], da_subln_g[l][None, :], lam_init)
        yhg = _hgrn(qd, kd, ke, hi, sg, dec, hg_norm_g[l][None, :])

        wr = jnp.concatenate([w_re[l].T, w_rg[l].T,
                              jnp.zeros((ROUTER_ROWS - N_EXPERTS - N_GROUPS, d), F32)], axis=0)
        wr_hi = wr.astype(BF16)
        wr_lo = (wr - wr_hi.astype(F32)).astype(BF16)
        br = jnp.concatenate([b_re[l], b_rg[l],
                              jnp.zeros((ROUTER_ROWS - N_EXPERTS - N_GROUPS,), F32)])[:, None]
        x1, h2, eidx, wts = _merge(yda, yhg, gda, ghg, x, mod, norm2_g[l][None, :],
                                   w_up_da[l].astype(BF16), w_up_hg[l].astype(BF16),
                                   w_out[l].astype(BF16), wr_hi, wr_lo, br)

        e_mat = jnp.transpose(eidx, (1, 0, 2)).reshape(2 * n_tok // LANES, LANES)
        slots, blk_e, n_used = _positions(e_mat, n_tok, n_blk, n_blk_pad)
        h2 = h2.reshape((n_tok,) + h2.shape[2:])
        y2 = _experts(blk_e[0, :n_blk], n_used[0, :1], slots, h2, w1[l], w3[l], w2[l])
        wts_tok = jnp.transpose(wts, (0, 2, 1)).reshape(n_tok, 2)
        x = _combine(y2, x1.reshape(n_tok, d), mod, wts_tok, final_g[None, :], s,
                     last=(l == depth - 1)).reshape(b, s, d)
    return x
```

```python
import functools
import math

import jax
import jax.numpy as jnp
from jax import lax
from jax.experimental import pallas as pl
from jax.experimental.pallas import tpu as pltpu

F32 = jnp.float32
BF16 = jnp.bfloat16
I32 = jnp.int32

EPS = 1e-6
NEG = -1e30
LOG2E = math.log2(math.e)

CHUNK = 64
DA_HEADS = 4
DA_HEAD_DIM = 64
HG_HEADS = 4
HG_DK = 128
HG_DV = 128
N_GROUPS = 4
EXPERTS_PER_GROUP = 8
N_EXPERTS = N_GROUPS * EXPERTS_PER_GROUP

LANES = 128
SUBLANES = 8
ADALN_COLS = 1024
ROW_TILE = 512
ATT_BLOCK = 512
BIAS_TERMS = 3
EXPERT_BLOCK = 256
GATHER_TILE = 512
ISSUE_UNROLL = 8
ROUTER_ROWS = 48
VMEM_LIMIT = 52 * 1024 * 1024


def _nt(a, b):
    return lax.dot_general(a, b, (((1,), (1,)), ((), ())), preferred_element_type=F32)


def _nn(a, b):
    return jnp.dot(a, b, preferred_element_type=F32)


def _split(a):
    hi = a.astype(BF16)
    lo = (a - hi.astype(F32)).astype(BF16)
    return hi, lo


def _sigmoid(x):
    return 1.0 / (1.0 + jnp.exp(-x))


def _silu(x):
    return x * _sigmoid(x)


def _adaln_kernel(c_ref, w_ref, b_ref, o_ref):
    c_hi, c_lo = _split(_silu(c_ref[...]))
    w_hi, w_lo = _split(w_ref[...])
    o_ref[...] = _nn(c_hi, w_hi) + _nn(c_lo, w_hi) + _nn(c_hi, w_lo) + b_ref[...]


def _adaln(c_pad, w, b):
    rows, d = c_pad.shape
    n = w.shape[1]
    tn = ADALN_COLS
    return pl.pallas_call(
        _adaln_kernel,
        grid=(n // tn,),
        in_specs=[
            pl.BlockSpec((rows, d), lambda j: (0, 0)),
            pl.BlockSpec((d, tn), lambda j: (0, j)),
            pl.BlockSpec((1, tn), lambda j: (0, j)),
        ],
        out_specs=pl.BlockSpec((rows, tn), lambda j: (0, j)),
        out_shape=jax.ShapeDtypeStruct((rows, n), F32),
        name="adaln",
    )(c_pad, w, b)


def _inproj_kernel(x_ref, sh_ref, sc_ref, g_ref, w_ref, lb_ref,
                   qda_ref, kda_ref, vda_ref, qd_ref, kd_ref, ke_ref, hi_ref, sg_ref,
                   dec_ref, gda_ref, ghg_ref, *, layer):
    x = x_ref[0]
    tm = x.shape[0]
    h = x * lax.rsqrt(jnp.mean(x * x, axis=-1, keepdims=True) + EPS) * g_ref[...]
    h = h * (1.0 + sc_ref[0]) + sh_ref[0]
    hb = h.astype(BF16)

    def proj(c0, width):
        return _nn(hb, w_ref[:, c0:c0 + width])

    w_da = DA_HEADS * 2 * DA_HEAD_DIM
    w_hg = HG_HEADS * HG_DK
    qda_ref[0] = (proj(0, w_da) * (LOG2E / math.sqrt(DA_HEAD_DIM))).astype(BF16)
    kda_ref[0] = proj(w_da, w_da).astype(BF16)
    vda_ref[0] = proj(2 * w_da, w_da).astype(BF16)
    c0 = 3 * w_da

    lbl = lb_ref[...]
    lbe = jnp.exp(lbl - jnp.max(lbl, axis=0, keepdims=True))
    lbs = lbe / jnp.sum(lbe, axis=0, keepdims=True)
    lb = jnp.sum(lbs[:layer + 1], axis=0, keepdims=True)

    f = lb + (1.0 - lb) * _sigmoid(proj(c0 + w_hg, w_hg))
    logf = jnp.log(f)
    kk = 1.0 - f
    pos = lax.broadcasted_iota(I32, logf.shape, 0) & (CHUNK - 1)
    a = logf
    step = 1
    while step < CHUNK:
        a = a + jnp.where(pos >= step, pltpu.roll(a, step, axis=0), 0.0)
        step *= 2
    n_ch = tm // CHUNK
    a = a.reshape(n_ch, CHUNK, w_hg)
    a_last = a[:, CHUNK - 1:CHUNK, :]
    kk3 = kk.reshape(n_ch, CHUNK, w_hg)
    qq3 = _silu(proj(c0, w_hg)).reshape(n_ch, CHUNK, w_hg)
    qd_ref[0] = (qq3 * jnp.exp(a)).reshape(tm, w_hg).astype(BF16)
    kd_ref[0] = (kk3 * jnp.exp(-a)).reshape(tm, w_hg).astype(BF16)
    ke_ref[0] = (kk3 * jnp.exp(a_last - a)).reshape(tm, w_hg).astype(BF16)
    dec_ref[0] = jnp.exp(a_last).reshape(n_ch, w_hg)
    hi_ref[0] = proj(c0 + 2 * w_hg, w_hg).astype(BF16)
    sg_ref[0] = _silu(proj(c0 + 3 * w_hg, w_hg)).astype(BF16)
    c1 = c0 + 4 * w_hg
    d = x.shape[1]
    gda_ref[0] = _sigmoid(proj(c1, d)).astype(BF16)
    ghg_ref[0] = _sigmoid(proj(c1 + d, d)).astype(BF16)


def _inproj(x, mod, g, w_bf, lb_logits, layer):
    b, s, d = x.shape
    tm = ROW_TILE
    n_cols = w_bf.shape[1]
    w_da = DA_HEADS * 2 * DA_HEAD_DIM
    w_hg = HG_HEADS * HG_DK

    def tok(width, dtype):
        return (pl.BlockSpec((1, tm, width), lambda bi, i: (bi, i, 0)),
                jax.ShapeDtypeStruct((b, s, width), dtype))

    outs = [tok(w_da, BF16)] * 3 + [tok(w_hg, BF16)] * 5
    outs.append((pl.BlockSpec((1, tm // CHUNK, w_hg), lambda bi, i: (bi, i, 0)),
                 jax.ShapeDtypeStruct((b, s // CHUNK, w_hg), F32)))
    outs += [tok(d, BF16)] * 2
    return pl.pallas_call(
        functools.partial(_inproj_kernel, layer=layer),
        grid=(b, s // tm),
        in_specs=[
            pl.BlockSpec((1, tm, d), lambda bi, i: (bi, i, 0)),
            pl.BlockSpec((1, 1, d), lambda bi, i: (bi, 0, 0)),
            pl.BlockSpec((1, 1, d), lambda bi, i: (bi, 0, 1)),
            pl.BlockSpec((1, d), lambda bi, i: (0, 0)),
            pl.BlockSpec((d, n_cols), lambda bi, i: (0, 0), pipeline_mode=pl.Buffered(1)),
            pl.BlockSpec(lb_logits.shape, lambda bi, i: (0, 0)),
        ],
        out_specs=[o[0] for o in outs],
        out_shape=[o[1] for o in outs],
        compiler_params=pltpu.CompilerParams(vmem_limit_bytes=VMEM_LIMIT),
        name="inproj",
    )(x, mod, mod, g, w_bf, lb_logits)


def _attn_kernel(q_ref, k_ref, v_ref, lq1_ref, lk1_ref, lq2_ref, lk2_ref, g_ref, o_ref, zs_ref,
                 kaug_ref, vaug_ref, qaug_ref, m_ref, acc_ref, sa_ref, sb_ref, zero_ref, zsem,
                 *, lam_init, slopes):
    hd = pl.program_id(1)
    i = pl.program_id(2)

    step = (pl.program_id(0) * pl.num_programs(1) + hd) * pl.num_programs(2) + i
    n_steps = pl.num_programs(0) * pl.num_programs(1) * pl.num_programs(2)
    zrows = zero_ref.shape[0]

    def zero_copy(t):
        return pltpu.make_async_copy(zero_ref, zs_ref.at[pl.ds(t * zrows, zrows)], zsem)

    @pl.when(step == 0)
    def _():
        zero_ref[...] = jnp.zeros(zero_ref.shape, F32)

    @pl.when(step > 0)
    def _():
        zero_copy(step - 1).wait()

    zero_copy(step).start()

    @pl.when(step == n_steps - 1)
    def _():
        zero_copy(step).wait()
    tq = q_ref.shape[1]
    hw = q_ref.shape[2]
    n_key = k_ref.shape[1]

    slope = jnp.float32(slopes[-1])
    for idx in range(len(slopes) - 2, -1, -1):
        slope = jnp.where(hd == idx, jnp.float32(slopes[idx]), slope)

    @pl.when(i == 0)
    def _():
        kaug_ref[:hw, :] = k_ref[0].astype(F32).T.astype(BF16)
        jrel = (lax.broadcasted_iota(I32, (hw, n_key), 1) & (tq - 1)).astype(F32) * slope
        krow = lax.broadcasted_iota(I32, (hw, n_key), 0)
        bias_rows = jnp.zeros((hw, n_key), F32)
        rest = jrel
        for term in range(BIAS_TERMS):
            part = rest.astype(BF16).astype(F32)
            bias_rows = jnp.where(krow == term, part, bias_rows)
            rest = rest - part
        kaug_ref[hw:, :] = bias_rows.astype(BF16)
        vaug_ref[:, :hw] = v_ref[0]
        vaug_ref[:, hw:] = jnp.ones((n_key, hw), BF16)

    q = q_ref[0]
    lane = lax.broadcasted_iota(I32, q.shape, 1)
    zero = jnp.zeros_like(q)
    ones = jnp.where(lane < BIAS_TERMS, 1.0, 0.0).astype(BF16)
    qaug_ref[:tq, :hw] = jnp.where(lane < DA_HEAD_DIM, q, zero)
    qaug_ref[tq:, :hw] = jnp.where(lane >= DA_HEAD_DIM, q, zero)
    qaug_ref[:tq, hw:] = ones
    qaug_ref[tq:, hw:] = ones

    m_ref[...] = jnp.full(m_ref.shape, NEG, F32)
    acc_ref[...] = jnp.zeros(acc_ref.shape, F32)

    def scores(j, buf):
        buf[...] = _nn(qaug_ref[...], kaug_ref[:, pl.ds(pl.multiple_of(j * tq, tq), tq)])

    def consume(j, buf, diagonal):
        s = buf[...]
        if diagonal:
            ii = lax.broadcasted_iota(I32, s.shape, 0) & (tq - 1)
            jj = lax.broadcasted_iota(I32, s.shape, 1)
            ahead = jnp.minimum(ii - jj, 0).astype(F32)
            s = s + jnp.where((jj // CHUNK) <= (ii // CHUNK), (2.0 * slope) * ahead, NEG)
            shift = jnp.float32(0.0)
        else:
            shift = -slope * ((i - j) * tq).astype(F32)
        m_old = m_ref[...]
        m_new = jnp.maximum(m_old, jnp.max(s, axis=-1, keepdims=True) + shift)
        p = jnp.exp2(s - jnp.concatenate([m_new - shift] * (tq // LANES), axis=1))
        alpha = jnp.concatenate([jnp.exp2(m_old - m_new)] * (acc_ref.shape[1] // LANES), axis=1)
        vb = vaug_ref[pl.ds(pl.multiple_of(j * tq, tq), tq), :]
        acc_ref[...] = alpha * acc_ref[...] + _nn(p.astype(BF16), vb)
        m_ref[...] = m_new

    scores(0, sa_ref)

    def pair(pi, carry):
        j = 2 * pi
        scores(j + 1, sb_ref)
        consume(j, sa_ref, False)
        scores(j + 2, sa_ref)
        consume(j + 1, sb_ref, False)
        return carry

    lax.fori_loop(0, i // 2, pair, 0)

    @pl.when(i % 2 == 0)
    def _():
        consume(i, sa_ref, True)

    @pl.when(i % 2 == 1)
    def _():
        scores(i, sb_ref)
        consume(i - 1, sa_ref, False)
        consume(i, sb_ref, True)

    acc = acc_ref[...]
    o1 = acc[:tq, :hw] / acc[:tq, hw:]
    o2 = acc[tq:, :hw] / acc[tq:, hw:]
    lam = (jnp.exp(jnp.sum(lq1_ref[...] * lk1_ref[...], axis=-1, keepdims=True))
           - jnp.exp(jnp.sum(lq2_ref[...] * lk2_ref[...], axis=-1, keepdims=True)) + lam_init)
    o = o1 - lam * o2
    o = o * lax.rsqrt(jnp.mean(o * o, axis=-1, keepdims=True) + EPS) * g_ref[...]
    o_ref[0] = (o * (1.0 - lam_init)).astype(BF16)


def _attention(q, k, v, lq1, lk1, lq2, lk2, subln_g, lam_init, zero_shape):
    b, s, _ = q.shape
    tq = ATT_BLOCK
    hw = 2 * DA_HEAD_DIM
    assert hw == LANES
    n_steps = b * DA_HEADS * (s // tq)
    assert zero_shape[0] % n_steps == 0
    zrows = zero_shape[0] // n_steps
    slopes = tuple(LOG2E * 2.0 ** (-8.0 * (h + 1) / DA_HEADS) for h in range(DA_HEADS))
    vec = lambda n: pl.BlockSpec((1, n), lambda bi, h, i: (0, 0))
    return pl.pallas_call(
        functools.partial(_attn_kernel, lam_init=lam_init, slopes=slopes),
        grid=(b, DA_HEADS, s // tq),
        in_specs=[
            pl.BlockSpec((1, tq, hw), lambda bi, h, i: (bi, i, h)),
            pl.BlockSpec((1, s, hw), lambda bi, h, i: (bi, 0, h)),
            pl.BlockSpec((1, s, hw), lambda bi, h, i: (bi, 0, h)),
            vec(DA_HEAD_DIM), vec(DA_HEAD_DIM), vec(DA_HEAD_DIM), vec(DA_HEAD_DIM), vec(hw),
        ],
        out_specs=[pl.BlockSpec((1, tq, hw), lambda bi, h, i: (bi, i, h)),
                   pl.BlockSpec(memory_space=pl.ANY)],
        out_shape=[jax.ShapeDtypeStruct((b, s, DA_HEADS * hw), BF16),
                   jax.ShapeDtypeStruct(zero_shape, F32)],
        scratch_shapes=[
            pltpu.VMEM((2 * hw, s), BF16),
            pltpu.VMEM((s, 2 * hw), BF16),
            pltpu.VMEM((2 * tq, 2 * hw), BF16),
            pltpu.VMEM((2 * tq, LANES), F32),
            pltpu.VMEM((2 * tq, 2 * hw), F32),
            pltpu.VMEM((2 * tq, tq), F32),
            pltpu.VMEM((2 * tq, tq), F32),
            pltpu.VMEM((zrows,) + tuple(zero_shape[1:]), F32),
            pltpu.SemaphoreType.DMA(()),
        ],
        compiler_params=pltpu.CompilerParams(vmem_limit_bytes=VMEM_LIMIT,
                                             dimension_semantics=("arbitrary",) * 3),
        name="attention",
    )(q, k, v, lq1, lk1, lq2, lk2, subln_g)


def _hgrn_kernel(qd_ref, kd_ref, ke_ref, v_ref, sg_ref, dec_ref, g_ref, w1_ref, w3_ref, w2_ref,
                 o_ref, w1b_ref, w3b_ref, w2b_ref, st_ref):
    @pl.when(pl.program_id(1) == 0)
    def _():
        st_ref[...] = jnp.zeros(st_ref.shape, F32)

    w1b_ref[...] = w1_ref[...].astype(BF16)
    w3b_ref[...] = w3_ref[...].astype(BF16)
    w2b_ref[...] = w2_ref[...].astype(BF16)

    n_ch = qd_ref.shape[1] // CHUNK
    r = lax.broadcasted_iota(I32, (CHUNK, CHUNK), 0)
    c = lax.broadcasted_iota(I32, (CHUNK, CHUNK), 1)
    causal = r >= c

    heads = [slice(h * HG_DK, (h + 1) * HG_DK) for h in range(HG_HEADS)]
    chunks = [slice(ci * CHUNK, (ci + 1) * CHUNK) for ci in range(n_ch)]
    incr = [[_nn(v_ref[0, rows, cols].astype(F32).T.astype(BF16), ke_ref[0, rows, cols])
             for rows in chunks] for cols in heads]
    start = []
    for h, cols in enumerate(heads):
        st = st_ref[h]
        before = []
        for ci in range(n_ch):
            before.append(st.astype(BF16))
            st = st * dec_ref[0, ci:ci + 1, cols] + incr[h][ci]
        st_ref[h] = st
        start.append(before)
    for h, cols in enumerate(heads):
        for ci, rows in enumerate(chunks):
            qd = qd_ref[0, rows, cols]
            scores = jnp.where(causal, _nt(qd, kd_ref[0, rows, cols]), 0.0).astype(BF16)
            o = _nn(scores, v_ref[0, rows, cols]) + _nt(qd, start[h][ci])
            o = o * lax.rsqrt(jnp.mean(o * o, axis=-1, keepdims=True) + EPS) * g_ref[...]
            o_ref[0, rows, cols] = (o * sg_ref[0, rows, cols].astype(F32)).astype(BF16)


def _hgrn(qd, kd, ke, v, sg, dec, norm_g, w1, w3, w2):
    b, s, w = qd.shape
    ts = ROW_TILE
    per_b = s // ts
    n_exp = w1.shape[0]
    assert n_exp % (b * per_b) == 0
    e_step = n_exp // (b * per_b)
    tok = pl.BlockSpec((1, ts, w), lambda bi, i: (bi, i, 0))
    wspec = lambda a: pl.BlockSpec((e_step,) + a.shape[1:], lambda bi, i: (bi * per_b + i, 0, 0))
    wout = lambda a: jax.ShapeDtypeStruct(a.shape, BF16)
    return pl.pallas_call(
        _hgrn_kernel,
        grid=(b, per_b),
        in_specs=[tok, tok, tok, tok, tok,
                  pl.BlockSpec((1, ts // CHUNK, w), lambda bi, i: (bi, i, 0)),
                  pl.BlockSpec((1, HG_DV), lambda bi, i: (0, 0)),
                  wspec(w1), wspec(w3), wspec(w2)],
        out_specs=[tok, wspec(w1), wspec(w3), wspec(w2)],
        out_shape=[jax.ShapeDtypeStruct((b, s, w), BF16), wout(w1), wout(w3), wout(w2)],
        scratch_shapes=[pltpu.VMEM((HG_HEADS, HG_DV, HG_DK), F32)],
        name="hgrn",
    )(qd, kd, ke, v, sg, dec, norm_g, w1, w3, w2)


def _merge_kernel(yda_ref, yhg_ref, gda_ref, ghg_ref, x_ref, gt1_ref, sh2_ref, sc2_ref, g2_ref,
                  wda_ref, whg_ref, wout_ref, wrh_ref, wrl_ref, br_ref,
                  x1_ref, h2_ref, eidx_ref, wts_ref):
    u = (gda_ref[0].astype(F32) * _nn(yda_ref[0], wda_ref[...])
         + ghg_ref[0].astype(F32) * _nn(yhg_ref[0], whg_ref[...]))
    x1 = x_ref[0] + gt1_ref[0] * _nn(u.astype(BF16), wout_ref[...])
    x1_ref[0] = x1
    h2 = x1 * lax.rsqrt(jnp.mean(x1 * x1, axis=-1, keepdims=True) + EPS) * g2_ref[...]
    h2 = h2 * (1.0 + sc2_ref[0]) + sh2_ref[0]
    h2_ref[0] = h2.reshape(h2.shape[0], SUBLANES, h2.shape[1] // SUBLANES)

    h_hi, h_lo = _split(h2)
    wrh = wrh_ref[...]
    both = _nt(jnp.concatenate([wrh, wrl_ref[...]], axis=0), h_hi)
    lg = both[:ROUTER_ROWS] + both[ROUTER_ROWS:] + _nt(wrh, h_lo) + br_ref[...]

    tm = h2.shape[0]
    eg = EXPERTS_PER_GROUP
    gl = lg[N_EXPERTS:N_EXPERTS + N_GROUPS]
    gmax = jnp.max(gl, axis=0, keepdims=True)
    g_p = 1.0 / jnp.sum(jnp.exp(gl - gmax), axis=0, keepdims=True)
    gi = lax.broadcasted_iota(I32, (N_GROUPS, tm), 0)
    g_idx = jnp.min(jnp.where(gl == gmax, gi, N_GROUPS), axis=0, keepdims=True)
    sel = lg[(N_GROUPS - 1) * eg:N_GROUPS * eg]
    for g in range(N_GROUPS - 2, -1, -1):
        sel = jnp.where(g_idx == g, lg[g * eg:(g + 1) * eg], sel)
    ei = lax.broadcasted_iota(I32, (eg, tm), 0)
    m1 = jnp.max(sel, axis=0, keepdims=True)
    i1 = jnp.min(jnp.where(sel == m1, ei, eg), axis=0, keepdims=True)
    sel2 = jnp.where(ei == i1, -jnp.inf, sel)
    m2 = jnp.max(sel2, axis=0, keepdims=True)
    i2 = jnp.min(jnp.where(sel2 == m2, ei, eg), axis=0, keepdims=True)
    t = jnp.exp(m2 - m1)
    wa = 1.0 / (1.0 + t)
    eidx_ref[0] = jnp.concatenate([g_idx * eg + i1, g_idx * eg + i2], axis=0)
    wts_ref[0] = jnp.concatenate([g_p * wa, g_p * (t * wa)], axis=0)


def _merge(yda, yhg, gda, ghg, x, mod, g2, wda, whg, wout, wrh, wrl, br):
    b, s, d = x.shape
    tm = ROW_TILE
    w = yda.shape[2]
    tok = lambda width: pl.BlockSpec((1, tm, width), lambda bi, i: (bi, i, 0))
    modspec = lambda k: pl.BlockSpec((1, 1, d), lambda bi, i: (bi, 0, k))
    full = lambda a: pl.BlockSpec(a.shape, lambda bi, i: (0,) * a.ndim)
    lane_rows = pl.BlockSpec((1, 2, tm), lambda bi, i: (bi, 0, i))
    return pl.pallas_call(
        _merge_kernel,
        grid=(b, s // tm),
        in_specs=[tok(w), tok(w), tok(d), tok(d), tok(d),
                  modspec(2), modspec(3), modspec(4), full(g2),
                  full(wda), full(whg), full(wout), full(wrh), full(wrl), full(br)],
        out_specs=[tok(d), pl.BlockSpec((1, tm, SUBLANES, d // SUBLANES), lambda bi, i: (bi, i, 0, 0)),
                   lane_rows, lane_rows],
        out_shape=[jax.ShapeDtypeStruct((b, s, d), F32),
                   jax.ShapeDtypeStruct((b, s, SUBLANES, d // SUBLANES), F32),
                   jax.ShapeDtypeStruct((b, 2, s), I32), jax.ShapeDtypeStruct((b, 2, s), F32)],
        compiler_params=pltpu.CompilerParams(vmem_limit_bytes=VMEM_LIMIT),
        name="merge",
    )(yda, yhg, gda, ghg, x, mod, mod, mod, g2, wda, whg, wout, wrh, wrl, br)


def _positions_kernel(e_ref, dest_ref, blk_ref, nused_ref):
    e_mat = e_ref[...]
    nr = e_mat.shape[0]
    nb = blk_ref.shape[1]
    r = lax.broadcasted_iota(I32, (LANES, LANES), 0)
    c = lax.broadcasted_iota(I32, (LANES, LANES), 1)
    incl = jnp.where(r <= c, 1.0, 0.0).astype(BF16)
    ones = jnp.ones((LANES, LANES), BF16)
    rr = lax.broadcasted_iota(I32, (nr, nr), 0)
    cc = lax.broadcasted_iota(I32, (nr, nr), 1)
    before = jnp.where(cc < rr, 1.0, 0.0).astype(BF16)
    blk_start = (lax.broadcasted_iota(I32, (1, nb), 1) * EXPERT_BLOCK).astype(F32)

    def per_expert(e, carry):
        pstart, dest, blk = carry
        hit = e_mat == e
        oh = jnp.where(hit, 1.0, 0.0).astype(BF16)
        in_row = _nn(oh, incl)
        row_tot = _nn(oh, ones)
        row_off = _nn(before, row_tot.astype(BF16))
        total = row_off[nr - 1:nr, :] + row_tot[nr - 1:nr, :]
        dest = jnp.where(hit, pstart + row_off + in_row - 1.0, dest)
        pend = pstart + jnp.floor((total + (EXPERT_BLOCK - 1)) * (1.0 / EXPERT_BLOCK)) * EXPERT_BLOCK
        blk = blk + jnp.where(blk_start >= pend[:, :1], 1.0, 0.0)
        return pend, dest, blk

    pstart0 = jnp.zeros((1, LANES), F32)
    pend, dest, blk = lax.fori_loop(
        0, N_EXPERTS, per_expert,
        (pstart0, jnp.zeros(e_mat.shape, F32), jnp.zeros((1, nb), F32)), unroll=4)
    dest_ref[...] = dest.astype(I32)
    blk_ref[...] = jnp.minimum(blk, N_EXPERTS - 1).astype(I32)
    nused_ref[...] = (pend * (1.0 / EXPERT_BLOCK)).astype(I32)


def _positions(e_mat, n_blk_pad):
    nr = e_mat.shape[0]
    return pl.pallas_call(
        _positions_kernel,
        out_shape=[jax.ShapeDtypeStruct((nr, LANES), I32),
                   jax.ShapeDtypeStruct((1, n_blk_pad), I32),
                   jax.ShapeDtypeStruct((1, LANES), I32)],
        name="positions",
    )(e_mat)


def _dispatch_kernel(dest_ref, h2_ref, xs_in_ref, xs_ref, hbuf, in_sem, out_sem, *, n_tok):
    del xs_in_ref
    i = pl.program_id(0)
    last = pl.num_programs(0) - 1
    slot = i % 2

    def load(step, buf):
        return pltpu.make_async_copy(h2_ref.at[pl.ds(step * GATHER_TILE, GATHER_TILE)], hbuf.at[buf],
                                     in_sem.at[buf])

    def drain(buf):
        for k in range(2):
            pltpu.make_async_copy(hbuf.at[buf], xs_ref.at[pl.ds(0, GATHER_TILE)], out_sem.at[buf, k]).wait()

    @pl.when(i == 0)
    def _():
        load(0, 0).start()

    @pl.when(i > 0)
    def _():
        drain(1 - slot)

    @pl.when(i < last)
    def _():
        load(i + 1, 1 - slot).start()

    load(i, slot).wait()
    base = i * GATHER_TILE

    def issue(r, carry):
        for k in range(2):
            pltpu.make_async_copy(hbuf.at[slot, r], xs_ref.at[dest_ref[k * n_tok + base + r]],
                                  out_sem.at[slot, k]).start(priority=k)
        return carry

    lax.fori_loop(0, GATHER_TILE, issue, 0, unroll=ISSUE_UNROLL)

    @pl.when(i == last)
    def _():
        drain(slot)


def _dispatch(dest, h2, xs_zero):
    n_tok = h2.shape[0]
    return pl.pallas_call(
        functools.partial(_dispatch_kernel, n_tok=n_tok),
        grid_spec=pltpu.PrefetchScalarGridSpec(
            num_scalar_prefetch=1,
            grid=(n_tok // GATHER_TILE,),
            in_specs=[pl.BlockSpec(memory_space=pl.ANY), pl.BlockSpec(memory_space=pl.ANY)],
            out_specs=pl.BlockSpec(memory_space=pl.ANY),
            scratch_shapes=[pltpu.VMEM((2, GATHER_TILE) + h2.shape[1:], F32),
                            pltpu.SemaphoreType.DMA((2,)), pltpu.SemaphoreType.DMA((2, 2))],
        ),
        out_shape=jax.ShapeDtypeStruct(xs_zero.shape, xs_zero.dtype),
        input_output_aliases={2: 0},
        compiler_params=pltpu.CompilerParams(dimension_semantics=("arbitrary",)),
        name="dispatch",
    )(dest, h2, xs_zero)


def _expert_kernel(blk_ref, nused_ref, xs_ref, w1_ref, w3_ref, w2_ref, ys_ref,
                   w1b, w3b, w2b, w1s, w3s, w2s, sem):
    i = pl.program_id(0)
    n_used = nused_ref[0]
    e = blk_ref[i]
    fresh = (i == 0) | (blk_ref[jnp.maximum(i - 1, 0)] != e)

    def fetch(expert):
        return (pltpu.make_async_copy(w1_ref.at[expert], w1s, sem.at[0]),
                pltpu.make_async_copy(w3_ref.at[expert], w3s, sem.at[1]),
                pltpu.make_async_copy(w2_ref.at[expert], w2s, sem.at[2]))

    @pl.when(i < n_used)
    def _():
        @pl.when(i == 0)
        def _():
            for cp in fetch(e):
                cp.start()

        @pl.when(fresh)
        def _():
            for cp in fetch(e):
                cp.wait()
            w1b[...] = w1s[...]
            w3b[...] = w3s[...]
            w2b[...] = w2s[...]
            nxt = lax.while_loop(lambda j: (j < n_used) & (blk_ref[jnp.minimum(j, n_used - 1)] == e),
                                 lambda j: j + 1, i + 1)

            @pl.when(nxt < n_used)
            def _():
                for cp in fetch(blk_ref[jnp.minimum(nxt, n_used - 1)]):
                    cp.start(priority=1)

        rows = xs_ref.shape[0]
        x = xs_ref[...].reshape(rows, w1b.shape[0]).astype(BF16)
        a = _nn(x, w1b[...])
        y = _nn((_silu(a) * _nn(x, w3b[...])).astype(BF16), w2b[...])
        ys_ref[...] = y.reshape(ys_ref.shape)

    @pl.when(i >= nused_ref[0])
    def _():
        ys_ref[...] = jnp.zeros(ys_ref.shape, F32)


def _experts(blk_e, n_used, xs, w1, w3, w2):
    p = xs.shape[0]
    tile = xs.shape[1:]
    d, de = w1.shape[1], w1.shape[2]
    rows = lambda i, blk, nu: (jnp.minimum(i, nu[0] - 1), 0, 0)
    hbm = pl.BlockSpec(memory_space=pl.ANY)
    return pl.pallas_call(
        _expert_kernel,
        grid_spec=pltpu.PrefetchScalarGridSpec(
            num_scalar_prefetch=2,
            grid=(p // EXPERT_BLOCK,),
            in_specs=[pl.BlockSpec((EXPERT_BLOCK,) + tile, rows), hbm, hbm, hbm],
            out_specs=pl.BlockSpec((EXPERT_BLOCK,) + tile, lambda i, blk, nu: (i, 0, 0)),
            scratch_shapes=[pltpu.VMEM((d, de), BF16), pltpu.VMEM((d, de), BF16),
                            pltpu.VMEM((de, d), BF16),
                            pltpu.VMEM((d, de), BF16), pltpu.VMEM((d, de), BF16),
                            pltpu.VMEM((de, d), BF16), pltpu.SemaphoreType.DMA((3,))],
        ),
        out_shape=jax.ShapeDtypeStruct(xs.shape, F32),
        compiler_params=pltpu.CompilerParams(vmem_limit_bytes=VMEM_LIMIT,
                                             dimension_semantics=("arbitrary",)),
        name="experts",
    )(blk_e, n_used, xs, w1, w3, w2)


def _combine_kernel(dest_ref, ys_ref, x1_ref, gt2_ref, wts_ref, g_ref, o_ref, ybuf, sem, *, n_tok, last):
    i = pl.program_id(0)
    slot = i % 2

    def gather(step, buf):
        base = step * GATHER_TILE

        def issue(r, carry):
            for k in range(2):
                pltpu.make_async_copy(ys_ref.at[dest_ref[k * n_tok + base + r]], ybuf.at[buf, k, r],
                                      sem.at[buf, k]).start(priority=k)
            return carry

        lax.fori_loop(0, GATHER_TILE, issue, 0, unroll=ISSUE_UNROLL)

    @pl.when(i == 0)
    def _():
        gather(0, 0)

    @pl.when(i + 1 < pl.num_programs(0))
    def _():
        gather(i + 1, 1 - slot)

    for k in range(2):
        pltpu.make_async_copy(ys_ref.at[pl.ds(0, GATHER_TILE)], ybuf.at[slot, k], sem.at[slot, k]).wait()

    wts = wts_ref[...]
    ya = ybuf[slot, 0].reshape(x1_ref.shape)
    yb = ybuf[slot, 1].reshape(x1_ref.shape)
    x = x1_ref[...] + gt2_ref[0] * (wts[:, 0:1] * ya + wts[:, 1:2] * yb)
    if last:
        x = x * lax.rsqrt(jnp.mean(x * x, axis=-1, keepdims=True) + EPS) * g_ref[...]
    o_ref[...] = x


def _combine(dest, ys, x1, mod, wts, final_g, seq, last=True):
    n_tok, d = x1.shape
    tt = GATHER_TILE
    per_batch = seq // tt
    return pl.pallas_call(
        functools.partial(_combine_kernel, n_tok=n_tok, last=last),
        grid_spec=pltpu.PrefetchScalarGridSpec(
            num_scalar_prefetch=1,
            grid=(n_tok // tt,),
            in_specs=[pl.BlockSpec(memory_space=pl.ANY),
                      pl.BlockSpec((tt, d), lambda i, dest: (i, 0)),
                      pl.BlockSpec((1, 1, d), lambda i, dest: (i // per_batch, 0, 5)),
                      pl.BlockSpec((tt, 2), lambda i, dest: (i, 0)),
                      pl.BlockSpec((1, d), lambda i, dest: (0, 0))],
            out_specs=pl.BlockSpec((tt, d), lambda i, dest: (i, 0)),
            scratch_shapes=[pltpu.VMEM((2, 2, tt) + ys.shape[1:], F32),
                            pltpu.SemaphoreType.DMA((2, 2))],
        ),
        out_shape=jax.ShapeDtypeStruct((n_tok, d), F32),
        compiler_params=pltpu.CompilerParams(dimension_semantics=("arbitrary",)),
        name="combine",
    )(dest, ys, x1, mod, wts, final_g)


def kernel(x, c, w_ada, b_ada, norm1_g, w_in, lambda_q1, lambda_k1, lambda_q2, lambda_k2, da_subln_g, hg_lb_logits, hg_norm_g, w_up_da, w_up_hg, w_out, norm2_g, w_rg, b_rg, w_re, b_re, w1, w3, w2, final_g):
    b, s, d = x.shape
    n_tok = b * s
    depth = w_ada.shape[0]
    assert s % ROW_TILE == 0 and s % ATT_BLOCK == 0 and s % GATHER_TILE == 0
    assert (2 * n_tok) % LANES == 0 and d == SUBLANES * LANES

    c_pad = jnp.pad(c, ((0, (-b) % 8), (0, 0)))
    n_slots = 2 * n_tok + N_EXPERTS * EXPERT_BLOCK
    n_blk = n_slots // EXPERT_BLOCK
    n_blk_pad = -(-n_blk // LANES) * LANES

    for l in range(depth):
        lam_init = 0.8 - 0.6 * math.exp(-0.3 * l)
        mod = _adaln(c_pad, w_ada[l], b_ada[l][None, :])[:b].reshape(b, 1, 6 * d)

        (qda, kda, vda, qd, kd, ke, hi, sg, dec, gda, ghg) = _inproj(
            x, mod, norm1_g[l][None, :], w_in[l].astype(BF16), hg_lb_logits, l)
        yda, xs_zero = _attention(qda, kda, vda, lambda_q1[l][None, :], lambda_k1[l][None, :],
                                  lambda_q2[l][None, :], lambda_k2[l][None, :], da_subln_g[l][None, :],
                                  lam_init, (n_slots, SUBLANES, d // SUBLANES))
        yhg, w1_bf, w3_bf, w2_bf = _hgrn(qd, kd, ke, hi, sg, dec, hg_norm_g[l][None, :], w1[l], w3[l], w2[l])

        wr = jnp.concatenate([w_re[l].T, w_rg[l].T,
                              jnp.zeros((ROUTER_ROWS - N_EXPERTS - N_GROUPS, d), F32)], axis=0)
        wr_hi = wr.astype(BF16)
        wr_lo = (wr - wr_hi.astype(F32)).astype(BF16)
        br = jnp.concatenate([b_re[l], b_rg[l],
                              jnp.zeros((ROUTER_ROWS - N_EXPERTS - N_GROUPS,), F32)])[:, None]
        x1, h2, eidx, wts = _merge(yda, yhg, gda, ghg, x, mod, norm2_g[l][None, :],
                                   w_up_da[l].astype(BF16), w_up_hg[l].astype(BF16),
                                   w_out[l].astype(BF16), wr_hi, wr_lo, br)

        e_mat = jnp.transpose(eidx, (1, 0, 2)).reshape(2 * n_tok // LANES, LANES)
        dest, blk_e, n_used = _positions(e_mat, n_blk_pad)
        dest = dest.reshape(2 * n_tok)
        h2 = h2.reshape((n_tok,) + h2.shape[2:])
        xs = _dispatch(dest, h2, xs_zero)
        ys = _experts(blk_e[0, :n_blk], n_used[0, :1], xs, w1_bf, w3_bf, w2_bf)
        wts_tok = jnp.transpose(wts, (0, 2, 1)).reshape(n_tok, 2)
        x = _combine(dest, ys, x1.reshape(n_tok, d), mod, wts_tok, final_g[None, :], s,
                     last=(l == depth - 1)).reshape(b, s, d)
    return x
```

```python
import functools
import math

import jax
import jax.numpy as jnp
from jax import lax
from jax.experimental import pallas as pl
from jax.experimental.pallas import tpu as pltpu

F32 = jnp.float32
BF16 = jnp.bfloat16
I32 = jnp.int32

EPS = 1e-6
NEG = -1e30
LOG2E = math.log2(math.e)

CHUNK = 64
DA_HEADS = 4
DA_HEAD_DIM = 64
HG_HEADS = 4
HG_DK = 128
HG_DV = 128
N_GROUPS = 4
EXPERTS_PER_GROUP = 8
N_EXPERTS = N_GROUPS * EXPERTS_PER_GROUP

LANES = 128
SUBLANES = 8
ADALN_COLS = 1024
ROW_TILE = 512
ATT_BLOCK = 512
BIAS_TERMS = 3
EXPERT_BLOCK = 256
GATHER_TILE = 512
ISSUE_UNROLL = 8
ROUTER_ROWS = 48
VMEM_LIMIT = 52 * 1024 * 1024


def _nt(a, b):
    return lax.dot_general(a, b, (((1,), (1,)), ((), ())), preferred_element_type=F32)


def _nn(a, b):
    return jnp.dot(a, b, preferred_element_type=F32)


def _split(a):
    hi = a.astype(BF16)
    lo = (a - hi.astype(F32)).astype(BF16)
    return hi, lo


def _sigmoid(x):
    return 1.0 / (1.0 + jnp.exp(-x))


def _silu(x):
    return x * _sigmoid(x)


def _adaln_kernel(c_ref, w_ref, b_ref, o_ref):
    c_hi, c_lo = _split(_silu(c_ref[...]))
    w_hi, w_lo = _split(w_ref[...])
    o_ref[...] = _nn(c_hi, w_hi) + _nn(c_lo, w_hi) + _nn(c_hi, w_lo) + b_ref[...]


def _adaln(c_pad, w, b):
    rows, d = c_pad.shape
    n = w.shape[1]
    tn = ADALN_COLS
    return pl.pallas_call(
        _adaln_kernel,
        grid=(n // tn,),
        in_specs=[
            pl.BlockSpec((rows, d), lambda j: (0, 0)),
            pl.BlockSpec((d, tn), lambda j: (0, j)),
            pl.BlockSpec((1, tn), lambda j: (0, j)),
        ],
        out_specs=pl.BlockSpec((rows, tn), lambda j: (0, j)),
        out_shape=jax.ShapeDtypeStruct((rows, n), F32),
        name="adaln",
    )(c_pad, w, b)


def _inproj_kernel(x_ref, sh_ref, sc_ref, g_ref, w_ref, lb_ref, w1_ref, w3_ref, w2_ref,
                   qda_ref, kda_ref, vda_ref, qd_ref, kd_ref, ke_ref, hi_ref, sg_ref,
                   dec_ref, gda_ref, ghg_ref, w1b_ref, w3b_ref, w2b_ref, *, layer):
    w1b_ref[...] = w1_ref[...].astype(BF16)
    w3b_ref[...] = w3_ref[...].astype(BF16)
    w2b_ref[...] = w2_ref[...].astype(BF16)

    x = x_ref[0]
    tm = x.shape[0]
    h = x * lax.rsqrt(jnp.mean(x * x, axis=-1, keepdims=True) + EPS) * g_ref[...]
    h = h * (1.0 + sc_ref[0]) + sh_ref[0]
    hb = h.astype(BF16)

    def proj(c0, width):
        return _nn(hb, w_ref[:, c0:c0 + width])

    w_da = DA_HEADS * 2 * DA_HEAD_DIM
    w_hg = HG_HEADS * HG_DK
    qda_ref[0] = (proj(0, w_da) * (LOG2E / math.sqrt(DA_HEAD_DIM))).astype(BF16)
    kda_ref[0] = proj(w_da, w_da).astype(BF16)
    vda_ref[0] = proj(2 * w_da, w_da).astype(BF16)
    c0 = 3 * w_da

    lbl = lb_ref[...]
    lbe = jnp.exp(lbl - jnp.max(lbl, axis=0, keepdims=True))
    lbs = lbe / jnp.sum(lbe, axis=0, keepdims=True)
    lb = jnp.sum(lbs[:layer + 1], axis=0, keepdims=True)

    f = lb + (1.0 - lb) * _sigmoid(proj(c0 + w_hg, w_hg))
    logf = jnp.log(f)
    kk = 1.0 - f
    pos = lax.broadcasted_iota(I32, logf.shape, 0) & (CHUNK - 1)
    a = logf
    step = 1
    while step < CHUNK:
        a = a + jnp.where(pos >= step, pltpu.roll(a, step, axis=0), 0.0)
        step *= 2
    n_ch = tm // CHUNK
    a = a.reshape(n_ch, CHUNK, w_hg)
    a_last = a[:, CHUNK - 1:CHUNK, :]
    kk3 = kk.reshape(n_ch, CHUNK, w_hg)
    qq3 = _silu(proj(c0, w_hg)).reshape(n_ch, CHUNK, w_hg)
    qd_ref[0] = (qq3 * jnp.exp(a)).reshape(tm, w_hg).astype(BF16)
    kd_ref[0] = (kk3 * jnp.exp(-a)).reshape(tm, w_hg).astype(BF16)
    ke_ref[0] = (kk3 * jnp.exp(a_last - a)).reshape(tm, w_hg).astype(BF16)
    dec_ref[0] = jnp.exp(a_last).reshape(n_ch, w_hg)
    hi_ref[0] = proj(c0 + 2 * w_hg, w_hg).astype(BF16)
    sg_ref[0] = _silu(proj(c0 + 3 * w_hg, w_hg)).astype(BF16)
    c1 = c0 + 4 * w_hg
    d = x.shape[1]
    gda_ref[0] = _sigmoid(proj(c1, d)).astype(BF16)
    ghg_ref[0] = _sigmoid(proj(c1 + d, d)).astype(BF16)


def _inproj(x, mod, g, w_bf, lb_logits, layer, w1, w3, w2):
    b, s, d = x.shape
    tm = ROW_TILE
    n_cols = w_bf.shape[1]
    per_b = s // tm
    assert w1.shape[0] % (b * per_b) == 0
    e_step = w1.shape[0] // (b * per_b)
    wspec = lambda a: pl.BlockSpec((e_step,) + a.shape[1:], lambda bi, i: (bi * per_b + i, 0, 0))
    w_da = DA_HEADS * 2 * DA_HEAD_DIM
    w_hg = HG_HEADS * HG_DK

    def tok(width, dtype):
        return (pl.BlockSpec((1, tm, width), lambda bi, i: (bi, i, 0)),
                jax.ShapeDtypeStruct((b, s, width), dtype))

    outs = [tok(w_da, BF16)] * 3 + [tok(w_hg, BF16)] * 5
    outs.append((pl.BlockSpec((1, tm // CHUNK, w_hg), lambda bi, i: (bi, i, 0)),
                 jax.ShapeDtypeStruct((b, s // CHUNK, w_hg), F32)))
    outs += [tok(d, BF16)] * 2
    outs += [(wspec(a), jax.ShapeDtypeStruct(a.shape, BF16)) for a in (w1, w3, w2)]
    return pl.pallas_call(
        functools.partial(_inproj_kernel, layer=layer),
        grid=(b, s // tm),
        in_specs=[
            pl.BlockSpec((1, tm, d), lambda bi, i: (bi, i, 0)),
            pl.BlockSpec((1, 1, d), lambda bi, i: (bi, 0, 0)),
            pl.BlockSpec((1, 1, d), lambda bi, i: (bi, 0, 1)),
            pl.BlockSpec((1, d), lambda bi, i: (0, 0)),
            pl.BlockSpec((d, n_cols), lambda bi, i: (0, 0), pipeline_mode=pl.Buffered(1)),
            pl.BlockSpec(lb_logits.shape, lambda bi, i: (0, 0)),
            wspec(w1), wspec(w3), wspec(w2),
        ],
        out_specs=[o[0] for o in outs],
        out_shape=[o[1] for o in outs],
        compiler_params=pltpu.CompilerParams(vmem_limit_bytes=VMEM_LIMIT),
        name="inproj",
    )(x, mod, mod, g, w_bf, lb_logits, w1, w3, w2)


def _attn_kernel(q_ref, k_ref, v_ref, lq1_ref, lk1_ref, lq2_ref, lk2_ref, g_ref, o_ref, zs_ref,
                 kaug_ref, vaug_ref, qaug_ref, m_ref, acc_ref, sa_ref, sb_ref, zero_ref, zsem,
                 *, lam_init, slopes):
    hd = pl.program_id(1)
    i = pl.program_id(2)

    step = (pl.program_id(0) * pl.num_programs(1) + hd) * pl.num_programs(2) + i
    n_steps = pl.num_programs(0) * pl.num_programs(1) * pl.num_programs(2)
    zrows = zero_ref.shape[0]

    def zero_copy(t):
        return pltpu.make_async_copy(zero_ref, zs_ref.at[pl.ds(t * zrows, zrows)], zsem)

    @pl.when(step == 0)
    def _():
        zero_ref[...] = jnp.zeros(zero_ref.shape, F32)

    @pl.when(step > 0)
    def _():
        zero_copy(step - 1).wait()

    zero_copy(step).start()

    @pl.when(step == n_steps - 1)
    def _():
        zero_copy(step).wait()
    tq = q_ref.shape[1]
    hw = q_ref.shape[2]
    n_key = k_ref.shape[1]

    slope = jnp.float32(slopes[-1])
    for idx in range(len(slopes) - 2, -1, -1):
        slope = jnp.where(hd == idx, jnp.float32(slopes[idx]), slope)

    @pl.when(i == 0)
    def _():
        kaug_ref[:hw, :] = k_ref[0].astype(F32).T.astype(BF16)
        jrel = (lax.broadcasted_iota(I32, (hw, n_key), 1) & (tq - 1)).astype(F32) * slope
        krow = lax.broadcasted_iota(I32, (hw, n_key), 0)
        bias_rows = jnp.zeros((hw, n_key), F32)
        rest = jrel
        for term in range(BIAS_TERMS):
            part = rest.astype(BF16).astype(F32)
            bias_rows = jnp.where(krow == term, part, bias_rows)
            rest = rest - part
        kaug_ref[hw:, :] = bias_rows.astype(BF16)
        vaug_ref[:, :hw] = v_ref[0]
        vaug_ref[:, hw:] = jnp.ones((n_key, hw), BF16)

    q = q_ref[0]
    lane = lax.broadcasted_iota(I32, q.shape, 1)
    zero = jnp.zeros_like(q)
    ones = jnp.where(lane < BIAS_TERMS, 1.0, 0.0).astype(BF16)
    qaug_ref[:tq, :hw] = jnp.where(lane < DA_HEAD_DIM, q, zero)
    qaug_ref[tq:, :hw] = jnp.where(lane >= DA_HEAD_DIM, q, zero)
    qaug_ref[:tq, hw:] = ones
    qaug_ref[tq:, hw:] = ones

    m_ref[...] = jnp.full(m_ref.shape, NEG, F32)
    acc_ref[...] = jnp.zeros(acc_ref.shape, F32)

    def scores(j, buf):
        buf[...] = _nn(qaug_ref[...], kaug_ref[:, pl.ds(pl.multiple_of(j * tq, tq), tq)])

    def consume(j, buf, diagonal):
        s = buf[...]
        if diagonal:
            ii = lax.broadcasted_iota(I32, s.shape, 0) & (tq - 1)
            jj = lax.broadcasted_iota(I32, s.shape, 1)
            ahead = jnp.minimum(ii - jj, 0).astype(F32)
            s = s + jnp.where((jj // CHUNK) <= (ii // CHUNK), (2.0 * slope) * ahead, NEG)
            shift = jnp.float32(0.0)
        else:
            shift = -slope * ((i - j) * tq).astype(F32)
        m_old = m_ref[...]
        m_new = jnp.maximum(m_old, jnp.max(s, axis=-1, keepdims=True) + shift)
        p = jnp.exp2(s - jnp.concatenate([m_new - shift] * (tq // LANES), axis=1))
        alpha = jnp.concatenate([jnp.exp2(m_old - m_new)] * (acc_ref.shape[1] // LANES), axis=1)
        vb = vaug_ref[pl.ds(pl.multiple_of(j * tq, tq), tq), :]
        acc_ref[...] = alpha * acc_ref[...] + _nn(p.astype(BF16), vb)
        m_ref[...] = m_new

    scores(0, sa_ref)

    def pair(pi, carry):
        j = 2 * pi
        scores(j + 1, sb_ref)
        consume(j, sa_ref, False)
        scores(j + 2, sa_ref)
        consume(j + 1, sb_ref, False)
        return carry

    lax.fori_loop(0, i // 2, pair, 0)

    @pl.when(i % 2 == 0)
    def _():
        consume(i, sa_ref, True)

    @pl.when(i % 2 == 1)
    def _():
        scores(i, sb_ref)
        consume(i - 1, sa_ref, False)
        consume(i, sb_ref, True)

    acc = acc_ref[...]
    o1 = acc[:tq, :hw] / acc[:tq, hw:]
    o2 = acc[tq:, :hw] / acc[tq:, hw:]
    lam = (jnp.exp(jnp.sum(lq1_ref[...] * lk1_ref[...], axis=-1, keepdims=True))
           - jnp.exp(jnp.sum(lq2_ref[...] * lk2_ref[...], axis=-1, keepdims=True)) + lam_init)
    o = o1 - lam * o2
    o = o * lax.rsqrt(jnp.mean(o * o, axis=-1, keepdims=True) + EPS) * g_ref[...]
    o_ref[0] = (o * (1.0 - lam_init)).astype(BF16)


def _attention(q, k, v, lq1, lk1, lq2, lk2, subln_g, lam_init, zero_shape):
    b, s, _ = q.shape
    tq = ATT_BLOCK
    hw = 2 * DA_HEAD_DIM
    assert hw == LANES
    n_steps = b * DA_HEADS * (s // tq)
    assert zero_shape[0] % n_steps == 0
    zrows = zero_shape[0] // n_steps
    slopes = tuple(LOG2E * 2.0 ** (-8.0 * (h + 1) / DA_HEADS) for h in range(DA_HEADS))
    vec = lambda n: pl.BlockSpec((1, n), lambda bi, h, i: (0, 0))
    return pl.pallas_call(
        functools.partial(_attn_kernel, lam_init=lam_init, slopes=slopes),
        grid=(b, DA_HEADS, s // tq),
        in_specs=[
            pl.BlockSpec((1, tq, hw), lambda bi, h, i: (bi, i, h)),
            pl.BlockSpec((1, s, hw), lambda bi, h, i: (bi, 0, h)),
            pl.BlockSpec((1, s, hw), lambda bi, h, i: (bi, 0, h)),
            vec(DA_HEAD_DIM), vec(DA_HEAD_DIM), vec(DA_HEAD_DIM), vec(DA_HEAD_DIM), vec(hw),
        ],
        out_specs=[pl.BlockSpec((1, tq, hw), lambda bi, h, i: (bi, i, h)),
                   pl.BlockSpec(memory_space=pl.ANY)],
        out_shape=[jax.ShapeDtypeStruct((b, s, DA_HEADS * hw), BF16),
                   jax.ShapeDtypeStruct(zero_shape, F32)],
        scratch_shapes=[
            pltpu.VMEM((2 * hw, s), BF16),
            pltpu.VMEM((s, 2 * hw), BF16),
            pltpu.VMEM((2 * tq, 2 * hw), BF16),
            pltpu.VMEM((2 * tq, LANES), F32),
            pltpu.VMEM((2 * tq, 2 * hw), F32),
            pltpu.VMEM((2 * tq, tq), F32),
            pltpu.VMEM((2 * tq, tq), F32),
            pltpu.VMEM((zrows,) + tuple(zero_shape[1:]), F32),
            pltpu.SemaphoreType.DMA(()),
        ],
        compiler_params=pltpu.CompilerParams(vmem_limit_bytes=VMEM_LIMIT,
                                             dimension_semantics=("arbitrary",) * 3),
        name="attention",
    )(q, k, v, lq1, lk1, lq2, lk2, subln_g)


def _hgrn_kernel(qd_ref, kd_ref, ke_ref, v_ref, sg_ref, dec_ref, g_ref, o_ref, st_ref):
    @pl.when(pl.program_id(1) == 0)
    def _():
        st_ref[...] = jnp.zeros(st_ref.shape, F32)

    n_ch = qd_ref.shape[1] // CHUNK
    r = lax.broadcasted_iota(I32, (CHUNK, CHUNK), 0)
    c = lax.broadcasted_iota(I32, (CHUNK, CHUNK), 1)
    causal = r >= c

    heads = [slice(h * HG_DK, (h + 1) * HG_DK) for h in range(HG_HEADS)]
    chunks = [slice(ci * CHUNK, (ci + 1) * CHUNK) for ci in range(n_ch)]
    incr = [[_nn(v_ref[0, rows, cols].astype(F32).T.astype(BF16), ke_ref[0, rows, cols])
             for rows in chunks] for cols in heads]
    start = []
    for h, cols in enumerate(heads):
        st = st_ref[h]
        before = []
        for ci in range(n_ch):
            before.append(st.astype(BF16))
            st = st * dec_ref[0, ci:ci + 1, cols] + incr[h][ci]
        st_ref[h] = st
        start.append(before)
    for h, cols in enumerate(heads):
        for ci, rows in enumerate(chunks):
            qd = qd_ref[0, rows, cols]
            scores = jnp.where(causal, _nt(qd, kd_ref[0, rows, cols]), 0.0).astype(BF16)
            o = _nn(scores, v_ref[0, rows, cols]) + _nt(qd, start[h][ci])
            o = o * lax.rsqrt(jnp.mean(o * o, axis=-1, keepdims=True) + EPS) * g_ref[...]
            o_ref[0, rows, cols] = (o * sg_ref[0, rows, cols].astype(F32)).astype(BF16)


def _hgrn(qd, kd, ke, v, sg, dec, norm_g):
    b, s, w = qd.shape
    ts = ROW_TILE
    tok = pl.BlockSpec((1, ts, w), lambda bi, i: (bi, i, 0))
    return pl.pallas_call(
        _hgrn_kernel,
        grid=(b, s // ts),
        in_specs=[tok, tok, tok, tok, tok,
                  pl.BlockSpec((1, ts // CHUNK, w), lambda bi, i: (bi, i, 0)),
                  pl.BlockSpec((1, HG_DV), lambda bi, i: (0, 0))],
        out_specs=tok,
        out_shape=jax.ShapeDtypeStruct((b, s, w), BF16),
        scratch_shapes=[pltpu.VMEM((HG_HEADS, HG_DV, HG_DK), F32)],
        name="hgrn",
    )(qd, kd, ke, v, sg, dec, norm_g)


def _merge_kernel(yda_ref, yhg_ref, gda_ref, ghg_ref, x_ref, gt1_ref, sh2_ref, sc2_ref, g2_ref,
                  wda_ref, whg_ref, wout_ref, wrh_ref, wrl_ref, br_ref,
                  x1_ref, h2_ref, eidx_ref, wts_ref):
    u = (gda_ref[0].astype(F32) * _nn(yda_ref[0], wda_ref[...])
         + ghg_ref[0].astype(F32) * _nn(yhg_ref[0], whg_ref[...]))
    x1 = x_ref[0] + gt1_ref[0] * _nn(u.astype(BF16), wout_ref[...])
    x1_ref[0] = x1
    h2 = x1 * lax.rsqrt(jnp.mean(x1 * x1, axis=-1, keepdims=True) + EPS) * g2_ref[...]
    h2 = h2 * (1.0 + sc2_ref[0]) + sh2_ref[0]
    h2_ref[0] = h2.reshape(h2.shape[0], SUBLANES, h2.shape[1] // SUBLANES)

    h_hi, h_lo = _split(h2)
    wrh = wrh_ref[...]
    both = _nt(jnp.concatenate([wrh, wrl_ref[...]], axis=0), h_hi)
    lg = both[:ROUTER_ROWS] + both[ROUTER_ROWS:] + _nt(wrh, h_lo) + br_ref[...]

    tm = h2.shape[0]
    eg = EXPERTS_PER_GROUP
    gl = lg[N_EXPERTS:N_EXPERTS + N_GROUPS]
    gmax = jnp.max(gl, axis=0, keepdims=True)
    g_p = 1.0 / jnp.sum(jnp.exp(gl - gmax), axis=0, keepdims=True)
    gi = lax.broadcasted_iota(I32, (N_GROUPS, tm), 0)
    g_idx = jnp.min(jnp.where(gl == gmax, gi, N_GROUPS), axis=0, keepdims=True)
    sel = lg[(N_GROUPS - 1) * eg:N_GROUPS * eg]
    for g in range(N_GROUPS - 2, -1, -1):
        sel = jnp.where(g_idx == g, lg[g * eg:(g + 1) * eg], sel)
    ei = lax.broadcasted_iota(I32, (eg, tm), 0)
    m1 = jnp.max(sel, axis=0, keepdims=True)
    i1 = jnp.min(jnp.where(sel == m1, ei, eg), axis=0, keepdims=True)
    sel2 = jnp.where(ei == i1, -jnp.inf, sel)
    m2 = jnp.max(sel2, axis=0, keepdims=True)
    i2 = jnp.min(jnp.where(sel2 == m2, ei, eg), axis=0, keepdims=True)
    t = jnp.exp(m2 - m1)
    wa = 1.0 / (1.0 + t)
    eidx_ref[0] = jnp.concatenate([g_idx * eg + i1, g_idx * eg + i2], axis=0)
    wts_ref[0] = jnp.concatenate([g_p * wa, g_p * (t * wa)], axis=0)


def _merge(yda, yhg, gda, ghg, x, mod, g2, wda, whg, wout, wrh, wrl, br):
    b, s, d = x.shape
    tm = ROW_TILE
    w = yda.shape[2]
    tok = lambda width: pl.BlockSpec((1, tm, width), lambda bi, i: (bi, i, 0))
    modspec = lambda k: pl.BlockSpec((1, 1, d), lambda bi, i: (bi, 0, k))
    full = lambda a: pl.BlockSpec(a.shape, lambda bi, i: (0,) * a.ndim)
    lane_rows = pl.BlockSpec((1, 2, tm), lambda bi, i: (bi, 0, i))
    return pl.pallas_call(
        _merge_kernel,
        grid=(b, s // tm),
        in_specs=[tok(w), tok(w), tok(d), tok(d), tok(d),
                  modspec(2), modspec(3), modspec(4), full(g2),
                  full(wda), full(whg), full(wout), full(wrh), full(wrl), full(br)],
        out_specs=[tok(d), pl.BlockSpec((1, tm, SUBLANES, d // SUBLANES), lambda bi, i: (bi, i, 0, 0)),
                   lane_rows, lane_rows],
        out_shape=[jax.ShapeDtypeStruct((b, s, d), F32),
                   jax.ShapeDtypeStruct((b, s, SUBLANES, d // SUBLANES), F32),
                   jax.ShapeDtypeStruct((b, 2, s), I32), jax.ShapeDtypeStruct((b, 2, s), F32)],
        compiler_params=pltpu.CompilerParams(vmem_limit_bytes=VMEM_LIMIT),
        name="merge",
    )(yda, yhg, gda, ghg, x, mod, mod, mod, g2, wda, whg, wout, wrh, wrl, br)


def _positions_kernel(e_ref, dest_ref, blk_ref, nused_ref):
    e_mat = e_ref[...]
    nr = e_mat.shape[0]
    nb = blk_ref.shape[1]
    r = lax.broadcasted_iota(I32, (LANES, LANES), 0)
    c = lax.broadcasted_iota(I32, (LANES, LANES), 1)
    incl = jnp.where(r <= c, 1.0, 0.0).astype(BF16)
    ones = jnp.ones((LANES, LANES), BF16)
    rr = lax.broadcasted_iota(I32, (nr, nr), 0)
    cc = lax.broadcasted_iota(I32, (nr, nr), 1)
    before = jnp.where(cc < rr, 1.0, 0.0).astype(BF16)
    blk_start = (lax.broadcasted_iota(I32, (1, nb), 1) * EXPERT_BLOCK).astype(F32)

    def per_expert(e, carry):
        pstart, dest, blk = carry
        hit = e_mat == e
        oh = jnp.where(hit, 1.0, 0.0).astype(BF16)
        in_row = _nn(oh, incl)
        row_tot = _nn(oh, ones)
        row_off = _nn(before, row_tot.astype(BF16))
        total = row_off[nr - 1:nr, :] + row_tot[nr - 1:nr, :]
        dest = jnp.where(hit, pstart + row_off + in_row - 1.0, dest)
        pend = pstart + jnp.floor((total + (EXPERT_BLOCK - 1)) * (1.0 / EXPERT_BLOCK)) * EXPERT_BLOCK
        blk = blk + jnp.where(blk_start >= pend[:, :1], 1.0, 0.0)
        return pend, dest, blk

    pstart0 = jnp.zeros((1, LANES), F32)
    pend, dest, blk = lax.fori_loop(
        0, N_EXPERTS, per_expert,
        (pstart0, jnp.zeros(e_mat.shape, F32), jnp.zeros((1, nb), F32)), unroll=4)
    dest_ref[...] = dest.astype(I32)
    blk_ref[...] = jnp.minimum(blk, N_EXPERTS - 1).astype(I32)
    nused_ref[...] = (pend * (1.0 / EXPERT_BLOCK)).astype(I32)


def _positions(e_mat, n_blk_pad):
    nr = e_mat.shape[0]
    return pl.pallas_call(
        _positions_kernel,
        out_shape=[jax.ShapeDtypeStruct((nr, LANES), I32),
                   jax.ShapeDtypeStruct((1, n_blk_pad), I32),
                   jax.ShapeDtypeStruct((1, LANES), I32)],
        name="positions",
    )(e_mat)


def _dispatch_kernel(dest_ref, h2_ref, xs_in_ref, xs_ref, hbuf, in_sem, out_sem, *, n_tok):
    del xs_in_ref
    i = pl.program_id(0)
    last = pl.num_programs(0) - 1
    slot = i % 2

    def load(step, buf):
        return pltpu.make_async_copy(h2_ref.at[pl.ds(step * GATHER_TILE, GATHER_TILE)], hbuf.at[buf],
                                     in_sem.at[buf])

    def drain(buf):
        for k in range(2):
            pltpu.make_async_copy(hbuf.at[buf], xs_ref.at[pl.ds(0, GATHER_TILE)], out_sem.at[buf, k]).wait()

    @pl.when(i == 0)
    def _():
        load(0, 0).start()

    @pl.when(i > 0)
    def _():
        drain(1 - slot)

    @pl.when(i < last)
    def _():
        load(i + 1, 1 - slot).start()

    load(i, slot).wait()
    base = i * GATHER_TILE

    def issue(r, carry):
        for k in range(2):
            pltpu.make_async_copy(hbuf.at[slot, r], xs_ref.at[dest_ref[k * n_tok + base + r]],
                                  out_sem.at[slot, k]).start(priority=k)
        return carry

    lax.fori_loop(0, GATHER_TILE, issue, 0, unroll=ISSUE_UNROLL)

    @pl.when(i == last)
    def _():
        drain(slot)


def _dispatch(dest, h2, xs_zero):
    n_tok = h2.shape[0]
    return pl.pallas_call(
        functools.partial(_dispatch_kernel, n_tok=n_tok),
        grid_spec=pltpu.PrefetchScalarGridSpec(
            num_scalar_prefetch=1,
            grid=(n_tok // GATHER_TILE,),
            in_specs=[pl.BlockSpec(memory_space=pl.ANY), pl.BlockSpec(memory_space=pl.ANY)],
            out_specs=pl.BlockSpec(memory_space=pl.ANY),
            scratch_shapes=[pltpu.VMEM((2, GATHER_TILE) + h2.shape[1:], F32),
                            pltpu.SemaphoreType.DMA((2,)), pltpu.SemaphoreType.DMA((2, 2))],
        ),
        out_shape=jax.ShapeDtypeStruct(xs_zero.shape, xs_zero.dtype),
        input_output_aliases={2: 0},
        compiler_params=pltpu.CompilerParams(dimension_semantics=("arbitrary",)),
        name="dispatch",
    )(dest, h2, xs_zero)


def _expert_kernel(blk_ref, nused_ref, xs_ref, w1_ref, w3_ref, w2_ref, ys_ref,
                   w1b, w3b, w2b, w1s, w3s, w2s, sem):
    i = pl.program_id(0)
    n_used = nused_ref[0]
    e = blk_ref[i]
    fresh = (i == 0) | (blk_ref[jnp.maximum(i - 1, 0)] != e)

    def fetch(expert):
        return (pltpu.make_async_copy(w1_ref.at[expert], w1s, sem.at[0]),
                pltpu.make_async_copy(w3_ref.at[expert], w3s, sem.at[1]),
                pltpu.make_async_copy(w2_ref.at[expert], w2s, sem.at[2]))

    @pl.when(i < n_used)
    def _():
        @pl.when(i == 0)
        def _():
            for cp in fetch(e):
                cp.start()

        @pl.when(fresh)
        def _():
            for cp in fetch(e):
                cp.wait()
            w1b[...] = w1s[...]
            w3b[...] = w3s[...]
            w2b[...] = w2s[...]
            nxt = lax.while_loop(lambda j: (j < n_used) & (blk_ref[jnp.minimum(j, n_used - 1)] == e),
                                 lambda j: j + 1, i + 1)

            @pl.when(nxt < n_used)
            def _():
                for cp in fetch(blk_ref[jnp.minimum(nxt, n_used - 1)]):
                    cp.start(priority=1)

        rows = xs_ref.shape[0]
        x = xs_ref[...].reshape(rows, w1b.shape[0]).astype(BF16)
        a = _nn(x, w1b[...])
        y = _nn((_silu(a) * _nn(x, w3b[...])).astype(BF16), w2b[...])
        ys_ref[...] = y.reshape(ys_ref.shape)

    @pl.when(i >= nused_ref[0])
    def _():
        ys_ref[...] = jnp.zeros(ys_ref.shape, F32)


def _experts(blk_e, n_used, xs, w1, w3, w2):
    p = xs.shape[0]
    tile = xs.shape[1:]
    d, de = w1.shape[1], w1.shape[2]
    rows = lambda i, blk, nu: (jnp.minimum(i, nu[0] - 1), 0, 0)
    hbm = pl.BlockSpec(memory_space=pl.ANY)
    return pl.pallas_call(
        _expert_kernel,
        grid_spec=pltpu.PrefetchScalarGridSpec(
            num_scalar_prefetch=2,
            grid=(p // EXPERT_BLOCK,),
            in_specs=[pl.BlockSpec((EXPERT_BLOCK,) + tile, rows), hbm, hbm, hbm],
            out_specs=pl.BlockSpec((EXPERT_BLOCK,) + tile, lambda i, blk, nu: (i, 0, 0)),
            scratch_shapes=[pltpu.VMEM((d, de), BF16), pltpu.VMEM((d, de), BF16),
                            pltpu.VMEM((de, d), BF16),
                            pltpu.VMEM((d, de), BF16), pltpu.VMEM((d, de), BF16),
                            pltpu.VMEM((de, d), BF16), pltpu.SemaphoreType.DMA((3,))],
        ),
        out_shape=jax.ShapeDtypeStruct(xs.shape, F32),
        compiler_params=pltpu.CompilerParams(vmem_limit_bytes=VMEM_LIMIT,
                                             dimension_semantics=("arbitrary",)),
        name="experts",
    )(blk_e, n_used, xs, w1, w3, w2)


def _combine_kernel(dest_ref, ys_ref, x1_ref, gt2_ref, wts_ref, g_ref, o_ref, ybuf, sem, *, n_tok, last):
    i = pl.program_id(0)
    slot = i % 2

    def gather(step, buf):
        base = step * GATHER_TILE

        def issue(r, carry):
            for k in range(2):
                pltpu.make_async_copy(ys_ref.at[dest_ref[k * n_tok + base + r]], ybuf.at[buf, k, r],
                                      sem.at[buf, k]).start(priority=k)
            return carry

        lax.fori_loop(0, GATHER_TILE, issue, 0, unroll=ISSUE_UNROLL)

    @pl.when(i == 0)
    def _():
        gather(0, 0)

    @pl.when(i + 1 < pl.num_programs(0))
    def _():
        gather(i + 1, 1 - slot)

    for k in range(2):
        pltpu.make_async_copy(ys_ref.at[pl.ds(0, GATHER_TILE)], ybuf.at[slot, k], sem.at[slot, k]).wait()

    wts = wts_ref[...]
    ya = ybuf[slot, 0].reshape(x1_ref.shape)
    yb = ybuf[slot, 1].reshape(x1_ref.shape)
    x = x1_ref[...] + gt2_ref[0] * (wts[:, 0:1] * ya + wts[:, 1:2] * yb)
    if last:
        x = x * lax.rsqrt(jnp.mean(x * x, axis=-1, keepdims=True) + EPS) * g_ref[...]
    o_ref[...] = x


def _combine(dest, ys, x1, mod, wts, final_g, seq, last=True):
    n_tok, d = x1.shape
    tt = GATHER_TILE
    per_batch = seq // tt
    return pl.pallas_call(
        functools.partial(_combine_kernel, n_tok=n_tok, last=last),
        grid_spec=pltpu.PrefetchScalarGridSpec(
            num_scalar_prefetch=1,
            grid=(n_tok // tt,),
            in_specs=[pl.BlockSpec(memory_space=pl.ANY),
                      pl.BlockSpec((tt, d), lambda i, dest: (i, 0)),
                      pl.BlockSpec((1, 1, d), lambda i, dest: (i // per_batch, 0, 5)),
                      pl.BlockSpec((tt, 2), lambda i, dest: (i, 0)),
                      pl.BlockSpec((1, d), lambda i, dest: (0, 0))],
            out_specs=pl.BlockSpec((tt, d), lambda i, dest: (i, 0)),
            scratch_shapes=[pltpu.VMEM((2, 2, tt) + ys.shape[1:], F32),
                            pltpu.SemaphoreType.DMA((2, 2))],
        ),
        out_shape=jax.ShapeDtypeStruct((n_tok, d), F32),
        compiler_params=pltpu.CompilerParams(dimension_semantics=("arbitrary",)),
        name="combine",
    )(dest, ys, x1, mod, wts, final_g)


def kernel(x, c, w_ada, b_ada, norm1_g, w_in, lambda_q1, lambda_k1, lambda_q2, lambda_k2, da_subln_g, hg_lb_logits, hg_norm_g, w_up_da, w_up_hg, w_out, norm2_g, w_rg, b_rg, w_re, b_re, w1, w3, w2, final_g):
    b, s, d = x.shape
    n_tok = b * s
    depth = w_ada.shape[0]
    assert s % ROW_TILE == 0 and s % ATT_BLOCK == 0 and s % GATHER_TILE == 0
    assert (2 * n_tok) % LANES == 0 and d == SUBLANES * LANES

    c_pad = jnp.pad(c, ((0, (-b) % 8), (0, 0)))
    n_slots = 2 * n_tok + N_EXPERTS * EXPERT_BLOCK
    n_blk = n_slots // EXPERT_BLOCK
    n_blk_pad = -(-n_blk // LANES) * LANES

    for l in range(depth):
        lam_init = 0.8 - 0.6 * math.exp(-0.3 * l)
        mod = _adaln(c_pad, w_ada[l], b_ada[l][None, :])[:b].reshape(b, 1, 6 * d)

        (qda, kda, vda, qd, kd, ke, hi, sg, dec, gda, ghg, w1_bf, w3_bf, w2_bf) = _inproj(
            x, mod, norm1_g[l][None, :], w_in[l].astype(BF16), hg_lb_logits, l, w1[l], w3[l], w2[l])
        yda, xs_zero = _attention(qda, kda, vda, lambda_q1[l][None, :], lambda_k1[l][None, :],
                                  lambda_q2[l][None, :], lambda_k2[l][None, :], da_subln_g[l][None, :],
                                  lam_init, (n_slots, SUBLANES, d // SUBLANES))
        yhg = _hgrn(qd, kd, ke, hi, sg, dec, hg_norm_g[l][None, :])

        wr = jnp.concatenate([w_re[l].T, w_rg[l].T,
                              jnp.zeros((ROUTER_ROWS - N_EXPERTS - N_GROUPS, d), F32)], axis=0)
        wr_hi = wr.astype(BF16)
        wr_lo = (wr - wr_hi.astype(F32)).astype(BF16)
        br = jnp.concatenate([b_re[l], b_rg[l],
                              jnp.zeros((ROUTER_ROWS - N_EXPERTS - N_GROUPS,), F32)])[:, None]
        x1, h2, eidx, wts = _merge(yda, yhg, gda, ghg, x, mod, norm2_g[l][None, :],
                                   w_up_da[l].astype(BF16), w_up_hg[l].astype(BF16),
                                   w_out[l].astype(BF16), wr_hi, wr_lo, br)

        e_mat = jnp.transpose(eidx, (1, 0, 2)).reshape(2 * n_tok // LANES, LANES)
        dest, blk_e, n_used = _positions(e_mat, n_blk_pad)
        dest = dest.reshape(2 * n_tok)
        h2 = h2.reshape((n_tok,) + h2.shape[2:])
        xs = _dispatch(dest, h2, xs_zero)
        ys = _experts(blk_e[0, :n_blk], n_used[0, :1], xs, w1_bf, w3_bf, w2_bf)
        wts_tok = jnp.transpose(wts, (0, 2, 1)).reshape(n_tok, 2)
        x = _combine(dest, ys, x1.reshape(n_tok, d), mod, wts_tok, final_g[None, :], s,
                     last=(l == depth - 1)).reshape(b, s, d)
    return x
```

```python
import functools
import math

import jax
import jax.numpy as jnp
from jax import lax
from jax.experimental import pallas as pl
from jax.experimental.pallas import tpu as pltpu

F32 = jnp.float32
BF16 = jnp.bfloat16
I32 = jnp.int32

EPS = 1e-6
NEG = -1e30
LOG2E = math.log2(math.e)

CHUNK = 64
DA_HEADS = 4
DA_HEAD_DIM = 64
HG_HEADS = 4
HG_DK = 128
HG_DV = 128
N_GROUPS = 4
EXPERTS_PER_GROUP = 8
N_EXPERTS = N_GROUPS * EXPERTS_PER_GROUP

LANES = 128
SUBLANES = 8
ADALN_COLS = 1024
ROW_TILE = 512
ATT_BLOCK = 512
BIAS_TERMS = 3
EXPERT_BLOCK = 256
GATHER_TILE = 512
ISSUE_UNROLL = 8
ROUTER_ROWS = 48
VMEM_LIMIT = 52 * 1024 * 1024


def _nt(a, b):
    return lax.dot_general(a, b, (((1,), (1,)), ((), ())), preferred_element_type=F32)


def _nn(a, b):
    return jnp.dot(a, b, preferred_element_type=F32)


def _split(a):
    hi = a.astype(BF16)
    lo = (a - hi.astype(F32)).astype(BF16)
    return hi, lo


def _sigmoid(x):
    return 1.0 / (1.0 + jnp.exp(-x))


def _silu(x):
    return x * _sigmoid(x)


def _adaln_kernel(c_ref, w_ref, b_ref, o_ref):
    c_hi, c_lo = _split(_silu(c_ref[...]))
    w_hi, w_lo = _split(w_ref[...])
    o_ref[...] = _nn(c_hi, w_hi) + _nn(c_lo, w_hi) + _nn(c_hi, w_lo) + b_ref[...]


def _adaln(c_pad, w, b):
    rows, d = c_pad.shape
    n = w.shape[1]
    tn = ADALN_COLS
    return pl.pallas_call(
        _adaln_kernel,
        grid=(n // tn,),
        in_specs=[
            pl.BlockSpec((rows, d), lambda j: (0, 0)),
            pl.BlockSpec((d, tn), lambda j: (0, j)),
            pl.BlockSpec((1, tn), lambda j: (0, j)),
        ],
        out_specs=pl.BlockSpec((rows, tn), lambda j: (0, j)),
        out_shape=jax.ShapeDtypeStruct((rows, n), F32),
        name="adaln",
    )(c_pad, w, b)


def _inproj_kernel(x_ref, sh_ref, sc_ref, g_ref, w_ref, lb_ref, w1_ref, w3_ref, w2_ref,
                   qda_ref, kda_ref, vda_ref, qd_ref, kd_ref, ke_ref, hi_ref, sg_ref,
                   dec_ref, gda_ref, ghg_ref, w1b_ref, w3b_ref, w2b_ref, *, layer):
    w1b_ref[...] = w1_ref[...].astype(BF16)
    w3b_ref[...] = w3_ref[...].astype(BF16)
    w2b_ref[...] = w2_ref[...].astype(BF16)

    x = x_ref[0]
    tm = x.shape[0]
    h = x * lax.rsqrt(jnp.mean(x * x, axis=-1, keepdims=True) + EPS) * g_ref[...]
    h = h * (1.0 + sc_ref[0]) + sh_ref[0]
    hb = h.astype(BF16)

    def proj(c0, width):
        return _nn(hb, w_ref[:, c0:c0 + width])

    w_da = DA_HEADS * 2 * DA_HEAD_DIM
    w_hg = HG_HEADS * HG_DK
    qda_ref[0] = (proj(0, w_da) * (LOG2E / math.sqrt(DA_HEAD_DIM))).astype(BF16)
    kda_ref[0] = proj(w_da, w_da).astype(BF16)
    vda_ref[0] = proj(2 * w_da, w_da).astype(BF16)
    c0 = 3 * w_da

    lbl = lb_ref[...]
    lbe = jnp.exp(lbl - jnp.max(lbl, axis=0, keepdims=True))
    lbs = lbe / jnp.sum(lbe, axis=0, keepdims=True)
    lb = jnp.sum(lbs[:layer + 1], axis=0, keepdims=True)

    f = lb + (1.0 - lb) * _sigmoid(proj(c0 + w_hg, w_hg))
    logf = jnp.log(f)
    kk = 1.0 - f
    pos = lax.broadcasted_iota(I32, logf.shape, 0) & (CHUNK - 1)
    a = logf
    step = 1
    while step < CHUNK:
        a = a + jnp.where(pos >= step, pltpu.roll(a, step, axis=0), 0.0)
        step *= 2
    n_ch = tm // CHUNK
    a = a.reshape(n_ch, CHUNK, w_hg)
    a_last = a[:, CHUNK - 1:CHUNK, :]
    kk3 = kk.reshape(n_ch, CHUNK, w_hg)
    qq3 = _silu(proj(c0, w_hg)).reshape(n_ch, CHUNK, w_hg)
    qd_ref[0] = (qq3 * jnp.exp(a)).reshape(tm, w_hg).astype(BF16)
    kd_ref[0] = (kk3 * jnp.exp(-a)).reshape(tm, w_hg).astype(BF16)
    ke_ref[0] = (kk3 * jnp.exp(a_last - a)).reshape(tm, w_hg).astype(BF16)
    dec_ref[0] = jnp.exp(a_last).reshape(n_ch, w_hg)
    hi_ref[0] = proj(c0 + 2 * w_hg, w_hg).astype(BF16)
    sg_ref[0] = _silu(proj(c0 + 3 * w_hg, w_hg)).astype(BF16)
    c1 = c0 + 4 * w_hg
    d = x.shape[1]
    gda_ref[0] = _sigmoid(proj(c1, d)).astype(BF16)
    ghg_ref[0] = _sigmoid(proj(c1 + d, d)).astype(BF16)


def _inproj(x, mod, g, w_bf, lb_logits, layer, w1, w3, w2):
    b, s, d = x.shape
    tm = ROW_TILE
    n_cols = w_bf.shape[1]
    per_b = s // tm
    assert w1.shape[0] % (b * per_b) == 0
    e_step = w1.shape[0] // (b * per_b)
    wspec = lambda a: pl.BlockSpec((e_step,) + a.shape[1:], lambda bi, i: (bi * per_b + i, 0, 0))
    w_da = DA_HEADS * 2 * DA_HEAD_DIM
    w_hg = HG_HEADS * HG_DK

    def tok(width, dtype):
        return (pl.BlockSpec((1, tm, width), lambda bi, i: (bi, i, 0)),
                jax.ShapeDtypeStruct((b, s, width), dtype))

    outs = [tok(w_da, BF16)] * 3 + [tok(w_hg, BF16)] * 5
    outs.append((pl.BlockSpec((1, tm // CHUNK, w_hg), lambda bi, i: (bi, i, 0)),
                 jax.ShapeDtypeStruct((b, s // CHUNK, w_hg), F32)))
    outs += [tok(d, BF16)] * 2
    outs += [(wspec(a), jax.ShapeDtypeStruct(a.shape, BF16)) for a in (w1, w3, w2)]
    return pl.pallas_call(
        functools.partial(_inproj_kernel, layer=layer),
        grid=(b, s // tm),
        in_specs=[
            pl.BlockSpec((1, tm, d), lambda bi, i: (bi, i, 0)),
            pl.BlockSpec((1, 1, d), lambda bi, i: (bi, 0, 0)),
            pl.BlockSpec((1, 1, d), lambda bi, i: (bi, 0, 1)),
            pl.BlockSpec((1, d), lambda bi, i: (0, 0)),
            pl.BlockSpec((d, n_cols), lambda bi, i: (0, 0), pipeline_mode=pl.Buffered(1)),
            pl.BlockSpec(lb_logits.shape, lambda bi, i: (0, 0)),
            wspec(w1), wspec(w3), wspec(w2),
        ],
        out_specs=[o[0] for o in outs],
        out_shape=[o[1] for o in outs],
        compiler_params=pltpu.CompilerParams(vmem_limit_bytes=VMEM_LIMIT),
        name="inproj",
    )(x, mod, mod, g, w_bf, lb_logits, w1, w3, w2)


def _attn_kernel(q_ref, k_ref, v_ref, lq1_ref, lk1_ref, lq2_ref, lk2_ref, g_ref, o_ref, zs_ref,
                 kaug_ref, vaug_ref, qaug_ref, m_ref, acc_ref, sa_ref, sb_ref, zero_ref, zsem,
                 *, lam_init, slopes):
    hd = pl.program_id(1)
    i = pl.program_id(2)

    step = (pl.program_id(0) * pl.num_programs(1) + hd) * pl.num_programs(2) + i
    n_steps = pl.num_programs(0) * pl.num_programs(1) * pl.num_programs(2)
    zrows = zero_ref.shape[0]

    def zero_copy(t):
        return pltpu.make_async_copy(zero_ref, zs_ref.at[pl.ds(t * zrows, zrows)], zsem)

    @pl.when(step == 0)
    def _():
        zero_ref[...] = jnp.zeros(zero_ref.shape, F32)

    @pl.when(step > 0)
    def _():
        zero_copy(step - 1).wait()

    zero_copy(step).start()

    @pl.when(step == n_steps - 1)
    def _():
        zero_copy(step).wait()
    tq = q_ref.shape[1]
    hw = q_ref.shape[2]
    n_key = k_ref.shape[1]

    slope = jnp.float32(slopes[-1])
    for idx in range(len(slopes) - 2, -1, -1):
        slope = jnp.where(hd == idx, jnp.float32(slopes[idx]), slope)

    @pl.when(i == 0)
    def _():
        kaug_ref[:hw, :] = k_ref[0].astype(F32).T.astype(BF16)
        jrel = (lax.broadcasted_iota(I32, (hw, n_key), 1) & (tq - 1)).astype(F32) * slope
        krow = lax.broadcasted_iota(I32, (hw, n_key), 0)
        bias_rows = jnp.zeros((hw, n_key), F32)
        rest = jrel
        for term in range(BIAS_TERMS):
            part = rest.astype(BF16).astype(F32)
            bias_rows = jnp.where(krow == term, part, bias_rows)
            rest = rest - part
        kaug_ref[hw:, :] = bias_rows.astype(BF16)
        vaug_ref[:, :hw] = v_ref[0]
        vaug_ref[:, hw:] = jnp.ones((n_key, hw), BF16)

    q = q_ref[0]
    lane = lax.broadcasted_iota(I32, q.shape, 1)
    zero = jnp.zeros_like(q)
    ones = jnp.where(lane < BIAS_TERMS, 1.0, 0.0).astype(BF16)
    qaug_ref[:tq, :hw] = jnp.where(lane < DA_HEAD_DIM, q, zero)
    qaug_ref[tq:, :hw] = jnp.where(lane >= DA_HEAD_DIM, q, zero)
    qaug_ref[:tq, hw:] = ones
    qaug_ref[tq:, hw:] = ones

    m_ref[...] = jnp.full(m_ref.shape, NEG, F32)
    acc_ref[...] = jnp.zeros(acc_ref.shape, F32)

    def scores(j, buf):
        buf[...] = _nn(qaug_ref[...], kaug_ref[:, pl.ds(pl.multiple_of(j * tq, tq), tq)])

    def consume(j, buf, diagonal):
        s = buf[...]
        if diagonal:
            ii = lax.broadcasted_iota(I32, s.shape, 0) & (tq - 1)
            jj = lax.broadcasted_iota(I32, s.shape, 1)
            ahead = jnp.minimum(ii - jj, 0).astype(F32)
            s = s + jnp.where((jj // CHUNK) <= (ii // CHUNK), (2.0 * slope) * ahead, NEG)
            shift = jnp.float32(0.0)
        else:
            shift = -slope * ((i - j) * tq).astype(F32)
        m_old = m_ref[...]
        m_new = jnp.maximum(m_old, jnp.max(s, axis=-1, keepdims=True) + shift)
        p = jnp.exp2(s - jnp.concatenate([m_new - shift] * (tq // LANES), axis=1))
        alpha = jnp.concatenate([jnp.exp2(m_old - m_new)] * (acc_ref.shape[1] // LANES), axis=1)
        vb = vaug_ref[pl.ds(pl.multiple_of(j * tq, tq), tq), :]
        acc_ref[...] = alpha * acc_ref[...] + _nn(p.astype(BF16), vb)
        m_ref[...] = m_new

    scores(0, sa_ref)

    def pair(pi, carry):
        j = 2 * pi
        scores(j + 1, sb_ref)
        consume(j, sa_ref, False)
        scores(j + 2, sa_ref)
        consume(j + 1, sb_ref, False)
        return carry

    lax.fori_loop(0, i // 2, pair, 0)

    @pl.when(i % 2 == 0)
    def _():
        consume(i, sa_ref, True)

    @pl.when(i % 2 == 1)
    def _():
        scores(i, sb_ref)
        consume(i - 1, sa_ref, False)
        consume(i, sb_ref, True)

    acc = acc_ref[...]
    o1 = acc[:tq, :hw] / acc[:tq, hw:]
    o2 = acc[tq:, :hw] / acc[tq:, hw:]
    lam = (jnp.exp(jnp.sum(lq1_ref[...] * lk1_ref[...], axis=-1, keepdims=True))
           - jnp.exp(jnp.sum(lq2_ref[...] * lk2_ref[...], axis=-1, keepdims=True)) + lam_init)
    o = o1 - lam * o2
    o = o * lax.rsqrt(jnp.mean(o * o, axis=-1, keepdims=True) + EPS) * g_ref[...]
    o_ref[0] = (o * (1.0 - lam_init)).astype(BF16)


def _attention(q, k, v, lq1, lk1, lq2, lk2, subln_g, lam_init, zero_shape):
    b, s, _ = q.shape
    tq = ATT_BLOCK
    hw = 2 * DA_HEAD_DIM
    assert hw == LANES
    n_steps = b * DA_HEADS * (s // tq)
    assert zero_shape[0] % n_steps == 0
    zrows = zero_shape[0] // n_steps
    slopes = tuple(LOG2E * 2.0 ** (-8.0 * (h + 1) / DA_HEADS) for h in range(DA_HEADS))
    vec = lambda n: pl.BlockSpec((1, n), lambda bi, h, i: (0, 0))
    return pl.pallas_call(
        functools.partial(_attn_kernel, lam_init=lam_init, slopes=slopes),
        grid=(b, DA_HEADS, s // tq),
        in_specs=[
            pl.BlockSpec((1, tq, hw), lambda bi, h, i: (bi, i, h)),
            pl.BlockSpec((1, s, hw), lambda bi, h, i: (bi, 0, h)),
            pl.BlockSpec((1, s, hw), lambda bi, h, i: (bi, 0, h)),
            vec(DA_HEAD_DIM), vec(DA_HEAD_DIM), vec(DA_HEAD_DIM), vec(DA_HEAD_DIM), vec(hw),
        ],
        out_specs=[pl.BlockSpec((1, tq, hw), lambda bi, h, i: (bi, i, h)),
                   pl.BlockSpec(memory_space=pl.ANY)],
        out_shape=[jax.ShapeDtypeStruct((b, s, DA_HEADS * hw), BF16),
                   jax.ShapeDtypeStruct(zero_shape, F32)],
        scratch_shapes=[
            pltpu.VMEM((2 * hw, s), BF16),
            pltpu.VMEM((s, 2 * hw), BF16),
            pltpu.VMEM((2 * tq, 2 * hw), BF16),
            pltpu.VMEM((2 * tq, LANES), F32),
            pltpu.VMEM((2 * tq, 2 * hw), F32),
            pltpu.VMEM((2 * tq, tq), F32),
            pltpu.VMEM((2 * tq, tq), F32),
            pltpu.VMEM((zrows,) + tuple(zero_shape[1:]), F32),
            pltpu.SemaphoreType.DMA(()),
        ],
        compiler_params=pltpu.CompilerParams(vmem_limit_bytes=VMEM_LIMIT,
                                             dimension_semantics=("arbitrary",) * 3),
        name="attention",
    )(q, k, v, lq1, lk1, lq2, lk2, subln_g)


def _hgrn_kernel(qd_ref, kd_ref, ke_ref, v_ref, sg_ref, dec_ref, g_ref, o_ref, st_ref):
    @pl.when(pl.program_id(1) == 0)
    def _():
        st_ref[...] = jnp.zeros(st_ref.shape, F32)

    n_ch = qd_ref.shape[1] // CHUNK
    r = lax.broadcasted_iota(I32, (CHUNK, CHUNK), 0)
    c = lax.broadcasted_iota(I32, (CHUNK, CHUNK), 1)
    causal = r >= c

    heads = [slice(h * HG_DK, (h + 1) * HG_DK) for h in range(HG_HEADS)]
    chunks = [slice(ci * CHUNK, (ci + 1) * CHUNK) for ci in range(n_ch)]
    incr = [[_nn(v_ref[0, rows, cols].astype(F32).T.astype(BF16), ke_ref[0, rows, cols])
             for rows in chunks] for cols in heads]
    start = []
    for h, cols in enumerate(heads):
        st = st_ref[h]
        before = []
        for ci in range(n_ch):
            before.append(st.astype(BF16))
            st = st * dec_ref[0, ci:ci + 1, cols] + incr[h][ci]
        st_ref[h] = st
        start.append(before)
    for h, cols in enumerate(heads):
        for ci, rows in enumerate(chunks):
            qd = qd_ref[0, rows, cols]
            scores = jnp.where(causal, _nt(qd, kd_ref[0, rows, cols]), 0.0).astype(BF16)
            o = _nn(scores, v_ref[0, rows, cols]) + _nt(qd, start[h][ci])
            o = o * lax.rsqrt(jnp.mean(o * o, axis=-1, keepdims=True) + EPS) * g_ref[...]
            o_ref[0, rows, cols] = (o * sg_ref[0, rows, cols].astype(F32)).astype(BF16)


def _hgrn(qd, kd, ke, v, sg, dec, norm_g):
    b, s, w = qd.shape
    ts = ROW_TILE
    tok = pl.BlockSpec((1, ts, w), lambda bi, i: (bi, i, 0))
    return pl.pallas_call(
        _hgrn_kernel,
        grid=(b, s // ts),
        in_specs=[tok, tok, tok, tok, tok,
                  pl.BlockSpec((1, ts // CHUNK, w), lambda bi, i: (bi, i, 0)),
                  pl.BlockSpec((1, HG_DV), lambda bi, i: (0, 0))],
        out_specs=tok,
        out_shape=jax.ShapeDtypeStruct((b, s, w), BF16),
        scratch_shapes=[pltpu.VMEM((HG_HEADS, HG_DV, HG_DK), F32)],
        name="hgrn",
    )(qd, kd, ke, v, sg, dec, norm_g)


def _merge_kernel(yda_ref, yhg_ref, gda_ref, ghg_ref, x_ref, gt1_ref, sh2_ref, sc2_ref, g2_ref,
                  wda_ref, whg_ref, wout_ref, wrh_ref, wrl_ref, br_ref,
                  x1_ref, h2_ref, eidx_ref, wts_ref):
    u = (gda_ref[0].astype(F32) * _nn(yda_ref[0], wda_ref[...])
         + ghg_ref[0].astype(F32) * _nn(yhg_ref[0], whg_ref[...]))
    x1 = x_ref[0] + gt1_ref[0] * _nn(u.astype(BF16), wout_ref[...])
    x1_ref[0] = x1
    h2 = x1 * lax.rsqrt(jnp.mean(x1 * x1, axis=-1, keepdims=True) + EPS) * g2_ref[...]
    h2 = h2 * (1.0 + sc2_ref[0]) + sh2_ref[0]
    h2_ref[0] = h2.reshape(h2.shape[0], SUBLANES, h2.shape[1] // SUBLANES)

    h_hi, h_lo = _split(h2)
    wrh = wrh_ref[...]
    both = _nt(jnp.concatenate([wrh, wrl_ref[...]], axis=0), h_hi)
    lg = both[:ROUTER_ROWS] + both[ROUTER_ROWS:] + _nt(wrh, h_lo) + br_ref[...]

    tm = h2.shape[0]
    eg = EXPERTS_PER_GROUP
    gl = lg[N_EXPERTS:N_EXPERTS + N_GROUPS]
    gmax = jnp.max(gl, axis=0, keepdims=True)
    g_p = 1.0 / jnp.sum(jnp.exp(gl - gmax), axis=0, keepdims=True)
    gi = lax.broadcasted_iota(I32, (N_GROUPS, tm), 0)
    g_idx = jnp.min(jnp.where(gl == gmax, gi, N_GROUPS), axis=0, keepdims=True)
    sel = lg[(N_GROUPS - 1) * eg:N_GROUPS * eg]
    for g in range(N_GROUPS - 2, -1, -1):
        sel = jnp.where(g_idx == g, lg[g * eg:(g + 1) * eg], sel)
    ei = lax.broadcasted_iota(I32, (eg, tm), 0)
    m1 = jnp.max(sel, axis=0, keepdims=True)
    i1 = jnp.min(jnp.where(sel == m1, ei, eg), axis=0, keepdims=True)
    sel2 = jnp.where(ei == i1, -jnp.inf, sel)
    m2 = jnp.max(sel2, axis=0, keepdims=True)
    i2 = jnp.min(jnp.where(sel2 == m2, ei, eg), axis=0, keepdims=True)
    t = jnp.exp(m2 - m1)
    wa = 1.0 / (1.0 + t)
    eidx_ref[0] = jnp.concatenate([g_idx * eg + i1, g_idx * eg + i2], axis=0)
    wts_ref[0] = jnp.concatenate([g_p * wa, g_p * (t * wa)], axis=0)


def _merge(yda, yhg, gda, ghg, x, mod, g2, wda, whg, wout, wrh, wrl, br):
    b, s, d = x.shape
    tm = ROW_TILE
    w = yda.shape[2]
    tok = lambda width: pl.BlockSpec((1, tm, width), lambda bi, i: (bi, i, 0))
    modspec = lambda k: pl.BlockSpec((1, 1, d), lambda bi, i: (bi, 0, k))
    full = lambda a: pl.BlockSpec(a.shape, lambda bi, i: (0,) * a.ndim)
    lane_rows = pl.BlockSpec((1, 2, tm), lambda bi, i: (bi, 0, i))
    return pl.pallas_call(
        _merge_kernel,
        grid=(b, s // tm),
        in_specs=[tok(w), tok(w), tok(d), tok(d), tok(d),
                  modspec(2), modspec(3), modspec(4), full(g2),
                  full(wda), full(whg), full(wout), full(wrh), full(wrl), full(br)],
        out_specs=[tok(d), pl.BlockSpec((1, tm, SUBLANES, d // SUBLANES), lambda bi, i: (bi, i, 0, 0)),
                   lane_rows, lane_rows],
        out_shape=[jax.ShapeDtypeStruct((b, s, d), F32),
                   jax.ShapeDtypeStruct((b, s, SUBLANES, d // SUBLANES), F32),
                   jax.ShapeDtypeStruct((b, 2, s), I32), jax.ShapeDtypeStruct((b, 2, s), F32)],
        compiler_params=pltpu.CompilerParams(vmem_limit_bytes=VMEM_LIMIT),
        name="merge",
    )(yda, yhg, gda, ghg, x, mod, mod, mod, g2, wda, whg, wout, wrh, wrl, br)


def _positions_kernel(e_ref, dest_ref, blk_ref, nused_ref):
    e_mat = e_ref[...]
    nr = e_mat.shape[0]
    nb = blk_ref.shape[1]
    r = lax.broadcasted_iota(I32, (LANES, LANES), 0)
    c = lax.broadcasted_iota(I32, (LANES, LANES), 1)
    incl = jnp.where(r <= c, 1.0, 0.0).astype(BF16)
    ones = jnp.ones((LANES, LANES), BF16)
    rr = lax.broadcasted_iota(I32, (nr, nr), 0)
    cc = lax.broadcasted_iota(I32, (nr, nr), 1)
    before = jnp.where(cc < rr, 1.0, 0.0).astype(BF16)
    blk_start = (lax.broadcasted_iota(I32, (1, nb), 1) * EXPERT_BLOCK).astype(F32)

    def per_expert(e, carry):
        pstart, dest, blk = carry
        hit = e_mat == e
        oh = jnp.where(hit, 1.0, 0.0).astype(BF16)
        in_row = _nn(oh, incl)
        row_tot = _nn(oh, ones)
        row_off = _nn(before, row_tot.astype(BF16))
        total = row_off[nr - 1:nr, :] + row_tot[nr - 1:nr, :]
        dest = jnp.where(hit, pstart + row_off + in_row - 1.0, dest)
        pend = pstart + jnp.floor((total + (EXPERT_BLOCK - 1)) * (1.0 / EXPERT_BLOCK)) * EXPERT_BLOCK
        blk = blk + jnp.where(blk_start >= pend[:, :1], 1.0, 0.0)
        return pend, dest, blk

    pstart0 = jnp.zeros((1, LANES), F32)
    pend, dest, blk = lax.fori_loop(
        0, N_EXPERTS, per_expert,
        (pstart0, jnp.zeros(e_mat.shape, F32), jnp.zeros((1, nb), F32)), unroll=4)
    dest_ref[...] = dest.astype(I32)
    blk_ref[...] = jnp.minimum(blk, N_EXPERTS - 1).astype(I32)
    nused_ref[...] = (pend * (1.0 / EXPERT_BLOCK)).astype(I32)


def _positions(e_mat, n_blk_pad):
    nr = e_mat.shape[0]
    return pl.pallas_call(
        _positions_kernel,
        out_shape=[jax.ShapeDtypeStruct((nr, LANES), I32),
                   jax.ShapeDtypeStruct((1, n_blk_pad), I32),
                   jax.ShapeDtypeStruct((1, LANES), I32)],
        name="positions",
    )(e_mat)


def _dispatch_kernel(dest_ref, h2_ref, xs_in_ref, xs_ref, hbuf, in_sem, out_sem, *, n_tok):
    del xs_in_ref
    i = pl.program_id(0)
    last = pl.num_programs(0) - 1
    slot = i % 2

    def load(step, buf):
        return pltpu.make_async_copy(h2_ref.at[pl.ds(step * GATHER_TILE, GATHER_TILE)], hbuf.at[buf],
                                     in_sem.at[buf])

    def drain(buf):
        for k in range(2):
            pltpu.make_async_copy(hbuf.at[buf], xs_ref.at[pl.ds(0, GATHER_TILE)], out_sem.at[buf, k]).wait()

    @pl.when(i == 0)
    def _():
        load(0, 0).start()

    @pl.when(i > 0)
    def _():
        drain(1 - slot)

    @pl.when(i < last)
    def _():
        load(i + 1, 1 - slot).start()

    load(i, slot).wait()
    base = i * GATHER_TILE

    def issue(r, carry):
        for k in range(2):
            pltpu.make_async_copy(hbuf.at[slot, r], xs_ref.at[dest_ref[k * n_tok + base + r]],
                                  out_sem.at[slot, k]).start(priority=k)
        return carry

    lax.fori_loop(0, GATHER_TILE, issue, 0, unroll=ISSUE_UNROLL)

    @pl.when(i == last)
    def _():
        drain(slot)


def _dispatch(dest, h2, xs_zero):
    n_tok = h2.shape[0]
    return pl.pallas_call(
        functools.partial(_dispatch_kernel, n_tok=n_tok),
        grid_spec=pltpu.PrefetchScalarGridSpec(
            num_scalar_prefetch=1,
            grid=(n_tok // GATHER_TILE,),
            in_specs=[pl.BlockSpec(memory_space=pl.ANY), pl.BlockSpec(memory_space=pl.ANY)],
            out_specs=pl.BlockSpec(memory_space=pl.ANY),
            scratch_shapes=[pltpu.VMEM((2, GATHER_TILE) + h2.shape[1:], F32),
                            pltpu.SemaphoreType.DMA((2,)), pltpu.SemaphoreType.DMA((2, 2))],
        ),
        out_shape=jax.ShapeDtypeStruct(xs_zero.shape, xs_zero.dtype),
        input_output_aliases={2: 0},
        compiler_params=pltpu.CompilerParams(dimension_semantics=("arbitrary",)),
        name="dispatch",
    )(dest, h2, xs_zero)


def _expert_kernel(blk_ref, nused_ref, xs_ref, w1_ref, w3_ref, w2_ref, ys_ref,
                   w1b, w3b, w2b, cur, sem):
    i = pl.program_id(0)
    n_used = nused_ref[0]
    e = blk_ref[i]
    fresh = (i == 0) | (blk_ref[jnp.maximum(i - 1, 0)] != e)

    def fetch(expert, which):
        return (pltpu.make_async_copy(w1_ref.at[expert], w1b.at[which], sem.at[0]),
                pltpu.make_async_copy(w3_ref.at[expert], w3b.at[which], sem.at[1]),
                pltpu.make_async_copy(w2_ref.at[expert], w2b.at[which], sem.at[2]))

    @pl.when(i < n_used)
    def _():
        @pl.when(i == 0)
        def _():
            cur[0] = 0
            for cp in fetch(e, 0):
                cp.start()

        @pl.when(fresh & (i > 0))
        def _():
            cur[0] = 1 - cur[0]

        @pl.when(fresh)
        def _():
            for cp in fetch(e, cur[0]):
                cp.wait()
            nxt = lax.while_loop(lambda j: (j < n_used) & (blk_ref[jnp.minimum(j, n_used - 1)] == e),
                                 lambda j: j + 1, i + 1)

            @pl.when(nxt < n_used)
            def _():
                for cp in fetch(blk_ref[jnp.minimum(nxt, n_used - 1)], 1 - cur[0]):
                    cp.start(priority=1)

        which = cur[0]
        rows = xs_ref.shape[0]
        x = xs_ref[...].reshape(rows, w1b.shape[1]).astype(BF16)
        a = _nn(x, w1b[which])
        y = _nn((_silu(a) * _nn(x, w3b[which])).astype(BF16), w2b[which])
        ys_ref[...] = y.reshape(ys_ref.shape)

    @pl.when(i >= nused_ref[0])
    def _():
        ys_ref[...] = jnp.zeros(ys_ref.shape, F32)


def _experts(blk_e, n_used, xs, w1, w3, w2):
    p = xs.shape[0]
    tile = xs.shape[1:]
    d, de = w1.shape[1], w1.shape[2]
    rows = lambda i, blk, nu: (jnp.minimum(i, nu[0] - 1), 0, 0)
    hbm = pl.BlockSpec(memory_space=pl.ANY)
    return pl.pallas_call(
        _expert_kernel,
        grid_spec=pltpu.PrefetchScalarGridSpec(
            num_scalar_prefetch=2,
            grid=(p // EXPERT_BLOCK,),
            in_specs=[pl.BlockSpec((EXPERT_BLOCK,) + tile, rows), hbm, hbm, hbm],
            out_specs=pl.BlockSpec((EXPERT_BLOCK,) + tile, lambda i, blk, nu: (i, 0, 0)),
            scratch_shapes=[pltpu.VMEM((2, d, de), BF16), pltpu.VMEM((2, d, de), BF16),
                            pltpu.VMEM((2, de, d), BF16),
                            pltpu.SMEM((1,), I32), pltpu.SemaphoreType.DMA((3,))],
        ),
        out_shape=jax.ShapeDtypeStruct(xs.shape, F32),
        compiler_params=pltpu.CompilerParams(vmem_limit_bytes=VMEM_LIMIT,
                                             dimension_semantics=("arbitrary",)),
        name="experts",
    )(blk_e, n_used, xs, w1, w3, w2)


def _combine_kernel(dest_ref, ys_ref, x1_ref, gt2_ref, wts_ref, g_ref, o_ref, ybuf, sem, *, n_tok, last):
    i = pl.program_id(0)
    slot = i % 2

    def gather(step, buf):
        base = step * GATHER_TILE

        def issue(r, carry):
            for k in range(2):
                pltpu.make_async_copy(ys_ref.at[dest_ref[k * n_tok + base + r]], ybuf.at[buf, k, r],
                                      sem.at[buf, k]).start(priority=k)
            return carry

        lax.fori_loop(0, GATHER_TILE, issue, 0, unroll=ISSUE_UNROLL)

    @pl.when(i == 0)
    def _():
        gather(0, 0)

    @pl.when(i + 1 < pl.num_programs(0))
    def _():
        gather(i + 1, 1 - slot)

    for k in range(2):
        pltpu.make_async_copy(ys_ref.at[pl.ds(0, GATHER_TILE)], ybuf.at[slot, k], sem.at[slot, k]).wait()

    wts = wts_ref[...]
    ya = ybuf[slot, 0].reshape(x1_ref.shape)
    yb = ybuf[slot, 1].reshape(x1_ref.shape)
    x = x1_ref[...] + gt2_ref[0] * (wts[:, 0:1] * ya + wts[:, 1:2] * yb)
    if last:
        x = x * lax.rsqrt(jnp.mean(x * x, axis=-1, keepdims=True) + EPS) * g_ref[...]
    o_ref[...] = x


def _combine(dest, ys, x1, mod, wts, final_g, seq, last=True):
    n_tok, d = x1.shape
    tt = GATHER_TILE
    per_batch = seq // tt
    return pl.pallas_call(
        functools.partial(_combine_kernel, n_tok=n_tok, last=last),
        grid_spec=pltpu.PrefetchScalarGridSpec(
            num_scalar_prefetch=1,
            grid=(n_tok // tt,),
            in_specs=[pl.BlockSpec(memory_space=pl.ANY),
                      pl.BlockSpec((tt, d), lambda i, dest: (i, 0)),
                      pl.BlockSpec((1, 1, d), lambda i, dest: (i // per_batch, 0, 5)),
                      pl.BlockSpec((tt, 2), lambda i, dest: (i, 0)),
                      pl.BlockSpec((1, d), lambda i, dest: (0, 0))],
            out_specs=pl.BlockSpec((tt, d), lambda i, dest: (i, 0)),
            scratch_shapes=[pltpu.VMEM((2, 2, tt) + ys.shape[1:], F32),
                            pltpu.SemaphoreType.DMA((2, 2))],
        ),
        out_shape=jax.ShapeDtypeStruct((n_tok, d), F32),
        compiler_params=pltpu.CompilerParams(dimension_semantics=("arbitrary",)),
        name="combine",
    )(dest, ys, x1, mod, wts, final_g)


def kernel(x, c, w_ada, b_ada, norm1_g, w_in, lambda_q1, lambda_k1, lambda_q2, lambda_k2, da_subln_g, hg_lb_logits, hg_norm_g, w_up_da, w_up_hg, w_out, norm2_g, w_rg, b_rg, w_re, b_re, w1, w3, w2, final_g):
    b, s, d = x.shape
    n_tok = b * s
    depth = w_ada.shape[0]
    assert s % ROW_TILE == 0 and s % ATT_BLOCK == 0 and s % GATHER_TILE == 0
    assert (2 * n_tok) % LANES == 0 and d == SUBLANES * LANES

    c_pad = jnp.pad(c, ((0, (-b) % 8), (0, 0)))
    n_slots = 2 * n_tok + N_EXPERTS * EXPERT_BLOCK
    n_blk = n_slots // EXPERT_BLOCK
    n_blk_pad = -(-n_blk // LANES) * LANES

    for l in range(depth):
        lam_init = 0.8 - 0.6 * math.exp(-0.3 * l)
        mod = _adaln(c_pad, w_ada[l], b_ada[l][None, :])[:b].reshape(b, 1, 6 * d)

        (qda, kda, vda, qd, kd, ke, hi, sg, dec, gda, ghg, w1_bf, w3_bf, w2_bf) = _inproj(
            x, mod, norm1_g[l][None, :], w_in[l].astype(BF16), hg_lb_logits, l, w1[l], w3[l], w2[l])
        yda, xs_zero = _attention(qda, kda, vda, lambda_q1[l][None, :], lambda_k1[l][None, :],
                                  lambda_q2[l][None, :], lambda_k2[l][None, :], da_subln_g[l][None, :],
                                  lam_init, (n_slots, SUBLANES, d // SUBLANES))
        yhg = _hgrn(qd, kd, ke, hi, sg, dec, hg_norm_g[l][None, :])

        wr = jnp.concatenate([w_re[l].T, w_rg[l].T,
                              jnp.zeros((ROUTER_ROWS - N_EXPERTS - N_GROUPS, d), F32)], axis=0)
        wr_hi = wr.astype(BF16)
        wr_lo = (wr - wr_hi.astype(F32)).astype(BF16)
        br = jnp.concatenate([b_re[l], b_rg[l],
                              jnp.zeros((ROUTER_ROWS - N_EXPERTS - N_GROUPS,), F32)])[:, None]
        x1, h2, eidx, wts = _merge(yda, yhg, gda, ghg, x, mod, norm2_g[l][None, :],
                                   w_up_da[l].astype(BF16), w_up_hg[l].astype(BF16),
                                   w_out[l].astype(BF16), wr_hi, wr_lo, br)

        e_mat = jnp.transpose(eidx, (1, 0, 2)).reshape(2 * n_tok // LANES, LANES)
        dest, blk_e, n_used = _positions(e_mat, n_blk_pad)
        dest = dest.reshape(2 * n_tok)
        h2 = h2.reshape((n_tok,) + h2.shape[2:])
        xs = _dispatch(dest, h2, xs_zero)
        ys = _experts(blk_e[0, :n_blk], n_used[0, :1], xs, w1_bf, w3_bf, w2_bf)
        wts_tok = jnp.transpose(wts, (0, 2, 1)).reshape(n_tok, 2)
        x = _combine(dest, ys, x1.reshape(n_tok, d), mod, wts_tok, final_g[None, :], s,
                     last=(l == depth - 1)).reshape(b, s, d)
    return x
```

```python
import functools
import math

import jax
import jax.numpy as jnp
from jax import lax
from jax.experimental import pallas as pl
from jax.experimental.pallas import tpu as pltpu

F32 = jnp.float32
BF16 = jnp.bfloat16
I32 = jnp.int32
U32 = jnp.uint32

EPS = 1e-6
NEG = -1e30
LOG2E = math.log2(math.e)

CHUNK = 64
DA_HEADS = 4
DA_HEAD_DIM = 64
HG_HEADS = 4
HG_DK = 128
HG_DV = 128
N_GROUPS = 4
EXPERTS_PER_GROUP = 8
N_EXPERTS = N_GROUPS * EXPERTS_PER_GROUP

LANES = 128
SUBLANES = 8
ADALN_COLS = 1024
ROW_TILE = 512
ATT_BLOCK = 512
BIAS_TERMS = 3
EXPERT_BLOCK = 256
GATHER_TILE = 512
ISSUE_UNROLL = 8
ROUTER_ROWS = 48
VMEM_LIMIT = 52 * 1024 * 1024


def _nt(a, b):
    return lax.dot_general(a, b, (((1,), (1,)), ((), ())), preferred_element_type=F32)


def _nn(a, b):
    return jnp.dot(a, b, preferred_element_type=F32)


def _split(a):
    hi = a.astype(BF16)
    lo = (a - hi.astype(F32)).astype(BF16)
    return hi, lo


def _sigmoid(x):
    return 1.0 / (1.0 + jnp.exp(-x))


def _silu(x):
    return x * _sigmoid(x)


def _pack_rows(y):
    rows, cols = y.shape[0] // 2, y.shape[1] // 2
    bits = lax.bitcast_convert_type(y.astype(BF16).astype(F32), U32)
    words = bits[:, :cols] | lax.shift_right_logical(bits[:, cols:], jnp.uint32(16))
    return jnp.concatenate([words[:rows], words[rows:]], axis=1)


def _unpack_rows(packed):
    cols = packed.shape[1] // 2
    words = jnp.concatenate([packed[:, :cols], packed[:, cols:]], axis=0)
    first = lax.bitcast_convert_type(words & jnp.uint32(0xFFFF0000), F32)
    second = lax.bitcast_convert_type(lax.shift_left(words, jnp.uint32(16)), F32)
    return jnp.concatenate([first, second], axis=1)


def _adaln_kernel(c_ref, w_ref, b_ref, o_ref):
    c_hi, c_lo = _split(_silu(c_ref[...]))
    w_hi, w_lo = _split(w_ref[...])
    o_ref[...] = _nn(c_hi, w_hi) + _nn(c_lo, w_hi) + _nn(c_hi, w_lo) + b_ref[...]


def _adaln(c_pad, w, b):
    rows, d = c_pad.shape
    n = w.shape[1]
    tn = ADALN_COLS
    return pl.pallas_call(
        _adaln_kernel,
        grid=(n // tn,),
        in_specs=[
            pl.BlockSpec((rows, d), lambda j: (0, 0)),
            pl.BlockSpec((d, tn), lambda j: (0, j)),
            pl.BlockSpec((1, tn), lambda j: (0, j)),
        ],
        out_specs=pl.BlockSpec((rows, tn), lambda j: (0, j)),
        out_shape=jax.ShapeDtypeStruct((rows, n), F32),
        name="adaln",
    )(c_pad, w, b)


def _inproj_kernel(x_ref, sh_ref, sc_ref, g_ref, w_ref, lb_ref, w1_ref, w3_ref, w2_ref,
                   qda_ref, kda_ref, vda_ref, qd_ref, kd_ref, ke_ref, hi_ref, sg_ref,
                   dec_ref, gda_ref, ghg_ref, w1b_ref, w3b_ref, w2b_ref, *, layer):
    w1b_ref[...] = w1_ref[...].astype(BF16)
    w3b_ref[...] = w3_ref[...].astype(BF16)
    w2b_ref[...] = w2_ref[...].astype(BF16)

    x = x_ref[0]
    tm = x.shape[0]
    h = x * lax.rsqrt(jnp.mean(x * x, axis=-1, keepdims=True) + EPS) * g_ref[...]
    h = h * (1.0 + sc_ref[0]) + sh_ref[0]
    hb = h.astype(BF16)

    def proj(c0, width):
        return _nn(hb, w_ref[:, c0:c0 + width])

    w_da = DA_HEADS * 2 * DA_HEAD_DIM
    w_hg = HG_HEADS * HG_DK
    qda_ref[0] = (proj(0, w_da) * (LOG2E / math.sqrt(DA_HEAD_DIM))).astype(BF16)
    kda_ref[0] = proj(w_da, w_da).astype(BF16)
    vda_ref[0] = proj(2 * w_da, w_da).astype(BF16)
    c0 = 3 * w_da

    lbl = lb_ref[...]
    lbe = jnp.exp(lbl - jnp.max(lbl, axis=0, keepdims=True))
    lbs = lbe / jnp.sum(lbe, axis=0, keepdims=True)
    lb = jnp.sum(lbs[:layer + 1], axis=0, keepdims=True)

    f = lb + (1.0 - lb) * _sigmoid(proj(c0 + w_hg, w_hg))
    logf = jnp.log(f)
    kk = 1.0 - f
    pos = lax.broadcasted_iota(I32, logf.shape, 0) & (CHUNK - 1)
    a = logf
    step = 1
    while step < CHUNK:
        a = a + jnp.where(pos >= step, pltpu.roll(a, step, axis=0), 0.0)
        step *= 2
    n_ch = tm // CHUNK
    a = a.reshape(n_ch, CHUNK, w_hg)
    a_last = a[:, CHUNK - 1:CHUNK, :]
    kk3 = kk.reshape(n_ch, CHUNK, w_hg)
    qq3 = _silu(proj(c0, w_hg)).reshape(n_ch, CHUNK, w_hg)
    qd_ref[0] = (qq3 * jnp.exp(a)).reshape(tm, w_hg).astype(BF16)
    kd_ref[0] = (kk3 * jnp.exp(-a)).reshape(tm, w_hg).astype(BF16)
    ke_ref[0] = (kk3 * jnp.exp(a_last - a)).reshape(tm, w_hg).astype(BF16)
    dec_ref[0] = jnp.exp(a_last).reshape(n_ch, w_hg)
    hi_ref[0] = proj(c0 + 2 * w_hg, w_hg).astype(BF16)
    sg_ref[0] = _silu(proj(c0 + 3 * w_hg, w_hg)).astype(BF16)
    c1 = c0 + 4 * w_hg
    d = x.shape[1]
    gda_ref[0] = _sigmoid(proj(c1, d)).astype(BF16)
    ghg_ref[0] = _sigmoid(proj(c1 + d, d)).astype(BF16)


def _inproj(x, mod, g, w_bf, lb_logits, layer, w1, w3, w2):
    b, s, d = x.shape
    tm = ROW_TILE
    n_cols = w_bf.shape[1]
    per_b = s // tm
    assert w1.shape[0] % (b * per_b) == 0
    e_step = w1.shape[0] // (b * per_b)
    wspec = lambda a: pl.BlockSpec((e_step,) + a.shape[1:], lambda bi, i: (bi * per_b + i, 0, 0))
    w_da = DA_HEADS * 2 * DA_HEAD_DIM
    w_hg = HG_HEADS * HG_DK

    def tok(width, dtype):
        return (pl.BlockSpec((1, tm, width), lambda bi, i: (bi, i, 0)),
                jax.ShapeDtypeStruct((b, s, width), dtype))

    outs = [tok(w_da, BF16)] * 3 + [tok(w_hg, BF16)] * 5
    outs.append((pl.BlockSpec((1, tm // CHUNK, w_hg), lambda bi, i: (bi, i, 0)),
                 jax.ShapeDtypeStruct((b, s // CHUNK, w_hg), F32)))
    outs += [tok(d, BF16)] * 2
    outs += [(wspec(a), jax.ShapeDtypeStruct(a.shape, BF16)) for a in (w1, w3, w2)]
    return pl.pallas_call(
        functools.partial(_inproj_kernel, layer=layer),
        grid=(b, s // tm),
        in_specs=[
            pl.BlockSpec((1, tm, d), lambda bi, i: (bi, i, 0)),
            pl.BlockSpec((1, 1, d), lambda bi, i: (bi, 0, 0)),
            pl.BlockSpec((1, 1, d), lambda bi, i: (bi, 0, 1)),
            pl.BlockSpec((1, d), lambda bi, i: (0, 0)),
            pl.BlockSpec((d, n_cols), lambda bi, i: (0, 0), pipeline_mode=pl.Buffered(1)),
            pl.BlockSpec(lb_logits.shape, lambda bi, i: (0, 0)),
            wspec(w1), wspec(w3), wspec(w2),
        ],
        out_specs=[o[0] for o in outs],
        out_shape=[o[1] for o in outs],
        compiler_params=pltpu.CompilerParams(vmem_limit_bytes=VMEM_LIMIT),
        name="inproj",
    )(x, mod, mod, g, w_bf, lb_logits, w1, w3, w2)


def _attn_kernel(q_ref, k_ref, v_ref, lq1_ref, lk1_ref, lq2_ref, lk2_ref, g_ref, o_ref, zs_ref,
                 kaug_ref, vaug_ref, qaug_ref, m_ref, acc_ref, sa_ref, sb_ref, zero_ref, zsem,
                 *, lam_init, slopes):
    hd = pl.program_id(1)
    i = pl.program_id(2)

    step = (pl.program_id(0) * pl.num_programs(1) + hd) * pl.num_programs(2) + i
    n_steps = pl.num_programs(0) * pl.num_programs(1) * pl.num_programs(2)
    zrows = zero_ref.shape[0]

    def zero_copy(t):
        return pltpu.make_async_copy(zero_ref, zs_ref.at[pl.ds(t * zrows, zrows)], zsem)

    @pl.when(step == 0)
    def _():
        zero_ref[...] = jnp.zeros(zero_ref.shape, F32)

    @pl.when(step > 0)
    def _():
        zero_copy(step - 1).wait()

    zero_copy(step).start()

    @pl.when(step == n_steps - 1)
    def _():
        zero_copy(step).wait()
    tq = q_ref.shape[1]
    hw = q_ref.shape[2]
    n_key = k_ref.shape[1]

    slope = jnp.float32(slopes[-1])
    for idx in range(len(slopes) - 2, -1, -1):
        slope = jnp.where(hd == idx, jnp.float32(slopes[idx]), slope)

    @pl.when(i == 0)
    def _():
        kaug_ref[:hw, :] = k_ref[0].astype(F32).T.astype(BF16)
        jrel = (lax.broadcasted_iota(I32, (hw, n_key), 1) & (tq - 1)).astype(F32) * slope
        krow = lax.broadcasted_iota(I32, (hw, n_key), 0)
        bias_rows = jnp.zeros((hw, n_key), F32)
        rest = jrel
        for term in range(BIAS_TERMS):
            part = rest.astype(BF16).astype(F32)
            bias_rows = jnp.where(krow == term, part, bias_rows)
            rest = rest - part
        kaug_ref[hw:, :] = bias_rows.astype(BF16)
        vaug_ref[:, :hw] = v_ref[0]
        vaug_ref[:, hw:] = jnp.ones((n_key, hw), BF16)

    q = q_ref[0]
    lane = lax.broadcasted_iota(I32, q.shape, 1)
    zero = jnp.zeros_like(q)
    ones = jnp.where(lane < BIAS_TERMS, 1.0, 0.0).astype(BF16)
    qaug_ref[:tq, :hw] = jnp.where(lane < DA_HEAD_DIM, q, zero)
    qaug_ref[tq:, :hw] = jnp.where(lane >= DA_HEAD_DIM, q, zero)
    qaug_ref[:tq, hw:] = ones
    qaug_ref[tq:, hw:] = ones

    m_ref[...] = jnp.full(m_ref.shape, NEG, F32)
    acc_ref[...] = jnp.zeros(acc_ref.shape, F32)

    def scores(j, buf):
        buf[...] = _nn(qaug_ref[...], kaug_ref[:, pl.ds(pl.multiple_of(j * tq, tq), tq)])

    def consume(j, buf, diagonal):
        s = buf[...]
        if diagonal:
            ii = lax.broadcasted_iota(I32, s.shape, 0) & (tq - 1)
            jj = lax.broadcasted_iota(I32, s.shape, 1)
            ahead = jnp.minimum(ii - jj, 0).astype(F32)
            s = s + jnp.where((jj // CHUNK) <= (ii // CHUNK), (2.0 * slope) * ahead, NEG)
            shift = jnp.float32(0.0)
        else:
            shift = -slope * ((i - j) * tq).astype(F32)
        m_old = m_ref[...]
        m_new = jnp.maximum(m_old, jnp.max(s, axis=-1, keepdims=True) + shift)
        p = jnp.exp2(s - jnp.concatenate([m_new - shift] * (tq // LANES), axis=1))
        alpha = jnp.concatenate([jnp.exp2(m_old - m_new)] * (acc_ref.shape[1] // LANES), axis=1)
        vb = vaug_ref[pl.ds(pl.multiple_of(j * tq, tq), tq), :]
        acc_ref[...] = alpha * acc_ref[...] + _nn(p.astype(BF16), vb)
        m_ref[...] = m_new

    scores(0, sa_ref)

    def pair(pi, carry):
        j = 2 * pi
        scores(j + 1, sb_ref)
        consume(j, sa_ref, False)
        scores(j + 2, sa_ref)
        consume(j + 1, sb_ref, False)
        return carry

    lax.fori_loop(0, i // 2, pair, 0)

    @pl.when(i % 2 == 0)
    def _():
        consume(i, sa_ref, True)

    @pl.when(i % 2 == 1)
    def _():
        scores(i, sb_ref)
        consume(i - 1, sa_ref, False)
        consume(i, sb_ref, True)

    acc = acc_ref[...]
    o1 = acc[:tq, :hw] / acc[:tq, hw:]
    o2 = acc[tq:, :hw] / acc[tq:, hw:]
    lam = (jnp.exp(jnp.sum(lq1_ref[...] * lk1_ref[...], axis=-1, keepdims=True))
           - jnp.exp(jnp.sum(lq2_ref[...] * lk2_ref[...], axis=-1, keepdims=True)) + lam_init)
    o = o1 - lam * o2
    o = o * lax.rsqrt(jnp.mean(o * o, axis=-1, keepdims=True) + EPS) * g_ref[...]
    o_ref[0] = (o * (1.0 - lam_init)).astype(BF16)


def _attention(q, k, v, lq1, lk1, lq2, lk2, subln_g, lam_init, zero_shape):
    b, s, _ = q.shape
    tq = ATT_BLOCK
    hw = 2 * DA_HEAD_DIM
    assert hw == LANES
    n_steps = b * DA_HEADS * (s // tq)
    assert zero_shape[0] % n_steps == 0
    zrows = zero_shape[0] // n_steps
    slopes = tuple(LOG2E * 2.0 ** (-8.0 * (h + 1) / DA_HEADS) for h in range(DA_HEADS))
    vec = lambda n: pl.BlockSpec((1, n), lambda bi, h, i: (0, 0))
    return pl.pallas_call(
        functools.partial(_attn_kernel, lam_init=lam_init, slopes=slopes),
        grid=(b, DA_HEADS, s // tq),
        in_specs=[
            pl.BlockSpec((1, tq, hw), lambda bi, h, i: (bi, i, h)),
            pl.BlockSpec((1, s, hw), lambda bi, h, i: (bi, 0, h)),
            pl.BlockSpec((1, s, hw), lambda bi, h, i: (bi, 0, h)),
            vec(DA_HEAD_DIM), vec(DA_HEAD_DIM), vec(DA_HEAD_DIM), vec(DA_HEAD_DIM), vec(hw),
        ],
        out_specs=[pl.BlockSpec((1, tq, hw), lambda bi, h, i: (bi, i, h)),
                   pl.BlockSpec(memory_space=pl.ANY)],
        out_shape=[jax.ShapeDtypeStruct((b, s, DA_HEADS * hw), BF16),
                   jax.ShapeDtypeStruct(zero_shape, F32)],
        scratch_shapes=[
            pltpu.VMEM((2 * hw, s), BF16),
            pltpu.VMEM((s, 2 * hw), BF16),
            pltpu.VMEM((2 * tq, 2 * hw), BF16),
            pltpu.VMEM((2 * tq, LANES), F32),
            pltpu.VMEM((2 * tq, 2 * hw), F32),
            pltpu.VMEM((2 * tq, tq), F32),
            pltpu.VMEM((2 * tq, tq), F32),
            pltpu.VMEM((zrows,) + tuple(zero_shape[1:]), F32),
            pltpu.SemaphoreType.DMA(()),
        ],
        compiler_params=pltpu.CompilerParams(vmem_limit_bytes=VMEM_LIMIT,
                                             dimension_semantics=("arbitrary",) * 3),
        name="attention",
    )(q, k, v, lq1, lk1, lq2, lk2, subln_g)


def _hgrn_kernel(qd_ref, kd_ref, ke_ref, v_ref, sg_ref, dec_ref, g_ref, o_ref, st_ref):
    @pl.when(pl.program_id(1) == 0)
    def _():
        st_ref[...] = jnp.zeros(st_ref.shape, F32)

    n_ch = qd_ref.shape[1] // CHUNK
    r = lax.broadcasted_iota(I32, (CHUNK, CHUNK), 0)
    c = lax.broadcasted_iota(I32, (CHUNK, CHUNK), 1)
    causal = r >= c

    heads = [slice(h * HG_DK, (h + 1) * HG_DK) for h in range(HG_HEADS)]
    chunks = [slice(ci * CHUNK, (ci + 1) * CHUNK) for ci in range(n_ch)]
    incr = [[_nn(v_ref[0, rows, cols].astype(F32).T.astype(BF16), ke_ref[0, rows, cols])
             for rows in chunks] for cols in heads]
    start = []
    for h, cols in enumerate(heads):
        st = st_ref[h]
        before = []
        for ci in range(n_ch):
            before.append(st.astype(BF16))
            st = st * dec_ref[0, ci:ci + 1, cols] + incr[h][ci]
        st_ref[h] = st
        start.append(before)
    for h, cols in enumerate(heads):
        for ci, rows in enumerate(chunks):
            qd = qd_ref[0, rows, cols]
            scores = jnp.where(causal, _nt(qd, kd_ref[0, rows, cols]), 0.0).astype(BF16)
            o = _nn(scores, v_ref[0, rows, cols]) + _nt(qd, start[h][ci])
            o = o * lax.rsqrt(jnp.mean(o * o, axis=-1, keepdims=True) + EPS) * g_ref[...]
            o_ref[0, rows, cols] = (o * sg_ref[0, rows, cols].astype(F32)).astype(BF16)


def _hgrn(qd, kd, ke, v, sg, dec, norm_g):
    b, s, w = qd.shape
    ts = ROW_TILE
    tok = pl.BlockSpec((1, ts, w), lambda bi, i: (bi, i, 0))
    return pl.pallas_call(
        _hgrn_kernel,
        grid=(b, s // ts),
        in_specs=[tok, tok, tok, tok, tok,
                  pl.BlockSpec((1, ts // CHUNK, w), lambda bi, i: (bi, i, 0)),
                  pl.BlockSpec((1, HG_DV), lambda bi, i: (0, 0))],
        out_specs=tok,
        out_shape=jax.ShapeDtypeStruct((b, s, w), BF16),
        scratch_shapes=[pltpu.VMEM((HG_HEADS, HG_DV, HG_DK), F32)],
        name="hgrn",
    )(qd, kd, ke, v, sg, dec, norm_g)


def _merge_kernel(yda_ref, yhg_ref, gda_ref, ghg_ref, x_ref, gt1_ref, sh2_ref, sc2_ref, g2_ref,
                  wda_ref, whg_ref, wout_ref, wrh_ref, wrl_ref, br_ref,
                  x1_ref, h2_ref, eidx_ref, wts_ref):
    u = (gda_ref[0].astype(F32) * _nn(yda_ref[0], wda_ref[...])
         + ghg_ref[0].astype(F32) * _nn(yhg_ref[0], whg_ref[...]))
    x1 = x_ref[0] + gt1_ref[0] * _nn(u.astype(BF16), wout_ref[...])
    x1_ref[0] = x1
    h2 = x1 * lax.rsqrt(jnp.mean(x1 * x1, axis=-1, keepdims=True) + EPS) * g2_ref[...]
    h2 = h2 * (1.0 + sc2_ref[0]) + sh2_ref[0]
    h2_ref[0] = h2.reshape(h2.shape[0], SUBLANES, h2.shape[1] // SUBLANES)

    h_hi, h_lo = _split(h2)
    wrh = wrh_ref[...]
    both = _nt(jnp.concatenate([wrh, wrl_ref[...]], axis=0), h_hi)
    lg = both[:ROUTER_ROWS] + both[ROUTER_ROWS:] + _nt(wrh, h_lo) + br_ref[...]

    tm = h2.shape[0]
    eg = EXPERTS_PER_GROUP
    gl = lg[N_EXPERTS:N_EXPERTS + N_GROUPS]
    gmax = jnp.max(gl, axis=0, keepdims=True)
    g_p = 1.0 / jnp.sum(jnp.exp(gl - gmax), axis=0, keepdims=True)
    gi = lax.broadcasted_iota(I32, (N_GROUPS, tm), 0)
    g_idx = jnp.min(jnp.where(gl == gmax, gi, N_GROUPS), axis=0, keepdims=True)
    sel = lg[(N_GROUPS - 1) * eg:N_GROUPS * eg]
    for g in range(N_GROUPS - 2, -1, -1):
        sel = jnp.where(g_idx == g, lg[g * eg:(g + 1) * eg], sel)
    ei = lax.broadcasted_iota(I32, (eg, tm), 0)
    m1 = jnp.max(sel, axis=0, keepdims=True)
    i1 = jnp.min(jnp.where(sel == m1, ei, eg), axis=0, keepdims=True)
    sel2 = jnp.where(ei == i1, -jnp.inf, sel)
    m2 = jnp.max(sel2, axis=0, keepdims=True)
    i2 = jnp.min(jnp.where(sel2 == m2, ei, eg), axis=0, keepdims=True)
    t = jnp.exp(m2 - m1)
    wa = 1.0 / (1.0 + t)
    eidx_ref[0] = jnp.concatenate([g_idx * eg + i1, g_idx * eg + i2], axis=0)
    wts_ref[0] = jnp.concatenate([g_p * wa, g_p * (t * wa)], axis=0)


def _merge(yda, yhg, gda, ghg, x, mod, g2, wda, whg, wout, wrh, wrl, br):
    b, s, d = x.shape
    tm = ROW_TILE
    w = yda.shape[2]
    tok = lambda width: pl.BlockSpec((1, tm, width), lambda bi, i: (bi, i, 0))
    modspec = lambda k: pl.BlockSpec((1, 1, d), lambda bi, i: (bi, 0, k))
    full = lambda a: pl.BlockSpec(a.shape, lambda bi, i: (0,) * a.ndim)
    lane_rows = pl.BlockSpec((1, 2, tm), lambda bi, i: (bi, 0, i))
    return pl.pallas_call(
        _merge_kernel,
        grid=(b, s // tm),
        in_specs=[tok(w), tok(w), tok(d), tok(d), tok(d),
                  modspec(2), modspec(3), modspec(4), full(g2),
                  full(wda), full(whg), full(wout), full(wrh), full(wrl), full(br)],
        out_specs=[tok(d), pl.BlockSpec((1, tm, SUBLANES, d // SUBLANES), lambda bi, i: (bi, i, 0, 0)),
                   lane_rows, lane_rows],
        out_shape=[jax.ShapeDtypeStruct((b, s, d), F32),
                   jax.ShapeDtypeStruct((b, s, SUBLANES, d // SUBLANES), F32),
                   jax.ShapeDtypeStruct((b, 2, s), I32), jax.ShapeDtypeStruct((b, 2, s), F32)],
        compiler_params=pltpu.CompilerParams(vmem_limit_bytes=VMEM_LIMIT),
        name="merge",
    )(yda, yhg, gda, ghg, x, mod, mod, mod, g2, wda, whg, wout, wrh, wrl, br)


def _positions_kernel(e_ref, dest_ref, half_ref, blk_ref, nused_ref):
    e_mat = e_ref[...]
    nr = e_mat.shape[0]
    nb = blk_ref.shape[1]
    r = lax.broadcasted_iota(I32, (LANES, LANES), 0)
    c = lax.broadcasted_iota(I32, (LANES, LANES), 1)
    incl = jnp.where(r <= c, 1.0, 0.0).astype(BF16)
    ones = jnp.ones((LANES, LANES), BF16)
    rr = lax.broadcasted_iota(I32, (nr, nr), 0)
    cc = lax.broadcasted_iota(I32, (nr, nr), 1)
    before = jnp.where(cc < rr, 1.0, 0.0).astype(BF16)
    blk_start = (lax.broadcasted_iota(I32, (1, nb), 1) * EXPERT_BLOCK).astype(F32)

    def per_expert(e, carry):
        pstart, dest, blk = carry
        hit = e_mat == e
        oh = jnp.where(hit, 1.0, 0.0).astype(BF16)
        in_row = _nn(oh, incl)
        row_tot = _nn(oh, ones)
        row_off = _nn(before, row_tot.astype(BF16))
        total = row_off[nr - 1:nr, :] + row_tot[nr - 1:nr, :]
        dest = jnp.where(hit, pstart + row_off + in_row - 1.0, dest)
        pend = pstart + jnp.floor((total + (EXPERT_BLOCK - 1)) * (1.0 / EXPERT_BLOCK)) * EXPERT_BLOCK
        blk = blk + jnp.where(blk_start >= pend[:, :1], 1.0, 0.0)
        return pend, dest, blk

    pstart0 = jnp.zeros((1, LANES), F32)
    pend, dest, blk = lax.fori_loop(
        0, N_EXPERTS, per_expert,
        (pstart0, jnp.zeros(e_mat.shape, F32), jnp.zeros((1, nb), F32)), unroll=4)
    dest_ref[...] = dest.astype(I32)
    half_rows = EXPERT_BLOCK // 2
    block = jnp.floor(dest * (1.0 / EXPERT_BLOCK))
    in_block = dest - block * EXPERT_BLOCK
    upper = jnp.floor(in_block * (1.0 / half_rows))
    tile = block * half_rows + in_block - upper * half_rows
    half_ref[...] = (SUBLANES * tile + (SUBLANES // 2) * upper).astype(I32)
    blk_ref[...] = jnp.minimum(blk, N_EXPERTS - 1).astype(I32)
    nused_ref[...] = (pend * (1.0 / EXPERT_BLOCK)).astype(I32)


def _positions(e_mat, n_blk_pad):
    nr = e_mat.shape[0]
    return pl.pallas_call(
        _positions_kernel,
        out_shape=[jax.ShapeDtypeStruct((nr, LANES), I32),
                   jax.ShapeDtypeStruct((nr, LANES), I32),
                   jax.ShapeDtypeStruct((1, n_blk_pad), I32),
                   jax.ShapeDtypeStruct((1, LANES), I32)],
        name="positions",
    )(e_mat)


def _dispatch_kernel(dest_ref, h2_ref, xs_in_ref, xs_ref, hbuf, in_sem, out_sem, *, n_tok):
    del xs_in_ref
    i = pl.program_id(0)
    last = pl.num_programs(0) - 1
    slot = i % 2

    def load(step, buf):
        return pltpu.make_async_copy(h2_ref.at[pl.ds(step * GATHER_TILE, GATHER_TILE)], hbuf.at[buf],
                                     in_sem.at[buf])

    def drain(buf):
        for k in range(2):
            pltpu.make_async_copy(hbuf.at[buf], xs_ref.at[pl.ds(0, GATHER_TILE)], out_sem.at[buf, k]).wait()

    @pl.when(i == 0)
    def _():
        load(0, 0).start()

    @pl.when(i > 0)
    def _():
        drain(1 - slot)

    @pl.when(i < last)
    def _():
        load(i + 1, 1 - slot).start()

    load(i, slot).wait()
    base = i * GATHER_TILE

    def issue(r, carry):
        for k in range(2):
            pltpu.make_async_copy(hbuf.at[slot, r], xs_ref.at[dest_ref[k * n_tok + base + r]],
                                  out_sem.at[slot, k]).start(priority=k)
        return carry

    lax.fori_loop(0, GATHER_TILE, issue, 0, unroll=ISSUE_UNROLL)

    @pl.when(i == last)
    def _():
        drain(slot)


def _dispatch(dest, h2, xs_zero):
    n_tok = h2.shape[0]
    return pl.pallas_call(
        functools.partial(_dispatch_kernel, n_tok=n_tok),
        grid_spec=pltpu.PrefetchScalarGridSpec(
            num_scalar_prefetch=1,
            grid=(n_tok // GATHER_TILE,),
            in_specs=[pl.BlockSpec(memory_space=pl.ANY), pl.BlockSpec(memory_space=pl.ANY)],
            out_specs=pl.BlockSpec(memory_space=pl.ANY),
            scratch_shapes=[pltpu.VMEM((2, GATHER_TILE) + h2.shape[1:], F32),
                            pltpu.SemaphoreType.DMA((2,)), pltpu.SemaphoreType.DMA((2, 2))],
        ),
        out_shape=jax.ShapeDtypeStruct(xs_zero.shape, xs_zero.dtype),
        input_output_aliases={2: 0},
        compiler_params=pltpu.CompilerParams(dimension_semantics=("arbitrary",)),
        name="dispatch",
    )(dest, h2, xs_zero)


def _expert_kernel(blk_ref, nused_ref, xs_ref, w1_ref, w3_ref, w2_ref, ys_ref,
                   w1b, w3b, w2b, cur, sem):
    i = pl.program_id(0)
    n_used = nused_ref[0]
    e = blk_ref[i]
    fresh = (i == 0) | (blk_ref[jnp.maximum(i - 1, 0)] != e)

    def fetch(expert, which):
        return (pltpu.make_async_copy(w1_ref.at[expert], w1b.at[which], sem.at[0]),
                pltpu.make_async_copy(w3_ref.at[expert], w3b.at[which], sem.at[1]),
                pltpu.make_async_copy(w2_ref.at[expert], w2b.at[which], sem.at[2]))

    @pl.when(i < n_used)
    def _():
        @pl.when(i == 0)
        def _():
            cur[0] = 0
            for cp in fetch(e, 0):
                cp.start()

        @pl.when(fresh & (i > 0))
        def _():
            cur[0] = 1 - cur[0]

        @pl.when(fresh)
        def _():
            for cp in fetch(e, cur[0]):
                cp.wait()
            nxt = lax.while_loop(lambda j: (j < n_used) & (blk_ref[jnp.minimum(j, n_used - 1)] == e),
                                 lambda j: j + 1, i + 1)

            @pl.when(nxt < n_used)
            def _():
                for cp in fetch(blk_ref[jnp.minimum(nxt, n_used - 1)], 1 - cur[0]):
                    cp.start(priority=1)

        which = cur[0]
        rows = xs_ref.shape[0]
        x = xs_ref[...].reshape(rows, w1b.shape[1]).astype(BF16)
        a = _nn(x, w1b[which])
        y = _nn((_silu(a) * _nn(x, w3b[which])).astype(BF16), w2b[which])
        ys_ref[...] = _pack_rows(y).reshape((rows // 2,) + xs_ref.shape[1:]).reshape(ys_ref.shape)

    @pl.when(i >= nused_ref[0])
    def _():
        ys_ref[...] = jnp.zeros(ys_ref.shape, ys_ref.dtype)


def _experts(blk_e, n_used, xs, w1, w3, w2):
    p = xs.shape[0]
    tile = xs.shape[1:]
    d, de = w1.shape[1], w1.shape[2]
    rows = lambda i, blk, nu: (jnp.minimum(i, nu[0] - 1), 0, 0)
    hbm = pl.BlockSpec(memory_space=pl.ANY)
    return pl.pallas_call(
        _expert_kernel,
        grid_spec=pltpu.PrefetchScalarGridSpec(
            num_scalar_prefetch=2,
            grid=(p // EXPERT_BLOCK,),
            in_specs=[pl.BlockSpec((EXPERT_BLOCK,) + tile, rows), hbm, hbm, hbm],
            out_specs=pl.BlockSpec((EXPERT_BLOCK // 2 * tile[0], tile[1]), lambda i, blk, nu: (i, 0)),
            scratch_shapes=[pltpu.VMEM((2, d, de), BF16), pltpu.VMEM((2, d, de), BF16),
                            pltpu.VMEM((2, de, d), BF16),
                            pltpu.SMEM((1,), I32), pltpu.SemaphoreType.DMA((3,))],
        ),
        out_shape=jax.ShapeDtypeStruct((p // 2 * tile[0], tile[1]), U32),
        compiler_params=pltpu.CompilerParams(vmem_limit_bytes=VMEM_LIMIT,
                                             dimension_semantics=("arbitrary",)),
        name="experts",
    )(blk_e, n_used, xs, w1, w3, w2)


def _combine_kernel(half_ref, ys_ref, x1_ref, gt2_ref, wts_ref, g_ref, o_ref, ybuf, sem, *, n_tok, last):
    i = pl.program_id(0)
    slot = i % 2
    half = SUBLANES // 2
    pairs = GATHER_TILE // 2

    def gather(step, buf):
        for up in range(2):
            base = step * GATHER_TILE + up * pairs

            def issue(r, carry):
                for k in range(2):
                    row = pl.multiple_of(half_ref[k * n_tok + base + r], half)
                    pltpu.make_async_copy(ys_ref.at[pl.ds(row, half)],
                                          ybuf.at[buf, k, pl.ds(r * SUBLANES + up * half, half)],
                                          sem.at[buf, k]).start(priority=k)
                return carry

            lax.fori_loop(0, pairs, issue, 0, unroll=ISSUE_UNROLL)

    @pl.when(i == 0)
    def _():
        gather(0, 0)

    @pl.when(i + 1 < pl.num_programs(0))
    def _():
        gather(i + 1, 1 - slot)

    for k in range(2):
        pltpu.make_async_copy(ys_ref.at[pl.ds(0, pairs * SUBLANES)], ybuf.at[slot, k], sem.at[slot, k]).wait()

    def tokens(k):
        tiles = ybuf[slot, k].reshape(pairs, SUBLANES, LANES)
        return _unpack_rows(tiles.reshape(pairs, x1_ref.shape[1]))

    wts = wts_ref[...]
    ya = tokens(0)
    yb = tokens(1)
    x = x1_ref[...] + gt2_ref[0] * (wts[:, 0:1] * ya + wts[:, 1:2] * yb)
    if last:
        x = x * lax.rsqrt(jnp.mean(x * x, axis=-1, keepdims=True) + EPS) * g_ref[...]
    o_ref[...] = x


def _combine(half, ys, x1, mod, wts, final_g, seq, last=True):
    n_tok, d = x1.shape
    tt = GATHER_TILE
    per_batch = seq // tt
    return pl.pallas_call(
        functools.partial(_combine_kernel, n_tok=n_tok, last=last),
        grid_spec=pltpu.PrefetchScalarGridSpec(
            num_scalar_prefetch=1,
            grid=(n_tok // tt,),
            in_specs=[pl.BlockSpec(memory_space=pl.ANY),
                      pl.BlockSpec((tt, d), lambda i, half: (i, 0)),
                      pl.BlockSpec((1, 1, d), lambda i, half: (i // per_batch, 0, 5)),
                      pl.BlockSpec((tt, 2), lambda i, half: (i, 0)),
                      pl.BlockSpec((1, d), lambda i, half: (0, 0))],
            out_specs=pl.BlockSpec((tt, d), lambda i, half: (i, 0)),
            scratch_shapes=[pltpu.VMEM((2, 2, tt // 2 * SUBLANES, LANES), U32),
                            pltpu.SemaphoreType.DMA((2, 2))],
        ),
        out_shape=jax.ShapeDtypeStruct((n_tok, d), F32),
        compiler_params=pltpu.CompilerParams(dimension_semantics=("arbitrary",)),
        name="combine",
    )(half, ys, x1, mod, wts, final_g)


def kernel(x, c, w_ada, b_ada, norm1_g, w_in, lambda_q1, lambda_k1, lambda_q2, lambda_k2, da_subln_g, hg_lb_logits, hg_norm_g, w_up_da, w_up_hg, w_out, norm2_g, w_rg, b_rg, w_re, b_re, w1, w3, w2, final_g):
    b, s, d = x.shape
    n_tok = b * s
    depth = w_ada.shape[0]
    assert s % ROW_TILE == 0 and s % ATT_BLOCK == 0 and s % GATHER_TILE == 0
    assert (2 * n_tok) % LANES == 0 and d == SUBLANES * LANES

    c_pad = jnp.pad(c, ((0, (-b) % 8), (0, 0)))
    n_slots = 2 * n_tok + N_EXPERTS * EXPERT_BLOCK
    n_blk = n_slots // EXPERT_BLOCK
    n_blk_pad = -(-n_blk // LANES) * LANES

    for l in range(depth):
        lam_init = 0.8 - 0.6 * math.exp(-0.3 * l)
        mod = _adaln(c_pad, w_ada[l], b_ada[l][None, :])[:b].reshape(b, 1, 6 * d)

        (qda, kda, vda, qd, kd, ke, hi, sg, dec, gda, ghg, w1_bf, w3_bf, w2_bf) = _inproj(
            x, mod, norm1_g[l][None, :], w_in[l].astype(BF16), hg_lb_logits, l, w1[l], w3[l], w2[l])
        yda, xs_zero = _attention(qda, kda, vda, lambda_q1[l][None, :], lambda_k1[l][None, :],
                                  lambda_q2[l][None, :], lambda_k2[l][None, :], da_subln_g[l][None, :],
                                  lam_init, (n_slots, SUBLANES, d // SUBLANES))
        yhg = _hgrn(qd, kd, ke, hi, sg, dec, hg_norm_g[l][None, :])

        wr = jnp.concatenate([w_re[l].T, w_rg[l].T,
                              jnp.zeros((ROUTER_ROWS - N_EXPERTS - N_GROUPS, d), F32)], axis=0)
        wr_hi = wr.astype(BF16)
        wr_lo = (wr - wr_hi.astype(F32)).astype(BF16)
        br = jnp.concatenate([b_re[l], b_rg[l],
                              jnp.zeros((ROUTER_ROWS - N_EXPERTS - N_GROUPS,), F32)])[:, None]
        x1, h2, eidx, wts = _merge(yda, yhg, gda, ghg, x, mod, norm2_g[l][None, :],
                                   w_up_da[l].astype(BF16), w_up_hg[l].astype(BF16),
                                   w_out[l].astype(BF16), wr_hi, wr_lo, br)

        e_mat = jnp.transpose(eidx, (1, 0, 2)).reshape(2 * n_tok // LANES, LANES)
        dest, half, blk_e, n_used = _positions(e_mat, n_blk_pad)
        dest = dest.reshape(2 * n_tok)
        h2 = h2.reshape((n_tok,) + h2.shape[2:])
        xs = _dispatch(dest, h2, xs_zero)
        ys = _experts(blk_e[0, :n_blk], n_used[0, :1], xs, w1_bf, w3_bf, w2_bf)
        wts_tok = jnp.transpose(wts, (0, 2, 1)).reshape(n_tok, 2)
        x = _combine(half.reshape(2 * n_tok), ys, x1.reshape(n_tok, d), mod, wts_tok, final_g[None, :], s,
                     last=(l == depth - 1)).reshape(b, s, d)
    return x
```

```python
import functools
import math

import jax
import jax.numpy as jnp
from jax import lax
from jax.experimental import pallas as pl
from jax.experimental.pallas import tpu as pltpu

F32 = jnp.float32
BF16 = jnp.bfloat16
I32 = jnp.int32
U32 = jnp.uint32

EPS = 1e-6
NEG = -1e30
LOG2E = math.log2(math.e)

CHUNK = 64
DA_HEADS = 4
DA_HEAD_DIM = 64
HG_HEADS = 4
HG_DK = 128
HG_DV = 128
N_GROUPS = 4
EXPERTS_PER_GROUP = 8
N_EXPERTS = N_GROUPS * EXPERTS_PER_GROUP

LANES = 128
SUBLANES = 8
ADALN_COLS = 1024
ROW_TILE = 512
ATT_BLOCK = 512
BIAS_TERMS = 3
EXPERT_BLOCK = 256
GATHER_TILE = 512
ISSUE_UNROLL = 8
ROUTER_ROWS = 48
VMEM_LIMIT = 52 * 1024 * 1024


def _nt(a, b):
    return lax.dot_general(a, b, (((1,), (1,)), ((), ())), preferred_element_type=F32)


def _nn(a, b):
    return jnp.dot(a, b, preferred_element_type=F32)


def _split(a):
    hi = a.astype(BF16)
    lo = (a - hi.astype(F32)).astype(BF16)
    return hi, lo


def _sigmoid(x):
    return 1.0 / (1.0 + jnp.exp(-x))


def _silu(x):
    return x * _sigmoid(x)


def _pack_rows(y):
    rows, cols = y.shape[0] // 2, y.shape[1] // 2
    bits = lax.bitcast_convert_type(y.astype(BF16).astype(F32), U32)
    words = bits[:, :cols] | lax.shift_right_logical(bits[:, cols:], jnp.uint32(16))
    return jnp.concatenate([words[:rows], words[rows:]], axis=1)


def _unpack_rows(packed):
    cols = packed.shape[1] // 2
    words = jnp.concatenate([packed[:, :cols], packed[:, cols:]], axis=0)
    first = lax.bitcast_convert_type(words & jnp.uint32(0xFFFF0000), F32)
    second = lax.bitcast_convert_type(lax.shift_left(words, jnp.uint32(16)), F32)
    return jnp.concatenate([first, second], axis=1)


def _adaln_kernel(c_ref, w_ref, b_ref, o_ref):
    c_hi, c_lo = _split(_silu(c_ref[...]))
    w_hi, w_lo = _split(w_ref[...])
    o_ref[...] = _nn(c_hi, w_hi) + _nn(c_lo, w_hi) + _nn(c_hi, w_lo) + b_ref[...]


def _adaln(c_pad, w, b):
    rows, d = c_pad.shape
    n = w.shape[1]
    tn = ADALN_COLS
    return pl.pallas_call(
        _adaln_kernel,
        grid=(n // tn,),
        in_specs=[
            pl.BlockSpec((rows, d), lambda j: (0, 0)),
            pl.BlockSpec((d, tn), lambda j: (0, j)),
            pl.BlockSpec((1, tn), lambda j: (0, j)),
        ],
        out_specs=pl.BlockSpec((rows, tn), lambda j: (0, j)),
        out_shape=jax.ShapeDtypeStruct((rows, n), F32),
        name="adaln",
    )(c_pad, w, b)


def _inproj_kernel(x_ref, sh_ref, sc_ref, g_ref, w_ref, lb_ref, w1_ref, w3_ref, w2_ref,
                   qda_ref, kda_ref, vda_ref, qd_ref, kd_ref, ke_ref, hi_ref, sg_ref,
                   dec_ref, gda_ref, ghg_ref, w1b_ref, w3b_ref, w2b_ref, *, layer):
    w1b_ref[...] = w1_ref[...].astype(BF16)
    w3b_ref[...] = w3_ref[...].astype(BF16)
    w2b_ref[...] = w2_ref[...].astype(BF16)

    x = x_ref[0]
    tm = x.shape[0]
    h = x * lax.rsqrt(jnp.mean(x * x, axis=-1, keepdims=True) + EPS) * g_ref[...]
    h = h * (1.0 + sc_ref[0]) + sh_ref[0]
    hb = h.astype(BF16)

    def proj(c0, width):
        return _nn(hb, w_ref[:, c0:c0 + width])

    w_da = DA_HEADS * 2 * DA_HEAD_DIM
    w_hg = HG_HEADS * HG_DK
    qda_ref[0] = (proj(0, w_da) * (LOG2E / math.sqrt(DA_HEAD_DIM))).astype(BF16)
    kda_ref[0] = proj(w_da, w_da).astype(BF16)
    vda_ref[0] = proj(2 * w_da, w_da).astype(BF16)
    c0 = 3 * w_da

    lbl = lb_ref[...]
    lbe = jnp.exp(lbl - jnp.max(lbl, axis=0, keepdims=True))
    lbs = lbe / jnp.sum(lbe, axis=0, keepdims=True)
    lb = jnp.sum(lbs[:layer + 1], axis=0, keepdims=True)

    f = lb + (1.0 - lb) * _sigmoid(proj(c0 + w_hg, w_hg))
    logf = jnp.log(f)
    kk = 1.0 - f
    pos = lax.broadcasted_iota(I32, logf.shape, 0) & (CHUNK - 1)
    a = logf
    step = 1
    while step < CHUNK:
        a = a + jnp.where(pos >= step, pltpu.roll(a, step, axis=0), 0.0)
        step *= 2
    n_ch = tm // CHUNK
    a = a.reshape(n_ch, CHUNK, w_hg)
    a_last = a[:, CHUNK - 1:CHUNK, :]
    kk3 = kk.reshape(n_ch, CHUNK, w_hg)
    qq3 = _silu(proj(c0, w_hg)).reshape(n_ch, CHUNK, w_hg)
    qd_ref[0] = (qq3 * jnp.exp(a)).reshape(tm, w_hg).astype(BF16)
    kd_ref[0] = (kk3 * jnp.exp(-a)).reshape(tm, w_hg).astype(BF16)
    ke_ref[0] = (kk3 * jnp.exp(a_last - a)).reshape(tm, w_hg).astype(BF16)
    dec_ref[0] = jnp.exp(a_last).reshape(n_ch, w_hg)
    hi_ref[0] = proj(c0 + 2 * w_hg, w_hg).astype(BF16)
    sg_ref[0] = _silu(proj(c0 + 3 * w_hg, w_hg)).astype(BF16)
    c1 = c0 + 4 * w_hg
    d = x.shape[1]
    gda_ref[0] = _sigmoid(proj(c1, d)).astype(BF16)
    ghg_ref[0] = _sigmoid(proj(c1 + d, d)).astype(BF16)


def _inproj(x, mod, g, w_bf, lb_logits, layer, w1, w3, w2):
    b, s, d = x.shape
    tm = ROW_TILE
    n_cols = w_bf.shape[1]
    per_b = s // tm
    assert w1.shape[0] % (b * per_b) == 0
    e_step = w1.shape[0] // (b * per_b)
    wspec = lambda a: pl.BlockSpec((e_step,) + a.shape[1:], lambda bi, i: (bi * per_b + i, 0, 0))
    w_da = DA_HEADS * 2 * DA_HEAD_DIM
    w_hg = HG_HEADS * HG_DK

    def tok(width, dtype):
        return (pl.BlockSpec((1, tm, width), lambda bi, i: (bi, i, 0)),
                jax.ShapeDtypeStruct((b, s, width), dtype))

    outs = [tok(w_da, BF16)] * 3 + [tok(w_hg, BF16)] * 5
    outs.append((pl.BlockSpec((1, tm // CHUNK, w_hg), lambda bi, i: (bi, i, 0)),
                 jax.ShapeDtypeStruct((b, s // CHUNK, w_hg), F32)))
    outs += [tok(d, BF16)] * 2
    outs += [(wspec(a), jax.ShapeDtypeStruct(a.shape, BF16)) for a in (w1, w3, w2)]
    return pl.pallas_call(
        functools.partial(_inproj_kernel, layer=layer),
        grid=(b, s // tm),
        in_specs=[
            pl.BlockSpec((1, tm, d), lambda bi, i: (bi, i, 0)),
            pl.BlockSpec((1, 1, d), lambda bi, i: (bi, 0, 0)),
            pl.BlockSpec((1, 1, d), lambda bi, i: (bi, 0, 1)),
            pl.BlockSpec((1, d), lambda bi, i: (0, 0)),
            pl.BlockSpec((d, n_cols), lambda bi, i: (0, 0), pipeline_mode=pl.Buffered(1)),
            pl.BlockSpec(lb_logits.shape, lambda bi, i: (0, 0)),
            wspec(w1), wspec(w3), wspec(w2),
        ],
        out_specs=[o[0] for o in outs],
        out_shape=[o[1] for o in outs],
        compiler_params=pltpu.CompilerParams(vmem_limit_bytes=VMEM_LIMIT),
        name="inproj",
    )(x, mod, mod, g, w_bf, lb_logits, w1, w3, w2)


def _attn_kernel(q_ref, k_ref, v_ref, lq1_ref, lk1_ref, lq2_ref, lk2_ref, g_ref, o_ref, zs_ref,
                 kaug_ref, vaug_ref, qaug_ref, m_ref, acc_ref, sa_ref, sb_ref, zero_ref, zsem,
                 *, lam_init, slopes):
    hd = pl.program_id(1)
    i = pl.program_id(2)

    step = (pl.program_id(0) * pl.num_programs(1) + hd) * pl.num_programs(2) + i
    n_steps = pl.num_programs(0) * pl.num_programs(1) * pl.num_programs(2)
    zrows = zero_ref.shape[0]

    def zero_copy(t):
        return pltpu.make_async_copy(zero_ref, zs_ref.at[pl.ds(t * zrows, zrows)], zsem)

    @pl.when(step == 0)
    def _():
        zero_ref[...] = jnp.zeros(zero_ref.shape, zero_ref.dtype)

    @pl.when(step > 0)
    def _():
        zero_copy(step - 1).wait()

    zero_copy(step).start()

    @pl.when(step == n_steps - 1)
    def _():
        zero_copy(step).wait()
    tq = q_ref.shape[1]
    hw = q_ref.shape[2]
    n_key = k_ref.shape[1]

    slope = jnp.float32(slopes[-1])
    for idx in range(len(slopes) - 2, -1, -1):
        slope = jnp.where(hd == idx, jnp.float32(slopes[idx]), slope)

    @pl.when(i == 0)
    def _():
        kaug_ref[:hw, :] = k_ref[0].astype(F32).T.astype(BF16)
        jrel = (lax.broadcasted_iota(I32, (hw, n_key), 1) & (tq - 1)).astype(F32) * slope
        krow = lax.broadcasted_iota(I32, (hw, n_key), 0)
        bias_rows = jnp.zeros((hw, n_key), F32)
        rest = jrel
        for term in range(BIAS_TERMS):
            part = rest.astype(BF16).astype(F32)
            bias_rows = jnp.where(krow == term, part, bias_rows)
            rest = rest - part
        kaug_ref[hw:, :] = bias_rows.astype(BF16)
        vaug_ref[:, :hw] = v_ref[0]
        vaug_ref[:, hw:] = jnp.ones((n_key, hw), BF16)

    q = q_ref[0]
    lane = lax.broadcasted_iota(I32, q.shape, 1)
    zero = jnp.zeros_like(q)
    ones = jnp.where(lane < BIAS_TERMS, 1.0, 0.0).astype(BF16)
    qaug_ref[:tq, :hw] = jnp.where(lane < DA_HEAD_DIM, q, zero)
    qaug_ref[tq:, :hw] = jnp.where(lane >= DA_HEAD_DIM, q, zero)
    qaug_ref[:tq, hw:] = ones
    qaug_ref[tq:, hw:] = ones

    m_ref[...] = jnp.full(m_ref.shape, NEG, F32)
    acc_ref[...] = jnp.zeros(acc_ref.shape, F32)

    def scores(j, buf):
        buf[...] = _nn(qaug_ref[...], kaug_ref[:, pl.ds(pl.multiple_of(j * tq, tq), tq)])

    def consume(j, buf, diagonal):
        s = buf[...]
        if diagonal:
            ii = lax.broadcasted_iota(I32, s.shape, 0) & (tq - 1)
            jj = lax.broadcasted_iota(I32, s.shape, 1)
            ahead = jnp.minimum(ii - jj, 0).astype(F32)
            s = s + jnp.where((jj // CHUNK) <= (ii // CHUNK), (2.0 * slope) * ahead, NEG)
            shift = jnp.float32(0.0)
        else:
            shift = -slope * ((i - j) * tq).astype(F32)
        m_old = m_ref[...]
        m_new = jnp.maximum(m_old, jnp.max(s, axis=-1, keepdims=True) + shift)
        p = jnp.exp2(s - jnp.concatenate([m_new - shift] * (tq // LANES), axis=1))
        alpha = jnp.concatenate([jnp.exp2(m_old - m_new)] * (acc_ref.shape[1] // LANES), axis=1)
        vb = vaug_ref[pl.ds(pl.multiple_of(j * tq, tq), tq), :]
        acc_ref[...] = alpha * acc_ref[...] + _nn(p.astype(BF16), vb)
        m_ref[...] = m_new

    scores(0, sa_ref)

    def pair(pi, carry):
        j = 2 * pi
        scores(j + 1, sb_ref)
        consume(j, sa_ref, False)
        scores(j + 2, sa_ref)
        consume(j + 1, sb_ref, False)
        return carry

    lax.fori_loop(0, i // 2, pair, 0)

    @pl.when(i % 2 == 0)
    def _():
        consume(i, sa_ref, True)

    @pl.when(i % 2 == 1)
    def _():
        scores(i, sb_ref)
        consume(i - 1, sa_ref, False)
        consume(i, sb_ref, True)

    acc = acc_ref[...]
    o1 = acc[:tq, :hw] / acc[:tq, hw:]
    o2 = acc[tq:, :hw] / acc[tq:, hw:]
    lam = (jnp.exp(jnp.sum(lq1_ref[...] * lk1_ref[...], axis=-1, keepdims=True))
           - jnp.exp(jnp.sum(lq2_ref[...] * lk2_ref[...], axis=-1, keepdims=True)) + lam_init)
    o = o1 - lam * o2
    o = o * lax.rsqrt(jnp.mean(o * o, axis=-1, keepdims=True) + EPS) * g_ref[...]
    o_ref[0] = (o * (1.0 - lam_init)).astype(BF16)


def _attention(q, k, v, lq1, lk1, lq2, lk2, subln_g, lam_init, zero_buf):
    b, s, _ = q.shape
    zero_shape = zero_buf.shape
    tq = ATT_BLOCK
    hw = 2 * DA_HEAD_DIM
    assert hw == LANES
    n_steps = b * DA_HEADS * (s // tq)
    assert zero_shape[0] % n_steps == 0
    zrows = zero_shape[0] // n_steps
    slopes = tuple(LOG2E * 2.0 ** (-8.0 * (h + 1) / DA_HEADS) for h in range(DA_HEADS))
    vec = lambda n: pl.BlockSpec((1, n), lambda bi, h, i: (0, 0))
    return pl.pallas_call(
        functools.partial(_attn_kernel, lam_init=lam_init, slopes=slopes),
        grid=(b, DA_HEADS, s // tq),
        in_specs=[
            pl.BlockSpec((1, tq, hw), lambda bi, h, i: (bi, i, h)),
            pl.BlockSpec((1, s, hw), lambda bi, h, i: (bi, 0, h)),
            pl.BlockSpec((1, s, hw), lambda bi, h, i: (bi, 0, h)),
            vec(DA_HEAD_DIM), vec(DA_HEAD_DIM), vec(DA_HEAD_DIM), vec(DA_HEAD_DIM), vec(hw),
        ],
        out_specs=[pl.BlockSpec((1, tq, hw), lambda bi, h, i: (bi, i, h)),
                   pl.BlockSpec(memory_space=pl.ANY)],
        out_shape=[jax.ShapeDtypeStruct((b, s, DA_HEADS * hw), BF16),
                   zero_buf],
        scratch_shapes=[
            pltpu.VMEM((2 * hw, s), BF16),
            pltpu.VMEM((s, 2 * hw), BF16),
            pltpu.VMEM((2 * tq, 2 * hw), BF16),
            pltpu.VMEM((2 * tq, LANES), F32),
            pltpu.VMEM((2 * tq, 2 * hw), F32),
            pltpu.VMEM((2 * tq, tq), F32),
            pltpu.VMEM((2 * tq, tq), F32),
            pltpu.VMEM((zrows,) + tuple(zero_shape[1:]), zero_buf.dtype),
            pltpu.SemaphoreType.DMA(()),
        ],
        compiler_params=pltpu.CompilerParams(vmem_limit_bytes=VMEM_LIMIT,
                                             dimension_semantics=("arbitrary",) * 3),
        name="attention",
    )(q, k, v, lq1, lk1, lq2, lk2, subln_g)


def _hgrn_kernel(qd_ref, kd_ref, ke_ref, v_ref, sg_ref, dec_ref, g_ref, o_ref, st_ref):
    @pl.when(pl.program_id(1) == 0)
    def _():
        st_ref[...] = jnp.zeros(st_ref.shape, F32)

    n_ch = qd_ref.shape[1] // CHUNK
    r = lax.broadcasted_iota(I32, (CHUNK, CHUNK), 0)
    c = lax.broadcasted_iota(I32, (CHUNK, CHUNK), 1)
    causal = r >= c

    heads = [slice(h * HG_DK, (h + 1) * HG_DK) for h in range(HG_HEADS)]
    chunks = [slice(ci * CHUNK, (ci + 1) * CHUNK) for ci in range(n_ch)]
    incr = [[_nn(v_ref[0, rows, cols].astype(F32).T.astype(BF16), ke_ref[0, rows, cols])
             for rows in chunks] for cols in heads]
    start = []
    for h, cols in enumerate(heads):
        st = st_ref[h]
        before = []
        for ci in range(n_ch):
            before.append(st.astype(BF16))
            st = st * dec_ref[0, ci:ci + 1, cols] + incr[h][ci]
        st_ref[h] = st
        start.append(before)
    for h, cols in enumerate(heads):
        for ci, rows in enumerate(chunks):
            qd = qd_ref[0, rows, cols]
            scores = jnp.where(causal, _nt(qd, kd_ref[0, rows, cols]), 0.0).astype(BF16)
            o = _nn(scores, v_ref[0, rows, cols]) + _nt(qd, start[h][ci])
            o = o * lax.rsqrt(jnp.mean(o * o, axis=-1, keepdims=True) + EPS) * g_ref[...]
            o_ref[0, rows, cols] = (o * sg_ref[0, rows, cols].astype(F32)).astype(BF16)


def _hgrn(qd, kd, ke, v, sg, dec, norm_g):
    b, s, w = qd.shape
    ts = ROW_TILE
    tok = pl.BlockSpec((1, ts, w), lambda bi, i: (bi, i, 0))
    return pl.pallas_call(
        _hgrn_kernel,
        grid=(b, s // ts),
        in_specs=[tok, tok, tok, tok, tok,
                  pl.BlockSpec((1, ts // CHUNK, w), lambda bi, i: (bi, i, 0)),
                  pl.BlockSpec((1, HG_DV), lambda bi, i: (0, 0))],
        out_specs=tok,
        out_shape=jax.ShapeDtypeStruct((b, s, w), BF16),
        scratch_shapes=[pltpu.VMEM((HG_HEADS, HG_DV, HG_DK), F32)],
        name="hgrn",
    )(qd, kd, ke, v, sg, dec, norm_g)


def _merge_kernel(yda_ref, yhg_ref, gda_ref, ghg_ref, x_ref, gt1_ref, sh2_ref, sc2_ref, g2_ref,
                  wda_ref, whg_ref, wout_ref, wrh_ref, wrl_ref, br_ref,
                  x1_ref, h2_ref, eidx_ref, wts_ref):
    u = (gda_ref[0].astype(F32) * _nn(yda_ref[0], wda_ref[...])
         + ghg_ref[0].astype(F32) * _nn(yhg_ref[0], whg_ref[...]))
    x1 = x_ref[0] + gt1_ref[0] * _nn(u.astype(BF16), wout_ref[...])
    x1_ref[0] = x1
    h2 = x1 * lax.rsqrt(jnp.mean(x1 * x1, axis=-1, keepdims=True) + EPS) * g2_ref[...]
    h2 = h2 * (1.0 + sc2_ref[0]) + sh2_ref[0]
    h2_ref[0] = _pack_rows(h2).reshape(h2.shape[0] // 2, SUBLANES, LANES).reshape(h2_ref.shape[1:])

    h_hi, h_lo = _split(h2)
    wrh = wrh_ref[...]
    both = _nt(jnp.concatenate([wrh, wrl_ref[...]], axis=0), h_hi)
    lg = both[:ROUTER_ROWS] + both[ROUTER_ROWS:] + _nt(wrh, h_lo) + br_ref[...]

    tm = h2.shape[0]
    eg = EXPERTS_PER_GROUP
    gl = lg[N_EXPERTS:N_EXPERTS + N_GROUPS]
    gmax = jnp.max(gl, axis=0, keepdims=True)
    g_p = 1.0 / jnp.sum(jnp.exp(gl - gmax), axis=0, keepdims=True)
    gi = lax.broadcasted_iota(I32, (N_GROUPS, tm), 0)
    g_idx = jnp.min(jnp.where(gl == gmax, gi, N_GROUPS), axis=0, keepdims=True)
    sel = lg[(N_GROUPS - 1) * eg:N_GROUPS * eg]
    for g in range(N_GROUPS - 2, -1, -1):
        sel = jnp.where(g_idx == g, lg[g * eg:(g + 1) * eg], sel)
    ei = lax.broadcasted_iota(I32, (eg, tm), 0)
    m1 = jnp.max(sel, axis=0, keepdims=True)
    i1 = jnp.min(jnp.where(sel == m1, ei, eg), axis=0, keepdims=True)
    sel2 = jnp.where(ei == i1, -jnp.inf, sel)
    m2 = jnp.max(sel2, axis=0, keepdims=True)
    i2 = jnp.min(jnp.where(sel2 == m2, ei, eg), axis=0, keepdims=True)
    t = jnp.exp(m2 - m1)
    wa = 1.0 / (1.0 + t)
    eidx_ref[0] = jnp.concatenate([g_idx * eg + i1, g_idx * eg + i2], axis=0)
    wts_ref[0] = jnp.concatenate([g_p * wa, g_p * (t * wa)], axis=0)


def _merge(yda, yhg, gda, ghg, x, mod, g2, wda, whg, wout, wrh, wrl, br):
    b, s, d = x.shape
    tm = ROW_TILE
    w = yda.shape[2]
    tok = lambda width: pl.BlockSpec((1, tm, width), lambda bi, i: (bi, i, 0))
    modspec = lambda k: pl.BlockSpec((1, 1, d), lambda bi, i: (bi, 0, k))
    full = lambda a: pl.BlockSpec(a.shape, lambda bi, i: (0,) * a.ndim)
    lane_rows = pl.BlockSpec((1, 2, tm), lambda bi, i: (bi, 0, i))
    return pl.pallas_call(
        _merge_kernel,
        grid=(b, s // tm),
        in_specs=[tok(w), tok(w), tok(d), tok(d), tok(d),
                  modspec(2), modspec(3), modspec(4), full(g2),
                  full(wda), full(whg), full(wout), full(wrh), full(wrl), full(br)],
        out_specs=[tok(d), pl.BlockSpec((1, tm // 2 * SUBLANES, LANES), lambda bi, i: (bi, i, 0)),
                   lane_rows, lane_rows],
        out_shape=[jax.ShapeDtypeStruct((b, s, d), F32),
                   jax.ShapeDtypeStruct((b, s // 2 * SUBLANES, LANES), U32),
                   jax.ShapeDtypeStruct((b, 2, s), I32), jax.ShapeDtypeStruct((b, 2, s), F32)],
        compiler_params=pltpu.CompilerParams(vmem_limit_bytes=VMEM_LIMIT),
        name="merge",
    )(yda, yhg, gda, ghg, x, mod, mod, mod, g2, wda, whg, wout, wrh, wrl, br)


def _positions_kernel(e_ref, half_ref, blk_ref, nused_ref):
    e_mat = e_ref[...]
    nr = e_mat.shape[0]
    nb = blk_ref.shape[1]
    r = lax.broadcasted_iota(I32, (LANES, LANES), 0)
    c = lax.broadcasted_iota(I32, (LANES, LANES), 1)
    incl = jnp.where(r <= c, 1.0, 0.0).astype(BF16)
    ones = jnp.ones((LANES, LANES), BF16)
    rr = lax.broadcasted_iota(I32, (nr, nr), 0)
    cc = lax.broadcasted_iota(I32, (nr, nr), 1)
    before = jnp.where(cc < rr, 1.0, 0.0).astype(BF16)
    blk_start = (lax.broadcasted_iota(I32, (1, nb), 1) * EXPERT_BLOCK).astype(F32)

    def per_expert(e, carry):
        pstart, dest, blk = carry
        hit = e_mat == e
        oh = jnp.where(hit, 1.0, 0.0).astype(BF16)
        in_row = _nn(oh, incl)
        row_tot = _nn(oh, ones)
        row_off = _nn(before, row_tot.astype(BF16))
        total = row_off[nr - 1:nr, :] + row_tot[nr - 1:nr, :]
        dest = jnp.where(hit, pstart + row_off + in_row - 1.0, dest)
        pend = pstart + jnp.floor((total + (EXPERT_BLOCK - 1)) * (1.0 / EXPERT_BLOCK)) * EXPERT_BLOCK
        blk = blk + jnp.where(blk_start >= pend[:, :1], 1.0, 0.0)
        return pend, dest, blk

    pstart0 = jnp.zeros((1, LANES), F32)
    pend, dest, blk = lax.fori_loop(
        0, N_EXPERTS, per_expert,
        (pstart0, jnp.zeros(e_mat.shape, F32), jnp.zeros((1, nb), F32)), unroll=4)
    half_rows = EXPERT_BLOCK // 2
    block = jnp.floor(dest * (1.0 / EXPERT_BLOCK))
    in_block = dest - block * EXPERT_BLOCK
    upper = jnp.floor(in_block * (1.0 / half_rows))
    tile = block * half_rows + in_block - upper * half_rows
    half_ref[...] = (SUBLANES * tile + (SUBLANES // 2) * upper).astype(I32)
    blk_ref[...] = jnp.minimum(blk, N_EXPERTS - 1).astype(I32)
    nused_ref[...] = (pend * (1.0 / EXPERT_BLOCK)).astype(I32)


def _positions(e_mat, n_blk_pad):
    nr = e_mat.shape[0]
    return pl.pallas_call(
        _positions_kernel,
        out_shape=[jax.ShapeDtypeStruct((nr, LANES), I32),
                   jax.ShapeDtypeStruct((1, n_blk_pad), I32),
                   jax.ShapeDtypeStruct((1, LANES), I32)],
        name="positions",
    )(e_mat)


def _dispatch_kernel(half_ref, h2_ref, xs_in_ref, xs_ref, hbuf, in_sem, out_sem, *, n_tok):
    del xs_in_ref
    i = pl.program_id(0)
    last = pl.num_programs(0) - 1
    slot = i % 2
    half = SUBLANES // 2
    pairs = GATHER_TILE // 2
    tile_rows = pairs * SUBLANES

    def load(step, buf):
        return pltpu.make_async_copy(h2_ref.at[pl.ds(step * tile_rows, tile_rows)], hbuf.at[buf],
                                     in_sem.at[buf])

    def drain(buf):
        for k in range(2):
            pltpu.make_async_copy(hbuf.at[buf], xs_ref.at[pl.ds(0, tile_rows)], out_sem.at[buf, k]).wait()

    @pl.when(i == 0)
    def _():
        load(0, 0).start()

    @pl.when(i > 0)
    def _():
        drain(1 - slot)

    @pl.when(i < last)
    def _():
        load(i + 1, 1 - slot).start()

    load(i, slot).wait()

    for up in range(2):
        base = i * GATHER_TILE + up * pairs

        def issue(r, carry):
            for k in range(2):
                row = pl.multiple_of(half_ref[k * n_tok + base + r], half)
                pltpu.make_async_copy(hbuf.at[slot, pl.ds(r * SUBLANES + up * half, half)],
                                      xs_ref.at[pl.ds(row, half)], out_sem.at[slot, k]).start(priority=k)
            return carry

        lax.fori_loop(0, pairs, issue, 0, unroll=ISSUE_UNROLL)

    @pl.when(i == last)
    def _():
        drain(slot)


def _dispatch(half, h2, xs_zero):
    n_tok = h2.shape[0] // (SUBLANES // 2)
    return pl.pallas_call(
        functools.partial(_dispatch_kernel, n_tok=n_tok),
        grid_spec=pltpu.PrefetchScalarGridSpec(
            num_scalar_prefetch=1,
            grid=(n_tok // GATHER_TILE,),
            in_specs=[pl.BlockSpec(memory_space=pl.ANY), pl.BlockSpec(memory_space=pl.ANY)],
            out_specs=pl.BlockSpec(memory_space=pl.ANY),
            scratch_shapes=[pltpu.VMEM((2, GATHER_TILE // 2 * SUBLANES, LANES), h2.dtype),
                            pltpu.SemaphoreType.DMA((2,)), pltpu.SemaphoreType.DMA((2, 2))],
        ),
        out_shape=jax.ShapeDtypeStruct(xs_zero.shape, xs_zero.dtype),
        input_output_aliases={2: 0},
        compiler_params=pltpu.CompilerParams(dimension_semantics=("arbitrary",)),
        name="dispatch",
    )(half, h2, xs_zero)


def _expert_kernel(blk_ref, nused_ref, xs_ref, w1_ref, w3_ref, w2_ref, ys_ref,
                   w1b, w3b, w2b, cur, sem):
    i = pl.program_id(0)
    n_used = nused_ref[0]
    e = blk_ref[i]
    fresh = (i == 0) | (blk_ref[jnp.maximum(i - 1, 0)] != e)

    def fetch(expert, which):
        return (pltpu.make_async_copy(w1_ref.at[expert], w1b.at[which], sem.at[0]),
                pltpu.make_async_copy(w3_ref.at[expert], w3b.at[which], sem.at[1]),
                pltpu.make_async_copy(w2_ref.at[expert], w2b.at[which], sem.at[2]))

    @pl.when(i < n_used)
    def _():
        @pl.when(i == 0)
        def _():
            cur[0] = 0
            for cp in fetch(e, 0):
                cp.start()

        @pl.when(fresh & (i > 0))
        def _():
            cur[0] = 1 - cur[0]

        @pl.when(fresh)
        def _():
            for cp in fetch(e, cur[0]):
                cp.wait()
            nxt = lax.while_loop(lambda j: (j < n_used) & (blk_ref[jnp.minimum(j, n_used - 1)] == e),
                                 lambda j: j + 1, i + 1)

            @pl.when(nxt < n_used)
            def _():
                for cp in fetch(blk_ref[jnp.minimum(nxt, n_used - 1)], 1 - cur[0]):
                    cp.start(priority=1)

        which = cur[0]
        tiles = (xs_ref.shape[0] // SUBLANES, SUBLANES, LANES)
        x = _unpack_rows(xs_ref[...].reshape(tiles).reshape(tiles[0], w1b.shape[1])).astype(BF16)
        a = _nn(x, w1b[which])
        y = _nn((_silu(a) * _nn(x, w3b[which])).astype(BF16), w2b[which])
        ys_ref[...] = _pack_rows(y).reshape(tiles).reshape(ys_ref.shape)

    @pl.when(i >= nused_ref[0])
    def _():
        ys_ref[...] = jnp.zeros(ys_ref.shape, ys_ref.dtype)


def _experts(blk_e, n_used, xs, w1, w3, w2):
    block_rows = EXPERT_BLOCK // 2 * SUBLANES
    d, de = w1.shape[1], w1.shape[2]
    rows = lambda i, blk, nu: (jnp.minimum(i, nu[0] - 1), 0)
    hbm = pl.BlockSpec(memory_space=pl.ANY)
    return pl.pallas_call(
        _expert_kernel,
        grid_spec=pltpu.PrefetchScalarGridSpec(
            num_scalar_prefetch=2,
            grid=(xs.shape[0] // block_rows,),
            in_specs=[pl.BlockSpec((block_rows, LANES), rows), hbm, hbm, hbm],
            out_specs=pl.BlockSpec((block_rows, LANES), lambda i, blk, nu: (i, 0)),
            scratch_shapes=[pltpu.VMEM((2, d, de), BF16), pltpu.VMEM((2, d, de), BF16),
                            pltpu.VMEM((2, de, d), BF16),
                            pltpu.SMEM((1,), I32), pltpu.SemaphoreType.DMA((3,))],
        ),
        out_shape=jax.ShapeDtypeStruct(xs.shape, xs.dtype),
        compiler_params=pltpu.CompilerParams(vmem_limit_bytes=VMEM_LIMIT,
                                             dimension_semantics=("arbitrary",)),
        name="experts",
    )(blk_e, n_used, xs, w1, w3, w2)


def _combine_kernel(half_ref, ys_ref, x1_ref, gt2_ref, wts_ref, g_ref, o_ref, ybuf, sem, *, n_tok, last):
    i = pl.program_id(0)
    slot = i % 2
    half = SUBLANES // 2
    pairs = GATHER_TILE // 2

    def gather(step, buf):
        for up in range(2):
            base = step * GATHER_TILE + up * pairs

            def issue(r, carry):
                for k in range(2):
                    row = pl.multiple_of(half_ref[k * n_tok + base + r], half)
                    pltpu.make_async_copy(ys_ref.at[pl.ds(row, half)],
                                          ybuf.at[buf, k, pl.ds(r * SUBLANES + up * half, half)],
                                          sem.at[buf, k]).start(priority=k)
                return carry

            lax.fori_loop(0, pairs, issue, 0, unroll=ISSUE_UNROLL)

    @pl.when(i == 0)
    def _():
        gather(0, 0)

    @pl.when(i + 1 < pl.num_programs(0))
    def _():
        gather(i + 1, 1 - slot)

    for k in range(2):
        pltpu.make_async_copy(ys_ref.at[pl.ds(0, pairs * SUBLANES)], ybuf.at[slot, k], sem.at[slot, k]).wait()

    def tokens(k):
        tiles = ybuf[slot, k].reshape(pairs, SUBLANES, LANES)
        return _unpack_rows(tiles.reshape(pairs, x1_ref.shape[1]))

    wts = wts_ref[...]
    ya = tokens(0)
    yb = tokens(1)
    x = x1_ref[...] + gt2_ref[0] * (wts[:, 0:1] * ya + wts[:, 1:2] * yb)
    if last:
        x = x * lax.rsqrt(jnp.mean(x * x, axis=-1, keepdims=True) + EPS) * g_ref[...]
    o_ref[...] = x


def _combine(half, ys, x1, mod, wts, final_g, seq, last=True):
    n_tok, d = x1.shape
    tt = GATHER_TILE
    per_batch = seq // tt
    return pl.pallas_call(
        functools.partial(_combine_kernel, n_tok=n_tok, last=last),
        grid_spec=pltpu.PrefetchScalarGridSpec(
            num_scalar_prefetch=1,
            grid=(n_tok // tt,),
            in_specs=[pl.BlockSpec(memory_space=pl.ANY),
                      pl.BlockSpec((tt, d), lambda i, half: (i, 0)),
                      pl.BlockSpec((1, 1, d), lambda i, half: (i // per_batch, 0, 5)),
                      pl.BlockSpec((tt, 2), lambda i, half: (i, 0)),
                      pl.BlockSpec((1, d), lambda i, half: (0, 0))],
            out_specs=pl.BlockSpec((tt, d), lambda i, half: (i, 0)),
            scratch_shapes=[pltpu.VMEM((2, 2, tt // 2 * SUBLANES, LANES), U32),
                            pltpu.SemaphoreType.DMA((2, 2))],
        ),
        out_shape=jax.ShapeDtypeStruct((n_tok, d), F32),
        compiler_params=pltpu.CompilerParams(dimension_semantics=("arbitrary",)),
        name="combine",
    )(half, ys, x1, mod, wts, final_g)


def kernel(x, c, w_ada, b_ada, norm1_g, w_in, lambda_q1, lambda_k1, lambda_q2, lambda_k2, da_subln_g, hg_lb_logits, hg_norm_g, w_up_da, w_up_hg, w_out, norm2_g, w_rg, b_rg, w_re, b_re, w1, w3, w2, final_g):
    b, s, d = x.shape
    n_tok = b * s
    depth = w_ada.shape[0]
    assert s % ROW_TILE == 0 and s % ATT_BLOCK == 0 and GATHER_TILE == ROW_TILE
    assert (2 * n_tok) % LANES == 0 and d == SUBLANES * LANES

    c_pad = jnp.pad(c, ((0, (-b) % 8), (0, 0)))
    n_slots = 2 * n_tok + N_EXPERTS * EXPERT_BLOCK
    n_blk = n_slots // EXPERT_BLOCK
    n_blk_pad = -(-n_blk // LANES) * LANES

    for l in range(depth):
        lam_init = 0.8 - 0.6 * math.exp(-0.3 * l)
        mod = _adaln(c_pad, w_ada[l], b_ada[l][None, :])[:b].reshape(b, 1, 6 * d)

        (qda, kda, vda, qd, kd, ke, hi, sg, dec, gda, ghg, w1_bf, w3_bf, w2_bf) = _inproj(
            x, mod, norm1_g[l][None, :], w_in[l].astype(BF16), hg_lb_logits, l, w1[l], w3[l], w2[l])
        yda, xs_zero = _attention(qda, kda, vda, lambda_q1[l][None, :], lambda_k1[l][None, :],
                                  lambda_q2[l][None, :], lambda_k2[l][None, :], da_subln_g[l][None, :],
                                  lam_init, jax.ShapeDtypeStruct((n_slots // 2 * SUBLANES, LANES), U32))
        yhg = _hgrn(qd, kd, ke, hi, sg, dec, hg_norm_g[l][None, :])

        wr = jnp.concatenate([w_re[l].T, w_rg[l].T,
                              jnp.zeros((ROUTER_ROWS - N_EXPERTS - N_GROUPS, d), F32)], axis=0)
        wr_hi = wr.astype(BF16)
        wr_lo = (wr - wr_hi.astype(F32)).astype(BF16)
        br = jnp.concatenate([b_re[l], b_rg[l],
                              jnp.zeros((ROUTER_ROWS - N_EXPERTS - N_GROUPS,), F32)])[:, None]
        x1, h2, eidx, wts = _merge(yda, yhg, gda, ghg, x, mod, norm2_g[l][None, :],
                                   w_up_da[l].astype(BF16), w_up_hg[l].astype(BF16),
                                   w_out[l].astype(BF16), wr_hi, wr_lo, br)

        e_mat = jnp.transpose(eidx, (1, 0, 2)).reshape(2 * n_tok // LANES, LANES)
        half, blk_e, n_used = _positions(e_mat, n_blk_pad)
        half = half.reshape(2 * n_tok)
        xs = _dispatch(half, h2.reshape(b * h2.shape[1], LANES), xs_zero)
        ys = _experts(blk_e[0, :n_blk], n_used[0, :1], xs, w1_bf, w3_bf, w2_bf)
        wts_tok = jnp.transpose(wts, (0, 2, 1)).reshape(n_tok, 2)
        x = _combine(half, ys, x1.reshape(n_tok, d), mod, wts_tok, final_g[None, :], s,
                     last=(l == depth - 1)).reshape(b, s, d)
    return x
```

```python
import functools
import math

import jax
import jax.numpy as jnp
from jax import lax
from jax.experimental import pallas as pl
from jax.experimental.pallas import tpu as pltpu

F32 = jnp.float32
BF16 = jnp.bfloat16
I32 = jnp.int32
U32 = jnp.uint32

EPS = 1e-6
NEG = -1e30
LOG2E = math.log2(math.e)

CHUNK = 64
DA_HEADS = 4
DA_HEAD_DIM = 64
HG_HEADS = 4
HG_DK = 128
HG_DV = 128
N_GROUPS = 4
EXPERTS_PER_GROUP = 8
N_EXPERTS = N_GROUPS * EXPERTS_PER_GROUP

LANES = 128
SUBLANES = 8
ADALN_COLS = 1024
ROW_TILE = 512
ATT_BLOCK = 512
BIAS_TERMS = 3
EXPERT_BLOCK = 256
GATHER_TILE = 512
ISSUE_UNROLL = 8
ROUTER_ROWS = 48
VMEM_LIMIT = 52 * 1024 * 1024


def _nt(a, b):
    return lax.dot_general(a, b, (((1,), (1,)), ((), ())), preferred_element_type=F32)


def _nn(a, b):
    return jnp.dot(a, b, preferred_element_type=F32)


def _split(a):
    hi = a.astype(BF16)
    lo = (a - hi.astype(F32)).astype(BF16)
    return hi, lo


def _sigmoid(x):
    return 1.0 / (1.0 + jnp.exp(-x))


def _silu(x):
    return x * _sigmoid(x)


def _pack_rows(y):
    rows, cols = y.shape[0] // 2, y.shape[1] // 2
    bits = lax.bitcast_convert_type(y.astype(BF16).astype(F32), U32)
    words = bits[:, :cols] | lax.shift_right_logical(bits[:, cols:], jnp.uint32(16))
    return jnp.concatenate([words[:rows], words[rows:]], axis=1)


def _unpack_rows(packed):
    cols = packed.shape[1] // 2
    words = jnp.concatenate([packed[:, :cols], packed[:, cols:]], axis=0)
    first = lax.bitcast_convert_type(words & jnp.uint32(0xFFFF0000), F32)
    second = lax.bitcast_convert_type(lax.shift_left(words, jnp.uint32(16)), F32)
    return jnp.concatenate([first, second], axis=1)


def _adaln_kernel(c_ref, w_ref, b_ref, o_ref):
    c_hi, c_lo = _split(_silu(c_ref[...]))
    w_hi, w_lo = _split(w_ref[...])
    o_ref[...] = _nn(c_hi, w_hi) + _nn(c_lo, w_hi) + _nn(c_hi, w_lo) + b_ref[...]


def _adaln(c_pad, w, b):
    rows, d = c_pad.shape
    n = w.shape[1]
    tn = ADALN_COLS
    return pl.pallas_call(
        _adaln_kernel,
        grid=(n // tn,),
        in_specs=[
            pl.BlockSpec((rows, d), lambda j: (0, 0)),
            pl.BlockSpec((d, tn), lambda j: (0, j)),
            pl.BlockSpec((1, tn), lambda j: (0, j)),
        ],
        out_specs=pl.BlockSpec((rows, tn), lambda j: (0, j)),
        out_shape=jax.ShapeDtypeStruct((rows, n), F32),
        name="adaln",
    )(c_pad, w, b)


def _inproj_kernel(x_ref, sh_ref, sc_ref, g_ref, w_ref, lb_ref, w1_ref, w3_ref, w2_ref,
                   qda_ref, kda_ref, vda_ref, qd_ref, kd_ref, ke_ref, hi_ref, sg_ref,
                   dec_ref, gda_ref, ghg_ref, w1b_ref, w3b_ref, w2b_ref, *, layer):
    w1b_ref[...] = w1_ref[...].astype(BF16)
    w3b_ref[...] = w3_ref[...].astype(BF16)
    w2b_ref[...] = w2_ref[...].astype(BF16)

    x = x_ref[0]
    tm = x.shape[0]
    h = x * lax.rsqrt(jnp.mean(x * x, axis=-1, keepdims=True) + EPS) * g_ref[...]
    h = h * (1.0 + sc_ref[0]) + sh_ref[0]
    hb = h.astype(BF16)

    def proj(c0, width):
        return _nn(hb, w_ref[:, c0:c0 + width])

    w_da = DA_HEADS * 2 * DA_HEAD_DIM
    w_hg = HG_HEADS * HG_DK
    qda_ref[0] = (proj(0, w_da) * (LOG2E / math.sqrt(DA_HEAD_DIM))).astype(BF16)
    kda_ref[0] = proj(w_da, w_da).astype(BF16)
    vda_ref[0] = proj(2 * w_da, w_da).astype(BF16)
    c0 = 3 * w_da

    lbl = lb_ref[...]
    lbe = jnp.exp(lbl - jnp.max(lbl, axis=0, keepdims=True))
    lbs = lbe / jnp.sum(lbe, axis=0, keepdims=True)
    lb = jnp.sum(lbs[:layer + 1], axis=0, keepdims=True)

    f = lb + (1.0 - lb) * _sigmoid(proj(c0 + w_hg, w_hg))
    logf = jnp.log(f)
    kk = 1.0 - f
    pos = lax.broadcasted_iota(I32, logf.shape, 0) & (CHUNK - 1)
    a = logf
    step = 1
    while step < CHUNK:
        a = a + jnp.where(pos >= step, pltpu.roll(a, step, axis=0), 0.0)
        step *= 2
    n_ch = tm // CHUNK
    a = a.reshape(n_ch, CHUNK, w_hg)
    a_last = a[:, CHUNK - 1:CHUNK, :]
    kk3 = kk.reshape(n_ch, CHUNK, w_hg)
    qq3 = _silu(proj(c0, w_hg)).reshape(n_ch, CHUNK, w_hg)
    qd_ref[0] = (qq3 * jnp.exp(a)).reshape(tm, w_hg).astype(BF16)
    kd_ref[0] = (kk3 * jnp.exp(-a)).reshape(tm, w_hg).astype(BF16)
    ke_ref[0] = (kk3 * jnp.exp(a_last - a)).reshape(tm, w_hg).astype(BF16)
    dec_ref[0] = jnp.exp(a_last).reshape(n_ch, w_hg)
    hi_ref[0] = proj(c0 + 2 * w_hg, w_hg).astype(BF16)
    sg_ref[0] = _silu(proj(c0 + 3 * w_hg, w_hg)).astype(BF16)
    c1 = c0 + 4 * w_hg
    d = x.shape[1]
    gda_ref[0] = _sigmoid(proj(c1, d)).astype(BF16)
    ghg_ref[0] = _sigmoid(proj(c1 + d, d)).astype(BF16)


def _inproj(x, mod, g, w_bf, lb_logits, layer, w1, w3, w2):
    b, s, d = x.shape
    tm = ROW_TILE
    n_cols = w_bf.shape[1]
    per_b = s // tm
    assert w1.shape[0] % (b * per_b) == 0
    e_step = w1.shape[0] // (b * per_b)
    wspec = lambda a: pl.BlockSpec((e_step,) + a.shape[1:], lambda bi, i: (bi * per_b + i, 0, 0))
    w_da = DA_HEADS * 2 * DA_HEAD_DIM
    w_hg = HG_HEADS * HG_DK

    def tok(width, dtype):
        return (pl.BlockSpec((1, tm, width), lambda bi, i: (bi, i, 0)),
                jax.ShapeDtypeStruct((b, s, width), dtype))

    outs = [tok(w_da, BF16)] * 3 + [tok(w_hg, BF16)] * 5
    outs.append((pl.BlockSpec((1, tm // CHUNK, w_hg), lambda bi, i: (bi, i, 0)),
                 jax.ShapeDtypeStruct((b, s // CHUNK, w_hg), F32)))
    outs += [tok(d, BF16)] * 2
    outs += [(wspec(a), jax.ShapeDtypeStruct(a.shape, BF16)) for a in (w1, w3, w2)]
    return pl.pallas_call(
        functools.partial(_inproj_kernel, layer=layer),
        grid=(b, s // tm),
        in_specs=[
            pl.BlockSpec((1, tm, d), lambda bi, i: (bi, i, 0)),
            pl.BlockSpec((1, 1, d), lambda bi, i: (bi, 0, 0)),
            pl.BlockSpec((1, 1, d), lambda bi, i: (bi, 0, 1)),
            pl.BlockSpec((1, d), lambda bi, i: (0, 0)),
            pl.BlockSpec((d, n_cols), lambda bi, i: (0, 0), pipeline_mode=pl.Buffered(1)),
            pl.BlockSpec(lb_logits.shape, lambda bi, i: (0, 0)),
            wspec(w1), wspec(w3), wspec(w2),
        ],
        out_specs=[o[0] for o in outs],
        out_shape=[o[1] for o in outs],
        compiler_params=pltpu.CompilerParams(vmem_limit_bytes=VMEM_LIMIT),
        name="inproj",
    )(x, mod, mod, g, w_bf, lb_logits, w1, w3, w2)


def _attn_kernel(q_ref, k_ref, v_ref, lq1_ref, lk1_ref, lq2_ref, lk2_ref, g_ref, o_ref, zs_ref,
                 kaug_ref, vaug_ref, qaug_ref, m_ref, acc_ref, sa_ref, sb_ref, zero_ref, zsem,
                 *, lam_init, slopes):
    hd = pl.program_id(1)
    i = pl.program_id(2)

    step = (pl.program_id(0) * pl.num_programs(1) + hd) * pl.num_programs(2) + i
    n_steps = pl.num_programs(0) * pl.num_programs(1) * pl.num_programs(2)
    zrows = zero_ref.shape[0]

    def zero_copy(t):
        return pltpu.make_async_copy(zero_ref, zs_ref.at[pl.ds(t * zrows, zrows)], zsem)

    @pl.when(step == 0)
    def _():
        zero_ref[...] = jnp.zeros(zero_ref.shape, zero_ref.dtype)

    @pl.when(step > 0)
    def _():
        zero_copy(step - 1).wait()

    zero_copy(step).start()

    @pl.when(step == n_steps - 1)
    def _():
        zero_copy(step).wait()
    tq = q_ref.shape[1]
    hw = q_ref.shape[2]
    n_key = k_ref.shape[1]

    slope = jnp.float32(slopes[-1])
    for idx in range(len(slopes) - 2, -1, -1):
        slope = jnp.where(hd == idx, jnp.float32(slopes[idx]), slope)

    @pl.when(i == 0)
    def _():
        kaug_ref[:hw, :] = k_ref[0].astype(F32).T.astype(BF16)
        jrel = (lax.broadcasted_iota(I32, (hw, n_key), 1) & (tq - 1)).astype(F32) * slope
        krow = lax.broadcasted_iota(I32, (hw, n_key), 0)
        bias_rows = jnp.zeros((hw, n_key), F32)
        rest = jrel
        for term in range(BIAS_TERMS):
            part = rest.astype(BF16).astype(F32)
            bias_rows = jnp.where(krow == term, part, bias_rows)
            rest = rest - part
        kaug_ref[hw:, :] = bias_rows.astype(BF16)
        vaug_ref[:, :hw] = v_ref[0]
        vaug_ref[:, hw:] = jnp.ones((n_key, hw), BF16)

    q = q_ref[0]
    lane = lax.broadcasted_iota(I32, q.shape, 1)
    zero = jnp.zeros_like(q)
    ones = jnp.where(lane < BIAS_TERMS, 1.0, 0.0).astype(BF16)
    qaug_ref[:tq, :hw] = jnp.where(lane < DA_HEAD_DIM, q, zero)
    qaug_ref[tq:, :hw] = jnp.where(lane >= DA_HEAD_DIM, q, zero)
    qaug_ref[:tq, hw:] = ones
    qaug_ref[tq:, hw:] = ones

    m_ref[...] = jnp.full(m_ref.shape, NEG, F32)
    acc_ref[...] = jnp.zeros(acc_ref.shape, F32)

    def scores(j, buf):
        buf[...] = _nn(qaug_ref[...], kaug_ref[:, pl.ds(pl.multiple_of(j * tq, tq), tq)])

    def consume(j, buf, diagonal):
        s = buf[...]
        if diagonal:
            ii = lax.broadcasted_iota(I32, s.shape, 0) & (tq - 1)
            jj = lax.broadcasted_iota(I32, s.shape, 1)
            ahead = jnp.minimum(ii - jj, 0).astype(F32)
            s = s + jnp.where((jj // CHUNK) <= (ii // CHUNK), (2.0 * slope) * ahead, NEG)
            shift = jnp.float32(0.0)
        else:
            shift = -slope * ((i - j) * tq).astype(F32)
        m_old = m_ref[...]
        m_new = jnp.maximum(m_old, jnp.max(s, axis=-1, keepdims=True) + shift)
        p = jnp.exp2(s - jnp.concatenate([m_new - shift] * (tq // LANES), axis=1))
        alpha = jnp.concatenate([jnp.exp2(m_old - m_new)] * (acc_ref.shape[1] // LANES), axis=1)
        vb = vaug_ref[pl.ds(pl.multiple_of(j * tq, tq), tq), :]
        acc_ref[...] = alpha * acc_ref[...] + _nn(p.astype(BF16), vb)
        m_ref[...] = m_new

    scores(0, sa_ref)

    def pair(pi, carry):
        j = 2 * pi
        scores(j + 1, sb_ref)
        consume(j, sa_ref, False)
        scores(j + 2, sa_ref)
        consume(j + 1, sb_ref, False)
        return carry

    lax.fori_loop(0, i // 2, pair, 0)

    @pl.when(i % 2 == 0)
    def _():
        consume(i, sa_ref, True)

    @pl.when(i % 2 == 1)
    def _():
        scores(i, sb_ref)
        consume(i - 1, sa_ref, False)
        consume(i, sb_ref, True)

    acc = acc_ref[...]
    o1 = acc[:tq, :hw] / acc[:tq, hw:]
    o2 = acc[tq:, :hw] / acc[tq:, hw:]
    lam = (jnp.exp(jnp.sum(lq1_ref[...] * lk1_ref[...], axis=-1, keepdims=True))
           - jnp.exp(jnp.sum(lq2_ref[...] * lk2_ref[...], axis=-1, keepdims=True)) + lam_init)
    o = o1 - lam * o2
    o = o * lax.rsqrt(jnp.mean(o * o, axis=-1, keepdims=True) + EPS) * g_ref[...]
    o_ref[0] = (o * (1.0 - lam_init)).astype(BF16)


def _attention(q, k, v, lq1, lk1, lq2, lk2, subln_g, lam_init, zero_buf):
    b, s, _ = q.shape
    zero_shape = zero_buf.shape
    tq = ATT_BLOCK
    hw = 2 * DA_HEAD_DIM
    assert hw == LANES
    n_steps = b * DA_HEADS * (s // tq)
    assert zero_shape[0] % n_steps == 0
    zrows = zero_shape[0] // n_steps
    slopes = tuple(LOG2E * 2.0 ** (-8.0 * (h + 1) / DA_HEADS) for h in range(DA_HEADS))
    vec = lambda n: pl.BlockSpec((1, n), lambda bi, h, i: (0, 0))
    return pl.pallas_call(
        functools.partial(_attn_kernel, lam_init=lam_init, slopes=slopes),
        grid=(b, DA_HEADS, s // tq),
        in_specs=[
            pl.BlockSpec((1, tq, hw), lambda bi, h, i: (bi, i, h)),
            pl.BlockSpec((1, s, hw), lambda bi, h, i: (bi, 0, h)),
            pl.BlockSpec((1, s, hw), lambda bi, h, i: (bi, 0, h)),
            vec(DA_HEAD_DIM), vec(DA_HEAD_DIM), vec(DA_HEAD_DIM), vec(DA_HEAD_DIM), vec(hw),
        ],
        out_specs=[pl.BlockSpec((1, tq, hw), lambda bi, h, i: (bi, i, h)),
                   pl.BlockSpec(memory_space=pl.ANY)],
        out_shape=[jax.ShapeDtypeStruct((b, s, DA_HEADS * hw), BF16),
                   zero_buf],
        scratch_shapes=[
            pltpu.VMEM((2 * hw, s), BF16),
            pltpu.VMEM((s, 2 * hw), BF16),
            pltpu.VMEM((2 * tq, 2 * hw), BF16),
            pltpu.VMEM((2 * tq, LANES), F32),
            pltpu.VMEM((2 * tq, 2 * hw), F32),
            pltpu.VMEM((2 * tq, tq), F32),
            pltpu.VMEM((2 * tq, tq), F32),
            pltpu.VMEM((zrows,) + tuple(zero_shape[1:]), zero_buf.dtype),
            pltpu.SemaphoreType.DMA(()),
        ],
        compiler_params=pltpu.CompilerParams(vmem_limit_bytes=VMEM_LIMIT,
                                             dimension_semantics=("arbitrary",) * 3),
        name="attention",
    )(q, k, v, lq1, lk1, lq2, lk2, subln_g)


def _hgrn_kernel(qd_ref, kd_ref, ke_ref, v_ref, sg_ref, dec_ref, g_ref, o_ref, st_ref):
    @pl.when(pl.program_id(1) == 0)
    def _():
        st_ref[...] = jnp.zeros(st_ref.shape, F32)

    n_ch = qd_ref.shape[1] // CHUNK
    r = lax.broadcasted_iota(I32, (CHUNK, CHUNK), 0)
    c = lax.broadcasted_iota(I32, (CHUNK, CHUNK), 1)
    causal = r >= c

    heads = [slice(h * HG_DK, (h + 1) * HG_DK) for h in range(HG_HEADS)]
    chunks = [slice(ci * CHUNK, (ci + 1) * CHUNK) for ci in range(n_ch)]
    incr = [[_nn(v_ref[0, rows, cols].astype(F32).T.astype(BF16), ke_ref[0, rows, cols])
             for rows in chunks] for cols in heads]
    start = []
    for h, cols in enumerate(heads):
        st = st_ref[h]
        before = []
        for ci in range(n_ch):
            before.append(st.astype(BF16))
            st = st * dec_ref[0, ci:ci + 1, cols] + incr[h][ci]
        st_ref[h] = st
        start.append(before)
    for h, cols in enumerate(heads):
        for ci, rows in enumerate(chunks):
            qd = qd_ref[0, rows, cols]
            scores = jnp.where(causal, _nt(qd, kd_ref[0, rows, cols]), 0.0).astype(BF16)
            o = _nn(scores, v_ref[0, rows, cols]) + _nt(qd, start[h][ci])
            o = o * lax.rsqrt(jnp.mean(o * o, axis=-1, keepdims=True) + EPS) * g_ref[...]
            o_ref[0, rows, cols] = (o * sg_ref[0, rows, cols].astype(F32)).astype(BF16)


def _hgrn(qd, kd, ke, v, sg, dec, norm_g):
    b, s, w = qd.shape
    ts = ROW_TILE
    tok = pl.BlockSpec((1, ts, w), lambda bi, i: (bi, i, 0))
    return pl.pallas_call(
        _hgrn_kernel,
        grid=(b, s // ts),
        in_specs=[tok, tok, tok, tok, tok,
                  pl.BlockSpec((1, ts // CHUNK, w), lambda bi, i: (bi, i, 0)),
                  pl.BlockSpec((1, HG_DV), lambda bi, i: (0, 0))],
        out_specs=tok,
        out_shape=jax.ShapeDtypeStruct((b, s, w), BF16),
        scratch_shapes=[pltpu.VMEM((HG_HEADS, HG_DV, HG_DK), F32)],
        name="hgrn",
    )(qd, kd, ke, v, sg, dec, norm_g)


def _merge_kernel(yda_ref, yhg_ref, gda_ref, ghg_ref, x_ref, gt1_ref, sh2_ref, sc2_ref, g2_ref,
                  wda_ref, whg_ref, wout_ref, wrh_ref, wrl_ref, br_ref,
                  x1_ref, h2_ref, eidx_ref, wts_ref):
    u = (gda_ref[0].astype(F32) * _nn(yda_ref[0], wda_ref[...])
         + ghg_ref[0].astype(F32) * _nn(yhg_ref[0], whg_ref[...]))
    x1 = x_ref[0] + gt1_ref[0] * _nn(u.astype(BF16), wout_ref[...])
    x1_ref[0] = x1
    h2 = x1 * lax.rsqrt(jnp.mean(x1 * x1, axis=-1, keepdims=True) + EPS) * g2_ref[...]
    h2 = h2 * (1.0 + sc2_ref[0]) + sh2_ref[0]
    h2_ref[0] = _pack_rows(h2).reshape(h2.shape[0] // 2, SUBLANES, LANES).reshape(h2_ref.shape[1:])

    h_hi, h_lo = _split(h2)
    wrh = wrh_ref[...]
    both = _nt(jnp.concatenate([wrh, wrl_ref[...]], axis=0), h_hi)
    lg = both[:ROUTER_ROWS] + both[ROUTER_ROWS:] + _nt(wrh, h_lo) + br_ref[...]

    tm = h2.shape[0]
    eg = EXPERTS_PER_GROUP
    gl = lg[N_EXPERTS:N_EXPERTS + N_GROUPS]
    gmax = jnp.max(gl, axis=0, keepdims=True)
    g_p = 1.0 / jnp.sum(jnp.exp(gl - gmax), axis=0, keepdims=True)
    gi = lax.broadcasted_iota(I32, (N_GROUPS, tm), 0)
    g_idx = jnp.min(jnp.where(gl == gmax, gi, N_GROUPS), axis=0, keepdims=True)
    sel = lg[(N_GROUPS - 1) * eg:N_GROUPS * eg]
    for g in range(N_GROUPS - 2, -1, -1):
        sel = jnp.where(g_idx == g, lg[g * eg:(g + 1) * eg], sel)
    ei = lax.broadcasted_iota(I32, (eg, tm), 0)
    m1 = jnp.max(sel, axis=0, keepdims=True)
    i1 = jnp.min(jnp.where(sel == m1, ei, eg), axis=0, keepdims=True)
    sel2 = jnp.where(ei == i1, -jnp.inf, sel)
    m2 = jnp.max(sel2, axis=0, keepdims=True)
    i2 = jnp.min(jnp.where(sel2 == m2, ei, eg), axis=0, keepdims=True)
    t = jnp.exp(m2 - m1)
    wa = 1.0 / (1.0 + t)
    eidx_ref[0] = jnp.concatenate([g_idx * eg + i1, g_idx * eg + i2], axis=0)
    wts_ref[0] = jnp.concatenate([g_p * wa, g_p * (t * wa)], axis=0)


def _merge(yda, yhg, gda, ghg, x, mod, g2, wda, whg, wout, wrh, wrl, br):
    b, s, d = x.shape
    tm = ROW_TILE
    w = yda.shape[2]
    tok = lambda width: pl.BlockSpec((1, tm, width), lambda bi, i: (bi, i, 0))
    modspec = lambda k: pl.BlockSpec((1, 1, d), lambda bi, i: (bi, 0, k))
    full = lambda a: pl.BlockSpec(a.shape, lambda bi, i: (0,) * a.ndim)
    lane_rows = pl.BlockSpec((1, 2, tm), lambda bi, i: (bi, 0, i))
    return pl.pallas_call(
        _merge_kernel,
        grid=(b, s // tm),
        in_specs=[tok(w), tok(w), tok(d), tok(d), tok(d),
                  modspec(2), modspec(3), modspec(4), full(g2),
                  full(wda), full(whg), full(wout), full(wrh), full(wrl), full(br)],
        out_specs=[tok(d), pl.BlockSpec((1, tm // 2 * SUBLANES, LANES), lambda bi, i: (bi, i, 0)),
                   lane_rows, lane_rows],
        out_shape=[jax.ShapeDtypeStruct((b, s, d), F32),
                   jax.ShapeDtypeStruct((b, s // 2 * SUBLANES, LANES), U32),
                   jax.ShapeDtypeStruct((b, 2, s), I32), jax.ShapeDtypeStruct((b, 2, s), F32)],
        compiler_params=pltpu.CompilerParams(vmem_limit_bytes=VMEM_LIMIT),
        name="merge",
    )(yda, yhg, gda, ghg, x, mod, mod, mod, g2, wda, whg, wout, wrh, wrl, br)


def _positions_kernel(e_ref, half_ref, blk_ref, nused_ref):
    e_mat = e_ref[...]
    nr = e_mat.shape[0]
    nb = blk_ref.shape[1]
    r = lax.broadcasted_iota(I32, (LANES, LANES), 0)
    c = lax.broadcasted_iota(I32, (LANES, LANES), 1)
    incl = jnp.where(r <= c, 1.0, 0.0).astype(BF16)
    ones = jnp.ones((LANES, LANES), BF16)
    rr = lax.broadcasted_iota(I32, (nr, nr), 0)
    cc = lax.broadcasted_iota(I32, (nr, nr), 1)
    before = jnp.where(cc < rr, 1.0, 0.0).astype(BF16)
    blk_start = (lax.broadcasted_iota(I32, (1, nb), 1) * EXPERT_BLOCK).astype(F32)

    def per_expert(e, carry):
        pstart, dest, blk = carry
        hit = e_mat == e
        oh = jnp.where(hit, 1.0, 0.0).astype(BF16)
        in_row = _nn(oh, incl)
        row_tot = _nn(oh, ones)
        row_off = _nn(before, row_tot.astype(BF16))
        total = row_off[nr - 1:nr, :] + row_tot[nr - 1:nr, :]
        dest = jnp.where(hit, pstart + row_off + in_row - 1.0, dest)
        pend = pstart + jnp.floor((total + (EXPERT_BLOCK - 1)) * (1.0 / EXPERT_BLOCK)) * EXPERT_BLOCK
        blk = blk + jnp.where(blk_start >= pend[:, :1], 1.0, 0.0)
        return pend, dest, blk

    pstart0 = jnp.zeros((1, LANES), F32)
    pend, dest, blk = lax.fori_loop(
        0, N_EXPERTS, per_expert,
        (pstart0, jnp.zeros(e_mat.shape, F32), jnp.zeros((1, nb), F32)), unroll=4)
    half_rows = EXPERT_BLOCK // 2
    block = jnp.floor(dest * (1.0 / EXPERT_BLOCK))
    in_block = dest - block * EXPERT_BLOCK
    upper = jnp.floor(in_block * (1.0 / half_rows))
    tile = block * half_rows + in_block - upper * half_rows
    half_ref[...] = (SUBLANES * tile + (SUBLANES // 2) * upper).astype(I32)
    blk_ref[...] = jnp.minimum(blk, N_EXPERTS - 1).astype(I32)
    nused_ref[...] = (pend * (1.0 / EXPERT_BLOCK)).astype(I32)


def _positions(e_mat, n_blk_pad):
    nr = e_mat.shape[0]
    return pl.pallas_call(
        _positions_kernel,
        out_shape=[jax.ShapeDtypeStruct((nr, LANES), I32),
                   jax.ShapeDtypeStruct((1, n_blk_pad), I32),
                   jax.ShapeDtypeStruct((1, LANES), I32)],
        name="positions",
    )(e_mat)


def _dispatch_kernel(half_ref, h2_ref, xs_in_ref, xs_ref, hbuf, in_sem, out_sem, *, n_tok):
    del xs_in_ref
    i = pl.program_id(0)
    last = pl.num_programs(0) - 1
    slot = i % 2
    half = SUBLANES // 2
    pairs = GATHER_TILE // 2
    tile_rows = pairs * SUBLANES

    def load(step, buf):
        return pltpu.make_async_copy(h2_ref.at[pl.ds(step * tile_rows, tile_rows)], hbuf.at[buf],
                                     in_sem.at[buf])

    def drain(buf):
        for k in range(2):
            pltpu.make_async_copy(hbuf.at[buf], xs_ref.at[pl.ds(0, tile_rows)], out_sem.at[buf, k]).wait()

    @pl.when(i == 0)
    def _():
        load(0, 0).start()

    @pl.when(i > 0)
    def _():
        drain(1 - slot)

    @pl.when(i < last)
    def _():
        load(i + 1, 1 - slot).start()

    load(i, slot).wait()

    for up in range(2):
        base = i * GATHER_TILE + up * pairs

        def issue(r, carry):
            for k in range(2):
                row = pl.multiple_of(half_ref[k * n_tok + base + r], half)
                pltpu.make_async_copy(hbuf.at[slot, pl.ds(r * SUBLANES + up * half, half)],
                                      xs_ref.at[pl.ds(row, half)], out_sem.at[slot, k]).start(priority=k)
            return carry

        lax.fori_loop(0, pairs, issue, 0, unroll=ISSUE_UNROLL)

    @pl.when(i == last)
    def _():
        drain(slot)


def _dispatch(half, h2, xs_zero):
    n_tok = h2.shape[0] // (SUBLANES // 2)
    return pl.pallas_call(
        functools.partial(_dispatch_kernel, n_tok=n_tok),
        grid_spec=pltpu.PrefetchScalarGridSpec(
            num_scalar_prefetch=1,
            grid=(n_tok // GATHER_TILE,),
            in_specs=[pl.BlockSpec(memory_space=pl.ANY), pl.BlockSpec(memory_space=pl.ANY)],
            out_specs=pl.BlockSpec(memory_space=pl.ANY),
            scratch_shapes=[pltpu.VMEM((2, GATHER_TILE // 2 * SUBLANES, LANES), h2.dtype),
                            pltpu.SemaphoreType.DMA((2,)), pltpu.SemaphoreType.DMA((2, 2))],
        ),
        out_shape=jax.ShapeDtypeStruct(xs_zero.shape, xs_zero.dtype),
        input_output_aliases={2: 0},
        compiler_params=pltpu.CompilerParams(dimension_semantics=("arbitrary",)),
        name="dispatch",
    )(half, h2, xs_zero)


def _expert_kernel(blk_ref, nused_ref, xs_ref, w1_ref, w3_ref, w2_ref, ys_ref,
                   w1b, w3b, w2b, ybuf, cur, sem):
    i = pl.program_id(0)
    n_used = nused_ref[0]
    e = blk_ref[jnp.minimum(i, pl.num_programs(0) - 2)]
    fresh = (i == 0) | (blk_ref[jnp.maximum(i - 1, 0)] != e)

    def fetch(expert, which):
        return (pltpu.make_async_copy(w1_ref.at[expert], w1b.at[which], sem.at[0]),
                pltpu.make_async_copy(w3_ref.at[expert], w3b.at[which], sem.at[1]),
                pltpu.make_async_copy(w2_ref.at[expert], w2b.at[which], sem.at[2]))

    @pl.when(i < n_used)
    def _():
        @pl.when(i == 0)
        def _():
            cur[0] = 0
            for cp in fetch(e, 0):
                cp.start()

        @pl.when(fresh & (i > 0))
        def _():
            cur[0] = 1 - cur[0]

        @pl.when(fresh)
        def _():
            for cp in fetch(e, cur[0]):
                cp.wait()
            nxt = lax.while_loop(lambda j: (j < n_used) & (blk_ref[jnp.minimum(j, n_used - 1)] == e),
                                 lambda j: j + 1, i + 1)

            @pl.when(nxt < n_used)
            def _():
                for cp in fetch(blk_ref[jnp.minimum(nxt, n_used - 1)], 1 - cur[0]):
                    cp.start(priority=1)

    tiles = (xs_ref.shape[0] // SUBLANES, SUBLANES, LANES)

    def compute():
        which = cur[0]
        x = _unpack_rows(xs_ref[...].reshape(tiles).reshape(tiles[0], w1b.shape[1])).astype(BF16)
        a = _nn(x, w1b[which])
        ybuf[...] = _nn((_silu(a) * _nn(x, w3b[which])).astype(BF16), w2b[which])

    def flush():
        ys_ref[...] = _pack_rows(ybuf[...]).reshape(tiles).reshape(ys_ref.shape)

    @pl.when(i == 0)
    def _():
        compute()

    @pl.when((i > 0) & (i < n_used))
    def _():
        flush()
        compute()

    @pl.when((i > 0) & (i == n_used))
    def _():
        flush()

    @pl.when(i > n_used)
    def _():
        ys_ref[...] = jnp.zeros(ys_ref.shape, ys_ref.dtype)


def _experts(blk_e, n_used, xs, w1, w3, w2):
    block_rows = EXPERT_BLOCK // 2 * SUBLANES
    d, de = w1.shape[1], w1.shape[2]
    rows = lambda i, blk, nu: (jnp.minimum(i, nu[0] - 1), 0)
    hbm = pl.BlockSpec(memory_space=pl.ANY)
    return pl.pallas_call(
        _expert_kernel,
        grid_spec=pltpu.PrefetchScalarGridSpec(
            num_scalar_prefetch=2,
            grid=(xs.shape[0] // block_rows + 1,),
            in_specs=[pl.BlockSpec((block_rows, LANES), rows), hbm, hbm, hbm],
            out_specs=pl.BlockSpec((block_rows, LANES), lambda i, blk, nu: (jnp.maximum(i - 1, 0), 0)),
            scratch_shapes=[pltpu.VMEM((2, d, de), BF16), pltpu.VMEM((2, d, de), BF16),
                            pltpu.VMEM((2, de, d), BF16), pltpu.VMEM((EXPERT_BLOCK, d), F32),
                            pltpu.SMEM((1,), I32), pltpu.SemaphoreType.DMA((3,))],
        ),
        out_shape=jax.ShapeDtypeStruct(xs.shape, xs.dtype),
        compiler_params=pltpu.CompilerParams(vmem_limit_bytes=VMEM_LIMIT,
                                             dimension_semantics=("arbitrary",)),
        name="experts",
    )(blk_e, n_used, xs, w1, w3, w2)


def _combine_kernel(half_ref, ys_ref, x1_ref, gt2_ref, wts_ref, g_ref, o_ref, ybuf, sem, *, n_tok, last):
    i = pl.program_id(0)
    slot = i % 2
    half = SUBLANES // 2
    pairs = GATHER_TILE // 2

    def gather(step, buf):
        for up in range(2):
            base = step * GATHER_TILE + up * pairs

            def issue(r, carry):
                for k in range(2):
                    row = pl.multiple_of(half_ref[k * n_tok + base + r], half)
                    pltpu.make_async_copy(ys_ref.at[pl.ds(row, half)],
                                          ybuf.at[buf, k, pl.ds(r * SUBLANES + up * half, half)],
                                          sem.at[buf, k]).start(priority=k)
                return carry

            lax.fori_loop(0, pairs, issue, 0, unroll=ISSUE_UNROLL)

    @pl.when(i == 0)
    def _():
        gather(0, 0)

    @pl.when(i + 1 < pl.num_programs(0))
    def _():
        gather(i + 1, 1 - slot)

    for k in range(2):
        pltpu.make_async_copy(ys_ref.at[pl.ds(0, pairs * SUBLANES)], ybuf.at[slot, k], sem.at[slot, k]).wait()

    def tokens(k):
        tiles = ybuf[slot, k].reshape(pairs, SUBLANES, LANES)
        return _unpack_rows(tiles.reshape(pairs, x1_ref.shape[1]))

    wts = wts_ref[...]
    ya = tokens(0)
    yb = tokens(1)
    x = x1_ref[...] + gt2_ref[0] * (wts[:, 0:1] * ya + wts[:, 1:2] * yb)
    if last:
        x = x * lax.rsqrt(jnp.mean(x * x, axis=-1, keepdims=True) + EPS) * g_ref[...]
    o_ref[...] = x


def _combine(half, ys, x1, mod, wts, final_g, seq, last=True):
    n_tok, d = x1.shape
    tt = GATHER_TILE
    per_batch = seq // tt
    return pl.pallas_call(
        functools.partial(_combine_kernel, n_tok=n_tok, last=last),
        grid_spec=pltpu.PrefetchScalarGridSpec(
            num_scalar_prefetch=1,
            grid=(n_tok // tt,),
            in_specs=[pl.BlockSpec(memory_space=pl.ANY),
                      pl.BlockSpec((tt, d), lambda i, half: (i, 0)),
                      pl.BlockSpec((1, 1, d), lambda i, half: (i // per_batch, 0, 5)),
                      pl.BlockSpec((tt, 2), lambda i, half: (i, 0)),
                      pl.BlockSpec((1, d), lambda i, half: (0, 0))],
            out_specs=pl.BlockSpec((tt, d), lambda i, half: (i, 0)),
            scratch_shapes=[pltpu.VMEM((2, 2, tt // 2 * SUBLANES, LANES), U32),
                            pltpu.SemaphoreType.DMA((2, 2))],
        ),
        out_shape=jax.ShapeDtypeStruct((n_tok, d), F32),
        compiler_params=pltpu.CompilerParams(dimension_semantics=("arbitrary",)),
        name="combine",
    )(half, ys, x1, mod, wts, final_g)


def kernel(x, c, w_ada, b_ada, norm1_g, w_in, lambda_q1, lambda_k1, lambda_q2, lambda_k2, da_subln_g, hg_lb_logits, hg_norm_g, w_up_da, w_up_hg, w_out, norm2_g, w_rg, b_rg, w_re, b_re, w1, w3, w2, final_g):
    b, s, d = x.shape
    n_tok = b * s
    depth = w_ada.shape[0]
    assert s % ROW_TILE == 0 and s % ATT_BLOCK == 0 and GATHER_TILE == ROW_TILE
    assert (2 * n_tok) % LANES == 0 and d == SUBLANES * LANES

    c_pad = jnp.pad(c, ((0, (-b) % 8), (0, 0)))
    n_slots = 2 * n_tok + N_EXPERTS * EXPERT_BLOCK
    n_blk = n_slots // EXPERT_BLOCK
    n_blk_pad = -(-n_blk // LANES) * LANES

    for l in range(depth):
        lam_init = 0.8 - 0.6 * math.exp(-0.3 * l)
        mod = _adaln(c_pad, w_ada[l], b_ada[l][None, :])[:b].reshape(b, 1, 6 * d)

        (qda, kda, vda, qd, kd, ke, hi, sg, dec, gda, ghg, w1_bf, w3_bf, w2_bf) = _inproj(
            x, mod, norm1_g[l][None, :], w_in[l].astype(BF16), hg_lb_logits, l, w1[l], w3[l], w2[l])
        yda, xs_zero = _attention(qda, kda, vda, lambda_q1[l][None, :], lambda_k1[l][None, :],
                                  lambda_q2[l][None, :], lambda_k2[l][None, :], da_subln_g[l][None, :],
                                  lam_init, jax.ShapeDtypeStruct((n_slots // 2 * SUBLANES, LANES), U32))
        yhg = _hgrn(qd, kd, ke, hi, sg, dec, hg_norm_g[l][None, :])

        wr = jnp.concatenate([w_re[l].T, w_rg[l].T,
                              jnp.zeros((ROUTER_ROWS - N_EXPERTS - N_GROUPS, d), F32)], axis=0)
        wr_hi = wr.astype(BF16)
        wr_lo = (wr - wr_hi.astype(F32)).astype(BF16)
        br = jnp.concatenate([b_re[l], b_rg[l],
                              jnp.zeros((ROUTER_ROWS - N_EXPERTS - N_GROUPS,), F32)])[:, None]
        x1, h2, eidx, wts = _merge(yda, yhg, gda, ghg, x, mod, norm2_g[l][None, :],
                                   w_up_da[l].astype(BF16), w_up_hg[l].astype(BF16),
                                   w_out[l].astype(BF16), wr_hi, wr_lo, br)

        e_mat = jnp.transpose(eidx, (1, 0, 2)).reshape(2 * n_tok // LANES, LANES)
        half, blk_e, n_used = _positions(e_mat, n_blk_pad)
        half = half.reshape(2 * n_tok)
        xs = _dispatch(half, h2.reshape(b * h2.shape[1], LANES), xs_zero)
        ys = _experts(blk_e[0, :n_blk], n_used[0, :1], xs, w1_bf, w3_bf, w2_bf)
        wts_tok = jnp.transpose(wts, (0, 2, 1)).reshape(n_tok, 2)
        x = _combine(half, ys, x1.reshape(n_tok, d), mod, wts_tok, final_g[None, :], s,
                     last=(l == depth - 1)).reshape(b, s, d)
    return x
```

```python
import functools
import math

import jax
import jax.numpy as jnp
from jax import lax
from jax.experimental import pallas as pl
from jax.experimental.pallas import tpu as pltpu

F32 = jnp.float32
BF16 = jnp.bfloat16
I32 = jnp.int32
U32 = jnp.uint32

EPS = 1e-6
NEG = -1e30
LOG2E = math.log2(math.e)

CHUNK = 64
DA_HEADS = 4
DA_HEAD_DIM = 64
HG_HEADS = 4
HG_DK = 128
HG_DV = 128
N_GROUPS = 4
EXPERTS_PER_GROUP = 8
N_EXPERTS = N_GROUPS * EXPERTS_PER_GROUP

LANES = 128
SUBLANES = 8
ADALN_COLS = 1024
ROW_TILE = 512
ATT_BLOCK = 512
BIAS_TERMS = 3
EXPERT_BLOCK = 256
GATHER_TILE = 512
ISSUE_UNROLL = 8
COMBINE_UNROLL = 4
ROUTER_ROWS = 48
VMEM_LIMIT = 52 * 1024 * 1024


def _nt(a, b):
    return lax.dot_general(a, b, (((1,), (1,)), ((), ())), preferred_element_type=F32)


def _nn(a, b):
    return jnp.dot(a, b, preferred_element_type=F32)


def _split(a):
    hi = a.astype(BF16)
    lo = (a - hi.astype(F32)).astype(BF16)
    return hi, lo


def _sigmoid(x):
    return 1.0 / (1.0 + jnp.exp(-x))


def _silu(x):
    return x * _sigmoid(x)


def _pack_rows(y):
    rows, cols = y.shape[0] // 2, y.shape[1] // 2
    bits = lax.bitcast_convert_type(y.astype(BF16).astype(F32), U32)
    words = bits[:, :cols] | lax.shift_right_logical(bits[:, cols:], jnp.uint32(16))
    return jnp.concatenate([words[:rows], words[rows:]], axis=1)


def _unpack_rows(packed):
    cols = packed.shape[1] // 2
    words = jnp.concatenate([packed[:, :cols], packed[:, cols:]], axis=0)
    first = lax.bitcast_convert_type(words & jnp.uint32(0xFFFF0000), F32)
    second = lax.bitcast_convert_type(lax.shift_left(words, jnp.uint32(16)), F32)
    return jnp.concatenate([first, second], axis=1)


def _adaln_kernel(c_ref, w_ref, b_ref, o_ref):
    c_hi, c_lo = _split(_silu(c_ref[...]))
    w_hi, w_lo = _split(w_ref[...])
    o_ref[...] = _nn(c_hi, w_hi) + _nn(c_lo, w_hi) + _nn(c_hi, w_lo) + b_ref[...]


def _adaln(c_pad, w, b):
    rows, d = c_pad.shape
    n = w.shape[1]
    tn = ADALN_COLS
    return pl.pallas_call(
        _adaln_kernel,
        grid=(n // tn,),
        in_specs=[
            pl.BlockSpec((rows, d), lambda j: (0, 0)),
            pl.BlockSpec((d, tn), lambda j: (0, j)),
            pl.BlockSpec((1, tn), lambda j: (0, j)),
        ],
        out_specs=pl.BlockSpec((rows, tn), lambda j: (0, j)),
        out_shape=jax.ShapeDtypeStruct((rows, n), F32),
        name="adaln",
    )(c_pad, w, b)


def _inproj_kernel(x_ref, sh_ref, sc_ref, g_ref, w_ref, lb_ref, w1_ref, w3_ref, w2_ref,
                   qda_ref, kda_ref, vda_ref, qd_ref, kd_ref, ke_ref, hi_ref, sg_ref,
                   dec_ref, gda_ref, ghg_ref, w1b_ref, w3b_ref, w2b_ref, *, layer):
    w1b_ref[...] = w1_ref[...].astype(BF16)
    w3b_ref[...] = w3_ref[...].astype(BF16)
    w2b_ref[...] = w2_ref[...].astype(BF16)

    x = x_ref[0]
    tm = x.shape[0]
    h = x * lax.rsqrt(jnp.mean(x * x, axis=-1, keepdims=True) + EPS) * g_ref[...]
    h = h * (1.0 + sc_ref[0]) + sh_ref[0]
    hb = h.astype(BF16)

    def proj(c0, width):
        return _nn(hb, w_ref[:, c0:c0 + width])

    w_da = DA_HEADS * 2 * DA_HEAD_DIM
    w_hg = HG_HEADS * HG_DK
    qda_ref[0] = (proj(0, w_da) * (LOG2E / math.sqrt(DA_HEAD_DIM))).astype(BF16)
    kda_ref[0] = proj(w_da, w_da).astype(BF16)
    vda_ref[0] = proj(2 * w_da, w_da).astype(BF16)
    c0 = 3 * w_da

    lbl = lb_ref[...]
    lbe = jnp.exp(lbl - jnp.max(lbl, axis=0, keepdims=True))
    lbs = lbe / jnp.sum(lbe, axis=0, keepdims=True)
    lb = jnp.sum(lbs[:layer + 1], axis=0, keepdims=True)

    f = lb + (1.0 - lb) * _sigmoid(proj(c0 + w_hg, w_hg))
    logf = jnp.log(f)
    kk = 1.0 - f
    pos = lax.broadcasted_iota(I32, logf.shape, 0) & (CHUNK - 1)
    a = logf
    step = 1
    while step < CHUNK:
        a = a + jnp.where(pos >= step, pltpu.roll(a, step, axis=0), 0.0)
        step *= 2
    n_ch = tm // CHUNK
    a = a.reshape(n_ch, CHUNK, w_hg)
    a_last = a[:, CHUNK - 1:CHUNK, :]
    kk3 = kk.reshape(n_ch, CHUNK, w_hg)
    qq3 = _silu(proj(c0, w_hg)).reshape(n_ch, CHUNK, w_hg)
    qd_ref[0] = (qq3 * jnp.exp(a)).reshape(tm, w_hg).astype(BF16)
    kd_ref[0] = (kk3 * jnp.exp(-a)).reshape(tm, w_hg).astype(BF16)
    ke_ref[0] = (kk3 * jnp.exp(a_last - a)).reshape(tm, w_hg).astype(BF16)
    dec_ref[0] = jnp.exp(a_last).reshape(n_ch, w_hg)
    hi_ref[0] = proj(c0 + 2 * w_hg, w_hg).astype(BF16)
    sg_ref[0] = _silu(proj(c0 + 3 * w_hg, w_hg)).astype(BF16)
    c1 = c0 + 4 * w_hg
    d = x.shape[1]
    gda_ref[0] = _sigmoid(proj(c1, d)).astype(BF16)
    ghg_ref[0] = _sigmoid(proj(c1 + d, d)).astype(BF16)


def _inproj(x, mod, g, w_bf, lb_logits, layer, w1, w3, w2):
    b, s, d = x.shape
    tm = ROW_TILE
    n_cols = w_bf.shape[1]
    per_b = s // tm
    assert w1.shape[0] % (b * per_b) == 0
    e_step = w1.shape[0] // (b * per_b)
    wspec = lambda a: pl.BlockSpec((e_step,) + a.shape[1:], lambda bi, i: (bi * per_b + i, 0, 0))
    w_da = DA_HEADS * 2 * DA_HEAD_DIM
    w_hg = HG_HEADS * HG_DK

    def tok(width, dtype):
        return (pl.BlockSpec((1, tm, width), lambda bi, i: (bi, i, 0)),
                jax.ShapeDtypeStruct((b, s, width), dtype))

    outs = [tok(w_da, BF16)] * 3 + [tok(w_hg, BF16)] * 5
    outs.append((pl.BlockSpec((1, tm // CHUNK, w_hg), lambda bi, i: (bi, i, 0)),
                 jax.ShapeDtypeStruct((b, s // CHUNK, w_hg), F32)))
    outs += [tok(d, BF16)] * 2
    outs += [(wspec(a), jax.ShapeDtypeStruct(a.shape, BF16)) for a in (w1, w3, w2)]
    return pl.pallas_call(
        functools.partial(_inproj_kernel, layer=layer),
        grid=(b, s // tm),
        in_specs=[
            pl.BlockSpec((1, tm, d), lambda bi, i: (bi, i, 0)),
            pl.BlockSpec((1, 1, d), lambda bi, i: (bi, 0, 0)),
            pl.BlockSpec((1, 1, d), lambda bi, i: (bi, 0, 1)),
            pl.BlockSpec((1, d), lambda bi, i: (0, 0)),
            pl.BlockSpec((d, n_cols), lambda bi, i: (0, 0), pipeline_mode=pl.Buffered(1)),
            pl.BlockSpec(lb_logits.shape, lambda bi, i: (0, 0)),
            wspec(w1), wspec(w3), wspec(w2),
        ],
        out_specs=[o[0] for o in outs],
        out_shape=[o[1] for o in outs],
        compiler_params=pltpu.CompilerParams(vmem_limit_bytes=VMEM_LIMIT),
        name="inproj",
    )(x, mod, mod, g, w_bf, lb_logits, w1, w3, w2)


def _attn_kernel(q_ref, k_ref, v_ref, lq1_ref, lk1_ref, lq2_ref, lk2_ref, g_ref, o_ref, zs_ref,
                 kaug_ref, vaug_ref, qaug_ref, m_ref, acc_ref, sa_ref, sb_ref, zero_ref, zsem,
                 *, lam_init, slopes):
    hd = pl.program_id(1)
    i = pl.program_id(2)

    step = (pl.program_id(0) * pl.num_programs(1) + hd) * pl.num_programs(2) + i
    n_steps = pl.num_programs(0) * pl.num_programs(1) * pl.num_programs(2)
    zrows = zero_ref.shape[0]

    def zero_copy(t):
        return pltpu.make_async_copy(zero_ref, zs_ref.at[pl.ds(t * zrows, zrows)], zsem)

    @pl.when(step == 0)
    def _():
        zero_ref[...] = jnp.zeros(zero_ref.shape, zero_ref.dtype)

    @pl.when(step > 0)
    def _():
        zero_copy(step - 1).wait()

    zero_copy(step).start()

    @pl.when(step == n_steps - 1)
    def _():
        zero_copy(step).wait()
    tq = q_ref.shape[1]
    hw = q_ref.shape[2]
    n_key = k_ref.shape[1]

    slope = jnp.float32(slopes[-1])
    for idx in range(len(slopes) - 2, -1, -1):
        slope = jnp.where(hd == idx, jnp.float32(slopes[idx]), slope)

    @pl.when(i == 0)
    def _():
        kaug_ref[:hw, :] = k_ref[0].astype(F32).T.astype(BF16)
        jrel = (lax.broadcasted_iota(I32, (hw, n_key), 1) & (tq - 1)).astype(F32) * slope
        krow = lax.broadcasted_iota(I32, (hw, n_key), 0)
        bias_rows = jnp.zeros((hw, n_key), F32)
        rest = jrel
        for term in range(BIAS_TERMS):
            part = rest.astype(BF16).astype(F32)
            bias_rows = jnp.where(krow == term, part, bias_rows)
            rest = rest - part
        kaug_ref[hw:, :] = bias_rows.astype(BF16)
        vaug_ref[:, :hw] = v_ref[0]
        vaug_ref[:, hw:] = jnp.ones((n_key, hw), BF16)

    q = q_ref[0]
    lane = lax.broadcasted_iota(I32, q.shape, 1)
    zero = jnp.zeros_like(q)
    ones = jnp.where(lane < BIAS_TERMS, 1.0, 0.0).astype(BF16)
    qaug_ref[:tq, :hw] = jnp.where(lane < DA_HEAD_DIM, q, zero)
    qaug_ref[tq:, :hw] = jnp.where(lane >= DA_HEAD_DIM, q, zero)
    qaug_ref[:tq, hw:] = ones
    qaug_ref[tq:, hw:] = ones

    m_ref[...] = jnp.full(m_ref.shape, NEG, F32)
    acc_ref[...] = jnp.zeros(acc_ref.shape, F32)

    def scores(j, buf):
        buf[...] = _nn(qaug_ref[...], kaug_ref[:, pl.ds(pl.multiple_of(j * tq, tq), tq)])

    def consume(j, buf, diagonal):
        s = buf[...]
        if diagonal:
            ii = lax.broadcasted_iota(I32, s.shape, 0) & (tq - 1)
            jj = lax.broadcasted_iota(I32, s.shape, 1)
            ahead = jnp.minimum(ii - jj, 0).astype(F32)
            s = s + jnp.where((jj // CHUNK) <= (ii // CHUNK), (2.0 * slope) * ahead, NEG)
            shift = jnp.float32(0.0)
        else:
            shift = -slope * ((i - j) * tq).astype(F32)
        m_old = m_ref[...]
        m_new = jnp.maximum(m_old, jnp.max(s, axis=-1, keepdims=True) + shift)
        p = jnp.exp2(s - jnp.concatenate([m_new - shift] * (tq // LANES), axis=1))
        alpha = jnp.concatenate([jnp.exp2(m_old - m_new)] * (acc_ref.shape[1] // LANES), axis=1)
        vb = vaug_ref[pl.ds(pl.multiple_of(j * tq, tq), tq), :]
        acc_ref[...] = alpha * acc_ref[...] + _nn(p.astype(BF16), vb)
        m_ref[...] = m_new

    scores(0, sa_ref)

    def pair(pi, carry):
        j = 2 * pi
        scores(j + 1, sb_ref)
        consume(j, sa_ref, False)
        scores(j + 2, sa_ref)
        consume(j + 1, sb_ref, False)
        return carry

    lax.fori_loop(0, i // 2, pair, 0)

    @pl.when(i % 2 == 0)
    def _():
        consume(i, sa_ref, True)

    @pl.when(i % 2 == 1)
    def _():
        scores(i, sb_ref)
        consume(i - 1, sa_ref, False)
        consume(i, sb_ref, True)

    acc = acc_ref[...]
    o1 = acc[:tq, :hw] / acc[:tq, hw:]
    o2 = acc[tq:, :hw] / acc[tq:, hw:]
    lam = (jnp.exp(jnp.sum(lq1_ref[...] * lk1_ref[...], axis=-1, keepdims=True))
           - jnp.exp(jnp.sum(lq2_ref[...] * lk2_ref[...], axis=-1, keepdims=True)) + lam_init)
    o = o1 - lam * o2
    o = o * lax.rsqrt(jnp.mean(o * o, axis=-1, keepdims=True) + EPS) * g_ref[...]
    o_ref[0] = (o * (1.0 - lam_init)).astype(BF16)


def _attention(q, k, v, lq1, lk1, lq2, lk2, subln_g, lam_init, zero_buf):
    b, s, _ = q.shape
    zero_shape = zero_buf.shape
    tq = ATT_BLOCK
    hw = 2 * DA_HEAD_DIM
    assert hw == LANES
    n_steps = b * DA_HEADS * (s // tq)
    assert zero_shape[0] % n_steps == 0
    zrows = zero_shape[0] // n_steps
    slopes = tuple(LOG2E * 2.0 ** (-8.0 * (h + 1) / DA_HEADS) for h in range(DA_HEADS))
    vec = lambda n: pl.BlockSpec((1, n), lambda bi, h, i: (0, 0))
    return pl.pallas_call(
        functools.partial(_attn_kernel, lam_init=lam_init, slopes=slopes),
        grid=(b, DA_HEADS, s // tq),
        in_specs=[
            pl.BlockSpec((1, tq, hw), lambda bi, h, i: (bi, i, h)),
            pl.BlockSpec((1, s, hw), lambda bi, h, i: (bi, 0, h)),
            pl.BlockSpec((1, s, hw), lambda bi, h, i: (bi, 0, h)),
            vec(DA_HEAD_DIM), vec(DA_HEAD_DIM), vec(DA_HEAD_DIM), vec(DA_HEAD_DIM), vec(hw),
        ],
        out_specs=[pl.BlockSpec((1, tq, hw), lambda bi, h, i: (bi, i, h)),
                   pl.BlockSpec(memory_space=pl.ANY)],
        out_shape=[jax.ShapeDtypeStruct((b, s, DA_HEADS * hw), BF16),
                   zero_buf],
        scratch_shapes=[
            pltpu.VMEM((2 * hw, s), BF16),
            pltpu.VMEM((s, 2 * hw), BF16),
            pltpu.VMEM((2 * tq, 2 * hw), BF16),
            pltpu.VMEM((2 * tq, LANES), F32),
            pltpu.VMEM((2 * tq, 2 * hw), F32),
            pltpu.VMEM((2 * tq, tq), F32),
            pltpu.VMEM((2 * tq, tq), F32),
            pltpu.VMEM((zrows,) + tuple(zero_shape[1:]), zero_buf.dtype),
            pltpu.SemaphoreType.DMA(()),
        ],
        compiler_params=pltpu.CompilerParams(vmem_limit_bytes=VMEM_LIMIT,
                                             dimension_semantics=("arbitrary",) * 3),
        name="attention",
    )(q, k, v, lq1, lk1, lq2, lk2, subln_g)


def _hgrn_kernel(qd_ref, kd_ref, ke_ref, v_ref, sg_ref, dec_ref, g_ref, o_ref, st_ref):
    @pl.when(pl.program_id(1) == 0)
    def _():
        st_ref[...] = jnp.zeros(st_ref.shape, F32)

    n_ch = qd_ref.shape[1] // CHUNK
    r = lax.broadcasted_iota(I32, (CHUNK, CHUNK), 0)
    c = lax.broadcasted_iota(I32, (CHUNK, CHUNK), 1)
    causal = r >= c

    heads = [slice(h * HG_DK, (h + 1) * HG_DK) for h in range(HG_HEADS)]
    chunks = [slice(ci * CHUNK, (ci + 1) * CHUNK) for ci in range(n_ch)]
    incr = [[_nn(v_ref[0, rows, cols].astype(F32).T.astype(BF16), ke_ref[0, rows, cols])
             for rows in chunks] for cols in heads]
    start = []
    for h, cols in enumerate(heads):
        st = st_ref[h]
        before = []
        for ci in range(n_ch):
            before.append(st.astype(BF16))
            st = st * dec_ref[0, ci:ci + 1, cols] + incr[h][ci]
        st_ref[h] = st
        start.append(before)
    for h, cols in enumerate(heads):
        for ci, rows in enumerate(chunks):
            qd = qd_ref[0, rows, cols]
            scores = jnp.where(causal, _nt(qd, kd_ref[0, rows, cols]), 0.0).astype(BF16)
            o = _nn(scores, v_ref[0, rows, cols]) + _nt(qd, start[h][ci])
            o = o * lax.rsqrt(jnp.mean(o * o, axis=-1, keepdims=True) + EPS) * g_ref[...]
            o_ref[0, rows, cols] = (o * sg_ref[0, rows, cols].astype(F32)).astype(BF16)


def _hgrn(qd, kd, ke, v, sg, dec, norm_g):
    b, s, w = qd.shape
    ts = ROW_TILE
    tok = pl.BlockSpec((1, ts, w), lambda bi, i: (bi, i, 0))
    return pl.pallas_call(
        _hgrn_kernel,
        grid=(b, s // ts),
        in_specs=[tok, tok, tok, tok, tok,
                  pl.BlockSpec((1, ts // CHUNK, w), lambda bi, i: (bi, i, 0)),
                  pl.BlockSpec((1, HG_DV), lambda bi, i: (0, 0))],
        out_specs=tok,
        out_shape=jax.ShapeDtypeStruct((b, s, w), BF16),
        scratch_shapes=[pltpu.VMEM((HG_HEADS, HG_DV, HG_DK), F32)],
        name="hgrn",
    )(qd, kd, ke, v, sg, dec, norm_g)


def _merge_kernel(yda_ref, yhg_ref, gda_ref, ghg_ref, x_ref, gt1_ref, sh2_ref, sc2_ref, g2_ref,
                  wda_ref, whg_ref, wout_ref, wrh_ref, wrl_ref, br_ref,
                  x1_ref, h2_ref, eidx_ref, wts_ref):
    u = (gda_ref[0].astype(F32) * _nn(yda_ref[0], wda_ref[...])
         + ghg_ref[0].astype(F32) * _nn(yhg_ref[0], whg_ref[...]))
    x1 = x_ref[0] + gt1_ref[0] * _nn(u.astype(BF16), wout_ref[...])
    x1_ref[0] = x1
    h2 = x1 * lax.rsqrt(jnp.mean(x1 * x1, axis=-1, keepdims=True) + EPS) * g2_ref[...]
    h2 = h2 * (1.0 + sc2_ref[0]) + sh2_ref[0]
    h2_ref[0] = _pack_rows(h2).reshape(h2.shape[0] // 2, SUBLANES, LANES).reshape(h2_ref.shape[1:])

    h_hi, h_lo = _split(h2)
    wrh = wrh_ref[...]
    both = _nt(jnp.concatenate([wrh, wrl_ref[...]], axis=0), h_hi)
    lg = both[:ROUTER_ROWS] + both[ROUTER_ROWS:] + _nt(wrh, h_lo) + br_ref[...]

    tm = h2.shape[0]
    eg = EXPERTS_PER_GROUP
    gl = lg[N_EXPERTS:N_EXPERTS + N_GROUPS]
    gmax = jnp.max(gl, axis=0, keepdims=True)
    g_p = 1.0 / jnp.sum(jnp.exp(gl - gmax), axis=0, keepdims=True)
    gi = lax.broadcasted_iota(I32, (N_GROUPS, tm), 0)
    g_idx = jnp.min(jnp.where(gl == gmax, gi, N_GROUPS), axis=0, keepdims=True)
    sel = lg[(N_GROUPS - 1) * eg:N_GROUPS * eg]
    for g in range(N_GROUPS - 2, -1, -1):
        sel = jnp.where(g_idx == g, lg[g * eg:(g + 1) * eg], sel)
    ei = lax.broadcasted_iota(I32, (eg, tm), 0)
    m1 = jnp.max(sel, axis=0, keepdims=True)
    i1 = jnp.min(jnp.where(sel == m1, ei, eg), axis=0, keepdims=True)
    sel2 = jnp.where(ei == i1, -jnp.inf, sel)
    m2 = jnp.max(sel2, axis=0, keepdims=True)
    i2 = jnp.min(jnp.where(sel2 == m2, ei, eg), axis=0, keepdims=True)
    t = jnp.exp(m2 - m1)
    wa = 1.0 / (1.0 + t)
    eidx_ref[0] = jnp.concatenate([g_idx * eg + i1, g_idx * eg + i2], axis=0)
    wts_ref[0] = jnp.concatenate([g_p * wa, g_p * (t * wa)], axis=0)


def _merge(yda, yhg, gda, ghg, x, mod, g2, wda, whg, wout, wrh, wrl, br):
    b, s, d = x.shape
    tm = ROW_TILE
    w = yda.shape[2]
    tok = lambda width: pl.BlockSpec((1, tm, width), lambda bi, i: (bi, i, 0))
    modspec = lambda k: pl.BlockSpec((1, 1, d), lambda bi, i: (bi, 0, k))
    full = lambda a: pl.BlockSpec(a.shape, lambda bi, i: (0,) * a.ndim)
    lane_rows = pl.BlockSpec((1, 2, tm), lambda bi, i: (bi, 0, i))
    return pl.pallas_call(
        _merge_kernel,
        grid=(b, s // tm),
        in_specs=[tok(w), tok(w), tok(d), tok(d), tok(d),
                  modspec(2), modspec(3), modspec(4), full(g2),
                  full(wda), full(whg), full(wout), full(wrh), full(wrl), full(br)],
        out_specs=[tok(d), pl.BlockSpec((1, tm // 2 * SUBLANES, LANES), lambda bi, i: (bi, i, 0)),
                   lane_rows, lane_rows],
        out_shape=[jax.ShapeDtypeStruct((b, s, d), F32),
                   jax.ShapeDtypeStruct((b, s // 2 * SUBLANES, LANES), U32),
                   jax.ShapeDtypeStruct((b, 2, s), I32), jax.ShapeDtypeStruct((b, 2, s), F32)],
        compiler_params=pltpu.CompilerParams(vmem_limit_bytes=VMEM_LIMIT),
        name="merge",
    )(yda, yhg, gda, ghg, x, mod, mod, mod, g2, wda, whg, wout, wrh, wrl, br)


def _positions_kernel(e_ref, half_ref, blk_ref, nused_ref):
    e_mat = e_ref[...]
    nr = e_mat.shape[0]
    nb = blk_ref.shape[1]
    r = lax.broadcasted_iota(I32, (LANES, LANES), 0)
    c = lax.broadcasted_iota(I32, (LANES, LANES), 1)
    incl = jnp.where(r <= c, 1.0, 0.0).astype(BF16)
    ones = jnp.ones((LANES, LANES), BF16)
    rr = lax.broadcasted_iota(I32, (nr, nr), 0)
    cc = lax.broadcasted_iota(I32, (nr, nr), 1)
    before = jnp.where(cc < rr, 1.0, 0.0).astype(BF16)
    blk_start = (lax.broadcasted_iota(I32, (1, nb), 1) * EXPERT_BLOCK).astype(F32)

    def per_expert(e, carry):
        pstart, dest, blk = carry
        hit = e_mat == e
        oh = jnp.where(hit, 1.0, 0.0).astype(BF16)
        in_row = _nn(oh, incl)
        row_tot = _nn(oh, ones)
        row_off = _nn(before, row_tot.astype(BF16))
        total = row_off[nr - 1:nr, :] + row_tot[nr - 1:nr, :]
        dest = jnp.where(hit, pstart + row_off + in_row - 1.0, dest)
        pend = pstart + jnp.floor((total + (EXPERT_BLOCK - 1)) * (1.0 / EXPERT_BLOCK)) * EXPERT_BLOCK
        blk = blk + jnp.where(blk_start >= pend[:, :1], 1.0, 0.0)
        return pend, dest, blk

    pstart0 = jnp.zeros((1, LANES), F32)
    pend, dest, blk = lax.fori_loop(
        0, N_EXPERTS, per_expert,
        (pstart0, jnp.zeros(e_mat.shape, F32), jnp.zeros((1, nb), F32)), unroll=4)
    half_rows = EXPERT_BLOCK // 2
    block = jnp.floor(dest * (1.0 / EXPERT_BLOCK))
    in_block = dest - block * EXPERT_BLOCK
    upper = jnp.floor(in_block * (1.0 / half_rows))
    tile = block * half_rows + in_block - upper * half_rows
    half_ref[...] = (SUBLANES * tile + (SUBLANES // 2) * upper).astype(I32)
    blk_ref[...] = jnp.minimum(blk, N_EXPERTS - 1).astype(I32)
    nused_ref[...] = (pend * (1.0 / EXPERT_BLOCK)).astype(I32)


def _positions(e_mat, n_blk_pad):
    nr = e_mat.shape[0]
    return pl.pallas_call(
        _positions_kernel,
        out_shape=[jax.ShapeDtypeStruct((nr, LANES), I32),
                   jax.ShapeDtypeStruct((1, n_blk_pad), I32),
                   jax.ShapeDtypeStruct((1, LANES), I32)],
        name="positions",
    )(e_mat)


def _dispatch_kernel(half_ref, h2_ref, xs_in_ref, xs_ref, hbuf, in_sem, out_sem, *, n_tok):
    del xs_in_ref
    i = pl.program_id(0)
    last = pl.num_programs(0) - 1
    slot = i % 2
    half = SUBLANES // 2
    pairs = GATHER_TILE // 2
    tile_rows = pairs * SUBLANES

    def load(step, buf):
        return pltpu.make_async_copy(h2_ref.at[pl.ds(step * tile_rows, tile_rows)], hbuf.at[buf],
                                     in_sem.at[buf])

    def drain(buf):
        for k in range(2):
            pltpu.make_async_copy(hbuf.at[buf], xs_ref.at[pl.ds(0, tile_rows)], out_sem.at[buf, k]).wait()

    @pl.when(i == 0)
    def _():
        load(0, 0).start()

    @pl.when(i > 0)
    def _():
        drain(1 - slot)

    @pl.when(i < last)
    def _():
        load(i + 1, 1 - slot).start()

    load(i, slot).wait()

    for up in range(2):
        base = i * GATHER_TILE + up * pairs

        def issue(r, carry):
            for k in range(2):
                row = pl.multiple_of(half_ref[k * n_tok + base + r], half)
                pltpu.make_async_copy(hbuf.at[slot, pl.ds(r * SUBLANES + up * half, half)],
                                      xs_ref.at[pl.ds(row, half)], out_sem.at[slot, k]).start(priority=k)
            return carry

        lax.fori_loop(0, pairs, issue, 0, unroll=ISSUE_UNROLL)

    @pl.when(i == last)
    def _():
        drain(slot)


def _dispatch(half, h2, xs_zero):
    n_tok = h2.shape[0] // (SUBLANES // 2)
    return pl.pallas_call(
        functools.partial(_dispatch_kernel, n_tok=n_tok),
        grid_spec=pltpu.PrefetchScalarGridSpec(
            num_scalar_prefetch=1,
            grid=(n_tok // GATHER_TILE,),
            in_specs=[pl.BlockSpec(memory_space=pl.ANY), pl.BlockSpec(memory_space=pl.ANY)],
            out_specs=pl.BlockSpec(memory_space=pl.ANY),
            scratch_shapes=[pltpu.VMEM((2, GATHER_TILE // 2 * SUBLANES, LANES), h2.dtype),
                            pltpu.SemaphoreType.DMA((2,)), pltpu.SemaphoreType.DMA((2, 2))],
        ),
        out_shape=jax.ShapeDtypeStruct(xs_zero.shape, xs_zero.dtype),
        input_output_aliases={2: 0},
        compiler_params=pltpu.CompilerParams(dimension_semantics=("arbitrary",)),
        name="dispatch",
    )(half, h2, xs_zero)


def _expert_kernel(blk_ref, nused_ref, xs_ref, w1_ref, w3_ref, w2_ref, ys_ref,
                   w1b, w3b, w2b, ybuf, cur, sem):
    i = pl.program_id(0)
    n_used = nused_ref[0]
    e = blk_ref[jnp.minimum(i, pl.num_programs(0) - 2)]
    fresh = (i == 0) | (blk_ref[jnp.maximum(i - 1, 0)] != e)

    def fetch(expert, which):
        return (pltpu.make_async_copy(w1_ref.at[expert], w1b.at[which], sem.at[0]),
                pltpu.make_async_copy(w3_ref.at[expert], w3b.at[which], sem.at[1]),
                pltpu.make_async_copy(w2_ref.at[expert], w2b.at[which], sem.at[2]))

    @pl.when(i < n_used)
    def _():
        @pl.when(i == 0)
        def _():
            cur[0] = 0
            for cp in fetch(e, 0):
                cp.start()

        @pl.when(fresh & (i > 0))
        def _():
            cur[0] = 1 - cur[0]

        @pl.when(fresh)
        def _():
            for cp in fetch(e, cur[0]):
                cp.wait()
            nxt = lax.while_loop(lambda j: (j < n_used) & (blk_ref[jnp.minimum(j, n_used - 1)] == e),
                                 lambda j: j + 1, i + 1)

            @pl.when(nxt < n_used)
            def _():
                for cp in fetch(blk_ref[jnp.minimum(nxt, n_used - 1)], 1 - cur[0]):
                    cp.start(priority=1)

    tiles = (xs_ref.shape[0] // SUBLANES, SUBLANES, LANES)

    def compute():
        which = cur[0]
        x = _unpack_rows(xs_ref[...].reshape(tiles).reshape(tiles[0], w1b.shape[1])).astype(BF16)
        a = _nn(x, w1b[which])
        ybuf[...] = _nn((_silu(a) * _nn(x, w3b[which])).astype(BF16), w2b[which])

    def flush():
        ys_ref[...] = _pack_rows(ybuf[...]).reshape(tiles).reshape(ys_ref.shape)

    @pl.when(i == 0)
    def _():
        compute()

    @pl.when((i > 0) & (i < n_used))
    def _():
        flush()
        compute()

    @pl.when((i > 0) & (i == n_used))
    def _():
        flush()

    @pl.when(i > n_used)
    def _():
        ys_ref[...] = jnp.zeros(ys_ref.shape, ys_ref.dtype)


def _experts(blk_e, n_used, xs, w1, w3, w2):
    block_rows = EXPERT_BLOCK // 2 * SUBLANES
    d, de = w1.shape[1], w1.shape[2]
    rows = lambda i, blk, nu: (jnp.minimum(i, nu[0] - 1), 0)
    hbm = pl.BlockSpec(memory_space=pl.ANY)
    return pl.pallas_call(
        _expert_kernel,
        grid_spec=pltpu.PrefetchScalarGridSpec(
            num_scalar_prefetch=2,
            grid=(xs.shape[0] // block_rows + 1,),
            in_specs=[pl.BlockSpec((block_rows, LANES), rows), hbm, hbm, hbm],
            out_specs=pl.BlockSpec((block_rows, LANES), lambda i, blk, nu: (jnp.maximum(i - 1, 0), 0)),
            scratch_shapes=[pltpu.VMEM((2, d, de), BF16), pltpu.VMEM((2, d, de), BF16),
                            pltpu.VMEM((2, de, d), BF16), pltpu.VMEM((EXPERT_BLOCK, d), F32),
                            pltpu.SMEM((1,), I32), pltpu.SemaphoreType.DMA((3,))],
        ),
        out_shape=jax.ShapeDtypeStruct(xs.shape, xs.dtype),
        compiler_params=pltpu.CompilerParams(vmem_limit_bytes=VMEM_LIMIT,
                                             dimension_semantics=("arbitrary",)),
        name="experts",
    )(blk_e, n_used, xs, w1, w3, w2)


def _combine_kernel(half_ref, ys_ref, x1_ref, gt2_ref, wts_ref, g_ref, o_ref, ybuf, sem, *, n_tok, last):
    i = pl.program_id(0)
    last_step = pl.num_programs(0) - 1
    slot = i % 2
    d = x1_ref.shape[1]
    half = SUBLANES // 2
    pairs = GATHER_TILE // 2
    n_groups = pairs // ISSUE_UNROLL

    def gather(step, buf, j0):
        for up in range(2):
            base = step * GATHER_TILE + up * pairs + j0
            for u in range(ISSUE_UNROLL):
                for k in range(2):
                    row = pl.multiple_of(half_ref[k * n_tok + base + u], half)
                    pltpu.make_async_copy(ys_ref.at[pl.ds(row, half)],
                                          ybuf.at[buf, k, pl.ds((j0 + u) * SUBLANES + up * half, half)],
                                          sem.at[buf, k]).start(priority=k)

    def drain(buf):
        for k in range(2):
            pltpu.make_async_copy(ys_ref.at[pl.ds(0, pairs * SUBLANES)], ybuf.at[buf, k], sem.at[buf, k]).wait()

    @pl.when(i == 0)
    def _():
        def first(g, carry):
            gather(0, 0, g * ISSUE_UNROLL)
            return carry

        lax.fori_loop(0, n_groups, first, 0)

    drain(slot)

    nxt = jnp.minimum(i + 1, last_step)
    gt2 = gt2_ref[0]
    gain = g_ref[...]

    def group(g, carry):
        j0 = pl.multiple_of(g * ISSUE_UNROLL, ISSUE_UNROLL)
        lower, upper = pl.ds(j0, ISSUE_UNROLL), pl.ds(pairs + j0, ISSUE_UNROLL)
        tile_rows = pl.ds(pl.multiple_of(j0 * SUBLANES, ISSUE_UNROLL * SUBLANES), ISSUE_UNROLL * SUBLANES)

        def tokens(k):
            tiles = ybuf[slot, k, tile_rows, :].reshape(ISSUE_UNROLL, SUBLANES, LANES)
            return _unpack_rows(tiles.reshape(ISSUE_UNROLL, d))

        w = jnp.concatenate([wts_ref[lower, :], wts_ref[upper, :]], axis=0)
        x = jnp.concatenate([x1_ref[lower, :], x1_ref[upper, :]], axis=0)
        x = x + gt2 * (w[:, 0:1] * tokens(0) + w[:, 1:2] * tokens(1))
        if last:
            x = x * lax.rsqrt(jnp.mean(x * x, axis=-1, keepdims=True) + EPS) * gain
        o_ref[lower, :] = x[:ISSUE_UNROLL]
        o_ref[upper, :] = x[ISSUE_UNROLL:]
        gather(nxt, 1 - slot, j0)
        return carry

    lax.fori_loop(0, n_groups, group, 0, unroll=COMBINE_UNROLL)

    @pl.when(i == last_step)
    def _():
        drain(1 - slot)


def _combine(half, ys, x1, mod, wts, final_g, seq, last=True):
    n_tok, d = x1.shape
    tt = GATHER_TILE
    per_batch = seq // tt
    return pl.pallas_call(
        functools.partial(_combine_kernel, n_tok=n_tok, last=last),
        grid_spec=pltpu.PrefetchScalarGridSpec(
            num_scalar_prefetch=1,
            grid=(n_tok // tt,),
            in_specs=[pl.BlockSpec(memory_space=pl.ANY),
                      pl.BlockSpec((tt, d), lambda i, half: (i, 0)),
                      pl.BlockSpec((1, 1, d), lambda i, half: (i // per_batch, 0, 5)),
                      pl.BlockSpec((tt, 2), lambda i, half: (i, 0)),
                      pl.BlockSpec((1, d), lambda i, half: (0, 0))],
            out_specs=pl.BlockSpec((tt, d), lambda i, half: (i, 0)),
            scratch_shapes=[pltpu.VMEM((2, 2, tt // 2 * SUBLANES, LANES), U32),
                            pltpu.SemaphoreType.DMA((2, 2))],
        ),
        out_shape=jax.ShapeDtypeStruct((n_tok, d), F32),
        compiler_params=pltpu.CompilerParams(dimension_semantics=("arbitrary",)),
        name="combine",
    )(half, ys, x1, mod, wts, final_g)


def kernel(x, c, w_ada, b_ada, norm1_g, w_in, lambda_q1, lambda_k1, lambda_q2, lambda_k2, da_subln_g, hg_lb_logits, hg_norm_g, w_up_da, w_up_hg, w_out, norm2_g, w_rg, b_rg, w_re, b_re, w1, w3, w2, final_g):
    b, s, d = x.shape
    n_tok = b * s
    depth = w_ada.shape[0]
    assert s % ROW_TILE == 0 and s % ATT_BLOCK == 0 and GATHER_TILE == ROW_TILE
    assert (2 * n_tok) % LANES == 0 and d == SUBLANES * LANES

    c_pad = jnp.pad(c, ((0, (-b) % 8), (0, 0)))
    n_slots = 2 * n_tok + N_EXPERTS * EXPERT_BLOCK
    n_blk = n_slots // EXPERT_BLOCK
    n_blk_pad = -(-n_blk // LANES) * LANES

    for l in range(depth):
        lam_init = 0.8 - 0.6 * math.exp(-0.3 * l)
        mod = _adaln(c_pad, w_ada[l], b_ada[l][None, :])[:b].reshape(b, 1, 6 * d)

        (qda, kda, vda, qd, kd, ke, hi, sg, dec, gda, ghg, w1_bf, w3_bf, w2_bf) = _inproj(
            x, mod, norm1_g[l][None, :], w_in[l].astype(BF16), hg_lb_logits, l, w1[l], w3[l], w2[l])
        yda, xs_zero = _attention(qda, kda, vda, lambda_q1[l][None, :], lambda_k1[l][None, :],
                                  lambda_q2[l][None, :], lambda_k2[l][None, :], da_subln_g[l][None, :],
                                  lam_init, jax.ShapeDtypeStruct((n_slots // 2 * SUBLANES, LANES), U32))
        yhg = _hgrn(qd, kd, ke, hi, sg, dec, hg_norm_g[l][None, :])

        wr = jnp.concatenate([w_re[l].T, w_rg[l].T,
                              jnp.zeros((ROUTER_ROWS - N_EXPERTS - N_GROUPS, d), F32)], axis=0)
        wr_hi = wr.astype(BF16)
        wr_lo = (wr - wr_hi.astype(F32)).astype(BF16)
        br = jnp.concatenate([b_re[l], b_rg[l],
                              jnp.zeros((ROUTER_ROWS - N_EXPERTS - N_GROUPS,), F32)])[:, None]
        x1, h2, eidx, wts = _merge(yda, yhg, gda, ghg, x, mod, norm2_g[l][None, :],
                                   w_up_da[l].astype(BF16), w_up_hg[l].astype(BF16),
                                   w_out[l].astype(BF16), wr_hi, wr_lo, br)

        e_mat = jnp.transpose(eidx, (1, 0, 2)).reshape(2 * n_tok // LANES, LANES)
        half, blk_e, n_used = _positions(e_mat, n_blk_pad)
        half = half.reshape(2 * n_tok)
        xs = _dispatch(half, h2.reshape(b * h2.shape[1], LANES), xs_zero)
        ys = _experts(blk_e[0, :n_blk], n_used[0, :1], xs, w1_bf, w3_bf, w2_bf)
        wts_tok = jnp.transpose(wts, (0, 2, 1)).reshape(n_tok, 2)
        x = _combine(half, ys, x1.reshape(n_tok, d), mod, wts_tok, final_g[None, :], s,
                     last=(l == depth - 1)).reshape(b, s, d)
    return x
```

```python
import functools
import math

import jax
import jax.numpy as jnp
from jax import lax
from jax.experimental import pallas as pl
from jax.experimental.pallas import tpu as pltpu

F32 = jnp.float32
BF16 = jnp.bfloat16
I32 = jnp.int32
U32 = jnp.uint32

EPS = 1e-6
NEG = -1e30
LOG2E = math.log2(math.e)

CHUNK = 64
DA_HEADS = 4
DA_HEAD_DIM = 64
HG_HEADS = 4
HG_DK = 128
HG_DV = 128
N_GROUPS = 4
EXPERTS_PER_GROUP = 8
N_EXPERTS = N_GROUPS * EXPERTS_PER_GROUP

LANES = 128
SUBLANES = 8
ADALN_COLS = 1024
ROW_TILE = 512
ATT_BLOCK = 512
BIAS_TERMS = 3
EXPERT_BLOCK = 256
GATHER_TILE = 512
ISSUE_UNROLL = 8
COMBINE_UNROLL = 4
ROUTER_ROWS = 48
VMEM_LIMIT = 52 * 1024 * 1024


def _nt(a, b):
    return lax.dot_general(a, b, (((1,), (1,)), ((), ())), preferred_element_type=F32)


def _nn(a, b):
    return jnp.dot(a, b, preferred_element_type=F32)


def _split(a):
    hi = a.astype(BF16)
    lo = (a - hi.astype(F32)).astype(BF16)
    return hi, lo


def _sigmoid(x):
    return 1.0 / (1.0 + jnp.exp(-x))


def _silu(x):
    return x * _sigmoid(x)


def _pack_rows(y):
    rows, cols = y.shape[0] // 2, y.shape[1] // 2
    bits = lax.bitcast_convert_type(y.astype(BF16).astype(F32), U32)
    words = bits[:, :cols] | lax.shift_right_logical(bits[:, cols:], jnp.uint32(16))
    return jnp.concatenate([words[:rows], words[rows:]], axis=1)


def _unpack_rows(packed):
    cols = packed.shape[1] // 2
    words = jnp.concatenate([packed[:, :cols], packed[:, cols:]], axis=0)
    first = lax.bitcast_convert_type(words & jnp.uint32(0xFFFF0000), F32)
    second = lax.bitcast_convert_type(lax.shift_left(words, jnp.uint32(16)), F32)
    return jnp.concatenate([first, second], axis=1)


def _adaln_kernel(c_ref, w_ref, b_ref, o_ref):
    c_hi, c_lo = _split(_silu(c_ref[...]))
    w_hi, w_lo = _split(w_ref[...])
    o_ref[...] = _nn(c_hi, w_hi) + _nn(c_lo, w_hi) + _nn(c_hi, w_lo) + b_ref[...]


def _adaln(c, w, b):
    rows, d = c.shape
    n = w.shape[1]
    tn = ADALN_COLS
    return pl.pallas_call(
        _adaln_kernel,
        grid=(n // tn,),
        in_specs=[
            pl.BlockSpec((rows, d), lambda j: (0, 0)),
            pl.BlockSpec((d, tn), lambda j: (0, j)),
            pl.BlockSpec((1, tn), lambda j: (0, j)),
        ],
        out_specs=pl.BlockSpec((rows, tn), lambda j: (0, j)),
        out_shape=jax.ShapeDtypeStruct((rows, n), F32),
        name="adaln",
    )(c, w, b)


def _inproj_kernel(x_ref, sh_ref, sc_ref, g_ref, w_ref, lb_ref, w1_ref, w3_ref, w2_ref,
                   qda_ref, kda_ref, vda_ref, qd_ref, kd_ref, ke_ref, hi_ref, sg_ref,
                   dec_ref, gda_ref, ghg_ref, w1b_ref, w3b_ref, w2b_ref, *, layer):
    w1b_ref[...] = w1_ref[...].astype(BF16)
    w3b_ref[...] = w3_ref[...].astype(BF16)
    w2b_ref[...] = w2_ref[...].astype(BF16)

    x = x_ref[0]
    tm = x.shape[0]
    h = x * lax.rsqrt(jnp.mean(x * x, axis=-1, keepdims=True) + EPS) * g_ref[...]
    h = h * (1.0 + sc_ref[0]) + sh_ref[0]
    hb = h.astype(BF16)

    def proj(c0, width):
        return _nn(hb, w_ref[:, c0:c0 + width])

    w_da = DA_HEADS * 2 * DA_HEAD_DIM
    w_hg = HG_HEADS * HG_DK
    qda_ref[0] = (proj(0, w_da) * (LOG2E / math.sqrt(DA_HEAD_DIM))).astype(BF16)
    kda_ref[0] = proj(w_da, w_da).astype(BF16)
    vda_ref[0] = proj(2 * w_da, w_da).astype(BF16)
    c0 = 3 * w_da

    lbl = lb_ref[...]
    lbe = jnp.exp(lbl - jnp.max(lbl, axis=0, keepdims=True))
    lbs = lbe / jnp.sum(lbe, axis=0, keepdims=True)
    lb = jnp.sum(lbs[:layer + 1], axis=0, keepdims=True)

    f = lb + (1.0 - lb) * _sigmoid(proj(c0 + w_hg, w_hg))
    logf = jnp.log(f)
    kk = 1.0 - f
    pos = lax.broadcasted_iota(I32, logf.shape, 0) & (CHUNK - 1)
    a = logf
    step = 1
    while step < CHUNK:
        a = a + jnp.where(pos >= step, pltpu.roll(a, step, axis=0), 0.0)
        step *= 2
    n_ch = tm // CHUNK
    a = a.reshape(n_ch, CHUNK, w_hg)
    a_last = a[:, CHUNK - 1:CHUNK, :]
    kk3 = kk.reshape(n_ch, CHUNK, w_hg)
    qq3 = _silu(proj(c0, w_hg)).reshape(n_ch, CHUNK, w_hg)
    qd_ref[0] = (qq3 * jnp.exp(a)).reshape(tm, w_hg).astype(BF16)
    kd_ref[0] = (kk3 * jnp.exp(-a)).reshape(tm, w_hg).astype(BF16)
    ke_ref[0] = (kk3 * jnp.exp(a_last - a)).reshape(tm, w_hg).astype(BF16)
    dec_ref[0] = jnp.exp(a_last).reshape(n_ch, w_hg)
    hi_ref[0] = proj(c0 + 2 * w_hg, w_hg).astype(BF16)
    sg_ref[0] = _silu(proj(c0 + 3 * w_hg, w_hg)).astype(BF16)
    c1 = c0 + 4 * w_hg
    d = x.shape[1]
    gda_ref[0] = _sigmoid(proj(c1, d)).astype(BF16)
    ghg_ref[0] = _sigmoid(proj(c1 + d, d)).astype(BF16)


def _inproj(x, mod, g, w_bf, lb_logits, layer, w1, w3, w2):
    b, s, d = x.shape
    tm = ROW_TILE
    n_cols = w_bf.shape[1]
    per_b = s // tm
    assert w1.shape[0] % (b * per_b) == 0
    e_step = w1.shape[0] // (b * per_b)
    wspec = lambda a: pl.BlockSpec((e_step,) + a.shape[1:], lambda bi, i: (bi * per_b + i, 0, 0))
    w_da = DA_HEADS * 2 * DA_HEAD_DIM
    w_hg = HG_HEADS * HG_DK

    def tok(width, dtype):
        return (pl.BlockSpec((1, tm, width), lambda bi, i: (bi, i, 0)),
                jax.ShapeDtypeStruct((b, s, width), dtype))

    outs = [tok(w_da, BF16)] * 3 + [tok(w_hg, BF16)] * 5
    outs.append((pl.BlockSpec((1, tm // CHUNK, w_hg), lambda bi, i: (bi, i, 0)),
                 jax.ShapeDtypeStruct((b, s // CHUNK, w_hg), F32)))
    outs += [tok(d, BF16)] * 2
    outs += [(wspec(a), jax.ShapeDtypeStruct(a.shape, BF16)) for a in (w1, w3, w2)]
    return pl.pallas_call(
        functools.partial(_inproj_kernel, layer=layer),
        grid=(b, s // tm),
        in_specs=[
            pl.BlockSpec((1, tm, d), lambda bi, i: (bi, i, 0)),
            pl.BlockSpec((1, 1, d), lambda bi, i: (bi, 0, 0)),
            pl.BlockSpec((1, 1, d), lambda bi, i: (bi, 0, 1)),
            pl.BlockSpec((1, d), lambda bi, i: (0, 0)),
            pl.BlockSpec((d, n_cols), lambda bi, i: (0, 0), pipeline_mode=pl.Buffered(1)),
            pl.BlockSpec(lb_logits.shape, lambda bi, i: (0, 0)),
            wspec(w1), wspec(w3), wspec(w2),
        ],
        out_specs=[o[0] for o in outs],
        out_shape=[o[1] for o in outs],
        compiler_params=pltpu.CompilerParams(vmem_limit_bytes=VMEM_LIMIT),
        name="inproj",
    )(x, mod, mod, g, w_bf, lb_logits, w1, w3, w2)


def _attn_kernel(q_ref, k_ref, v_ref, lq1_ref, lk1_ref, lq2_ref, lk2_ref, g_ref, o_ref, zs_ref,
                 kaug_ref, vaug_ref, qaug_ref, m_ref, acc_ref, sa_ref, sb_ref, zero_ref, zsem,
                 *, lam_init, slopes):
    hd = pl.program_id(1)
    i = pl.program_id(2)

    step = (pl.program_id(0) * pl.num_programs(1) + hd) * pl.num_programs(2) + i
    n_steps = pl.num_programs(0) * pl.num_programs(1) * pl.num_programs(2)
    zrows = zero_ref.shape[0]

    def zero_copy(t):
        return pltpu.make_async_copy(zero_ref, zs_ref.at[pl.ds(t * zrows, zrows)], zsem)

    @pl.when(step == 0)
    def _():
        zero_ref[...] = jnp.zeros(zero_ref.shape, zero_ref.dtype)

    @pl.when(step > 0)
    def _():
        zero_copy(step - 1).wait()

    zero_copy(step).start()

    @pl.when(step == n_steps - 1)
    def _():
        zero_copy(step).wait()
    tq = q_ref.shape[1]
    hw = q_ref.shape[2]
    n_key = k_ref.shape[1]

    slope = jnp.float32(slopes[-1])
    for idx in range(len(slopes) - 2, -1, -1):
        slope = jnp.where(hd == idx, jnp.float32(slopes[idx]), slope)

    @pl.when(i == 0)
    def _():
        kaug_ref[:hw, :] = k_ref[0].astype(F32).T.astype(BF16)
        jrel = (lax.broadcasted_iota(I32, (hw, n_key), 1) & (tq - 1)).astype(F32) * slope
        krow = lax.broadcasted_iota(I32, (hw, n_key), 0)
        bias_rows = jnp.zeros((hw, n_key), F32)
        rest = jrel
        for term in range(BIAS_TERMS):
            part = rest.astype(BF16).astype(F32)
            bias_rows = jnp.where(krow == term, part, bias_rows)
            rest = rest - part
        kaug_ref[hw:, :] = bias_rows.astype(BF16)
        vaug_ref[:, :hw] = v_ref[0]
        vaug_ref[:, hw:] = jnp.ones((n_key, hw), BF16)

    q = q_ref[0]
    lane = lax.broadcasted_iota(I32, q.shape, 1)
    zero = jnp.zeros_like(q)
    ones = jnp.where(lane < BIAS_TERMS, 1.0, 0.0).astype(BF16)
    qaug_ref[:tq, :hw] = jnp.where(lane < DA_HEAD_DIM, q, zero)
    qaug_ref[tq:, :hw] = jnp.where(lane >= DA_HEAD_DIM, q, zero)
    qaug_ref[:tq, hw:] = ones
    qaug_ref[tq:, hw:] = ones

    m_ref[...] = jnp.full(m_ref.shape, NEG, F32)
    acc_ref[...] = jnp.zeros(acc_ref.shape, F32)

    def scores(j, buf):
        buf[...] = _nn(qaug_ref[...], kaug_ref[:, pl.ds(pl.multiple_of(j * tq, tq), tq)])

    def consume(j, buf, diagonal):
        s = buf[...]
        if diagonal:
            ii = lax.broadcasted_iota(I32, s.shape, 0) & (tq - 1)
            jj = lax.broadcasted_iota(I32, s.shape, 1)
            ahead = jnp.minimum(ii - jj, 0).astype(F32)
            s = s + jnp.where((jj // CHUNK) <= (ii // CHUNK), (2.0 * slope) * ahead, NEG)
            shift = jnp.float32(0.0)
        else:
            shift = -slope * ((i - j) * tq).astype(F32)
        m_old = m_ref[...]
        m_new = jnp.maximum(m_old, jnp.max(s, axis=-1, keepdims=True) + shift)
        p = jnp.exp2(s - jnp.concatenate([m_new - shift] * (tq // LANES), axis=1))
        alpha = jnp.concatenate([jnp.exp2(m_old - m_new)] * (acc_ref.shape[1] // LANES), axis=1)
        vb = vaug_ref[pl.ds(pl.multiple_of(j * tq, tq), tq), :]
        acc_ref[...] = alpha * acc_ref[...] + _nn(p.astype(BF16), vb)
        m_ref[...] = m_new

    scores(0, sa_ref)

    def pair(pi, carry):
        j = 2 * pi
        scores(j + 1, sb_ref)
        consume(j, sa_ref, False)
        scores(j + 2, sa_ref)
        consume(j + 1, sb_ref, False)
        return carry

    lax.fori_loop(0, i // 2, pair, 0)

    @pl.when(i % 2 == 0)
    def _():
        consume(i, sa_ref, True)

    @pl.when(i % 2 == 1)
    def _():
        scores(i, sb_ref)
        consume(i - 1, sa_ref, False)
        consume(i, sb_ref, True)

    acc = acc_ref[...]
    o1 = acc[:tq, :hw] / acc[:tq, hw:]
    o2 = acc[tq:, :hw] / acc[tq:, hw:]
    lam = (jnp.exp(jnp.sum(lq1_ref[...] * lk1_ref[...], axis=-1, keepdims=True))
           - jnp.exp(jnp.sum(lq2_ref[...] * lk2_ref[...], axis=-1, keepdims=True)) + lam_init)
    o = o1 - lam * o2
    o = o * lax.rsqrt(jnp.mean(o * o, axis=-1, keepdims=True) + EPS) * g_ref[...]
    o_ref[0] = (o * (1.0 - lam_init)).astype(BF16)


def _attention(q, k, v, lq1, lk1, lq2, lk2, subln_g, lam_init, zero_buf):
    b, s, _ = q.shape
    zero_shape = zero_buf.shape
    tq = ATT_BLOCK
    hw = 2 * DA_HEAD_DIM
    assert hw == LANES
    n_steps = b * DA_HEADS * (s // tq)
    assert zero_shape[0] % n_steps == 0
    zrows = zero_shape[0] // n_steps
    slopes = tuple(LOG2E * 2.0 ** (-8.0 * (h + 1) / DA_HEADS) for h in range(DA_HEADS))
    vec = lambda n: pl.BlockSpec((1, n), lambda bi, h, i: (0, 0))
    return pl.pallas_call(
        functools.partial(_attn_kernel, lam_init=lam_init, slopes=slopes),
        grid=(b, DA_HEADS, s // tq),
        in_specs=[
            pl.BlockSpec((1, tq, hw), lambda bi, h, i: (bi, i, h)),
            pl.BlockSpec((1, s, hw), lambda bi, h, i: (bi, 0, h)),
            pl.BlockSpec((1, s, hw), lambda bi, h, i: (bi, 0, h)),
            vec(DA_HEAD_DIM), vec(DA_HEAD_DIM), vec(DA_HEAD_DIM), vec(DA_HEAD_DIM), vec(hw),
        ],
        out_specs=[pl.BlockSpec((1, tq, hw), lambda bi, h, i: (bi, i, h)),
                   pl.BlockSpec(memory_space=pl.ANY)],
        out_shape=[jax.ShapeDtypeStruct((b, s, DA_HEADS * hw), BF16),
                   zero_buf],
        scratch_shapes=[
            pltpu.VMEM((2 * hw, s), BF16),
            pltpu.VMEM((s, 2 * hw), BF16),
            pltpu.VMEM((2 * tq, 2 * hw), BF16),
            pltpu.VMEM((2 * tq, LANES), F32),
            pltpu.VMEM((2 * tq, 2 * hw), F32),
            pltpu.VMEM((2 * tq, tq), F32),
            pltpu.VMEM((2 * tq, tq), F32),
            pltpu.VMEM((zrows,) + tuple(zero_shape[1:]), zero_buf.dtype),
            pltpu.SemaphoreType.DMA(()),
        ],
        compiler_params=pltpu.CompilerParams(vmem_limit_bytes=VMEM_LIMIT,
                                             dimension_semantics=("arbitrary",) * 3),
        name="attention",
    )(q, k, v, lq1, lk1, lq2, lk2, subln_g)


def _hgrn_kernel(qd_ref, kd_ref, ke_ref, v_ref, sg_ref, dec_ref, g_ref, o_ref, st_ref):
    @pl.when(pl.program_id(1) == 0)
    def _():
        st_ref[...] = jnp.zeros(st_ref.shape, F32)

    n_ch = qd_ref.shape[1] // CHUNK
    r = lax.broadcasted_iota(I32, (CHUNK, CHUNK), 0)
    c = lax.broadcasted_iota(I32, (CHUNK, CHUNK), 1)
    causal = r >= c

    heads = [slice(h * HG_DK, (h + 1) * HG_DK) for h in range(HG_HEADS)]
    chunks = [slice(ci * CHUNK, (ci + 1) * CHUNK) for ci in range(n_ch)]
    incr = [[_nn(v_ref[0, rows, cols].astype(F32).T.astype(BF16), ke_ref[0, rows, cols])
             for rows in chunks] for cols in heads]
    start = []
    for h, cols in enumerate(heads):
        st = st_ref[h]
        before = []
        for ci in range(n_ch):
            before.append(st.astype(BF16))
            st = st * dec_ref[0, ci:ci + 1, cols] + incr[h][ci]
        st_ref[h] = st
        start.append(before)
    for h, cols in enumerate(heads):
        for ci, rows in enumerate(chunks):
            qd = qd_ref[0, rows, cols]
            scores = jnp.where(causal, _nt(qd, kd_ref[0, rows, cols]), 0.0).astype(BF16)
            o = _nn(scores, v_ref[0, rows, cols]) + _nt(qd, start[h][ci])
            o = o * lax.rsqrt(jnp.mean(o * o, axis=-1, keepdims=True) + EPS) * g_ref[...]
            o_ref[0, rows, cols] = (o * sg_ref[0, rows, cols].astype(F32)).astype(BF16)


def _hgrn(qd, kd, ke, v, sg, dec, norm_g):
    b, s, w = qd.shape
    ts = ROW_TILE
    tok = pl.BlockSpec((1, ts, w), lambda bi, i: (bi, i, 0))
    return pl.pallas_call(
        _hgrn_kernel,
        grid=(b, s // ts),
        in_specs=[tok, tok, tok, tok, tok,
                  pl.BlockSpec((1, ts // CHUNK, w), lambda bi, i: (bi, i, 0)),
                  pl.BlockSpec((1, HG_DV), lambda bi, i: (0, 0))],
        out_specs=tok,
        out_shape=jax.ShapeDtypeStruct((b, s, w), BF16),
        scratch_shapes=[pltpu.VMEM((HG_HEADS, HG_DV, HG_DK), F32)],
        name="hgrn",
    )(qd, kd, ke, v, sg, dec, norm_g)


def _merge_kernel(yda_ref, yhg_ref, gda_ref, ghg_ref, x_ref, gt1_ref, sh2_ref, sc2_ref, g2_ref,
                  wda_ref, whg_ref, wout_ref, wrh_ref, wrl_ref, br_ref,
                  x1_ref, h2_ref, eidx_ref, wts_ref):
    u = (gda_ref[0].astype(F32) * _nn(yda_ref[0], wda_ref[...])
         + ghg_ref[0].astype(F32) * _nn(yhg_ref[0], whg_ref[...]))
    x1 = x_ref[0] + gt1_ref[0] * _nn(u.astype(BF16), wout_ref[...])
    x1_ref[0] = x1
    h2 = x1 * lax.rsqrt(jnp.mean(x1 * x1, axis=-1, keepdims=True) + EPS) * g2_ref[...]
    h2 = h2 * (1.0 + sc2_ref[0]) + sh2_ref[0]
    h2_ref[0] = _pack_rows(h2).reshape(h2.shape[0] // 2, SUBLANES, LANES).reshape(h2_ref.shape[1:])

    h_hi, h_lo = _split(h2)
    wrh = wrh_ref[...]
    both = _nt(jnp.concatenate([wrh, wrl_ref[...]], axis=0), h_hi)
    lg = both[:ROUTER_ROWS] + both[ROUTER_ROWS:] + _nt(wrh, h_lo) + br_ref[...]

    tm = h2.shape[0]
    eg = EXPERTS_PER_GROUP
    gl = lg[N_EXPERTS:N_EXPERTS + N_GROUPS]
    gmax = jnp.max(gl, axis=0, keepdims=True)
    g_p = 1.0 / jnp.sum(jnp.exp(gl - gmax), axis=0, keepdims=True)
    gi = lax.broadcasted_iota(I32, (N_GROUPS, tm), 0)
    g_idx = jnp.min(jnp.where(gl == gmax, gi, N_GROUPS), axis=0, keepdims=True)
    sel = lg[(N_GROUPS - 1) * eg:N_GROUPS * eg]
    for g in range(N_GROUPS - 2, -1, -1):
        sel = jnp.where(g_idx == g, lg[g * eg:(g + 1) * eg], sel)
    ei = lax.broadcasted_iota(I32, (eg, tm), 0)
    m1 = jnp.max(sel, axis=0, keepdims=True)
    i1 = jnp.min(jnp.where(sel == m1, ei, eg), axis=0, keepdims=True)
    sel2 = jnp.where(ei == i1, -jnp.inf, sel)
    m2 = jnp.max(sel2, axis=0, keepdims=True)
    i2 = jnp.min(jnp.where(sel2 == m2, ei, eg), axis=0, keepdims=True)
    t = jnp.exp(m2 - m1)
    wa = 1.0 / (1.0 + t)
    eidx_ref[0] = jnp.concatenate([g_idx * eg + i1, g_idx * eg + i2], axis=0)
    wts_ref[0] = jnp.concatenate([g_p * wa, g_p * (t * wa)], axis=0)


def _merge(yda, yhg, gda, ghg, x, mod, g2, wda, whg, wout, wrh, wrl, br):
    b, s, d = x.shape
    tm = ROW_TILE
    w = yda.shape[2]
    tok = lambda width: pl.BlockSpec((1, tm, width), lambda bi, i: (bi, i, 0))
    modspec = lambda k: pl.BlockSpec((1, 1, d), lambda bi, i: (bi, 0, k))
    full = lambda a: pl.BlockSpec(a.shape, lambda bi, i: (0,) * a.ndim)
    lane_rows = pl.BlockSpec((1, 2, tm), lambda bi, i: (bi, 0, i))
    return pl.pallas_call(
        _merge_kernel,
        grid=(b, s // tm),
        in_specs=[tok(w), tok(w), tok(d), tok(d), tok(d),
                  modspec(2), modspec(3), modspec(4), full(g2),
                  full(wda), full(whg), full(wout), full(wrh), full(wrl), full(br)],
        out_specs=[tok(d), pl.BlockSpec((1, tm // 2 * SUBLANES, LANES), lambda bi, i: (bi, i, 0)),
                   lane_rows, lane_rows],
        out_shape=[jax.ShapeDtypeStruct((b, s, d), F32),
                   jax.ShapeDtypeStruct((b, s // 2 * SUBLANES, LANES), U32),
                   jax.ShapeDtypeStruct((b, 2, s), I32), jax.ShapeDtypeStruct((b, 2, s), F32)],
        compiler_params=pltpu.CompilerParams(vmem_limit_bytes=VMEM_LIMIT),
        name="merge",
    )(yda, yhg, gda, ghg, x, mod, mod, mod, g2, wda, whg, wout, wrh, wrl, br)


def _positions_kernel(e_ref, half_ref, blk_ref, nused_ref):
    e_mat = e_ref[...]
    nr = e_mat.shape[0]
    nb = blk_ref.shape[1]
    r = lax.broadcasted_iota(I32, (LANES, LANES), 0)
    c = lax.broadcasted_iota(I32, (LANES, LANES), 1)
    incl = jnp.where(r <= c, 1.0, 0.0).astype(BF16)
    ones = jnp.ones((LANES, LANES), BF16)
    rr = lax.broadcasted_iota(I32, (nr, nr), 0)
    cc = lax.broadcasted_iota(I32, (nr, nr), 1)
    before = jnp.where(cc < rr, 1.0, 0.0).astype(BF16)
    blk_start = (lax.broadcasted_iota(I32, (1, nb), 1) * EXPERT_BLOCK).astype(F32)

    def per_expert(e, carry):
        pstart, dest, blk = carry
        hit = e_mat == e
        oh = jnp.where(hit, 1.0, 0.0).astype(BF16)
        in_row = _nn(oh, incl)
        row_tot = _nn(oh, ones)
        row_off = _nn(before, row_tot.astype(BF16))
        total = row_off[nr - 1:nr, :] + row_tot[nr - 1:nr, :]
        dest = jnp.where(hit, pstart + row_off + in_row - 1.0, dest)
        pend = pstart + jnp.floor((total + (EXPERT_BLOCK - 1)) * (1.0 / EXPERT_BLOCK)) * EXPERT_BLOCK
        blk = blk + jnp.where(blk_start >= pend[:, :1], 1.0, 0.0)
        return pend, dest, blk

    pstart0 = jnp.zeros((1, LANES), F32)
    pend, dest, blk = lax.fori_loop(
        0, N_EXPERTS, per_expert,
        (pstart0, jnp.zeros(e_mat.shape, F32), jnp.zeros((1, nb), F32)), unroll=4)
    half_rows = EXPERT_BLOCK // 2
    block = jnp.floor(dest * (1.0 / EXPERT_BLOCK))
    in_block = dest - block * EXPERT_BLOCK
    upper = jnp.floor(in_block * (1.0 / half_rows))
    tile = block * half_rows + in_block - upper * half_rows
    half_ref[...] = (SUBLANES * tile + (SUBLANES // 2) * upper).astype(I32)
    blk_ref[...] = jnp.minimum(blk, N_EXPERTS - 1).astype(I32)
    nused_ref[...] = (pend * (1.0 / EXPERT_BLOCK)).astype(I32)


def _positions(e_mat, n_blk_pad):
    nr = e_mat.shape[0]
    return pl.pallas_call(
        _positions_kernel,
        out_shape=[jax.ShapeDtypeStruct((nr, LANES), I32),
                   jax.ShapeDtypeStruct((1, n_blk_pad), I32),
                   jax.ShapeDtypeStruct((1, LANES), I32)],
        name="positions",
    )(e_mat)


def _dispatch_kernel(half_ref, h2_ref, xs_in_ref, xs_ref, hbuf, in_sem, out_sem, *, n_tok):
    del xs_in_ref
    i = pl.program_id(0)
    last = pl.num_programs(0) - 1
    slot = i % 2
    half = SUBLANES // 2
    pairs = GATHER_TILE // 2
    tile_rows = pairs * SUBLANES

    def load(step, buf):
        return pltpu.make_async_copy(h2_ref.at[pl.ds(step * tile_rows, tile_rows)], hbuf.at[buf],
                                     in_sem.at[buf])

    def drain(buf):
        for k in range(2):
            pltpu.make_async_copy(hbuf.at[buf], xs_ref.at[pl.ds(0, tile_rows)], out_sem.at[buf, k]).wait()

    @pl.when(i == 0)
    def _():
        load(0, 0).start()

    @pl.when(i > 0)
    def _():
        drain(1 - slot)

    @pl.when(i < last)
    def _():
        load(i + 1, 1 - slot).start()

    load(i, slot).wait()

    for up in range(2):
        base = i * GATHER_TILE + up * pairs

        def issue(r, carry):
            for k in range(2):
                row = pl.multiple_of(half_ref[k * n_tok + base + r], half)
                pltpu.make_async_copy(hbuf.at[slot, pl.ds(r * SUBLANES + up * half, half)],
                                      xs_ref.at[pl.ds(row, half)], out_sem.at[slot, k]).start(priority=k)
            return carry

        lax.fori_loop(0, pairs, issue, 0, unroll=ISSUE_UNROLL)

    @pl.when(i == last)
    def _():
        drain(slot)


def _dispatch(half, h2, xs_zero):
    n_tok = h2.shape[0] // (SUBLANES // 2)
    return pl.pallas_call(
        functools.partial(_dispatch_kernel, n_tok=n_tok),
        grid_spec=pltpu.PrefetchScalarGridSpec(
            num_scalar_prefetch=1,
            grid=(n_tok // GATHER_TILE,),
            in_specs=[pl.BlockSpec(memory_space=pl.ANY), pl.BlockSpec(memory_space=pl.ANY)],
            out_specs=pl.BlockSpec(memory_space=pl.ANY),
            scratch_shapes=[pltpu.VMEM((2, GATHER_TILE // 2 * SUBLANES, LANES), h2.dtype),
                            pltpu.SemaphoreType.DMA((2,)), pltpu.SemaphoreType.DMA((2, 2))],
        ),
        out_shape=jax.ShapeDtypeStruct(xs_zero.shape, xs_zero.dtype),
        input_output_aliases={2: 0},
        compiler_params=pltpu.CompilerParams(dimension_semantics=("arbitrary",)),
        name="dispatch",
    )(half, h2, xs_zero)


def _expert_kernel(blk_ref, nused_ref, xs_ref, w1_ref, w3_ref, w2_ref, ys_ref,
                   w1b, w3b, w2b, ybuf, cur, sem):
    i = pl.program_id(0)
    n_used = nused_ref[0]
    e = blk_ref[jnp.minimum(i, pl.num_programs(0) - 2)]
    fresh = (i == 0) | (blk_ref[jnp.maximum(i - 1, 0)] != e)

    def fetch(expert, which):
        return (pltpu.make_async_copy(w1_ref.at[expert], w1b.at[which], sem.at[0]),
                pltpu.make_async_copy(w3_ref.at[expert], w3b.at[which], sem.at[1]),
                pltpu.make_async_copy(w2_ref.at[expert], w2b.at[which], sem.at[2]))

    @pl.when(i < n_used)
    def _():
        @pl.when(i == 0)
        def _():
            cur[0] = 0
            for cp in fetch(e, 0):
                cp.start()

        @pl.when(fresh & (i > 0))
        def _():
            cur[0] = 1 - cur[0]

        @pl.when(fresh)
        def _():
            for cp in fetch(e, cur[0]):
                cp.wait()
            nxt = lax.while_loop(lambda j: (j < n_used) & (blk_ref[jnp.minimum(j, n_used - 1)] == e),
                                 lambda j: j + 1, i + 1)

            @pl.when(nxt < n_used)
            def _():
                for cp in fetch(blk_ref[jnp.minimum(nxt, n_used - 1)], 1 - cur[0]):
                    cp.start(priority=1)

    tiles = (xs_ref.shape[0] // SUBLANES, SUBLANES, LANES)

    def compute():
        which = cur[0]
        x = _unpack_rows(xs_ref[...].reshape(tiles).reshape(tiles[0], w1b.shape[1])).astype(BF16)
        a = _nn(x, w1b[which])
        ybuf[...] = _nn((_silu(a) * _nn(x, w3b[which])).astype(BF16), w2b[which])

    def flush():
        ys_ref[...] = _pack_rows(ybuf[...]).reshape(tiles).reshape(ys_ref.shape)

    @pl.when(i == 0)
    def _():
        compute()

    @pl.when((i > 0) & (i < n_used))
    def _():
        flush()
        compute()

    @pl.when((i > 0) & (i == n_used))
    def _():
        flush()

    @pl.when(i > n_used)
    def _():
        ys_ref[...] = jnp.zeros(ys_ref.shape, ys_ref.dtype)


def _experts(blk_e, n_used, xs, w1, w3, w2):
    block_rows = EXPERT_BLOCK // 2 * SUBLANES
    d, de = w1.shape[1], w1.shape[2]
    rows = lambda i, blk, nu: (jnp.minimum(i, nu[0] - 1), 0)
    hbm = pl.BlockSpec(memory_space=pl.ANY)
    return pl.pallas_call(
        _expert_kernel,
        grid_spec=pltpu.PrefetchScalarGridSpec(
            num_scalar_prefetch=2,
            grid=(xs.shape[0] // block_rows + 1,),
            in_specs=[pl.BlockSpec((block_rows, LANES), rows), hbm, hbm, hbm],
            out_specs=pl.BlockSpec((block_rows, LANES), lambda i, blk, nu: (jnp.maximum(i - 1, 0), 0)),
            scratch_shapes=[pltpu.VMEM((2, d, de), BF16), pltpu.VMEM((2, d, de), BF16),
                            pltpu.VMEM((2, de, d), BF16), pltpu.VMEM((EXPERT_BLOCK, d), F32),
                            pltpu.SMEM((1,), I32), pltpu.SemaphoreType.DMA((3,))],
        ),
        out_shape=jax.ShapeDtypeStruct(xs.shape, xs.dtype),
        compiler_params=pltpu.CompilerParams(vmem_limit_bytes=VMEM_LIMIT,
                                             dimension_semantics=("arbitrary",)),
        name="experts",
    )(blk_e, n_used, xs, w1, w3, w2)


def _combine_kernel(half_ref, ys_ref, x1_ref, gt2_ref, wts_ref, g_ref, o_ref, ybuf, sem, *, n_tok, last):
    i = pl.program_id(0)
    last_step = pl.num_programs(0) - 1
    slot = i % 2
    d = x1_ref.shape[1]
    half = SUBLANES // 2
    pairs = GATHER_TILE // 2
    n_groups = pairs // ISSUE_UNROLL

    def gather(step, buf, j0):
        for up in range(2):
            base = step * GATHER_TILE + up * pairs + j0
            for u in range(ISSUE_UNROLL):
                for k in range(2):
                    row = pl.multiple_of(half_ref[k * n_tok + base + u], half)
                    pltpu.make_async_copy(ys_ref.at[pl.ds(row, half)],
                                          ybuf.at[buf, k, pl.ds((j0 + u) * SUBLANES + up * half, half)],
                                          sem.at[buf, k]).start(priority=k)

    def drain(buf):
        for k in range(2):
            pltpu.make_async_copy(ys_ref.at[pl.ds(0, pairs * SUBLANES)], ybuf.at[buf, k], sem.at[buf, k]).wait()

    @pl.when(i == 0)
    def _():
        def first(g, carry):
            gather(0, 0, g * ISSUE_UNROLL)
            return carry

        lax.fori_loop(0, n_groups, first, 0)

    drain(slot)

    nxt = jnp.minimum(i + 1, last_step)
    gt2 = gt2_ref[0]
    gain = g_ref[...]

    def group(g, carry):
        j0 = pl.multiple_of(g * ISSUE_UNROLL, ISSUE_UNROLL)
        lower, upper = pl.ds(j0, ISSUE_UNROLL), pl.ds(pairs + j0, ISSUE_UNROLL)
        tile_rows = pl.ds(pl.multiple_of(j0 * SUBLANES, ISSUE_UNROLL * SUBLANES), ISSUE_UNROLL * SUBLANES)

        def tokens(k):
            tiles = ybuf[slot, k, tile_rows, :].reshape(ISSUE_UNROLL, SUBLANES, LANES)
            return _unpack_rows(tiles.reshape(ISSUE_UNROLL, d))

        w = jnp.concatenate([wts_ref[lower, :], wts_ref[upper, :]], axis=0)
        x = jnp.concatenate([x1_ref[lower, :], x1_ref[upper, :]], axis=0)
        x = x + gt2 * (w[:, 0:1] * tokens(0) + w[:, 1:2] * tokens(1))
        if last:
            x = x * lax.rsqrt(jnp.mean(x * x, axis=-1, keepdims=True) + EPS) * gain
        o_ref[lower, :] = x[:ISSUE_UNROLL]
        o_ref[upper, :] = x[ISSUE_UNROLL:]
        gather(nxt, 1 - slot, j0)
        return carry

    lax.fori_loop(0, n_groups, group, 0, unroll=COMBINE_UNROLL)

    @pl.when(i == last_step)
    def _():
        drain(1 - slot)


def _combine(half, ys, x1, mod, wts, final_g, seq, last=True):
    n_tok, d = x1.shape
    tt = GATHER_TILE
    per_batch = seq // tt
    return pl.pallas_call(
        functools.partial(_combine_kernel, n_tok=n_tok, last=last),
        grid_spec=pltpu.PrefetchScalarGridSpec(
            num_scalar_prefetch=1,
            grid=(n_tok // tt,),
            in_specs=[pl.BlockSpec(memory_space=pl.ANY),
                      pl.BlockSpec((tt, d), lambda i, half: (i, 0)),
                      pl.BlockSpec((1, 1, d), lambda i, half: (i // per_batch, 0, 5)),
                      pl.BlockSpec((tt, 2), lambda i, half: (i, 0)),
                      pl.BlockSpec((1, d), lambda i, half: (0, 0))],
            out_specs=pl.BlockSpec((tt, d), lambda i, half: (i, 0)),
            scratch_shapes=[pltpu.VMEM((2, 2, tt // 2 * SUBLANES, LANES), U32),
                            pltpu.SemaphoreType.DMA((2, 2))],
        ),
        out_shape=jax.ShapeDtypeStruct((n_tok, d), F32),
        compiler_params=pltpu.CompilerParams(dimension_semantics=("arbitrary",)),
        name="combine",
    )(half, ys, x1, mod, wts, final_g)


def kernel(x, c, w_ada, b_ada, norm1_g, w_in, lambda_q1, lambda_k1, lambda_q2, lambda_k2, da_subln_g, hg_lb_logits, hg_norm_g, w_up_da, w_up_hg, w_out, norm2_g, w_rg, b_rg, w_re, b_re, w1, w3, w2, final_g):
    b, s, d = x.shape
    n_tok = b * s
    depth = w_ada.shape[0]
    assert s % ROW_TILE == 0 and s % ATT_BLOCK == 0 and GATHER_TILE == ROW_TILE
    assert (2 * n_tok) % LANES == 0 and d == SUBLANES * LANES

    n_slots = 2 * n_tok + N_EXPERTS * EXPERT_BLOCK
    n_blk = n_slots // EXPERT_BLOCK
    n_blk_pad = -(-n_blk // LANES) * LANES

    for l in range(depth):
        lam_init = 0.8 - 0.6 * math.exp(-0.3 * l)
        mod = _adaln(c, w_ada[l], b_ada[l][None, :]).reshape(b, 1, 6 * d)

        (qda, kda, vda, qd, kd, ke, hi, sg, dec, gda, ghg, w1_bf, w3_bf, w2_bf) = _inproj(
            x, mod, norm1_g[l][None, :], w_in[l].astype(BF16), hg_lb_logits, l, w1[l], w3[l], w2[l])
        yda, xs_zero = _attention(qda, kda, vda, lambda_q1[l][None, :], lambda_k1[l][None, :],
                                  lambda_q2[l][None, :], lambda_k2[l][None, :], da_subln_g[l][None, :],
                                  lam_init, jax.ShapeDtypeStruct((n_slots // 2 * SUBLANES, LANES), U32))
        yhg = _hgrn(qd, kd, ke, hi, sg, dec, hg_norm_g[l][None, :])

        wr = jnp.concatenate([w_re[l].T, w_rg[l].T,
                              jnp.zeros((ROUTER_ROWS - N_EXPERTS - N_GROUPS, d), F32)], axis=0)
        wr_hi = wr.astype(BF16)
        wr_lo = (wr - wr_hi.astype(F32)).astype(BF16)
        br = jnp.concatenate([b_re[l], b_rg[l],
                              jnp.zeros((ROUTER_ROWS - N_EXPERTS - N_GROUPS,), F32)])[:, None]
        x1, h2, eidx, wts = _merge(yda, yhg, gda, ghg, x, mod, norm2_g[l][None, :],
                                   w_up_da[l].astype(BF16), w_up_hg[l].astype(BF16),
                                   w_out[l].astype(BF16), wr_hi, wr_lo, br)

        e_mat = jnp.transpose(eidx, (1, 0, 2)).reshape(2 * n_tok // LANES, LANES)
        half, blk_e, n_used = _positions(e_mat, n_blk_pad)
        half = half.reshape(2 * n_tok)
        xs = _dispatch(half, h2.reshape(b * h2.shape[1], LANES), xs_zero)
        ys = _experts(blk_e[0, :n_blk], n_used[0, :1], xs, w1_bf, w3_bf, w2_bf)
        wts_tok = jnp.transpose(wts, (0, 2, 1)).reshape(n_tok, 2)
        x = _combine(half, ys, x1.reshape(n_tok, d), mod, wts_tok, final_g[None, :], s,
                     last=(l == depth - 1)).reshape(b, s, d)
    return x
```

```python
import functools
import math

import jax
import jax.numpy as jnp
from jax import lax
from jax.experimental import pallas as pl
from jax.experimental.pallas import tpu as pltpu

F32 = jnp.float32
BF16 = jnp.bfloat16
I32 = jnp.int32
U32 = jnp.uint32

EPS = 1e-6
NEG = -1e30
LOG2E = math.log2(math.e)

CHUNK = 64
DA_HEADS = 4
DA_HEAD_DIM = 64
HG_HEADS = 4
HG_DK = 128
HG_DV = 128
N_GROUPS = 4
EXPERTS_PER_GROUP = 8
N_EXPERTS = N_GROUPS * EXPERTS_PER_GROUP

LANES = 128
SUBLANES = 8
ADALN_COLS = 1024
ROW_TILE = 512
ATT_BLOCK = 512
BIAS_TERMS = 3
EXPERT_BLOCK = 256
GATHER_TILE = 512
ISSUE_UNROLL = 8
COMBINE_UNROLL = 4
ROUTER_ROWS = 48
VMEM_LIMIT = 52 * 1024 * 1024


def _nt(a, b):
    return lax.dot_general(a, b, (((1,), (1,)), ((), ())), preferred_element_type=F32)


def _nn(a, b):
    return jnp.dot(a, b, preferred_element_type=F32)


def _split(a):
    hi = a.astype(BF16)
    lo = (a - hi.astype(F32)).astype(BF16)
    return hi, lo


def _sigmoid(x):
    return 1.0 / (1.0 + jnp.exp(-x))


def _silu(x):
    return x * _sigmoid(x)


def _pack_rows(y):
    rows, cols = y.shape[0] // 2, y.shape[1] // 2
    bits = lax.bitcast_convert_type(y.astype(BF16).astype(F32), U32)
    words = bits[:, :cols] | lax.shift_right_logical(bits[:, cols:], jnp.uint32(16))
    return jnp.concatenate([words[:rows], words[rows:]], axis=1)


def _unpack_rows(packed):
    cols = packed.shape[1] // 2
    words = jnp.concatenate([packed[:, :cols], packed[:, cols:]], axis=0)
    first = lax.bitcast_convert_type(words & jnp.uint32(0xFFFF0000), F32)
    second = lax.bitcast_convert_type(lax.shift_left(words, jnp.uint32(16)), F32)
    return jnp.concatenate([first, second], axis=1)


def _adaln_kernel(c_ref, w_ref, b_ref, o_ref):
    c_hi, c_lo = _split(_silu(c_ref[...]))
    w_hi, w_lo = _split(w_ref[...])
    o_ref[...] = _nn(c_hi, w_hi) + _nn(c_lo, w_hi) + _nn(c_hi, w_lo) + b_ref[...]


def _adaln(c_pad, w, b):
    rows, d = c_pad.shape
    n = w.shape[1]
    tn = ADALN_COLS
    return pl.pallas_call(
        _adaln_kernel,
        grid=(n // tn,),
        in_specs=[
            pl.BlockSpec((rows, d), lambda j: (0, 0)),
            pl.BlockSpec((d, tn), lambda j: (0, j)),
            pl.BlockSpec((1, tn), lambda j: (0, j)),
        ],
        out_specs=pl.BlockSpec((rows, tn), lambda j: (0, j)),
        out_shape=jax.ShapeDtypeStruct((rows, n), F32),
        name="adaln",
    )(c_pad, w, b)


def _inproj_kernel(x_ref, sh_ref, sc_ref, g_ref, w_ref, lb_ref, *refs, layer, n_cast):
    (qda_ref, kda_ref, vda_ref, qd_ref, kd_ref, ke_ref, hi_ref, sg_ref,
     dec_ref, gda_ref, ghg_ref) = refs[n_cast:-n_cast]
    for src_ref, dst_ref in zip(refs[:n_cast], refs[-n_cast:]):
        dst_ref[...] = src_ref[...].astype(BF16)

    x = x_ref[0]
    tm = x.shape[0]
    h = x * lax.rsqrt(jnp.mean(x * x, axis=-1, keepdims=True) + EPS) * g_ref[...]
    h = h * (1.0 + sc_ref[0]) + sh_ref[0]
    hb = h.astype(BF16)

    def proj(c0, width):
        return _nn(hb, w_ref[:, c0:c0 + width])

    w_da = DA_HEADS * 2 * DA_HEAD_DIM
    w_hg = HG_HEADS * HG_DK
    qda_ref[0] = (proj(0, w_da) * (LOG2E / math.sqrt(DA_HEAD_DIM))).astype(BF16)
    kda_ref[0] = proj(w_da, w_da).astype(BF16)
    vda_ref[0] = proj(2 * w_da, w_da).astype(BF16)
    c0 = 3 * w_da

    lbl = lb_ref[...]
    lbe = jnp.exp(lbl - jnp.max(lbl, axis=0, keepdims=True))
    lbs = lbe / jnp.sum(lbe, axis=0, keepdims=True)
    lb = jnp.sum(lbs[:layer + 1], axis=0, keepdims=True)

    f = lb + (1.0 - lb) * _sigmoid(proj(c0 + w_hg, w_hg))
    logf = jnp.log(f)
    kk = 1.0 - f
    pos = lax.broadcasted_iota(I32, logf.shape, 0) & (CHUNK - 1)
    a = logf
    step = 1
    while step < CHUNK:
        a = a + jnp.where(pos >= step, pltpu.roll(a, step, axis=0), 0.0)
        step *= 2
    n_ch = tm // CHUNK
    a = a.reshape(n_ch, CHUNK, w_hg)
    a_last = a[:, CHUNK - 1:CHUNK, :]
    kk3 = kk.reshape(n_ch, CHUNK, w_hg)
    qq3 = _silu(proj(c0, w_hg)).reshape(n_ch, CHUNK, w_hg)
    qd_ref[0] = (qq3 * jnp.exp(a)).reshape(tm, w_hg).astype(BF16)
    kd_ref[0] = (kk3 * jnp.exp(-a)).reshape(tm, w_hg).astype(BF16)
    ke_ref[0] = (kk3 * jnp.exp(a_last - a)).reshape(tm, w_hg).astype(BF16)
    dec_ref[0] = jnp.exp(a_last).reshape(n_ch, w_hg)
    hi_ref[0] = proj(c0 + 2 * w_hg, w_hg).astype(BF16)
    sg_ref[0] = _silu(proj(c0 + 3 * w_hg, w_hg)).astype(BF16)
    c1 = c0 + 4 * w_hg
    d = x.shape[1]
    gda_ref[0] = _sigmoid(proj(c1, d)).astype(BF16)
    ghg_ref[0] = _sigmoid(proj(c1 + d, d)).astype(BF16)


def _inproj(x, mod, g, w_bf, lb_logits, layer, to_cast):
    b, s, d = x.shape
    tm = ROW_TILE
    n_cols = w_bf.shape[1]
    per_b = s // tm
    assert all(a.shape[0] % (b * per_b) == 0 for a in to_cast)
    wspec = lambda a: pl.BlockSpec((a.shape[0] // (b * per_b),) + a.shape[1:],
                                   lambda bi, i: (bi * per_b + i, 0, 0))
    w_da = DA_HEADS * 2 * DA_HEAD_DIM
    w_hg = HG_HEADS * HG_DK

    def tok(width, dtype):
        return (pl.BlockSpec((1, tm, width), lambda bi, i: (bi, i, 0)),
                jax.ShapeDtypeStruct((b, s, width), dtype))

    outs = [tok(w_da, BF16)] * 3 + [tok(w_hg, BF16)] * 5
    outs.append((pl.BlockSpec((1, tm // CHUNK, w_hg), lambda bi, i: (bi, i, 0)),
                 jax.ShapeDtypeStruct((b, s // CHUNK, w_hg), F32)))
    outs += [tok(d, BF16)] * 2
    outs += [(wspec(a), jax.ShapeDtypeStruct(a.shape, BF16)) for a in to_cast]
    return pl.pallas_call(
        functools.partial(_inproj_kernel, layer=layer, n_cast=len(to_cast)),
        grid=(b, s // tm),
        in_specs=[
            pl.BlockSpec((1, tm, d), lambda bi, i: (bi, i, 0)),
            pl.BlockSpec((1, 1, d), lambda bi, i: (bi, 0, 0)),
            pl.BlockSpec((1, 1, d), lambda bi, i: (bi, 0, 1)),
            pl.BlockSpec((1, d), lambda bi, i: (0, 0)),
            pl.BlockSpec((d, n_cols), lambda bi, i: (0, 0), pipeline_mode=pl.Buffered(1)),
            pl.BlockSpec(lb_logits.shape, lambda bi, i: (0, 0)),
        ] + [wspec(a) for a in to_cast],
        out_specs=[o[0] for o in outs],
        out_shape=[o[1] for o in outs],
        compiler_params=pltpu.CompilerParams(vmem_limit_bytes=VMEM_LIMIT),
        name="inproj",
    )(x, mod, mod, g, w_bf, lb_logits, *to_cast)


def _attn_kernel(q_ref, k_ref, v_ref, lq1_ref, lk1_ref, lq2_ref, lk2_ref, g_ref, o_ref, zs_ref,
                 kaug_ref, vaug_ref, qaug_ref, m_ref, acc_ref, sa_ref, sb_ref, zero_ref, zsem,
                 *, lam_init, slopes):
    hd = pl.program_id(1)
    i = pl.program_id(2)

    step = (pl.program_id(0) * pl.num_programs(1) + hd) * pl.num_programs(2) + i
    n_steps = pl.num_programs(0) * pl.num_programs(1) * pl.num_programs(2)
    zrows = zero_ref.shape[0]

    def zero_copy(t):
        return pltpu.make_async_copy(zero_ref, zs_ref.at[pl.ds(t * zrows, zrows)], zsem)

    @pl.when(step == 0)
    def _():
        zero_ref[...] = jnp.zeros(zero_ref.shape, zero_ref.dtype)

    @pl.when(step > 0)
    def _():
        zero_copy(step - 1).wait()

    zero_copy(step).start()

    @pl.when(step == n_steps - 1)
    def _():
        zero_copy(step).wait()
    tq = q_ref.shape[1]
    hw = q_ref.shape[2]
    n_key = k_ref.shape[1]

    slope = jnp.float32(slopes[-1])
    for idx in range(len(slopes) - 2, -1, -1):
        slope = jnp.where(hd == idx, jnp.float32(slopes[idx]), slope)

    @pl.when(i == 0)
    def _():
        kaug_ref[:hw, :] = k_ref[0].astype(F32).T.astype(BF16)
        jrel = (lax.broadcasted_iota(I32, (hw, n_key), 1) & (tq - 1)).astype(F32) * slope
        krow = lax.broadcasted_iota(I32, (hw, n_key), 0)
        bias_rows = jnp.zeros((hw, n_key), F32)
        rest = jrel
        for term in range(BIAS_TERMS):
            part = rest.astype(BF16).astype(F32)
            bias_rows = jnp.where(krow == term, part, bias_rows)
            rest = rest - part
        kaug_ref[hw:, :] = bias_rows.astype(BF16)
        vaug_ref[:, :hw] = v_ref[0]
        vaug_ref[:, hw:] = jnp.ones((n_key, hw), BF16)

    q = q_ref[0]
    lane = lax.broadcasted_iota(I32, q.shape, 1)
    zero = jnp.zeros_like(q)
    ones = jnp.where(lane < BIAS_TERMS, 1.0, 0.0).astype(BF16)
    qaug_ref[:tq, :hw] = jnp.where(lane < DA_HEAD_DIM, q, zero)
    qaug_ref[tq:, :hw] = jnp.where(lane >= DA_HEAD_DIM, q, zero)
    qaug_ref[:tq, hw:] = ones
    qaug_ref[tq:, hw:] = ones

    m_ref[...] = jnp.full(m_ref.shape, NEG, F32)
    acc_ref[...] = jnp.zeros(acc_ref.shape, F32)

    def scores(j, buf):
        buf[...] = _nn(qaug_ref[...], kaug_ref[:, pl.ds(pl.multiple_of(j * tq, tq), tq)])

    def consume(j, buf, diagonal):
        s = buf[...]
        if diagonal:
            ii = lax.broadcasted_iota(I32, s.shape, 0) & (tq - 1)
            jj = lax.broadcasted_iota(I32, s.shape, 1)
            ahead = jnp.minimum(ii - jj, 0).astype(F32)
            s = s + jnp.where((jj // CHUNK) <= (ii // CHUNK), (2.0 * slope) * ahead, NEG)
            shift = jnp.float32(0.0)
        else:
            shift = -slope * ((i - j) * tq).astype(F32)
        m_old = m_ref[...]
        m_new = jnp.maximum(m_old, jnp.max(s, axis=-1, keepdims=True) + shift)
        p = jnp.exp2(s - jnp.concatenate([m_new - shift] * (tq // LANES), axis=1))
        alpha = jnp.concatenate([jnp.exp2(m_old - m_new)] * (acc_ref.shape[1] // LANES), axis=1)
        vb = vaug_ref[pl.ds(pl.multiple_of(j * tq, tq), tq), :]
        acc_ref[...] = alpha * acc_ref[...] + _nn(p.astype(BF16), vb)
        m_ref[...] = m_new

    scores(0, sa_ref)

    def pair(pi, carry):
        j = 2 * pi
        scores(j + 1, sb_ref)
        consume(j, sa_ref, False)
        scores(j + 2, sa_ref)
        consume(j + 1, sb_ref, False)
        return carry

    lax.fori_loop(0, i // 2, pair, 0)

    @pl.when(i % 2 == 0)
    def _():
        consume(i, sa_ref, True)

    @pl.when(i % 2 == 1)
    def _():
        scores(i, sb_ref)
        consume(i - 1, sa_ref, False)
        consume(i, sb_ref, True)

    acc = acc_ref[...]
    o1 = acc[:tq, :hw] / acc[:tq, hw:]
    o2 = acc[tq:, :hw] / acc[tq:, hw:]
    lam = (jnp.exp(jnp.sum(lq1_ref[...] * lk1_ref[...], axis=-1, keepdims=True))
           - jnp.exp(jnp.sum(lq2_ref[...] * lk2_ref[...], axis=-1, keepdims=True)) + lam_init)
    o = o1 - lam * o2
    o = o * lax.rsqrt(jnp.mean(o * o, axis=-1, keepdims=True) + EPS) * g_ref[...]
    o_ref[0] = (o * (1.0 - lam_init)).astype(BF16)


def _attention(q, k, v, lq1, lk1, lq2, lk2, subln_g, lam_init, zero_buf):
    b, s, _ = q.shape
    zero_shape = zero_buf.shape
    tq = ATT_BLOCK
    hw = 2 * DA_HEAD_DIM
    assert hw == LANES
    n_steps = b * DA_HEADS * (s // tq)
    assert zero_shape[0] % n_steps == 0
    zrows = zero_shape[0] // n_steps
    slopes = tuple(LOG2E * 2.0 ** (-8.0 * (h + 1) / DA_HEADS) for h in range(DA_HEADS))
    vec = lambda n: pl.BlockSpec((1, n), lambda bi, h, i: (0, 0))
    return pl.pallas_call(
        functools.partial(_attn_kernel, lam_init=lam_init, slopes=slopes),
        grid=(b, DA_HEADS, s // tq),
        in_specs=[
            pl.BlockSpec((1, tq, hw), lambda bi, h, i: (bi, i, h)),
            pl.BlockSpec((1, s, hw), lambda bi, h, i: (bi, 0, h)),
            pl.BlockSpec((1, s, hw), lambda bi, h, i: (bi, 0, h)),
            vec(DA_HEAD_DIM), vec(DA_HEAD_DIM), vec(DA_HEAD_DIM), vec(DA_HEAD_DIM), vec(hw),
        ],
        out_specs=[pl.BlockSpec((1, tq, hw), lambda bi, h, i: (bi, i, h)),
                   pl.BlockSpec(memory_space=pl.ANY)],
        out_shape=[jax.ShapeDtypeStruct((b, s, DA_HEADS * hw), BF16),
                   zero_buf],
        scratch_shapes=[
            pltpu.VMEM((2 * hw, s), BF16),
            pltpu.VMEM((s, 2 * hw), BF16),
            pltpu.VMEM((2 * tq, 2 * hw), BF16),
            pltpu.VMEM((2 * tq, LANES), F32),
            pltpu.VMEM((2 * tq, 2 * hw), F32),
            pltpu.VMEM((2 * tq, tq), F32),
            pltpu.VMEM((2 * tq, tq), F32),
            pltpu.VMEM((zrows,) + tuple(zero_shape[1:]), zero_buf.dtype),
            pltpu.SemaphoreType.DMA(()),
        ],
        compiler_params=pltpu.CompilerParams(vmem_limit_bytes=VMEM_LIMIT,
                                             dimension_semantics=("arbitrary",) * 3),
        name="attention",
    )(q, k, v, lq1, lk1, lq2, lk2, subln_g)


def _hgrn_kernel(qd_ref, kd_ref, ke_ref, v_ref, sg_ref, dec_ref, g_ref, o_ref, st_ref):
    @pl.when(pl.program_id(1) == 0)
    def _():
        st_ref[...] = jnp.zeros(st_ref.shape, F32)

    n_ch = qd_ref.shape[1] // CHUNK
    r = lax.broadcasted_iota(I32, (CHUNK, CHUNK), 0)
    c = lax.broadcasted_iota(I32, (CHUNK, CHUNK), 1)
    causal = r >= c

    heads = [slice(h * HG_DK, (h + 1) * HG_DK) for h in range(HG_HEADS)]
    chunks = [slice(ci * CHUNK, (ci + 1) * CHUNK) for ci in range(n_ch)]
    incr = [[_nn(v_ref[0, rows, cols].astype(F32).T.astype(BF16), ke_ref[0, rows, cols])
             for rows in chunks] for cols in heads]
    start = []
    for h, cols in enumerate(heads):
        st = st_ref[h]
        before = []
        for ci in range(n_ch):
            before.append(st.astype(BF16))
            st = st * dec_ref[0, ci:ci + 1, cols] + incr[h][ci]
        st_ref[h] = st
        start.append(before)
    for h, cols in enumerate(heads):
        for ci, rows in enumerate(chunks):
            qd = qd_ref[0, rows, cols]
            scores = jnp.where(causal, _nt(qd, kd_ref[0, rows, cols]), 0.0).astype(BF16)
            o = _nn(scores, v_ref[0, rows, cols]) + _nt(qd, start[h][ci])
            o = o * lax.rsqrt(jnp.mean(o * o, axis=-1, keepdims=True) + EPS) * g_ref[...]
            o_ref[0, rows, cols] = (o * sg_ref[0, rows, cols].astype(F32)).astype(BF16)


def _hgrn(qd, kd, ke, v, sg, dec, norm_g):
    b, s, w = qd.shape
    ts = ROW_TILE
    tok = pl.BlockSpec((1, ts, w), lambda bi, i: (bi, i, 0))
    return pl.pallas_call(
        _hgrn_kernel,
        grid=(b, s // ts),
        in_specs=[tok, tok, tok, tok, tok,
                  pl.BlockSpec((1, ts // CHUNK, w), lambda bi, i: (bi, i, 0)),
                  pl.BlockSpec((1, HG_DV), lambda bi, i: (0, 0))],
        out_specs=tok,
        out_shape=jax.ShapeDtypeStruct((b, s, w), BF16),
        scratch_shapes=[pltpu.VMEM((HG_HEADS, HG_DV, HG_DK), F32)],
        name="hgrn",
    )(qd, kd, ke, v, sg, dec, norm_g)


def _merge_kernel(yda_ref, yhg_ref, gda_ref, ghg_ref, x_ref, gt1_ref, sh2_ref, sc2_ref, g2_ref,
                  wda_ref, whg_ref, wout_ref, wrh_ref, wrl_ref, br_ref,
                  x1_ref, h2_ref, eidx_ref, wts_ref):
    u = (gda_ref[0].astype(F32) * _nn(yda_ref[0], wda_ref[...])
         + ghg_ref[0].astype(F32) * _nn(yhg_ref[0], whg_ref[...]))
    x1 = x_ref[0] + gt1_ref[0] * _nn(u.astype(BF16), wout_ref[...])
    x1_ref[0] = x1
    h2 = x1 * lax.rsqrt(jnp.mean(x1 * x1, axis=-1, keepdims=True) + EPS) * g2_ref[...]
    h2 = h2 * (1.0 + sc2_ref[0]) + sh2_ref[0]
    h2_ref[0] = _pack_rows(h2).reshape(h2.shape[0] // 2, SUBLANES, LANES).reshape(h2_ref.shape[1:])

    h_hi, h_lo = _split(h2)
    wrh = wrh_ref[...]
    both = _nt(jnp.concatenate([wrh, wrl_ref[...]], axis=0), h_hi)
    lg = both[:ROUTER_ROWS] + both[ROUTER_ROWS:] + _nt(wrh, h_lo) + br_ref[...]

    tm = h2.shape[0]
    eg = EXPERTS_PER_GROUP
    gl = lg[N_EXPERTS:N_EXPERTS + N_GROUPS]
    gmax = jnp.max(gl, axis=0, keepdims=True)
    g_p = 1.0 / jnp.sum(jnp.exp(gl - gmax), axis=0, keepdims=True)
    gi = lax.broadcasted_iota(I32, (N_GROUPS, tm), 0)
    g_idx = jnp.min(jnp.where(gl == gmax, gi, N_GROUPS), axis=0, keepdims=True)
    sel = lg[(N_GROUPS - 1) * eg:N_GROUPS * eg]
    for g in range(N_GROUPS - 2, -1, -1):
        sel = jnp.where(g_idx == g, lg[g * eg:(g + 1) * eg], sel)
    ei = lax.broadcasted_iota(I32, (eg, tm), 0)
    m1 = jnp.max(sel, axis=0, keepdims=True)
    i1 = jnp.min(jnp.where(sel == m1, ei, eg), axis=0, keepdims=True)
    sel2 = jnp.where(ei == i1, -jnp.inf, sel)
    m2 = jnp.max(sel2, axis=0, keepdims=True)
    i2 = jnp.min(jnp.where(sel2 == m2, ei, eg), axis=0, keepdims=True)
    t = jnp.exp(m2 - m1)
    wa = 1.0 / (1.0 + t)
    eidx_ref[0] = jnp.concatenate([g_idx * eg + i1, g_idx * eg + i2], axis=0)
    wts_ref[0] = jnp.concatenate([g_p * wa, g_p * (t * wa)], axis=0)


def _merge(yda, yhg, gda, ghg, x, mod, g2, wda, whg, wout, wrh, wrl, br):
    b, s, d = x.shape
    tm = ROW_TILE
    w = yda.shape[2]
    tok = lambda width: pl.BlockSpec((1, tm, width), lambda bi, i: (bi, i, 0))
    modspec = lambda k: pl.BlockSpec((1, 1, d), lambda bi, i: (bi, 0, k))
    full = lambda a: pl.BlockSpec(a.shape, lambda bi, i: (0,) * a.ndim)
    lane_rows = pl.BlockSpec((1, 2, tm), lambda bi, i: (bi, 0, i))
    return pl.pallas_call(
        _merge_kernel,
        grid=(b, s // tm),
        in_specs=[tok(w), tok(w), tok(d), tok(d), tok(d),
                  modspec(2), modspec(3), modspec(4), full(g2),
                  full(wda), full(whg), full(wout), full(wrh), full(wrl), full(br)],
        out_specs=[tok(d), pl.BlockSpec((1, tm // 2 * SUBLANES, LANES), lambda bi, i: (bi, i, 0)),
                   lane_rows, lane_rows],
        out_shape=[jax.ShapeDtypeStruct((b, s, d), F32),
                   jax.ShapeDtypeStruct((b, s // 2 * SUBLANES, LANES), U32),
                   jax.ShapeDtypeStruct((b, 2, s), I32), jax.ShapeDtypeStruct((b, 2, s), F32)],
        compiler_params=pltpu.CompilerParams(vmem_limit_bytes=VMEM_LIMIT),
        name="merge",
    )(yda, yhg, gda, ghg, x, mod, mod, mod, g2, wda, whg, wout, wrh, wrl, br)


def _positions_kernel(e_ref, half_ref, blk_ref, nused_ref):
    e_mat = e_ref[...]
    nr = e_mat.shape[0]
    nb = blk_ref.shape[1]
    r = lax.broadcasted_iota(I32, (LANES, LANES), 0)
    c = lax.broadcasted_iota(I32, (LANES, LANES), 1)
    incl = jnp.where(r <= c, 1.0, 0.0).astype(BF16)
    ones = jnp.ones((LANES, LANES), BF16)
    rr = lax.broadcasted_iota(I32, (nr, nr), 0)
    cc = lax.broadcasted_iota(I32, (nr, nr), 1)
    before = jnp.where(cc < rr, 1.0, 0.0).astype(BF16)
    blk_start = (lax.broadcasted_iota(I32, (1, nb), 1) * EXPERT_BLOCK).astype(F32)

    def per_expert(e, carry):
        pstart, dest, blk = carry
        hit = e_mat == e
        oh = jnp.where(hit, 1.0, 0.0).astype(BF16)
        in_row = _nn(oh, incl)
        row_tot = _nn(oh, ones)
        row_off = _nn(before, row_tot.astype(BF16))
        total = row_off[nr - 1:nr, :] + row_tot[nr - 1:nr, :]
        dest = jnp.where(hit, pstart + row_off + in_row - 1.0, dest)
        pend = pstart + jnp.floor((total + (EXPERT_BLOCK - 1)) * (1.0 / EXPERT_BLOCK)) * EXPERT_BLOCK
        blk = blk + jnp.where(blk_start >= pend[:, :1], 1.0, 0.0)
        return pend, dest, blk

    pstart0 = jnp.zeros((1, LANES), F32)
    pend, dest, blk = lax.fori_loop(
        0, N_EXPERTS, per_expert,
        (pstart0, jnp.zeros(e_mat.shape, F32), jnp.zeros((1, nb), F32)), unroll=4)
    half_rows = EXPERT_BLOCK // 2
    block = jnp.floor(dest * (1.0 / EXPERT_BLOCK))
    in_block = dest - block * EXPERT_BLOCK
    upper = jnp.floor(in_block * (1.0 / half_rows))
    tile = block * half_rows + in_block - upper * half_rows
    half_ref[...] = (SUBLANES * tile + (SUBLANES // 2) * upper).astype(I32)
    blk_ref[...] = jnp.minimum(blk, N_EXPERTS - 1).astype(I32)
    nused_ref[...] = (pend * (1.0 / EXPERT_BLOCK)).astype(I32)


def _positions(e_mat, n_blk_pad):
    nr = e_mat.shape[0]
    return pl.pallas_call(
        _positions_kernel,
        out_shape=[jax.ShapeDtypeStruct((nr, LANES), I32),
                   jax.ShapeDtypeStruct((1, n_blk_pad), I32),
                   jax.ShapeDtypeStruct((1, LANES), I32)],
        name="positions",
    )(e_mat)


def _dispatch_kernel(half_ref, h2_ref, xs_in_ref, xs_ref, hbuf, in_sem, out_sem, *, n_tok):
    del xs_in_ref
    i = pl.program_id(0)
    last = pl.num_programs(0) - 1
    slot = i % 2
    half = SUBLANES // 2
    pairs = GATHER_TILE // 2
    tile_rows = pairs * SUBLANES

    def load(step, buf):
        return pltpu.make_async_copy(h2_ref.at[pl.ds(step * tile_rows, tile_rows)], hbuf.at[buf],
                                     in_sem.at[buf])

    def drain(buf):
        for k in range(2):
            pltpu.make_async_copy(hbuf.at[buf], xs_ref.at[pl.ds(0, tile_rows)], out_sem.at[buf, k]).wait()

    @pl.when(i == 0)
    def _():
        load(0, 0).start()

    @pl.when(i > 0)
    def _():
        drain(1 - slot)

    @pl.when(i < last)
    def _():
        load(i + 1, 1 - slot).start()

    load(i, slot).wait()

    for up in range(2):
        base = i * GATHER_TILE + up * pairs

        def issue(r, carry):
            for k in range(2):
                row = pl.multiple_of(half_ref[k * n_tok + base + r], half)
                pltpu.make_async_copy(hbuf.at[slot, pl.ds(r * SUBLANES + up * half, half)],
                                      xs_ref.at[pl.ds(row, half)], out_sem.at[slot, k]).start(priority=k)
            return carry

        lax.fori_loop(0, pairs, issue, 0, unroll=ISSUE_UNROLL)

    @pl.when(i == last)
    def _():
        drain(slot)


def _dispatch(half, h2, xs_zero):
    n_tok = h2.shape[0] // (SUBLANES // 2)
    return pl.pallas_call(
        functools.partial(_dispatch_kernel, n_tok=n_tok),
        grid_spec=pltpu.PrefetchScalarGridSpec(
            num_scalar_prefetch=1,
            grid=(n_tok // GATHER_TILE,),
            in_specs=[pl.BlockSpec(memory_space=pl.ANY), pl.BlockSpec(memory_space=pl.ANY)],
            out_specs=pl.BlockSpec(memory_space=pl.ANY),
            scratch_shapes=[pltpu.VMEM((2, GATHER_TILE // 2 * SUBLANES, LANES), h2.dtype),
                            pltpu.SemaphoreType.DMA((2,)), pltpu.SemaphoreType.DMA((2, 2))],
        ),
        out_shape=jax.ShapeDtypeStruct(xs_zero.shape, xs_zero.dtype),
        input_output_aliases={2: 0},
        compiler_params=pltpu.CompilerParams(dimension_semantics=("arbitrary",)),
        name="dispatch",
    )(half, h2, xs_zero)


def _expert_kernel(blk_ref, nused_ref, xs_ref, w1_ref, w3_ref, w2_ref, ys_ref,
                   w1b, w3b, w2b, ybuf, cur, sem):
    i = pl.program_id(0)
    n_used = nused_ref[0]
    e = blk_ref[jnp.minimum(i, pl.num_programs(0) - 2)]
    fresh = (i == 0) | (blk_ref[jnp.maximum(i - 1, 0)] != e)

    def fetch(expert, which):
        return (pltpu.make_async_copy(w1_ref.at[expert], w1b.at[which], sem.at[0]),
                pltpu.make_async_copy(w3_ref.at[expert], w3b.at[which], sem.at[1]),
                pltpu.make_async_copy(w2_ref.at[expert], w2b.at[which], sem.at[2]))

    @pl.when(i < n_used)
    def _():
        @pl.when(i == 0)
        def _():
            cur[0] = 0
            for cp in fetch(e, 0):
                cp.start()

        @pl.when(fresh & (i > 0))
        def _():
            cur[0] = 1 - cur[0]

        @pl.when(fresh)
        def _():
            for cp in fetch(e, cur[0]):
                cp.wait()
            nxt = lax.while_loop(lambda j: (j < n_used) & (blk_ref[jnp.minimum(j, n_used - 1)] == e),
                                 lambda j: j + 1, i + 1)

            @pl.when(nxt < n_used)
            def _():
                for cp in fetch(blk_ref[jnp.minimum(nxt, n_used - 1)], 1 - cur[0]):
                    cp.start(priority=1)

    tiles = (xs_ref.shape[0] // SUBLANES, SUBLANES, LANES)

    def compute():
        which = cur[0]
        x = _unpack_rows(xs_ref[...].reshape(tiles).reshape(tiles[0], w1b.shape[1])).astype(BF16)
        a = _nn(x, w1b[which])
        ybuf[...] = _nn((_silu(a) * _nn(x, w3b[which])).astype(BF16), w2b[which])

    def flush():
        ys_ref[...] = _pack_rows(ybuf[...]).reshape(tiles).reshape(ys_ref.shape)

    @pl.when(i == 0)
    def _():
        compute()

    @pl.when((i > 0) & (i < n_used))
    def _():
        flush()
        compute()

    @pl.when((i > 0) & (i == n_used))
    def _():
        flush()

    @pl.when(i > n_used)
    def _():
        ys_ref[...] = jnp.zeros(ys_ref.shape, ys_ref.dtype)


def _experts(blk_e, n_used, xs, w1, w3, w2):
    block_rows = EXPERT_BLOCK // 2 * SUBLANES
    d, de = w1.shape[1], w1.shape[2]
    rows = lambda i, blk, nu: (jnp.minimum(i, nu[0] - 1), 0)
    hbm = pl.BlockSpec(memory_space=pl.ANY)
    return pl.pallas_call(
        _expert_kernel,
        grid_spec=pltpu.PrefetchScalarGridSpec(
            num_scalar_prefetch=2,
            grid=(xs.shape[0] // block_rows + 1,),
            in_specs=[pl.BlockSpec((block_rows, LANES), rows), hbm, hbm, hbm],
            out_specs=pl.BlockSpec((block_rows, LANES), lambda i, blk, nu: (jnp.maximum(i - 1, 0), 0)),
            scratch_shapes=[pltpu.VMEM((2, d, de), BF16), pltpu.VMEM((2, d, de), BF16),
                            pltpu.VMEM((2, de, d), BF16), pltpu.VMEM((EXPERT_BLOCK, d), F32),
                            pltpu.SMEM((1,), I32), pltpu.SemaphoreType.DMA((3,))],
        ),
        out_shape=jax.ShapeDtypeStruct(xs.shape, xs.dtype),
        compiler_params=pltpu.CompilerParams(vmem_limit_bytes=VMEM_LIMIT,
                                             dimension_semantics=("arbitrary",)),
        name="experts",
    )(blk_e, n_used, xs, w1, w3, w2)


def _combine_kernel(half_ref, ys_ref, x1_ref, gt2_ref, wts_ref, g_ref, o_ref, ybuf, sem, *, n_tok, last):
    i = pl.program_id(0)
    last_step = pl.num_programs(0) - 1
    slot = i % 2
    d = x1_ref.shape[1]
    half = SUBLANES // 2
    pairs = GATHER_TILE // 2
    n_groups = pairs // ISSUE_UNROLL

    def gather(step, buf, j0):
        for up in range(2):
            base = step * GATHER_TILE + up * pairs + j0
            for u in range(ISSUE_UNROLL):
                for k in range(2):
                    row = pl.multiple_of(half_ref[k * n_tok + base + u], half)
                    pltpu.make_async_copy(ys_ref.at[pl.ds(row, half)],
                                          ybuf.at[buf, k, pl.ds((j0 + u) * SUBLANES + up * half, half)],
                                          sem.at[buf, k]).start(priority=k)

    def drain(buf):
        for k in range(2):
            pltpu.make_async_copy(ys_ref.at[pl.ds(0, pairs * SUBLANES)], ybuf.at[buf, k], sem.at[buf, k]).wait()

    @pl.when(i == 0)
    def _():
        def first(g, carry):
            gather(0, 0, g * ISSUE_UNROLL)
            return carry

        lax.fori_loop(0, n_groups, first, 0)

    drain(slot)

    nxt = jnp.minimum(i + 1, last_step)
    gt2 = gt2_ref[0]
    gain = g_ref[...]

    def group(g, carry):
        j0 = pl.multiple_of(g * ISSUE_UNROLL, ISSUE_UNROLL)
        lower, upper = pl.ds(j0, ISSUE_UNROLL), pl.ds(pairs + j0, ISSUE_UNROLL)
        tile_rows = pl.ds(pl.multiple_of(j0 * SUBLANES, ISSUE_UNROLL * SUBLANES), ISSUE_UNROLL * SUBLANES)

        def tokens(k):
            tiles = ybuf[slot, k, tile_rows, :].reshape(ISSUE_UNROLL, SUBLANES, LANES)
            return _unpack_rows(tiles.reshape(ISSUE_UNROLL, d))

        w = jnp.concatenate([wts_ref[lower, :], wts_ref[upper, :]], axis=0)
        x = jnp.concatenate([x1_ref[lower, :], x1_ref[upper, :]], axis=0)
        x = x + gt2 * (w[:, 0:1] * tokens(0) + w[:, 1:2] * tokens(1))
        if last:
            x = x * lax.rsqrt(jnp.mean(x * x, axis=-1, keepdims=True) + EPS) * gain
        o_ref[lower, :] = x[:ISSUE_UNROLL]
        o_ref[upper, :] = x[ISSUE_UNROLL:]
        gather(nxt, 1 - slot, j0)
        return carry

    lax.fori_loop(0, n_groups, group, 0, unroll=COMBINE_UNROLL)

    @pl.when(i == last_step)
    def _():
        drain(1 - slot)


def _combine(half, ys, x1, mod, wts, final_g, seq, last=True):
    n_tok, d = x1.shape
    tt = GATHER_TILE
    per_batch = seq // tt
    return pl.pallas_call(
        functools.partial(_combine_kernel, n_tok=n_tok, last=last),
        grid_spec=pltpu.PrefetchScalarGridSpec(
            num_scalar_prefetch=1,
            grid=(n_tok // tt,),
            in_specs=[pl.BlockSpec(memory_space=pl.ANY),
                      pl.BlockSpec((tt, d), lambda i, half: (i, 0)),
                      pl.BlockSpec((1, 1, d), lambda i, half: (i // per_batch, 0, 5)),
                      pl.BlockSpec((tt, 2), lambda i, half: (i, 0)),
                      pl.BlockSpec((1, d), lambda i, half: (0, 0))],
            out_specs=pl.BlockSpec((tt, d), lambda i, half: (i, 0)),
            scratch_shapes=[pltpu.VMEM((2, 2, tt // 2 * SUBLANES, LANES), U32),
                            pltpu.SemaphoreType.DMA((2, 2))],
        ),
        out_shape=jax.ShapeDtypeStruct((n_tok, d), F32),
        compiler_params=pltpu.CompilerParams(dimension_semantics=("arbitrary",)),
        name="combine",
    )(half, ys, x1, mod, wts, final_g)


def kernel(x, c, w_ada, b_ada, norm1_g, w_in, lambda_q1, lambda_k1, lambda_q2, lambda_k2, da_subln_g, hg_lb_logits, hg_norm_g, w_up_da, w_up_hg, w_out, norm2_g, w_rg, b_rg, w_re, b_re, w1, w3, w2, final_g):
    b, s, d = x.shape
    n_tok = b * s
    depth = w_ada.shape[0]
    assert s % ROW_TILE == 0 and s % ATT_BLOCK == 0 and GATHER_TILE == ROW_TILE
    assert (2 * n_tok) % LANES == 0 and d == SUBLANES * LANES

    c_pad = jnp.pad(c, ((0, (-b) % 8), (0, 0)))
    n_slots = 2 * n_tok + N_EXPERTS * EXPERT_BLOCK
    n_blk = n_slots // EXPERT_BLOCK
    n_blk_pad = -(-n_blk // LANES) * LANES

    for l in range(depth):
        lam_init = 0.8 - 0.6 * math.exp(-0.3 * l)
        mod = _adaln(c_pad, w_ada[l], b_ada[l][None, :])[:b].reshape(b, 1, 6 * d)

        n_steps = n_tok // ROW_TILE
        sliced = lambda a: a.reshape((n_steps, a.shape[0] // n_steps) + a.shape[1:])
        (qda, kda, vda, qd, kd, ke, hi, sg, dec, gda, ghg,
         w1_bf, w3_bf, w2_bf, wda_bf, whg_bf, wout_bf) = _inproj(
            x, mod, norm1_g[l][None, :], w_in[l].astype(BF16), hg_lb_logits, l,
            (w1[l], w3[l], w2[l], sliced(w_up_da[l]), sliced(w_up_hg[l]), sliced(w_out[l])))
        yda, xs_zero = _attention(qda, kda, vda, lambda_q1[l][None, :], lambda_k1[l][None, :],
                                  lambda_q2[l][None, :], lambda_k2[l][None, :], da_subln_g[l][None, :],
                                  lam_init, jax.ShapeDtypeStruct((n_slots // 2 * SUBLANES, LANES), U32))
        yhg = _hgrn(qd, kd, ke, hi, sg, dec, hg_norm_g[l][None, :])

        wr = jnp.concatenate([w_re[l].T, w_rg[l].T,
                              jnp.zeros((ROUTER_ROWS - N_EXPERTS - N_GROUPS, d), F32)], axis=0)
        wr_hi = wr.astype(BF16)
        wr_lo = (wr - wr_hi.astype(F32)).astype(BF16)
        br = jnp.concatenate([b_re[l], b_rg[l],
                              jnp.zeros((ROUTER_ROWS - N_EXPERTS - N_GROUPS,), F32)])[:, None]
        x1, h2, eidx, wts = _merge(yda, yhg, gda, ghg, x, mod, norm2_g[l][None, :],
                                   wda_bf.reshape(w_up_da[l].shape), whg_bf.reshape(w_up_hg[l].shape),
                                   wout_bf.reshape(w_out[l].shape), wr_hi, wr_lo, br)

        e_mat = jnp.transpose(eidx, (1, 0, 2)).reshape(2 * n_tok // LANES, LANES)
        half, blk_e, n_used = _positions(e_mat, n_blk_pad)
        half = half.reshape(2 * n_tok)
        xs = _dispatch(half, h2.reshape(b * h2.shape[1], LANES), xs_zero)
        ys = _experts(blk_e[0, :n_blk], n_used[0, :1], xs, w1_bf, w3_bf, w2_bf)
        wts_tok = jnp.transpose(wts, (0, 2, 1)).reshape(n_tok, 2)
        x = _combine(half, ys, x1.reshape(n_tok, d), mod, wts_tok, final_g[None, :], s,
                     last=(l == depth - 1)).reshape(b, s, d)
    return x
```

```python
import functools
import math

import jax
import jax.numpy as jnp
from jax import lax
from jax.experimental import pallas as pl
from jax.experimental.pallas import tpu as pltpu

F32 = jnp.float32
BF16 = jnp.bfloat16
I32 = jnp.int32
U32 = jnp.uint32

EPS = 1e-6
NEG = -1e30
LOG2E = math.log2(math.e)

CHUNK = 64
DA_HEADS = 4
DA_HEAD_DIM = 64
HG_HEADS = 4
HG_DK = 128
HG_DV = 128
N_GROUPS = 4
EXPERTS_PER_GROUP = 8
N_EXPERTS = N_GROUPS * EXPERTS_PER_GROUP

LANES = 128
SUBLANES = 8
ADALN_COLS = 1024
ROW_TILE = 512
ATT_BLOCK = 512
BIAS_TERMS = 3
EXPERT_BLOCK = 256
GATHER_TILE = 512
ISSUE_UNROLL = 8
COMBINE_UNROLL = 4
ROUTER_ROWS = 48
VMEM_LIMIT = 52 * 1024 * 1024


def _nt(a, b):
    return lax.dot_general(a, b, (((1,), (1,)), ((), ())), preferred_element_type=F32)


def _nn(a, b):
    return jnp.dot(a, b, preferred_element_type=F32)


def _split(a):
    hi = a.astype(BF16)
    lo = (a - hi.astype(F32)).astype(BF16)
    return hi, lo


def _sigmoid(x):
    return 1.0 / (1.0 + jnp.exp(-x))


def _silu(x):
    return x * _sigmoid(x)


def _pack_rows(y):
    rows, cols = y.shape[0] // 2, y.shape[1] // 2
    bits = lax.bitcast_convert_type(y.astype(BF16).astype(F32), U32)
    words = bits[:, :cols] | lax.shift_right_logical(bits[:, cols:], jnp.uint32(16))
    return jnp.concatenate([words[:rows], words[rows:]], axis=1)


def _unpack_rows(packed):
    cols = packed.shape[1] // 2
    words = jnp.concatenate([packed[:, :cols], packed[:, cols:]], axis=0)
    first = lax.bitcast_convert_type(words & jnp.uint32(0xFFFF0000), F32)
    second = lax.bitcast_convert_type(lax.shift_left(words, jnp.uint32(16)), F32)
    return jnp.concatenate([first, second], axis=1)


def _adaln_kernel(c_ref, w_ref, b_ref, o_ref):
    c_hi, c_lo = _split(_silu(c_ref[...]))
    w_hi, w_lo = _split(w_ref[...])
    o_ref[...] = _nn(c_hi, w_hi) + _nn(c_lo, w_hi) + _nn(c_hi, w_lo) + b_ref[...]


def _adaln(c_pad, w, b):
    rows, d = c_pad.shape
    n = w.shape[1]
    tn = ADALN_COLS
    return pl.pallas_call(
        _adaln_kernel,
        grid=(n // tn,),
        in_specs=[
            pl.BlockSpec((rows, d), lambda j: (0, 0)),
            pl.BlockSpec((d, tn), lambda j: (0, j)),
            pl.BlockSpec((1, tn), lambda j: (0, j)),
        ],
        out_specs=pl.BlockSpec((rows, tn), lambda j: (0, j)),
        out_shape=jax.ShapeDtypeStruct((rows, n), F32),
        name="adaln",
    )(c_pad, w, b)


def _inproj_kernel(x_ref, sh_ref, sc_ref, g_ref, w_ref, lb_ref, *refs, layer, n_cast):
    (qda_ref, kda_ref, vda_ref, qd_ref, kd_ref, ke_ref, hi_ref, sg_ref,
     dec_ref, gda_ref, ghg_ref) = refs[n_cast:-n_cast]
    for src_ref, dst_ref in zip(refs[:n_cast], refs[-n_cast:]):
        dst_ref[...] = src_ref[...].astype(BF16)

    x = x_ref[0]
    tm = x.shape[0]
    h = x * lax.rsqrt(jnp.mean(x * x, axis=-1, keepdims=True) + EPS) * g_ref[...]
    h = h * (1.0 + sc_ref[0]) + sh_ref[0]
    hb = h.astype(BF16)

    def proj(c0, width):
        return _nn(hb, w_ref[:, c0:c0 + width])

    w_da = DA_HEADS * 2 * DA_HEAD_DIM
    w_hg = HG_HEADS * HG_DK
    qda_ref[0] = (proj(0, w_da) * (LOG2E / math.sqrt(DA_HEAD_DIM))).astype(BF16)
    kda_ref[0] = proj(w_da, w_da).astype(BF16)
    vda_ref[0] = proj(2 * w_da, w_da).astype(BF16)
    c0 = 3 * w_da

    lbl = lb_ref[...]
    lbe = jnp.exp(lbl - jnp.max(lbl, axis=0, keepdims=True))
    lbs = lbe / jnp.sum(lbe, axis=0, keepdims=True)
    lb = jnp.sum(lbs[:layer + 1], axis=0, keepdims=True)

    f = lb + (1.0 - lb) * _sigmoid(proj(c0 + w_hg, w_hg))
    logf = jnp.log(f)
    kk = 1.0 - f
    pos = lax.broadcasted_iota(I32, logf.shape, 0) & (CHUNK - 1)
    a = logf
    step = 1
    while step < CHUNK:
        a = a + jnp.where(pos >= step, pltpu.roll(a, step, axis=0), 0.0)
        step *= 2
    n_ch = tm // CHUNK
    a = a.reshape(n_ch, CHUNK, w_hg)
    a_last = a[:, CHUNK - 1:CHUNK, :]
    kk3 = kk.reshape(n_ch, CHUNK, w_hg)
    qq3 = _silu(proj(c0, w_hg)).reshape(n_ch, CHUNK, w_hg)
    qd_ref[0] = (qq3 * jnp.exp(a)).reshape(tm, w_hg).astype(BF16)
    kd_ref[0] = (kk3 * jnp.exp(-a)).reshape(tm, w_hg).astype(BF16)
    ke_ref[0] = (kk3 * jnp.exp(a_last - a)).reshape(tm, w_hg).astype(BF16)
    dec_ref[0] = jnp.exp(a_last).reshape(n_ch, w_hg)
    hi_ref[0] = proj(c0 + 2 * w_hg, w_hg).astype(BF16)
    sg_ref[0] = _silu(proj(c0 + 3 * w_hg, w_hg)).astype(BF16)
    c1 = c0 + 4 * w_hg
    d = x.shape[1]
    gda_ref[0] = _sigmoid(proj(c1, d)).astype(BF16)
    ghg_ref[0] = _sigmoid(proj(c1 + d, d)).astype(BF16)


def _inproj(x, mod, g, w_bf, lb_logits, layer, to_cast):
    b, s, d = x.shape
    tm = ROW_TILE
    n_cols = w_bf.shape[1]
    per_b = s // tm
    assert all(a.shape[0] % (b * per_b) == 0 for a in to_cast)
    wspec = lambda a: pl.BlockSpec((a.shape[0] // (b * per_b),) + a.shape[1:],
                                   lambda bi, i: (bi * per_b + i, 0, 0))
    w_da = DA_HEADS * 2 * DA_HEAD_DIM
    w_hg = HG_HEADS * HG_DK

    def tok(width, dtype):
        return (pl.BlockSpec((1, tm, width), lambda bi, i: (bi, i, 0)),
                jax.ShapeDtypeStruct((b, s, width), dtype))

    outs = [tok(w_da, BF16)] * 3 + [tok(w_hg, BF16)] * 5
    outs.append((pl.BlockSpec((1, tm // CHUNK, w_hg), lambda bi, i: (bi, i, 0)),
                 jax.ShapeDtypeStruct((b, s // CHUNK, w_hg), F32)))
    outs += [tok(d, BF16)] * 2
    outs += [(wspec(a), jax.ShapeDtypeStruct(a.shape, BF16)) for a in to_cast]
    return pl.pallas_call(
        functools.partial(_inproj_kernel, layer=layer, n_cast=len(to_cast)),
        grid=(b, s // tm),
        in_specs=[
            pl.BlockSpec((1, tm, d), lambda bi, i: (bi, i, 0)),
            pl.BlockSpec((1, 1, d), lambda bi, i: (bi, 0, 0)),
            pl.BlockSpec((1, 1, d), lambda bi, i: (bi, 0, 1)),
            pl.BlockSpec((1, d), lambda bi, i: (0, 0)),
            pl.BlockSpec((d, n_cols), lambda bi, i: (0, 0), pipeline_mode=pl.Buffered(1)),
            pl.BlockSpec(lb_logits.shape, lambda bi, i: (0, 0)),
        ] + [wspec(a) for a in to_cast],
        out_specs=[o[0] for o in outs],
        out_shape=[o[1] for o in outs],
        compiler_params=pltpu.CompilerParams(vmem_limit_bytes=VMEM_LIMIT),
        name="inproj",
    )(x, mod, mod, g, w_bf, lb_logits, *to_cast)


def _attn_kernel(q_ref, k_ref, v_ref, lq1_ref, lk1_ref, lq2_ref, lk2_ref, g_ref, o_ref, zs_ref,
                 kaug_ref, vaug_ref, qaug_ref, m_ref, acc_ref, sa_ref, sb_ref, zero_ref, zsem,
                 *, lam_init, slopes):
    hd = pl.program_id(1)
    i = pl.program_id(2)

    step = (pl.program_id(0) * pl.num_programs(1) + hd) * pl.num_programs(2) + i
    n_steps = pl.num_programs(0) * pl.num_programs(1) * pl.num_programs(2)
    zrows = zero_ref.shape[0]

    def zero_copy(t):
        return pltpu.make_async_copy(zero_ref, zs_ref.at[pl.ds(t * zrows, zrows)], zsem)

    @pl.when(step == 0)
    def _():
        zero_ref[...] = jnp.zeros(zero_ref.shape, zero_ref.dtype)

    @pl.when(step > 0)
    def _():
        zero_copy(step - 1).wait()

    zero_copy(step).start()

    @pl.when(step == n_steps - 1)
    def _():
        zero_copy(step).wait()
    tq = q_ref.shape[1]
    hw = q_ref.shape[2]
    n_key = k_ref.shape[1]

    slope = jnp.float32(slopes[-1])
    for idx in range(len(slopes) - 2, -1, -1):
        slope = jnp.where(hd == idx, jnp.float32(slopes[idx]), slope)

    @pl.when(i == 0)
    def _():
        kaug_ref[:hw, :] = k_ref[0].astype(F32).T.astype(BF16)
        jrel = (lax.broadcasted_iota(I32, (hw, n_key), 1) & (tq - 1)).astype(F32) * slope
        krow = lax.broadcasted_iota(I32, (hw, n_key), 0)
        bias_rows = jnp.zeros((hw, n_key), F32)
        rest = jrel
        for term in range(BIAS_TERMS):
            part = rest.astype(BF16).astype(F32)
            bias_rows = jnp.where(krow == term, part, bias_rows)
            rest = rest - part
        kaug_ref[hw:, :] = bias_rows.astype(BF16)
        vaug_ref[:, :hw] = v_ref[0]
        vaug_ref[:, hw:] = jnp.ones((n_key, hw), BF16)

    q = q_ref[0]
    lane = lax.broadcasted_iota(I32, q.shape, 1)
    zero = jnp.zeros_like(q)
    ones = jnp.where(lane < BIAS_TERMS, 1.0, 0.0).astype(BF16)
    qaug_ref[:tq, :hw] = jnp.where(lane < DA_HEAD_DIM, q, zero)
    qaug_ref[tq:, :hw] = jnp.where(lane >= DA_HEAD_DIM, q, zero)
    qaug_ref[:tq, hw:] = ones
    qaug_ref[tq:, hw:] = ones

    m_ref[...] = jnp.full(m_ref.shape, NEG, F32)
    acc_ref[...] = jnp.zeros(acc_ref.shape, F32)

    def scores(j, buf):
        buf[...] = _nn(qaug_ref[...], kaug_ref[:, pl.ds(pl.multiple_of(j * tq, tq), tq)])

    def consume(j, buf, diagonal):
        s = buf[...]
        if diagonal:
            ii = lax.broadcasted_iota(I32, s.shape, 0) & (tq - 1)
            jj = lax.broadcasted_iota(I32, s.shape, 1)
            ahead = jnp.minimum(ii - jj, 0).astype(F32)
            s = s + jnp.where((jj // CHUNK) <= (ii // CHUNK), (2.0 * slope) * ahead, NEG)
            shift = jnp.float32(0.0)
        else:
            shift = -slope * ((i - j) * tq).astype(F32)
        m_old = m_ref[...]
        m_new = jnp.maximum(m_old, jnp.max(s, axis=-1, keepdims=True) + shift)
        p = jnp.exp2(s - jnp.concatenate([m_new - shift] * (tq // LANES), axis=1))
        alpha = jnp.concatenate([jnp.exp2(m_old - m_new)] * (acc_ref.shape[1] // LANES), axis=1)
        vb = vaug_ref[pl.ds(pl.multiple_of(j * tq, tq), tq), :]
        acc_ref[...] = alpha * acc_ref[...] + _nn(p.astype(BF16), vb)
        m_ref[...] = m_new

    scores(0, sa_ref)

    def pair(pi, carry):
        j = 2 * pi
        scores(j + 1, sb_ref)
        consume(j, sa_ref, False)
        scores(j + 2, sa_ref)
        consume(j + 1, sb_ref, False)
        return carry

    lax.fori_loop(0, i // 2, pair, 0)

    @pl.when(i % 2 == 0)
    def _():
        consume(i, sa_ref, True)

    @pl.when(i % 2 == 1)
    def _():
        scores(i, sb_ref)
        consume(i - 1, sa_ref, False)
        consume(i, sb_ref, True)

    acc = acc_ref[...]
    o1 = acc[:tq, :hw] / acc[:tq, hw:]
    o2 = acc[tq:, :hw] / acc[tq:, hw:]
    lam = (jnp.exp(jnp.sum(lq1_ref[...] * lk1_ref[...], axis=-1, keepdims=True))
           - jnp.exp(jnp.sum(lq2_ref[...] * lk2_ref[...], axis=-1, keepdims=True)) + lam_init)
    o = o1 - lam * o2
    o = o * lax.rsqrt(jnp.mean(o * o, axis=-1, keepdims=True) + EPS) * g_ref[...]
    o_ref[0] = (o * (1.0 - lam_init)).astype(BF16)


def _attention(q, k, v, lq1, lk1, lq2, lk2, subln_g, lam_init, zero_buf):
    b, s, _ = q.shape
    zero_shape = zero_buf.shape
    tq = ATT_BLOCK
    hw = 2 * DA_HEAD_DIM
    assert hw == LANES
    n_steps = b * DA_HEADS * (s // tq)
    assert zero_shape[0] % n_steps == 0
    zrows = zero_shape[0] // n_steps
    slopes = tuple(LOG2E * 2.0 ** (-8.0 * (h + 1) / DA_HEADS) for h in range(DA_HEADS))
    vec = lambda n: pl.BlockSpec((1, n), lambda bi, h, i: (0, 0))
    return pl.pallas_call(
        functools.partial(_attn_kernel, lam_init=lam_init, slopes=slopes),
        grid=(b, DA_HEADS, s // tq),
        in_specs=[
            pl.BlockSpec((1, tq, hw), lambda bi, h, i: (bi, i, h)),
            pl.BlockSpec((1, s, hw), lambda bi, h, i: (bi, 0, h)),
            pl.BlockSpec((1, s, hw), lambda bi, h, i: (bi, 0, h)),
            vec(DA_HEAD_DIM), vec(DA_HEAD_DIM), vec(DA_HEAD_DIM), vec(DA_HEAD_DIM), vec(hw),
        ],
        out_specs=[pl.BlockSpec((1, tq, hw), lambda bi, h, i: (bi, i, h)),
                   pl.BlockSpec(memory_space=pl.ANY)],
        out_shape=[jax.ShapeDtypeStruct((b, s, DA_HEADS * hw), BF16),
                   zero_buf],
        scratch_shapes=[
            pltpu.VMEM((2 * hw, s), BF16),
            pltpu.VMEM((s, 2 * hw), BF16),
            pltpu.VMEM((2 * tq, 2 * hw), BF16),
            pltpu.VMEM((2 * tq, LANES), F32),
            pltpu.VMEM((2 * tq, 2 * hw), F32),
            pltpu.VMEM((2 * tq, tq), F32),
            pltpu.VMEM((2 * tq, tq), F32),
            pltpu.VMEM((zrows,) + tuple(zero_shape[1:]), zero_buf.dtype),
            pltpu.SemaphoreType.DMA(()),
        ],
        compiler_params=pltpu.CompilerParams(vmem_limit_bytes=VMEM_LIMIT,
                                             dimension_semantics=("arbitrary",) * 3),
        name="attention",
    )(q, k, v, lq1, lk1, lq2, lk2, subln_g)


def _hgrn_kernel(qd_ref, kd_ref, ke_ref, v_ref, sg_ref, dec_ref, g_ref, o_ref, st_ref):
    @pl.when(pl.program_id(1) == 0)
    def _():
        st_ref[...] = jnp.zeros(st_ref.shape, F32)

    n_ch = qd_ref.shape[1] // CHUNK
    r = lax.broadcasted_iota(I32, (CHUNK, CHUNK), 0)
    c = lax.broadcasted_iota(I32, (CHUNK, CHUNK), 1)
    causal = r >= c

    heads = [slice(h * HG_DK, (h + 1) * HG_DK) for h in range(HG_HEADS)]
    chunks = [slice(ci * CHUNK, (ci + 1) * CHUNK) for ci in range(n_ch)]
    incr = [[_nn(v_ref[0, rows, cols].astype(F32).T.astype(BF16), ke_ref[0, rows, cols])
             for rows in chunks] for cols in heads]
    start = []
    for h, cols in enumerate(heads):
        st = st_ref[h]
        before = []
        for ci in range(n_ch):
            before.append(st.astype(BF16))
            st = st * dec_ref[0, ci:ci + 1, cols] + incr[h][ci]
        st_ref[h] = st
        start.append(before)
    for h, cols in enumerate(heads):
        for ci, rows in enumerate(chunks):
            qd = qd_ref[0, rows, cols]
            scores = jnp.where(causal, _nt(qd, kd_ref[0, rows, cols]), 0.0).astype(BF16)
            o = _nn(scores, v_ref[0, rows, cols]) + _nt(qd, start[h][ci])
            o = o * lax.rsqrt(jnp.mean(o * o, axis=-1, keepdims=True) + EPS) * g_ref[...]
            o_ref[0, rows, cols] = (o * sg_ref[0, rows, cols].astype(F32)).astype(BF16)


def _hgrn(qd, kd, ke, v, sg, dec, norm_g):
    b, s, w = qd.shape
    ts = ROW_TILE
    tok = pl.BlockSpec((1, ts, w), lambda bi, i: (bi, i, 0))
    return pl.pallas_call(
        _hgrn_kernel,
        grid=(b, s // ts),
        in_specs=[tok, tok, tok, tok, tok,
                  pl.BlockSpec((1, ts // CHUNK, w), lambda bi, i: (bi, i, 0)),
                  pl.BlockSpec((1, HG_DV), lambda bi, i: (0, 0))],
        out_specs=tok,
        out_shape=jax.ShapeDtypeStruct((b, s, w), BF16),
        scratch_shapes=[pltpu.VMEM((HG_HEADS, HG_DV, HG_DK), F32)],
        name="hgrn",
    )(qd, kd, ke, v, sg, dec, norm_g)


def _merge_kernel(yda_ref, yhg_ref, gda_ref, ghg_ref, x_ref, gt1_ref, sh2_ref, sc2_ref, g2_ref,
                  wda_ref, whg_ref, wout_ref, wrh_ref, wrl_ref, br_ref,
                  x1_ref, h2_ref, eidx_ref, wts_ref):
    u = (gda_ref[0].astype(F32) * _nn(yda_ref[0], wda_ref[...])
         + ghg_ref[0].astype(F32) * _nn(yhg_ref[0], whg_ref[...]))
    x1 = x_ref[0] + gt1_ref[0] * _nn(u.astype(BF16), wout_ref[...])
    x1_ref[0] = x1
    h2 = x1 * lax.rsqrt(jnp.mean(x1 * x1, axis=-1, keepdims=True) + EPS) * g2_ref[...]
    h2 = h2 * (1.0 + sc2_ref[0]) + sh2_ref[0]
    h2_ref[0] = _pack_rows(h2).reshape(h2.shape[0] // 2, SUBLANES, LANES).reshape(h2_ref.shape[1:])

    h_hi, h_lo = _split(h2)
    wrh = wrh_ref[...]
    both = _nt(jnp.concatenate([wrh, wrl_ref[...]], axis=0), h_hi)
    lg = both[:ROUTER_ROWS] + both[ROUTER_ROWS:] + _nt(wrh, h_lo) + br_ref[...]

    tm = h2.shape[0]
    eg = EXPERTS_PER_GROUP
    gl = lg[N_EXPERTS:N_EXPERTS + N_GROUPS]
    gmax = jnp.max(gl, axis=0, keepdims=True)
    g_p = 1.0 / jnp.sum(jnp.exp(gl - gmax), axis=0, keepdims=True)
    gi = lax.broadcasted_iota(I32, (N_GROUPS, tm), 0)
    g_idx = jnp.min(jnp.where(gl == gmax, gi, N_GROUPS), axis=0, keepdims=True)
    sel = lg[(N_GROUPS - 1) * eg:N_GROUPS * eg]
    for g in range(N_GROUPS - 2, -1, -1):
        sel = jnp.where(g_idx == g, lg[g * eg:(g + 1) * eg], sel)
    ei = lax.broadcasted_iota(I32, (eg, tm), 0)
    m1 = jnp.max(sel, axis=0, keepdims=True)
    i1 = jnp.min(jnp.where(sel == m1, ei, eg), axis=0, keepdims=True)
    sel2 = jnp.where(ei == i1, -jnp.inf, sel)
    m2 = jnp.max(sel2, axis=0, keepdims=True)
    i2 = jnp.min(jnp.where(sel2 == m2, ei, eg), axis=0, keepdims=True)
    t = jnp.exp(m2 - m1)
    wa = 1.0 / (1.0 + t)
    eidx_ref[0] = jnp.concatenate([g_idx * eg + i1, g_idx * eg + i2], axis=0)
    wts_ref[0] = jnp.concatenate([g_p * wa, g_p * (t * wa)], axis=0)


def _merge(yda, yhg, gda, ghg, x, mod, g2, wda, whg, wout, wrh, wrl, br):
    b, s, d = x.shape
    tm = ROW_TILE
    w = yda.shape[2]
    tok = lambda width: pl.BlockSpec((1, tm, width), lambda bi, i: (bi, i, 0))
    modspec = lambda k: pl.BlockSpec((1, 1, d), lambda bi, i: (bi, 0, k))
    full = lambda a: pl.BlockSpec(a.shape, lambda bi, i: (0,) * a.ndim)
    lane_rows = pl.BlockSpec((1, 2, tm), lambda bi, i: (bi, 0, i))
    return pl.pallas_call(
        _merge_kernel,
        grid=(b, s // tm),
        in_specs=[tok(w), tok(w), tok(d), tok(d), tok(d),
                  modspec(2), modspec(3), modspec(4), full(g2),
                  full(wda), full(whg), full(wout), full(wrh), full(wrl), full(br)],
        out_specs=[tok(d), pl.BlockSpec((1, tm // 2 * SUBLANES, LANES), lambda bi, i: (bi, i, 0)),
                   lane_rows, lane_rows],
        out_shape=[jax.ShapeDtypeStruct((b, s, d), F32),
                   jax.ShapeDtypeStruct((b, s // 2 * SUBLANES, LANES), U32),
                   jax.ShapeDtypeStruct((b, 2, s), I32), jax.ShapeDtypeStruct((b, 2, s), F32)],
        compiler_params=pltpu.CompilerParams(vmem_limit_bytes=VMEM_LIMIT),
        name="merge",
    )(yda, yhg, gda, ghg, x, mod, mod, mod, g2, wda, whg, wout, wrh, wrl, br)


def _positions_kernel(e_ref, half_ref, blk_ref, nused_ref):
    e_mat = e_ref[...]
    nr = e_mat.shape[0]
    nb = blk_ref.shape[1]
    r = lax.broadcasted_iota(I32, (LANES, LANES), 0)
    c = lax.broadcasted_iota(I32, (LANES, LANES), 1)
    incl = jnp.where(r <= c, 1.0, 0.0).astype(BF16)
    ones = jnp.ones((LANES, LANES), BF16)
    rr = lax.broadcasted_iota(I32, (nr, nr), 0)
    cc = lax.broadcasted_iota(I32, (nr, nr), 1)
    before = jnp.where(cc < rr, 1.0, 0.0).astype(BF16)
    blk_start = (lax.broadcasted_iota(I32, (1, nb), 1) * EXPERT_BLOCK).astype(F32)

    def per_expert(e, carry):
        pstart, dest, blk = carry
        hit = e_mat == e
        oh = jnp.where(hit, 1.0, 0.0).astype(BF16)
        in_row = _nn(oh, incl)
        row_tot = _nn(oh, ones)
        row_off = _nn(before, row_tot.astype(BF16))
        total = row_off[nr - 1:nr, :] + row_tot[nr - 1:nr, :]
        dest = jnp.where(hit, pstart + row_off + in_row - 1.0, dest)
        pend = pstart + jnp.floor((total + (EXPERT_BLOCK - 1)) * (1.0 / EXPERT_BLOCK)) * EXPERT_BLOCK
        blk = blk + jnp.where(blk_start >= pend[:, :1], 1.0, 0.0)
        return pend, dest, blk

    pstart0 = jnp.zeros((1, LANES), F32)
    pend, dest, blk = lax.fori_loop(
        0, N_EXPERTS, per_expert,
        (pstart0, jnp.zeros(e_mat.shape, F32), jnp.zeros((1, nb), F32)), unroll=4)
    half_rows = EXPERT_BLOCK // 2
    block = jnp.floor(dest * (1.0 / EXPERT_BLOCK))
    in_block = dest - block * EXPERT_BLOCK
    upper = jnp.floor(in_block * (1.0 / half_rows))
    tile = block * half_rows + in_block - upper * half_rows
    half_ref[...] = (SUBLANES * tile + (SUBLANES // 2) * upper).astype(I32)
    blk_ref[...] = jnp.minimum(blk, N_EXPERTS - 1).astype(I32)
    nused_ref[...] = (pend * (1.0 / EXPERT_BLOCK)).astype(I32)


def _positions(e_mat, n_blk_pad):
    nr = e_mat.shape[0]
    return pl.pallas_call(
        _positions_kernel,
        out_shape=[jax.ShapeDtypeStruct((nr, LANES), I32),
                   jax.ShapeDtypeStruct((1, n_blk_pad), I32),
                   jax.ShapeDtypeStruct((1, LANES), I32)],
        name="positions",
    )(e_mat)


def _dispatch_kernel(half_ref, h2_ref, xs_in_ref, xs_ref, hbuf, in_sem, out_sem, *, n_tok):
    del xs_in_ref
    i = pl.program_id(0)
    last = pl.num_programs(0) - 1
    slot = i % 2
    half = SUBLANES // 2
    pairs = GATHER_TILE // 2
    tile_rows = pairs * SUBLANES

    def load(step, buf):
        return pltpu.make_async_copy(h2_ref.at[pl.ds(step * tile_rows, tile_rows)], hbuf.at[buf],
                                     in_sem.at[buf])

    def drain(buf):
        for k in range(2):
            pltpu.make_async_copy(hbuf.at[buf], xs_ref.at[pl.ds(0, tile_rows)], out_sem.at[buf, k]).wait()

    @pl.when(i == 0)
    def _():
        load(0, 0).start()

    @pl.when(i > 0)
    def _():
        drain(1 - slot)

    @pl.when(i < last)
    def _():
        load(i + 1, 1 - slot).start()

    load(i, slot).wait()

    for up in range(2):
        base = i * GATHER_TILE + up * pairs

        def issue(r, carry):
            for k in range(2):
                row = pl.multiple_of(half_ref[k * n_tok + base + r], half)
                pltpu.make_async_copy(hbuf.at[slot, pl.ds(r * SUBLANES + up * half, half)],
                                      xs_ref.at[pl.ds(row, half)], out_sem.at[slot, k]).start(priority=k)
            return carry

        lax.fori_loop(0, pairs, issue, 0, unroll=ISSUE_UNROLL)

    @pl.when(i == last)
    def _():
        drain(slot)


def _dispatch(half, h2, xs_zero):
    n_tok = h2.shape[0] // (SUBLANES // 2)
    return pl.pallas_call(
        functools.partial(_dispatch_kernel, n_tok=n_tok),
        grid_spec=pltpu.PrefetchScalarGridSpec(
            num_scalar_prefetch=1,
            grid=(n_tok // GATHER_TILE,),
            in_specs=[pl.BlockSpec(memory_space=pl.ANY), pl.BlockSpec(memory_space=pl.ANY)],
            out_specs=pl.BlockSpec(memory_space=pl.ANY),
            scratch_shapes=[pltpu.VMEM((2, GATHER_TILE // 2 * SUBLANES, LANES), h2.dtype),
                            pltpu.SemaphoreType.DMA((2,)), pltpu.SemaphoreType.DMA((2, 2))],
        ),
        out_shape=jax.ShapeDtypeStruct(xs_zero.shape, xs_zero.dtype),
        input_output_aliases={2: 0},
        compiler_params=pltpu.CompilerParams(dimension_semantics=("arbitrary",)),
        name="dispatch",
    )(half, h2, xs_zero)


def _expert_kernel(blk_ref, nused_ref, xs_ref, w1_ref, w3_ref, w2_ref, ys_ref,
                   w1b, w3b, w2b, ybuf, cur, sem):
    i = pl.program_id(0)
    n_used = nused_ref[0]
    e = blk_ref[jnp.minimum(i, pl.num_programs(0) - 2)]
    fresh = (i == 0) | (blk_ref[jnp.maximum(i - 1, 0)] != e)

    def fetch(expert, which):
        return (pltpu.make_async_copy(w1_ref.at[expert], w1b.at[which], sem.at[0]),
                pltpu.make_async_copy(w3_ref.at[expert], w3b.at[which], sem.at[1]),
                pltpu.make_async_copy(w2_ref.at[expert], w2b.at[which], sem.at[2]))

    @pl.when(i < n_used)
    def _():
        @pl.when(i == 0)
        def _():
            cur[0] = 0
            for cp in fetch(e, 0):
                cp.start()

        @pl.when(fresh & (i > 0))
        def _():
            cur[0] = 1 - cur[0]

        @pl.when(fresh)
        def _():
            for cp in fetch(e, cur[0]):
                cp.wait()
            nxt = lax.while_loop(lambda j: (j < n_used) & (blk_ref[jnp.minimum(j, n_used - 1)] == e),
                                 lambda j: j + 1, i + 1)

            @pl.when(nxt < n_used)
            def _():
                for cp in fetch(blk_ref[jnp.minimum(nxt, n_used - 1)], 1 - cur[0]):
                    cp.start(priority=1)

    tiles = (xs_ref.shape[0] // SUBLANES, SUBLANES, LANES)

    def compute():
        which = cur[0]
        x = _unpack_rows(xs_ref[...].reshape(tiles).reshape(tiles[0], w1b.shape[1])).astype(BF16)
        a = _nn(x, w1b[which])
        ybuf[...] = _nn((_silu(a) * _nn(x, w3b[which])).astype(BF16), w2b[which])

    def flush():
        ys_ref[...] = _pack_rows(ybuf[...]).reshape(tiles).reshape(ys_ref.shape)

    @pl.when(i == 0)
    def _():
        compute()

    @pl.when((i > 0) & (i < n_used))
    def _():
        flush()
        compute()

    @pl.when((i > 0) & (i == n_used))
    def _():
        flush()


def _experts(blk_e, n_used, xs, w1, w3, w2):
    block_rows = EXPERT_BLOCK // 2 * SUBLANES
    d, de = w1.shape[1], w1.shape[2]
    rows = lambda i, blk, nu: (jnp.minimum(i, nu[0] - 1), 0)
    hbm = pl.BlockSpec(memory_space=pl.ANY)
    return pl.pallas_call(
        _expert_kernel,
        grid_spec=pltpu.PrefetchScalarGridSpec(
            num_scalar_prefetch=2,
            grid=(xs.shape[0] // block_rows + 1,),
            in_specs=[pl.BlockSpec((block_rows, LANES), rows), hbm, hbm, hbm],
            out_specs=pl.BlockSpec((block_rows, LANES),
                                   lambda i, blk, nu: (jnp.minimum(jnp.maximum(i - 1, 0), nu[0] - 1), 0)),
            scratch_shapes=[pltpu.VMEM((2, d, de), BF16), pltpu.VMEM((2, d, de), BF16),
                            pltpu.VMEM((2, de, d), BF16), pltpu.VMEM((EXPERT_BLOCK, d), F32),
                            pltpu.SMEM((1,), I32), pltpu.SemaphoreType.DMA((3,))],
        ),
        out_shape=jax.ShapeDtypeStruct(xs.shape, xs.dtype),
        input_output_aliases={2: 0},
        compiler_params=pltpu.CompilerParams(vmem_limit_bytes=VMEM_LIMIT,
                                             dimension_semantics=("arbitrary",)),
        name="experts",
    )(blk_e, n_used, xs, w1, w3, w2)


def _combine_kernel(half_ref, ys_ref, x1_ref, gt2_ref, wts_ref, g_ref, o_ref, ybuf, sem, *, n_tok, last):
    i = pl.program_id(0)
    last_step = pl.num_programs(0) - 1
    slot = i % 2
    d = x1_ref.shape[1]
    half = SUBLANES // 2
    pairs = GATHER_TILE // 2
    n_groups = pairs // ISSUE_UNROLL

    def gather(step, buf, j0):
        for up in range(2):
            base = step * GATHER_TILE + up * pairs + j0
            for u in range(ISSUE_UNROLL):
                for k in range(2):
                    row = pl.multiple_of(half_ref[k * n_tok + base + u], half)
                    pltpu.make_async_copy(ys_ref.at[pl.ds(row, half)],
                                          ybuf.at[buf, k, pl.ds((j0 + u) * SUBLANES + up * half, half)],
                                          sem.at[buf, k]).start(priority=k)

    def drain(buf):
        for k in range(2):
            pltpu.make_async_copy(ys_ref.at[pl.ds(0, pairs * SUBLANES)], ybuf.at[buf, k], sem.at[buf, k]).wait()

    @pl.when(i == 0)
    def _():
        def first(g, carry):
            gather(0, 0, g * ISSUE_UNROLL)
            return carry

        lax.fori_loop(0, n_groups, first, 0)

    drain(slot)

    nxt = jnp.minimum(i + 1, last_step)
    gt2 = gt2_ref[0]
    gain = g_ref[...]

    def group(g, carry):
        j0 = pl.multiple_of(g * ISSUE_UNROLL, ISSUE_UNROLL)
        lower, upper = pl.ds(j0, ISSUE_UNROLL), pl.ds(pairs + j0, ISSUE_UNROLL)
        tile_rows = pl.ds(pl.multiple_of(j0 * SUBLANES, ISSUE_UNROLL * SUBLANES), ISSUE_UNROLL * SUBLANES)

        def tokens(k):
            tiles = ybuf[slot, k, tile_rows, :].reshape(ISSUE_UNROLL, SUBLANES, LANES)
            return _unpack_rows(tiles.reshape(ISSUE_UNROLL, d))

        w = jnp.concatenate([wts_ref[lower, :], wts_ref[upper, :]], axis=0)
        x = jnp.concatenate([x1_ref[lower, :], x1_ref[upper, :]], axis=0)
        x = x + gt2 * (w[:, 0:1] * tokens(0) + w[:, 1:2] * tokens(1))
        if last:
            x = x * lax.rsqrt(jnp.mean(x * x, axis=-1, keepdims=True) + EPS) * gain
        o_ref[lower, :] = x[:ISSUE_UNROLL]
        o_ref[upper, :] = x[ISSUE_UNROLL:]
        gather(nxt, 1 - slot, j0)
        return carry

    lax.fori_loop(0, n_groups, group, 0, unroll=COMBINE_UNROLL)

    @pl.when(i == last_step)
    def _():
        drain(1 - slot)


def _combine(half, ys, x1, mod, wts, final_g, seq, last=True):
    n_tok, d = x1.shape
    tt = GATHER_TILE
    per_batch = seq // tt
    return pl.pallas_call(
        functools.partial(_combine_kernel, n_tok=n_tok, last=last),
        grid_spec=pltpu.PrefetchScalarGridSpec(
            num_scalar_prefetch=1,
            grid=(n_tok // tt,),
            in_specs=[pl.BlockSpec(memory_space=pl.ANY),
                      pl.BlockSpec((tt, d), lambda i, half: (i, 0)),
                      pl.BlockSpec((1, 1, d), lambda i, half: (i // per_batch, 0, 5)),
                      pl.BlockSpec((tt, 2), lambda i, half: (i, 0)),
                      pl.BlockSpec((1, d), lambda i, half: (0, 0))],
            out_specs=pl.BlockSpec((tt, d), lambda i, half: (i, 0)),
            scratch_shapes=[pltpu.VMEM((2, 2, tt // 2 * SUBLANES, LANES), U32),
                            pltpu.SemaphoreType.DMA((2, 2))],
        ),
        out_shape=jax.ShapeDtypeStruct((n_tok, d), F32),
        compiler_params=pltpu.CompilerParams(dimension_semantics=("arbitrary",)),
        name="combine",
    )(half, ys, x1, mod, wts, final_g)


def kernel(x, c, w_ada, b_ada, norm1_g, w_in, lambda_q1, lambda_k1, lambda_q2, lambda_k2, da_subln_g, hg_lb_logits, hg_norm_g, w_up_da, w_up_hg, w_out, norm2_g, w_rg, b_rg, w_re, b_re, w1, w3, w2, final_g):
    b, s, d = x.shape
    n_tok = b * s
    depth = w_ada.shape[0]
    assert s % ROW_TILE == 0 and s % ATT_BLOCK == 0 and GATHER_TILE == ROW_TILE
    assert (2 * n_tok) % LANES == 0 and d == SUBLANES * LANES

    c_pad = jnp.pad(c, ((0, (-b) % 8), (0, 0)))
    n_slots = 2 * n_tok + N_EXPERTS * EXPERT_BLOCK
    n_blk = n_slots // EXPERT_BLOCK
    n_blk_pad = -(-n_blk // LANES) * LANES

    for l in range(depth):
        lam_init = 0.8 - 0.6 * math.exp(-0.3 * l)
        mod = _adaln(c_pad, w_ada[l], b_ada[l][None, :])[:b].reshape(b, 1, 6 * d)

        n_steps = n_tok // ROW_TILE
        sliced = lambda a: a.reshape((n_steps, a.shape[0] // n_steps) + a.shape[1:])
        (qda, kda, vda, qd, kd, ke, hi, sg, dec, gda, ghg,
         w1_bf, w3_bf, w2_bf, wda_bf, whg_bf, wout_bf) = _inproj(
            x, mod, norm1_g[l][None, :], w_in[l].astype(BF16), hg_lb_logits, l,
            (w1[l], w3[l], w2[l], sliced(w_up_da[l]), sliced(w_up_hg[l]), sliced(w_out[l])))
        yda, xs_zero = _attention(qda, kda, vda, lambda_q1[l][None, :], lambda_k1[l][None, :],
                                  lambda_q2[l][None, :], lambda_k2[l][None, :], da_subln_g[l][None, :],
                                  lam_init, jax.ShapeDtypeStruct((n_slots // 2 * SUBLANES, LANES), U32))
        yhg = _hgrn(qd, kd, ke, hi, sg, dec, hg_norm_g[l][None, :])

        wr = jnp.concatenate([w_re[l].T, w_rg[l].T,
                              jnp.zeros((ROUTER_ROWS - N_EXPERTS - N_GROUPS, d), F32)], axis=0)
        wr_hi = wr.astype(BF16)
        wr_lo = (wr - wr_hi.astype(F32)).astype(BF16)
        br = jnp.concatenate([b_re[l], b_rg[l],
                              jnp.zeros((ROUTER_ROWS - N_EXPERTS - N_GROUPS,), F32)])[:, None]
        x1, h2, eidx, wts = _merge(yda, yhg, gda, ghg, x, mod, norm2_g[l][None, :],
                                   wda_bf.reshape(w_up_da[l].shape), whg_bf.reshape(w_up_hg[l].shape),
                                   wout_bf.reshape(w_out[l].shape), wr_hi, wr_lo, br)

        e_mat = jnp.transpose(eidx, (1, 0, 2)).reshape(2 * n_tok // LANES, LANES)
        half, blk_e, n_used = _positions(e_mat, n_blk_pad)
        half = half.reshape(2 * n_tok)
        xs = _dispatch(half, h2.reshape(b * h2.shape[1], LANES), xs_zero)
        ys = _experts(blk_e[0, :n_blk], n_used[0, :1], xs, w1_bf, w3_bf, w2_bf)
        wts_tok = jnp.transpose(wts, (0, 2, 1)).reshape(n_tok, 2)
        x = _combine(half, ys, x1.reshape(n_tok, d), mod, wts_tok, final_g[None, :], s,
                     last=(l == depth - 1)).reshape(b, s, d)
    return x
```

```python
import functools
import math

import jax
import jax.numpy as jnp
from jax import lax
from jax.experimental import pallas as pl
from jax.experimental.pallas import tpu as pltpu

F32 = jnp.float32
BF16 = jnp.bfloat16
I32 = jnp.int32
U32 = jnp.uint32

EPS = 1e-6
NEG = -1e30
LOG2E = math.log2(math.e)

CHUNK = 64
DA_HEADS = 4
DA_HEAD_DIM = 64
HG_HEADS = 4
HG_DK = 128
HG_DV = 128
N_GROUPS = 4
EXPERTS_PER_GROUP = 8
N_EXPERTS = N_GROUPS * EXPERTS_PER_GROUP

LANES = 128
SUBLANES = 8
ADALN_COLS = 1024
ROW_TILE = 512
ATT_BLOCK = 512
BIAS_TERMS = 3
EXPERT_BLOCK = 256
GATHER_TILE = 512
ISSUE_UNROLL = 8
COMBINE_UNROLL = 4
ROUTER_ROWS = 48
VMEM_LIMIT = 52 * 1024 * 1024


def _nt(a, b):
    return lax.dot_general(a, b, (((1,), (1,)), ((), ())), preferred_element_type=F32)


def _nn(a, b):
    return jnp.dot(a, b, preferred_element_type=F32)


def _split(a):
    hi = a.astype(BF16)
    lo = (a - hi.astype(F32)).astype(BF16)
    return hi, lo


def _sigmoid(x):
    return 1.0 / (1.0 + jnp.exp(-x))


def _silu(x):
    return x * _sigmoid(x)


def _pack_rows(y):
    rows, cols = y.shape[0] // 2, y.shape[1] // 2
    bits = lax.bitcast_convert_type(y.astype(BF16).astype(F32), U32)
    words = bits[:, :cols] | lax.shift_right_logical(bits[:, cols:], jnp.uint32(16))
    return jnp.concatenate([words[:rows], words[rows:]], axis=1)


def _unpack_rows(packed):
    cols = packed.shape[1] // 2
    words = jnp.concatenate([packed[:, :cols], packed[:, cols:]], axis=0)
    first = lax.bitcast_convert_type(words & jnp.uint32(0xFFFF0000), F32)
    second = lax.bitcast_convert_type(lax.shift_left(words, jnp.uint32(16)), F32)
    return jnp.concatenate([first, second], axis=1)


def _adaln_kernel(c_ref, w_ref, b_ref, o_ref):
    c_hi, c_lo = _split(_silu(c_ref[...]))
    w_hi, w_lo = _split(w_ref[...])
    o_ref[...] = _nn(c_hi, w_hi) + _nn(c_lo, w_hi) + _nn(c_hi, w_lo) + b_ref[...]


def _adaln(c_pad, w, b):
    rows, d = c_pad.shape
    n = w.shape[1]
    tn = ADALN_COLS
    return pl.pallas_call(
        _adaln_kernel,
        grid=(n // tn,),
        in_specs=[
            pl.BlockSpec((rows, d), lambda j: (0, 0)),
            pl.BlockSpec((d, tn), lambda j: (0, j)),
            pl.BlockSpec((1, tn), lambda j: (0, j)),
        ],
        out_specs=pl.BlockSpec((rows, tn), lambda j: (0, j)),
        out_shape=jax.ShapeDtypeStruct((rows, n), F32),
        name="adaln",
    )(c_pad, w, b)


def _inproj_kernel(x_ref, sh_ref, sc_ref, g_ref, w_ref, lb_ref, *refs, layer, n_cast):
    (qda_ref, kda_ref, vda_ref, qd_ref, kd_ref, ke_ref, hi_ref, sg_ref,
     dec_ref, gda_ref, ghg_ref) = refs[n_cast:-n_cast]
    for src_ref, dst_ref in zip(refs[:n_cast], refs[-n_cast:]):
        dst_ref[...] = src_ref[...].astype(BF16)

    x = x_ref[0]
    tm = x.shape[0]
    h = x * lax.rsqrt(jnp.mean(x * x, axis=-1, keepdims=True) + EPS) * g_ref[...]
    h = h * (1.0 + sc_ref[0]) + sh_ref[0]
    hb = h.astype(BF16)

    def proj(c0, width):
        return _nn(hb, w_ref[:, c0:c0 + width])

    w_da = DA_HEADS * 2 * DA_HEAD_DIM
    w_hg = HG_HEADS * HG_DK
    qda_ref[0] = (proj(0, w_da) * (LOG2E / math.sqrt(DA_HEAD_DIM))).astype(BF16)
    kda_ref[0] = proj(w_da, w_da).astype(BF16)
    vda_ref[0] = proj(2 * w_da, w_da).astype(BF16)
    c0 = 3 * w_da

    lbl = lb_ref[...]
    lbe = jnp.exp(lbl - jnp.max(lbl, axis=0, keepdims=True))
    lbs = lbe / jnp.sum(lbe, axis=0, keepdims=True)
    lb = jnp.sum(lbs[:layer + 1], axis=0, keepdims=True)

    f = lb + (1.0 - lb) * _sigmoid(proj(c0 + w_hg, w_hg))
    logf = jnp.log(f)
    kk = 1.0 - f
    pos = lax.broadcasted_iota(I32, logf.shape, 0) & (CHUNK - 1)
    a = logf
    step = 1
    while step < CHUNK:
        a = a + jnp.where(pos >= step, pltpu.roll(a, step, axis=0), 0.0)
        step *= 2
    n_ch = tm // CHUNK
    a = a.reshape(n_ch, CHUNK, w_hg)
    a_last = a[:, CHUNK - 1:CHUNK, :]
    kk3 = kk.reshape(n_ch, CHUNK, w_hg)
    qq3 = _silu(proj(c0, w_hg)).reshape(n_ch, CHUNK, w_hg)
    qd_ref[0] = (qq3 * jnp.exp(a)).reshape(tm, w_hg).astype(BF16)
    kd_ref[0] = (kk3 * jnp.exp(-a)).reshape(tm, w_hg).astype(BF16)
    ke_ref[0] = (kk3 * jnp.exp(a_last - a)).reshape(tm, w_hg).astype(BF16)
    dec_ref[0] = jnp.exp(a_last).reshape(n_ch, w_hg)
    hi_ref[0] = proj(c0 + 2 * w_hg, w_hg).astype(BF16)
    sg_ref[0] = _silu(proj(c0 + 3 * w_hg, w_hg)).astype(BF16)
    c1 = c0 + 4 * w_hg
    d = x.shape[1]
    gda_ref[0] = _sigmoid(proj(c1, d)).astype(BF16)
    ghg_ref[0] = _sigmoid(proj(c1 + d, d)).astype(BF16)


def _inproj(x, mod, g, w_bf, lb_logits, layer, to_cast):
    b, s, d = x.shape
    tm = ROW_TILE
    n_cols = w_bf.shape[1]
    per_b = s // tm
    assert all(a.shape[0] % (b * per_b) == 0 for a in to_cast)
    wspec = lambda a: pl.BlockSpec((a.shape[0] // (b * per_b),) + a.shape[1:],
                                   lambda bi, i: (bi * per_b + i, 0, 0))
    w_da = DA_HEADS * 2 * DA_HEAD_DIM
    w_hg = HG_HEADS * HG_DK

    def tok(width, dtype):
        return (pl.BlockSpec((1, tm, width), lambda bi, i: (bi, i, 0)),
                jax.ShapeDtypeStruct((b, s, width), dtype))

    outs = [tok(w_da, BF16)] * 3 + [tok(w_hg, BF16)] * 5
    outs.append((pl.BlockSpec((1, tm // CHUNK, w_hg), lambda bi, i: (bi, i, 0)),
                 jax.ShapeDtypeStruct((b, s // CHUNK, w_hg), F32)))
    outs += [tok(d, BF16)] * 2
    outs += [(wspec(a), jax.ShapeDtypeStruct(a.shape, BF16)) for a in to_cast]
    return pl.pallas_call(
        functools.partial(_inproj_kernel, layer=layer, n_cast=len(to_cast)),
        grid=(b, s // tm),
        in_specs=[
            pl.BlockSpec((1, tm, d), lambda bi, i: (bi, i, 0)),
            pl.BlockSpec((1, 1, d), lambda bi, i: (bi, 0, 0)),
            pl.BlockSpec((1, 1, d), lambda bi, i: (bi, 0, 1)),
            pl.BlockSpec((1, d), lambda bi, i: (0, 0)),
            pl.BlockSpec((d, n_cols), lambda bi, i: (0, 0), pipeline_mode=pl.Buffered(1)),
            pl.BlockSpec(lb_logits.shape, lambda bi, i: (0, 0)),
        ] + [wspec(a) for a in to_cast],
        out_specs=[o[0] for o in outs],
        out_shape=[o[1] for o in outs],
        compiler_params=pltpu.CompilerParams(vmem_limit_bytes=VMEM_LIMIT),
        name="inproj",
    )(x, mod, mod, g, w_bf, lb_logits, *to_cast)


def _attn_kernel(q_ref, k_ref, v_ref, lq1_ref, lk1_ref, lq2_ref, lk2_ref, g_ref, o_ref, zs_ref,
                 kaug_ref, vaug_ref, qaug_ref, m_ref, acc_ref, sa_ref, sb_ref, zero_ref, zsem,
                 *, lam_init, slopes):
    hd = pl.program_id(1)
    i = pl.program_id(2)

    step = (pl.program_id(0) * pl.num_programs(1) + hd) * pl.num_programs(2) + i
    n_steps = pl.num_programs(0) * pl.num_programs(1) * pl.num_programs(2)
    zrows = zero_ref.shape[0]

    def zero_copy(t):
        return pltpu.make_async_copy(zero_ref, zs_ref.at[pl.ds(t * zrows, zrows)], zsem)

    @pl.when(step == 0)
    def _():
        zero_ref[...] = jnp.zeros(zero_ref.shape, zero_ref.dtype)

    @pl.when(step > 0)
    def _():
        zero_copy(step - 1).wait()

    zero_copy(step).start()

    @pl.when(step == n_steps - 1)
    def _():
        zero_copy(step).wait()
    tq = q_ref.shape[1]
    hw = q_ref.shape[2]
    n_key = k_ref.shape[1]

    slope = jnp.float32(slopes[-1])
    for idx in range(len(slopes) - 2, -1, -1):
        slope = jnp.where(hd == idx, jnp.float32(slopes[idx]), slope)

    @pl.when(i == 0)
    def _():
        kaug_ref[:hw, :] = k_ref[0].astype(F32).T.astype(BF16)
        jrel = (lax.broadcasted_iota(I32, (hw, n_key), 1) & (tq - 1)).astype(F32) * slope
        krow = lax.broadcasted_iota(I32, (hw, n_key), 0)
        bias_rows = jnp.zeros((hw, n_key), F32)
        rest = jrel
        for term in range(BIAS_TERMS):
            part = rest.astype(BF16).astype(F32)
            bias_rows = jnp.where(krow == term, part, bias_rows)
            rest = rest - part
        kaug_ref[hw:, :] = bias_rows.astype(BF16)
        vaug_ref[:, :hw] = v_ref[0]
        vaug_ref[:, hw:] = jnp.ones((n_key, hw), BF16)

    q = q_ref[0]
    lane = lax.broadcasted_iota(I32, q.shape, 1)
    zero = jnp.zeros_like(q)
    ones = jnp.where(lane < BIAS_TERMS, 1.0, 0.0).astype(BF16)
    qaug_ref[:tq, :hw] = jnp.where(lane < DA_HEAD_DIM, q, zero)
    qaug_ref[tq:, :hw] = jnp.where(lane >= DA_HEAD_DIM, q, zero)
    qaug_ref[:tq, hw:] = ones
    qaug_ref[tq:, hw:] = ones

    m_ref[...] = jnp.full(m_ref.shape, NEG, F32)
    acc_ref[...] = jnp.zeros(acc_ref.shape, F32)

    def scores(j, buf):
        buf[...] = _nn(qaug_ref[...], kaug_ref[:, pl.ds(pl.multiple_of(j * tq, tq), tq)])

    def consume(j, buf, diagonal):
        s = buf[...]
        if diagonal:
            ii = lax.broadcasted_iota(I32, s.shape, 0) & (tq - 1)
            jj = lax.broadcasted_iota(I32, s.shape, 1)
            ahead = jnp.minimum(ii - jj, 0).astype(F32)
            s = s + jnp.where((jj // CHUNK) <= (ii // CHUNK), (2.0 * slope) * ahead, NEG)
            shift = jnp.float32(0.0)
        else:
            shift = -slope * ((i - j) * tq).astype(F32)
        m_old = m_ref[...]
        m_new = jnp.maximum(m_old, jnp.max(s, axis=-1, keepdims=True) + shift)
        p = jnp.exp2(s - jnp.concatenate([m_new - shift] * (tq // LANES), axis=1))
        alpha = jnp.concatenate([jnp.exp2(m_old - m_new)] * (acc_ref.shape[1] // LANES), axis=1)
        vb = vaug_ref[pl.ds(pl.multiple_of(j * tq, tq), tq), :]
        acc_ref[...] = alpha * acc_ref[...] + _nn(p.astype(BF16), vb)
        m_ref[...] = m_new

    scores(0, sa_ref)

    def pair(pi, carry):
        j = 2 * pi
        scores(j + 1, sb_ref)
        consume(j, sa_ref, False)
        scores(j + 2, sa_ref)
        consume(j + 1, sb_ref, False)
        return carry

    lax.fori_loop(0, i // 2, pair, 0)

    @pl.when(i % 2 == 0)
    def _():
        consume(i, sa_ref, True)

    @pl.when(i % 2 == 1)
    def _():
        scores(i, sb_ref)
        consume(i - 1, sa_ref, False)
        consume(i, sb_ref, True)

    acc = acc_ref[...]
    o1 = acc[:tq, :hw] / acc[:tq, hw:]
    o2 = acc[tq:, :hw] / acc[tq:, hw:]
    lam = (jnp.exp(jnp.sum(lq1_ref[...] * lk1_ref[...], axis=-1, keepdims=True))
           - jnp.exp(jnp.sum(lq2_ref[...] * lk2_ref[...], axis=-1, keepdims=True)) + lam_init)
    o = o1 - lam * o2
    o = o * lax.rsqrt(jnp.mean(o * o, axis=-1, keepdims=True) + EPS) * g_ref[...]
    o_ref[0] = (o * (1.0 - lam_init)).astype(BF16)


def _attention(q, k, v, lq1, lk1, lq2, lk2, subln_g, lam_init, zero_buf):
    b, s, _ = q.shape
    zero_shape = zero_buf.shape
    tq = ATT_BLOCK
    hw = 2 * DA_HEAD_DIM
    assert hw == LANES
    n_steps = b * DA_HEADS * (s // tq)
    assert zero_shape[0] % n_steps == 0
    zrows = zero_shape[0] // n_steps
    slopes = tuple(LOG2E * 2.0 ** (-8.0 * (h + 1) / DA_HEADS) for h in range(DA_HEADS))
    vec = lambda n: pl.BlockSpec((1, n), lambda bi, h, i: (0, 0))
    return pl.pallas_call(
        functools.partial(_attn_kernel, lam_init=lam_init, slopes=slopes),
        grid=(b, DA_HEADS, s // tq),
        in_specs=[
            pl.BlockSpec((1, tq, hw), lambda bi, h, i: (bi, i, h)),
            pl.BlockSpec((1, s, hw), lambda bi, h, i: (bi, 0, h)),
            pl.BlockSpec((1, s, hw), lambda bi, h, i: (bi, 0, h)),
            vec(DA_HEAD_DIM), vec(DA_HEAD_DIM), vec(DA_HEAD_DIM), vec(DA_HEAD_DIM), vec(hw),
        ],
        out_specs=[pl.BlockSpec((1, tq, hw), lambda bi, h, i: (bi, i, h)),
                   pl.BlockSpec(memory_space=pl.ANY)],
        out_shape=[jax.ShapeDtypeStruct((b, s, DA_HEADS * hw), BF16),
                   zero_buf],
        scratch_shapes=[
            pltpu.VMEM((2 * hw, s), BF16),
            pltpu.VMEM((s, 2 * hw), BF16),
            pltpu.VMEM((2 * tq, 2 * hw), BF16),
            pltpu.VMEM((2 * tq, LANES), F32),
            pltpu.VMEM((2 * tq, 2 * hw), F32),
            pltpu.VMEM((2 * tq, tq), F32),
            pltpu.VMEM((2 * tq, tq), F32),
            pltpu.VMEM((zrows,) + tuple(zero_shape[1:]), zero_buf.dtype),
            pltpu.SemaphoreType.DMA(()),
        ],
        compiler_params=pltpu.CompilerParams(vmem_limit_bytes=VMEM_LIMIT,
                                             dimension_semantics=("arbitrary",) * 3),
        name="attention",
    )(q, k, v, lq1, lk1, lq2, lk2, subln_g)


def _hgrn_kernel(qd_ref, kd_ref, ke_ref, v_ref, sg_ref, dec_ref, g_ref, o_ref, st_ref):
    @pl.when(pl.program_id(1) == 0)
    def _():
        st_ref[...] = jnp.zeros(st_ref.shape, F32)

    n_ch = qd_ref.shape[1] // CHUNK
    r = lax.broadcasted_iota(I32, (CHUNK, CHUNK), 0)
    c = lax.broadcasted_iota(I32, (CHUNK, CHUNK), 1)
    causal = r >= c

    heads = [slice(h * HG_DK, (h + 1) * HG_DK) for h in range(HG_HEADS)]
    chunks = [slice(ci * CHUNK, (ci + 1) * CHUNK) for ci in range(n_ch)]
    incr = [[_nn(v_ref[0, rows, cols].astype(F32).T.astype(BF16), ke_ref[0, rows, cols])
             for rows in chunks] for cols in heads]
    start = []
    for h, cols in enumerate(heads):
        st = st_ref[h]
        before = []
        for ci in range(n_ch):
            before.append(st.astype(BF16))
            st = st * dec_ref[0, ci:ci + 1, cols] + incr[h][ci]
        st_ref[h] = st
        start.append(before)
    for h, cols in enumerate(heads):
        for ci, rows in enumerate(chunks):
            qd = qd_ref[0, rows, cols]
            scores = jnp.where(causal, _nt(qd, kd_ref[0, rows, cols]), 0.0).astype(BF16)
            o = _nn(scores, v_ref[0, rows, cols]) + _nt(qd, start[h][ci])
            o = o * lax.rsqrt(jnp.mean(o * o, axis=-1, keepdims=True) + EPS) * g_ref[...]
            o_ref[0, rows, cols] = (o * sg_ref[0, rows, cols].astype(F32)).astype(BF16)


def _hgrn(qd, kd, ke, v, sg, dec, norm_g):
    b, s, w = qd.shape
    ts = ROW_TILE
    tok = pl.BlockSpec((1, ts, w), lambda bi, i: (bi, i, 0))
    return pl.pallas_call(
        _hgrn_kernel,
        grid=(b, s // ts),
        in_specs=[tok, tok, tok, tok, tok,
                  pl.BlockSpec((1, ts // CHUNK, w), lambda bi, i: (bi, i, 0)),
                  pl.BlockSpec((1, HG_DV), lambda bi, i: (0, 0))],
        out_specs=tok,
        out_shape=jax.ShapeDtypeStruct((b, s, w), BF16),
        scratch_shapes=[pltpu.VMEM((HG_HEADS, HG_DV, HG_DK), F32)],
        name="hgrn",
    )(qd, kd, ke, v, sg, dec, norm_g)


def _merge_kernel(yda_ref, yhg_ref, gda_ref, ghg_ref, x_ref, gt1_ref, sh2_ref, sc2_ref, g2_ref,
                  wda_ref, whg_ref, wout_ref, wrh_ref, wrl_ref, br_ref,
                  x1_ref, h2_ref, eidx_ref, wts_ref):
    u = (gda_ref[0].astype(F32) * _nn(yda_ref[0], wda_ref[...])
         + ghg_ref[0].astype(F32) * _nn(yhg_ref[0], whg_ref[...]))
    x1 = x_ref[0] + gt1_ref[0] * _nn(u.astype(BF16), wout_ref[...])
    x1_ref[0] = x1
    h2 = x1 * lax.rsqrt(jnp.mean(x1 * x1, axis=-1, keepdims=True) + EPS) * g2_ref[...]
    h2 = h2 * (1.0 + sc2_ref[0]) + sh2_ref[0]
    h2_ref[0] = _pack_rows(h2).reshape(h2.shape[0] // 2, SUBLANES, LANES).reshape(h2_ref.shape[1:])

    h_hi, h_lo = _split(h2)
    wrh = wrh_ref[...]
    both = _nt(jnp.concatenate([wrh, wrl_ref[...]], axis=0), h_hi)
    lg = both[:ROUTER_ROWS] + both[ROUTER_ROWS:] + _nt(wrh, h_lo) + br_ref[...]

    tm = h2.shape[0]
    eg = EXPERTS_PER_GROUP
    gl = lg[N_EXPERTS:N_EXPERTS + N_GROUPS]
    gmax = jnp.max(gl, axis=0, keepdims=True)
    g_p = 1.0 / jnp.sum(jnp.exp(gl - gmax), axis=0, keepdims=True)
    gi = lax.broadcasted_iota(I32, (N_GROUPS, tm), 0)
    g_idx = jnp.min(jnp.where(gl == gmax, gi, N_GROUPS), axis=0, keepdims=True)
    sel = lg[(N_GROUPS - 1) * eg:N_GROUPS * eg]
    for g in range(N_GROUPS - 2, -1, -1):
        sel = jnp.where(g_idx == g, lg[g * eg:(g + 1) * eg], sel)
    ei = lax.broadcasted_iota(I32, (eg, tm), 0)
    m1 = jnp.max(sel, axis=0, keepdims=True)
    i1 = jnp.min(jnp.where(sel == m1, ei, eg), axis=0, keepdims=True)
    sel2 = jnp.where(ei == i1, -jnp.inf, sel)
    m2 = jnp.max(sel2, axis=0, keepdims=True)
    i2 = jnp.min(jnp.where(sel2 == m2, ei, eg), axis=0, keepdims=True)
    t = jnp.exp(m2 - m1)
    wa = 1.0 / (1.0 + t)
    eidx_ref[0] = jnp.concatenate([g_idx * eg + i1, g_idx * eg + i2], axis=0)
    wts_ref[0] = jnp.concatenate([g_p * wa, g_p * (t * wa)], axis=0)


def _merge(yda, yhg, gda, ghg, x, mod, g2, wda, whg, wout, wrh, wrl, br):
    b, s, d = x.shape
    tm = ROW_TILE
    w = yda.shape[2]
    tok = lambda width: pl.BlockSpec((1, tm, width), lambda bi, i: (bi, i, 0))
    modspec = lambda k: pl.BlockSpec((1, 1, d), lambda bi, i: (bi, 0, k))
    full = lambda a: pl.BlockSpec(a.shape, lambda bi, i: (0,) * a.ndim)
    lane_rows = pl.BlockSpec((1, 2, tm), lambda bi, i: (bi, 0, i))
    return pl.pallas_call(
        _merge_kernel,
        grid=(b, s // tm),
        in_specs=[tok(w), tok(w), tok(d), tok(d), tok(d),
                  modspec(2), modspec(3), modspec(4), full(g2),
                  full(wda), full(whg), full(wout), full(wrh), full(wrl), full(br)],
        out_specs=[tok(d), pl.BlockSpec((1, tm // 2 * SUBLANES, LANES), lambda bi, i: (bi, i, 0)),
                   lane_rows, lane_rows],
        out_shape=[jax.ShapeDtypeStruct((b, s, d), F32),
                   jax.ShapeDtypeStruct((b, s // 2 * SUBLANES, LANES), U32),
                   jax.ShapeDtypeStruct((b, 2, s), I32), jax.ShapeDtypeStruct((b, 2, s), F32)],
        compiler_params=pltpu.CompilerParams(vmem_limit_bytes=VMEM_LIMIT),
        name="merge",
    )(yda, yhg, gda, ghg, x, mod, mod, mod, g2, wda, whg, wout, wrh, wrl, br)


def _positions_kernel(e_ref, half_ref, blk_ref, nused_ref):
    e_mat = e_ref[...]
    nr = e_mat.shape[0]
    nb = blk_ref.shape[1]
    r = lax.broadcasted_iota(I32, (LANES, LANES), 0)
    c = lax.broadcasted_iota(I32, (LANES, LANES), 1)
    incl = jnp.where(r <= c, 1.0, 0.0).astype(BF16)
    ones = jnp.ones((LANES, LANES), BF16)
    rr = lax.broadcasted_iota(I32, (nr, nr), 0)
    cc = lax.broadcasted_iota(I32, (nr, nr), 1)
    before = jnp.where(cc < rr, 1.0, 0.0).astype(BF16)
    blk_start = (lax.broadcasted_iota(I32, (1, nb), 1) * EXPERT_BLOCK).astype(F32)

    def per_expert(e, carry):
        pstart, dest, blk = carry
        hit = e_mat == e
        oh = jnp.where(hit, 1.0, 0.0).astype(BF16)
        in_row = _nn(oh, incl)
        row_tot = _nn(oh, ones)
        row_off = _nn(before, row_tot.astype(BF16))
        total = row_off[nr - 1:nr, :] + row_tot[nr - 1:nr, :]
        dest = jnp.where(hit, pstart + row_off + in_row - 1.0, dest)
        pend = pstart + jnp.floor((total + (EXPERT_BLOCK - 1)) * (1.0 / EXPERT_BLOCK)) * EXPERT_BLOCK
        blk = blk + jnp.where(blk_start >= pend[:, :1], 1.0, 0.0)
        return pend, dest, blk

    pstart0 = jnp.zeros((1, LANES), F32)
    pend, dest, blk = lax.fori_loop(
        0, N_EXPERTS, per_expert,
        (pstart0, jnp.zeros(e_mat.shape, F32), jnp.zeros((1, nb), F32)), unroll=4)
    half_rows = EXPERT_BLOCK // 2
    block = jnp.floor(dest * (1.0 / EXPERT_BLOCK))
    in_block = dest - block * EXPERT_BLOCK
    upper = jnp.floor(in_block * (1.0 / half_rows))
    tile = block * half_rows + in_block - upper * half_rows
    half_ref[...] = (SUBLANES * tile + (SUBLANES // 2) * upper).astype(I32)
    blk_ref[...] = jnp.minimum(blk, N_EXPERTS - 1).astype(I32)
    nused_ref[...] = (pend * (1.0 / EXPERT_BLOCK)).astype(I32)


def _positions(e_mat, n_blk_pad):
    nr = e_mat.shape[0]
    return pl.pallas_call(
        _positions_kernel,
        out_shape=[jax.ShapeDtypeStruct((nr, LANES), I32),
                   jax.ShapeDtypeStruct((1, n_blk_pad), I32),
                   jax.ShapeDtypeStruct((1, LANES), I32)],
        name="positions",
    )(e_mat)


def _dispatch_kernel(half_ref, h2_ref, xs_in_ref, xs_ref, hbuf, in_sem, out_sem, *, n_tok):
    del xs_in_ref
    i = pl.program_id(0)
    last = pl.num_programs(0) - 1
    slot = i % 2
    half = SUBLANES // 2
    pairs = GATHER_TILE // 2
    tile_rows = pairs * SUBLANES

    def load(step, buf):
        return pltpu.make_async_copy(h2_ref.at[pl.ds(step * tile_rows, tile_rows)], hbuf.at[buf],
                                     in_sem.at[buf])

    def drain(buf):
        for k in range(2):
            pltpu.make_async_copy(hbuf.at[buf], xs_ref.at[pl.ds(0, tile_rows)], out_sem.at[buf, k]).wait()

    @pl.when(i == 0)
    def _():
        load(0, 0).start()

    @pl.when(i > 0)
    def _():
        drain(1 - slot)

    @pl.when(i < last)
    def _():
        load(i + 1, 1 - slot).start()

    load(i, slot).wait()

    for up in range(2):
        base = i * GATHER_TILE + up * pairs

        def issue(r, carry):
            for k in range(2):
                row = pl.multiple_of(half_ref[k * n_tok + base + r], half)
                pltpu.make_async_copy(hbuf.at[slot, pl.ds(r * SUBLANES + up * half, half)],
                                      xs_ref.at[pl.ds(row, half)], out_sem.at[slot, k]).start(priority=k)
            return carry

        lax.fori_loop(0, pairs, issue, 0, unroll=ISSUE_UNROLL)

    @pl.when(i == last)
    def _():
        drain(slot)


def _dispatch(half, h2, xs_zero):
    n_tok = h2.shape[0] // (SUBLANES // 2)
    return pl.pallas_call(
        functools.partial(_dispatch_kernel, n_tok=n_tok),
        grid_spec=pltpu.PrefetchScalarGridSpec(
            num_scalar_prefetch=1,
            grid=(n_tok // GATHER_TILE,),
            in_specs=[pl.BlockSpec(memory_space=pl.ANY), pl.BlockSpec(memory_space=pl.ANY)],
            out_specs=pl.BlockSpec(memory_space=pl.ANY),
            scratch_shapes=[pltpu.VMEM((2, GATHER_TILE // 2 * SUBLANES, LANES), h2.dtype),
                            pltpu.SemaphoreType.DMA((2,)), pltpu.SemaphoreType.DMA((2, 2))],
        ),
        out_shape=jax.ShapeDtypeStruct(xs_zero.shape, xs_zero.dtype),
        input_output_aliases={2: 0},
        compiler_params=pltpu.CompilerParams(dimension_semantics=("arbitrary",)),
        name="dispatch",
    )(half, h2, xs_zero)


def _expert_kernel(blk_ref, nused_ref, xs_ref, w1_ref, w3_ref, w2_ref, ys_ref,
                   w1b, w3b, w2b, ybuf, cur, sem):
    i = pl.program_id(0)
    n_used = nused_ref[0]
    e = blk_ref[jnp.minimum(i, pl.num_programs(0) - 2)]
    fresh = (i == 0) | (blk_ref[jnp.maximum(i - 1, 0)] != e)

    def fetch(expert, which):
        return (pltpu.make_async_copy(w1_ref.at[expert], w1b.at[which], sem.at[0]),
                pltpu.make_async_copy(w3_ref.at[expert], w3b.at[which], sem.at[1]),
                pltpu.make_async_copy(w2_ref.at[expert], w2b.at[which], sem.at[2]))

    @pl.when(i < n_used)
    def _():
        @pl.when(i == 0)
        def _():
            cur[0] = 0
            for cp in fetch(e, 0):
                cp.start()

        @pl.when(fresh & (i > 0))
        def _():
            cur[0] = 1 - cur[0]

        @pl.when(fresh)
        def _():
            for cp in fetch(e, cur[0]):
                cp.wait()
            nxt = lax.while_loop(lambda j: (j < n_used) & (blk_ref[jnp.minimum(j, n_used - 1)] == e),
                                 lambda j: j + 1, i + 1)

            @pl.when(nxt < n_used)
            def _():
                for cp in fetch(blk_ref[jnp.minimum(nxt, n_used - 1)], 1 - cur[0]):
                    cp.start(priority=1)

    tiles = (xs_ref.shape[0] // SUBLANES, SUBLANES, LANES)

    def compute():
        which = cur[0]
        x = _unpack_rows(xs_ref[...].reshape(tiles).reshape(tiles[0], w1b.shape[1])).astype(BF16)
        a = _nn(x, w1b[which])
        ybuf[...] = _nn((_silu(a) * _nn(x, w3b[which])).astype(BF16), w2b[which])

    def flush():
        ys_ref[...] = _pack_rows(ybuf[...]).reshape(tiles).reshape(ys_ref.shape)

    @pl.when(i == 0)
    def _():
        compute()

    @pl.when((i > 0) & (i < n_used))
    def _():
        flush()
        compute()

    @pl.when((i > 0) & (i == n_used))
    def _():
        flush()


def _experts(blk_e, n_used, xs, w1, w3, w2):
    block_rows = EXPERT_BLOCK // 2 * SUBLANES
    d, de = w1.shape[1], w1.shape[2]
    rows = lambda i, blk, nu: (jnp.minimum(i, nu[0] - 1), 0)
    hbm = pl.BlockSpec(memory_space=pl.ANY)
    return pl.pallas_call(
        _expert_kernel,
        grid_spec=pltpu.PrefetchScalarGridSpec(
            num_scalar_prefetch=2,
            grid=(xs.shape[0] // block_rows + 1,),
            in_specs=[pl.BlockSpec((block_rows, LANES), rows), hbm, hbm, hbm],
            out_specs=pl.BlockSpec((block_rows, LANES),
                                   lambda i, blk, nu: (jnp.minimum(jnp.maximum(i - 1, 0), nu[0] - 1), 0)),
            scratch_shapes=[pltpu.VMEM((2, d, de), BF16), pltpu.VMEM((2, d, de), BF16),
                            pltpu.VMEM((2, de, d), BF16), pltpu.VMEM((EXPERT_BLOCK, d), F32),
                            pltpu.SMEM((1,), I32), pltpu.SemaphoreType.DMA((3,))],
        ),
        out_shape=jax.ShapeDtypeStruct(xs.shape, xs.dtype),
        input_output_aliases={2: 0},
        compiler_params=pltpu.CompilerParams(vmem_limit_bytes=VMEM_LIMIT,
                                             dimension_semantics=("arbitrary",)),
        name="experts",
    )(blk_e, n_used, xs, w1, w3, w2)


def _combine_kernel(half_ref, ys_ref, x1_ref, gt2_ref, wts_ref, g_ref, o_ref, ybuf, wtok, sem, *, n_tok, last):
    i = pl.program_id(0)
    last_step = pl.num_programs(0) - 1
    slot = i % 2
    d = x1_ref.shape[1]
    half = SUBLANES // 2
    pairs = GATHER_TILE // 2
    n_groups = pairs // ISSUE_UNROLL

    def gather(step, buf, j0):
        for up in range(2):
            base = step * GATHER_TILE + up * pairs + j0
            for u in range(ISSUE_UNROLL):
                for k in range(2):
                    row = pl.multiple_of(half_ref[k * n_tok + base + u], half)
                    pltpu.make_async_copy(ys_ref.at[pl.ds(row, half)],
                                          ybuf.at[buf, k, pl.ds((j0 + u) * SUBLANES + up * half, half)],
                                          sem.at[buf, k]).start(priority=k)

    def drain(buf):
        for k in range(2):
            pltpu.make_async_copy(ys_ref.at[pl.ds(0, pairs * SUBLANES)], ybuf.at[buf, k], sem.at[buf, k]).wait()

    @pl.when(i == 0)
    def _():
        def first(g, carry):
            gather(0, 0, g * ISSUE_UNROLL)
            return carry

        lax.fori_loop(0, n_groups, first, 0)

    drain(slot)

    nxt = jnp.minimum(i + 1, last_step)
    gt2 = gt2_ref[0]
    gain = g_ref[...]
    w_rows = wts_ref[0]
    wtok[...] = jnp.concatenate([w_rows, jnp.zeros((SUBLANES - w_rows.shape[0], w_rows.shape[1]), F32)], axis=0).T

    def group(g, carry):
        j0 = pl.multiple_of(g * ISSUE_UNROLL, ISSUE_UNROLL)
        lower, upper = pl.ds(j0, ISSUE_UNROLL), pl.ds(pairs + j0, ISSUE_UNROLL)
        tile_rows = pl.ds(pl.multiple_of(j0 * SUBLANES, ISSUE_UNROLL * SUBLANES), ISSUE_UNROLL * SUBLANES)

        def tokens(k):
            tiles = ybuf[slot, k, tile_rows, :].reshape(ISSUE_UNROLL, SUBLANES, LANES)
            return _unpack_rows(tiles.reshape(ISSUE_UNROLL, d))

        w = jnp.concatenate([wtok[lower, :], wtok[upper, :]], axis=0)
        x = jnp.concatenate([x1_ref[lower, :], x1_ref[upper, :]], axis=0)
        x = x + gt2 * (w[:, 0:1] * tokens(0) + w[:, 1:2] * tokens(1))
        if last:
            x = x * lax.rsqrt(jnp.mean(x * x, axis=-1, keepdims=True) + EPS) * gain
        o_ref[lower, :] = x[:ISSUE_UNROLL]
        o_ref[upper, :] = x[ISSUE_UNROLL:]
        gather(nxt, 1 - slot, j0)
        return carry

    lax.fori_loop(0, n_groups, group, 0, unroll=COMBINE_UNROLL)

    @pl.when(i == last_step)
    def _():
        drain(1 - slot)


def _combine(half, ys, x1, mod, wts, final_g, seq, last=True):
    n_tok, d = x1.shape
    tt = GATHER_TILE
    per_batch = seq // tt
    return pl.pallas_call(
        functools.partial(_combine_kernel, n_tok=n_tok, last=last),
        grid_spec=pltpu.PrefetchScalarGridSpec(
            num_scalar_prefetch=1,
            grid=(n_tok // tt,),
            in_specs=[pl.BlockSpec(memory_space=pl.ANY),
                      pl.BlockSpec((tt, d), lambda i, half: (i, 0)),
                      pl.BlockSpec((1, 1, d), lambda i, half: (i // per_batch, 0, 5)),
                      pl.BlockSpec((1, 2, tt), lambda i, half: (i // per_batch, 0, i % per_batch)),
                      pl.BlockSpec((1, d), lambda i, half: (0, 0))],
            out_specs=pl.BlockSpec((tt, d), lambda i, half: (i, 0)),
            scratch_shapes=[pltpu.VMEM((2, 2, tt // 2 * SUBLANES, LANES), U32),
                            pltpu.VMEM((tt, SUBLANES), F32),
                            pltpu.SemaphoreType.DMA((2, 2))],
        ),
        out_shape=jax.ShapeDtypeStruct((n_tok, d), F32),
        compiler_params=pltpu.CompilerParams(dimension_semantics=("arbitrary",)),
        name="combine",
    )(half, ys, x1, mod, wts, final_g)


def kernel(x, c, w_ada, b_ada, norm1_g, w_in, lambda_q1, lambda_k1, lambda_q2, lambda_k2, da_subln_g, hg_lb_logits, hg_norm_g, w_up_da, w_up_hg, w_out, norm2_g, w_rg, b_rg, w_re, b_re, w1, w3, w2, final_g):
    b, s, d = x.shape
    n_tok = b * s
    depth = w_ada.shape[0]
    assert s % ROW_TILE == 0 and s % ATT_BLOCK == 0 and GATHER_TILE == ROW_TILE
    assert (2 * n_tok) % LANES == 0 and d == SUBLANES * LANES

    c_pad = jnp.pad(c, ((0, (-b) % 8), (0, 0)))
    n_slots = 2 * n_tok + N_EXPERTS * EXPERT_BLOCK
    n_blk = n_slots // EXPERT_BLOCK
    n_blk_pad = -(-n_blk // LANES) * LANES

    for l in range(depth):
        lam_init = 0.8 - 0.6 * math.exp(-0.3 * l)
        mod = _adaln(c_pad, w_ada[l], b_ada[l][None, :])[:b].reshape(b, 1, 6 * d)

        n_steps = n_tok // ROW_TILE
        sliced = lambda a: a.reshape((n_steps, a.shape[0] // n_steps) + a.shape[1:])
        (qda, kda, vda, qd, kd, ke, hi, sg, dec, gda, ghg,
         w1_bf, w3_bf, w2_bf, wda_bf, whg_bf, wout_bf) = _inproj(
            x, mod, norm1_g[l][None, :], w_in[l].astype(BF16), hg_lb_logits, l,
            (w1[l], w3[l], w2[l], sliced(w_up_da[l]), sliced(w_up_hg[l]), sliced(w_out[l])))
        yda, xs_zero = _attention(qda, kda, vda, lambda_q1[l][None, :], lambda_k1[l][None, :],
                                  lambda_q2[l][None, :], lambda_k2[l][None, :], da_subln_g[l][None, :],
                                  lam_init, jax.ShapeDtypeStruct((n_slots // 2 * SUBLANES, LANES), U32))
        yhg = _hgrn(qd, kd, ke, hi, sg, dec, hg_norm_g[l][None, :])

        wr = jnp.concatenate([w_re[l].T, w_rg[l].T,
                              jnp.zeros((ROUTER_ROWS - N_EXPERTS - N_GROUPS, d), F32)], axis=0)
        wr_hi = wr.astype(BF16)
        wr_lo = (wr - wr_hi.astype(F32)).astype(BF16)
        br = jnp.concatenate([b_re[l], b_rg[l],
                              jnp.zeros((ROUTER_ROWS - N_EXPERTS - N_GROUPS,), F32)])[:, None]
        x1, h2, eidx, wts = _merge(yda, yhg, gda, ghg, x, mod, norm2_g[l][None, :],
                                   wda_bf.reshape(w_up_da[l].shape), whg_bf.reshape(w_up_hg[l].shape),
                                   wout_bf.reshape(w_out[l].shape), wr_hi, wr_lo, br)

        e_mat = jnp.transpose(eidx, (1, 0, 2)).reshape(2 * n_tok // LANES, LANES)
        half, blk_e, n_used = _positions(e_mat, n_blk_pad)
        half = half.reshape(2 * n_tok)
        xs = _dispatch(half, h2.reshape(b * h2.shape[1], LANES), xs_zero)
        ys = _experts(blk_e[0, :n_blk], n_used[0, :1], xs, w1_bf, w3_bf, w2_bf)
        x = _combine(half, ys, x1.reshape(n_tok, d), mod, wts, final_g[None, :], s,
                     last=(l == depth - 1)).reshape(b, s, d)
    return x
```

```python
import functools
import math

import jax
import jax.numpy as jnp
from jax import lax
from jax.experimental import pallas as pl
from jax.experimental.pallas import tpu as pltpu

F32 = jnp.float32
BF16 = jnp.bfloat16
I32 = jnp.int32
U32 = jnp.uint32

EPS = 1e-6
NEG = -1e30
LOG2E = math.log2(math.e)

CHUNK = 64
DA_HEADS = 4
DA_HEAD_DIM = 64
HG_HEADS = 4
HG_DK = 128
HG_DV = 128
N_GROUPS = 4
EXPERTS_PER_GROUP = 8
N_EXPERTS = N_GROUPS * EXPERTS_PER_GROUP

LANES = 128
SUBLANES = 8
ADALN_COLS = 1024
ADALN_BUFFERS = 3
ROW_TILE = 512
ATT_BLOCK = 512
BIAS_TERMS = 3
EXPERT_BLOCK = 256
GATHER_TILE = 512
ISSUE_UNROLL = 8
COMBINE_UNROLL = 4
ROUTER_ROWS = 48
VMEM_LIMIT = 52 * 1024 * 1024


def _nt(a, b):
    return lax.dot_general(a, b, (((1,), (1,)), ((), ())), preferred_element_type=F32)


def _nn(a, b):
    return jnp.dot(a, b, preferred_element_type=F32)


def _split(a):
    hi = a.astype(BF16)
    lo = (a - hi.astype(F32)).astype(BF16)
    return hi, lo


def _sigmoid(x):
    return 1.0 / (1.0 + jnp.exp(-x))


def _silu(x):
    return x * _sigmoid(x)


def _pack_rows(y):
    rows, cols = y.shape[0] // 2, y.shape[1] // 2
    bits = lax.bitcast_convert_type(y.astype(BF16).astype(F32), U32)
    words = bits[:, :cols] | lax.shift_right_logical(bits[:, cols:], jnp.uint32(16))
    return jnp.concatenate([words[:rows], words[rows:]], axis=1)


def _unpack_rows(packed):
    cols = packed.shape[1] // 2
    words = jnp.concatenate([packed[:, :cols], packed[:, cols:]], axis=0)
    first = lax.bitcast_convert_type(words & jnp.uint32(0xFFFF0000), F32)
    second = lax.bitcast_convert_type(lax.shift_left(words, jnp.uint32(16)), F32)
    return jnp.concatenate([first, second], axis=1)


def _adaln_kernel(c_ref, w_ref, b_ref, o_ref, wbuf, sem):
    n_buf, _, tn = wbuf.shape
    n_chunks = w_ref.shape[1] // tn

    def fetch(j):
        return pltpu.make_async_copy(w_ref.at[:, pl.ds(j * tn, tn)], wbuf.at[j % n_buf], sem.at[j % n_buf])

    for j in range(min(n_buf, n_chunks)):
        fetch(j).start()
    c_hi, c_lo = _split(_silu(c_ref[...]))
    for j in range(n_chunks):
        cols = slice(j * tn, (j + 1) * tn)
        fetch(j).wait()
        w_hi, w_lo = _split(wbuf[j % n_buf])
        o_ref[:, cols] = _nn(c_hi, w_hi) + _nn(c_lo, w_hi) + _nn(c_hi, w_lo) + b_ref[:, cols]
        if j + n_buf < n_chunks:
            fetch(j + n_buf).start()


def _adaln(c_pad, w, b):
    rows, d = c_pad.shape
    n = w.shape[1]
    assert n % ADALN_COLS == 0
    vmem = pl.BlockSpec(memory_space=pltpu.VMEM)
    return pl.pallas_call(
        _adaln_kernel,
        in_specs=[vmem, pl.BlockSpec(memory_space=pl.ANY), vmem],
        out_specs=vmem,
        out_shape=jax.ShapeDtypeStruct((rows, n), F32),
        scratch_shapes=[pltpu.VMEM((ADALN_BUFFERS, d, ADALN_COLS), F32),
                        pltpu.SemaphoreType.DMA((ADALN_BUFFERS,))],
        compiler_params=pltpu.CompilerParams(vmem_limit_bytes=VMEM_LIMIT),
        name="adaln",
    )(c_pad, w, b)


def _inproj_kernel(x_ref, sh_ref, sc_ref, g_ref, w_ref, lb_ref, *refs, layer, n_cast):
    (qda_ref, kda_ref, vda_ref, qd_ref, kd_ref, ke_ref, hi_ref, sg_ref,
     dec_ref, gda_ref, ghg_ref) = refs[n_cast:-n_cast]
    for src_ref, dst_ref in zip(refs[:n_cast], refs[-n_cast:]):
        dst_ref[...] = src_ref[...].astype(BF16)

    x = x_ref[0]
    tm = x.shape[0]
    h = x * lax.rsqrt(jnp.mean(x * x, axis=-1, keepdims=True) + EPS) * g_ref[...]
    h = h * (1.0 + sc_ref[0]) + sh_ref[0]
    hb = h.astype(BF16)

    def proj(c0, width):
        return _nn(hb, w_ref[:, c0:c0 + width])

    w_da = DA_HEADS * 2 * DA_HEAD_DIM
    w_hg = HG_HEADS * HG_DK
    qda_ref[0] = (proj(0, w_da) * (LOG2E / math.sqrt(DA_HEAD_DIM))).astype(BF16)
    kda_ref[0] = proj(w_da, w_da).astype(BF16)
    vda_ref[0] = proj(2 * w_da, w_da).astype(BF16)
    c0 = 3 * w_da

    lbl = lb_ref[...]
    lbe = jnp.exp(lbl - jnp.max(lbl, axis=0, keepdims=True))
    lbs = lbe / jnp.sum(lbe, axis=0, keepdims=True)
    lb = jnp.sum(lbs[:layer + 1], axis=0, keepdims=True)

    f = lb + (1.0 - lb) * _sigmoid(proj(c0 + w_hg, w_hg))
    logf = jnp.log(f)
    kk = 1.0 - f
    pos = lax.broadcasted_iota(I32, logf.shape, 0) & (CHUNK - 1)
    a = logf
    step = 1
    while step < CHUNK:
        a = a + jnp.where(pos >= step, pltpu.roll(a, step, axis=0), 0.0)
        step *= 2
    n_ch = tm // CHUNK
    a = a.reshape(n_ch, CHUNK, w_hg)
    a_last = a[:, CHUNK - 1:CHUNK, :]
    kk3 = kk.reshape(n_ch, CHUNK, w_hg)
    qq3 = _silu(proj(c0, w_hg)).reshape(n_ch, CHUNK, w_hg)
    qd_ref[0] = (qq3 * jnp.exp(a)).reshape(tm, w_hg).astype(BF16)
    kd_ref[0] = (kk3 * jnp.exp(-a)).reshape(tm, w_hg).astype(BF16)
    ke_ref[0] = (kk3 * jnp.exp(a_last - a)).reshape(tm, w_hg).astype(BF16)
    dec_ref[0] = jnp.exp(a_last).reshape(n_ch, w_hg)
    hi_ref[0] = proj(c0 + 2 * w_hg, w_hg).astype(BF16)
    sg_ref[0] = _silu(proj(c0 + 3 * w_hg, w_hg)).astype(BF16)
    c1 = c0 + 4 * w_hg
    d = x.shape[1]
    gda_ref[0] = _sigmoid(proj(c1, d)).astype(BF16)
    ghg_ref[0] = _sigmoid(proj(c1 + d, d)).astype(BF16)


def _inproj(x, mod, g, w_bf, lb_logits, layer, to_cast):
    b, s, d = x.shape
    tm = ROW_TILE
    n_cols = w_bf.shape[1]
    per_b = s // tm
    assert all(a.shape[0] % (b * per_b) == 0 for a in to_cast)
    wspec = lambda a: pl.BlockSpec((a.shape[0] // (b * per_b),) + a.shape[1:],
                                   lambda bi, i: (bi * per_b + i, 0, 0))
    w_da = DA_HEADS * 2 * DA_HEAD_DIM
    w_hg = HG_HEADS * HG_DK

    def tok(width, dtype):
        return (pl.BlockSpec((1, tm, width), lambda bi, i: (bi, i, 0)),
                jax.ShapeDtypeStruct((b, s, width), dtype))

    outs = [tok(w_da, BF16)] * 3 + [tok(w_hg, BF16)] * 5
    outs.append((pl.BlockSpec((1, tm // CHUNK, w_hg), lambda bi, i: (bi, i, 0)),
                 jax.ShapeDtypeStruct((b, s // CHUNK, w_hg), F32)))
    outs += [tok(d, BF16)] * 2
    outs += [(wspec(a), jax.ShapeDtypeStruct(a.shape, BF16)) for a in to_cast]
    return pl.pallas_call(
        functools.partial(_inproj_kernel, layer=layer, n_cast=len(to_cast)),
        grid=(b, s // tm),
        in_specs=[
            pl.BlockSpec((1, tm, d), lambda bi, i: (bi, i, 0)),
            pl.BlockSpec((1, 1, d), lambda bi, i: (bi, 0, 0)),
            pl.BlockSpec((1, 1, d), lambda bi, i: (bi, 0, 1)),
            pl.BlockSpec((1, d), lambda bi, i: (0, 0)),
            pl.BlockSpec((d, n_cols), lambda bi, i: (0, 0), pipeline_mode=pl.Buffered(1)),
            pl.BlockSpec(lb_logits.shape, lambda bi, i: (0, 0)),
        ] + [wspec(a) for a in to_cast],
        out_specs=[o[0] for o in outs],
        out_shape=[o[1] for o in outs],
        compiler_params=pltpu.CompilerParams(vmem_limit_bytes=VMEM_LIMIT),
        name="inproj",
    )(x, mod, mod, g, w_bf, lb_logits, *to_cast)


def _attn_kernel(q_ref, k_ref, v_ref, lq1_ref, lk1_ref, lq2_ref, lk2_ref, g_ref, o_ref, zs_ref,
                 kaug_ref, vaug_ref, qaug_ref, m_ref, acc_ref, sa_ref, sb_ref, zero_ref, zsem,
                 *, lam_init, slopes):
    hd = pl.program_id(1)
    i = pl.program_id(2)

    step = (pl.program_id(0) * pl.num_programs(1) + hd) * pl.num_programs(2) + i
    n_steps = pl.num_programs(0) * pl.num_programs(1) * pl.num_programs(2)
    zrows = zero_ref.shape[0]

    def zero_copy(t):
        return pltpu.make_async_copy(zero_ref, zs_ref.at[pl.ds(t * zrows, zrows)], zsem)

    @pl.when(step == 0)
    def _():
        zero_ref[...] = jnp.zeros(zero_ref.shape, zero_ref.dtype)

    @pl.when(step > 0)
    def _():
        zero_copy(step - 1).wait()

    zero_copy(step).start()

    @pl.when(step == n_steps - 1)
    def _():
        zero_copy(step).wait()
    tq = q_ref.shape[1]
    hw = q_ref.shape[2]
    n_key = k_ref.shape[1]

    slope = jnp.float32(slopes[-1])
    for idx in range(len(slopes) - 2, -1, -1):
        slope = jnp.where(hd == idx, jnp.float32(slopes[idx]), slope)

    @pl.when(i == 0)
    def _():
        kaug_ref[:hw, :] = k_ref[0].astype(F32).T.astype(BF16)
        jrel = (lax.broadcasted_iota(I32, (hw, n_key), 1) & (tq - 1)).astype(F32) * slope
        krow = lax.broadcasted_iota(I32, (hw, n_key), 0)
        bias_rows = jnp.zeros((hw, n_key), F32)
        rest = jrel
        for term in range(BIAS_TERMS):
            part = rest.astype(BF16).astype(F32)
            bias_rows = jnp.where(krow == term, part, bias_rows)
            rest = rest - part
        kaug_ref[hw:, :] = bias_rows.astype(BF16)
        vaug_ref[:, :hw] = v_ref[0]
        vaug_ref[:, hw:] = jnp.ones((n_key, hw), BF16)

    q = q_ref[0]
    lane = lax.broadcasted_iota(I32, q.shape, 1)
    zero = jnp.zeros_like(q)
    ones = jnp.where(lane < BIAS_TERMS, 1.0, 0.0).astype(BF16)
    qaug_ref[:tq, :hw] = jnp.where(lane < DA_HEAD_DIM, q, zero)
    qaug_ref[tq:, :hw] = jnp.where(lane >= DA_HEAD_DIM, q, zero)
    qaug_ref[:tq, hw:] = ones
    qaug_ref[tq:, hw:] = ones

    m_ref[...] = jnp.full(m_ref.shape, NEG, F32)
    acc_ref[...] = jnp.zeros(acc_ref.shape, F32)

    def scores(j, buf):
        buf[...] = _nn(qaug_ref[...], kaug_ref[:, pl.ds(pl.multiple_of(j * tq, tq), tq)])

    def consume(j, buf, diagonal):
        s = buf[...]
        if diagonal:
            ii = lax.broadcasted_iota(I32, s.shape, 0) & (tq - 1)
            jj = lax.broadcasted_iota(I32, s.shape, 1)
            ahead = jnp.minimum(ii - jj, 0).astype(F32)
            s = s + jnp.where((jj // CHUNK) <= (ii // CHUNK), (2.0 * slope) * ahead, NEG)
            shift = jnp.float32(0.0)
        else:
            shift = -slope * ((i - j) * tq).astype(F32)
        m_old = m_ref[...]
        m_new = jnp.maximum(m_old, jnp.max(s, axis=-1, keepdims=True) + shift)
        p = jnp.exp2(s - jnp.concatenate([m_new - shift] * (tq // LANES), axis=1))
        alpha = jnp.concatenate([jnp.exp2(m_old - m_new)] * (acc_ref.shape[1] // LANES), axis=1)
        vb = vaug_ref[pl.ds(pl.multiple_of(j * tq, tq), tq), :]
        acc_ref[...] = alpha * acc_ref[...] + _nn(p.astype(BF16), vb)
        m_ref[...] = m_new

    scores(0, sa_ref)

    def pair(pi, carry):
        j = 2 * pi
        scores(j + 1, sb_ref)
        consume(j, sa_ref, False)
        scores(j + 2, sa_ref)
        consume(j + 1, sb_ref, False)
        return carry

    lax.fori_loop(0, i // 2, pair, 0)

    @pl.when(i % 2 == 0)
    def _():
        consume(i, sa_ref, True)

    @pl.when(i % 2 == 1)
    def _():
        scores(i, sb_ref)
        consume(i - 1, sa_ref, False)
        consume(i, sb_ref, True)

    acc = acc_ref[...]
    o1 = acc[:tq, :hw] / acc[:tq, hw:]
    o2 = acc[tq:, :hw] / acc[tq:, hw:]
    lam = (jnp.exp(jnp.sum(lq1_ref[...] * lk1_ref[...], axis=-1, keepdims=True))
           - jnp.exp(jnp.sum(lq2_ref[...] * lk2_ref[...], axis=-1, keepdims=True)) + lam_init)
    o = o1 - lam * o2
    o = o * lax.rsqrt(jnp.mean(o * o, axis=-1, keepdims=True) + EPS) * g_ref[...]
    o_ref[0] = (o * (1.0 - lam_init)).astype(BF16)


def _attention(q, k, v, lq1, lk1, lq2, lk2, subln_g, lam_init, zero_buf):
    b, s, _ = q.shape
    zero_shape = zero_buf.shape
    tq = ATT_BLOCK
    hw = 2 * DA_HEAD_DIM
    assert hw == LANES
    n_steps = b * DA_HEADS * (s // tq)
    assert zero_shape[0] % n_steps == 0
    zrows = zero_shape[0] // n_steps
    slopes = tuple(LOG2E * 2.0 ** (-8.0 * (h + 1) / DA_HEADS) for h in range(DA_HEADS))
    vec = lambda n: pl.BlockSpec((1, n), lambda bi, h, i: (0, 0))
    return pl.pallas_call(
        functools.partial(_attn_kernel, lam_init=lam_init, slopes=slopes),
        grid=(b, DA_HEADS, s // tq),
        in_specs=[
            pl.BlockSpec((1, tq, hw), lambda bi, h, i: (bi, i, h)),
            pl.BlockSpec((1, s, hw), lambda bi, h, i: (bi, 0, h)),
            pl.BlockSpec((1, s, hw), lambda bi, h, i: (bi, 0, h)),
            vec(DA_HEAD_DIM), vec(DA_HEAD_DIM), vec(DA_HEAD_DIM), vec(DA_HEAD_DIM), vec(hw),
        ],
        out_specs=[pl.BlockSpec((1, tq, hw), lambda bi, h, i: (bi, i, h)),
                   pl.BlockSpec(memory_space=pl.ANY)],
        out_shape=[jax.ShapeDtypeStruct((b, s, DA_HEADS * hw), BF16),
                   zero_buf],
        scratch_shapes=[
            pltpu.VMEM((2 * hw, s), BF16),
            pltpu.VMEM((s, 2 * hw), BF16),
            pltpu.VMEM((2 * tq, 2 * hw), BF16),
            pltpu.VMEM((2 * tq, LANES), F32),
            pltpu.VMEM((2 * tq, 2 * hw), F32),
            pltpu.VMEM((2 * tq, tq), F32),
            pltpu.VMEM((2 * tq, tq), F32),
            pltpu.VMEM((zrows,) + tuple(zero_shape[1:]), zero_buf.dtype),
            pltpu.SemaphoreType.DMA(()),
        ],
        compiler_params=pltpu.CompilerParams(vmem_limit_bytes=VMEM_LIMIT,
                                             dimension_semantics=("arbitrary",) * 3),
        name="attention",
    )(q, k, v, lq1, lk1, lq2, lk2, subln_g)


def _hgrn_kernel(qd_ref, kd_ref, ke_ref, v_ref, sg_ref, dec_ref, g_ref, o_ref, st_ref):
    @pl.when(pl.program_id(1) == 0)
    def _():
        st_ref[...] = jnp.zeros(st_ref.shape, F32)

    n_ch = qd_ref.shape[1] // CHUNK
    r = lax.broadcasted_iota(I32, (CHUNK, CHUNK), 0)
    c = lax.broadcasted_iota(I32, (CHUNK, CHUNK), 1)
    causal = r >= c

    heads = [slice(h * HG_DK, (h + 1) * HG_DK) for h in range(HG_HEADS)]
    chunks = [slice(ci * CHUNK, (ci + 1) * CHUNK) for ci in range(n_ch)]
    incr = [[_nn(v_ref[0, rows, cols].astype(F32).T.astype(BF16), ke_ref[0, rows, cols])
             for rows in chunks] for cols in heads]
    start = []
    for h, cols in enumerate(heads):
        st = st_ref[h]
        before = []
        for ci in range(n_ch):
            before.append(st.astype(BF16))
            st = st * dec_ref[0, ci:ci + 1, cols] + incr[h][ci]
        st_ref[h] = st
        start.append(before)
    for h, cols in enumerate(heads):
        for ci, rows in enumerate(chunks):
            qd = qd_ref[0, rows, cols]
            scores = jnp.where(causal, _nt(qd, kd_ref[0, rows, cols]), 0.0).astype(BF16)
            o = _nn(scores, v_ref[0, rows, cols]) + _nt(qd, start[h][ci])
            o = o * lax.rsqrt(jnp.mean(o * o, axis=-1, keepdims=True) + EPS) * g_ref[...]
            o_ref[0, rows, cols] = (o * sg_ref[0, rows, cols].astype(F32)).astype(BF16)


def _hgrn(qd, kd, ke, v, sg, dec, norm_g):
    b, s, w = qd.shape
    ts = ROW_TILE
    tok = pl.BlockSpec((1, ts, w), lambda bi, i: (bi, i, 0))
    return pl.pallas_call(
        _hgrn_kernel,
        grid=(b, s // ts),
        in_specs=[tok, tok, tok, tok, tok,
                  pl.BlockSpec((1, ts // CHUNK, w), lambda bi, i: (bi, i, 0)),
                  pl.BlockSpec((1, HG_DV), lambda bi, i: (0, 0))],
        out_specs=tok,
        out_shape=jax.ShapeDtypeStruct((b, s, w), BF16),
        scratch_shapes=[pltpu.VMEM((HG_HEADS, HG_DV, HG_DK), F32)],
        name="hgrn",
    )(qd, kd, ke, v, sg, dec, norm_g)


def _merge_kernel(yda_ref, yhg_ref, gda_ref, ghg_ref, x_ref, gt1_ref, sh2_ref, sc2_ref, g2_ref,
                  wda_ref, whg_ref, wout_ref, wrh_ref, wrl_ref, br_ref,
                  x1_ref, h2_ref, eidx_ref, wts_ref):
    u = (gda_ref[0].astype(F32) * _nn(yda_ref[0], wda_ref[...])
         + ghg_ref[0].astype(F32) * _nn(yhg_ref[0], whg_ref[...]))
    x1 = x_ref[0] + gt1_ref[0] * _nn(u.astype(BF16), wout_ref[...])
    x1_ref[0] = x1
    h2 = x1 * lax.rsqrt(jnp.mean(x1 * x1, axis=-1, keepdims=True) + EPS) * g2_ref[...]
    h2 = h2 * (1.0 + sc2_ref[0]) + sh2_ref[0]
    h2_ref[0] = _pack_rows(h2).reshape(h2.shape[0] // 2, SUBLANES, LANES).reshape(h2_ref.shape[1:])

    h_hi, h_lo = _split(h2)
    wrh = wrh_ref[...]
    both = _nt(jnp.concatenate([wrh, wrl_ref[...]], axis=0), h_hi)
    lg = both[:ROUTER_ROWS] + both[ROUTER_ROWS:] + _nt(wrh, h_lo) + br_ref[...]

    tm = h2.shape[0]
    eg = EXPERTS_PER_GROUP
    gl = lg[N_EXPERTS:N_EXPERTS + N_GROUPS]
    gmax = jnp.max(gl, axis=0, keepdims=True)
    g_p = 1.0 / jnp.sum(jnp.exp(gl - gmax), axis=0, keepdims=True)
    gi = lax.broadcasted_iota(I32, (N_GROUPS, tm), 0)
    g_idx = jnp.min(jnp.where(gl == gmax, gi, N_GROUPS), axis=0, keepdims=True)
    sel = lg[(N_GROUPS - 1) * eg:N_GROUPS * eg]
    for g in range(N_GROUPS - 2, -1, -1):
        sel = jnp.where(g_idx == g, lg[g * eg:(g + 1) * eg], sel)
    ei = lax.broadcasted_iota(I32, (eg, tm), 0)
    m1 = jnp.max(sel, axis=0, keepdims=True)
    i1 = jnp.min(jnp.where(sel == m1, ei, eg), axis=0, keepdims=True)
    sel2 = jnp.where(ei == i1, -jnp.inf, sel)
    m2 = jnp.max(sel2, axis=0, keepdims=True)
    i2 = jnp.min(jnp.where(sel2 == m2, ei, eg), axis=0, keepdims=True)
    t = jnp.exp(m2 - m1)
    wa = 1.0 / (1.0 + t)
    eidx_ref[0] = jnp.concatenate([g_idx * eg + i1, g_idx * eg + i2], axis=0)
    wts_ref[0] = jnp.concatenate([g_p * wa, g_p * (t * wa)], axis=0)


def _merge(yda, yhg, gda, ghg, x, mod, g2, wda, whg, wout, wrh, wrl, br):
    b, s, d = x.shape
    tm = ROW_TILE
    w = yda.shape[2]
    tok = lambda width: pl.BlockSpec((1, tm, width), lambda bi, i: (bi, i, 0))
    modspec = lambda k: pl.BlockSpec((1, 1, d), lambda bi, i: (bi, 0, k))
    full = lambda a: pl.BlockSpec(a.shape, lambda bi, i: (0,) * a.ndim)
    lane_rows = pl.BlockSpec((1, 2, tm), lambda bi, i: (bi, 0, i))
    return pl.pallas_call(
        _merge_kernel,
        grid=(b, s // tm),
        in_specs=[tok(w), tok(w), tok(d), tok(d), tok(d),
                  modspec(2), modspec(3), modspec(4), full(g2),
                  full(wda), full(whg), full(wout), full(wrh), full(wrl), full(br)],
        out_specs=[tok(d), pl.BlockSpec((1, tm // 2 * SUBLANES, LANES), lambda bi, i: (bi, i, 0)),
                   lane_rows, lane_rows],
        out_shape=[jax.ShapeDtypeStruct((b, s, d), F32),
                   jax.ShapeDtypeStruct((b, s // 2 * SUBLANES, LANES), U32),
                   jax.ShapeDtypeStruct((b, 2, s), I32), jax.ShapeDtypeStruct((b, 2, s), F32)],
        compiler_params=pltpu.CompilerParams(vmem_limit_bytes=VMEM_LIMIT),
        name="merge",
    )(yda, yhg, gda, ghg, x, mod, mod, mod, g2, wda, whg, wout, wrh, wrl, br)


def _positions_kernel(e_ref, half_ref, blk_ref, nused_ref):
    e_mat = e_ref[...]
    nr = e_mat.shape[0]
    nb = blk_ref.shape[1]
    r = lax.broadcasted_iota(I32, (LANES, LANES), 0)
    c = lax.broadcasted_iota(I32, (LANES, LANES), 1)
    incl = jnp.where(r <= c, 1.0, 0.0).astype(BF16)
    ones = jnp.ones((LANES, LANES), BF16)
    rr = lax.broadcasted_iota(I32, (nr, nr), 0)
    cc = lax.broadcasted_iota(I32, (nr, nr), 1)
    before = jnp.where(cc < rr, 1.0, 0.0).astype(BF16)
    blk_start = (lax.broadcasted_iota(I32, (1, nb), 1) * EXPERT_BLOCK).astype(F32)

    def per_expert(e, carry):
        pstart, dest, blk = carry
        hit = e_mat == e
        oh = jnp.where(hit, 1.0, 0.0).astype(BF16)
        in_row = _nn(oh, incl)
        row_tot = _nn(oh, ones)
        row_off = _nn(before, row_tot.astype(BF16))
        total = row_off[nr - 1:nr, :] + row_tot[nr - 1:nr, :]
        dest = jnp.where(hit, pstart + row_off + in_row - 1.0, dest)
        pend = pstart + jnp.floor((total + (EXPERT_BLOCK - 1)) * (1.0 / EXPERT_BLOCK)) * EXPERT_BLOCK
        blk = blk + jnp.where(blk_start >= pend[:, :1], 1.0, 0.0)
        return pend, dest, blk

    pstart0 = jnp.zeros((1, LANES), F32)
    pend, dest, blk = lax.fori_loop(
        0, N_EXPERTS, per_expert,
        (pstart0, jnp.zeros(e_mat.shape, F32), jnp.zeros((1, nb), F32)), unroll=4)
    half_rows = EXPERT_BLOCK // 2
    block = jnp.floor(dest * (1.0 / EXPERT_BLOCK))
    in_block = dest - block * EXPERT_BLOCK
    upper = jnp.floor(in_block * (1.0 / half_rows))
    tile = block * half_rows + in_block - upper * half_rows
    half_ref[...] = (SUBLANES * tile + (SUBLANES // 2) * upper).astype(I32)
    blk_ref[...] = jnp.minimum(blk, N_EXPERTS - 1).astype(I32)
    nused_ref[...] = (pend * (1.0 / EXPERT_BLOCK)).astype(I32)


def _positions(e_mat, n_blk_pad):
    nr = e_mat.shape[0]
    return pl.pallas_call(
        _positions_kernel,
        out_shape=[jax.ShapeDtypeStruct((nr, LANES), I32),
                   jax.ShapeDtypeStruct((1, n_blk_pad), I32),
                   jax.ShapeDtypeStruct((1, LANES), I32)],
        name="positions",
    )(e_mat)


def _dispatch_kernel(half_ref, h2_ref, xs_in_ref, xs_ref, hbuf, in_sem, out_sem, *, n_tok):
    del xs_in_ref
    i = pl.program_id(0)
    last = pl.num_programs(0) - 1
    slot = i % 2
    half = SUBLANES // 2
    pairs = GATHER_TILE // 2
    tile_rows = pairs * SUBLANES

    def load(step, buf):
        return pltpu.make_async_copy(h2_ref.at[pl.ds(step * tile_rows, tile_rows)], hbuf.at[buf],
                                     in_sem.at[buf])

    def drain(buf):
        for k in range(2):
            pltpu.make_async_copy(hbuf.at[buf], xs_ref.at[pl.ds(0, tile_rows)], out_sem.at[buf, k]).wait()

    @pl.when(i == 0)
    def _():
        load(0, 0).start()

    @pl.when(i > 0)
    def _():
        drain(1 - slot)

    @pl.when(i < last)
    def _():
        load(i + 1, 1 - slot).start()

    load(i, slot).wait()

    for up in range(2):
        base = i * GATHER_TILE + up * pairs

        def issue(r, carry):
            for k in range(2):
                row = pl.multiple_of(half_ref[k * n_tok + base + r], half)
                pltpu.make_async_copy(hbuf.at[slot, pl.ds(r * SUBLANES + up * half, half)],
                                      xs_ref.at[pl.ds(row, half)], out_sem.at[slot, k]).start(priority=k)
            return carry

        lax.fori_loop(0, pairs, issue, 0, unroll=ISSUE_UNROLL)

    @pl.when(i == last)
    def _():
        drain(slot)


def _dispatch(half, h2, xs_zero):
    n_tok = h2.shape[0] // (SUBLANES // 2)
    return pl.pallas_call(
        functools.partial(_dispatch_kernel, n_tok=n_tok),
        grid_spec=pltpu.PrefetchScalarGridSpec(
            num_scalar_prefetch=1,
            grid=(n_tok // GATHER_TILE,),
            in_specs=[pl.BlockSpec(memory_space=pl.ANY), pl.BlockSpec(memory_space=pl.ANY)],
            out_specs=pl.BlockSpec(memory_space=pl.ANY),
            scratch_shapes=[pltpu.VMEM((2, GATHER_TILE // 2 * SUBLANES, LANES), h2.dtype),
                            pltpu.SemaphoreType.DMA((2,)), pltpu.SemaphoreType.DMA((2, 2))],
        ),
        out_shape=jax.ShapeDtypeStruct(xs_zero.shape, xs_zero.dtype),
        input_output_aliases={2: 0},
        compiler_params=pltpu.CompilerParams(dimension_semantics=("arbitrary",)),
        name="dispatch",
    )(half, h2, xs_zero)


def _expert_kernel(blk_ref, nused_ref, xs_ref, w1_ref, w3_ref, w2_ref, ys_ref,
                   w1b, w3b, w2b, ybuf, cur, sem):
    i = pl.program_id(0)
    n_used = nused_ref[0]
    e = blk_ref[jnp.minimum(i, pl.num_programs(0) - 2)]
    fresh = (i == 0) | (blk_ref[jnp.maximum(i - 1, 0)] != e)

    def fetch(expert, which):
        return (pltpu.make_async_copy(w1_ref.at[expert], w1b.at[which], sem.at[0]),
                pltpu.make_async_copy(w3_ref.at[expert], w3b.at[which], sem.at[1]),
                pltpu.make_async_copy(w2_ref.at[expert], w2b.at[which], sem.at[2]))

    @pl.when(i < n_used)
    def _():
        @pl.when(i == 0)
        def _():
            cur[0] = 0
            for cp in fetch(e, 0):
                cp.start()

        @pl.when(fresh & (i > 0))
        def _():
            cur[0] = 1 - cur[0]

        @pl.when(fresh)
        def _():
            for cp in fetch(e, cur[0]):
                cp.wait()
            nxt = lax.while_loop(lambda j: (j < n_used) & (blk_ref[jnp.minimum(j, n_used - 1)] == e),
                                 lambda j: j + 1, i + 1)

            @pl.when(nxt < n_used)
            def _():
                for cp in fetch(blk_ref[jnp.minimum(nxt, n_used - 1)], 1 - cur[0]):
                    cp.start(priority=1)

    tiles = (xs_ref.shape[0] // SUBLANES, SUBLANES, LANES)

    def compute():
        which = cur[0]
        x = _unpack_rows(xs_ref[...].reshape(tiles).reshape(tiles[0], w1b.shape[1])).astype(BF16)
        a = _nn(x, w1b[which])
        ybuf[...] = _nn((_silu(a) * _nn(x, w3b[which])).astype(BF16), w2b[which])

    def flush():
        ys_ref[...] = _pack_rows(ybuf[...]).reshape(tiles).reshape(ys_ref.shape)

    @pl.when(i == 0)
    def _():
        compute()

    @pl.when((i > 0) & (i < n_used))
    def _():
        flush()
        compute()

    @pl.when((i > 0) & (i == n_used))
    def _():
        flush()


def _experts(blk_e, n_used, xs, w1, w3, w2):
    block_rows = EXPERT_BLOCK // 2 * SUBLANES
    d, de = w1.shape[1], w1.shape[2]
    rows = lambda i, blk, nu: (jnp.minimum(i, nu[0] - 1), 0)
    hbm = pl.BlockSpec(memory_space=pl.ANY)
    return pl.pallas_call(
        _expert_kernel,
        grid_spec=pltpu.PrefetchScalarGridSpec(
            num_scalar_prefetch=2,
            grid=(xs.shape[0] // block_rows + 1,),
            in_specs=[pl.BlockSpec((block_rows, LANES), rows), hbm, hbm, hbm],
            out_specs=pl.BlockSpec((block_rows, LANES),
                                   lambda i, blk, nu: (jnp.minimum(jnp.maximum(i - 1, 0), nu[0] - 1), 0)),
            scratch_shapes=[pltpu.VMEM((2, d, de), BF16), pltpu.VMEM((2, d, de), BF16),
                            pltpu.VMEM((2, de, d), BF16), pltpu.VMEM((EXPERT_BLOCK, d), F32),
                            pltpu.SMEM((1,), I32), pltpu.SemaphoreType.DMA((3,))],
        ),
        out_shape=jax.ShapeDtypeStruct(xs.shape, xs.dtype),
        input_output_aliases={2: 0},
        compiler_params=pltpu.CompilerParams(vmem_limit_bytes=VMEM_LIMIT,
                                             dimension_semantics=("arbitrary",)),
        name="experts",
    )(blk_e, n_used, xs, w1, w3, w2)


def _combine_kernel(half_ref, ys_ref, x1_ref, gt2_ref, wts_ref, g_ref, o_ref, ybuf, sem, *, n_tok, last):
    i = pl.program_id(0)
    last_step = pl.num_programs(0) - 1
    slot = i % 2
    d = x1_ref.shape[1]
    half = SUBLANES // 2
    pairs = GATHER_TILE // 2
    n_groups = pairs // ISSUE_UNROLL

    def gather(step, buf, j0):
        for up in range(2):
            base = step * GATHER_TILE + up * pairs + j0
            for u in range(ISSUE_UNROLL):
                for k in range(2):
                    row = pl.multiple_of(half_ref[k * n_tok + base + u], half)
                    pltpu.make_async_copy(ys_ref.at[pl.ds(row, half)],
                                          ybuf.at[buf, k, pl.ds((j0 + u) * SUBLANES + up * half, half)],
                                          sem.at[buf, k]).start(priority=k)

    def drain(buf):
        for k in range(2):
            pltpu.make_async_copy(ys_ref.at[pl.ds(0, pairs * SUBLANES)], ybuf.at[buf, k], sem.at[buf, k]).wait()

    @pl.when(i == 0)
    def _():
        def first(g, carry):
            gather(0, 0, g * ISSUE_UNROLL)
            return carry

        lax.fori_loop(0, n_groups, first, 0)

    drain(slot)

    nxt = jnp.minimum(i + 1, last_step)
    gt2 = gt2_ref[0]
    gain = g_ref[...]

    def group(g, carry):
        j0 = pl.multiple_of(g * ISSUE_UNROLL, ISSUE_UNROLL)
        lower, upper = pl.ds(j0, ISSUE_UNROLL), pl.ds(pairs + j0, ISSUE_UNROLL)
        tile_rows = pl.ds(pl.multiple_of(j0 * SUBLANES, ISSUE_UNROLL * SUBLANES), ISSUE_UNROLL * SUBLANES)

        def tokens(k):
            tiles = ybuf[slot, k, tile_rows, :].reshape(ISSUE_UNROLL, SUBLANES, LANES)
            return _unpack_rows(tiles.reshape(ISSUE_UNROLL, d))

        w = jnp.concatenate([wts_ref[lower, :], wts_ref[upper, :]], axis=0)
        x = jnp.concatenate([x1_ref[lower, :], x1_ref[upper, :]], axis=0)
        x = x + gt2 * (w[:, 0:1] * tokens(0) + w[:, 1:2] * tokens(1))
        if last:
            x = x * lax.rsqrt(jnp.mean(x * x, axis=-1, keepdims=True) + EPS) * gain
        o_ref[lower, :] = x[:ISSUE_UNROLL]
        o_ref[upper, :] = x[ISSUE_UNROLL:]
        gather(nxt, 1 - slot, j0)
        return carry

    lax.fori_loop(0, n_groups, group, 0, unroll=COMBINE_UNROLL)

    @pl.when(i == last_step)
    def _():
        drain(1 - slot)


def _combine(half, ys, x1, mod, wts, final_g, seq, last=True):
    n_tok, d = x1.shape
    tt = GATHER_TILE
    per_batch = seq // tt
    return pl.pallas_call(
        functools.partial(_combine_kernel, n_tok=n_tok, last=last),
        grid_spec=pltpu.PrefetchScalarGridSpec(
            num_scalar_prefetch=1,
            grid=(n_tok // tt,),
            in_specs=[pl.BlockSpec(memory_space=pl.ANY),
                      pl.BlockSpec((tt, d), lambda i, half: (i, 0)),
                      pl.BlockSpec((1, 1, d), lambda i, half: (i // per_batch, 0, 5)),
                      pl.BlockSpec((tt, 2), lambda i, half: (i, 0)),
                      pl.BlockSpec((1, d), lambda i, half: (0, 0))],
            out_specs=pl.BlockSpec((tt, d), lambda i, half: (i, 0)),
            scratch_shapes=[pltpu.VMEM((2, 2, tt // 2 * SUBLANES, LANES), U32),
                            pltpu.SemaphoreType.DMA((2, 2))],
        ),
        out_shape=jax.ShapeDtypeStruct((n_tok, d), F32),
        compiler_params=pltpu.CompilerParams(dimension_semantics=("arbitrary",)),
        name="combine",
    )(half, ys, x1, mod, wts, final_g)


def kernel(x, c, w_ada, b_ada, norm1_g, w_in, lambda_q1, lambda_k1, lambda_q2, lambda_k2, da_subln_g, hg_lb_logits, hg_norm_g, w_up_da, w_up_hg, w_out, norm2_g, w_rg, b_rg, w_re, b_re, w1, w3, w2, final_g):
    b, s, d = x.shape
    n_tok = b * s
    depth = w_ada.shape[0]
    assert s % ROW_TILE == 0 and s % ATT_BLOCK == 0 and GATHER_TILE == ROW_TILE
    assert (2 * n_tok) % LANES == 0 and d == SUBLANES * LANES

    c_pad = jnp.pad(c, ((0, (-b) % 8), (0, 0)))
    n_slots = 2 * n_tok + N_EXPERTS * EXPERT_BLOCK
    n_blk = n_slots // EXPERT_BLOCK
    n_blk_pad = -(-n_blk // LANES) * LANES

    for l in range(depth):
        lam_init = 0.8 - 0.6 * math.exp(-0.3 * l)
        mod = _adaln(c_pad, w_ada[l], b_ada[l][None, :])[:b].reshape(b, 1, 6 * d)

        n_steps = n_tok // ROW_TILE
        sliced = lambda a: a.reshape((n_steps, a.shape[0] // n_steps) + a.shape[1:])
        (qda, kda, vda, qd, kd, ke, hi, sg, dec, gda, ghg,
         w1_bf, w3_bf, w2_bf, wda_bf, whg_bf, wout_bf) = _inproj(
            x, mod, norm1_g[l][None, :], w_in[l].astype(BF16), hg_lb_logits, l,
            (w1[l], w3[l], w2[l], sliced(w_up_da[l]), sliced(w_up_hg[l]), sliced(w_out[l])))
        yda, xs_zero = _attention(qda, kda, vda, lambda_q1[l][None, :], lambda_k1[l][None, :],
                                  lambda_q2[l][None, :], lambda_k2[l][None, :], da_subln_g[l][None, :],
                                  lam_init, jax.ShapeDtypeStruct((n_slots // 2 * SUBLANES, LANES), U32))
        yhg = _hgrn(qd, kd, ke, hi, sg, dec, hg_norm_g[l][None, :])

        wr = jnp.concatenate([w_re[l].T, w_rg[l].T,
                              jnp.zeros((ROUTER_ROWS - N_EXPERTS - N_GROUPS, d), F32)], axis=0)
        wr_hi = wr.astype(BF16)
        wr_lo = (wr - wr_hi.astype(F32)).astype(BF16)
        br = jnp.concatenate([b_re[l], b_rg[l],
                              jnp.zeros((ROUTER_ROWS - N_EXPERTS - N_GROUPS,), F32)])[:, None]
        x1, h2, eidx, wts = _merge(yda, yhg, gda, ghg, x, mod, norm2_g[l][None, :],
                                   wda_bf.reshape(w_up_da[l].shape), whg_bf.reshape(w_up_hg[l].shape),
                                   wout_bf.reshape(w_out[l].shape), wr_hi, wr_lo, br)

        e_mat = jnp.transpose(eidx, (1, 0, 2)).reshape(2 * n_tok // LANES, LANES)
        half, blk_e, n_used = _positions(e_mat, n_blk_pad)
        half = half.reshape(2 * n_tok)
        xs = _dispatch(half, h2.reshape(b * h2.shape[1], LANES), xs_zero)
        ys = _experts(blk_e[0, :n_blk], n_used[0, :1], xs, w1_bf, w3_bf, w2_bf)
        wts_tok = jnp.transpose(wts, (0, 2, 1)).reshape(n_tok, 2)
        x = _combine(half, ys, x1.reshape(n_tok, d), mod, wts_tok, final_g[None, :], s,
                     last=(l == depth - 1)).reshape(b, s, d)
    return x
```
